```python
import jax, jax.numpy as jnp
from jax import lax
import numpy as np

D_MODEL = 2048
BATCH = 8
SEQ = 4096
DEPTH = 2

N_A_LAYERS = DEPTH // 2
N_B_LAYERS = DEPTH - N_A_LAYERS
CONV_WIDTH = 31
N_HEADS = 16
HEAD_DIM = D_MODEL // N_HEADS
D_FF = -(-(8 * D_MODEL) // (3 * 256)) * 256
BRANCHES = ((128, 1), (512, 4), (2048, 16))
BLOCK = 128
RMS_EPS = 1e-6
LN_EPS = 1e-5

kernel_name = "yoco_conformer_dilated_alibi"


def _rmsnorm(x, g):
    xf = x.astype(jnp.float32)
    y = xf * lax.rsqrt(jnp.mean(xf * xf, axis=-1, keepdims=True) + RMS_EPS)
    return (y * g.astype(jnp.float32)).astype(x.dtype)


def _layernorm(x, g, b):
    xf = x.astype(jnp.float32)
    mu = jnp.mean(xf, axis=-1, keepdims=True)
    var = jnp.mean(jnp.square(xf - mu), axis=-1, keepdims=True)
    y = (xf - mu) * lax.rsqrt(var + LN_EPS) * g.astype(jnp.float32) + b.astype(jnp.float32)
    return y.astype(x.dtype)


def _swiglu(h, w_gate, w_up, w_down):
    return (jax.nn.silu(h @ w_gate) * (h @ w_up)) @ w_down


def _conformer_conv(h, w1, b1, dw, dw_b, ln_g, ln_b, w2, b2):
    u = h @ w1 + b1
    a, gate = jnp.split(u, 2, axis=-1)
    u = a * jax.nn.sigmoid(gate)
    u = lax.conv_general_dilated(
        u, dw[:, None, :].astype(u.dtype), window_strides=(1,),
        padding=[(CONV_WIDTH - 1, 0)],
        dimension_numbers=("NWC", "WIO", "NWC"),
        feature_group_count=u.shape[-1]) + dw_b
    u = jax.nn.silu(_layernorm(u, ln_g, ln_b))
    return u @ w2 + b2


def _alibi_slopes():
    h = jnp.arange(1, N_HEADS + 1, dtype=jnp.float32)
    return jnp.exp2(-8.0 * h / N_HEADS)


def _residue_major(t, d):
    b, s, h, e = t.shape
    L = s // d
    return t.reshape(b, L, d, h, e).transpose(0, 2, 1, 3, 4), L


def _shared_branch_kv(k, v):
    out = []
    for window, d in BRANCHES:
        n_off = window // d
        kr, L = _residue_major(k, d)
        vr, _ = _residue_major(v, d)
        nb = -(-L // BLOCK)
        pad = ((0, 0), (0, 0), (n_off, nb * BLOCK - L), (0, 0), (0, 0))
        idx = jnp.arange(nb)[:, None] * BLOCK + jnp.arange(BLOCK + n_off)[None, :]
        out.append((jnp.pad(kr, pad)[:, :, idx], jnp.pad(vr, pad)[:, :, idx]))
    return out


def _dilated_attention(q, branch_kv):
    b, s, h, e = q.shape
    slopes = _alibi_slopes()
    scale = HEAD_DIM ** -0.5
    outs, lses = [], []
    for (window, d), (k_win, v_win) in zip(BRANCHES, branch_kv):
        n_off = window // d
        qr, L = _residue_major(q, d)
        nb = -(-L // BLOCK)
        qr = jnp.pad(qr, ((0, 0), (0, 0), (0, nb * BLOCK - L), (0, 0), (0, 0)))
        qr = qr.reshape(b, d, nb, BLOCK, h, e)
        scores = jnp.einsum("brnqhe,brnkhe->brnhqk", qr, k_win,
                            preferred_element_type=jnp.float32) * scale
        qi = jnp.arange(BLOCK)[:, None]
        kk = jnp.arange(BLOCK + n_off)[None, :]
        j = qi - kk + n_off
        key_idx = jnp.arange(nb)[:, None, None] * BLOCK + kk[None] - n_off
        valid = (j >= 0) & (j <= n_off) & (key_idx >= 0)
        bias = -slopes[:, None, None] * (d * j).astype(jnp.float32)
        logits = jnp.where(valid[None, None, :, None], scores + bias[None, None, None], -jnp.inf)
        m = jnp.max(logits, axis=-1, keepdims=True)
        p = jnp.exp(logits - m)
        den = jnp.sum(p, axis=-1, keepdims=True)
        o = jnp.einsum("brnhqk,brnkhe->brnqhe", p / den, v_win.astype(jnp.float32))
        lse = (m + jnp.log(den))[..., 0].transpose(0, 1, 2, 4, 3)
        o = o.reshape(b, d, nb * BLOCK, h, e)[:, :, :L].transpose(0, 2, 1, 3, 4).reshape(b, s, h, e)
        lse = lse.reshape(b, d, nb * BLOCK, h)[:, :, :L].transpose(0, 2, 1, 3).reshape(b, s, h)
        outs.append(o)
        lses.append(lse)
    w = jax.nn.softmax(jnp.stack(lses, axis=0), axis=0)
    return jnp.sum(w[..., None] * jnp.stack(outs, axis=0), axis=0)


def _fwd_setup_inputs(seed: int = 0) -> dict:
    key = jax.random.key(seed)
    ks = jax.random.split(key, 24)
    D, F, W = D_MODEL, D_FF, CONV_WIDTH

    def dense(k, shape, fan_in):
        return jax.random.normal(k, shape, jnp.float32) * (fan_in ** -0.5)

    def gain(k, shape):
        return 1.0 + 0.02 * jax.random.normal(k, shape, jnp.float32)

    def bias(k, shape):
        return 0.02 * jax.random.normal(k, shape, jnp.float32)

    return {
        "x": jax.random.normal(ks[0], (BATCH, SEQ, D), jnp.float32),
        "a_norm_g": gain(ks[1], (N_A_LAYERS, D)),
        "conv_w1": dense(ks[2], (N_A_LAYERS, D, 2 * D), D),
        "conv_b1": bias(ks[3], (N_A_LAYERS, 2 * D)),
        "conv_dw": dense(ks[4], (N_A_LAYERS, W, D), W),
        "conv_dw_b": bias(ks[5], (N_A_LAYERS, D)),
        "conv_ln_g": gain(ks[6], (N_A_LAYERS, D)),
        "conv_ln_b": bias(ks[7], (N_A_LAYERS, D)),
        "conv_w2": dense(ks[8], (N_A_LAYERS, D, D), D),
        "conv_b2": bias(ks[9], (N_A_LAYERS, D)),
        "kv_norm_g": gain(ks[10], (D,)),
        "w_k": dense(ks[11], (D, D), D),
        "w_v": dense(ks[12], (D, D), D),
        "b_norm_g": gain(ks[13], (N_B_LAYERS, D)),
        "w_q": dense(ks[14], (N_B_LAYERS, D, D), D),
        "w_o": dense(ks[15], (N_B_LAYERS, D, D), D),
        "ffn_norm_g": gain(ks[16], (DEPTH, D)),
        "ffn_w_gate": dense(ks[17], (DEPTH, D, F), D),
        "ffn_w_up": dense(ks[18], (DEPTH, D, F), D),
        "ffn_w_down": dense(ks[19], (DEPTH, F, D), F),
        "final_norm_g": gain(ks[20], (D,)),
    }


def _fwd_reference(x, a_norm_g, conv_w1, conv_b1, conv_dw, conv_dw_b, conv_ln_g, conv_ln_b,
              conv_w2, conv_b2, kv_norm_g, w_k, w_v, b_norm_g, w_q, w_o,
              ffn_norm_g, ffn_w_gate, ffn_w_up, ffn_w_down, final_norm_g):
    b, s, _ = x.shape
    h = x
    shared_kv = None
    for layer in range(DEPTH):
        if layer < N_A_LAYERS:
            a = layer
            h = h + _conformer_conv(_rmsnorm(h, a_norm_g[a]), conv_w1[a], conv_b1[a],
                                    conv_dw[a], conv_dw_b[a], conv_ln_g[a], conv_ln_b[a],
                                    conv_w2[a], conv_b2[a])
        else:
            if layer == N_A_LAYERS:
                kv_in = _rmsnorm(h, kv_norm_g)
                k = (kv_in @ w_k).reshape(b, s, N_HEADS, HEAD_DIM)
                v = (kv_in @ w_v).reshape(b, s, N_HEADS, HEAD_DIM)
                shared_kv = _shared_branch_kv(k, v)
            i = layer - N_A_LAYERS
            q = (_rmsnorm(h, b_norm_g[i]) @ w_q[i]).reshape(b, s, N_HEADS, HEAD_DIM)
            att = _dilated_attention(q, shared_kv).astype(h.dtype).reshape(b, s, D_MODEL)
            h = h + att @ w_o[i]
        h = h + _swiglu(_rmsnorm(h, ffn_norm_g[layer]), ffn_w_gate[layer],
                        ffn_w_up[layer], ffn_w_down[layer])
    return _rmsnorm(h, final_norm_g)


import jax as _jax
import jax.numpy as _jnp

TWIN_FORMAT = 'train_step'
FWD_PARAMS = ['x', 'a_norm_g', 'conv_w1', 'conv_b1', 'conv_dw', 'conv_dw_b', 'conv_ln_g', 'conv_ln_b', 'conv_w2', 'conv_b2', 'kv_norm_g', 'w_k', 'w_v', 'b_norm_g', 'w_q', 'w_o', 'ffn_norm_g', 'ffn_w_gate', 'ffn_w_up', 'ffn_w_down', 'final_norm_g']
TWIN_WEIGHTS = ['a_norm_g', 'conv_w1', 'conv_b1', 'conv_dw', 'conv_dw_b', 'conv_ln_g', 'conv_ln_b', 'conv_w2', 'conv_b2', 'kv_norm_g', 'w_k', 'w_v', 'b_norm_g', 'w_q', 'w_o', 'ffn_norm_g', 'ffn_w_gate', 'ffn_w_up', 'ffn_w_down', 'final_norm_g']
TWIN_DIFF_INPUT = 'x'
TWIN_INPUTS = ['x', 'a_norm_g', 'conv_w1', 'conv_b1', 'conv_dw', 'conv_dw_b', 'conv_ln_g', 'conv_ln_b', 'conv_w2', 'conv_b2', 'kv_norm_g', 'w_k', 'w_v', 'b_norm_g', 'w_q', 'w_o', 'ffn_norm_g', 'ffn_w_gate', 'ffn_w_up', 'ffn_w_down', 'final_norm_g', 'loss_target', 'm_a_norm_g', 'm_conv_w1', 'm_conv_b1', 'm_conv_dw', 'm_conv_dw_b', 'm_conv_ln_g', 'm_conv_ln_b', 'm_conv_w2', 'm_conv_b2', 'm_kv_norm_g', 'm_w_k', 'm_w_v', 'm_b_norm_g', 'm_w_q', 'm_w_o', 'm_ffn_norm_g', 'm_ffn_w_gate', 'm_ffn_w_up', 'm_ffn_w_down', 'm_final_norm_g', 'v_a_norm_g', 'v_conv_w1', 'v_conv_b1', 'v_conv_dw', 'v_conv_dw_b', 'v_conv_ln_g', 'v_conv_ln_b', 'v_conv_w2', 'v_conv_b2', 'v_kv_norm_g', 'v_w_k', 'v_w_v', 'v_b_norm_g', 'v_w_q', 'v_w_o', 'v_ffn_norm_g', 'v_ffn_w_gate', 'v_ffn_w_up', 'v_ffn_w_down', 'v_final_norm_g']
TWIN_OUTPUTS = ['loss', 'grad_x', 'grad_a_norm_g', 'grad_conv_w1', 'grad_conv_b1', 'grad_conv_dw', 'grad_conv_dw_b', 'grad_conv_ln_g', 'grad_conv_ln_b', 'grad_conv_w2', 'grad_conv_b2', 'grad_kv_norm_g', 'grad_w_k', 'grad_w_v', 'grad_b_norm_g', 'grad_w_q', 'grad_w_o', 'grad_ffn_norm_g', 'grad_ffn_w_gate', 'grad_ffn_w_up', 'grad_ffn_w_down', 'grad_final_norm_g', 'delta_a_norm_g', 'delta_conv_w1', 'delta_conv_b1', 'delta_conv_dw', 'delta_conv_dw_b', 'delta_conv_ln_g', 'delta_conv_ln_b', 'delta_conv_w2', 'delta_conv_b2', 'delta_kv_norm_g', 'delta_w_k', 'delta_w_v', 'delta_b_norm_g', 'delta_w_q', 'delta_w_o', 'delta_ffn_norm_g', 'delta_ffn_w_gate', 'delta_ffn_w_up', 'delta_ffn_w_down', 'delta_final_norm_g', 'new_m_a_norm_g', 'new_m_conv_w1', 'new_m_conv_b1', 'new_m_conv_dw', 'new_m_conv_dw_b', 'new_m_conv_ln_g', 'new_m_conv_ln_b', 'new_m_conv_w2', 'new_m_conv_b2', 'new_m_kv_norm_g', 'new_m_w_k', 'new_m_w_v', 'new_m_b_norm_g', 'new_m_w_q', 'new_m_w_o', 'new_m_ffn_norm_g', 'new_m_ffn_w_gate', 'new_m_ffn_w_up', 'new_m_ffn_w_down', 'new_m_final_norm_g', 'new_v_a_norm_g', 'new_v_conv_w1', 'new_v_conv_b1', 'new_v_conv_dw', 'new_v_conv_dw_b', 'new_v_conv_ln_g', 'new_v_conv_ln_b', 'new_v_conv_w2', 'new_v_conv_b2', 'new_v_kv_norm_g', 'new_v_w_k', 'new_v_w_v', 'new_v_b_norm_g', 'new_v_w_q', 'new_v_w_o', 'new_v_ffn_norm_g', 'new_v_ffn_w_gate', 'new_v_ffn_w_up', 'new_v_ffn_w_down', 'new_v_final_norm_g']
TWIN_LEAF_KINDS = {'loss': 'loss', 'grad_x': 'grad_x', 'grad_a_norm_g': 'grad_w', 'grad_conv_w1': 'grad_w', 'grad_conv_b1': 'grad_w', 'grad_conv_dw': 'grad_w', 'grad_conv_dw_b': 'grad_w', 'grad_conv_ln_g': 'grad_w', 'grad_conv_ln_b': 'grad_w', 'grad_conv_w2': 'grad_w', 'grad_conv_b2': 'grad_w', 'grad_kv_norm_g': 'grad_w', 'grad_w_k': 'grad_w', 'grad_w_v': 'grad_w', 'grad_b_norm_g': 'grad_w', 'grad_w_q': 'grad_w', 'grad_w_o': 'grad_w', 'grad_ffn_norm_g': 'grad_w', 'grad_ffn_w_gate': 'grad_w', 'grad_ffn_w_up': 'grad_w', 'grad_ffn_w_down': 'grad_w', 'grad_final_norm_g': 'grad_w', 'delta_a_norm_g': 'delta_w', 'delta_conv_w1': 'delta_w', 'delta_conv_b1': 'delta_w', 'delta_conv_dw': 'delta_w', 'delta_conv_dw_b': 'delta_w', 'delta_conv_ln_g': 'delta_w', 'delta_conv_ln_b': 'delta_w', 'delta_conv_w2': 'delta_w', 'delta_conv_b2': 'delta_w', 'delta_kv_norm_g': 'delta_w', 'delta_w_k': 'delta_w', 'delta_w_v': 'delta_w', 'delta_b_norm_g': 'delta_w', 'delta_w_q': 'delta_w', 'delta_w_o': 'delta_w', 'delta_ffn_norm_g': 'delta_w', 'delta_ffn_w_gate': 'delta_w', 'delta_ffn_w_up': 'delta_w', 'delta_ffn_w_down': 'delta_w', 'delta_final_norm_g': 'delta_w', 'new_m_a_norm_g': 'new_m', 'new_m_conv_w1': 'new_m', 'new_m_conv_b1': 'new_m', 'new_m_conv_dw': 'new_m', 'new_m_conv_dw_b': 'new_m', 'new_m_conv_ln_g': 'new_m', 'new_m_conv_ln_b': 'new_m', 'new_m_conv_w2': 'new_m', 'new_m_conv_b2': 'new_m', 'new_m_kv_norm_g': 'new_m', 'new_m_w_k': 'new_m', 'new_m_w_v': 'new_m', 'new_m_b_norm_g': 'new_m', 'new_m_w_q': 'new_m', 'new_m_w_o': 'new_m', 'new_m_ffn_norm_g': 'new_m', 'new_m_ffn_w_gate': 'new_m', 'new_m_ffn_w_up': 'new_m', 'new_m_ffn_w_down': 'new_m', 'new_m_final_norm_g': 'new_m', 'new_v_a_norm_g': 'new_v', 'new_v_conv_w1': 'new_v', 'new_v_conv_b1': 'new_v', 'new_v_conv_dw': 'new_v', 'new_v_conv_dw_b': 'new_v', 'new_v_conv_ln_g': 'new_v', 'new_v_conv_ln_b': 'new_v', 'new_v_conv_w2': 'new_v', 'new_v_conv_b2': 'new_v', 'new_v_kv_norm_g': 'new_v', 'new_v_w_k': 'new_v', 'new_v_w_v': 'new_v', 'new_v_b_norm_g': 'new_v', 'new_v_w_q': 'new_v', 'new_v_w_o': 'new_v', 'new_v_ffn_norm_g': 'new_v', 'new_v_ffn_w_gate': 'new_v', 'new_v_ffn_w_up': 'new_v', 'new_v_ffn_w_down': 'new_v', 'new_v_final_norm_g': 'new_v'}


def _forward(args):
    return _fwd_reference(*[args[k] for k in FWD_PARAMS])


def _output_shape():
    def fwd():
        inp = _fwd_setup_inputs(0)
        return _fwd_reference(*[inp[k] for k in FWD_PARAMS])
    out = _jax.eval_shape(fwd)
    return out.shape, out.dtype

N_MICROBATCH = 1
ADAM_LR = 0.001
ADAM_B1 = 0.9
ADAM_B2 = 0.999
ADAM_EPS = 1e-08
ADAM_WD = 0.01
ADAM_STEP = 10
PER_EXAMPLE_BATCH_AXIS = {'x': 0, 'loss_target': 0}
SHARED_INPUTS = []
_WEIGHT_DTYPES = {'a_norm_g': _jnp.float32, 'conv_w1': _jnp.float32, 'conv_b1': _jnp.float32, 'conv_dw': _jnp.float32, 'conv_dw_b': _jnp.float32, 'conv_ln_g': _jnp.float32, 'conv_ln_b': _jnp.float32, 'conv_w2': _jnp.float32, 'conv_b2': _jnp.float32, 'kv_norm_g': _jnp.float32, 'w_k': _jnp.float32, 'w_v': _jnp.float32, 'b_norm_g': _jnp.float32, 'w_q': _jnp.float32, 'w_o': _jnp.float32, 'ffn_norm_g': _jnp.float32, 'ffn_w_gate': _jnp.float32, 'ffn_w_up': _jnp.float32, 'ffn_w_down': _jnp.float32, 'final_norm_g': _jnp.float32}
MOMENT_SCALE = {'a_norm_g': 6.303024e-02, 'conv_w1': 4.401345e-02, 'conv_b1': 6.091561e-02, 'conv_dw': 5.801534e-02, 'conv_dw_b': 1.380744e-01, 'conv_ln_g': 7.766834e-02, 'conv_ln_b': 7.922752e-02, 'conv_w2': 5.922504e-02, 'conv_b2': 1.343476e-01, 'kv_norm_g': 3.479792e-02, 'w_k': 1.767219e-02, 'w_v': 3.023055e-02, 'b_norm_g': 1.785688e-02, 'w_q': 1.767900e-02, 'w_o': 3.017582e-02, 'ffn_norm_g': 5.797369e-02, 'ffn_w_gate': 2.512971e-02, 'ffn_w_up': 2.443916e-02, 'ffn_w_down': 4.056942e-02, 'final_norm_g': 1.602637e+01}


def _to_microbatches(a, axis):
    t = _jnp.moveaxis(a, axis, 0)
    t = t.reshape((N_MICROBATCH, t.shape[0] // N_MICROBATCH) + t.shape[1:])
    return _jnp.moveaxis(t, 1, axis + 1)


def setup_inputs(seed: int = 0) -> dict:
    inp = _fwd_setup_inputs(seed)
    key = _jax.random.fold_in(_jax.random.key(seed), 7919)
    shape, _ = _output_shape()
    out = dict(inp)
    out["loss_target"] = _jax.random.normal(_jax.random.fold_in(key, 0), shape, _jnp.float32)
    for i, name in enumerate(TWIN_WEIGHTS):
        w = inp[name].astype(_jnp.float32)
        if MOMENT_SCALE is None:
            s = _jnp.sqrt(_jnp.mean(_jnp.square(w)) + 1e-30)
        else:
            s = MOMENT_SCALE[name]
        km, kv = _jax.random.split(_jax.random.fold_in(key, i + 1))
        out[name] = w
        out["m_" + name] = s * _jax.random.normal(km, w.shape, _jnp.float32)
        out["v_" + name] = (s * s) * _jax.random.uniform(kv, w.shape, _jnp.float32, 0.5, 1.5)
    if N_MICROBATCH > 1:
        for name, axis in PER_EXAMPLE_BATCH_AXIS.items():
            out[name] = _to_microbatches(out[name], axis)
    return {'x': out['x'], 'a_norm_g': out['a_norm_g'], 'conv_w1': out['conv_w1'], 'conv_b1': out['conv_b1'], 'conv_dw': out['conv_dw'], 'conv_dw_b': out['conv_dw_b'], 'conv_ln_g': out['conv_ln_g'], 'conv_ln_b': out['conv_ln_b'], 'conv_w2': out['conv_w2'], 'conv_b2': out['conv_b2'], 'kv_norm_g': out['kv_norm_g'], 'w_k': out['w_k'], 'w_v': out['w_v'], 'b_norm_g': out['b_norm_g'], 'w_q': out['w_q'], 'w_o': out['w_o'], 'ffn_norm_g': out['ffn_norm_g'], 'ffn_w_gate': out['ffn_w_gate'], 'ffn_w_up': out['ffn_w_up'], 'ffn_w_down': out['ffn_w_down'], 'final_norm_g': out['final_norm_g'], 'loss_target': out['loss_target'], 'm_a_norm_g': out['m_a_norm_g'], 'm_conv_w1': out['m_conv_w1'], 'm_conv_b1': out['m_conv_b1'], 'm_conv_dw': out['m_conv_dw'], 'm_conv_dw_b': out['m_conv_dw_b'], 'm_conv_ln_g': out['m_conv_ln_g'], 'm_conv_ln_b': out['m_conv_ln_b'], 'm_conv_w2': out['m_conv_w2'], 'm_conv_b2': out['m_conv_b2'], 'm_kv_norm_g': out['m_kv_norm_g'], 'm_w_k': out['m_w_k'], 'm_w_v': out['m_w_v'], 'm_b_norm_g': out['m_b_norm_g'], 'm_w_q': out['m_w_q'], 'm_w_o': out['m_w_o'], 'm_ffn_norm_g': out['m_ffn_norm_g'], 'm_ffn_w_gate': out['m_ffn_w_gate'], 'm_ffn_w_up': out['m_ffn_w_up'], 'm_ffn_w_down': out['m_ffn_w_down'], 'm_final_norm_g': out['m_final_norm_g'], 'v_a_norm_g': out['v_a_norm_g'], 'v_conv_w1': out['v_conv_w1'], 'v_conv_b1': out['v_conv_b1'], 'v_conv_dw': out['v_conv_dw'], 'v_conv_dw_b': out['v_conv_dw_b'], 'v_conv_ln_g': out['v_conv_ln_g'], 'v_conv_ln_b': out['v_conv_ln_b'], 'v_conv_w2': out['v_conv_w2'], 'v_conv_b2': out['v_conv_b2'], 'v_kv_norm_g': out['v_kv_norm_g'], 'v_w_k': out['v_w_k'], 'v_w_v': out['v_w_v'], 'v_b_norm_g': out['v_b_norm_g'], 'v_w_q': out['v_w_q'], 'v_w_o': out['v_w_o'], 'v_ffn_norm_g': out['v_ffn_norm_g'], 'v_ffn_w_gate': out['v_ffn_w_gate'], 'v_ffn_w_up': out['v_ffn_w_up'], 'v_ffn_w_down': out['v_ffn_w_down'], 'v_final_norm_g': out['v_final_norm_g']}


def _loss(weights, diff, rest, loss_target):
    with _jax.named_scope("forward"):
        args = {**rest, TWIN_DIFF_INPUT: diff, **{k: w.astype(_WEIGHT_DTYPES[k]) for k, w in weights.items()}}
        y = _forward(args)
    with _jax.named_scope("loss_head"):
        err = _jnp.square(y.astype(_jnp.float32) - loss_target)
        return 0.5 * _jnp.sum(_jnp.mean(err, axis=-1)) if err.ndim else 0.5 * err


def _adamw(w, g, m, v):
    m = ADAM_B1 * m + (1.0 - ADAM_B1) * g
    v = ADAM_B2 * v + (1.0 - ADAM_B2) * _jnp.square(g)
    m_hat = m / (1.0 - ADAM_B1 ** ADAM_STEP)
    v_hat = v / (1.0 - ADAM_B2 ** ADAM_STEP)
    delta = -ADAM_LR * (m_hat / (_jnp.sqrt(v_hat) + ADAM_EPS) + ADAM_WD * w)
    return delta, m, v


def reference(x, a_norm_g, conv_w1, conv_b1, conv_dw, conv_dw_b, conv_ln_g, conv_ln_b, conv_w2, conv_b2, kv_norm_g, w_k, w_v, b_norm_g, w_q, w_o, ffn_norm_g, ffn_w_gate, ffn_w_up, ffn_w_down, final_norm_g, loss_target, m_a_norm_g, m_conv_w1, m_conv_b1, m_conv_dw, m_conv_dw_b, m_conv_ln_g, m_conv_ln_b, m_conv_w2, m_conv_b2, m_kv_norm_g, m_w_k, m_w_v, m_b_norm_g, m_w_q, m_w_o, m_ffn_norm_g, m_ffn_w_gate, m_ffn_w_up, m_ffn_w_down, m_final_norm_g, v_a_norm_g, v_conv_w1, v_conv_b1, v_conv_dw, v_conv_dw_b, v_conv_ln_g, v_conv_ln_b, v_conv_w2, v_conv_b2, v_kv_norm_g, v_w_k, v_w_v, v_b_norm_g, v_w_q, v_w_o, v_ffn_norm_g, v_ffn_w_gate, v_ffn_w_up, v_ffn_w_down, v_final_norm_g):
    given = dict(x=x, a_norm_g=a_norm_g, conv_w1=conv_w1, conv_b1=conv_b1, conv_dw=conv_dw, conv_dw_b=conv_dw_b, conv_ln_g=conv_ln_g, conv_ln_b=conv_ln_b, conv_w2=conv_w2, conv_b2=conv_b2, kv_norm_g=kv_norm_g, w_k=w_k, w_v=w_v, b_norm_g=b_norm_g, w_q=w_q, w_o=w_o, ffn_norm_g=ffn_norm_g, ffn_w_gate=ffn_w_gate, ffn_w_up=ffn_w_up, ffn_w_down=ffn_w_down, final_norm_g=final_norm_g, loss_target=loss_target, m_a_norm_g=m_a_norm_g, m_conv_w1=m_conv_w1, m_conv_b1=m_conv_b1, m_conv_dw=m_conv_dw, m_conv_dw_b=m_conv_dw_b, m_conv_ln_g=m_conv_ln_g, m_conv_ln_b=m_conv_ln_b, m_conv_w2=m_conv_w2, m_conv_b2=m_conv_b2, m_kv_norm_g=m_kv_norm_g, m_w_k=m_w_k, m_w_v=m_w_v, m_b_norm_g=m_b_norm_g, m_w_q=m_w_q, m_w_o=m_w_o, m_ffn_norm_g=m_ffn_norm_g, m_ffn_w_gate=m_ffn_w_gate, m_ffn_w_up=m_ffn_w_up, m_ffn_w_down=m_ffn_w_down, m_final_norm_g=m_final_norm_g, v_a_norm_g=v_a_norm_g, v_conv_w1=v_conv_w1, v_conv_b1=v_conv_b1, v_conv_dw=v_conv_dw, v_conv_dw_b=v_conv_dw_b, v_conv_ln_g=v_conv_ln_g, v_conv_ln_b=v_conv_ln_b, v_conv_w2=v_conv_w2, v_conv_b2=v_conv_b2, v_kv_norm_g=v_kv_norm_g, v_w_k=v_w_k, v_w_v=v_w_v, v_b_norm_g=v_b_norm_g, v_w_q=v_w_q, v_w_o=v_w_o, v_ffn_norm_g=v_ffn_norm_g, v_ffn_w_gate=v_ffn_w_gate, v_ffn_w_up=v_ffn_w_up, v_ffn_w_down=v_ffn_w_down, v_final_norm_g=v_final_norm_g)
    weights = {n: given[n] for n in TWIN_WEIGHTS}
    shared = {n: given[n] for n in SHARED_INPUTS}
    per_example = {n: given[n] for n in ['x']}
    grad_fn = _jax.value_and_grad(_loss, argnums=(0, 1))

    def one_microbatch(ex, loss_target):
        ex = dict(ex)
        diff = ex.pop(TWIN_DIFF_INPUT)
        return grad_fn(weights, diff, {**shared, **ex}, loss_target)

    if N_MICROBATCH == 1:
        loss, (grad_w, grad_x) = one_microbatch(per_example, given["loss_target"])
    else:
        def body(carry, xs):
            loss_sum, grad_sum = carry
            l_k, (gw_k, gx_k) = one_microbatch(xs[0], xs[1])
            with _jax.named_scope("update"):
                return (loss_sum + l_k, _jax.tree.map(_jnp.add, grad_sum, gw_k)), gx_k

        init = (_jnp.zeros((), _jnp.float32), _jax.tree.map(_jnp.zeros_like, weights))
        (loss, grad_w), grad_x = _jax.lax.scan(body, init, (per_example, given["loss_target"]))
    with _jax.named_scope("update"):
        delta_w, new_m, new_v = {}, {}, {}
        for n in TWIN_WEIGHTS:
            delta_w[n], new_m[n], new_v[n] = _adamw(weights[n], grad_w[n], given["m_" + n], given["v_" + n])
    return (loss, grad_x, *[grad_w[n] for n in TWIN_WEIGHTS], *[delta_w[n] for n in TWIN_WEIGHTS],
            *[new_m[n] for n in TWIN_WEIGHTS], *[new_v[n] for n in TWIN_WEIGHTS])
```

```python
import functools

import jax
import jax.numpy as jnp
from jax import lax
from jax.experimental import pallas as pl
from jax.experimental.pallas import tpu as pltpu

ND = 8
HEAD = 128
BLK = 128
BRANCH_DILATIONS = (1, 4, 16)
CONV_W = 31
CONV_PAD = 32
RMS_EPS = 1e-6
LN_EPS = 1e-5
LR, B1, B2, ADAM_EPS, WD, STEP = 0.001, 0.9, 0.999, 1e-08, 0.01, 10
VMEM_LIMIT = 56 * 1024 * 1024

F32, BF16 = jnp.float32, jnp.bfloat16
SDS = jax.ShapeDtypeStruct
MESH = pl.DeviceIdType.MESH
ANY = pl.BlockSpec(memory_space=pl.ANY)

NN = (((1,), (0,)), ((), ()))
NT = (((1,), (1,)), ((), ()))
TN = (((0,), (0,)), ((), ()))


def _dot(a, b, dims):
    return lax.dot_general(a, b, dims, preferred_element_type=F32)


def _cp(*sem):
    return pltpu.CompilerParams(dimension_semantics=sem, vmem_limit_bytes=VMEM_LIMIT)


def _slot(dev):
    return 4 * (dev % 2) + dev // 2


def _sigmoid(v):
    return 1.0 / (1.0 + jnp.exp(-v))


def _all_gather(arrs, name):
    n = len(arrs)

    def body(*refs):
        ins, outs = refs[:n], refs[n:2 * n]
        send_sems, recv_sems, local_sems = refs[2 * n:]
        x, y, c = lax.axis_index("x"), lax.axis_index("y"), lax.axis_index("c")
        me, sib = (x, y, c), (x, y, 1 - c)
        chips = [(1 - x, y), (x, 1 - y), (1 - x, 1 - y)]

        def copy(a, k, block, to, src=None):
            dst = outs[a].at[4 * block[0] + 2 * block[1] + block[2]]
            return pltpu.make_async_remote_copy(
                src_ref=dst if src is None else src, dst_ref=dst,
                send_sem=send_sems.at[7 * a + k], recv_sem=recv_sems.at[7 * a + k],
                device_id=to, device_id_type=MESH)

        mine = [pltpu.make_async_copy(ins[a], outs[a].at[4 * x + 2 * y + c], local_sems.at[a]) for a in range(n)]
        for cp in mine:
            cp.start()
        first = []
        for a in range(n):
            first.append(copy(a, 0, me, sib, src=ins[a]))
            first += [copy(a, 1 + j, me, (*chip, c), src=ins[a]) for j, chip in enumerate(chips)]
        for cp in first:
            cp.start()
        passed = []
        for a in range(n):
            for j, chip in enumerate(chips):
                copy(a, 1 + j, (*chip, c), me).wait_recv()
                fwd = copy(a, 4 + j, (*chip, c), sib)
                fwd.start()
                passed.append(fwd)
        for a in range(n):
            copy(a, 0, sib, me).wait_recv()
            for j, chip in enumerate(chips):
                copy(a, 4 + j, (*chip, 1 - c), me).wait_recv()
        for cp in first + passed:
            cp.wait_send()
        for cp in mine:
            cp.wait()

    return pl.pallas_call(
        body, name=name,
        out_shape=[SDS((ND,) + a.shape, a.dtype) for a in arrs],
        in_specs=[ANY] * n, out_specs=[ANY] * n,
        scratch_shapes=[pltpu.SemaphoreType.DMA((7 * n,)), pltpu.SemaphoreType.DMA((7 * n,)),
                        pltpu.SemaphoreType.DMA((n,))],
    )(*arrs)


def _rs_sibling(arrs, name):
    n = len(arrs)

    def body(*refs):
        ins, outs = refs[:n], refs[n:2 * n]
        send_sems, recv_sems = refs[2 * n:]
        x, y, c = lax.axis_index("x"), lax.axis_index("y"), lax.axis_index("c")
        cps = [pltpu.make_async_remote_copy(
            src_ref=ins[a].at[pl.ds(4 * (1 - c), 4)], dst_ref=outs[a],
            send_sem=send_sems.at[a], recv_sem=recv_sems.at[a],
            device_id=(x, y, 1 - c), device_id_type=MESH) for a in range(n)]
        for cp in cps:
            cp.start()
        for cp in cps:
            cp.wait()

    return pl.pallas_call(
        body, name=name,
        out_shape=[SDS((4,) + a.shape[1:], a.dtype) for a in arrs],
        in_specs=[ANY] * n, out_specs=[ANY] * n,
        scratch_shapes=[pltpu.SemaphoreType.DMA((n,)), pltpu.SemaphoreType.DMA((n,))],
    )(*arrs)


def _rs_cross(arrs, name):
    n = len(arrs)

    def body(*refs):
        ins, outs = refs[:n], refs[n:2 * n]
        send_sems, recv_sems = refs[2 * n:]
        x, y, c = lax.axis_index("x"), lax.axis_index("y"), lax.axis_index("c")
        chips = [(1 - x, y), (x, 1 - y), (1 - x, 1 - y)]
        cps = [pltpu.make_async_remote_copy(
            src_ref=ins[a].at[2 * px + py], dst_ref=outs[a].at[k],
            send_sem=send_sems.at[3 * a + k], recv_sem=recv_sems.at[3 * a + k],
            device_id=(px, py, c), device_id_type=MESH)
            for a in range(n) for k, (px, py) in enumerate(chips)]
        for cp in cps:
            cp.start()
        for cp in cps:
            cp.wait()

    return pl.pallas_call(
        body, name=name,
        out_shape=[SDS((3,) + a.shape[1:], a.dtype) for a in arrs],
        in_specs=[ANY] * n, out_specs=[ANY] * n,
        scratch_shapes=[pltpu.SemaphoreType.DMA((3 * n,)), pltpu.SemaphoreType.DMA((3 * n,))],
    )(*arrs)


def _row_tile(rows):
    return 256 if rows % 256 == 0 else 128


def _rs_add(g, r1, c_idx, name):
    _, rows, cols = g.shape
    tr = _row_tile(rows)

    def body(c_ref, g_ref, r_ref, o_ref):
        o_ref[...] = (g_ref[...].astype(F32) + r_ref[...].astype(F32)).astype(o_ref.dtype)

    return pl.pallas_call(
        body, name=name,
        grid_spec=pltpu.PrefetchScalarGridSpec(
            num_scalar_prefetch=1, grid=(4, rows // tr),
            in_specs=[pl.BlockSpec((1, tr, cols), lambda q, i, c: (4 * c[0] + q, i, 0)),
                      pl.BlockSpec((1, tr, cols), lambda q, i, c: (q, i, 0))],
            out_specs=pl.BlockSpec((1, tr, cols), lambda q, i, c: (q, i, 0))),
        out_shape=SDS((4, rows, cols), g.dtype),
        compiler_params=_cp("parallel", "parallel"),
    )(c_idx, g, r1)


def _adam_math(w, g, m, v):
    m = B1 * m + (1.0 - B1) * g
    v = B2 * v + (1.0 - B2) * (g * g)
    m_hat = m / (1.0 - B1 ** STEP)
    v_hat = v / (1.0 - B2 ** STEP)
    delta = -LR * (m_hat / (jnp.sqrt(v_hat) + ADAM_EPS) + WD * w)
    return delta, m, v


def _adamw_big(w, m, v, t, r2, q_idx, name):
    rows, cols = w.shape
    tr = _row_tile(rows)

    def body(q_ref, w_ref, m_ref, v_ref, t_ref, r_ref, g_out, d_out, m_out, v_out):
        g = t_ref[0].astype(F32)
        for k in range(3):
            g = g + r_ref[k].astype(F32)
        d, mn, vn = _adam_math(w_ref[...], g, m_ref[...], v_ref[...])
        g_out[...], d_out[...], m_out[...], v_out[...] = g, d, mn, vn

    blk = pl.BlockSpec((tr, cols), lambda i, q: (i, 0))
    return pl.pallas_call(
        body, name=name,
        grid_spec=pltpu.PrefetchScalarGridSpec(
            num_scalar_prefetch=1, grid=(rows // tr,),
            in_specs=[blk, blk, blk,
                      pl.BlockSpec((1, tr, cols), lambda i, q: (q[0], i, 0)),
                      pl.BlockSpec((3, tr, cols), lambda i, q: (0, i, 0))],
            out_specs=[blk] * 4),
        out_shape=[SDS((rows, cols), F32)] * 4,
        compiler_params=_cp("parallel"),
    )(q_idx, w, m, v, t, r2)


def _sum_devices(g, name):
    _, rows, cols = g.shape

    def body(g_ref, o_ref):
        acc = g_ref[0]
        for k in range(1, ND):
            acc = acc + g_ref[k]
        o_ref[...] = acc

    return pl.pallas_call(body, name=name, out_shape=SDS((rows, cols), F32))(g)


def _adamw_small(w, g, m, v, name):
    def body(w_ref, g_ref, m_ref, v_ref, d_out, m_out, v_out):
        d, mn, vn = _adam_math(w_ref[...], g_ref[...], m_ref[...], v_ref[...])
        d_out[...], m_out[...], v_out[...] = d, mn, vn

    return pl.pallas_call(body, name=name, out_shape=[SDS(w.shape, F32)] * 3)(w, g, m, v)


ROWS = 256


def _rms_stats(x):
    r = lax.rsqrt(jnp.mean(x * x, axis=-1, keepdims=True) + RMS_EPS)
    return x * r, r


def _rms_fwd(x, gains, name):
    s, d = x.shape
    n = len(gains)

    def body(x_ref, *refs):
        xh, _ = _rms_stats(x_ref[...])
        for g_ref, o_ref in zip(refs[:n], refs[n:]):
            o_ref[...] = (xh * g_ref[...]).astype(BF16)

    row = pl.BlockSpec((ROWS, d), lambda i: (i, 0))
    vec = pl.BlockSpec((1, d), lambda i: (0, 0))
    return pl.pallas_call(
        body, name=name, grid=(s // ROWS,), in_specs=[row] + [vec] * n, out_specs=[row] * n,
        out_shape=[SDS((s, d), BF16)] * n, compiler_params=_cp("parallel"))(x, *gains)


def _rms_bwd_rows(xh, r, gain, dy):
    u = dy * gain
    return r * (u - xh * jnp.mean(u * xh, axis=-1, keepdims=True))


def _rms_bwd(x, pairs, dres, name, colsum=False):
    s, d = x.shape
    n = len(pairs)

    def body(x_ref, dres_ref, *refs):
        g_refs, dy_refs = refs[:n], refs[n:2 * n]
        dx_ref, dxb_ref = refs[2 * n], refs[2 * n + 1]
        dg_refs = refs[2 * n + 2:2 * n + 2 + n]
        cs_ref = refs[-1] if colsum else None
        first = pl.program_id(0) == 0
        xh, r = _rms_stats(x_ref[...])
        dx = dres_ref[...]
        for g_ref, dy_ref, dg_ref in zip(g_refs, dy_refs, dg_refs):
            dy = dy_ref[...]
            dx = dx + _rms_bwd_rows(xh, r, g_ref[...], dy)

            @pl.when(first)
            def _():
                dg_ref[...] = jnp.zeros_like(dg_ref)
            dg_ref[...] += jnp.sum(dy * xh, axis=0, keepdims=True)
        dx_ref[...] = dx
        dxb_ref[...] = dx.astype(BF16)
        if colsum:
            @pl.when(first)
            def _():
                cs_ref[...] = jnp.zeros_like(cs_ref)
            cs_ref[...] += jnp.sum(dx, axis=0, keepdims=True)

    row = pl.BlockSpec((ROWS, d), lambda i: (i, 0))
    vec = pl.BlockSpec((1, d), lambda i: (0, 0))
    nvec = n + (1 if colsum else 0)
    outs = pl.pallas_call(
        body, name=name, grid=(s // ROWS,),
        in_specs=[row, row] + [vec] * n + [row] * n,
        out_specs=[row, row] + [vec] * nvec,
        out_shape=[SDS((s, d), F32), SDS((s, d), BF16)] + [SDS((1, d), F32)] * nvec,
        compiler_params=_cp("arbitrary"),
    )(x, dres, *[p[0] for p in pairs], *[p[1] for p in pairs])
    return outs


def _final_loss(h, target, gain):
    s, d = h.shape

    def body(h_ref, t_ref, g_ref, dh_ref, dhb_ref, dg_ref, loss_ref):
        first = pl.program_id(0) == 0
        xh, r = _rms_stats(h_ref[...])
        gain_v = g_ref[...]
        e = xh * gain_v - t_ref[...]
        dy = e * (1.0 / d)
        dx = _rms_bwd_rows(xh, r, gain_v, dy)
        dh_ref[...] = dx
        dhb_ref[...] = dx.astype(BF16)

        @pl.when(first)
        def _():
            dg_ref[...] = jnp.zeros_like(dg_ref)
            loss_ref[...] = jnp.zeros_like(loss_ref)
        dg_ref[...] += jnp.sum(dy * xh, axis=0, keepdims=True)
        loss_ref[...] += jnp.full((1, 128), 0.5 / d, F32) * jnp.sum(e * e)

    row = pl.BlockSpec((ROWS, d), lambda i: (i, 0))
    vec = pl.BlockSpec((1, d), lambda i: (0, 0))
    return pl.pallas_call(
        body, name="final_loss", grid=(s // ROWS,),
        in_specs=[row, row, vec], out_specs=[row, row, vec, pl.BlockSpec((1, 128), lambda i: (0, 0))],
        out_shape=[SDS((s, d), F32), SDS((s, d), BF16), SDS((1, d), F32), SDS((1, 128), F32)],
        compiler_params=_cp("arbitrary"))(h, target, gain)


CT = 128


def _ln_stats(cv):
    mu = jnp.mean(cv, axis=-1, keepdims=True)
    xc = cv - mu
    rstd = lax.rsqrt(jnp.mean(xc * xc, axis=-1, keepdims=True) + LN_EPS)
    return xc * rstd, rstd


def _conv_fwd(glu, dw, dwb, lng, lnb):
    s, d = glu.shape
    hb = CT // CONV_PAD

    def body(x_ref, halo_ref, dw_ref, dwb_ref, lng_ref, lnb_ref, c_ref, s_ref):
        keep = (pl.program_id(0) > 0).astype(F32)

        def chunk(ci, carry):
            ls = pl.ds(pl.multiple_of(ci * 128, 128), 128)
            xf = jnp.concatenate([halo_ref[:, ls] * keep, x_ref[:, ls]], axis=0)
            acc = jnp.zeros((CT, 128), F32)
            for k in range(CONV_W):
                sh = CONV_W - 1 - k
                xs = pltpu.roll(xf, sh, 0) if sh else xf
                acc = acc + dw_ref[pl.ds(k, 1), ls] * xs[CONV_PAD:]
            c_ref[:, ls] = acc + dwb_ref[:, ls]
            return carry

        lax.fori_loop(0, d // 128, chunk, 0)
        xh, _ = _ln_stats(c_ref[...])
        yv = xh * lng_ref[...] + lnb_ref[...]
        s_ref[...] = (yv * _sigmoid(yv)).astype(BF16)

    row = pl.BlockSpec((CT, d), lambda i: (i, 0))
    halo = pl.BlockSpec((CONV_PAD, d), lambda i: (jnp.maximum(i * hb - 1, 0), 0))
    vec = pl.BlockSpec((1, d), lambda i: (0, 0))
    taps = pl.BlockSpec((CONV_PAD, d), lambda i: (0, 0))
    return pl.pallas_call(
        body, name="conv_fwd", grid=(s // CT,),
        in_specs=[row, halo, taps, vec, vec, vec], out_specs=[row, row],
        out_shape=[SDS((s, d), F32), SDS((s, d), BF16)], compiler_params=_cp("parallel"))(glu, glu, dw, dwb, lng, lnb)


def _ln_bwd(ds, cv, lng, lnb):
    s, d = cv.shape

    def body(ds_ref, c_ref, g_ref, b_ref, dc_ref, dg_ref, db_ref):
        first = pl.program_id(0) == 0
        xh, rstd = _ln_stats(c_ref[...])
        gv = g_ref[...]
        yv = xh * gv + b_ref[...]
        sg = _sigmoid(yv)
        dln = ds_ref[...] * (sg * (1.0 + yv * (1.0 - sg)))
        dxh = dln * gv
        dc_ref[...] = rstd * (dxh - jnp.mean(dxh, axis=-1, keepdims=True)
                              - xh * jnp.mean(dxh * xh, axis=-1, keepdims=True))

        @pl.when(first)
        def _():
            dg_ref[...] = jnp.zeros_like(dg_ref)
            db_ref[...] = jnp.zeros_like(db_ref)
        dg_ref[...] += jnp.sum(dln * xh, axis=0, keepdims=True)
        db_ref[...] += jnp.sum(dln, axis=0, keepdims=True)

    row = pl.BlockSpec((ROWS, d), lambda i: (i, 0))
    vec = pl.BlockSpec((1, d), lambda i: (0, 0))
    return pl.pallas_call(
        body, name="ln_bwd", grid=(s // ROWS,), in_specs=[row, row, vec, vec], out_specs=[row, vec, vec],
        out_shape=[SDS((s, d), F32), SDS((1, d), F32), SDS((1, d), F32)],
        compiler_params=_cp("arbitrary"))(ds, cv, lng, lnb)


def _conv_bwd(dc, glu, ua, ug, dw):
    s, d = dc.shape
    hb = CT // CONV_PAD
    nsteps = s // CT
    full = CT + CONV_PAD

    def body(dc_ref, dcn_ref, x_ref, xp_ref, ua_ref, ug_ref, dw_ref, du_ref, ddw_ref, ddwb_ref, db1_ref):
        i = pl.program_id(0)
        keep_prev = (i > 0).astype(F32)
        keep_next = (i < nsteps - 1).astype(F32)

        @pl.when(i == 0)
        def _():
            ddw_ref[...] = jnp.zeros_like(ddw_ref)
            ddwb_ref[...] = jnp.zeros_like(ddwb_ref)
            db1_ref[...] = jnp.zeros_like(db1_ref)

        def chunk(ci, carry):
            off = pl.multiple_of(ci * 128, 128)
            ls = pl.ds(off, 128)
            ls2 = pl.ds(pl.multiple_of(d + ci * 128, 128), 128)
            dcc = dc_ref[:, ls]
            dcf = jnp.concatenate([dcc, dcn_ref[:, ls] * keep_next], axis=0)
            xf = jnp.concatenate([xp_ref[:, ls] * keep_prev, x_ref[:, ls]], axis=0)
            dglu = jnp.zeros((CT, 128), F32)
            for k in range(CONV_W):
                sh = CONV_W - 1 - k
                dshift = pltpu.roll(dcf, full - sh, 0) if sh else dcf
                dglu = dglu + dw_ref[pl.ds(k, 1), ls] * dshift[:CT]
                xs = pltpu.roll(xf, sh, 0) if sh else xf
                ddw_ref[pl.ds(k, 1), ls] += jnp.sum(dcc * xs[CONV_PAD:], axis=0, keepdims=True)
            ddwb_ref[:, ls] += jnp.sum(dcc, axis=0, keepdims=True)
            av, gv = ua_ref[:, ls], ug_ref[:, ls]
            sg = _sigmoid(gv)
            da = dglu * sg
            dgt = dglu * av * sg * (1.0 - sg)
            du_ref[:, ls] = da.astype(BF16)
            du_ref[:, ls2] = dgt.astype(BF16)
            db1_ref[:, ls] += jnp.sum(da, axis=0, keepdims=True)
            db1_ref[:, ls2] += jnp.sum(dgt, axis=0, keepdims=True)
            return carry

        lax.fori_loop(0, d // 128, chunk, 0)

    row = pl.BlockSpec((CT, d), lambda i: (i, 0))
    prev = pl.BlockSpec((CONV_PAD, d), lambda i: (jnp.maximum(i * hb - 1, 0), 0))
    nxt = pl.BlockSpec((CONV_PAD, d), lambda i: (jnp.minimum((i + 1) * hb, s // CONV_PAD - 1), 0))
    taps = pl.BlockSpec((CONV_PAD, d), lambda i: (0, 0))
    return pl.pallas_call(
        body, name="conv_bwd", grid=(nsteps,),
        in_specs=[row, nxt, row, prev, row, row, taps],
        out_specs=[pl.BlockSpec((CT, 2 * d), lambda i: (i, 0)), taps, pl.BlockSpec((1, d), lambda i: (0, 0)),
                   pl.BlockSpec((1, 2 * d), lambda i: (0, 0))],
        out_shape=[SDS((s, 2 * d), BF16), SDS((CONV_PAD, d), F32), SDS((1, d), F32), SDS((1, 2 * d), F32)],
        compiler_params=_cp("arbitrary"))(dc, dc, glu, glu, ua, ug, dw)


TM = 512
TS = 512


def _glu_mm(n1, w1g, b1):
    s, d = n1.shape
    cw = w1g.shape[2]
    half = ND // 2

    def body(a_ref, wa_ref, wg_ref, ba_ref, bg_ref, ua_ref, ug_ref, glu_ref):
        a = a_ref[...]
        ua = _dot(a, wa_ref[0], NN) + ba_ref[...]
        ug = _dot(a, wg_ref[0], NN) + bg_ref[...]
        ua_ref[...], ug_ref[...] = ua, ug
        glu_ref[...] = ua * _sigmoid(ug)

    out = pl.BlockSpec((TM, cw), lambda m, i: (m, i))
    return pl.pallas_call(
        body, name="glu_mm", grid=(s // TM, half),
        in_specs=[pl.BlockSpec((TM, d), lambda m, i: (m, 0)),
                  pl.BlockSpec((1, d, cw), lambda m, i: (i, 0, 0)),
                  pl.BlockSpec((1, d, cw), lambda m, i: (i + half, 0, 0)),
                  pl.BlockSpec((1, cw), lambda m, i: (0, i)),
                  pl.BlockSpec((1, cw), lambda m, i: (0, i + half))],
        out_specs=[out, out, out], out_shape=[SDS((s, d), F32)] * 3,
        compiler_params=_cp("parallel", "arbitrary"))(n1, w1g, w1g, b1, b1)


def _mm_rows(a, wg, name, res=None, bias=None, out_dtype=F32, tn=512):
    s, kdim = a.shape
    _, kc, n = wg.shape
    assert kc * ND == kdim

    def body(a_ref, w_ref, *refs):
        o_ref = refs[-1]
        acc = _dot(a_ref[...], w_ref[...].reshape(kdim, tn), NN)
        rest = list(refs[:-1])
        if res is not None:
            acc = acc + rest.pop(0)[...]
        if bias is not None:
            acc = acc + rest.pop(0)[...]
        o_ref[...] = acc.astype(out_dtype)

    ins, specs = [a, wg], [pl.BlockSpec((TM, kdim), lambda m, j: (m, 0)), pl.BlockSpec((ND, kc, tn), lambda m, j: (0, 0, j))]
    if res is not None:
        ins.append(res)
        specs.append(pl.BlockSpec((TM, tn), lambda m, j: (m, j)))
    if bias is not None:
        ins.append(bias)
        specs.append(pl.BlockSpec((1, tn), lambda m, j: (0, j)))
    return pl.pallas_call(
        body, name=name, grid=(s // TM, n // tn), in_specs=specs,
        out_specs=pl.BlockSpec((TM, tn), lambda m, j: (m, j)), out_shape=SDS((s, n), out_dtype),
        compiler_params=_cp("parallel", "arbitrary"))(*ins)


def _swiglu_mm(n2, wgg, wug, layer):
    s, d = n2.shape
    fc = wgg.shape[3]

    def body(a_ref, wg_ref, wu_ref, g_ref, u_ref, act_ref):
        a = a_ref[...]
        g = _dot(a, wg_ref[0, 0], NN)
        u = _dot(a, wu_ref[0, 0], NN)
        g_ref[0], u_ref[0] = g, u
        act_ref[0] = (g * _sigmoid(g) * u).astype(BF16)

    wspec = pl.BlockSpec((1, 1, d, fc), lambda m, j: (j, layer, 0, 0))
    out = pl.BlockSpec((1, TM, fc), lambda m, j: (j, m, 0))
    return pl.pallas_call(
        body, name=f"swiglu_mm{layer}", grid=(s // TM, ND),
        in_specs=[pl.BlockSpec((TM, d), lambda m, j: (m, 0)), wspec, wspec],
        out_specs=[out, out, out], out_shape=[SDS((ND, s, fc), F32), SDS((ND, s, fc), F32), SDS((ND, s, fc), BF16)],
        compiler_params=_cp("parallel", "arbitrary"))(n2, wgg, wug)


def _down_mm(act, wdg, res, layer):
    _, s, fc = act.shape
    d = wdg.shape[3]

    def body(a_ref, w_ref, r_ref, o_ref, acc):
        j = pl.program_id(1)

        @pl.when(j == 0)
        def _():
            acc[...] = r_ref[...]
        acc[...] += _dot(a_ref[0], w_ref[0, 0], NN)

        @pl.when(j == ND - 1)
        def _():
            o_ref[...] = acc[...]

    row = pl.BlockSpec((TM, d), lambda m, j: (m, 0))
    return pl.pallas_call(
        body, name=f"down_mm{layer}", grid=(s // TM, ND),
        in_specs=[pl.BlockSpec((1, TM, fc), lambda m, j: (j, m, 0)),
                  pl.BlockSpec((1, 1, fc, d), lambda m, j: (j, layer, 0, 0)), row],
        out_specs=row, out_shape=SDS((s, d), F32), scratch_shapes=[pltpu.VMEM((TM, d), F32)],
        compiler_params=_cp("parallel", "arbitrary"))(act, wdg, res)


def _dact_mm(dh, wdg, gate, up, layer):
    s, d = dh.shape
    fc = wdg.shape[2]

    def body(a_ref, w_ref, g_ref, u_ref, dg_ref, du_ref):
        dact = _dot(a_ref[...], w_ref[0, 0], NT)
        g, u = g_ref[0], u_ref[0]
        sg = _sigmoid(g)
        du_ref[0] = (dact * (g * sg)).astype(BF16)
        dg_ref[0] = (dact * u * (sg * (1.0 + g * (1.0 - sg)))).astype(BF16)

    blk = pl.BlockSpec((1, TM, fc), lambda m, j: (j, m, 0))
    return pl.pallas_call(
        body, name=f"dact_mm{layer}", grid=(s // TM, ND),
        in_specs=[pl.BlockSpec((TM, d), lambda m, j: (m, 0)),
                  pl.BlockSpec((1, 1, fc, d), lambda m, j: (j, layer, 0, 0)), blk, blk],
        out_specs=[blk, blk], out_shape=[SDS((ND, s, fc), BF16)] * 2,
        compiler_params=_cp("parallel", "arbitrary"))(dh, wdg, gate, up)


def _dwd_mm(act, dh, prev, layer):
    _, s, fc = act.shape
    d = dh.shape[1]
    nk = s // TS

    def body(a_ref, b_ref, *refs):
        o_ref, acc = refs[-2], refs[-1]
        k = pl.program_id(1)

        @pl.when(k == 0)
        def _():
            acc[...] = jnp.zeros_like(acc)
        acc[...] += _dot(a_ref[0], b_ref[...], TN)

        @pl.when(k == nk - 1)
        def _():
            o_ref[0, 0] = acc[...].astype(BF16)

    ins = [act, dh]
    specs = [pl.BlockSpec((1, TS, fc), lambda j, k: (j, k, 0)), pl.BlockSpec((TS, d), lambda j, k: (k, 0))]
    alias = {}
    if prev is not None:
        ins.append(prev)
        specs.append(ANY)
        alias = {2: 0}
    return pl.pallas_call(
        body, name=f"dwd_mm{layer}", grid=(ND, nk), in_specs=specs,
        out_specs=pl.BlockSpec((1, 1, fc, d), lambda j, k: (_slot(j), layer, 0, 0)),
        out_shape=SDS((ND, 2, fc, d), BF16), scratch_shapes=[pltpu.VMEM((fc, d), F32)],
        input_output_aliases=alias, compiler_params=_cp("parallel", "arbitrary"))(*ins)


def _dwgu_mm(n2, dgate, dup, prev_g, prev_u, layer):
    s, d = n2.shape
    fc = dgate.shape[2]
    nk = s // TS
    have_prev = prev_g is not None

    def body(a_ref, g_ref, u_ref, *refs):
        og_ref, ou_ref, accg, accu = refs[-4:]
        k = pl.program_id(1)

        @pl.when(k == 0)
        def _():
            accg[...] = jnp.zeros_like(accg)
            accu[...] = jnp.zeros_like(accu)
        a = a_ref[...]
        accg[...] += _dot(a, g_ref[0], TN)
        accu[...] += _dot(a, u_ref[0], TN)

        @pl.when(k == nk - 1)
        def _():
            og_ref[0, 0] = accg[...].astype(BF16)
            ou_ref[0, 0] = accu[...].astype(BF16)

    blk = pl.BlockSpec((1, TS, fc), lambda j, k: (j, k, 0))
    ins, specs, alias = [n2, dgate, dup], [pl.BlockSpec((TS, d), lambda j, k: (k, 0)), blk, blk], {}
    if have_prev:
        ins += [prev_g, prev_u]
        specs += [ANY, ANY]
        alias = {3: 0, 4: 1}
    out = pl.BlockSpec((1, 1, d, fc), lambda j, k: (_slot(j), layer, 0, 0))
    return pl.pallas_call(
        body, name=f"dwgu_mm{layer}", grid=(ND, nk), in_specs=specs, out_specs=[out, out],
        out_shape=[SDS((ND, 2, d, fc), BF16)] * 2,
        scratch_shapes=[pltpu.VMEM((d, fc), F32), pltpu.VMEM((d, fc), F32)],
        input_output_aliases=alias, compiler_params=_cp("parallel", "arbitrary"))(*ins)


def _dn_ffn_mm(dgate, dup, wgg, wug, layer):
    _, s, fc = dgate.shape
    d = wgg.shape[2]

    def body(g_ref, u_ref, wg_ref, wu_ref, o_ref):
        j = pl.program_id(1)

        @pl.when(j == 0)
        def _():
            o_ref[...] = jnp.zeros_like(o_ref)
        o_ref[...] += _dot(g_ref[0], wg_ref[0, 0], NT) + _dot(u_ref[0], wu_ref[0, 0], NT)

    blk = pl.BlockSpec((1, TM, fc), lambda m, j: (j, m, 0))
    wspec = pl.BlockSpec((1, 1, d, fc), lambda m, j: (j, layer, 0, 0))
    return pl.pallas_call(
        body, name=f"dn_ffn_mm{layer}", grid=(s // TM, ND), in_specs=[blk, blk, wspec, wspec],
        out_specs=pl.BlockSpec((TM, d), lambda m, j: (m, 0)), out_shape=SDS((s, d), F32),
        compiler_params=_cp("parallel", "arbitrary"))(dgate, dup, wgg, wug)


def _mm_rows_t(pairs, name, out_dtype):
    s, n = pairs[0][0].shape
    _, kc, _ = pairs[0][1].shape
    np_ = len(pairs)

    def body(*refs):
        o_ref = refs[-1]
        acc = None
        for p in range(np_):
            t = _dot(refs[p][...], refs[np_ + p][0], NT)
            acc = t if acc is None else acc + t
        o_ref[...] = acc.astype(out_dtype)

    return pl.pallas_call(
        body, name=name, grid=(s // TM, ND),
        in_specs=[pl.BlockSpec((TM, n), lambda m, j: (m, 0))] * np_ + [pl.BlockSpec((1, kc, n), lambda m, j: (j, 0, 0))] * np_,
        out_specs=pl.BlockSpec((TM, kc), lambda m, j: (m, j)), out_shape=SDS((s, kc * ND), out_dtype),
        compiler_params=_cp("parallel", "arbitrary"))(*[p[0] for p in pairs], *[p[1] for p in pairs])


def _dw_rows_mm(a, b, name):
    s, kdim = a.shape
    n = b.shape[1]
    kc = kdim // ND
    nk = s // TS

    def body(a_ref, b_ref, o_ref, acc):
        k = pl.program_id(0)

        @pl.when(k == 0)
        def _():
            acc[...] = jnp.zeros_like(acc)
        acc[...] += _dot(a_ref[...], b_ref[...], TN)

        @pl.when(k == nk - 1)
        def _():
            for dev in range(ND):
                o_ref[_slot(dev)] = acc[kc * dev:kc * (dev + 1), :].astype(BF16)

    return pl.pallas_call(
        body, name=name, grid=(nk,),
        in_specs=[pl.BlockSpec((TS, kdim), lambda k: (k, 0)), pl.BlockSpec((TS, n), lambda k: (k, 0))],
        out_specs=pl.BlockSpec((ND, kc, n), lambda k: (0, 0, 0)), out_shape=SDS((ND, kc, n), BF16),
        scratch_shapes=[pltpu.VMEM((kdim, n), F32)], compiler_params=_cp("arbitrary"))(a, b)


def _dw1_mm(n1, du):
    s, d = n1.shape
    cw = du.shape[1] // ND
    nk = s // TS

    def body(a_ref, b_ref, o_ref, acc):
        k = pl.program_id(1)

        @pl.when(k == 0)
        def _():
            acc[...] = jnp.zeros_like(acc)
        acc[...] += _dot(a_ref[...], b_ref[...], TN)

        @pl.when(k == nk - 1)
        def _():
            o_ref[0] = acc[...].astype(BF16)

    return pl.pallas_call(
        body, name="dw1_mm", grid=(ND, nk),
        in_specs=[pl.BlockSpec((TS, d), lambda j, k: (k, 0)), pl.BlockSpec((TS, cw), lambda j, k: (k, j))],
        out_specs=pl.BlockSpec((1, d, cw), lambda j, k: (_slot(j), 0, 0)), out_shape=SDS((ND, d, cw), BF16),
        scratch_shapes=[pltpu.VMEM((d, cw), F32)], compiler_params=_cp("parallel", "arbitrary"))(n1, du)


def _dn1_mm(du, w1g):
    s = du.shape[0]
    _, d, cw = w1g.shape

    def body(a_ref, w_ref, o_ref):
        j = pl.program_id(1)

        @pl.when(j == 0)
        def _():
            o_ref[...] = jnp.zeros_like(o_ref)
        o_ref[...] += _dot(a_ref[...], w_ref[0], NT)

    return pl.pallas_call(
        body, name="dn1_mm", grid=(s // TM, ND),
        in_specs=[pl.BlockSpec((TM, cw), lambda m, j: (m, j)), pl.BlockSpec((1, d, cw), lambda m, j: (j, 0, 0))],
        out_specs=pl.BlockSpec((TM, d), lambda m, j: (m, 0)), out_shape=SDS((s, d), F32),
        compiler_params=_cp("parallel", "arbitrary"))(du, w1g)


NEG = -1e30


def _slopes(heads):
    return [2.0 ** (-8.0 * (h + 1) / heads) for h in range(heads)]


def _band():
    qi = lax.broadcasted_iota(jnp.int32, (BLK, BLK), 0)
    ki = lax.broadcasted_iota(jnp.int32, (BLK, BLK), 1)
    j_cur = qi - ki
    j_prev = qi - ki + BLK
    return j_cur, j_prev


def _to_branch(t, dil):
    s, d = t.shape
    return t[None] if dil == 1 else t.reshape(s // dil, dil, d).transpose(1, 0, 2)


def _from_branch(t, dil):
    _, l, d = t.shape
    return t.reshape(l * dil, d) if dil == 1 else t.transpose(1, 0, 2).reshape(l * dil, d)


def _attn_fwd(q, k, v, dil):
    _, l, d = q.shape
    heads = d // HEAD
    scale = HEAD ** -0.5
    slopes = _slopes(heads)

    def body(q_ref, kc_ref, kp_ref, vc_ref, vp_ref, o_ref, lse_ref):
        has_prev = pl.program_id(1) > 0
        j_cur, j_prev = _band()
        ok_cur = j_cur >= 0
        ok_prev = (j_prev <= BLK) & has_prev
        jf_cur, jf_prev = j_cur.astype(F32), j_prev.astype(F32)
        for h in range(heads):
            sl = slice(HEAD * h, HEAD * (h + 1))
            qh = q_ref[0, :, sl]
            pen = -slopes[h] * dil
            lc = jnp.where(ok_cur, _dot(qh, kc_ref[0, :, sl], NT) * scale + jf_cur * pen, NEG)
            lp = jnp.where(ok_prev, _dot(qh, kp_ref[0, :, sl], NT) * scale + jf_prev * pen, NEG)
            m = jnp.maximum(jnp.max(lc, axis=-1, keepdims=True), jnp.max(lp, axis=-1, keepdims=True))
            pc, pp = jnp.exp(lc - m), jnp.exp(lp - m)
            den = jnp.sum(pc, axis=-1, keepdims=True) + jnp.sum(pp, axis=-1, keepdims=True)
            o = _dot(pc.astype(BF16), vc_ref[0, :, sl], NN) + _dot(pp.astype(BF16), vp_ref[0, :, sl], NN)
            o_ref[0, :, sl] = o / den
            lse_ref[0, :, sl] = jnp.broadcast_to(m + jnp.log(den), (BLK, HEAD))

    cur = pl.BlockSpec((1, BLK, d), lambda r, b: (r, b, 0))
    prev = pl.BlockSpec((1, BLK, d), lambda r, b: (r, jnp.maximum(b - 1, 0), 0))
    return pl.pallas_call(
        body, name=f"attn_fwd_d{dil}", grid=(dil, l // BLK),
        in_specs=[cur, cur, prev, cur, prev], out_specs=[cur, cur],
        out_shape=[SDS((dil, l, d), F32)] * 2, compiler_params=_cp("parallel", "arbitrary"))(q, k, k, v, v)


def _attn_merge(outs, lses):
    s, d = outs[0].shape
    nb = len(outs)

    def body(*refs):
        o_refs, l_refs = refs[:nb], refs[nb:2 * nb]
        att_ref, lse_ref = refs[2 * nb:]
        ls = [r[...] for r in l_refs]
        m = functools.reduce(jnp.maximum, ls)
        ws = [jnp.exp(v - m) for v in ls]
        den = functools.reduce(jnp.add, ws)
        acc = functools.reduce(jnp.add, [w * r[...] for w, r in zip(ws, o_refs)])
        att_ref[...] = (acc / den).astype(BF16)
        lse_ref[...] = m + jnp.log(den)

    row = pl.BlockSpec((ROWS, d), lambda i: (i, 0))
    return pl.pallas_call(
        body, name="attn_merge", grid=(s // ROWS,), in_specs=[row] * (2 * nb), out_specs=[row, row],
        out_shape=[SDS((s, d), BF16), SDS((s, d), F32)], compiler_params=_cp("parallel"))(*outs, *lses)


def _attn_bwd(q, k, v, do, o, lse, dil):
    _, l, d = q.shape
    nb = l // BLK
    heads = d // HEAD
    scale = HEAD ** -0.5
    slopes = _slopes(heads)

    def body(q_ref, kc_ref, kp_ref, vc_ref, vp_ref, do_ref, o_ref, lse_ref, dq_ref, dk_ref, dv_ref, ck, cv):
        b = pl.program_id(1)

        @pl.when(b == 0)
        def _():
            ck[...] = jnp.zeros_like(ck)
            cv[...] = jnp.zeros_like(cv)

        @pl.when(b < nb)
        def _():
            has_prev = b > 0
            j_cur, j_prev = _band()
            ok_cur = j_cur >= 0
            ok_prev = (j_prev <= BLK) & has_prev
            jf_cur, jf_prev = j_cur.astype(F32), j_prev.astype(F32)
            for h in range(heads):
                sl = slice(HEAD * h, HEAD * (h + 1))
                qh, doh = q_ref[0, :, sl], do_ref[0, :, sl]
                kc, kp, vc, vp = kc_ref[0, :, sl], kp_ref[0, :, sl], vc_ref[0, :, sl], vp_ref[0, :, sl]
                lse_h = lse_ref[0, :, sl]
                pen = -slopes[h] * dil
                delta = jnp.sum(doh.astype(F32) * o_ref[0, :, sl].astype(F32), axis=-1, keepdims=True)
                pc = jnp.where(ok_cur, jnp.exp(_dot(qh, kc, NT) * scale + jf_cur * pen - lse_h), 0.0)
                pp = jnp.where(ok_prev, jnp.exp(_dot(qh, kp, NT) * scale + jf_prev * pen - lse_h), 0.0)
                dsc = (pc * (_dot(doh, vc, NT) - delta)).astype(BF16)
                dsp = (pp * (_dot(doh, vp, NT) - delta)).astype(BF16)
                dq_ref[0, :, sl] = (_dot(dsc, kc, NN) + _dot(dsp, kp, NN)) * scale
                dk_ref[0, :, sl] = ck[:, sl] + _dot(dsp, qh, TN) * scale
                dv_ref[0, :, sl] = cv[:, sl] + _dot(pp.astype(BF16), doh, TN)
                ck[:, sl] = _dot(dsc, qh, TN) * scale
                cv[:, sl] = _dot(pc.astype(BF16), doh, TN)

        @pl.when(b == nb)
        def _():
            dk_ref[0] = ck[...]
            dv_ref[0] = cv[...]

    cur = pl.BlockSpec((1, BLK, d), lambda r, b: (r, jnp.minimum(b, nb - 1), 0))
    prev = pl.BlockSpec((1, BLK, d), lambda r, b: (r, jnp.clip(b - 1, 0, nb - 1), 0))
    return pl.pallas_call(
        body, name=f"attn_bwd_d{dil}", grid=(dil, nb + 1),
        in_specs=[cur, cur, prev, cur, prev, cur, cur, cur], out_specs=[cur, prev, prev],
        out_shape=[SDS((dil, l, d), F32)] * 3,
        scratch_shapes=[pltpu.VMEM((BLK, d), F32), pltpu.VMEM((BLK, d), F32)],
        compiler_params=_cp("parallel", "arbitrary"))(q, k, k, v, v, do, o, lse)


def _sum_cast(xs, name):
    s, d = xs[0].shape

    def body(*refs):
        refs[-1][...] = functools.reduce(jnp.add, [r[...] for r in refs[:-1]]).astype(BF16)

    row = pl.BlockSpec((ROWS, d), lambda i: (i, 0))
    return pl.pallas_call(
        body, name=name, grid=(s // ROWS,), in_specs=[row] * len(xs), out_specs=row,
        out_shape=SDS((s, d), BF16), compiler_params=_cp("parallel"))(*xs)


def _pack_rows(vs, width):
    flat = jnp.concatenate([v.reshape(-1) for v in vs])
    spans, at = [], 0
    for v in vs:
        spans.append((at, v.size))
        at += v.size
    rows = -(-at // width)
    rows = -(-rows // 8) * 8
    flat = jnp.pad(flat, (0, rows * width - at))
    return flat.reshape(rows, width), spans


def kernel(x, a_norm_g, conv_w1, conv_b1, conv_dw, conv_dw_b, conv_ln_g, conv_ln_b, conv_w2, conv_b2, kv_norm_g, w_k, w_v, b_norm_g, w_q, w_o, ffn_norm_g, ffn_w_gate, ffn_w_up, ffn_w_down, final_norm_g, loss_target, m_a_norm_g, m_conv_w1, m_conv_b1, m_conv_dw, m_conv_dw_b, m_conv_ln_g, m_conv_ln_b, m_conv_w2, m_conv_b2, m_kv_norm_g, m_w_k, m_w_v, m_b_norm_g, m_w_q, m_w_o, m_ffn_norm_g, m_ffn_w_gate, m_ffn_w_up, m_ffn_w_down, m_final_norm_g, v_a_norm_g, v_conv_w1, v_conv_b1, v_conv_dw, v_conv_dw_b, v_conv_ln_g, v_conv_ln_b, v_conv_w2, v_conv_b2, v_kv_norm_g, v_w_k, v_w_v, v_b_norm_g, v_w_q, v_w_o, v_ffn_norm_g, v_ffn_w_gate, v_ffn_w_up, v_ffn_w_down, v_final_norm_g):
    s, d = x.shape[1], x.shape[2]
    dc = d // ND
    h0 = x[0]
    target = loss_target[0]
    xi, yi, ci = lax.axis_index("x"), lax.axis_index("y"), lax.axis_index("c")
    me = 4 * xi + 2 * yi + ci
    c_idx = jnp.reshape(ci, (1,)).astype(jnp.int32)
    q_idx = jnp.reshape(2 * xi + yi, (1,)).astype(jnp.int32)

    big = [conv_w1[0], conv_w2[0], w_k, w_v, w_q[0], w_o[0], ffn_w_gate, ffn_w_up, ffn_w_down]
    w1g, w2g, wkg, wvg, wqg, wog, wgg, wug, wdg = _all_gather([w.astype(BF16) for w in big], "gather_weights")
    small_shards = [a_norm_g, conv_b1, conv_dw, conv_dw_b, conv_ln_g, conv_ln_b, conv_b2]
    sp, sp_spans = _pack_rows(small_shards, dc)
    (spg,) = _all_gather([sp], "gather_small")
    spg = spg.reshape(ND, -1)

    def small_full(i, rows):
        at, size = sp_spans[i]
        return spg[:, at:at + size].reshape(ND, rows, size // rows).transpose(1, 0, 2).reshape(rows, -1)

    a_g = small_full(0, 1)
    b1 = small_full(1, 1)
    dw = jnp.pad(small_full(2, CONV_W), ((0, CONV_PAD - CONV_W), (0, 0)))
    dwb, lng, lnb, b2 = small_full(3, 1), small_full(4, 1), small_full(5, 1), small_full(6, 1)
    kv_g, q_g, fin_g = kv_norm_g.reshape(1, d), b_norm_g.reshape(1, d), final_norm_g.reshape(1, d)
    f_g = [ffn_norm_g[0:1], ffn_norm_g[1:2]]

    (n1,) = _rms_fwd(h0, [a_g], "rms_a")
    ua, ug, glu = _glu_mm(n1, w1g, b1)
    cv, sw = _conv_fwd(glu, dw, dwb, lng, lnb)
    h1 = _mm_rows(sw, w2g, "w2_mm", res=h0, bias=b2)

    def ffn_fwd(h, layer):
        (n2,) = _rms_fwd(h, [f_g[layer]], f"rms_f{layer}")
        gate, up, act = _swiglu_mm(n2, wgg, wug, layer)
        return _down_mm(act, wdg, h, layer), (n2, gate, up, act)

    h2, ffn0 = ffn_fwd(h1, 0)
    kvn, qn = _rms_fwd(h2, [kv_g, q_g], "rms_kvq")
    kk = _mm_rows(kvn, wkg, "k_mm", out_dtype=BF16)
    vv = _mm_rows(kvn, wvg, "v_mm", out_dtype=BF16)
    qq = _mm_rows(qn, wqg, "q_mm", out_dtype=BF16)
    branch = {dil: tuple(_to_branch(t, dil) for t in (qq, kk, vv)) for dil in BRANCH_DILATIONS}
    outs, lses = [], []
    for dil in BRANCH_DILATIONS:
        o_b, l_b = _attn_fwd(*branch[dil], dil)
        outs.append(_from_branch(o_b, dil))
        lses.append(_from_branch(l_b, dil))
    att, lse = _attn_merge(outs, lses)
    h3 = _mm_rows(att, wog, "wo_mm", res=h2)
    h4, ffn1 = ffn_fwd(h3, 1)

    dh4, dh4b, d_fin, loss_row = _final_loss(h4, target, fin_g)

    def ffn_bwd(h, dh, dhb, saved, layer, prev, name, colsum=False):
        n2, gate, up, act = saved
        dgate, dup = _dact_mm(dhb, wdg, gate, up, layer)
        gwd = _dwd_mm(act, dhb, prev[0], layer)
        gwg, gwu = _dwgu_mm(n2, dgate, dup, prev[1], prev[2], layer)
        dn2 = _dn_ffn_mm(dgate, dup, wgg, wug, layer)
        res = _rms_bwd(h, [(f_g[layer], dn2)], dh, name, colsum=colsum)
        return res, (gwd, gwg, gwu)

    (dh3, dh3b, d_f1), gffn = ffn_bwd(h3, dh4, dh4b, ffn1, 1, (None, None, None), "rms_f1_bwd")
    datt = _mm_rows_t([(dh3b, wog)], "datt_mm", BF16)
    g_wo = _dw_rows_mm(att, dh3b, "dwo_mm")
    dqs, dks, dvs = [], [], []
    for dil in BRANCH_DILATIONS:
        qb, kb, vb = branch[dil]
        dq_b, dk_b, dv_b = _attn_bwd(qb, kb, vb, _to_branch(datt, dil), _to_branch(att, dil), _to_branch(lse, dil), dil)
        dqs.append(_from_branch(dq_b, dil))
        dks.append(_from_branch(dk_b, dil))
        dvs.append(_from_branch(dv_b, dil))
    dq, dk, dv = _sum_cast(dqs, "dq_sum"), _sum_cast(dks, "dk_sum"), _sum_cast(dvs, "dv_sum")
    g_wq = _dw_rows_mm(qn, dq, "dwq_mm")
    g_wk = _dw_rows_mm(kvn, dk, "dwk_mm")
    g_wv = _dw_rows_mm(kvn, dv, "dwv_mm")
    dqn = _mm_rows_t([(dq, wqg)], "dqn_mm", F32)
    dkvn = _mm_rows_t([(dk, wkg), (dv, wvg)], "dkvn_mm", F32)
    dh2, dh2b, d_q, d_kv = _rms_bwd(h2, [(q_g, dqn), (kv_g, dkvn)], dh3, "rms_kvq_bwd")
    (dh1, dh1b, d_f0, d_b2), gffn = ffn_bwd(h1, dh2, dh2b, ffn0, 0, gffn, "rms_f0_bwd", colsum=True)
    g_wd, g_wg, g_wu = gffn
    dsw = _mm_rows_t([(dh1b, w2g)], "dsw_mm", F32)
    g_w2 = _dw_rows_mm(sw, dh1b, "dw2_mm")
    dcv, d_lng, d_lnb = _ln_bwd(dsw, cv, lng, lnb)
    du, d_dw, d_dwb, d_b1 = _conv_bwd(dcv, glu, ua, ug, dw)
    g_w1 = _dw1_mm(n1, du)
    dn1 = _dn1_mm(du, w1g)
    dx, _, d_a = _rms_bwd(h0, [(a_g, dn1)], dh1, "rms_a_bwd")

    gbig = [g_w1, g_w2, g_wk, g_wv, g_wq, g_wo, g_wg, g_wu, g_wd]
    gflat = [g.reshape(ND, -1, g.shape[-1]) for g in gbig]
    r1 = _rs_sibling(gflat, "rs_sibling")
    tsum = [_rs_add(g, r, c_idx, f"rs_add{i}") for i, (g, r) in enumerate(zip(gflat, r1))]
    r2 = _rs_cross(tsum, "rs_cross")
    big_w = [conv_w1, conv_w2, w_k, w_v, w_q, w_o, ffn_w_gate, ffn_w_up, ffn_w_down]
    big_m = [m_conv_w1, m_conv_w2, m_w_k, m_w_v, m_w_q, m_w_o, m_ffn_w_gate, m_ffn_w_up, m_ffn_w_down]
    big_v = [v_conv_w1, v_conv_w2, v_w_k, v_w_v, v_w_q, v_w_o, v_ffn_w_gate, v_ffn_w_up, v_ffn_w_down]
    big_out = []
    for i, (w, m, v) in enumerate(zip(big_w, big_m, big_v)):
        two = lambda t: t.reshape(-1, t.shape[-1])
        res = _adamw_big(two(w), two(m), two(v), tsum[i], r2[i], q_idx, f"adamw{i}")
        big_out.append([t.reshape(w.shape) for t in res])

    small_g = [d_a, d_b1, d_dw[:CONV_W], d_dwb, d_lng, d_lnb, d_b2, d_kv, d_q, d_f0, d_f1, d_fin, loss_row]
    gp, gp_spans = _pack_rows(small_g, d)
    (gpg,) = _all_gather([gp], "gather_small_grads")
    gsum = _sum_devices(gpg, "sum_small_grads").reshape(-1)

    def gfull(i):
        at, size = gp_spans[i]
        return gsum[at:at + size]

    def shard_of(vec, rows):
        return lax.dynamic_slice_in_dim(vec.reshape(rows, -1), me * (vec.size // rows // ND), vec.size // rows // ND, axis=1)

    loss = gfull(12)[0]
    small_grads = [
        shard_of(gfull(0), 1), shard_of(gfull(1), 1), shard_of(gfull(2), CONV_W)[None], shard_of(gfull(3), 1),
        shard_of(gfull(4), 1), shard_of(gfull(5), 1), shard_of(gfull(6), 1),
        gfull(7), gfull(8)[None], jnp.stack([gfull(9), gfull(10)]), gfull(11)]
    small_w = [a_norm_g, conv_b1, conv_dw, conv_dw_b, conv_ln_g, conv_ln_b, conv_b2, kv_norm_g, b_norm_g, ffn_norm_g, final_norm_g]
    small_m = [m_a_norm_g, m_conv_b1, m_conv_dw, m_conv_dw_b, m_conv_ln_g, m_conv_ln_b, m_conv_b2, m_kv_norm_g, m_b_norm_g, m_ffn_norm_g, m_final_norm_g]
    small_v = [v_a_norm_g, v_conv_b1, v_conv_dw, v_conv_dw_b, v_conv_ln_g, v_conv_ln_b, v_conv_b2, v_kv_norm_g, v_b_norm_g, v_ffn_norm_g, v_final_norm_g]
    small_grads = [g.reshape(w.shape) for g, w in zip(small_grads, small_w)]
    wp, spans = _pack_rows(small_w, 128)
    gpk, _ = _pack_rows(small_grads, 128)
    mp, _ = _pack_rows(small_m, 128)
    vp, _ = _pack_rows(small_v, 128)
    dp, mnp, vnp = _adamw_small(wp, gpk, mp, vp, "adamw_small")

    def unpack(packed):
        flat = packed.reshape(-1)
        return [flat[at:at + size].reshape(w.shape) for (at, size), w in zip(spans, small_w)]

    small_out = list(zip(small_grads, unpack(dp), unpack(mnp), unpack(vnp)))

    order = ["a_norm_g", "conv_w1", "conv_b1", "conv_dw", "conv_dw_b", "conv_ln_g", "conv_ln_b", "conv_w2", "conv_b2",
             "kv_norm_g", "w_k", "w_v", "b_norm_g", "w_q", "w_o", "ffn_norm_g", "ffn_w_gate", "ffn_w_up", "ffn_w_down",
             "final_norm_g"]
    big_names = ["conv_w1", "conv_w2", "w_k", "w_v", "w_q", "w_o", "ffn_w_gate", "ffn_w_up", "ffn_w_down"]
    small_names = ["a_norm_g", "conv_b1", "conv_dw", "conv_dw_b", "conv_ln_g", "conv_ln_b", "conv_b2", "kv_norm_g",
                   "b_norm_g", "ffn_norm_g", "final_norm_g"]
    table = {n: big_out[i] for i, n in enumerate(big_names)}
    table.update({n: small_out[i] for i, n in enumerate(small_names)})
    result = [loss, dx[None]]
    for kind in range(4):
        result += [table[n][kind] for n in order]
    return tuple(result)
```

```python
import functools

import jax
import jax.numpy as jnp
from jax import lax
from jax.experimental import pallas as pl
from jax.experimental.pallas import tpu as pltpu

ND = 8
HEAD = 128
BLK = 128
BRANCH_DILATIONS = (1, 4, 16)
CONV_W = 31
CONV_PAD = 32
RMS_EPS = 1e-6
LN_EPS = 1e-5
LR, B1, B2, ADAM_EPS, WD, STEP = 0.001, 0.9, 0.999, 1e-08, 0.01, 10
VMEM_LIMIT = 56 * 1024 * 1024

F32, BF16 = jnp.float32, jnp.bfloat16
SDS = jax.ShapeDtypeStruct
MESH = pl.DeviceIdType.MESH
ANY = pl.BlockSpec(memory_space=pl.ANY)

NN = (((1,), (0,)), ((), ()))
NT = (((1,), (1,)), ((), ()))
TN = (((0,), (0,)), ((), ()))


def _dot(a, b, dims):
    return lax.dot_general(a, b, dims, preferred_element_type=F32)


def _cp(*sem):
    return pltpu.CompilerParams(dimension_semantics=sem, vmem_limit_bytes=VMEM_LIMIT)


def _slot(dev):
    return 4 * (dev % 2) + dev // 2


def _sigmoid(v):
    return 1.0 / (1.0 + jnp.exp(-v))


class _Job:
    def __init__(self, ins, out_shapes, alias, nsem, nlocal, make):
        self.ins, self.out_shapes, self.alias = list(ins), list(out_shapes), dict(alias)
        self.nsem, self.nlocal, self.make = nsem, nlocal, make
        self.result = None


def _coords():
    return lax.axis_index("x"), lax.axis_index("y"), lax.axis_index("c")


def _remote(src, dst, send, recv, k, to):
    return pltpu.make_async_remote_copy(src_ref=src, dst_ref=dst, send_sem=send.at[k], recv_sem=recv.at[k],
                                        device_id=to, device_id_type=MESH)


def _job_gather_send(shards):
    n = len(shards)

    def make(ins, outs, send, recv, local):
        x, y, c = _coords()
        targets = [(x, y, 1 - c), (1 - x, y, c), (x, 1 - y, c), (1 - x, 1 - y, c)]
        cps = []
        for a in range(n):
            dst = outs[a].at[4 * x + 2 * y + c]
            cps.append(pltpu.make_async_copy(ins[a], dst, local.at[a]))
            cps += [_remote(ins[a], dst, send, recv, 4 * a + k, t) for k, t in enumerate(targets)]
        return cps

    return _Job(shards, [SDS((ND,) + s.shape, s.dtype) for s in shards], {}, 4 * n, n, make)


def _job_gather_forward(gathered):
    n = len(gathered)

    def make(ins, outs, send, recv, local):
        x, y, c = _coords()
        cps = []
        for a in range(n):
            for k, (px, py) in enumerate([(1 - x, y), (x, 1 - y), (1 - x, 1 - y)]):
                blk = outs[a].at[4 * px + 2 * py + c]
                cps.append(_remote(blk, blk, send, recv, 3 * a + k, (x, y, 1 - c)))
        return cps

    return _Job(gathered, [SDS(g.shape, g.dtype) for g in gathered], {i: i for i in range(n)}, 3 * n, 0, make)


def _job_scatter_sibling(grads):
    n = len(grads)

    def make(ins, outs, send, recv, local):
        x, y, c = _coords()
        return [_remote(ins[a].at[pl.ds(4 * (1 - c), 4)], outs[a], send, recv, a, (x, y, 1 - c)) for a in range(n)]

    return _Job(grads, [SDS((4,) + g.shape[1:], g.dtype) for g in grads], {}, n, 0, make)


def _job_scatter_cross(sums):
    n = len(sums)

    def make(ins, outs, send, recv, local):
        x, y, c = _coords()
        chips = [(1 - x, y), (x, 1 - y), (1 - x, 1 - y)]
        return [_remote(ins[a].at[2 * px + py], outs[a].at[k], send, recv, 3 * a + k, (px, py, c))
                for a in range(n) for k, (px, py) in enumerate(chips)]

    return _Job(sums, [SDS((3,) + t.shape[1:], t.dtype) for t in sums], {}, 3 * n, 0, make)


def _pc(body, *, name, grid, in_specs, out_specs, out_shape, args, scratch=(), sem=(), alias=None, jobs=()):
    jobs = list(jobs)
    n_in, n_out, n_scr = len(in_specs), len(out_shape), len(scratch)
    aliases = dict(alias or {})
    job_args, job_shapes, job_scratch = [], [], []
    for j in jobs:
        for src, dst in j.alias.items():
            aliases[n_in + len(job_args) + src] = n_out + len(job_shapes) + dst
        job_args += j.ins
        job_shapes += j.out_shapes
        job_scratch += [pltpu.SemaphoreType.DMA((j.nsem,)), pltpu.SemaphoreType.DMA((j.nsem,)),
                        pltpu.SemaphoreType.DMA((max(j.nlocal, 1),))]

    def wrapped(*refs):
        ins = refs[:n_in]
        p = n_in + len(job_args)
        outs = refs[p:p + n_out]
        p += n_out + len(job_shapes)
        scr = refs[p:p + n_scr]
        sems = refs[p + n_scr:]
        copies = []
        pi, po = n_in, n_in + len(job_args) + n_out
        for k, j in enumerate(jobs):
            copies += j.make(refs[pi:pi + len(j.ins)], refs[po:po + len(j.out_shapes)], *sems[3 * k:3 * k + 3])
            pi += len(j.ins)
            po += len(j.out_shapes)
        gridded = bool(copies) and bool(grid)
        if gridded:
            ids = [pl.program_id(i) for i in range(len(grid))]
            first = functools.reduce(jnp.logical_and, [i == 0 for i in ids])
            last = functools.reduce(jnp.logical_and, [i == g - 1 for i, g in zip(ids, grid)])

            @pl.when(first)
            def _():
                for cp in copies:
                    cp.start()
        else:
            for cp in copies:
                cp.start()
        body(*ins, *outs, *scr)
        if gridded:
            @pl.when(last)
            def _():
                for cp in copies:
                    cp.wait()
        else:
            for cp in copies:
                cp.wait()

    kw = dict(grid=grid) if grid else {}
    semantics = ["arbitrary"] * len(grid) if jobs else list(sem)
    res = pl.pallas_call(
        wrapped, name=name, in_specs=list(in_specs) + [ANY] * len(job_args),
        out_specs=list(out_specs) + [ANY] * len(job_shapes), out_shape=list(out_shape) + job_shapes,
        scratch_shapes=list(scratch) + job_scratch, input_output_aliases=aliases,
        compiler_params=_cp(*semantics), **kw)(*args, *job_args)
    p = n_out
    for j in jobs:
        j.result = list(res[p:p + len(j.out_shapes)])
        p += len(j.out_shapes)
    return list(res[:n_out])


def _comm_call(jobs, name):
    _pc(lambda: None, name=name, grid=(), in_specs=[], out_specs=[], out_shape=[], args=[], jobs=jobs)


def _all_gather(arrs, name):
    n = len(arrs)

    def body(*refs):
        ins, outs = refs[:n], refs[n:2 * n]
        send_sems, recv_sems, local_sems = refs[2 * n:]
        x, y, c = lax.axis_index("x"), lax.axis_index("y"), lax.axis_index("c")
        me, sib = (x, y, c), (x, y, 1 - c)
        chips = [(1 - x, y), (x, 1 - y), (1 - x, 1 - y)]

        def copy(a, k, block, to, src=None):
            dst = outs[a].at[4 * block[0] + 2 * block[1] + block[2]]
            return pltpu.make_async_remote_copy(
                src_ref=dst if src is None else src, dst_ref=dst,
                send_sem=send_sems.at[7 * a + k], recv_sem=recv_sems.at[7 * a + k],
                device_id=to, device_id_type=MESH)

        mine = [pltpu.make_async_copy(ins[a], outs[a].at[4 * x + 2 * y + c], local_sems.at[a]) for a in range(n)]
        for cp in mine:
            cp.start()
        first = []
        for a in range(n):
            first.append(copy(a, 0, me, sib, src=ins[a]))
            first += [copy(a, 1 + j, me, (*chip, c), src=ins[a]) for j, chip in enumerate(chips)]
        for cp in first:
            cp.start()
        passed = []
        for a in range(n):
            for j, chip in enumerate(chips):
                copy(a, 1 + j, (*chip, c), me).wait_recv()
                fwd = copy(a, 4 + j, (*chip, c), sib)
                fwd.start()
                passed.append(fwd)
        for a in range(n):
            copy(a, 0, sib, me).wait_recv()
            for j, chip in enumerate(chips):
                copy(a, 4 + j, (*chip, 1 - c), me).wait_recv()
        for cp in first + passed:
            cp.wait_send()
        for cp in mine:
            cp.wait()

    return pl.pallas_call(
        body, name=name,
        out_shape=[SDS((ND,) + a.shape, a.dtype) for a in arrs],
        in_specs=[ANY] * n, out_specs=[ANY] * n,
        scratch_shapes=[pltpu.SemaphoreType.DMA((7 * n,)), pltpu.SemaphoreType.DMA((7 * n,)),
                        pltpu.SemaphoreType.DMA((n,))],
    )(*arrs)


def _row_tile(rows):
    return next(t for t in (256, 128, 64, 32, 16) if rows % t == 0)


def _rs_add(g, r1, c_idx, name):
    _, rows, cols = g.shape
    tr = _row_tile(rows)

    def body(c_ref, g_ref, r_ref, o_ref):
        o_ref[...] = (g_ref[...].astype(F32) + r_ref[...].astype(F32)).astype(o_ref.dtype)

    return pl.pallas_call(
        body, name=name,
        grid_spec=pltpu.PrefetchScalarGridSpec(
            num_scalar_prefetch=1, grid=(4, rows // tr),
            in_specs=[pl.BlockSpec((1, tr, cols), lambda q, i, c: (4 * c[0] + q, i, 0)),
                      pl.BlockSpec((1, tr, cols), lambda q, i, c: (q, i, 0))],
            out_specs=pl.BlockSpec((1, tr, cols), lambda q, i, c: (q, i, 0))),
        out_shape=SDS((4, rows, cols), g.dtype),
        compiler_params=_cp("parallel", "parallel"),
    )(c_idx, g, r1)


def _adam_math(w, g, m, v):
    m = B1 * m + (1.0 - B1) * g
    v = B2 * v + (1.0 - B2) * (g * g)
    m_hat = m / (1.0 - B1 ** STEP)
    v_hat = v / (1.0 - B2 ** STEP)
    delta = -LR * (m_hat / (jnp.sqrt(v_hat) + ADAM_EPS) + WD * w)
    return delta, m, v


def _adamw_big(w, m, v, t, r2, q_idx, name, part=0, prev=None):
    _, rows, cols = t.shape
    tr = _row_tile(rows)
    nblk = rows // tr

    def body(q_ref, w_ref, m_ref, v_ref, t_ref, r_ref, *outs):
        g_out, d_out, m_out, v_out = outs[-4:]
        g = t_ref[0].astype(F32)
        for k in range(3):
            g = g + r_ref[k].astype(F32)
        d, mn, vn = _adam_math(w_ref[...], g, m_ref[...], v_ref[...])
        g_out[...], d_out[...], m_out[...], v_out[...] = g, d, mn, vn

    blk = pl.BlockSpec((tr, cols), lambda i, q: (part * nblk + i, 0))
    specs = [blk, blk, blk, pl.BlockSpec((1, tr, cols), lambda i, q: (q[0], i, 0)),
             pl.BlockSpec((3, tr, cols), lambda i, q: (0, i, 0))]
    ins = [q_idx, w, m, v, t, r2]
    alias = {}
    if prev is not None:
        specs += [ANY] * 4
        alias = {6 + k: k for k in range(4)}
        ins += list(prev)
    return pl.pallas_call(
        body, name=name,
        grid_spec=pltpu.PrefetchScalarGridSpec(num_scalar_prefetch=1, grid=(nblk,), in_specs=specs, out_specs=[blk] * 4),
        out_shape=[SDS(w.shape, F32)] * 4, input_output_aliases=alias,
        compiler_params=_cp("parallel"))(*ins)


def _sum_devices(g, name):
    _, rows, cols = g.shape

    def body(g_ref, o_ref):
        acc = g_ref[0]
        for k in range(1, ND):
            acc = acc + g_ref[k]
        o_ref[...] = acc

    return pl.pallas_call(body, name=name, out_shape=SDS((rows, cols), F32))(g)


def _adamw_small(w, g, m, v, name):
    def body(w_ref, g_ref, m_ref, v_ref, d_out, m_out, v_out):
        d, mn, vn = _adam_math(w_ref[...], g_ref[...], m_ref[...], v_ref[...])
        d_out[...], m_out[...], v_out[...] = d, mn, vn

    return pl.pallas_call(body, name=name, out_shape=[SDS(w.shape, F32)] * 3)(w, g, m, v)


ROWS = 256


def _rms_stats(x):
    r = lax.rsqrt(jnp.mean(x * x, axis=-1, keepdims=True) + RMS_EPS)
    return x * r, r


def _rms_fwd(x, gains, name):
    s, d = x.shape
    n = len(gains)

    def body(x_ref, *refs):
        xh, _ = _rms_stats(x_ref[...])
        for g_ref, o_ref in zip(refs[:n], refs[n:]):
            o_ref[...] = (xh * g_ref[...]).astype(BF16)

    row = pl.BlockSpec((ROWS, d), lambda i: (i, 0))
    vec = pl.BlockSpec((1, d), lambda i: (0, 0))
    return pl.pallas_call(
        body, name=name, grid=(s // ROWS,), in_specs=[row] + [vec] * n, out_specs=[row] * n,
        out_shape=[SDS((s, d), BF16)] * n, compiler_params=_cp("parallel"))(x, *gains)


def _rms_bwd_rows(xh, r, gain, dy):
    u = dy * gain
    return r * (u - xh * jnp.mean(u * xh, axis=-1, keepdims=True))


def _rms_bwd(x, pairs, dres, name, colsum=False):
    s, d = x.shape
    n = len(pairs)

    def body(x_ref, dres_ref, *refs):
        g_refs, dy_refs = refs[:n], refs[n:2 * n]
        dx_ref, dxb_ref = refs[2 * n], refs[2 * n + 1]
        dg_refs = refs[2 * n + 2:2 * n + 2 + n]
        cs_ref = refs[-1] if colsum else None
        first = pl.program_id(0) == 0
        xh, r = _rms_stats(x_ref[...])
        dx = dres_ref[...]
        for g_ref, dy_ref, dg_ref in zip(g_refs, dy_refs, dg_refs):
            dy = dy_ref[...]
            dx = dx + _rms_bwd_rows(xh, r, g_ref[...], dy)

            @pl.when(first)
            def _():
                dg_ref[...] = jnp.zeros_like(dg_ref)
            dg_ref[...] += jnp.sum(dy * xh, axis=0, keepdims=True)
        dx_ref[...] = dx
        dxb_ref[...] = dx.astype(BF16)
        if colsum:
            @pl.when(first)
            def _():
                cs_ref[...] = jnp.zeros_like(cs_ref)
            cs_ref[...] += jnp.sum(dx, axis=0, keepdims=True)

    row = pl.BlockSpec((ROWS, d), lambda i: (i, 0))
    vec = pl.BlockSpec((1, d), lambda i: (0, 0))
    nvec = n + (1 if colsum else 0)
    outs = pl.pallas_call(
        body, name=name, grid=(s // ROWS,),
        in_specs=[row, row] + [vec] * n + [row] * n,
        out_specs=[row, row] + [vec] * nvec,
        out_shape=[SDS((s, d), F32), SDS((s, d), BF16)] + [SDS((1, d), F32)] * nvec,
        compiler_params=_cp("arbitrary"),
    )(x, dres, *[p[0] for p in pairs], *[p[1] for p in pairs])
    return outs


def _final_loss(h, target, gain):
    s, d = h.shape

    def body(h_ref, t_ref, g_ref, dh_ref, dhb_ref, dg_ref, loss_ref):
        first = pl.program_id(0) == 0
        xh, r = _rms_stats(h_ref[...])
        gain_v = g_ref[...]
        e = xh * gain_v - t_ref[...]
        dy = e * (1.0 / d)
        dx = _rms_bwd_rows(xh, r, gain_v, dy)
        dh_ref[...] = dx
        dhb_ref[...] = dx.astype(BF16)

        @pl.when(first)
        def _():
            dg_ref[...] = jnp.zeros_like(dg_ref)
            loss_ref[...] = jnp.zeros_like(loss_ref)
        dg_ref[...] += jnp.sum(dy * xh, axis=0, keepdims=True)
        loss_ref[...] += jnp.full((1, 128), 0.5 / d, F32) * jnp.sum(e * e)

    row = pl.BlockSpec((ROWS, d), lambda i: (i, 0))
    vec = pl.BlockSpec((1, d), lambda i: (0, 0))
    return pl.pallas_call(
        body, name="final_loss", grid=(s // ROWS,),
        in_specs=[row, row, vec], out_specs=[row, row, vec, pl.BlockSpec((1, 128), lambda i: (0, 0))],
        out_shape=[SDS((s, d), F32), SDS((s, d), BF16), SDS((1, d), F32), SDS((1, 128), F32)],
        compiler_params=_cp("arbitrary"))(h, target, gain)


CT = 128


def _ln_stats(cv):
    mu = jnp.mean(cv, axis=-1, keepdims=True)
    xc = cv - mu
    rstd = lax.rsqrt(jnp.mean(xc * xc, axis=-1, keepdims=True) + LN_EPS)
    return xc * rstd, rstd


def _conv_fwd(glu, dw, dwb, lng, lnb, jobs=()):
    s, d = glu.shape
    hb = CT // CONV_PAD

    def body(x_ref, halo_ref, dw_ref, dwb_ref, lng_ref, lnb_ref, c_ref, s_ref):
        keep = (pl.program_id(0) > 0).astype(F32)

        def chunk(ci, carry):
            ls = pl.ds(pl.multiple_of(ci * 128, 128), 128)
            xf = jnp.concatenate([halo_ref[:, ls] * keep, x_ref[:, ls]], axis=0)
            acc = jnp.zeros((CT, 128), F32)
            for k in range(CONV_W):
                sh = CONV_W - 1 - k
                xs = pltpu.roll(xf, sh, 0) if sh else xf
                acc = acc + dw_ref[pl.ds(k, 1), ls] * xs[CONV_PAD:]
            c_ref[:, ls] = acc + dwb_ref[:, ls]
            return carry

        lax.fori_loop(0, d // 128, chunk, 0)
        xh, _ = _ln_stats(c_ref[...])
        yv = xh * lng_ref[...] + lnb_ref[...]
        s_ref[...] = (yv * _sigmoid(yv)).astype(BF16)

    row = pl.BlockSpec((CT, d), lambda i: (i, 0))
    halo = pl.BlockSpec((CONV_PAD, d), lambda i: (jnp.maximum(i * hb - 1, 0), 0))
    vec = pl.BlockSpec((1, d), lambda i: (0, 0))
    taps = pl.BlockSpec((CONV_PAD, d), lambda i: (0, 0))
    return _pc(
        body, name="conv_fwd", grid=(s // CT,),
        in_specs=[row, halo, taps, vec, vec, vec], out_specs=[row, row],
        out_shape=[SDS((s, d), F32), SDS((s, d), BF16)], sem=("parallel",),
        args=(glu, glu, dw, dwb, lng, lnb), jobs=jobs)


def _ln_bwd(ds, cv, lng, lnb):
    s, d = cv.shape

    def body(ds_ref, c_ref, g_ref, b_ref, dc_ref, dg_ref, db_ref):
        first = pl.program_id(0) == 0
        xh, rstd = _ln_stats(c_ref[...])
        gv = g_ref[...]
        yv = xh * gv + b_ref[...]
        sg = _sigmoid(yv)
        dln = ds_ref[...] * (sg * (1.0 + yv * (1.0 - sg)))
        dxh = dln * gv
        dc_ref[...] = rstd * (dxh - jnp.mean(dxh, axis=-1, keepdims=True)
                              - xh * jnp.mean(dxh * xh, axis=-1, keepdims=True))

        @pl.when(first)
        def _():
            dg_ref[...] = jnp.zeros_like(dg_ref)
            db_ref[...] = jnp.zeros_like(db_ref)
        dg_ref[...] += jnp.sum(dln * xh, axis=0, keepdims=True)
        db_ref[...] += jnp.sum(dln, axis=0, keepdims=True)

    row = pl.BlockSpec((ROWS, d), lambda i: (i, 0))
    vec = pl.BlockSpec((1, d), lambda i: (0, 0))
    return pl.pallas_call(
        body, name="ln_bwd", grid=(s // ROWS,), in_specs=[row, row, vec, vec], out_specs=[row, vec, vec],
        out_shape=[SDS((s, d), F32), SDS((1, d), F32), SDS((1, d), F32)],
        compiler_params=_cp("arbitrary"))(ds, cv, lng, lnb)


def _conv_bwd(dc, glu, ua, ug, dw, jobs=()):
    s, d = dc.shape
    hb = CT // CONV_PAD
    nsteps = s // CT
    full = CT + CONV_PAD

    def body(dc_ref, dcn_ref, x_ref, xp_ref, ua_ref, ug_ref, dw_ref, du_ref, ddw_ref, ddwb_ref, db1_ref):
        i = pl.program_id(0)
        keep_prev = (i > 0).astype(F32)
        keep_next = (i < nsteps - 1).astype(F32)

        @pl.when(i == 0)
        def _():
            ddw_ref[...] = jnp.zeros_like(ddw_ref)
            ddwb_ref[...] = jnp.zeros_like(ddwb_ref)
            db1_ref[...] = jnp.zeros_like(db1_ref)

        def chunk(ci, carry):
            off = pl.multiple_of(ci * 128, 128)
            ls = pl.ds(off, 128)
            ls2 = pl.ds(pl.multiple_of(d + ci * 128, 128), 128)
            dcc = dc_ref[:, ls]
            dcf = jnp.concatenate([dcc, dcn_ref[:, ls] * keep_next], axis=0)
            xf = jnp.concatenate([xp_ref[:, ls] * keep_prev, x_ref[:, ls]], axis=0)
            dglu = jnp.zeros((CT, 128), F32)
            for k in range(CONV_W):
                sh = CONV_W - 1 - k
                dshift = pltpu.roll(dcf, full - sh, 0) if sh else dcf
                dglu = dglu + dw_ref[pl.ds(k, 1), ls] * dshift[:CT]
                xs = pltpu.roll(xf, sh, 0) if sh else xf
                ddw_ref[pl.ds(k, 1), ls] += jnp.sum(dcc * xs[CONV_PAD:], axis=0, keepdims=True)
            ddwb_ref[:, ls] += jnp.sum(dcc, axis=0, keepdims=True)
            av, gv = ua_ref[:, ls], ug_ref[:, ls]
            sg = _sigmoid(gv)
            da = dglu * sg
            dgt = dglu * av * sg * (1.0 - sg)
            du_ref[:, ls] = da.astype(BF16)
            du_ref[:, ls2] = dgt.astype(BF16)
            db1_ref[:, ls] += jnp.sum(da, axis=0, keepdims=True)
            db1_ref[:, ls2] += jnp.sum(dgt, axis=0, keepdims=True)
            return carry

        lax.fori_loop(0, d // 128, chunk, 0)

    row = pl.BlockSpec((CT, d), lambda i: (i, 0))
    prev = pl.BlockSpec((CONV_PAD, d), lambda i: (jnp.maximum(i * hb - 1, 0), 0))
    nxt = pl.BlockSpec((CONV_PAD, d), lambda i: (jnp.minimum((i + 1) * hb, s // CONV_PAD - 1), 0))
    taps = pl.BlockSpec((CONV_PAD, d), lambda i: (0, 0))
    return _pc(
        body, name="conv_bwd", grid=(nsteps,),
        in_specs=[row, nxt, row, prev, row, row, taps],
        out_specs=[pl.BlockSpec((CT, 2 * d), lambda i: (i, 0)), taps, pl.BlockSpec((1, d), lambda i: (0, 0)),
                   pl.BlockSpec((1, 2 * d), lambda i: (0, 0))],
        out_shape=[SDS((s, 2 * d), BF16), SDS((CONV_PAD, d), F32), SDS((1, d), F32), SDS((1, 2 * d), F32)],
        sem=("arbitrary",), args=(dc, dc, glu, glu, ua, ug, dw), jobs=jobs)


TM = 512
TS = 512


def _glu_mm(n1, w1g, b1, jobs=()):
    s, d = n1.shape
    cw = w1g.shape[2]
    half = ND // 2

    def body(a_ref, wa_ref, wg_ref, ba_ref, bg_ref, ua_ref, ug_ref, glu_ref):
        a = a_ref[...]
        ua = _dot(a, wa_ref[0], NN) + ba_ref[...]
        ug = _dot(a, wg_ref[0], NN) + bg_ref[...]
        ua_ref[...], ug_ref[...] = ua, ug
        glu_ref[...] = ua * _sigmoid(ug)

    out = pl.BlockSpec((TM, cw), lambda m, i: (m, i))
    return _pc(
        body, name="glu_mm", grid=(s // TM, half),
        in_specs=[pl.BlockSpec((TM, d), lambda m, i: (m, 0)),
                  pl.BlockSpec((1, d, cw), lambda m, i: (i, 0, 0)),
                  pl.BlockSpec((1, d, cw), lambda m, i: (i + half, 0, 0)),
                  pl.BlockSpec((1, cw), lambda m, i: (0, i)),
                  pl.BlockSpec((1, cw), lambda m, i: (0, i + half))],
        out_specs=[out, out, out], out_shape=[SDS((s, d), F32)] * 3,
        sem=("parallel", "arbitrary"), args=(n1, w1g, w1g, b1, b1), jobs=jobs)


def _mm_rows(a, wg, name, res=None, bias=None, out_dtype=F32, tn=512, jobs=()):
    s, kdim = a.shape
    _, kc, n = wg.shape
    assert kc * ND == kdim

    def body(a_ref, w_ref, *refs):
        o_ref = refs[-1]
        acc = _dot(a_ref[...], w_ref[...].reshape(kdim, tn), NN)
        rest = list(refs[:-1])
        if res is not None:
            acc = acc + rest.pop(0)[...]
        if bias is not None:
            acc = acc + rest.pop(0)[...]
        o_ref[...] = acc.astype(out_dtype)

    ins, specs = [a, wg], [pl.BlockSpec((TM, kdim), lambda m, j: (m, 0)), pl.BlockSpec((ND, kc, tn), lambda m, j: (0, 0, j))]
    if res is not None:
        ins.append(res)
        specs.append(pl.BlockSpec((TM, tn), lambda m, j: (m, j)))
    if bias is not None:
        ins.append(bias)
        specs.append(pl.BlockSpec((1, tn), lambda m, j: (0, j)))
    return _pc(
        body, name=name, grid=(s // TM, n // tn), in_specs=specs,
        out_specs=[pl.BlockSpec((TM, tn), lambda m, j: (m, j))], out_shape=[SDS((s, n), out_dtype)],
        sem=("parallel", "arbitrary"), args=ins, jobs=jobs)[0]


def _swiglu_mm(n2, wgg, wug, name, jobs=()):
    s, d = n2.shape
    fc = wgg.shape[2]

    def body(a_ref, wg_ref, wu_ref, g_ref, u_ref, act_ref):
        a = a_ref[...]
        g = _dot(a, wg_ref[0], NN)
        u = _dot(a, wu_ref[0], NN)
        g_ref[0], u_ref[0] = g, u
        act_ref[0] = (g * _sigmoid(g) * u).astype(BF16)

    wspec = pl.BlockSpec((1, d, fc), lambda m, j: (j, 0, 0))
    out = pl.BlockSpec((1, TM, fc), lambda m, j: (j, m, 0))
    return _pc(
        body, name=name, grid=(s // TM, ND),
        in_specs=[pl.BlockSpec((TM, d), lambda m, j: (m, 0)), wspec, wspec],
        out_specs=[out, out, out], out_shape=[SDS((ND, s, fc), F32), SDS((ND, s, fc), F32), SDS((ND, s, fc), BF16)],
        sem=("parallel", "arbitrary"), args=(n2, wgg, wug), jobs=jobs)


def _down_mm(act, wdg, res, name, jobs=()):
    _, s, fc = act.shape
    d = wdg.shape[2]

    def body(a_ref, w_ref, r_ref, o_ref, acc):
        j = pl.program_id(1)

        @pl.when(j == 0)
        def _():
            acc[...] = r_ref[...]
        acc[...] += _dot(a_ref[0], w_ref[0], NN)

        @pl.when(j == ND - 1)
        def _():
            o_ref[...] = acc[...]

    row = pl.BlockSpec((TM, d), lambda m, j: (m, 0))
    return _pc(
        body, name=name, grid=(s // TM, ND),
        in_specs=[pl.BlockSpec((1, TM, fc), lambda m, j: (j, m, 0)),
                  pl.BlockSpec((1, fc, d), lambda m, j: (j, 0, 0)), row],
        out_specs=[row], out_shape=[SDS((s, d), F32)], scratch=[pltpu.VMEM((TM, d), F32)],
        sem=("parallel", "arbitrary"), args=(act, wdg, res), jobs=jobs)[0]


def _dact_mm(dh, wdg, gate, up, name, jobs=()):
    s, d = dh.shape
    fc = wdg.shape[1]

    def body(a_ref, w_ref, g_ref, u_ref, dg_ref, du_ref):
        dact = _dot(a_ref[...], w_ref[0], NT)
        g, u = g_ref[0], u_ref[0]
        sg = _sigmoid(g)
        du_ref[0] = (dact * (g * sg)).astype(BF16)
        dg_ref[0] = (dact * u * (sg * (1.0 + g * (1.0 - sg)))).astype(BF16)

    blk = pl.BlockSpec((1, TM, fc), lambda m, j: (j, m, 0))
    return _pc(
        body, name=name, grid=(s // TM, ND),
        in_specs=[pl.BlockSpec((TM, d), lambda m, j: (m, 0)),
                  pl.BlockSpec((1, fc, d), lambda m, j: (j, 0, 0)), blk, blk],
        out_specs=[blk, blk], out_shape=[SDS((ND, s, fc), BF16)] * 2,
        sem=("parallel", "arbitrary"), args=(dh, wdg, gate, up), jobs=jobs)


def _dwd_mm(act, dh, name, jobs=()):
    _, s, fc = act.shape
    d = dh.shape[1]
    nk = s // TS

    def body(a_ref, b_ref, o_ref, acc):
        k = pl.program_id(1)

        @pl.when(k == 0)
        def _():
            acc[...] = jnp.zeros_like(acc)
        acc[...] += _dot(a_ref[0], b_ref[...], TN)

        @pl.when(k == nk - 1)
        def _():
            o_ref[0] = acc[...].astype(BF16)

    return _pc(
        body, name=name, grid=(ND, nk),
        in_specs=[pl.BlockSpec((1, TS, fc), lambda j, k: (j, k, 0)), pl.BlockSpec((TS, d), lambda j, k: (k, 0))],
        out_specs=[pl.BlockSpec((1, fc, d), lambda j, k: (_slot(j), 0, 0))],
        out_shape=[SDS((ND, fc, d), BF16)], scratch=[pltpu.VMEM((fc, d), F32)],
        sem=("parallel", "arbitrary"), args=(act, dh), jobs=jobs)[0]


def _dwgu_mm(n2, dgate, dup, name, jobs=()):
    s, d = n2.shape
    fc = dgate.shape[2]
    nk = s // TS

    def body(a_ref, g_ref, u_ref, og_ref, ou_ref, accg, accu):
        k = pl.program_id(1)

        @pl.when(k == 0)
        def _():
            accg[...] = jnp.zeros_like(accg)
            accu[...] = jnp.zeros_like(accu)
        a = a_ref[...]
        accg[...] += _dot(a, g_ref[0], TN)
        accu[...] += _dot(a, u_ref[0], TN)

        @pl.when(k == nk - 1)
        def _():
            og_ref[0] = accg[...].astype(BF16)
            ou_ref[0] = accu[...].astype(BF16)

    blk = pl.BlockSpec((1, TS, fc), lambda j, k: (j, k, 0))
    out = pl.BlockSpec((1, d, fc), lambda j, k: (_slot(j), 0, 0))
    return _pc(
        body, name=name, grid=(ND, nk),
        in_specs=[pl.BlockSpec((TS, d), lambda j, k: (k, 0)), blk, blk], out_specs=[out, out],
        out_shape=[SDS((ND, d, fc), BF16)] * 2,
        scratch=[pltpu.VMEM((d, fc), F32), pltpu.VMEM((d, fc), F32)],
        sem=("parallel", "arbitrary"), args=(n2, dgate, dup), jobs=jobs)


def _dn_ffn_mm(dgate, dup, wgg, wug, name, jobs=()):
    _, s, fc = dgate.shape
    d = wgg.shape[1]

    def body(g_ref, u_ref, wg_ref, wu_ref, o_ref):
        j = pl.program_id(1)

        @pl.when(j == 0)
        def _():
            o_ref[...] = jnp.zeros_like(o_ref)
        o_ref[...] += _dot(g_ref[0], wg_ref[0], NT) + _dot(u_ref[0], wu_ref[0], NT)

    blk = pl.BlockSpec((1, TM, fc), lambda m, j: (j, m, 0))
    wspec = pl.BlockSpec((1, d, fc), lambda m, j: (j, 0, 0))
    return _pc(
        body, name=name, grid=(s // TM, ND), in_specs=[blk, blk, wspec, wspec],
        out_specs=[pl.BlockSpec((TM, d), lambda m, j: (m, 0))], out_shape=[SDS((s, d), F32)],
        sem=("parallel", "arbitrary"), args=(dgate, dup, wgg, wug), jobs=jobs)[0]


def _mm_rows_t(pairs, name, out_dtype, jobs=()):
    s, n = pairs[0][0].shape
    _, kc, _ = pairs[0][1].shape
    np_ = len(pairs)

    def body(*refs):
        o_ref = refs[-1]
        acc = None
        for p in range(np_):
            t = _dot(refs[p][...], refs[np_ + p][0], NT)
            acc = t if acc is None else acc + t
        o_ref[...] = acc.astype(out_dtype)

    return _pc(
        body, name=name, grid=(s // TM, ND),
        in_specs=[pl.BlockSpec((TM, n), lambda m, j: (m, 0))] * np_ + [pl.BlockSpec((1, kc, n), lambda m, j: (j, 0, 0))] * np_,
        out_specs=[pl.BlockSpec((TM, kc), lambda m, j: (m, j))], out_shape=[SDS((s, kc * ND), out_dtype)],
        sem=("parallel", "arbitrary"), args=[p[0] for p in pairs] + [p[1] for p in pairs], jobs=jobs)[0]


def _dw_rows_mm(a, b, name):
    s, kdim = a.shape
    n = b.shape[1]
    kc = kdim // ND
    nk = s // TS

    def body(a_ref, b_ref, o_ref, acc):
        k = pl.program_id(0)

        @pl.when(k == 0)
        def _():
            acc[...] = jnp.zeros_like(acc)
        acc[...] += _dot(a_ref[...], b_ref[...], TN)

        @pl.when(k == nk - 1)
        def _():
            for dev in range(ND):
                o_ref[_slot(dev)] = acc[kc * dev:kc * (dev + 1), :].astype(BF16)

    return pl.pallas_call(
        body, name=name, grid=(nk,),
        in_specs=[pl.BlockSpec((TS, kdim), lambda k: (k, 0)), pl.BlockSpec((TS, n), lambda k: (k, 0))],
        out_specs=pl.BlockSpec((ND, kc, n), lambda k: (0, 0, 0)), out_shape=SDS((ND, kc, n), BF16),
        scratch_shapes=[pltpu.VMEM((kdim, n), F32)], compiler_params=_cp("arbitrary"))(a, b)


def _dw1_mm(n1, du):
    s, d = n1.shape
    cw = du.shape[1] // ND
    nk = s // TS

    def body(a_ref, b_ref, o_ref, acc):
        k = pl.program_id(1)

        @pl.when(k == 0)
        def _():
            acc[...] = jnp.zeros_like(acc)
        acc[...] += _dot(a_ref[...], b_ref[...], TN)

        @pl.when(k == nk - 1)
        def _():
            o_ref[0] = acc[...].astype(BF16)

    return pl.pallas_call(
        body, name="dw1_mm", grid=(ND, nk),
        in_specs=[pl.BlockSpec((TS, d), lambda j, k: (k, 0)), pl.BlockSpec((TS, cw), lambda j, k: (k, j))],
        out_specs=pl.BlockSpec((1, d, cw), lambda j, k: (_slot(j), 0, 0)), out_shape=SDS((ND, d, cw), BF16),
        scratch_shapes=[pltpu.VMEM((d, cw), F32)], compiler_params=_cp("parallel", "arbitrary"))(n1, du)


def _dn1_mm(du, w1g, jobs=()):
    s = du.shape[0]
    _, d, cw = w1g.shape

    def body(a_ref, w_ref, o_ref):
        j = pl.program_id(1)

        @pl.when(j == 0)
        def _():
            o_ref[...] = jnp.zeros_like(o_ref)
        o_ref[...] += _dot(a_ref[...], w_ref[0], NT)

    return _pc(
        body, name="dn1_mm", grid=(s // TM, ND),
        in_specs=[pl.BlockSpec((TM, cw), lambda m, j: (m, j)), pl.BlockSpec((1, d, cw), lambda m, j: (j, 0, 0))],
        out_specs=[pl.BlockSpec((TM, d), lambda m, j: (m, 0))], out_shape=[SDS((s, d), F32)],
        sem=("parallel", "arbitrary"), args=(du, w1g), jobs=jobs)[0]


NEG = -1e30


def _slopes(heads):
    return [2.0 ** (-8.0 * (h + 1) / heads) for h in range(heads)]


def _band():
    qi = lax.broadcasted_iota(jnp.int32, (BLK, BLK), 0)
    ki = lax.broadcasted_iota(jnp.int32, (BLK, BLK), 1)
    j_cur = qi - ki
    j_prev = qi - ki + BLK
    return j_cur, j_prev


def _to_branch(t, dil):
    s, d = t.shape
    return t[None] if dil == 1 else t.reshape(s // dil, dil, d).transpose(1, 0, 2)


def _from_branch(t, dil):
    _, l, d = t.shape
    return t.reshape(l * dil, d) if dil == 1 else t.transpose(1, 0, 2).reshape(l * dil, d)


def _attn_fwd(q, k, v, dil, jobs=()):
    _, l, d = q.shape
    heads = d // HEAD
    scale = HEAD ** -0.5
    slopes = _slopes(heads)

    def body(q_ref, kc_ref, kp_ref, vc_ref, vp_ref, o_ref, lse_ref):
        has_prev = pl.program_id(1) > 0
        j_cur, j_prev = _band()
        ok_cur = j_cur >= 0
        ok_prev = (j_prev <= BLK) & has_prev
        jf_cur, jf_prev = j_cur.astype(F32), j_prev.astype(F32)
        for h in range(heads):
            sl = slice(HEAD * h, HEAD * (h + 1))
            qh = q_ref[0, :, sl]
            pen = -slopes[h] * dil
            lc = jnp.where(ok_cur, _dot(qh, kc_ref[0, :, sl], NT) * scale + jf_cur * pen, NEG)
            lp = jnp.where(ok_prev, _dot(qh, kp_ref[0, :, sl], NT) * scale + jf_prev * pen, NEG)
            m = jnp.maximum(jnp.max(lc, axis=-1, keepdims=True), jnp.max(lp, axis=-1, keepdims=True))
            pc, pp = jnp.exp(lc - m), jnp.exp(lp - m)
            den = jnp.sum(pc, axis=-1, keepdims=True) + jnp.sum(pp, axis=-1, keepdims=True)
            o = _dot(pc.astype(BF16), vc_ref[0, :, sl], NN) + _dot(pp.astype(BF16), vp_ref[0, :, sl], NN)
            o_ref[0, :, sl] = o / den
            lse_ref[0, :, sl] = jnp.broadcast_to(m + jnp.log(den), (BLK, HEAD))

    cur = pl.BlockSpec((1, BLK, d), lambda r, b: (r, b, 0))
    prev = pl.BlockSpec((1, BLK, d), lambda r, b: (r, jnp.maximum(b - 1, 0), 0))
    return _pc(
        body, name=f"attn_fwd_d{dil}", grid=(dil, l // BLK),
        in_specs=[cur, cur, prev, cur, prev], out_specs=[cur, cur],
        out_shape=[SDS((dil, l, d), F32)] * 2, sem=("parallel", "arbitrary"), args=(q, k, k, v, v), jobs=jobs)


def _attn_merge(outs, lses):
    s, d = outs[0].shape
    nb = len(outs)

    def body(*refs):
        o_refs, l_refs = refs[:nb], refs[nb:2 * nb]
        att_ref, lse_ref = refs[2 * nb:]
        ls = [r[...] for r in l_refs]
        m = functools.reduce(jnp.maximum, ls)
        ws = [jnp.exp(v - m) for v in ls]
        den = functools.reduce(jnp.add, ws)
        acc = functools.reduce(jnp.add, [w * r[...] for w, r in zip(ws, o_refs)])
        att_ref[...] = (acc / den).astype(BF16)
        lse_ref[...] = m + jnp.log(den)

    row = pl.BlockSpec((ROWS, d), lambda i: (i, 0))
    return pl.pallas_call(
        body, name="attn_merge", grid=(s // ROWS,), in_specs=[row] * (2 * nb), out_specs=[row, row],
        out_shape=[SDS((s, d), BF16), SDS((s, d), F32)], compiler_params=_cp("parallel"))(*outs, *lses)


def _attn_bwd(q, k, v, do, o, lse, dil, jobs=()):
    _, l, d = q.shape
    nb = l // BLK
    heads = d // HEAD
    scale = HEAD ** -0.5
    slopes = _slopes(heads)

    def body(q_ref, kc_ref, kp_ref, vc_ref, vp_ref, do_ref, o_ref, lse_ref, dq_ref, dk_ref, dv_ref, ck, cv):
        b = pl.program_id(1)

        @pl.when(b == 0)
        def _():
            ck[...] = jnp.zeros_like(ck)
            cv[...] = jnp.zeros_like(cv)

        @pl.when(b < nb)
        def _():
            has_prev = b > 0
            j_cur, j_prev = _band()
            ok_cur = j_cur >= 0
            ok_prev = (j_prev <= BLK) & has_prev
            jf_cur, jf_prev = j_cur.astype(F32), j_prev.astype(F32)
            for h in range(heads):
                sl = slice(HEAD * h, HEAD * (h + 1))
                qh, doh = q_ref[0, :, sl], do_ref[0, :, sl]
                kc, kp, vc, vp = kc_ref[0, :, sl], kp_ref[0, :, sl], vc_ref[0, :, sl], vp_ref[0, :, sl]
                lse_h = lse_ref[0, :, sl]
                pen = -slopes[h] * dil
                delta = jnp.sum(doh.astype(F32) * o_ref[0, :, sl].astype(F32), axis=-1, keepdims=True)
                pc = jnp.where(ok_cur, jnp.exp(_dot(qh, kc, NT) * scale + jf_cur * pen - lse_h), 0.0)
                pp = jnp.where(ok_prev, jnp.exp(_dot(qh, kp, NT) * scale + jf_prev * pen - lse_h), 0.0)
                dsc = (pc * (_dot(doh, vc, NT) - delta)).astype(BF16)
                dsp = (pp * (_dot(doh, vp, NT) - delta)).astype(BF16)
                dq_ref[0, :, sl] = (_dot(dsc, kc, NN) + _dot(dsp, kp, NN)) * scale
                dk_ref[0, :, sl] = ck[:, sl] + _dot(dsp, qh, TN) * scale
                dv_ref[0, :, sl] = cv[:, sl] + _dot(pp.astype(BF16), doh, TN)
                ck[:, sl] = _dot(dsc, qh, TN) * scale
                cv[:, sl] = _dot(pc.astype(BF16), doh, TN)

        @pl.when(b == nb)
        def _():
            dk_ref[0] = ck[...]
            dv_ref[0] = cv[...]

    cur = pl.BlockSpec((1, BLK, d), lambda r, b: (r, jnp.minimum(b, nb - 1), 0))
    prev = pl.BlockSpec((1, BLK, d), lambda r, b: (r, jnp.clip(b - 1, 0, nb - 1), 0))
    return _pc(
        body, name=f"attn_bwd_d{dil}", grid=(dil, nb + 1),
        in_specs=[cur, cur, prev, cur, prev, cur, cur, cur], out_specs=[cur, prev, prev],
        out_shape=[SDS((dil, l, d), F32)] * 3,
        scratch=[pltpu.VMEM((BLK, d), F32), pltpu.VMEM((BLK, d), F32)],
        sem=("parallel", "arbitrary"), args=(q, k, k, v, v, do, o, lse), jobs=jobs)


def _sum_cast(xs, name):
    s, d = xs[0].shape

    def body(*refs):
        refs[-1][...] = functools.reduce(jnp.add, [r[...] for r in refs[:-1]]).astype(BF16)

    row = pl.BlockSpec((ROWS, d), lambda i: (i, 0))
    return pl.pallas_call(
        body, name=name, grid=(s // ROWS,), in_specs=[row] * len(xs), out_specs=row,
        out_shape=SDS((s, d), BF16), compiler_params=_cp("parallel"))(*xs)


def _pack_rows(vs, width):
    flat = jnp.concatenate([v.reshape(-1) for v in vs])
    spans, at = [], 0
    for v in vs:
        spans.append((at, v.size))
        at += v.size
    rows = -(-at // width)
    rows = -(-rows // 8) * 8
    flat = jnp.pad(flat, (0, rows * width - at))
    return flat.reshape(rows, width), spans


def kernel(x, a_norm_g, conv_w1, conv_b1, conv_dw, conv_dw_b, conv_ln_g, conv_ln_b, conv_w2, conv_b2, kv_norm_g, w_k, w_v, b_norm_g, w_q, w_o, ffn_norm_g, ffn_w_gate, ffn_w_up, ffn_w_down, final_norm_g, loss_target, m_a_norm_g, m_conv_w1, m_conv_b1, m_conv_dw, m_conv_dw_b, m_conv_ln_g, m_conv_ln_b, m_conv_w2, m_conv_b2, m_kv_norm_g, m_w_k, m_w_v, m_b_norm_g, m_w_q, m_w_o, m_ffn_norm_g, m_ffn_w_gate, m_ffn_w_up, m_ffn_w_down, m_final_norm_g, v_a_norm_g, v_conv_w1, v_conv_b1, v_conv_dw, v_conv_dw_b, v_conv_ln_g, v_conv_ln_b, v_conv_w2, v_conv_b2, v_kv_norm_g, v_w_k, v_w_v, v_b_norm_g, v_w_q, v_w_o, v_ffn_norm_g, v_ffn_w_gate, v_ffn_w_up, v_ffn_w_down, v_final_norm_g):
    s, d = x.shape[1], x.shape[2]
    dc = d // ND
    h0 = x[0]
    target = loss_target[0]
    xi, yi, ci = lax.axis_index("x"), lax.axis_index("y"), lax.axis_index("c")
    me = 4 * xi + 2 * yi + ci
    c_idx = jnp.reshape(ci, (1,)).astype(jnp.int32)
    q_idx = jnp.reshape(2 * xi + yi, (1,)).astype(jnp.int32)

    bf = lambda w: w.astype(BF16)
    small_shards = [a_norm_g, conv_b1, conv_dw, conv_dw_b, conv_ln_g, conv_ln_b, conv_b2]
    sp, sp_spans = _pack_rows(small_shards, dc)
    w1g, spg = _all_gather([bf(conv_w1[0]), sp], "gather_first")
    spg = spg.reshape(ND, -1)

    def small_full(i, rows):
        at, size = sp_spans[i]
        return spg[:, at:at + size].reshape(ND, rows, size // rows).transpose(1, 0, 2).reshape(rows, -1)

    a_g = small_full(0, 1)
    b1 = small_full(1, 1)
    dw = jnp.pad(small_full(2, CONV_W), ((0, CONV_PAD - CONV_W), (0, 0)))
    dwb, lng, lnb, b2 = small_full(3, 1), small_full(4, 1), small_full(5, 1), small_full(6, 1)
    kv_g, q_g, fin_g = kv_norm_g.reshape(1, d), b_norm_g.reshape(1, d), final_norm_g.reshape(1, d)
    f_g = [ffn_norm_g[0:1], ffn_norm_g[1:2]]

    def send(*shards):
        return _job_gather_send([bf(t) for t in shards])

    def forward(job):
        return _job_gather_forward(job.result)

    (n1,) = _rms_fwd(h0, [a_g], "rms_a")
    s_a = send(conv_w2[0], ffn_w_gate[0])
    ua, ug, glu = _glu_mm(n1, w1g, b1, jobs=[s_a])
    f_a, s_b = forward(s_a), send(ffn_w_up[0])
    cv, sw = _conv_fwd(glu, dw, dwb, lng, lnb, jobs=[f_a, s_b])
    w2g, wg0 = f_a.result
    f_b, s_c = forward(s_b), send(ffn_w_down[0])
    h1 = _mm_rows(sw, w2g, "w2_mm", res=h0, bias=b2, jobs=[f_b, s_c])
    (wu0,) = f_b.result
    (n2a,) = _rms_fwd(h1, [f_g[0]], "rms_f0")
    f_c, s_d = forward(s_c), send(w_k, w_v, w_q[0], w_o[0])
    gate0, up0, act0 = _swiglu_mm(n2a, wg0, wu0, "swiglu_mm0", jobs=[f_c, s_d])
    (wd0,) = f_c.result
    f_d, s_e = forward(s_d), send(ffn_w_gate[1])
    h2 = _down_mm(act0, wd0, h1, "down_mm0", jobs=[f_d, s_e])
    wkg, wvg, wqg, wog = f_d.result
    kvn, qn = _rms_fwd(h2, [kv_g, q_g], "rms_kvq")
    f_e = forward(s_e)
    kk = _mm_rows(kvn, wkg, "k_mm", out_dtype=BF16, jobs=[f_e])
    (wg1,) = f_e.result
    vv = _mm_rows(kvn, wvg, "v_mm", out_dtype=BF16)
    qq = _mm_rows(qn, wqg, "q_mm", out_dtype=BF16)
    branch = {dil: tuple(_to_branch(t, dil) for t in (qq, kk, vv)) for dil in BRANCH_DILATIONS}
    s_f = send(ffn_w_up[1])
    o1, l1 = _attn_fwd(*branch[1], 1, jobs=[s_f])
    f_f, s_g = forward(s_f), send(ffn_w_down[1])
    o4, l4 = _attn_fwd(*branch[4], 4, jobs=[f_f, s_g])
    (wu1,) = f_f.result
    f_h = forward(s_g)
    o16, l16 = _attn_fwd(*branch[16], 16, jobs=[f_h])
    (wd1,) = f_h.result
    outs = [_from_branch(o, dil) for o, dil in zip((o1, o4, o16), BRANCH_DILATIONS)]
    lses = [_from_branch(o, dil) for o, dil in zip((l1, l4, l16), BRANCH_DILATIONS)]
    att, lse = _attn_merge(outs, lses)
    h3 = _mm_rows(att, wog, "wo_mm", res=h2)
    (n2b,) = _rms_fwd(h3, [f_g[1]], "rms_f1")
    gate1, up1, act1 = _swiglu_mm(n2b, wg1, wu1, "swiglu_mm1")
    h4 = _down_mm(act1, wd1, h3, "down_mm1")

    flat = lambda g: g.reshape(ND, -1, g.shape[-1])
    chip_sums, cross = {}, {}

    def to_sibling(**grads):
        job = _job_scatter_sibling([flat(g) for g in grads.values()])
        job.names = list(grads)
        return job

    def add_up(job):
        for n, g, r in zip(job.names, job.ins, job.result):
            chip_sums[n] = _rs_add(g, r, c_idx, f"rs_add_{n}")

    def to_chips(*names):
        job = _job_scatter_cross([chip_sums[n] for n in names])
        job.names = names
        return job

    def landed(job):
        cross.update(zip(job.names, job.result))

    dh4, dh4b, d_fin, loss_row = _final_loss(h4, target, fin_g)
    dgate1, dup1 = _dact_mm(dh4b, wd1, gate1, up1, "dact_mm1")
    g_wd1 = _dwd_mm(act1, dh4b, "dwd_mm1")
    j1 = to_sibling(wd1=g_wd1)
    g_wg1, g_wu1 = _dwgu_mm(n2b, dgate1, dup1, "dwgu_mm1", jobs=[j1])
    add_up(j1)
    j2, j3 = to_chips("wd1"), to_sibling(wg1=g_wg1, wu1=g_wu1)
    dn2b = _dn_ffn_mm(dgate1, dup1, wg1, wu1, "dn_ffn_mm1", jobs=[j2, j3])
    landed(j2)
    add_up(j3)
    dh3, dh3b, d_f1 = _rms_bwd(h3, [(f_g[1], dn2b)], dh4, "rms_f1_bwd")
    g_wo = _dw_rows_mm(att, dh3b, "dwo_mm")
    j4 = to_sibling(wo=g_wo)
    datt = _mm_rows_t([(dh3b, wog)], "datt_mm", BF16, jobs=[j4])
    add_up(j4)
    riders = {1: to_chips("wg1"), 4: to_chips("wu1"), 16: to_chips("wo")}
    dqs, dks, dvs = [], [], []
    for dil in BRANCH_DILATIONS:
        qb, kb, vb = branch[dil]
        dq_b, dk_b, dv_b = _attn_bwd(qb, kb, vb, _to_branch(datt, dil), _to_branch(att, dil), _to_branch(lse, dil),
                                     dil, jobs=[riders[dil]])
        landed(riders[dil])
        dqs.append(_from_branch(dq_b, dil))
        dks.append(_from_branch(dk_b, dil))
        dvs.append(_from_branch(dv_b, dil))
    dq, dk, dv = _sum_cast(dqs, "dq_sum"), _sum_cast(dks, "dk_sum"), _sum_cast(dvs, "dv_sum")
    g_wq = _dw_rows_mm(qn, dq, "dwq_mm")
    g_wk = _dw_rows_mm(kvn, dk, "dwk_mm")
    g_wv = _dw_rows_mm(kvn, dv, "dwv_mm")
    j5 = to_sibling(wq=g_wq, wk=g_wk, wv=g_wv)
    dqn = _mm_rows_t([(dq, wqg)], "dqn_mm", F32, jobs=[j5])
    add_up(j5)
    j6 = to_chips("wq", "wk")
    dkvn = _mm_rows_t([(dk, wkg), (dv, wvg)], "dkvn_mm", F32, jobs=[j6])
    landed(j6)
    dh2, dh2b, d_q, d_kv = _rms_bwd(h2, [(q_g, dqn), (kv_g, dkvn)], dh3, "rms_kvq_bwd")
    j7 = to_chips("wv")
    dgate0, dup0 = _dact_mm(dh2b, wd0, gate0, up0, "dact_mm0", jobs=[j7])
    landed(j7)
    g_wd0 = _dwd_mm(act0, dh2b, "dwd_mm0")
    j8 = to_sibling(wd0=g_wd0)
    g_wg0, g_wu0 = _dwgu_mm(n2a, dgate0, dup0, "dwgu_mm0", jobs=[j8])
    add_up(j8)
    j9, j10 = to_chips("wd0"), to_sibling(wg0=g_wg0, wu0=g_wu0)
    dn2a = _dn_ffn_mm(dgate0, dup0, wg0, wu0, "dn_ffn_mm0", jobs=[j9, j10])
    landed(j9)
    add_up(j10)
    dh1, dh1b, d_f0, d_b2 = _rms_bwd(h1, [(f_g[0], dn2a)], dh2, "rms_f0_bwd", colsum=True)
    g_w2 = _dw_rows_mm(sw, dh1b, "dw2_mm")
    j11, j12 = to_chips("wg0"), to_sibling(w2=g_w2)
    dsw = _mm_rows_t([(dh1b, w2g)], "dsw_mm", F32, jobs=[j11, j12])
    landed(j11)
    add_up(j12)
    dcv, d_lng, d_lnb = _ln_bwd(dsw, cv, lng, lnb)
    j13 = to_chips("wu0", "w2")
    du, d_dw, d_dwb, d_b1 = _conv_bwd(dcv, glu, ua, ug, dw, jobs=[j13])
    landed(j13)
    g_w1 = _dw1_mm(n1, du)
    j14 = to_sibling(w1=g_w1)
    dn1 = _dn1_mm(du, w1g, jobs=[j14])
    add_up(j14)
    dx, _, d_a = _rms_bwd(h0, [(a_g, dn1)], dh1, "rms_a_bwd")

    small_g = [d_a, d_b1, d_dw[:CONV_W], d_dwb, d_lng, d_lnb, d_b2, d_kv, d_q, d_f0, d_f1, d_fin, loss_row]
    gp, gp_spans = _pack_rows(small_g, d)
    j15, j16 = to_chips("w1"), _job_gather_send([gp])
    _comm_call([j15, j16], "rs_w1_send_small")
    landed(j15)
    j17 = _job_gather_forward(j16.result)
    _comm_call([j17], "forward_small")
    (gpg,) = j17.result

    two = lambda t: t.reshape(-1, t.shape[-1])

    def adam(w, m, v, names, tag):
        res = None
        for part, n in enumerate(names):
            res = _adamw_big(two(w), two(m), two(v), chip_sums[n], cross[n], q_idx, f"adamw_{tag}{part}", part, res)
        return [t.reshape(w.shape) for t in res]

    big_out = [
        adam(conv_w1, m_conv_w1, v_conv_w1, ["w1"], "w1"), adam(conv_w2, m_conv_w2, v_conv_w2, ["w2"], "w2"),
        adam(w_k, m_w_k, v_w_k, ["wk"], "wk"), adam(w_v, m_w_v, v_w_v, ["wv"], "wv"),
        adam(w_q, m_w_q, v_w_q, ["wq"], "wq"), adam(w_o, m_w_o, v_w_o, ["wo"], "wo"),
        adam(ffn_w_gate, m_ffn_w_gate, v_ffn_w_gate, ["wg0", "wg1"], "wg"),
        adam(ffn_w_up, m_ffn_w_up, v_ffn_w_up, ["wu0", "wu1"], "wu"),
        adam(ffn_w_down, m_ffn_w_down, v_ffn_w_down, ["wd0", "wd1"], "wd")]

    gsum = _sum_devices(gpg, "sum_small_grads").reshape(-1)

    def gfull(i):
        at, size = gp_spans[i]
        return gsum[at:at + size]

    def shard_of(vec, rows):
        return lax.dynamic_slice_in_dim(vec.reshape(rows, -1), me * (vec.size // rows // ND), vec.size // rows // ND, axis=1)

    loss = gfull(12)[0]
    small_grads = [
        shard_of(gfull(0), 1), shard_of(gfull(1), 1), shard_of(gfull(2), CONV_W)[None], shard_of(gfull(3), 1),
        shard_of(gfull(4), 1), shard_of(gfull(5), 1), shard_of(gfull(6), 1),
        gfull(7), gfull(8)[None], jnp.stack([gfull(9), gfull(10)]), gfull(11)]
    small_w = [a_norm_g, conv_b1, conv_dw, conv_dw_b, conv_ln_g, conv_ln_b, conv_b2, kv_norm_g, b_norm_g, ffn_norm_g, final_norm_g]
    small_m = [m_a_norm_g, m_conv_b1, m_conv_dw, m_conv_dw_b, m_conv_ln_g, m_conv_ln_b, m_conv_b2, m_kv_norm_g, m_b_norm_g, m_ffn_norm_g, m_final_norm_g]
    small_v = [v_a_norm_g, v_conv_b1, v_conv_dw, v_conv_dw_b, v_conv_ln_g, v_conv_ln_b, v_conv_b2, v_kv_norm_g, v_b_norm_g, v_ffn_norm_g, v_final_norm_g]
    small_grads = [g.reshape(w.shape) for g, w in zip(small_grads, small_w)]
    wp, spans = _pack_rows(small_w, 128)
    gpk, _ = _pack_rows(small_grads, 128)
    mp, _ = _pack_rows(small_m, 128)
    vp, _ = _pack_rows(small_v, 128)
    dp, mnp, vnp = _adamw_small(wp, gpk, mp, vp, "adamw_small")

    def unpack(packed):
        flat = packed.reshape(-1)
        return [flat[at:at + size].reshape(w.shape) for (at, size), w in zip(spans, small_w)]

    small_out = list(zip(small_grads, unpack(dp), unpack(mnp), unpack(vnp)))

    order = ["a_norm_g", "conv_w1", "conv_b1", "conv_dw", "conv_dw_b", "conv_ln_g", "conv_ln_b", "conv_w2", "conv_b2",
             "kv_norm_g", "w_k", "w_v", "b_norm_g", "w_q", "w_o", "ffn_norm_g", "ffn_w_gate", "ffn_w_up", "ffn_w_down",
             "final_norm_g"]
    big_names = ["conv_w1", "conv_w2", "w_k", "w_v", "w_q", "w_o", "ffn_w_gate", "ffn_w_up", "ffn_w_down"]
    small_names = ["a_norm_g", "conv_b1", "conv_dw", "conv_dw_b", "conv_ln_g", "conv_ln_b", "conv_b2", "kv_norm_g",
                   "b_norm_g", "ffn_norm_g", "final_norm_g"]
    table = {n: big_out[i] for i, n in enumerate(big_names)}
    table.update({n: small_out[i] for i, n in enumerate(small_names)})
    result = [loss, dx[None]]
    for kind in range(4):
        result += [table[n][kind] for n in order]
    return tuple(result)
```

```python
import functools

import jax
import jax.numpy as jnp
from jax import lax
from jax.experimental import pallas as pl
from jax.experimental.pallas import tpu as pltpu

ND = 8
HEAD = 128
BLK = 128
BRANCH_DILATIONS = (1, 4, 16)
CONV_W = 31
CONV_PAD = 32
RMS_EPS = 1e-6
LN_EPS = 1e-5
LR, B1, B2, ADAM_EPS, WD, STEP = 0.001, 0.9, 0.999, 1e-08, 0.01, 10
VMEM_LIMIT = 56 * 1024 * 1024

F32, BF16 = jnp.float32, jnp.bfloat16
SDS = jax.ShapeDtypeStruct
MESH = pl.DeviceIdType.MESH
ANY = pl.BlockSpec(memory_space=pl.ANY)

NN = (((1,), (0,)), ((), ()))
NT = (((1,), (1,)), ((), ()))
TN = (((0,), (0,)), ((), ()))


def _dot(a, b, dims):
    return lax.dot_general(a, b, dims, preferred_element_type=F32)


def _cp(*sem):
    return pltpu.CompilerParams(dimension_semantics=sem, vmem_limit_bytes=VMEM_LIMIT)


def _slot(dev):
    return 4 * (dev % 2) + dev // 2


def _sigmoid(v):
    return 1.0 / (1.0 + jnp.exp(-v))


class _Job:
    def __init__(self, ins, out_shapes, alias, nsem, nlocal, make):
        self.ins, self.out_shapes, self.alias = list(ins), list(out_shapes), dict(alias)
        self.nsem, self.nlocal, self.make = nsem, nlocal, make
        self.result = None


def _coords():
    return lax.axis_index("x"), lax.axis_index("y"), lax.axis_index("c")


def _remote(src, dst, send, recv, k, to):
    return pltpu.make_async_remote_copy(src_ref=src, dst_ref=dst, send_sem=send.at[k], recv_sem=recv.at[k],
                                        device_id=to, device_id_type=MESH)


def _job_gather_send(shards):
    n = len(shards)

    def make(ins, outs, send, recv, local):
        x, y, c = _coords()
        targets = [(x, y, 1 - c), (1 - x, y, c), (x, 1 - y, c), (1 - x, 1 - y, c)]
        cps = []
        for a in range(n):
            dst = outs[a].at[4 * x + 2 * y + c]
            cps.append(pltpu.make_async_copy(ins[a], dst, local.at[a]))
            cps += [_remote(ins[a], dst, send, recv, 4 * a + k, t) for k, t in enumerate(targets)]
        return cps

    return _Job(shards, [SDS((ND,) + s.shape, s.dtype) for s in shards], {}, 4 * n, n, make)


def _job_gather_forward(gathered):
    n = len(gathered)

    def make(ins, outs, send, recv, local):
        x, y, c = _coords()
        cps = []
        for a in range(n):
            for k, (px, py) in enumerate([(1 - x, y), (x, 1 - y), (1 - x, 1 - y)]):
                blk = outs[a].at[4 * px + 2 * py + c]
                cps.append(_remote(blk, blk, send, recv, 3 * a + k, (x, y, 1 - c)))
        return cps

    return _Job(gathered, [SDS(g.shape, g.dtype) for g in gathered], {i: i for i in range(n)}, 3 * n, 0, make)


def _job_scatter_sibling(grads):
    n = len(grads)

    def make(ins, outs, send, recv, local):
        x, y, c = _coords()
        return [_remote(ins[a].at[pl.ds(4 * (1 - c), 4)], outs[a], send, recv, a, (x, y, 1 - c)) for a in range(n)]

    return _Job(grads, [SDS((4,) + g.shape[1:], g.dtype) for g in grads], {}, n, 0, make)


def _job_scatter_cross(sums):
    n = len(sums)

    def make(ins, outs, send, recv, local):
        x, y, c = _coords()
        chips = [(1 - x, y), (x, 1 - y), (1 - x, 1 - y)]
        return [_remote(ins[a].at[2 * px + py], outs[a].at[k], send, recv, 3 * a + k, (px, py, c))
                for a in range(n) for k, (px, py) in enumerate(chips)]

    return _Job(sums, [SDS((3,) + t.shape[1:], t.dtype) for t in sums], {}, 3 * n, 0, make)


def _pc(body, *, name, grid, in_specs, out_specs, out_shape, args, scratch=(), sem=(), alias=None, jobs=()):
    jobs = list(jobs)
    n_in, n_out, n_scr = len(in_specs), len(out_shape), len(scratch)
    aliases = dict(alias or {})
    job_args, job_shapes, job_scratch = [], [], []
    for j in jobs:
        for src, dst in j.alias.items():
            aliases[n_in + len(job_args) + src] = n_out + len(job_shapes) + dst
        job_args += j.ins
        job_shapes += j.out_shapes
        job_scratch += [pltpu.SemaphoreType.DMA((j.nsem,)), pltpu.SemaphoreType.DMA((j.nsem,)),
                        pltpu.SemaphoreType.DMA((max(j.nlocal, 1),))]

    def wrapped(*refs):
        ins = refs[:n_in]
        p = n_in + len(job_args)
        outs = refs[p:p + n_out]
        p += n_out + len(job_shapes)
        scr = refs[p:p + n_scr]
        sems = refs[p + n_scr:]
        copies = []
        pi, po = n_in, n_in + len(job_args) + n_out
        for k, j in enumerate(jobs):
            copies += j.make(refs[pi:pi + len(j.ins)], refs[po:po + len(j.out_shapes)], *sems[3 * k:3 * k + 3])
            pi += len(j.ins)
            po += len(j.out_shapes)
        gridded = bool(copies) and bool(grid)
        if gridded:
            ids = [pl.program_id(i) for i in range(len(grid))]
            first = functools.reduce(jnp.logical_and, [i == 0 for i in ids])
            last = functools.reduce(jnp.logical_and, [i == g - 1 for i, g in zip(ids, grid)])

            @pl.when(first)
            def _():
                for cp in copies:
                    cp.start()
        else:
            for cp in copies:
                cp.start()
        body(*ins, *outs, *scr)
        if gridded:
            @pl.when(last)
            def _():
                for cp in copies:
                    cp.wait()
        else:
            for cp in copies:
                cp.wait()

    kw = dict(grid=grid) if grid else {}
    semantics = ["arbitrary"] * len(grid) if jobs else list(sem)
    res = pl.pallas_call(
        wrapped, name=name, in_specs=list(in_specs) + [ANY] * len(job_args),
        out_specs=list(out_specs) + [ANY] * len(job_shapes), out_shape=list(out_shape) + job_shapes,
        scratch_shapes=list(scratch) + job_scratch, input_output_aliases=aliases,
        compiler_params=_cp(*semantics), **kw)(*args, *job_args)
    p = n_out
    for j in jobs:
        j.result = list(res[p:p + len(j.out_shapes)])
        p += len(j.out_shapes)
    return list(res[:n_out])


def _comm_call(jobs, name):
    _pc(lambda: None, name=name, grid=(), in_specs=[], out_specs=[], out_shape=[], args=[], jobs=jobs)


def _all_gather(arrs, name):
    n = len(arrs)

    def body(*refs):
        ins, outs = refs[:n], refs[n:2 * n]
        send_sems, recv_sems, local_sems = refs[2 * n:]
        x, y, c = lax.axis_index("x"), lax.axis_index("y"), lax.axis_index("c")
        me, sib = (x, y, c), (x, y, 1 - c)
        chips = [(1 - x, y), (x, 1 - y), (1 - x, 1 - y)]

        def copy(a, k, block, to, src=None):
            dst = outs[a].at[4 * block[0] + 2 * block[1] + block[2]]
            return pltpu.make_async_remote_copy(
                src_ref=dst if src is None else src, dst_ref=dst,
                send_sem=send_sems.at[7 * a + k], recv_sem=recv_sems.at[7 * a + k],
                device_id=to, device_id_type=MESH)

        mine = [pltpu.make_async_copy(ins[a], outs[a].at[4 * x + 2 * y + c], local_sems.at[a]) for a in range(n)]
        for cp in mine:
            cp.start()
        first = []
        for a in range(n):
            first.append(copy(a, 0, me, sib, src=ins[a]))
            first += [copy(a, 1 + j, me, (*chip, c), src=ins[a]) for j, chip in enumerate(chips)]
        for cp in first:
            cp.start()
        passed = []
        for a in range(n):
            for j, chip in enumerate(chips):
                copy(a, 1 + j, (*chip, c), me).wait_recv()
                fwd = copy(a, 4 + j, (*chip, c), sib)
                fwd.start()
                passed.append(fwd)
        for a in range(n):
            copy(a, 0, sib, me).wait_recv()
            for j, chip in enumerate(chips):
                copy(a, 4 + j, (*chip, 1 - c), me).wait_recv()
        for cp in first + passed:
            cp.wait_send()
        for cp in mine:
            cp.wait()

    return pl.pallas_call(
        body, name=name,
        out_shape=[SDS((ND,) + a.shape, a.dtype) for a in arrs],
        in_specs=[ANY] * n, out_specs=[ANY] * n,
        scratch_shapes=[pltpu.SemaphoreType.DMA((7 * n,)), pltpu.SemaphoreType.DMA((7 * n,)),
                        pltpu.SemaphoreType.DMA((n,))],
    )(*arrs)


def _row_tile(rows):
    return next(t for t in (256, 128, 64, 32, 16) if rows % t == 0)


def _rs_add(g, r1, c_idx, name):
    _, rows, cols = g.shape
    tr = _row_tile(rows)

    def body(c_ref, g_ref, r_ref, o_ref):
        o_ref[...] = (g_ref[...].astype(F32) + r_ref[...].astype(F32)).astype(o_ref.dtype)

    return pl.pallas_call(
        body, name=name,
        grid_spec=pltpu.PrefetchScalarGridSpec(
            num_scalar_prefetch=1, grid=(4, rows // tr),
            in_specs=[pl.BlockSpec((1, tr, cols), lambda q, i, c: (4 * c[0] + q, i, 0)),
                      pl.BlockSpec((1, tr, cols), lambda q, i, c: (q, i, 0))],
            out_specs=pl.BlockSpec((1, tr, cols), lambda q, i, c: (q, i, 0))),
        out_shape=SDS((4, rows, cols), g.dtype),
        compiler_params=_cp("parallel", "parallel"),
    )(c_idx, g, r1)


def _adam_math(w, g, m, v):
    m = B1 * m + (1.0 - B1) * g
    v = B2 * v + (1.0 - B2) * (g * g)
    m_hat = m / (1.0 - B1 ** STEP)
    v_hat = v / (1.0 - B2 ** STEP)
    delta = -LR * (m_hat / (jnp.sqrt(v_hat) + ADAM_EPS) + WD * w)
    return delta, m, v


def _adamw_big(w, m, v, t, r2, q_idx, name, part=0, prev=None):
    _, rows, cols = t.shape
    tr = _row_tile(rows)
    nblk = rows // tr

    def body(q_ref, w_ref, m_ref, v_ref, t_ref, r_ref, *outs):
        g_out, d_out, m_out, v_out = outs[-4:]
        g = t_ref[0].astype(F32)
        for k in range(3):
            g = g + r_ref[k].astype(F32)
        d, mn, vn = _adam_math(w_ref[...], g, m_ref[...], v_ref[...])
        g_out[...], d_out[...], m_out[...], v_out[...] = g, d, mn, vn

    blk = pl.BlockSpec((tr, cols), lambda i, q: (part * nblk + i, 0))
    specs = [blk, blk, blk, pl.BlockSpec((1, tr, cols), lambda i, q: (q[0], i, 0)),
             pl.BlockSpec((3, tr, cols), lambda i, q: (0, i, 0))]
    ins = [q_idx, w, m, v, t, r2]
    alias = {}
    if prev is not None:
        specs += [ANY] * 4
        alias = {6 + k: k for k in range(4)}
        ins += list(prev)
    return pl.pallas_call(
        body, name=name,
        grid_spec=pltpu.PrefetchScalarGridSpec(num_scalar_prefetch=1, grid=(nblk,), in_specs=specs, out_specs=[blk] * 4),
        out_shape=[SDS(w.shape, F32)] * 4, input_output_aliases=alias,
        compiler_params=_cp("parallel"))(*ins)


def _sum_devices(g, name):
    _, rows, cols = g.shape

    def body(g_ref, o_ref):
        acc = g_ref[0]
        for k in range(1, ND):
            acc = acc + g_ref[k]
        o_ref[...] = acc

    return pl.pallas_call(body, name=name, out_shape=SDS((rows, cols), F32))(g)


def _adamw_small(w, g, m, v, name):
    def body(w_ref, g_ref, m_ref, v_ref, d_out, m_out, v_out):
        d, mn, vn = _adam_math(w_ref[...], g_ref[...], m_ref[...], v_ref[...])
        d_out[...], m_out[...], v_out[...] = d, mn, vn

    return pl.pallas_call(body, name=name, out_shape=[SDS(w.shape, F32)] * 3)(w, g, m, v)


ROWS = 256


def _rms_stats(x):
    r = lax.rsqrt(jnp.mean(x * x, axis=-1, keepdims=True) + RMS_EPS)
    return x * r, r


def _rms_fwd(x, gains, name):
    s, d = x.shape
    n = len(gains)

    def body(x_ref, *refs):
        xh, _ = _rms_stats(x_ref[...])
        for g_ref, o_ref in zip(refs[:n], refs[n:]):
            o_ref[...] = (xh * g_ref[...]).astype(BF16)

    row = pl.BlockSpec((ROWS, d), lambda i: (i, 0))
    vec = pl.BlockSpec((1, d), lambda i: (0, 0))
    return pl.pallas_call(
        body, name=name, grid=(s // ROWS,), in_specs=[row] + [vec] * n, out_specs=[row] * n,
        out_shape=[SDS((s, d), BF16)] * n, compiler_params=_cp("parallel"))(x, *gains)


def _rms_bwd_rows(xh, r, gain, dy):
    u = dy * gain
    return r * (u - xh * jnp.mean(u * xh, axis=-1, keepdims=True))


def _rms_bwd(x, pairs, dres, name, colsum=False):
    s, d = x.shape
    n = len(pairs)

    def body(x_ref, dres_ref, *refs):
        g_refs, dy_refs = refs[:n], refs[n:2 * n]
        dx_ref, dxb_ref = refs[2 * n], refs[2 * n + 1]
        dg_refs = refs[2 * n + 2:2 * n + 2 + n]
        cs_ref = refs[-1] if colsum else None
        first = pl.program_id(0) == 0
        xh, r = _rms_stats(x_ref[...])
        dx = dres_ref[...]
        for g_ref, dy_ref, dg_ref in zip(g_refs, dy_refs, dg_refs):
            dy = dy_ref[...]
            dx = dx + _rms_bwd_rows(xh, r, g_ref[...], dy)

            @pl.when(first)
            def _():
                dg_ref[...] = jnp.zeros_like(dg_ref)
            dg_ref[...] += jnp.sum(dy * xh, axis=0, keepdims=True)
        dx_ref[...] = dx
        dxb_ref[...] = dx.astype(BF16)
        if colsum:
            @pl.when(first)
            def _():
                cs_ref[...] = jnp.zeros_like(cs_ref)
            cs_ref[...] += jnp.sum(dx, axis=0, keepdims=True)

    row = pl.BlockSpec((ROWS, d), lambda i: (i, 0))
    vec = pl.BlockSpec((1, d), lambda i: (0, 0))
    nvec = n + (1 if colsum else 0)
    outs = pl.pallas_call(
        body, name=name, grid=(s // ROWS,),
        in_specs=[row, row] + [vec] * n + [row] * n,
        out_specs=[row, row] + [vec] * nvec,
        out_shape=[SDS((s, d), F32), SDS((s, d), BF16)] + [SDS((1, d), F32)] * nvec,
        compiler_params=_cp("arbitrary"),
    )(x, dres, *[p[0] for p in pairs], *[p[1] for p in pairs])
    return outs


def _final_loss(h, target, gain):
    s, d = h.shape

    def body(h_ref, t_ref, g_ref, dh_ref, dhb_ref, dg_ref, loss_ref):
        first = pl.program_id(0) == 0
        xh, r = _rms_stats(h_ref[...])
        gain_v = g_ref[...]
        e = xh * gain_v - t_ref[...]
        dy = e * (1.0 / d)
        dx = _rms_bwd_rows(xh, r, gain_v, dy)
        dh_ref[...] = dx
        dhb_ref[...] = dx.astype(BF16)

        @pl.when(first)
        def _():
            dg_ref[...] = jnp.zeros_like(dg_ref)
            loss_ref[...] = jnp.zeros_like(loss_ref)
        dg_ref[...] += jnp.sum(dy * xh, axis=0, keepdims=True)
        loss_ref[...] += jnp.full((1, 128), 0.5 / d, F32) * jnp.sum(e * e)

    row = pl.BlockSpec((ROWS, d), lambda i: (i, 0))
    vec = pl.BlockSpec((1, d), lambda i: (0, 0))
    return pl.pallas_call(
        body, name="final_loss", grid=(s // ROWS,),
        in_specs=[row, row, vec], out_specs=[row, row, vec, pl.BlockSpec((1, 128), lambda i: (0, 0))],
        out_shape=[SDS((s, d), F32), SDS((s, d), BF16), SDS((1, d), F32), SDS((1, 128), F32)],
        compiler_params=_cp("arbitrary"))(h, target, gain)


CT = 128


def _ln_stats(cv):
    mu = jnp.mean(cv, axis=-1, keepdims=True)
    xc = cv - mu
    rstd = lax.rsqrt(jnp.mean(xc * xc, axis=-1, keepdims=True) + LN_EPS)
    return xc * rstd, rstd


def _conv_fwd(glu, dw, dwb, lng, lnb, jobs=()):
    s, d = glu.shape
    hb = CT // CONV_PAD

    def body(x_ref, halo_ref, dw_ref, dwb_ref, lng_ref, lnb_ref, c_ref, s_ref):
        keep = (pl.program_id(0) > 0).astype(F32)

        def chunk(ci, carry):
            ls = pl.ds(pl.multiple_of(ci * 128, 128), 128)
            xf = jnp.concatenate([halo_ref[:, ls] * keep, x_ref[:, ls]], axis=0)
            acc = jnp.zeros((CT, 128), F32)
            for k in range(CONV_W):
                sh = CONV_W - 1 - k
                xs = pltpu.roll(xf, sh, 0) if sh else xf
                acc = acc + dw_ref[pl.ds(k, 1), ls] * xs[CONV_PAD:]
            c_ref[:, ls] = acc + dwb_ref[:, ls]
            return carry

        lax.fori_loop(0, d // 128, chunk, 0)
        xh, _ = _ln_stats(c_ref[...])
        yv = xh * lng_ref[...] + lnb_ref[...]
        s_ref[...] = (yv * _sigmoid(yv)).astype(BF16)

    row = pl.BlockSpec((CT, d), lambda i: (i, 0))
    halo = pl.BlockSpec((CONV_PAD, d), lambda i: (jnp.maximum(i * hb - 1, 0), 0))
    vec = pl.BlockSpec((1, d), lambda i: (0, 0))
    taps = pl.BlockSpec((CONV_PAD, d), lambda i: (0, 0))
    return _pc(
        body, name="conv_fwd", grid=(s // CT,),
        in_specs=[row, halo, taps, vec, vec, vec], out_specs=[row, row],
        out_shape=[SDS((s, d), F32), SDS((s, d), BF16)], sem=("parallel",),
        args=(glu, glu, dw, dwb, lng, lnb), jobs=jobs)


def _ln_bwd(ds, cv, lng, lnb):
    s, d = cv.shape

    def body(ds_ref, c_ref, g_ref, b_ref, dc_ref, dg_ref, db_ref):
        first = pl.program_id(0) == 0
        xh, rstd = _ln_stats(c_ref[...])
        gv = g_ref[...]
        yv = xh * gv + b_ref[...]
        sg = _sigmoid(yv)
        dln = ds_ref[...] * (sg * (1.0 + yv * (1.0 - sg)))
        dxh = dln * gv
        dc_ref[...] = rstd * (dxh - jnp.mean(dxh, axis=-1, keepdims=True)
                              - xh * jnp.mean(dxh * xh, axis=-1, keepdims=True))

        @pl.when(first)
        def _():
            dg_ref[...] = jnp.zeros_like(dg_ref)
            db_ref[...] = jnp.zeros_like(db_ref)
        dg_ref[...] += jnp.sum(dln * xh, axis=0, keepdims=True)
        db_ref[...] += jnp.sum(dln, axis=0, keepdims=True)

    row = pl.BlockSpec((ROWS, d), lambda i: (i, 0))
    vec = pl.BlockSpec((1, d), lambda i: (0, 0))
    return pl.pallas_call(
        body, name="ln_bwd", grid=(s // ROWS,), in_specs=[row, row, vec, vec], out_specs=[row, vec, vec],
        out_shape=[SDS((s, d), F32), SDS((1, d), F32), SDS((1, d), F32)],
        compiler_params=_cp("arbitrary"))(ds, cv, lng, lnb)


def _conv_bwd(dc, glu, ua, ug, dw, jobs=()):
    s, d = dc.shape
    hb = CT // CONV_PAD
    nsteps = s // CT
    full = CT + CONV_PAD

    def body(dc_ref, dcn_ref, x_ref, xp_ref, ua_ref, ug_ref, dw_ref, du_ref, ddw_ref, ddwb_ref, db1_ref):
        i = pl.program_id(0)
        keep_prev = (i > 0).astype(F32)
        keep_next = (i < nsteps - 1).astype(F32)

        @pl.when(i == 0)
        def _():
            ddw_ref[...] = jnp.zeros_like(ddw_ref)
            ddwb_ref[...] = jnp.zeros_like(ddwb_ref)
            db1_ref[...] = jnp.zeros_like(db1_ref)

        def chunk(ci, carry):
            off = pl.multiple_of(ci * 128, 128)
            ls = pl.ds(off, 128)
            ls2 = pl.ds(pl.multiple_of(d + ci * 128, 128), 128)
            dcc = dc_ref[:, ls]
            dcf = jnp.concatenate([dcc, dcn_ref[:, ls] * keep_next], axis=0)
            xf = jnp.concatenate([xp_ref[:, ls] * keep_prev, x_ref[:, ls]], axis=0)
            dglu = jnp.zeros((CT, 128), F32)
            for k in range(CONV_W):
                sh = CONV_W - 1 - k
                dshift = pltpu.roll(dcf, full - sh, 0) if sh else dcf
                dglu = dglu + dw_ref[pl.ds(k, 1), ls] * dshift[:CT]
                xs = pltpu.roll(xf, sh, 0) if sh else xf
                ddw_ref[pl.ds(k, 1), ls] += jnp.sum(dcc * xs[CONV_PAD:], axis=0, keepdims=True)
            ddwb_ref[:, ls] += jnp.sum(dcc, axis=0, keepdims=True)
            av, gv = ua_ref[:, ls], ug_ref[:, ls]
            sg = _sigmoid(gv)
            da = dglu * sg
            dgt = dglu * av * sg * (1.0 - sg)
            du_ref[:, ls] = da.astype(BF16)
            du_ref[:, ls2] = dgt.astype(BF16)
            db1_ref[:, ls] += jnp.sum(da, axis=0, keepdims=True)
            db1_ref[:, ls2] += jnp.sum(dgt, axis=0, keepdims=True)
            return carry

        lax.fori_loop(0, d // 128, chunk, 0)

    row = pl.BlockSpec((CT, d), lambda i: (i, 0))
    prev = pl.BlockSpec((CONV_PAD, d), lambda i: (jnp.maximum(i * hb - 1, 0), 0))
    nxt = pl.BlockSpec((CONV_PAD, d), lambda i: (jnp.minimum((i + 1) * hb, s // CONV_PAD - 1), 0))
    taps = pl.BlockSpec((CONV_PAD, d), lambda i: (0, 0))
    return _pc(
        body, name="conv_bwd", grid=(nsteps,),
        in_specs=[row, nxt, row, prev, row, row, taps],
        out_specs=[pl.BlockSpec((CT, 2 * d), lambda i: (i, 0)), taps, pl.BlockSpec((1, d), lambda i: (0, 0)),
                   pl.BlockSpec((1, 2 * d), lambda i: (0, 0))],
        out_shape=[SDS((s, 2 * d), BF16), SDS((CONV_PAD, d), F32), SDS((1, d), F32), SDS((1, 2 * d), F32)],
        sem=("arbitrary",), args=(dc, dc, glu, glu, ua, ug, dw), jobs=jobs)


TM = 1024
TS = 1024


def _glu_mm(n1, w1g, b1, jobs=()):
    s, d = n1.shape
    cw = w1g.shape[2]
    half = ND // 2

    def body(a_ref, wa_ref, wg_ref, ba_ref, bg_ref, ua_ref, ug_ref, glu_ref):
        a = a_ref[...]
        ua = _dot(a, wa_ref[0], NN) + ba_ref[...]
        ug = _dot(a, wg_ref[0], NN) + bg_ref[...]
        ua_ref[...], ug_ref[...] = ua, ug
        glu_ref[...] = ua * _sigmoid(ug)

    out = pl.BlockSpec((TM, cw), lambda m, i: (m, i))
    return _pc(
        body, name="glu_mm", grid=(s // TM, half),
        in_specs=[pl.BlockSpec((TM, d), lambda m, i: (m, 0)),
                  pl.BlockSpec((1, d, cw), lambda m, i: (i, 0, 0)),
                  pl.BlockSpec((1, d, cw), lambda m, i: (i + half, 0, 0)),
                  pl.BlockSpec((1, cw), lambda m, i: (0, i)),
                  pl.BlockSpec((1, cw), lambda m, i: (0, i + half))],
        out_specs=[out, out, out], out_shape=[SDS((s, d), F32)] * 3,
        sem=("parallel", "arbitrary"), args=(n1, w1g, w1g, b1, b1), jobs=jobs)


def _mm_rows(a, wg, name, res=None, bias=None, out_dtype=F32, tn=512, jobs=()):
    s, kdim = a.shape
    _, kc, n = wg.shape
    assert kc * ND == kdim

    def body(a_ref, w_ref, *refs):
        o_ref = refs[-1]
        acc = _dot(a_ref[...], w_ref[...].reshape(kdim, tn), NN)
        rest = list(refs[:-1])
        if res is not None:
            acc = acc + rest.pop(0)[...]
        if bias is not None:
            acc = acc + rest.pop(0)[...]
        o_ref[...] = acc.astype(out_dtype)

    ins, specs = [a, wg], [pl.BlockSpec((TM, kdim), lambda m, j: (m, 0)), pl.BlockSpec((ND, kc, tn), lambda m, j: (0, 0, j))]
    if res is not None:
        ins.append(res)
        specs.append(pl.BlockSpec((TM, tn), lambda m, j: (m, j)))
    if bias is not None:
        ins.append(bias)
        specs.append(pl.BlockSpec((1, tn), lambda m, j: (0, j)))
    return _pc(
        body, name=name, grid=(s // TM, n // tn), in_specs=specs,
        out_specs=[pl.BlockSpec((TM, tn), lambda m, j: (m, j))], out_shape=[SDS((s, n), out_dtype)],
        sem=("parallel", "arbitrary"), args=ins, jobs=jobs)[0]


def _swiglu_mm(n2, wgg, wug, name, jobs=()):
    s, d = n2.shape
    fc = wgg.shape[2]

    def body(a_ref, wg_ref, wu_ref, g_ref, u_ref, act_ref):
        a = a_ref[...]
        g = _dot(a, wg_ref[0], NN)
        u = _dot(a, wu_ref[0], NN)
        g_ref[0], u_ref[0] = g.astype(BF16), u.astype(BF16)
        act_ref[0] = (g * _sigmoid(g) * u).astype(BF16)

    wspec = pl.BlockSpec((1, d, fc), lambda m, j: (j, 0, 0))
    out = pl.BlockSpec((1, TM, fc), lambda m, j: (j, m, 0))
    return _pc(
        body, name=name, grid=(s // TM, ND),
        in_specs=[pl.BlockSpec((TM, d), lambda m, j: (m, 0)), wspec, wspec],
        out_specs=[out, out, out], out_shape=[SDS((ND, s, fc), BF16)] * 3,
        sem=("parallel", "arbitrary"), args=(n2, wgg, wug), jobs=jobs)


def _down_mm(act, wdg, res, name, jobs=()):
    _, s, fc = act.shape
    d = wdg.shape[2]

    def body(a_ref, w_ref, r_ref, o_ref):
        @pl.when(pl.program_id(1) == 0)
        def _():
            o_ref[...] = r_ref[...]
        o_ref[...] += _dot(a_ref[0], w_ref[0], NN)

    row = pl.BlockSpec((TM, d), lambda m, j: (m, 0))
    return _pc(
        body, name=name, grid=(s // TM, ND),
        in_specs=[pl.BlockSpec((1, TM, fc), lambda m, j: (j, m, 0)),
                  pl.BlockSpec((1, fc, d), lambda m, j: (j, 0, 0)), row],
        out_specs=[row], out_shape=[SDS((s, d), F32)],
        sem=("parallel", "arbitrary"), args=(act, wdg, res), jobs=jobs)[0]


def _dact_mm(dh, wdg, gate, up, name, jobs=()):
    s, d = dh.shape
    fc = wdg.shape[1]

    def body(a_ref, w_ref, g_ref, u_ref, dg_ref, du_ref):
        dact = _dot(a_ref[...], w_ref[0], NT)
        g, u = g_ref[0].astype(F32), u_ref[0].astype(F32)
        sg = _sigmoid(g)
        du_ref[0] = (dact * (g * sg)).astype(BF16)
        dg_ref[0] = (dact * u * (sg * (1.0 + g * (1.0 - sg)))).astype(BF16)

    blk = pl.BlockSpec((1, TM, fc), lambda m, j: (j, m, 0))
    return _pc(
        body, name=name, grid=(s // TM, ND),
        in_specs=[pl.BlockSpec((TM, d), lambda m, j: (m, 0)),
                  pl.BlockSpec((1, fc, d), lambda m, j: (j, 0, 0)), blk, blk],
        out_specs=[blk, blk], out_shape=[SDS((ND, s, fc), BF16)] * 2,
        sem=("parallel", "arbitrary"), args=(dh, wdg, gate, up), jobs=jobs)


def _dwd_mm(act, dh, name, jobs=()):
    _, s, fc = act.shape
    d = dh.shape[1]
    nk = s // TS

    def body(a_ref, b_ref, o_ref, acc):
        k = pl.program_id(1)

        @pl.when(k == 0)
        def _():
            acc[...] = jnp.zeros_like(acc)
        acc[...] += _dot(a_ref[0], b_ref[...], TN)

        @pl.when(k == nk - 1)
        def _():
            o_ref[0] = acc[...].astype(BF16)

    return _pc(
        body, name=name, grid=(ND, nk),
        in_specs=[pl.BlockSpec((1, TS, fc), lambda j, k: (j, k, 0)), pl.BlockSpec((TS, d), lambda j, k: (k, 0))],
        out_specs=[pl.BlockSpec((1, fc, d), lambda j, k: (_slot(j), 0, 0))],
        out_shape=[SDS((ND, fc, d), BF16)], scratch=[pltpu.VMEM((fc, d), F32)],
        sem=("parallel", "arbitrary"), args=(act, dh), jobs=jobs)[0]


def _dwgu_mm(n2, dgate, dup, name, jobs=()):
    s, d = n2.shape
    fc = dgate.shape[2]
    nk = s // TS

    def body(a_ref, g_ref, u_ref, og_ref, ou_ref, accg, accu):
        k = pl.program_id(1)

        @pl.when(k == 0)
        def _():
            accg[...] = jnp.zeros_like(accg)
            accu[...] = jnp.zeros_like(accu)
        a = a_ref[...]
        accg[...] += _dot(a, g_ref[0], TN)
        accu[...] += _dot(a, u_ref[0], TN)

        @pl.when(k == nk - 1)
        def _():
            og_ref[0] = accg[...].astype(BF16)
            ou_ref[0] = accu[...].astype(BF16)

    blk = pl.BlockSpec((1, TS, fc), lambda j, k: (j, k, 0))
    out = pl.BlockSpec((1, d, fc), lambda j, k: (_slot(j), 0, 0))
    return _pc(
        body, name=name, grid=(ND, nk),
        in_specs=[pl.BlockSpec((TS, d), lambda j, k: (k, 0)), blk, blk], out_specs=[out, out],
        out_shape=[SDS((ND, d, fc), BF16)] * 2,
        scratch=[pltpu.VMEM((d, fc), F32), pltpu.VMEM((d, fc), F32)],
        sem=("parallel", "arbitrary"), args=(n2, dgate, dup), jobs=jobs)


def _dn_ffn_mm(dgate, dup, wgg, wug, name, jobs=()):
    _, s, fc = dgate.shape
    d = wgg.shape[1]

    def body(g_ref, u_ref, wg_ref, wu_ref, o_ref):
        j = pl.program_id(1)

        @pl.when(j == 0)
        def _():
            o_ref[...] = jnp.zeros_like(o_ref)
        o_ref[...] += _dot(g_ref[0], wg_ref[0], NT) + _dot(u_ref[0], wu_ref[0], NT)

    blk = pl.BlockSpec((1, TM, fc), lambda m, j: (j, m, 0))
    wspec = pl.BlockSpec((1, d, fc), lambda m, j: (j, 0, 0))
    return _pc(
        body, name=name, grid=(s // TM, ND), in_specs=[blk, blk, wspec, wspec],
        out_specs=[pl.BlockSpec((TM, d), lambda m, j: (m, 0))], out_shape=[SDS((s, d), F32)],
        sem=("parallel", "arbitrary"), args=(dgate, dup, wgg, wug), jobs=jobs)[0]


def _mm_rows_t(pairs, name, out_dtype, jobs=()):
    s, n = pairs[0][0].shape
    _, kc, _ = pairs[0][1].shape
    np_ = len(pairs)

    def body(*refs):
        o_ref = refs[-1]
        acc = None
        for p in range(np_):
            t = _dot(refs[p][...], refs[np_ + p][0], NT)
            acc = t if acc is None else acc + t
        o_ref[...] = acc.astype(out_dtype)

    return _pc(
        body, name=name, grid=(s // TM, ND),
        in_specs=[pl.BlockSpec((TM, n), lambda m, j: (m, 0))] * np_ + [pl.BlockSpec((1, kc, n), lambda m, j: (j, 0, 0))] * np_,
        out_specs=[pl.BlockSpec((TM, kc), lambda m, j: (m, j))], out_shape=[SDS((s, kc * ND), out_dtype)],
        sem=("parallel", "arbitrary"), args=[p[0] for p in pairs] + [p[1] for p in pairs], jobs=jobs)[0]


def _dw_rows_mm(a, b, name):
    s, kdim = a.shape
    n = b.shape[1]
    kc = kdim // ND
    ts = TS // 2
    nk = s // ts

    def body(a_ref, b_ref, o_ref, acc):
        k = pl.program_id(0)

        @pl.when(k == 0)
        def _():
            acc[...] = jnp.zeros_like(acc)
        acc[...] += _dot(a_ref[...], b_ref[...], TN)

        @pl.when(k == nk - 1)
        def _():
            for dev in range(ND):
                o_ref[_slot(dev)] = acc[kc * dev:kc * (dev + 1), :].astype(BF16)

    return pl.pallas_call(
        body, name=name, grid=(nk,),
        in_specs=[pl.BlockSpec((ts, kdim), lambda k: (k, 0)), pl.BlockSpec((ts, n), lambda k: (k, 0))],
        out_specs=pl.BlockSpec((ND, kc, n), lambda k: (0, 0, 0)), out_shape=SDS((ND, kc, n), BF16),
        scratch_shapes=[pltpu.VMEM((kdim, n), F32)], compiler_params=_cp("arbitrary"))(a, b)


def _dw1_mm(n1, du):
    s, d = n1.shape
    cw = du.shape[1] // ND
    nk = s // TS

    def body(a_ref, b_ref, o_ref, acc):
        k = pl.program_id(1)

        @pl.when(k == 0)
        def _():
            acc[...] = jnp.zeros_like(acc)
        acc[...] += _dot(a_ref[...], b_ref[...], TN)

        @pl.when(k == nk - 1)
        def _():
            o_ref[0] = acc[...].astype(BF16)

    return pl.pallas_call(
        body, name="dw1_mm", grid=(ND, nk),
        in_specs=[pl.BlockSpec((TS, d), lambda j, k: (k, 0)), pl.BlockSpec((TS, cw), lambda j, k: (k, j))],
        out_specs=pl.BlockSpec((1, d, cw), lambda j, k: (_slot(j), 0, 0)), out_shape=SDS((ND, d, cw), BF16),
        scratch_shapes=[pltpu.VMEM((d, cw), F32)], compiler_params=_cp("parallel", "arbitrary"))(n1, du)


def _dn1_mm(du, w1g, jobs=()):
    s = du.shape[0]
    _, d, cw = w1g.shape

    def body(a_ref, w_ref, o_ref):
        j = pl.program_id(1)

        @pl.when(j == 0)
        def _():
            o_ref[...] = jnp.zeros_like(o_ref)
        o_ref[...] += _dot(a_ref[...], w_ref[0], NT)

    return _pc(
        body, name="dn1_mm", grid=(s // TM, ND),
        in_specs=[pl.BlockSpec((TM, cw), lambda m, j: (m, j)), pl.BlockSpec((1, d, cw), lambda m, j: (j, 0, 0))],
        out_specs=[pl.BlockSpec((TM, d), lambda m, j: (m, 0))], out_shape=[SDS((s, d), F32)],
        sem=("parallel", "arbitrary"), args=(du, w1g), jobs=jobs)[0]


NEG = -1e30


def _slopes(heads):
    return [2.0 ** (-8.0 * (h + 1) / heads) for h in range(heads)]


def _band(has_prev):
    qi = lax.broadcasted_iota(jnp.int32, (BLK, 2 * BLK), 0)
    ki = lax.broadcasted_iota(jnp.int32, (BLK, 2 * BLK), 1)
    j = qi - ki + BLK
    ok = (j >= 0) & (j <= BLK) & (has_prev | (ki >= BLK))
    return j.astype(F32), ok


def _to_branch(t, dil):
    s, d = t.shape
    return t.reshape(s // dil, dil * d)


def _from_branch(t, dil):
    l, dd = t.shape
    return t.reshape(l * dil, dd // dil)


def _attn_fwd(q, k, v, dil, jobs=()):
    l, d = q.shape[0], q.shape[1] // dil
    heads = d // HEAD
    scale = HEAD ** -0.5
    slopes = _slopes(heads)

    def body(q_ref, kc_ref, kp_ref, vc_ref, vp_ref, o_ref, lse_ref):
        jf, ok = _band(pl.program_id(1) > 0)
        for h in range(heads):
            sl = slice(HEAD * h, HEAD * (h + 1))
            kh = jnp.concatenate([kp_ref[:, sl], kc_ref[:, sl]], axis=0)
            vh = jnp.concatenate([vp_ref[:, sl], vc_ref[:, sl]], axis=0)
            logits = jnp.where(ok, _dot(q_ref[:, sl], kh, NT) * scale + jf * (-slopes[h] * dil), NEG)
            m = jnp.max(logits, axis=-1, keepdims=True)
            p = jnp.exp(logits - m)
            den = jnp.sum(p, axis=-1, keepdims=True)
            o_ref[:, sl] = _dot(p.astype(BF16), vh, NN) / den
            lse_ref[:, sl] = jnp.broadcast_to(m + jnp.log(den), (BLK, HEAD))

    cur = pl.BlockSpec((BLK, d), lambda r, b: (b, r))
    prev = pl.BlockSpec((BLK, d), lambda r, b: (jnp.maximum(b - 1, 0), r))
    return _pc(
        body, name=f"attn_fwd_d{dil}", grid=(dil, l // BLK),
        in_specs=[cur, cur, prev, cur, prev], out_specs=[cur, cur],
        out_shape=[SDS((l, dil * d), F32)] * 2, sem=("parallel", "arbitrary"), args=(q, k, k, v, v), jobs=jobs)


def _attn_merge(outs, lses):
    s, d = outs[0].shape
    nb = len(outs)

    def body(*refs):
        o_refs, l_refs = refs[:nb], refs[nb:2 * nb]
        att_ref, lse_ref = refs[2 * nb:]
        ls = [r[...] for r in l_refs]
        m = functools.reduce(jnp.maximum, ls)
        ws = [jnp.exp(v - m) for v in ls]
        den = functools.reduce(jnp.add, ws)
        acc = functools.reduce(jnp.add, [w * r[...] for w, r in zip(ws, o_refs)])
        att_ref[...] = (acc / den).astype(BF16)
        lse_ref[...] = m + jnp.log(den)

    row = pl.BlockSpec((ROWS, d), lambda i: (i, 0))
    return pl.pallas_call(
        body, name="attn_merge", grid=(s // ROWS,), in_specs=[row] * (2 * nb), out_specs=[row, row],
        out_shape=[SDS((s, d), BF16), SDS((s, d), F32)], compiler_params=_cp("parallel"))(*outs, *lses)


def _attn_bwd(q, k, v, do, o, lse, dil, jobs=()):
    l, d = q.shape[0], q.shape[1] // dil
    nb = l // BLK
    heads = d // HEAD
    scale = HEAD ** -0.5
    slopes = _slopes(heads)

    def body(q_ref, kc_ref, kp_ref, vc_ref, vp_ref, do_ref, o_ref, lse_ref, dq_ref, dk_ref, dv_ref, ck, cv):
        b = pl.program_id(1)

        @pl.when(b == 0)
        def _():
            ck[...] = jnp.zeros_like(ck)
            cv[...] = jnp.zeros_like(cv)

        @pl.when(b < nb)
        def _():
            jf, ok = _band(b > 0)
            for h in range(heads):
                sl = slice(HEAD * h, HEAD * (h + 1))
                qh, doh = q_ref[:, sl], do_ref[:, sl]
                kh = jnp.concatenate([kp_ref[:, sl], kc_ref[:, sl]], axis=0)
                vh = jnp.concatenate([vp_ref[:, sl], vc_ref[:, sl]], axis=0)
                lse_h = lse_ref[:, sl]
                lse2 = jnp.concatenate([lse_h, lse_h], axis=1)
                delta = jnp.sum(doh.astype(F32) * o_ref[:, sl].astype(F32), axis=-1, keepdims=True)
                p = jnp.where(ok, jnp.exp(_dot(qh, kh, NT) * scale + jf * (-slopes[h] * dil) - lse2), 0.0)
                ds = (p * (_dot(doh, vh, NT) - delta)).astype(BF16)
                dq_ref[:, sl] = _dot(ds, kh, NN) * scale
                dk2 = _dot(ds, qh, TN) * scale
                dv2 = _dot(p.astype(BF16), doh, TN)
                dk_ref[:, sl] = ck[:, sl] + dk2[:BLK]
                dv_ref[:, sl] = cv[:, sl] + dv2[:BLK]
                ck[:, sl] = dk2[BLK:]
                cv[:, sl] = dv2[BLK:]

        @pl.when(b == nb)
        def _():
            dk_ref[...] = ck[...]
            dv_ref[...] = cv[...]

    cur = pl.BlockSpec((BLK, d), lambda r, b: (jnp.minimum(b, nb - 1), r))
    prev = pl.BlockSpec((BLK, d), lambda r, b: (jnp.clip(b - 1, 0, nb - 1), r))
    return _pc(
        body, name=f"attn_bwd_d{dil}", grid=(dil, nb + 1),
        in_specs=[cur, cur, prev, cur, prev, cur, cur, cur], out_specs=[cur, prev, prev],
        out_shape=[SDS((l, dil * d), F32)] * 3,
        scratch=[pltpu.VMEM((BLK, d), F32), pltpu.VMEM((BLK, d), F32)],
        sem=("parallel", "arbitrary"), args=(q, k, k, v, v, do, o, lse), jobs=jobs)


def _sum_cast(xs, name):
    s, d = xs[0].shape

    def body(*refs):
        refs[-1][...] = functools.reduce(jnp.add, [r[...] for r in refs[:-1]]).astype(BF16)

    row = pl.BlockSpec((ROWS, d), lambda i: (i, 0))
    return pl.pallas_call(
        body, name=name, grid=(s // ROWS,), in_specs=[row] * len(xs), out_specs=row,
        out_shape=SDS((s, d), BF16), compiler_params=_cp("parallel"))(*xs)


def _pack_rows(vs, width):
    flat = jnp.concatenate([v.reshape(-1) for v in vs])
    spans, at = [], 0
    for v in vs:
        spans.append((at, v.size))
        at += v.size
    rows = -(-at // width)
    rows = -(-rows // 8) * 8
    flat = jnp.pad(flat, (0, rows * width - at))
    return flat.reshape(rows, width), spans


def kernel(x, a_norm_g, conv_w1, conv_b1, conv_dw, conv_dw_b, conv_ln_g, conv_ln_b, conv_w2, conv_b2, kv_norm_g, w_k, w_v, b_norm_g, w_q, w_o, ffn_norm_g, ffn_w_gate, ffn_w_up, ffn_w_down, final_norm_g, loss_target, m_a_norm_g, m_conv_w1, m_conv_b1, m_conv_dw, m_conv_dw_b, m_conv_ln_g, m_conv_ln_b, m_conv_w2, m_conv_b2, m_kv_norm_g, m_w_k, m_w_v, m_b_norm_g, m_w_q, m_w_o, m_ffn_norm_g, m_ffn_w_gate, m_ffn_w_up, m_ffn_w_down, m_final_norm_g, v_a_norm_g, v_conv_w1, v_conv_b1, v_conv_dw, v_conv_dw_b, v_conv_ln_g, v_conv_ln_b, v_conv_w2, v_conv_b2, v_kv_norm_g, v_w_k, v_w_v, v_b_norm_g, v_w_q, v_w_o, v_ffn_norm_g, v_ffn_w_gate, v_ffn_w_up, v_ffn_w_down, v_final_norm_g):
    s, d = x.shape[1], x.shape[2]
    dc = d // ND
    h0 = x[0]
    target = loss_target[0]
    xi, yi, ci = lax.axis_index("x"), lax.axis_index("y"), lax.axis_index("c")
    me = 4 * xi + 2 * yi + ci
    c_idx = jnp.reshape(ci, (1,)).astype(jnp.int32)
    q_idx = jnp.reshape(2 * xi + yi, (1,)).astype(jnp.int32)

    bf = lambda w: w.astype(BF16)
    small_shards = [a_norm_g, conv_b1, conv_dw, conv_dw_b, conv_ln_g, conv_ln_b, conv_b2]
    sp, sp_spans = _pack_rows(small_shards, dc)
    w1g, spg = _all_gather([bf(conv_w1[0]), sp], "gather_first")
    spg = spg.reshape(ND, -1)

    def small_full(i, rows):
        at, size = sp_spans[i]
        return spg[:, at:at + size].reshape(ND, rows, size // rows).transpose(1, 0, 2).reshape(rows, -1)

    a_g = small_full(0, 1)
    b1 = small_full(1, 1)
    dw = jnp.pad(small_full(2, CONV_W), ((0, CONV_PAD - CONV_W), (0, 0)))
    dwb, lng, lnb, b2 = small_full(3, 1), small_full(4, 1), small_full(5, 1), small_full(6, 1)
    kv_g, q_g, fin_g = kv_norm_g.reshape(1, d), b_norm_g.reshape(1, d), final_norm_g.reshape(1, d)
    f_g = [ffn_norm_g[0:1], ffn_norm_g[1:2]]

    def send(*shards):
        return _job_gather_send([bf(t) for t in shards])

    def forward(job):
        return _job_gather_forward(job.result)

    (n1,) = _rms_fwd(h0, [a_g], "rms_a")
    s_a = send(conv_w2[0], ffn_w_gate[0])
    ua, ug, glu = _glu_mm(n1, w1g, b1, jobs=[s_a])
    f_a, s_b = forward(s_a), send(ffn_w_up[0])
    cv, sw = _conv_fwd(glu, dw, dwb, lng, lnb, jobs=[f_a, s_b])
    w2g, wg0 = f_a.result
    f_b, s_c = forward(s_b), send(ffn_w_down[0])
    h1 = _mm_rows(sw, w2g, "w2_mm", res=h0, bias=b2, jobs=[f_b, s_c])
    (wu0,) = f_b.result
    (n2a,) = _rms_fwd(h1, [f_g[0]], "rms_f0")
    f_c, s_d = forward(s_c), send(w_k, w_v, w_q[0], w_o[0])
    gate0, up0, act0 = _swiglu_mm(n2a, wg0, wu0, "swiglu_mm0", jobs=[f_c, s_d])
    (wd0,) = f_c.result
    f_d, s_e = forward(s_d), send(ffn_w_gate[1])
    h2 = _down_mm(act0, wd0, h1, "down_mm0", jobs=[f_d, s_e])
    wkg, wvg, wqg, wog = f_d.result
    kvn, qn = _rms_fwd(h2, [kv_g, q_g], "rms_kvq")
    f_e = forward(s_e)
    kk = _mm_rows(kvn, wkg, "k_mm", out_dtype=BF16, jobs=[f_e])
    (wg1,) = f_e.result
    vv = _mm_rows(kvn, wvg, "v_mm", out_dtype=BF16)
    qq = _mm_rows(qn, wqg, "q_mm", out_dtype=BF16)
    branch = {dil: tuple(_to_branch(t, dil) for t in (qq, kk, vv)) for dil in BRANCH_DILATIONS}
    s_f = send(ffn_w_up[1])
    o1, l1 = _attn_fwd(*branch[1], 1, jobs=[s_f])
    f_f, s_g = forward(s_f), send(ffn_w_down[1])
    o4, l4 = _attn_fwd(*branch[4], 4, jobs=[f_f, s_g])
    (wu1,) = f_f.result
    f_h = forward(s_g)
    o16, l16 = _attn_fwd(*branch[16], 16, jobs=[f_h])
    (wd1,) = f_h.result
    outs = [_from_branch(o, dil) for o, dil in zip((o1, o4, o16), BRANCH_DILATIONS)]
    lses = [_from_branch(o, dil) for o, dil in zip((l1, l4, l16), BRANCH_DILATIONS)]
    att, lse = _attn_merge(outs, lses)
    h3 = _mm_rows(att, wog, "wo_mm", res=h2)
    (n2b,) = _rms_fwd(h3, [f_g[1]], "rms_f1")
    gate1, up1, act1 = _swiglu_mm(n2b, wg1, wu1, "swiglu_mm1")
    h4 = _down_mm(act1, wd1, h3, "down_mm1")

    flat = lambda g: g.reshape(ND, -1, g.shape[-1])
    chip_sums, cross = {}, {}

    def to_sibling(**grads):
        job = _job_scatter_sibling([flat(g) for g in grads.values()])
        job.names = list(grads)
        return job

    def add_up(job):
        for n, g, r in zip(job.names, job.ins, job.result):
            chip_sums[n] = _rs_add(g, r, c_idx, f"rs_add_{n}")

    def to_chips(*names):
        job = _job_scatter_cross([chip_sums[n] for n in names])
        job.names = names
        return job

    def landed(job):
        cross.update(zip(job.names, job.result))

    dh4, dh4b, d_fin, loss_row = _final_loss(h4, target, fin_g)
    dgate1, dup1 = _dact_mm(dh4b, wd1, gate1, up1, "dact_mm1")
    g_wd1 = _dwd_mm(act1, dh4b, "dwd_mm1")
    j1 = to_sibling(wd1=g_wd1)
    g_wg1, g_wu1 = _dwgu_mm(n2b, dgate1, dup1, "dwgu_mm1", jobs=[j1])
    add_up(j1)
    j2, j3 = to_chips("wd1"), to_sibling(wg1=g_wg1, wu1=g_wu1)
    dn2b = _dn_ffn_mm(dgate1, dup1, wg1, wu1, "dn_ffn_mm1", jobs=[j2, j3])
    landed(j2)
    add_up(j3)
    dh3, dh3b, d_f1 = _rms_bwd(h3, [(f_g[1], dn2b)], dh4, "rms_f1_bwd")
    g_wo = _dw_rows_mm(att, dh3b, "dwo_mm")
    j4 = to_sibling(wo=g_wo)
    datt = _mm_rows_t([(dh3b, wog)], "datt_mm", BF16, jobs=[j4])
    add_up(j4)
    riders = {1: to_chips("wg1"), 4: to_chips("wu1"), 16: to_chips("wo")}
    dqs, dks, dvs = [], [], []
    for dil in BRANCH_DILATIONS:
        qb, kb, vb = branch[dil]
        dq_b, dk_b, dv_b = _attn_bwd(qb, kb, vb, _to_branch(datt, dil), _to_branch(att, dil), _to_branch(lse, dil),
                                     dil, jobs=[riders[dil]])
        landed(riders[dil])
        dqs.append(_from_branch(dq_b, dil))
        dks.append(_from_branch(dk_b, dil))
        dvs.append(_from_branch(dv_b, dil))
    dq, dk, dv = _sum_cast(dqs, "dq_sum"), _sum_cast(dks, "dk_sum"), _sum_cast(dvs, "dv_sum")
    g_wq = _dw_rows_mm(qn, dq, "dwq_mm")
    g_wk = _dw_rows_mm(kvn, dk, "dwk_mm")
    g_wv = _dw_rows_mm(kvn, dv, "dwv_mm")
    j5 = to_sibling(wq=g_wq, wk=g_wk, wv=g_wv)
    dqn = _mm_rows_t([(dq, wqg)], "dqn_mm", F32, jobs=[j5])
    add_up(j5)
    j6 = to_chips("wq", "wk")
    dkvn = _mm_rows_t([(dk, wkg), (dv, wvg)], "dkvn_mm", F32, jobs=[j6])
    landed(j6)
    dh2, dh2b, d_q, d_kv = _rms_bwd(h2, [(q_g, dqn), (kv_g, dkvn)], dh3, "rms_kvq_bwd")
    j7 = to_chips("wv")
    dgate0, dup0 = _dact_mm(dh2b, wd0, gate0, up0, "dact_mm0", jobs=[j7])
    landed(j7)
    g_wd0 = _dwd_mm(act0, dh2b, "dwd_mm0")
    j8 = to_sibling(wd0=g_wd0)
    g_wg0, g_wu0 = _dwgu_mm(n2a, dgate0, dup0, "dwgu_mm0", jobs=[j8])
    add_up(j8)
    j9, j10 = to_chips("wd0"), to_sibling(wg0=g_wg0, wu0=g_wu0)
    dn2a = _dn_ffn_mm(dgate0, dup0, wg0, wu0, "dn_ffn_mm0", jobs=[j9, j10])
    landed(j9)
    add_up(j10)
    dh1, dh1b, d_f0, d_b2 = _rms_bwd(h1, [(f_g[0], dn2a)], dh2, "rms_f0_bwd", colsum=True)
    g_w2 = _dw_rows_mm(sw, dh1b, "dw2_mm")
    j11, j12 = to_chips("wg0"), to_sibling(w2=g_w2)
    dsw = _mm_rows_t([(dh1b, w2g)], "dsw_mm", F32, jobs=[j11, j12])
    landed(j11)
    add_up(j12)
    dcv, d_lng, d_lnb = _ln_bwd(dsw, cv, lng, lnb)
    j13 = to_chips("wu0", "w2")
    du, d_dw, d_dwb, d_b1 = _conv_bwd(dcv, glu, ua, ug, dw, jobs=[j13])
    landed(j13)
    g_w1 = _dw1_mm(n1, du)
    j14 = to_sibling(w1=g_w1)
    dn1 = _dn1_mm(du, w1g, jobs=[j14])
    add_up(j14)
    dx, _, d_a = _rms_bwd(h0, [(a_g, dn1)], dh1, "rms_a_bwd")

    small_g = [d_a, d_b1, d_dw[:CONV_W], d_dwb, d_lng, d_lnb, d_b2, d_kv, d_q, d_f0, d_f1, d_fin, loss_row]
    gp, gp_spans = _pack_rows(small_g, d)
    j15, j16 = to_chips("w1"), _job_gather_send([gp])
    _comm_call([j15, j16], "rs_w1_send_small")
    landed(j15)
    j17 = _job_gather_forward(j16.result)
    _comm_call([j17], "forward_small")
    (gpg,) = j17.result

    two = lambda t: t.reshape(-1, t.shape[-1])

    def adam(w, m, v, names, tag):
        res = None
        for part, n in enumerate(names):
            res = _adamw_big(two(w), two(m), two(v), chip_sums[n], cross[n], q_idx, f"adamw_{tag}{part}", part, res)
        return [t.reshape(w.shape) for t in res]

    big_out = [
        adam(conv_w1, m_conv_w1, v_conv_w1, ["w1"], "w1"), adam(conv_w2, m_conv_w2, v_conv_w2, ["w2"], "w2"),
        adam(w_k, m_w_k, v_w_k, ["wk"], "wk"), adam(w_v, m_w_v, v_w_v, ["wv"], "wv"),
        adam(w_q, m_w_q, v_w_q, ["wq"], "wq"), adam(w_o, m_w_o, v_w_o, ["wo"], "wo"),
        adam(ffn_w_gate, m_ffn_w_gate, v_ffn_w_gate, ["wg0", "wg1"], "wg"),
        adam(ffn_w_up, m_ffn_w_up, v_ffn_w_up, ["wu0", "wu1"], "wu"),
        adam(ffn_w_down, m_ffn_w_down, v_ffn_w_down, ["wd0", "wd1"], "wd")]

    gsum = _sum_devices(gpg, "sum_small_grads").reshape(-1)

    def gfull(i):
        at, size = gp_spans[i]
        return gsum[at:at + size]

    def shard_of(vec, rows):
        return lax.dynamic_slice_in_dim(vec.reshape(rows, -1), me * (vec.size // rows // ND), vec.size // rows // ND, axis=1)

    loss = gfull(12)[0]
    small_grads = [
        shard_of(gfull(0), 1), shard_of(gfull(1), 1), shard_of(gfull(2), CONV_W)[None], shard_of(gfull(3), 1),
        shard_of(gfull(4), 1), shard_of(gfull(5), 1), shard_of(gfull(6), 1),
        gfull(7), gfull(8)[None], jnp.stack([gfull(9), gfull(10)]), gfull(11)]
    small_w = [a_norm_g, conv_b1, conv_dw, conv_dw_b, conv_ln_g, conv_ln_b, conv_b2, kv_norm_g, b_norm_g, ffn_norm_g, final_norm_g]
    small_m = [m_a_norm_g, m_conv_b1, m_conv_dw, m_conv_dw_b, m_conv_ln_g, m_conv_ln_b, m_conv_b2, m_kv_norm_g, m_b_norm_g, m_ffn_norm_g, m_final_norm_g]
    small_v = [v_a_norm_g, v_conv_b1, v_conv_dw, v_conv_dw_b, v_conv_ln_g, v_conv_ln_b, v_conv_b2, v_kv_norm_g, v_b_norm_g, v_ffn_norm_g, v_final_norm_g]
    small_grads = [g.reshape(w.shape) for g, w in zip(small_grads, small_w)]
    wp, spans = _pack_rows(small_w, 128)
    gpk, _ = _pack_rows(small_grads, 128)
    mp, _ = _pack_rows(small_m, 128)
    vp, _ = _pack_rows(small_v, 128)
    dp, mnp, vnp = _adamw_small(wp, gpk, mp, vp, "adamw_small")

    def unpack(packed):
        flat = packed.reshape(-1)
        return [flat[at:at + size].reshape(w.shape) for (at, size), w in zip(spans, small_w)]

    small_out = list(zip(small_grads, unpack(dp), unpack(mnp), unpack(vnp)))

    order = ["a_norm_g", "conv_w1", "conv_b1", "conv_dw", "conv_dw_b", "conv_ln_g", "conv_ln_b", "conv_w2", "conv_b2",
             "kv_norm_g", "w_k", "w_v", "b_norm_g", "w_q", "w_o", "ffn_norm_g", "ffn_w_gate", "ffn_w_up", "ffn_w_down",
             "final_norm_g"]
    big_names = ["conv_w1", "conv_w2", "w_k", "w_v", "w_q", "w_o", "ffn_w_gate", "ffn_w_up", "ffn_w_down"]
    small_names = ["a_norm_g", "conv_b1", "conv_dw", "conv_dw_b", "conv_ln_g", "conv_ln_b", "conv_b2", "kv_norm_g",
                   "b_norm_g", "ffn_norm_g", "final_norm_g"]
    table = {n: big_out[i] for i, n in enumerate(big_names)}
    table.update({n: small_out[i] for i, n in enumerate(small_names)})
    result = [loss, dx[None]]
    for kind in range(4):
        result += [table[n][kind] for n in order]
    return tuple(result)
```

```python
import functools

import jax
import jax.numpy as jnp
from jax import lax
from jax.experimental import pallas as pl
from jax.experimental.pallas import tpu as pltpu

ND = 8
HEAD = 128
BLK = 128
BRANCH_DILATIONS = (1, 4, 16)
CONV_W = 31
CONV_PAD = 32
RMS_EPS = 1e-6
LN_EPS = 1e-5
LR, B1, B2, ADAM_EPS, WD, STEP = 0.001, 0.9, 0.999, 1e-08, 0.01, 10
VMEM_LIMIT = 56 * 1024 * 1024

F32, BF16 = jnp.float32, jnp.bfloat16
SDS = jax.ShapeDtypeStruct
MESH = pl.DeviceIdType.MESH
ANY = pl.BlockSpec(memory_space=pl.ANY)

NN = (((1,), (0,)), ((), ()))
NT = (((1,), (1,)), ((), ()))
TN = (((0,), (0,)), ((), ()))


def _dot(a, b, dims):
    return lax.dot_general(a, b, dims, preferred_element_type=F32)


def _cp(*sem):
    return pltpu.CompilerParams(dimension_semantics=sem, vmem_limit_bytes=VMEM_LIMIT)


def _slot(dev):
    return 4 * (dev % 2) + dev // 2


def _sigmoid(v):
    return 1.0 / (1.0 + jnp.exp(-v))


class _Job:
    def __init__(self, ins, out_shapes, alias, nsem, nlocal, make):
        self.ins, self.out_shapes, self.alias = list(ins), list(out_shapes), dict(alias)
        self.nsem, self.nlocal, self.make = nsem, nlocal, make
        self.result = None


def _coords():
    return lax.axis_index("x"), lax.axis_index("y"), lax.axis_index("c")


def _remote(src, dst, send, recv, k, to):
    return pltpu.make_async_remote_copy(src_ref=src, dst_ref=dst, send_sem=send.at[k], recv_sem=recv.at[k],
                                        device_id=to, device_id_type=MESH)


def _job_gather_send(shards):
    n = len(shards)

    def make(ins, outs, send, recv, local):
        x, y, c = _coords()
        targets = [(x, y, 1 - c), (1 - x, y, c), (x, 1 - y, c), (1 - x, 1 - y, c)]
        cps = []
        for a in range(n):
            dst = outs[a].at[4 * x + 2 * y + c]
            cps.append(pltpu.make_async_copy(ins[a], dst, local.at[a]))
            cps += [_remote(ins[a], dst, send, recv, 4 * a + k, t) for k, t in enumerate(targets)]
        return cps

    return _Job(shards, [SDS((ND,) + s.shape, s.dtype) for s in shards], {}, 4 * n, n, make)


def _job_gather_forward(gathered):
    n = len(gathered)

    def make(ins, outs, send, recv, local):
        x, y, c = _coords()
        cps = []
        for a in range(n):
            for k, (px, py) in enumerate([(1 - x, y), (x, 1 - y), (1 - x, 1 - y)]):
                blk = outs[a].at[4 * px + 2 * py + c]
                cps.append(_remote(blk, blk, send, recv, 3 * a + k, (x, y, 1 - c)))
        return cps

    return _Job(gathered, [SDS(g.shape, g.dtype) for g in gathered], {i: i for i in range(n)}, 3 * n, 0, make)


def _job_scatter_sibling(grads):
    n = len(grads)

    def make(ins, outs, send, recv, local):
        x, y, c = _coords()
        return [_remote(ins[a].at[pl.ds(4 * (1 - c), 4)], outs[a], send, recv, a, (x, y, 1 - c)) for a in range(n)]

    return _Job(grads, [SDS((4,) + g.shape[1:], g.dtype) for g in grads], {}, n, 0, make)


def _job_scatter_cross(sums):
    n = len(sums)

    def make(ins, outs, send, recv, local):
        x, y, c = _coords()
        chips = [(1 - x, y), (x, 1 - y), (1 - x, 1 - y)]
        return [_remote(ins[a].at[2 * px + py], outs[a].at[k], send, recv, 3 * a + k, (px, py, c))
                for a in range(n) for k, (px, py) in enumerate(chips)]

    return _Job(sums, [SDS((3,) + t.shape[1:], t.dtype) for t in sums], {}, 3 * n, 0, make)


def _pc(body, *, name, grid, in_specs, out_specs, out_shape, args, scratch=(), sem=(), alias=None, jobs=()):
    jobs = list(jobs)
    n_in, n_out, n_scr = len(in_specs), len(out_shape), len(scratch)
    aliases = dict(alias or {})
    job_args, job_shapes, job_scratch = [], [], []
    for j in jobs:
        for src, dst in j.alias.items():
            aliases[n_in + len(job_args) + src] = n_out + len(job_shapes) + dst
        job_args += j.ins
        job_shapes += j.out_shapes
        job_scratch += [pltpu.SemaphoreType.DMA((j.nsem,)), pltpu.SemaphoreType.DMA((j.nsem,)),
                        pltpu.SemaphoreType.DMA((max(j.nlocal, 1),))]

    def wrapped(*refs):
        ins = refs[:n_in]
        p = n_in + len(job_args)
        outs = refs[p:p + n_out]
        p += n_out + len(job_shapes)
        scr = refs[p:p + n_scr]
        sems = refs[p + n_scr:]
        copies = []
        pi, po = n_in, n_in + len(job_args) + n_out
        for k, j in enumerate(jobs):
            copies += j.make(refs[pi:pi + len(j.ins)], refs[po:po + len(j.out_shapes)], *sems[3 * k:3 * k + 3])
            pi += len(j.ins)
            po += len(j.out_shapes)
        gridded = bool(copies) and bool(grid)
        if gridded:
            ids = [pl.program_id(i) for i in range(len(grid))]
            first = functools.reduce(jnp.logical_and, [i == 0 for i in ids])
            last = functools.reduce(jnp.logical_and, [i == g - 1 for i, g in zip(ids, grid)])

            @pl.when(first)
            def _():
                for cp in copies:
                    cp.start()
        else:
            for cp in copies:
                cp.start()
        body(*ins, *outs, *scr)
        if gridded:
            @pl.when(last)
            def _():
                for cp in copies:
                    cp.wait()
        else:
            for cp in copies:
                cp.wait()

    kw = dict(grid=grid) if grid else {}
    semantics = ["arbitrary"] * len(grid) if jobs else list(sem)
    res = pl.pallas_call(
        wrapped, name=name, in_specs=list(in_specs) + [ANY] * len(job_args),
        out_specs=list(out_specs) + [ANY] * len(job_shapes), out_shape=list(out_shape) + job_shapes,
        scratch_shapes=list(scratch) + job_scratch, input_output_aliases=aliases,
        compiler_params=_cp(*semantics), **kw)(*args, *job_args)
    p = n_out
    for j in jobs:
        j.result = list(res[p:p + len(j.out_shapes)])
        p += len(j.out_shapes)
    return list(res[:n_out])


def _comm_call(jobs, name):
    _pc(lambda: None, name=name, grid=(), in_specs=[], out_specs=[], out_shape=[], args=[], jobs=jobs)


def _all_gather(arrs, name):
    n = len(arrs)

    def body(*refs):
        ins, outs = refs[:n], refs[n:2 * n]
        send_sems, recv_sems, local_sems = refs[2 * n:]
        x, y, c = lax.axis_index("x"), lax.axis_index("y"), lax.axis_index("c")
        me, sib = (x, y, c), (x, y, 1 - c)
        chips = [(1 - x, y), (x, 1 - y), (1 - x, 1 - y)]

        def copy(a, k, block, to, src=None):
            dst = outs[a].at[4 * block[0] + 2 * block[1] + block[2]]
            return pltpu.make_async_remote_copy(
                src_ref=dst if src is None else src, dst_ref=dst,
                send_sem=send_sems.at[7 * a + k], recv_sem=recv_sems.at[7 * a + k],
                device_id=to, device_id_type=MESH)

        mine = [pltpu.make_async_copy(ins[a], outs[a].at[4 * x + 2 * y + c], local_sems.at[a]) for a in range(n)]
        for cp in mine:
            cp.start()
        first = []
        for a in range(n):
            first.append(copy(a, 0, me, sib, src=ins[a]))
            first += [copy(a, 1 + j, me, (*chip, c), src=ins[a]) for j, chip in enumerate(chips)]
        for cp in first:
            cp.start()
        passed = []
        for a in range(n):
            for j, chip in enumerate(chips):
                copy(a, 1 + j, (*chip, c), me).wait_recv()
                fwd = copy(a, 4 + j, (*chip, c), sib)
                fwd.start()
                passed.append(fwd)
        for a in range(n):
            copy(a, 0, sib, me).wait_recv()
            for j, chip in enumerate(chips):
                copy(a, 4 + j, (*chip, 1 - c), me).wait_recv()
        for cp in first + passed:
            cp.wait_send()
        for cp in mine:
            cp.wait()

    return pl.pallas_call(
        body, name=name,
        out_shape=[SDS((ND,) + a.shape, a.dtype) for a in arrs],
        in_specs=[ANY] * n, out_specs=[ANY] * n,
        scratch_shapes=[pltpu.SemaphoreType.DMA((7 * n,)), pltpu.SemaphoreType.DMA((7 * n,)),
                        pltpu.SemaphoreType.DMA((n,))],
    )(*arrs)


def _row_tile(rows):
    return next(t for t in (256, 128, 64, 32, 16) if rows % t == 0)


def _rs_add(g, r1, c_idx, name):
    _, rows, cols = g.shape
    tr = _row_tile(rows)

    def body(c_ref, g_ref, r_ref, o_ref):
        o_ref[...] = (g_ref[...].astype(F32) + r_ref[...].astype(F32)).astype(o_ref.dtype)

    return pl.pallas_call(
        body, name=name,
        grid_spec=pltpu.PrefetchScalarGridSpec(
            num_scalar_prefetch=1, grid=(4, rows // tr),
            in_specs=[pl.BlockSpec((1, tr, cols), lambda q, i, c: (4 * c[0] + q, i, 0)),
                      pl.BlockSpec((1, tr, cols), lambda q, i, c: (q, i, 0))],
            out_specs=pl.BlockSpec((1, tr, cols), lambda q, i, c: (q, i, 0))),
        out_shape=SDS((4, rows, cols), g.dtype),
        compiler_params=_cp("parallel", "parallel"),
    )(c_idx, g, r1)


def _adam_math(w, g, m, v):
    m = B1 * m + (1.0 - B1) * g
    v = B2 * v + (1.0 - B2) * (g * g)
    m_hat = m / (1.0 - B1 ** STEP)
    v_hat = v / (1.0 - B2 ** STEP)
    delta = -LR * (m_hat / (jnp.sqrt(v_hat) + ADAM_EPS) + WD * w)
    return delta, m, v


def _adamw_big(w, m, v, t, r2, q_idx, name, part=0, prev=None):
    _, rows, cols = t.shape
    tr = _row_tile(rows)
    nblk = rows // tr

    def body(q_ref, w_ref, m_ref, v_ref, t_ref, r_ref, *outs):
        g_out, d_out, m_out, v_out = outs[-4:]
        g = t_ref[0].astype(F32)
        for k in range(3):
            g = g + r_ref[k].astype(F32)
        d, mn, vn = _adam_math(w_ref[...], g, m_ref[...], v_ref[...])
        g_out[...], d_out[...], m_out[...], v_out[...] = g, d, mn, vn

    blk = pl.BlockSpec((tr, cols), lambda i, q: (part * nblk + i, 0))
    specs = [blk, blk, blk, pl.BlockSpec((1, tr, cols), lambda i, q: (q[0], i, 0)),
             pl.BlockSpec((3, tr, cols), lambda i, q: (0, i, 0))]
    ins = [q_idx, w, m, v, t, r2]
    alias = {}
    if prev is not None:
        specs += [ANY] * 4
        alias = {6 + k: k for k in range(4)}
        ins += list(prev)
    return pl.pallas_call(
        body, name=name,
        grid_spec=pltpu.PrefetchScalarGridSpec(num_scalar_prefetch=1, grid=(nblk,), in_specs=specs, out_specs=[blk] * 4),
        out_shape=[SDS(w.shape, F32)] * 4, input_output_aliases=alias,
        compiler_params=_cp("parallel"))(*ins)


def _sum_devices(g, name):
    _, rows, cols = g.shape

    def body(g_ref, o_ref):
        acc = g_ref[0]
        for k in range(1, ND):
            acc = acc + g_ref[k]
        o_ref[...] = acc

    return pl.pallas_call(body, name=name, out_shape=SDS((rows, cols), F32))(g)


def _adamw_small(w, g, m, v, name):
    def body(w_ref, g_ref, m_ref, v_ref, d_out, m_out, v_out):
        d, mn, vn = _adam_math(w_ref[...], g_ref[...], m_ref[...], v_ref[...])
        d_out[...], m_out[...], v_out[...] = d, mn, vn

    return pl.pallas_call(body, name=name, out_shape=[SDS(w.shape, F32)] * 3)(w, g, m, v)


ROWS = 256


def _rms_stats(x):
    r = lax.rsqrt(jnp.mean(x * x, axis=-1, keepdims=True) + RMS_EPS)
    return x * r, r


def _rms_fwd(x, gains, name):
    s, d = x.shape
    n = len(gains)

    def body(x_ref, *refs):
        xh, _ = _rms_stats(x_ref[...])
        for g_ref, o_ref in zip(refs[:n], refs[n:]):
            o_ref[...] = (xh * g_ref[...]).astype(BF16)

    row = pl.BlockSpec((ROWS, d), lambda i: (i, 0))
    vec = pl.BlockSpec((1, d), lambda i: (0, 0))
    return pl.pallas_call(
        body, name=name, grid=(s // ROWS,), in_specs=[row] + [vec] * n, out_specs=[row] * n,
        out_shape=[SDS((s, d), BF16)] * n, compiler_params=_cp("parallel"))(x, *gains)


def _rms_bwd_rows(xh, r, gain, dy):
    u = dy * gain
    return r * (u - xh * jnp.mean(u * xh, axis=-1, keepdims=True))


def _rms_bwd(x, pairs, dres, name, colsum=False):
    s, d = x.shape
    n = len(pairs)

    def body(x_ref, dres_ref, *refs):
        g_refs, dy_refs = refs[:n], refs[n:2 * n]
        dx_ref, dxb_ref = refs[2 * n], refs[2 * n + 1]
        dg_refs = refs[2 * n + 2:2 * n + 2 + n]
        cs_ref = refs[-1] if colsum else None
        first = pl.program_id(0) == 0
        xh, r = _rms_stats(x_ref[...])
        dx = dres_ref[...]
        for g_ref, dy_ref, dg_ref in zip(g_refs, dy_refs, dg_refs):
            dy = dy_ref[...]
            dx = dx + _rms_bwd_rows(xh, r, g_ref[...], dy)

            @pl.when(first)
            def _():
                dg_ref[...] = jnp.zeros_like(dg_ref)
            dg_ref[...] += jnp.sum(dy * xh, axis=0, keepdims=True)
        dx_ref[...] = dx
        dxb_ref[...] = dx.astype(BF16)
        if colsum:
            @pl.when(first)
            def _():
                cs_ref[...] = jnp.zeros_like(cs_ref)
            cs_ref[...] += jnp.sum(dx, axis=0, keepdims=True)

    row = pl.BlockSpec((ROWS, d), lambda i: (i, 0))
    vec = pl.BlockSpec((1, d), lambda i: (0, 0))
    nvec = n + (1 if colsum else 0)
    outs = pl.pallas_call(
        body, name=name, grid=(s // ROWS,),
        in_specs=[row, row] + [vec] * n + [row] * n,
        out_specs=[row, row] + [vec] * nvec,
        out_shape=[SDS((s, d), F32), SDS((s, d), BF16)] + [SDS((1, d), F32)] * nvec,
        compiler_params=_cp("arbitrary"),
    )(x, dres, *[p[0] for p in pairs], *[p[1] for p in pairs])
    return outs


def _final_loss(h, target, gain):
    s, d = h.shape

    def body(h_ref, t_ref, g_ref, dh_ref, dhb_ref, dg_ref, loss_ref):
        first = pl.program_id(0) == 0
        xh, r = _rms_stats(h_ref[...])
        gain_v = g_ref[...]
        e = xh * gain_v - t_ref[...]
        dy = e * (1.0 / d)
        dx = _rms_bwd_rows(xh, r, gain_v, dy)
        dh_ref[...] = dx
        dhb_ref[...] = dx.astype(BF16)

        @pl.when(first)
        def _():
            dg_ref[...] = jnp.zeros_like(dg_ref)
            loss_ref[...] = jnp.zeros_like(loss_ref)
        dg_ref[...] += jnp.sum(dy * xh, axis=0, keepdims=True)
        loss_ref[...] += jnp.full((1, 128), 0.5 / d, F32) * jnp.sum(e * e)

    row = pl.BlockSpec((ROWS, d), lambda i: (i, 0))
    vec = pl.BlockSpec((1, d), lambda i: (0, 0))
    return pl.pallas_call(
        body, name="final_loss", grid=(s // ROWS,),
        in_specs=[row, row, vec], out_specs=[row, row, vec, pl.BlockSpec((1, 128), lambda i: (0, 0))],
        out_shape=[SDS((s, d), F32), SDS((s, d), BF16), SDS((1, d), F32), SDS((1, 128), F32)],
        compiler_params=_cp("arbitrary"))(h, target, gain)


CT = 128


def _ln_stats(cv):
    mu = jnp.mean(cv, axis=-1, keepdims=True)
    xc = cv - mu
    rstd = lax.rsqrt(jnp.mean(xc * xc, axis=-1, keepdims=True) + LN_EPS)
    return xc * rstd, rstd


def _conv_fwd(glu, dw, dwb, lng, lnb, jobs=()):
    s, d = glu.shape
    hb = CT // CONV_PAD

    def body(x_ref, halo_ref, dw_ref, dwb_ref, lng_ref, lnb_ref, c_ref, s_ref):
        keep = (pl.program_id(0) > 0).astype(F32)

        def chunk(ci, carry):
            ls = pl.ds(pl.multiple_of(ci * 128, 128), 128)
            xf = jnp.concatenate([halo_ref[:, ls] * keep, x_ref[:, ls]], axis=0)
            acc = jnp.zeros((CT, 128), F32)
            for k in range(CONV_W):
                sh = CONV_W - 1 - k
                xs = pltpu.roll(xf, sh, 0) if sh else xf
                acc = acc + dw_ref[pl.ds(k, 1), ls] * xs[CONV_PAD:]
            c_ref[:, ls] = acc + dwb_ref[:, ls]
            return carry

        lax.fori_loop(0, d // 128, chunk, 0)
        xh, _ = _ln_stats(c_ref[...])
        yv = xh * lng_ref[...] + lnb_ref[...]
        s_ref[...] = (yv * _sigmoid(yv)).astype(BF16)

    row = pl.BlockSpec((CT, d), lambda i: (i, 0))
    halo = pl.BlockSpec((CONV_PAD, d), lambda i: (jnp.maximum(i * hb - 1, 0), 0))
    vec = pl.BlockSpec((1, d), lambda i: (0, 0))
    taps = pl.BlockSpec((CONV_PAD, d), lambda i: (0, 0))
    return _pc(
        body, name="conv_fwd", grid=(s // CT,),
        in_specs=[row, halo, taps, vec, vec, vec], out_specs=[row, row],
        out_shape=[SDS((s, d), F32), SDS((s, d), BF16)], sem=("parallel",),
        args=(glu, glu, dw, dwb, lng, lnb), jobs=jobs)


def _ln_bwd(ds, cv, lng, lnb):
    s, d = cv.shape

    def body(ds_ref, c_ref, g_ref, b_ref, dc_ref, dg_ref, db_ref):
        first = pl.program_id(0) == 0
        xh, rstd = _ln_stats(c_ref[...])
        gv = g_ref[...]
        yv = xh * gv + b_ref[...]
        sg = _sigmoid(yv)
        dln = ds_ref[...] * (sg * (1.0 + yv * (1.0 - sg)))
        dxh = dln * gv
        dc_ref[...] = rstd * (dxh - jnp.mean(dxh, axis=-1, keepdims=True)
                              - xh * jnp.mean(dxh * xh, axis=-1, keepdims=True))

        @pl.when(first)
        def _():
            dg_ref[...] = jnp.zeros_like(dg_ref)
            db_ref[...] = jnp.zeros_like(db_ref)
        dg_ref[...] += jnp.sum(dln * xh, axis=0, keepdims=True)
        db_ref[...] += jnp.sum(dln, axis=0, keepdims=True)

    row = pl.BlockSpec((ROWS, d), lambda i: (i, 0))
    vec = pl.BlockSpec((1, d), lambda i: (0, 0))
    return pl.pallas_call(
        body, name="ln_bwd", grid=(s // ROWS,), in_specs=[row, row, vec, vec], out_specs=[row, vec, vec],
        out_shape=[SDS((s, d), F32), SDS((1, d), F32), SDS((1, d), F32)],
        compiler_params=_cp("arbitrary"))(ds, cv, lng, lnb)


def _conv_bwd(dc, glu, ua, ug, dw, jobs=()):
    s, d = dc.shape
    hb = CT // CONV_PAD
    nsteps = s // CT
    full = CT + CONV_PAD

    def body(dc_ref, dcn_ref, x_ref, xp_ref, ua_ref, ug_ref, dw_ref, du_ref, ddw_ref, ddwb_ref, db1_ref):
        i = pl.program_id(0)
        keep_prev = (i > 0).astype(F32)
        keep_next = (i < nsteps - 1).astype(F32)

        @pl.when(i == 0)
        def _():
            ddw_ref[...] = jnp.zeros_like(ddw_ref)
            ddwb_ref[...] = jnp.zeros_like(ddwb_ref)
            db1_ref[...] = jnp.zeros_like(db1_ref)

        def chunk(ci, carry):
            off = pl.multiple_of(ci * 128, 128)
            ls = pl.ds(off, 128)
            ls2 = pl.ds(pl.multiple_of(d + ci * 128, 128), 128)
            dcc = dc_ref[:, ls]
            dcf = jnp.concatenate([dcc, dcn_ref[:, ls] * keep_next], axis=0)
            xf = jnp.concatenate([xp_ref[:, ls] * keep_prev, x_ref[:, ls]], axis=0)
            dglu = jnp.zeros((CT, 128), F32)
            for k in range(CONV_W):
                sh = CONV_W - 1 - k
                dshift = pltpu.roll(dcf, full - sh, 0) if sh else dcf
                dglu = dglu + dw_ref[pl.ds(k, 1), ls] * dshift[:CT]
                xs = pltpu.roll(xf, sh, 0) if sh else xf
                ddw_ref[pl.ds(k, 1), ls] += jnp.sum(dcc * xs[CONV_PAD:], axis=0, keepdims=True)
            ddwb_ref[:, ls] += jnp.sum(dcc, axis=0, keepdims=True)
            av, gv = ua_ref[:, ls], ug_ref[:, ls]
            sg = _sigmoid(gv)
            da = dglu * sg
            dgt = dglu * av * sg * (1.0 - sg)
            du_ref[:, ls] = da.astype(BF16)
            du_ref[:, ls2] = dgt.astype(BF16)
            db1_ref[:, ls] += jnp.sum(da, axis=0, keepdims=True)
            db1_ref[:, ls2] += jnp.sum(dgt, axis=0, keepdims=True)
            return carry

        lax.fori_loop(0, d // 128, chunk, 0)

    row = pl.BlockSpec((CT, d), lambda i: (i, 0))
    prev = pl.BlockSpec((CONV_PAD, d), lambda i: (jnp.maximum(i * hb - 1, 0), 0))
    nxt = pl.BlockSpec((CONV_PAD, d), lambda i: (jnp.minimum((i + 1) * hb, s // CONV_PAD - 1), 0))
    taps = pl.BlockSpec((CONV_PAD, d), lambda i: (0, 0))
    return _pc(
        body, name="conv_bwd", grid=(nsteps,),
        in_specs=[row, nxt, row, prev, row, row, taps],
        out_specs=[pl.BlockSpec((CT, 2 * d), lambda i: (i, 0)), taps, pl.BlockSpec((1, d), lambda i: (0, 0)),
                   pl.BlockSpec((1, 2 * d), lambda i: (0, 0))],
        out_shape=[SDS((s, 2 * d), BF16), SDS((CONV_PAD, d), F32), SDS((1, d), F32), SDS((1, 2 * d), F32)],
        sem=("arbitrary",), args=(dc, dc, glu, glu, ua, ug, dw), jobs=jobs)


TM = 1024
TS = 1024


def _glu_mm(n1, w1g, b1, jobs=()):
    s, d = n1.shape
    cw = w1g.shape[2]
    half = ND // 2

    def body(a_ref, wa_ref, wg_ref, ba_ref, bg_ref, ua_ref, ug_ref, glu_ref):
        a = a_ref[...]
        ua = _dot(a, wa_ref[0], NN) + ba_ref[...]
        ug = _dot(a, wg_ref[0], NN) + bg_ref[...]
        ua_ref[...], ug_ref[...] = ua, ug
        glu_ref[...] = ua * _sigmoid(ug)

    out = pl.BlockSpec((TM, cw), lambda m, i: (m, i))
    return _pc(
        body, name="glu_mm", grid=(s // TM, half),
        in_specs=[pl.BlockSpec((TM, d), lambda m, i: (m, 0)),
                  pl.BlockSpec((1, d, cw), lambda m, i: (i, 0, 0)),
                  pl.BlockSpec((1, d, cw), lambda m, i: (i + half, 0, 0)),
                  pl.BlockSpec((1, cw), lambda m, i: (0, i)),
                  pl.BlockSpec((1, cw), lambda m, i: (0, i + half))],
        out_specs=[out, out, out], out_shape=[SDS((s, d), F32)] * 3,
        sem=("parallel", "arbitrary"), args=(n1, w1g, w1g, b1, b1), jobs=jobs)


def _mm_rows(a, wg, name, res=None, bias=None, out_dtype=F32, tn=512, branches=False, jobs=()):
    s, kdim = a.shape
    _, kc, n = wg.shape
    assert kc * ND == kdim
    nx = 2 + (res is not None) + (bias is not None)

    def body(*refs):
        acc = _dot(refs[0][...], refs[1][...].reshape(kdim, tn), NN)
        for extra in refs[2:nx]:
            acc = acc + extra[...]
        refs[nx][...] = acc.astype(out_dtype)
        if branches:
            scr = refs[-1]
            _stage(scr, acc)
            for o_ref, dil in zip(refs[nx + 1:], SPLIT_DILATIONS):
                _split_rows(scr, o_ref, dil)

    ins, specs = [a, wg], [pl.BlockSpec((TM, kdim), lambda m, j: (m, 0)), pl.BlockSpec((ND, kc, tn), lambda m, j: (0, 0, j))]
    if res is not None:
        ins.append(res)
        specs.append(pl.BlockSpec((TM, tn), lambda m, j: (m, j)))
    if bias is not None:
        ins.append(bias)
        specs.append(pl.BlockSpec((1, tn), lambda m, j: (0, j)))
    out_specs, out_shape, scratch = [pl.BlockSpec((TM, tn), lambda m, j: (m, j))], [SDS((s, n), out_dtype)], []
    if branches:
        out_specs += _branch_specs(TM, tn, lambda dil, m, j: (0, m, j))
        out_shape += [SDS((dil, s // dil, n), out_dtype) for dil in SPLIT_DILATIONS]
        scratch = [pltpu.VMEM((tn // 128, TM, 128), F32)]
    outs = _pc(body, name=name, grid=(s // TM, n // tn), in_specs=specs, out_specs=out_specs, out_shape=out_shape,
               scratch=scratch, sem=("parallel", "arbitrary"), args=ins, jobs=jobs)
    return [outs[0][None]] + outs[1:] if branches else outs[0]


def _swiglu_mm(n2, wgg, wug, name, jobs=()):
    s, d = n2.shape
    fc = wgg.shape[2]

    def body(a_ref, wg_ref, wu_ref, g_ref, u_ref, act_ref):
        a = a_ref[...]
        g = _dot(a, wg_ref[0], NN)
        u = _dot(a, wu_ref[0], NN)
        g_ref[0], u_ref[0] = g.astype(BF16), u.astype(BF16)
        act_ref[0] = (g * _sigmoid(g) * u).astype(BF16)

    wspec = pl.BlockSpec((1, d, fc), lambda m, j: (j, 0, 0))
    out = pl.BlockSpec((1, TM, fc), lambda m, j: (j, m, 0))
    return _pc(
        body, name=name, grid=(s // TM, ND),
        in_specs=[pl.BlockSpec((TM, d), lambda m, j: (m, 0)), wspec, wspec],
        out_specs=[out, out, out], out_shape=[SDS((ND, s, fc), BF16)] * 3,
        sem=("parallel", "arbitrary"), args=(n2, wgg, wug), jobs=jobs)


def _down_mm(act, wdg, res, name, jobs=()):
    _, s, fc = act.shape
    d = wdg.shape[2]

    def body(a_ref, w_ref, r_ref, o_ref):
        @pl.when(pl.program_id(1) == 0)
        def _():
            o_ref[...] = r_ref[...]
        o_ref[...] += _dot(a_ref[0], w_ref[0], NN)

    row = pl.BlockSpec((TM, d), lambda m, j: (m, 0))
    return _pc(
        body, name=name, grid=(s // TM, ND),
        in_specs=[pl.BlockSpec((1, TM, fc), lambda m, j: (j, m, 0)),
                  pl.BlockSpec((1, fc, d), lambda m, j: (j, 0, 0)), row],
        out_specs=[row], out_shape=[SDS((s, d), F32)],
        sem=("parallel", "arbitrary"), args=(act, wdg, res), jobs=jobs)[0]


def _dact_mm(dh, wdg, gate, up, name, jobs=()):
    s, d = dh.shape
    fc = wdg.shape[1]

    def body(a_ref, w_ref, g_ref, u_ref, dg_ref, du_ref):
        dact = _dot(a_ref[...], w_ref[0], NT)
        g, u = g_ref[0].astype(F32), u_ref[0].astype(F32)
        sg = _sigmoid(g)
        du_ref[0] = (dact * (g * sg)).astype(BF16)
        dg_ref[0] = (dact * u * (sg * (1.0 + g * (1.0 - sg)))).astype(BF16)

    blk = pl.BlockSpec((1, TM, fc), lambda m, j: (j, m, 0))
    return _pc(
        body, name=name, grid=(s // TM, ND),
        in_specs=[pl.BlockSpec((TM, d), lambda m, j: (m, 0)),
                  pl.BlockSpec((1, fc, d), lambda m, j: (j, 0, 0)), blk, blk],
        out_specs=[blk, blk], out_shape=[SDS((ND, s, fc), BF16)] * 2,
        sem=("parallel", "arbitrary"), args=(dh, wdg, gate, up), jobs=jobs)


def _dwd_mm(act, dh, name, jobs=()):
    _, s, fc = act.shape
    d = dh.shape[1]
    nk = s // TS

    def body(a_ref, b_ref, o_ref, acc):
        k = pl.program_id(1)

        @pl.when(k == 0)
        def _():
            acc[...] = jnp.zeros_like(acc)
        acc[...] += _dot(a_ref[0], b_ref[...], TN)

        @pl.when(k == nk - 1)
        def _():
            o_ref[0] = acc[...].astype(BF16)

    return _pc(
        body, name=name, grid=(ND, nk),
        in_specs=[pl.BlockSpec((1, TS, fc), lambda j, k: (j, k, 0)), pl.BlockSpec((TS, d), lambda j, k: (k, 0))],
        out_specs=[pl.BlockSpec((1, fc, d), lambda j, k: (_slot(j), 0, 0))],
        out_shape=[SDS((ND, fc, d), BF16)], scratch=[pltpu.VMEM((fc, d), F32)],
        sem=("parallel", "arbitrary"), args=(act, dh), jobs=jobs)[0]


def _dwgu_mm(n2, dgate, dup, name, jobs=()):
    s, d = n2.shape
    fc = dgate.shape[2]
    nk = s // TS

    def body(a_ref, g_ref, u_ref, og_ref, ou_ref, accg, accu):
        k = pl.program_id(1)

        @pl.when(k == 0)
        def _():
            accg[...] = jnp.zeros_like(accg)
            accu[...] = jnp.zeros_like(accu)
        a = a_ref[...]
        accg[...] += _dot(a, g_ref[0], TN)
        accu[...] += _dot(a, u_ref[0], TN)

        @pl.when(k == nk - 1)
        def _():
            og_ref[0] = accg[...].astype(BF16)
            ou_ref[0] = accu[...].astype(BF16)

    blk = pl.BlockSpec((1, TS, fc), lambda j, k: (j, k, 0))
    out = pl.BlockSpec((1, d, fc), lambda j, k: (_slot(j), 0, 0))
    return _pc(
        body, name=name, grid=(ND, nk),
        in_specs=[pl.BlockSpec((TS, d), lambda j, k: (k, 0)), blk, blk], out_specs=[out, out],
        out_shape=[SDS((ND, d, fc), BF16)] * 2,
        scratch=[pltpu.VMEM((d, fc), F32), pltpu.VMEM((d, fc), F32)],
        sem=("parallel", "arbitrary"), args=(n2, dgate, dup), jobs=jobs)


def _dn_ffn_mm(dgate, dup, wgg, wug, name, jobs=()):
    _, s, fc = dgate.shape
    d = wgg.shape[1]

    def body(g_ref, u_ref, wg_ref, wu_ref, o_ref):
        j = pl.program_id(1)

        @pl.when(j == 0)
        def _():
            o_ref[...] = jnp.zeros_like(o_ref)
        o_ref[...] += _dot(g_ref[0], wg_ref[0], NT) + _dot(u_ref[0], wu_ref[0], NT)

    blk = pl.BlockSpec((1, TM, fc), lambda m, j: (j, m, 0))
    wspec = pl.BlockSpec((1, d, fc), lambda m, j: (j, 0, 0))
    return _pc(
        body, name=name, grid=(s // TM, ND), in_specs=[blk, blk, wspec, wspec],
        out_specs=[pl.BlockSpec((TM, d), lambda m, j: (m, 0))], out_shape=[SDS((s, d), F32)],
        sem=("parallel", "arbitrary"), args=(dgate, dup, wgg, wug), jobs=jobs)[0]


def _mm_rows_t(pairs, name, out_dtype, branches=False, jobs=()):
    s, n = pairs[0][0].shape
    _, kc, _ = pairs[0][1].shape
    np_ = len(pairs)

    def body(*refs):
        acc = None
        for p in range(np_):
            t = _dot(refs[p][...], refs[np_ + p][0], NT)
            acc = t if acc is None else acc + t
        refs[2 * np_][...] = acc.astype(out_dtype)
        if branches:
            scr = refs[-1]
            _stage(scr, acc)
            for o_ref, dil in zip(refs[2 * np_ + 1:], SPLIT_DILATIONS):
                _split_rows(scr, o_ref, dil)

    out_specs, out_shape, scratch = [pl.BlockSpec((TM, kc), lambda m, j: (m, j))], [SDS((s, kc * ND), out_dtype)], []
    if branches:
        out_specs += _branch_specs(TM, kc, lambda dil, m, j: (0, m, j))
        out_shape += [SDS((dil, s // dil, kc * ND), out_dtype) for dil in SPLIT_DILATIONS]
        scratch = [pltpu.VMEM((kc // 128, TM, 128), F32)]
    outs = _pc(
        body, name=name, grid=(s // TM, ND),
        in_specs=[pl.BlockSpec((TM, n), lambda m, j: (m, 0))] * np_ + [pl.BlockSpec((1, kc, n), lambda m, j: (j, 0, 0))] * np_,
        out_specs=out_specs, out_shape=out_shape, scratch=scratch,
        sem=("parallel", "arbitrary"), args=[p[0] for p in pairs] + [p[1] for p in pairs], jobs=jobs)
    return [outs[0][None]] + outs[1:] if branches else outs[0]


def _dw_rows_mm(a, b, name):
    s, kdim = a.shape
    n = b.shape[1]
    kc = kdim // ND
    ts = TS // 2
    nk = s // ts

    def body(a_ref, b_ref, o_ref, acc):
        k = pl.program_id(0)

        @pl.when(k == 0)
        def _():
            acc[...] = jnp.zeros_like(acc)
        acc[...] += _dot(a_ref[...], b_ref[...], TN)

        @pl.when(k == nk - 1)
        def _():
            for dev in range(ND):
                o_ref[_slot(dev)] = acc[kc * dev:kc * (dev + 1), :].astype(BF16)

    return pl.pallas_call(
        body, name=name, grid=(nk,),
        in_specs=[pl.BlockSpec((ts, kdim), lambda k: (k, 0)), pl.BlockSpec((ts, n), lambda k: (k, 0))],
        out_specs=pl.BlockSpec((ND, kc, n), lambda k: (0, 0, 0)), out_shape=SDS((ND, kc, n), BF16),
        scratch_shapes=[pltpu.VMEM((kdim, n), F32)], compiler_params=_cp("arbitrary"))(a, b)


def _dw1_mm(n1, du):
    s, d = n1.shape
    cw = du.shape[1] // ND
    nk = s // TS

    def body(a_ref, b_ref, o_ref, acc):
        k = pl.program_id(1)

        @pl.when(k == 0)
        def _():
            acc[...] = jnp.zeros_like(acc)
        acc[...] += _dot(a_ref[...], b_ref[...], TN)

        @pl.when(k == nk - 1)
        def _():
            o_ref[0] = acc[...].astype(BF16)

    return pl.pallas_call(
        body, name="dw1_mm", grid=(ND, nk),
        in_specs=[pl.BlockSpec((TS, d), lambda j, k: (k, 0)), pl.BlockSpec((TS, cw), lambda j, k: (k, j))],
        out_specs=pl.BlockSpec((1, d, cw), lambda j, k: (_slot(j), 0, 0)), out_shape=SDS((ND, d, cw), BF16),
        scratch_shapes=[pltpu.VMEM((d, cw), F32)], compiler_params=_cp("parallel", "arbitrary"))(n1, du)


def _dn1_mm(du, w1g, jobs=()):
    s = du.shape[0]
    _, d, cw = w1g.shape

    def body(a_ref, w_ref, o_ref):
        j = pl.program_id(1)

        @pl.when(j == 0)
        def _():
            o_ref[...] = jnp.zeros_like(o_ref)
        o_ref[...] += _dot(a_ref[...], w_ref[0], NT)

    return _pc(
        body, name="dn1_mm", grid=(s // TM, ND),
        in_specs=[pl.BlockSpec((TM, cw), lambda m, j: (m, j)), pl.BlockSpec((1, d, cw), lambda m, j: (j, 0, 0))],
        out_specs=[pl.BlockSpec((TM, d), lambda m, j: (m, 0))], out_shape=[SDS((s, d), F32)],
        sem=("parallel", "arbitrary"), args=(du, w1g), jobs=jobs)[0]


NEG = -1e30


def _slopes(heads):
    return [2.0 ** (-8.0 * (h + 1) / heads) for h in range(heads)]


def _band(has_prev):
    qi = lax.broadcasted_iota(jnp.int32, (BLK, 2 * BLK), 0)
    ki = lax.broadcasted_iota(jnp.int32, (BLK, 2 * BLK), 1)
    j = qi - ki + BLK
    ok = (j >= 0) & (j <= BLK) & (has_prev | (ki >= BLK))
    return j.astype(F32), ok


SPLIT_DILATIONS = tuple(dil for dil in BRANCH_DILATIONS if dil > 1)


def _lane_chunks(w):
    return [slice(128 * c, 128 * (c + 1)) for c in range(w // 128)]


def _stage(scr, tile):
    for c, ls in enumerate(_lane_chunks(tile.shape[1])):
        scr[c] = tile[:, ls]


def _split_rows(scr, o_ref, dil):
    _, n, w = o_ref.shape
    for r in range(dil):
        for c, ls in enumerate(_lane_chunks(w)):
            o_ref[r, :, ls] = scr[c, pl.ds(r, n, stride=dil), :].astype(o_ref.dtype)


def _join_rows(i_ref, scr, dil):
    _, n, w = i_ref.shape
    for r in range(dil):
        for c, ls in enumerate(_lane_chunks(w)):
            scr[c, pl.ds(r, n, stride=dil), :] = i_ref[r, :, ls].astype(F32)


def _unstage(scr, w):
    return jnp.concatenate([scr[c] for c in range(w // 128)], axis=1)


def _branch_specs(rows, w, index):
    return [pl.BlockSpec((dil, rows // dil, w), functools.partial(index, dil)) for dil in SPLIT_DILATIONS]


def _attn_fwd(q, k, v, dil, jobs=()):
    _, l, d = q.shape
    heads = d // HEAD
    assert heads <= HEAD
    scale = HEAD ** -0.5
    slopes = _slopes(heads)

    def body(q_ref, kc_ref, kp_ref, vc_ref, vp_ref, o_ref, lse_ref):
        jf, ok = _band(pl.program_id(1) > 0)
        lane = lax.broadcasted_iota(jnp.int32, (BLK, HEAD), 1)
        lse = jnp.zeros((BLK, HEAD), F32)
        for h in range(heads):
            sl = slice(HEAD * h, HEAD * (h + 1))
            kh = jnp.concatenate([kp_ref[0, :, sl], kc_ref[0, :, sl]], axis=0)
            vh = jnp.concatenate([vp_ref[0, :, sl], vc_ref[0, :, sl]], axis=0)
            logits = jnp.where(ok, _dot(q_ref[0, :, sl], kh, NT) * scale + jf * (-slopes[h] * dil), NEG)
            m = jnp.max(logits, axis=-1, keepdims=True)
            p = jnp.exp(logits - m)
            den = jnp.sum(p, axis=-1, keepdims=True)
            o_ref[0, :, sl] = _dot(p.astype(BF16), vh, NN) / den
            lse = jnp.where(lane == h, m + jnp.log(den), lse)
        lse_ref[0] = lse

    cur = pl.BlockSpec((1, BLK, d), lambda r, b: (r, b, 0))
    prev = pl.BlockSpec((1, BLK, d), lambda r, b: (r, jnp.maximum(b - 1, 0), 0))
    return _pc(
        body, name=f"attn_fwd_d{dil}", grid=(dil, l // BLK),
        in_specs=[cur, cur, prev, cur, prev], out_specs=[cur, pl.BlockSpec((1, BLK, HEAD), lambda r, b: (r, b, 0))],
        out_shape=[SDS((dil, l, d), F32), SDS((dil, l, HEAD), F32)], sem=("parallel", "arbitrary"),
        args=(q, k, k, v, v), jobs=jobs)


def _attn_merge(outs, lses):
    _, s, d = outs[0].shape
    heads = d // HEAD
    nb = len(outs)
    nsplit = nb - 1

    def body(*refs):
        o_refs, l_refs = refs[:nb], refs[nb:2 * nb]
        att_refs, lse_refs = refs[2 * nb:3 * nb], refs[3 * nb:4 * nb]
        scr_o, scr_l, scr_att = refs[4 * nb:4 * nb + nsplit], refs[4 * nb + nsplit:4 * nb + 2 * nsplit], refs[-1]
        ls = [l_refs[0][...]]
        for k, dil in enumerate(SPLIT_DILATIONS):
            _join_rows(o_refs[1 + k], scr_o[k], dil)
            _join_rows(l_refs[1 + k], scr_l[k], dil)
            ls.append(scr_l[k][0])
        m = functools.reduce(jnp.maximum, ls)
        ws = [jnp.exp(v - m) for v in ls]
        den = functools.reduce(jnp.add, ws)
        ws = [w / den for w in ws]
        lse_refs[0][...] = m + jnp.log(den)
        scr_l[0][0] = m + jnp.log(den)
        for h in range(heads):
            sl = slice(HEAD * h, HEAD * (h + 1))
            slab = ws[0][:, h:h + 1] * o_refs[0][:, sl]
            for k in range(nsplit):
                slab = slab + ws[1 + k][:, h:h + 1] * scr_o[k][h]
            att_refs[0][:, sl] = slab.astype(BF16)
            scr_att[h] = slab
        for k, dil in enumerate(SPLIT_DILATIONS):
            _split_rows(scr_att, att_refs[1 + k], dil)
            _split_rows(scr_l[0], lse_refs[1 + k], dil)

    def specs(w):
        return [pl.BlockSpec((ROWS, w), lambda i: (i, 0))] + _branch_specs(ROWS, w, lambda dil, i: (0, i, 0))

    def shapes(w, dt):
        return [SDS((s, w), dt)] + [SDS((dil, s // dil, w), dt) for dil in SPLIT_DILATIONS]

    wide, narrow = pltpu.VMEM((heads, ROWS, 128), F32), pltpu.VMEM((1, ROWS, 128), F32)
    res = pl.pallas_call(
        body, name="attn_merge", grid=(s // ROWS,), in_specs=specs(d) + specs(HEAD), out_specs=specs(d) + specs(HEAD),
        out_shape=shapes(d, BF16) + shapes(HEAD, F32),
        scratch_shapes=[wide] * nsplit + [narrow] * nsplit + [wide],
        compiler_params=_cp("parallel"))(outs[0].reshape(s, d), *outs[1:], lses[0].reshape(s, HEAD), *lses[1:])
    return list(res[:nb]), list(res[nb:])


def _attn_bwd(q, k, v, do, o, lse, dil, jobs=()):
    _, l, d = q.shape
    nb = l // BLK
    heads = d // HEAD
    scale = HEAD ** -0.5
    slopes = _slopes(heads)

    def body(q_ref, kc_ref, kp_ref, vc_ref, vp_ref, do_ref, o_ref, lse_ref, dq_ref, dk_ref, dv_ref, ck, cv):
        b = pl.program_id(1)

        @pl.when(b == 0)
        def _():
            ck[...] = jnp.zeros_like(ck)
            cv[...] = jnp.zeros_like(cv)

        @pl.when(b < nb)
        def _():
            jf, ok = _band(b > 0)
            for h in range(heads):
                sl = slice(HEAD * h, HEAD * (h + 1))
                qh, doh = q_ref[0, :, sl], do_ref[0, :, sl]
                kh = jnp.concatenate([kp_ref[0, :, sl], kc_ref[0, :, sl]], axis=0)
                vh = jnp.concatenate([vp_ref[0, :, sl], vc_ref[0, :, sl]], axis=0)
                lse_h = lse_ref[0, :, h:h + 1]
                delta = jnp.sum(doh.astype(F32) * o_ref[0, :, sl].astype(F32), axis=-1, keepdims=True)
                p = jnp.where(ok, jnp.exp(_dot(qh, kh, NT) * scale + jf * (-slopes[h] * dil) - lse_h), 0.0)
                ds = (p * (_dot(doh, vh, NT) - delta)).astype(BF16)
                dq_ref[0, :, sl] = _dot(ds, kh, NN) * scale
                dk2 = _dot(ds, qh, TN) * scale
                dv2 = _dot(p.astype(BF16), doh, TN)
                dk_ref[0, :, sl] = ck[:, sl] + dk2[:BLK]
                dv_ref[0, :, sl] = cv[:, sl] + dv2[:BLK]
                ck[:, sl] = dk2[BLK:]
                cv[:, sl] = dv2[BLK:]

        @pl.when(b == nb)
        def _():
            dk_ref[0] = ck[...]
            dv_ref[0] = cv[...]

    cur = pl.BlockSpec((1, BLK, d), lambda r, b: (r, jnp.minimum(b, nb - 1), 0))
    prev = pl.BlockSpec((1, BLK, d), lambda r, b: (r, jnp.clip(b - 1, 0, nb - 1), 0))
    lse_spec = pl.BlockSpec((1, BLK, HEAD), lambda r, b: (r, jnp.minimum(b, nb - 1), 0))
    return _pc(
        body, name=f"attn_bwd_d{dil}", grid=(dil, nb + 1),
        in_specs=[cur, cur, prev, cur, prev, cur, cur, lse_spec], out_specs=[cur, prev, prev],
        out_shape=[SDS((dil, l, d), F32)] * 3,
        scratch=[pltpu.VMEM((BLK, d), F32), pltpu.VMEM((BLK, d), F32)],
        sem=("parallel", "arbitrary"), args=(q, k, k, v, v, do, o, lse), jobs=jobs)


def _sum_cast(xs, name):
    _, s, d = xs[0].shape
    nsplit = len(xs) - 1

    def body(*refs):
        i_refs, o_ref, scr = refs[:nsplit + 1], refs[nsplit + 1], refs[nsplit + 2:]
        acc = i_refs[0][...]
        for k, dil in enumerate(SPLIT_DILATIONS):
            _join_rows(i_refs[1 + k], scr[k], dil)
            acc = acc + _unstage(scr[k], d)
        o_ref[...] = acc.astype(BF16)

    row = pl.BlockSpec((ROWS, d), lambda i: (i, 0))
    return pl.pallas_call(
        body, name=name, grid=(s // ROWS,), in_specs=[row] + _branch_specs(ROWS, d, lambda dil, i: (0, i, 0)),
        out_specs=row, out_shape=SDS((s, d), BF16),
        scratch_shapes=[pltpu.VMEM((d // 128, ROWS, 128), F32)] * nsplit,
        compiler_params=_cp("parallel"))(xs[0].reshape(s, d), *xs[1:])


def _pack_rows(vs, width):
    flat = jnp.concatenate([v.reshape(-1) for v in vs])
    spans, at = [], 0
    for v in vs:
        spans.append((at, v.size))
        at += v.size
    rows = -(-at // width)
    rows = -(-rows // 8) * 8
    flat = jnp.pad(flat, (0, rows * width - at))
    return flat.reshape(rows, width), spans


def kernel(x, a_norm_g, conv_w1, conv_b1, conv_dw, conv_dw_b, conv_ln_g, conv_ln_b, conv_w2, conv_b2, kv_norm_g, w_k, w_v, b_norm_g, w_q, w_o, ffn_norm_g, ffn_w_gate, ffn_w_up, ffn_w_down, final_norm_g, loss_target, m_a_norm_g, m_conv_w1, m_conv_b1, m_conv_dw, m_conv_dw_b, m_conv_ln_g, m_conv_ln_b, m_conv_w2, m_conv_b2, m_kv_norm_g, m_w_k, m_w_v, m_b_norm_g, m_w_q, m_w_o, m_ffn_norm_g, m_ffn_w_gate, m_ffn_w_up, m_ffn_w_down, m_final_norm_g, v_a_norm_g, v_conv_w1, v_conv_b1, v_conv_dw, v_conv_dw_b, v_conv_ln_g, v_conv_ln_b, v_conv_w2, v_conv_b2, v_kv_norm_g, v_w_k, v_w_v, v_b_norm_g, v_w_q, v_w_o, v_ffn_norm_g, v_ffn_w_gate, v_ffn_w_up, v_ffn_w_down, v_final_norm_g):
    s, d = x.shape[1], x.shape[2]
    dc = d // ND
    h0 = x[0]
    target = loss_target[0]
    xi, yi, ci = lax.axis_index("x"), lax.axis_index("y"), lax.axis_index("c")
    me = 4 * xi + 2 * yi + ci
    c_idx = jnp.reshape(ci, (1,)).astype(jnp.int32)
    q_idx = jnp.reshape(2 * xi + yi, (1,)).astype(jnp.int32)

    bf = lambda w: w.astype(BF16)
    small_shards = [a_norm_g, conv_b1, conv_dw, conv_dw_b, conv_ln_g, conv_ln_b, conv_b2]
    sp, sp_spans = _pack_rows(small_shards, dc)
    w1g, spg = _all_gather([bf(conv_w1[0]), sp], "gather_first")
    spg = spg.reshape(ND, -1)

    def small_full(i, rows):
        at, size = sp_spans[i]
        return spg[:, at:at + size].reshape(ND, rows, size // rows).transpose(1, 0, 2).reshape(rows, -1)

    a_g = small_full(0, 1)
    b1 = small_full(1, 1)
    dw = jnp.pad(small_full(2, CONV_W), ((0, CONV_PAD - CONV_W), (0, 0)))
    dwb, lng, lnb, b2 = small_full(3, 1), small_full(4, 1), small_full(5, 1), small_full(6, 1)
    kv_g, q_g, fin_g = kv_norm_g.reshape(1, d), b_norm_g.reshape(1, d), final_norm_g.reshape(1, d)
    f_g = [ffn_norm_g[0:1], ffn_norm_g[1:2]]

    def send(*shards):
        return _job_gather_send([bf(t) for t in shards])

    def forward(job):
        return _job_gather_forward(job.result)

    (n1,) = _rms_fwd(h0, [a_g], "rms_a")
    s_a = send(conv_w2[0], ffn_w_gate[0])
    ua, ug, glu = _glu_mm(n1, w1g, b1, jobs=[s_a])
    f_a, s_b = forward(s_a), send(ffn_w_up[0])
    cv, sw = _conv_fwd(glu, dw, dwb, lng, lnb, jobs=[f_a, s_b])
    w2g, wg0 = f_a.result
    f_b, s_c = forward(s_b), send(ffn_w_down[0])
    h1 = _mm_rows(sw, w2g, "w2_mm", res=h0, bias=b2, jobs=[f_b, s_c])
    (wu0,) = f_b.result
    (n2a,) = _rms_fwd(h1, [f_g[0]], "rms_f0")
    f_c, s_d = forward(s_c), send(w_k, w_v, w_q[0], w_o[0])
    gate0, up0, act0 = _swiglu_mm(n2a, wg0, wu0, "swiglu_mm0", jobs=[f_c, s_d])
    (wd0,) = f_c.result
    f_d, s_e = forward(s_d), send(ffn_w_gate[1])
    h2 = _down_mm(act0, wd0, h1, "down_mm0", jobs=[f_d, s_e])
    wkg, wvg, wqg, wog = f_d.result
    kvn, qn = _rms_fwd(h2, [kv_g, q_g], "rms_kvq")
    f_e = forward(s_e)
    kk = _mm_rows(kvn, wkg, "k_mm", out_dtype=BF16, branches=True, jobs=[f_e])
    (wg1,) = f_e.result
    vv = _mm_rows(kvn, wvg, "v_mm", out_dtype=BF16, branches=True)
    qq = _mm_rows(qn, wqg, "q_mm", out_dtype=BF16, branches=True)
    branch = {dil: (qq[i], kk[i], vv[i]) for i, dil in enumerate(BRANCH_DILATIONS)}
    s_f = send(ffn_w_up[1])
    o1, l1 = _attn_fwd(*branch[1], 1, jobs=[s_f])
    f_f, s_g = forward(s_f), send(ffn_w_down[1])
    o4, l4 = _attn_fwd(*branch[4], 4, jobs=[f_f, s_g])
    (wu1,) = f_f.result
    f_h = forward(s_g)
    o16, l16 = _attn_fwd(*branch[16], 16, jobs=[f_h])
    (wd1,) = f_h.result
    atts, lses = _attn_merge([o1, o4, o16], [l1, l4, l16])
    att = atts[0]
    atts, lses = [att[None]] + atts[1:], [lses[0][None]] + lses[1:]
    h3 = _mm_rows(att, wog, "wo_mm", res=h2)
    (n2b,) = _rms_fwd(h3, [f_g[1]], "rms_f1")
    gate1, up1, act1 = _swiglu_mm(n2b, wg1, wu1, "swiglu_mm1")
    h4 = _down_mm(act1, wd1, h3, "down_mm1")

    flat = lambda g: g.reshape(ND, -1, g.shape[-1])
    chip_sums, cross = {}, {}

    def to_sibling(**grads):
        job = _job_scatter_sibling([flat(g) for g in grads.values()])
        job.names = list(grads)
        return job

    def add_up(job):
        for n, g, r in zip(job.names, job.ins, job.result):
            chip_sums[n] = _rs_add(g, r, c_idx, f"rs_add_{n}")

    def to_chips(*names):
        job = _job_scatter_cross([chip_sums[n] for n in names])
        job.names = names
        return job

    def landed(job):
        cross.update(zip(job.names, job.result))

    dh4, dh4b, d_fin, loss_row = _final_loss(h4, target, fin_g)
    dgate1, dup1 = _dact_mm(dh4b, wd1, gate1, up1, "dact_mm1")
    g_wd1 = _dwd_mm(act1, dh4b, "dwd_mm1")
    j1 = to_sibling(wd1=g_wd1)
    g_wg1, g_wu1 = _dwgu_mm(n2b, dgate1, dup1, "dwgu_mm1", jobs=[j1])
    add_up(j1)
    j2, j3 = to_chips("wd1"), to_sibling(wg1=g_wg1, wu1=g_wu1)
    dn2b = _dn_ffn_mm(dgate1, dup1, wg1, wu1, "dn_ffn_mm1", jobs=[j2, j3])
    landed(j2)
    add_up(j3)
    dh3, dh3b, d_f1 = _rms_bwd(h3, [(f_g[1], dn2b)], dh4, "rms_f1_bwd")
    g_wo = _dw_rows_mm(att, dh3b, "dwo_mm")
    j4 = to_sibling(wo=g_wo)
    datt = _mm_rows_t([(dh3b, wog)], "datt_mm", BF16, branches=True, jobs=[j4])
    add_up(j4)
    riders = {1: to_chips("wg1"), 4: to_chips("wu1"), 16: to_chips("wo")}
    dqs, dks, dvs = [], [], []
    for i, dil in enumerate(BRANCH_DILATIONS):
        qb, kb, vb = branch[dil]
        dq_b, dk_b, dv_b = _attn_bwd(qb, kb, vb, datt[i], atts[i], lses[i], dil, jobs=[riders[dil]])
        landed(riders[dil])
        dqs.append(dq_b)
        dks.append(dk_b)
        dvs.append(dv_b)
    dq, dk, dv = _sum_cast(dqs, "dq_sum"), _sum_cast(dks, "dk_sum"), _sum_cast(dvs, "dv_sum")
    g_wq = _dw_rows_mm(qn, dq, "dwq_mm")
    g_wk = _dw_rows_mm(kvn, dk, "dwk_mm")
    g_wv = _dw_rows_mm(kvn, dv, "dwv_mm")
    j5 = to_sibling(wq=g_wq, wk=g_wk, wv=g_wv)
    dqn = _mm_rows_t([(dq, wqg)], "dqn_mm", F32, jobs=[j5])
    add_up(j5)
    j6 = to_chips("wq", "wk")
    dkvn = _mm_rows_t([(dk, wkg), (dv, wvg)], "dkvn_mm", F32, jobs=[j6])
    landed(j6)
    dh2, dh2b, d_q, d_kv = _rms_bwd(h2, [(q_g, dqn), (kv_g, dkvn)], dh3, "rms_kvq_bwd")
    j7 = to_chips("wv")
    dgate0, dup0 = _dact_mm(dh2b, wd0, gate0, up0, "dact_mm0", jobs=[j7])
    landed(j7)
    g_wd0 = _dwd_mm(act0, dh2b, "dwd_mm0")
    j8 = to_sibling(wd0=g_wd0)
    g_wg0, g_wu0 = _dwgu_mm(n2a, dgate0, dup0, "dwgu_mm0", jobs=[j8])
    add_up(j8)
    j9, j10 = to_chips("wd0"), to_sibling(wg0=g_wg0, wu0=g_wu0)
    dn2a = _dn_ffn_mm(dgate0, dup0, wg0, wu0, "dn_ffn_mm0", jobs=[j9, j10])
    landed(j9)
    add_up(j10)
    dh1, dh1b, d_f0, d_b2 = _rms_bwd(h1, [(f_g[0], dn2a)], dh2, "rms_f0_bwd", colsum=True)
    g_w2 = _dw_rows_mm(sw, dh1b, "dw2_mm")
    j11, j12 = to_chips("wg0"), to_sibling(w2=g_w2)
    dsw = _mm_rows_t([(dh1b, w2g)], "dsw_mm", F32, jobs=[j11, j12])
    landed(j11)
    add_up(j12)
    dcv, d_lng, d_lnb = _ln_bwd(dsw, cv, lng, lnb)
    j13 = to_chips("wu0", "w2")
    du, d_dw, d_dwb, d_b1 = _conv_bwd(dcv, glu, ua, ug, dw, jobs=[j13])
    landed(j13)
    g_w1 = _dw1_mm(n1, du)
    j14 = to_sibling(w1=g_w1)
    dn1 = _dn1_mm(du, w1g, jobs=[j14])
    add_up(j14)
    dx, _, d_a = _rms_bwd(h0, [(a_g, dn1)], dh1, "rms_a_bwd")

    small_g = [d_a, d_b1, d_dw[:CONV_W], d_dwb, d_lng, d_lnb, d_b2, d_kv, d_q, d_f0, d_f1, d_fin, loss_row]
    gp, gp_spans = _pack_rows(small_g, d)
    j15, j16 = to_chips("w1"), _job_gather_send([gp])
    _comm_call([j15, j16], "rs_w1_send_small")
    landed(j15)
    j17 = _job_gather_forward(j16.result)
    _comm_call([j17], "forward_small")
    (gpg,) = j17.result

    two = lambda t: t.reshape(-1, t.shape[-1])

    def adam(w, m, v, names, tag):
        res = None
        for part, n in enumerate(names):
            res = _adamw_big(two(w), two(m), two(v), chip_sums[n], cross[n], q_idx, f"adamw_{tag}{part}", part, res)
        return [t.reshape(w.shape) for t in res]

    big_out = [
        adam(conv_w1, m_conv_w1, v_conv_w1, ["w1"], "w1"), adam(conv_w2, m_conv_w2, v_conv_w2, ["w2"], "w2"),
        adam(w_k, m_w_k, v_w_k, ["wk"], "wk"), adam(w_v, m_w_v, v_w_v, ["wv"], "wv"),
        adam(w_q, m_w_q, v_w_q, ["wq"], "wq"), adam(w_o, m_w_o, v_w_o, ["wo"], "wo"),
        adam(ffn_w_gate, m_ffn_w_gate, v_ffn_w_gate, ["wg0", "wg1"], "wg"),
        adam(ffn_w_up, m_ffn_w_up, v_ffn_w_up, ["wu0", "wu1"], "wu"),
        adam(ffn_w_down, m_ffn_w_down, v_ffn_w_down, ["wd0", "wd1"], "wd")]

    gsum = _sum_devices(gpg, "sum_small_grads").reshape(-1)

    def gfull(i):
        at, size = gp_spans[i]
        return gsum[at:at + size]

    def shard_of(vec, rows):
        return lax.dynamic_slice_in_dim(vec.reshape(rows, -1), me * (vec.size // rows // ND), vec.size // rows // ND, axis=1)

    loss = gfull(12)[0]
    small_grads = [
        shard_of(gfull(0), 1), shard_of(gfull(1), 1), shard_of(gfull(2), CONV_W)[None], shard_of(gfull(3), 1),
        shard_of(gfull(4), 1), shard_of(gfull(5), 1), shard_of(gfull(6), 1),
        gfull(7), gfull(8)[None], jnp.stack([gfull(9), gfull(10)]), gfull(11)]
    small_w = [a_norm_g, conv_b1, conv_dw, conv_dw_b, conv_ln_g, conv_ln_b, conv_b2, kv_norm_g, b_norm_g, ffn_norm_g, final_norm_g]
    small_m = [m_a_norm_g, m_conv_b1, m_conv_dw, m_conv_dw_b, m_conv_ln_g, m_conv_ln_b, m_conv_b2, m_kv_norm_g, m_b_norm_g, m_ffn_norm_g, m_final_norm_g]
    small_v = [v_a_norm_g, v_conv_b1, v_conv_dw, v_conv_dw_b, v_conv_ln_g, v_conv_ln_b, v_conv_b2, v_kv_norm_g, v_b_norm_g, v_ffn_norm_g, v_final_norm_g]
    small_grads = [g.reshape(w.shape) for g, w in zip(small_grads, small_w)]
    wp, spans = _pack_rows(small_w, 128)
    gpk, _ = _pack_rows(small_grads, 128)
    mp, _ = _pack_rows(small_m, 128)
    vp, _ = _pack_rows(small_v, 128)
    dp, mnp, vnp = _adamw_small(wp, gpk, mp, vp, "adamw_small")

    def unpack(packed):
        flat = packed.reshape(-1)
        return [flat[at:at + size].reshape(w.shape) for (at, size), w in zip(spans, small_w)]

    small_out = list(zip(small_grads, unpack(dp), unpack(mnp), unpack(vnp)))

    order = ["a_norm_g", "conv_w1", "conv_b1", "conv_dw", "conv_dw_b", "conv_ln_g", "conv_ln_b", "conv_w2", "conv_b2",
             "kv_norm_g", "w_k", "w_v", "b_norm_g", "w_q", "w_o", "ffn_norm_g", "ffn_w_gate", "ffn_w_up", "ffn_w_down",
             "final_norm_g"]
    big_names = ["conv_w1", "conv_w2", "w_k", "w_v", "w_q", "w_o", "ffn_w_gate", "ffn_w_up", "ffn_w_down"]
    small_names = ["a_norm_g", "conv_b1", "conv_dw", "conv_dw_b", "conv_ln_g", "conv_ln_b", "conv_b2", "kv_norm_g",
                   "b_norm_g", "ffn_norm_g", "final_norm_g"]
    table = {n: big_out[i] for i, n in enumerate(big_names)}
    table.update({n: small_out[i] for i, n in enumerate(small_names)})
    result = [loss, dx[None]]
    for kind in range(4):
        result += [table[n][kind] for n in order]
    return tuple(result)
```

```python
import functools

import jax
import jax.numpy as jnp
from jax import lax
from jax.experimental import pallas as pl
from jax.experimental.pallas import tpu as pltpu

ND = 8
HEAD = 128
BLK = 128
BRANCH_DILATIONS = (1, 4, 16)
CONV_W = 31
CONV_PAD = 32
RMS_EPS = 1e-6
LN_EPS = 1e-5
LR, B1, B2, ADAM_EPS, WD, STEP = 0.001, 0.9, 0.999, 1e-08, 0.01, 10
VMEM_LIMIT = 56 * 1024 * 1024

F32, BF16 = jnp.float32, jnp.bfloat16
SDS = jax.ShapeDtypeStruct
MESH = pl.DeviceIdType.MESH
ANY = pl.BlockSpec(memory_space=pl.ANY)

NN = (((1,), (0,)), ((), ()))
NT = (((1,), (1,)), ((), ()))
TN = (((0,), (0,)), ((), ()))


def _dot(a, b, dims):
    return lax.dot_general(a, b, dims, preferred_element_type=F32)


def _cp(*sem):
    return pltpu.CompilerParams(dimension_semantics=sem, vmem_limit_bytes=VMEM_LIMIT)


def _slot(dev):
    return 4 * (dev % 2) + dev // 2


def _sigmoid(v):
    return 1.0 / (1.0 + jnp.exp(-v))


class _Job:
    def __init__(self, ins, out_shapes, alias, nsem, nlocal, make):
        self.ins, self.out_shapes, self.alias = list(ins), list(out_shapes), dict(alias)
        self.nsem, self.nlocal, self.make = nsem, nlocal, make
        self.result = None


def _coords():
    return lax.axis_index("x"), lax.axis_index("y"), lax.axis_index("c")


def _remote(src, dst, send, recv, k, to):
    return pltpu.make_async_remote_copy(src_ref=src, dst_ref=dst, send_sem=send.at[k], recv_sem=recv.at[k],
                                        device_id=to, device_id_type=MESH)


def _job_gather_send(shards):
    n = len(shards)

    def make(ins, outs, send, recv, local):
        x, y, c = _coords()
        targets = [(x, y, 1 - c), (1 - x, y, c), (x, 1 - y, c), (1 - x, 1 - y, c)]
        cps = []
        for a in range(n):
            dst = outs[a].at[4 * x + 2 * y + c]
            cps.append(pltpu.make_async_copy(ins[a], dst, local.at[a]))
            cps += [_remote(ins[a], dst, send, recv, 4 * a + k, t) for k, t in enumerate(targets)]
        return cps

    return _Job(shards, [SDS((ND,) + s.shape, s.dtype) for s in shards], {}, 4 * n, n, make)


def _job_gather_forward(gathered):
    n = len(gathered)

    def make(ins, outs, send, recv, local):
        x, y, c = _coords()
        cps = []
        for a in range(n):
            for k, (px, py) in enumerate([(1 - x, y), (x, 1 - y), (1 - x, 1 - y)]):
                blk = outs[a].at[4 * px + 2 * py + c]
                cps.append(_remote(blk, blk, send, recv, 3 * a + k, (x, y, 1 - c)))
        return cps

    return _Job(gathered, [SDS(g.shape, g.dtype) for g in gathered], {i: i for i in range(n)}, 3 * n, 0, make)


def _job_scatter_sibling(grads):
    n = len(grads)

    def make(ins, outs, send, recv, local):
        x, y, c = _coords()
        return [_remote(ins[a].at[pl.ds(4 * (1 - c), 4)], outs[a], send, recv, a, (x, y, 1 - c)) for a in range(n)]

    return _Job(grads, [SDS((4,) + g.shape[1:], g.dtype) for g in grads], {}, n, 0, make)


def _job_scatter_cross(sums):
    n = len(sums)

    def make(ins, outs, send, recv, local):
        x, y, c = _coords()
        chips = [(1 - x, y), (x, 1 - y), (1 - x, 1 - y)]
        return [_remote(ins[a].at[2 * px + py], outs[a].at[k], send, recv, 3 * a + k, (px, py, c))
                for a in range(n) for k, (px, py) in enumerate(chips)]

    return _Job(sums, [SDS((3,) + t.shape[1:], t.dtype) for t in sums], {}, 3 * n, 0, make)


def _pc(body, *, name, grid, in_specs, out_specs, out_shape, args, scratch=(), sem=(), alias=None, jobs=()):
    jobs = list(jobs)
    n_in, n_out, n_scr = len(in_specs), len(out_shape), len(scratch)
    aliases = dict(alias or {})
    job_args, job_shapes, job_scratch = [], [], []
    for j in jobs:
        for src, dst in j.alias.items():
            aliases[n_in + len(job_args) + src] = n_out + len(job_shapes) + dst
        job_args += j.ins
        job_shapes += j.out_shapes
        job_scratch += [pltpu.SemaphoreType.DMA((j.nsem,)), pltpu.SemaphoreType.DMA((j.nsem,)),
                        pltpu.SemaphoreType.DMA((max(j.nlocal, 1),))]

    def wrapped(*refs):
        ins = refs[:n_in]
        p = n_in + len(job_args)
        outs = refs[p:p + n_out]
        p += n_out + len(job_shapes)
        scr = refs[p:p + n_scr]
        sems = refs[p + n_scr:]
        copies = []
        pi, po = n_in, n_in + len(job_args) + n_out
        for k, j in enumerate(jobs):
            copies += j.make(refs[pi:pi + len(j.ins)], refs[po:po + len(j.out_shapes)], *sems[3 * k:3 * k + 3])
            pi += len(j.ins)
            po += len(j.out_shapes)
        gridded = bool(copies) and bool(grid)
        if gridded:
            ids = [pl.program_id(i) for i in range(len(grid))]
            first = functools.reduce(jnp.logical_and, [i == 0 for i in ids])
            last = functools.reduce(jnp.logical_and, [i == g - 1 for i, g in zip(ids, grid)])

            @pl.when(first)
            def _():
                for cp in copies:
                    cp.start()
        else:
            for cp in copies:
                cp.start()
        body(*ins, *outs, *scr)
        if gridded:
            @pl.when(last)
            def _():
                for cp in copies:
                    cp.wait()
        else:
            for cp in copies:
                cp.wait()

    kw = dict(grid=grid) if grid else {}
    semantics = ["arbitrary"] * len(grid) if jobs else list(sem)
    res = pl.pallas_call(
        wrapped, name=name, in_specs=list(in_specs) + [ANY] * len(job_args),
        out_specs=list(out_specs) + [ANY] * len(job_shapes), out_shape=list(out_shape) + job_shapes,
        scratch_shapes=list(scratch) + job_scratch, input_output_aliases=aliases,
        compiler_params=_cp(*semantics), **kw)(*args, *job_args)
    p = n_out
    for j in jobs:
        j.result = list(res[p:p + len(j.out_shapes)])
        p += len(j.out_shapes)
    return list(res[:n_out])


def _comm_call(jobs, name):
    _pc(lambda: None, name=name, grid=(), in_specs=[], out_specs=[], out_shape=[], args=[], jobs=jobs)


def _all_gather(arrs, name):
    n = len(arrs)

    def body(*refs):
        ins, outs = refs[:n], refs[n:2 * n]
        send_sems, recv_sems, local_sems = refs[2 * n:]
        x, y, c = lax.axis_index("x"), lax.axis_index("y"), lax.axis_index("c")
        me, sib = (x, y, c), (x, y, 1 - c)
        chips = [(1 - x, y), (x, 1 - y), (1 - x, 1 - y)]

        def copy(a, k, block, to, src=None):
            dst = outs[a].at[4 * block[0] + 2 * block[1] + block[2]]
            return pltpu.make_async_remote_copy(
                src_ref=dst if src is None else src, dst_ref=dst,
                send_sem=send_sems.at[7 * a + k], recv_sem=recv_sems.at[7 * a + k],
                device_id=to, device_id_type=MESH)

        mine = [pltpu.make_async_copy(ins[a], outs[a].at[4 * x + 2 * y + c], local_sems.at[a]) for a in range(n)]
        for cp in mine:
            cp.start()
        first = []
        for a in range(n):
            first.append(copy(a, 0, me, sib, src=ins[a]))
            first += [copy(a, 1 + j, me, (*chip, c), src=ins[a]) for j, chip in enumerate(chips)]
        for cp in first:
            cp.start()
        passed = []
        for a in range(n):
            for j, chip in enumerate(chips):
                copy(a, 1 + j, (*chip, c), me).wait_recv()
                fwd = copy(a, 4 + j, (*chip, c), sib)
                fwd.start()
                passed.append(fwd)
        for a in range(n):
            copy(a, 0, sib, me).wait_recv()
            for j, chip in enumerate(chips):
                copy(a, 4 + j, (*chip, 1 - c), me).wait_recv()
        for cp in first + passed:
            cp.wait_send()
        for cp in mine:
            cp.wait()

    return pl.pallas_call(
        body, name=name,
        out_shape=[SDS((ND,) + a.shape, a.dtype) for a in arrs],
        in_specs=[ANY] * n, out_specs=[ANY] * n,
        scratch_shapes=[pltpu.SemaphoreType.DMA((7 * n,)), pltpu.SemaphoreType.DMA((7 * n,)),
                        pltpu.SemaphoreType.DMA((n,))],
    )(*arrs)


ELEMENTWISE_TILE_BYTES = 3 * 512 * 1024


def _row_tile(rows, cols):
    fits = [t for t in range(16, rows + 1, 16) if rows % t == 0 and 4 * t * cols <= ELEMENTWISE_TILE_BYTES]
    return max(fits)


def _rs_add(g, r1, c_idx, name):
    _, rows, cols = g.shape

    def body(c_ref, g_ref, r_ref, o_ref):
        o_ref[...] = (g_ref[...].astype(F32) + r_ref[...].astype(F32)).astype(o_ref.dtype)

    return pl.pallas_call(
        body, name=name,
        grid_spec=pltpu.PrefetchScalarGridSpec(
            num_scalar_prefetch=1, grid=(4,),
            in_specs=[pl.BlockSpec((1, rows, cols), lambda q, c: (4 * c[0] + q, 0, 0)),
                      pl.BlockSpec((1, rows, cols), lambda q, c: (q, 0, 0))],
            out_specs=pl.BlockSpec((1, rows, cols), lambda q, c: (q, 0, 0))),
        out_shape=SDS((4, rows, cols), g.dtype),
        compiler_params=_cp("parallel"),
    )(c_idx, g, r1)


def _adam_math(w, g, m, v):
    m = B1 * m + (1.0 - B1) * g
    v = B2 * v + (1.0 - B2) * (g * g)
    m_hat = m / (1.0 - B1 ** STEP)
    v_hat = v / (1.0 - B2 ** STEP)
    delta = -LR * (m_hat / (jnp.sqrt(v_hat) + ADAM_EPS) + WD * w)
    return delta, m, v


def _adamw_big(w, m, v, t, r2, q_idx, name, part=0, prev=None):
    _, rows, cols = t.shape
    tr = _row_tile(rows, cols)
    nblk = rows // tr

    def body(q_ref, w_ref, m_ref, v_ref, t_ref, r_ref, *outs):
        g_out, d_out, m_out, v_out = outs[-4:]
        g = t_ref[0].astype(F32)
        for k in range(3):
            g = g + r_ref[k].astype(F32)
        d, mn, vn = _adam_math(w_ref[...], g, m_ref[...], v_ref[...])
        g_out[...], d_out[...], m_out[...], v_out[...] = g, d, mn, vn

    blk = pl.BlockSpec((tr, cols), lambda i, q: (part * nblk + i, 0))
    specs = [blk, blk, blk, pl.BlockSpec((1, tr, cols), lambda i, q: (q[0], i, 0)),
             pl.BlockSpec((3, tr, cols), lambda i, q: (0, i, 0))]
    ins = [q_idx, w, m, v, t, r2]
    alias = {}
    if prev is not None:
        specs += [ANY] * 4
        alias = {6 + k: k for k in range(4)}
        ins += list(prev)
    return pl.pallas_call(
        body, name=name,
        grid_spec=pltpu.PrefetchScalarGridSpec(num_scalar_prefetch=1, grid=(nblk,), in_specs=specs, out_specs=[blk] * 4),
        out_shape=[SDS(w.shape, F32)] * 4, input_output_aliases=alias,
        compiler_params=_cp("parallel"))(*ins)


def _sum_devices(g, name):
    _, rows, cols = g.shape

    def body(g_ref, o_ref):
        acc = g_ref[0]
        for k in range(1, ND):
            acc = acc + g_ref[k]
        o_ref[...] = acc

    return pl.pallas_call(body, name=name, out_shape=SDS((rows, cols), F32))(g)


def _adamw_small(w, g, m, v, name):
    def body(w_ref, g_ref, m_ref, v_ref, d_out, m_out, v_out):
        d, mn, vn = _adam_math(w_ref[...], g_ref[...], m_ref[...], v_ref[...])
        d_out[...], m_out[...], v_out[...] = d, mn, vn

    return pl.pallas_call(body, name=name, out_shape=[SDS(w.shape, F32)] * 3)(w, g, m, v)


ROWS = 256


def _rms_stats(x):
    r = lax.rsqrt(jnp.mean(x * x, axis=-1, keepdims=True) + RMS_EPS)
    return x * r, r


def _rms_fwd(x, gains, name):
    s, d = x.shape
    n = len(gains)

    def body(x_ref, *refs):
        xh, _ = _rms_stats(x_ref[...])
        for g_ref, o_ref in zip(refs[:n], refs[n:]):
            o_ref[...] = (xh * g_ref[...]).astype(BF16)

    row = pl.BlockSpec((ROWS, d), lambda i: (i, 0))
    vec = pl.BlockSpec((1, d), lambda i: (0, 0))
    return pl.pallas_call(
        body, name=name, grid=(s // ROWS,), in_specs=[row] + [vec] * n, out_specs=[row] * n,
        out_shape=[SDS((s, d), BF16)] * n, compiler_params=_cp("parallel"))(x, *gains)


def _rms_bwd_rows(xh, r, gain, dy):
    u = dy * gain
    return r * (u - xh * jnp.mean(u * xh, axis=-1, keepdims=True))


def _rms_bwd(x, pairs, dres, name, colsum=False):
    s, d = x.shape
    n = len(pairs)

    def body(x_ref, dres_ref, *refs):
        g_refs, dy_refs = refs[:n], refs[n:2 * n]
        dx_ref, dxb_ref = refs[2 * n], refs[2 * n + 1]
        dg_refs = refs[2 * n + 2:2 * n + 2 + n]
        cs_ref = refs[-1] if colsum else None
        first = pl.program_id(0) == 0
        xh, r = _rms_stats(x_ref[...])
        dx = dres_ref[...]
        for g_ref, dy_ref, dg_ref in zip(g_refs, dy_refs, dg_refs):
            dy = dy_ref[...]
            dx = dx + _rms_bwd_rows(xh, r, g_ref[...], dy)

            @pl.when(first)
            def _():
                dg_ref[...] = jnp.zeros_like(dg_ref)
            dg_ref[...] += jnp.sum(dy * xh, axis=0, keepdims=True)
        dx_ref[...] = dx
        dxb_ref[...] = dx.astype(BF16)
        if colsum:
            @pl.when(first)
            def _():
                cs_ref[...] = jnp.zeros_like(cs_ref)
            cs_ref[...] += jnp.sum(dx, axis=0, keepdims=True)

    row = pl.BlockSpec((ROWS, d), lambda i: (i, 0))
    vec = pl.BlockSpec((1, d), lambda i: (0, 0))
    nvec = n + (1 if colsum else 0)
    outs = pl.pallas_call(
        body, name=name, grid=(s // ROWS,),
        in_specs=[row, row] + [vec] * n + [row] * n,
        out_specs=[row, row] + [vec] * nvec,
        out_shape=[SDS((s, d), F32), SDS((s, d), BF16)] + [SDS((1, d), F32)] * nvec,
        compiler_params=_cp("arbitrary"),
    )(x, dres, *[p[0] for p in pairs], *[p[1] for p in pairs])
    return outs


def _final_loss(h, target, gain):
    s, d = h.shape

    def body(h_ref, t_ref, g_ref, dh_ref, dhb_ref, dg_ref, loss_ref):
        first = pl.program_id(0) == 0
        xh, r = _rms_stats(h_ref[...])
        gain_v = g_ref[...]
        e = xh * gain_v - t_ref[...]
        dy = e * (1.0 / d)
        dx = _rms_bwd_rows(xh, r, gain_v, dy)
        dh_ref[...] = dx
        dhb_ref[...] = dx.astype(BF16)

        @pl.when(first)
        def _():
            dg_ref[...] = jnp.zeros_like(dg_ref)
            loss_ref[...] = jnp.zeros_like(loss_ref)
        dg_ref[...] += jnp.sum(dy * xh, axis=0, keepdims=True)
        loss_ref[...] += jnp.full((1, 128), 0.5 / d, F32) * jnp.sum(e * e)

    row = pl.BlockSpec((ROWS, d), lambda i: (i, 0))
    vec = pl.BlockSpec((1, d), lambda i: (0, 0))
    return pl.pallas_call(
        body, name="final_loss", grid=(s // ROWS,),
        in_specs=[row, row, vec], out_specs=[row, row, vec, pl.BlockSpec((1, 128), lambda i: (0, 0))],
        out_shape=[SDS((s, d), F32), SDS((s, d), BF16), SDS((1, d), F32), SDS((1, 128), F32)],
        compiler_params=_cp("arbitrary"))(h, target, gain)


CT = 128


def _ln_stats(cv):
    mu = jnp.mean(cv, axis=-1, keepdims=True)
    xc = cv - mu
    rstd = lax.rsqrt(jnp.mean(xc * xc, axis=-1, keepdims=True) + LN_EPS)
    return xc * rstd, rstd


def _conv_fwd(glu, dw, dwb, lng, lnb, jobs=()):
    s, d = glu.shape
    hb = CT // CONV_PAD

    def body(x_ref, halo_ref, dw_ref, dwb_ref, lng_ref, lnb_ref, c_ref, s_ref):
        keep = (pl.program_id(0) > 0).astype(F32)

        def chunk(ci, carry):
            ls = pl.ds(pl.multiple_of(ci * 128, 128), 128)
            xf = jnp.concatenate([halo_ref[:, ls] * keep, x_ref[:, ls]], axis=0)
            acc = jnp.zeros((CT, 128), F32)
            for k in range(CONV_W):
                sh = CONV_W - 1 - k
                xs = pltpu.roll(xf, sh, 0) if sh else xf
                acc = acc + dw_ref[pl.ds(k, 1), ls] * xs[CONV_PAD:]
            c_ref[:, ls] = acc + dwb_ref[:, ls]
            return carry

        lax.fori_loop(0, d // 128, chunk, 0)
        xh, _ = _ln_stats(c_ref[...])
        yv = xh * lng_ref[...] + lnb_ref[...]
        s_ref[...] = (yv * _sigmoid(yv)).astype(BF16)

    row = pl.BlockSpec((CT, d), lambda i: (i, 0))
    halo = pl.BlockSpec((CONV_PAD, d), lambda i: (jnp.maximum(i * hb - 1, 0), 0))
    vec = pl.BlockSpec((1, d), lambda i: (0, 0))
    taps = pl.BlockSpec((CONV_PAD, d), lambda i: (0, 0))
    return _pc(
        body, name="conv_fwd", grid=(s // CT,),
        in_specs=[row, halo, taps, vec, vec, vec], out_specs=[row, row],
        out_shape=[SDS((s, d), F32), SDS((s, d), BF16)], sem=("parallel",),
        args=(glu, glu, dw, dwb, lng, lnb), jobs=jobs)


def _ln_bwd(ds, cv, lng, lnb):
    s, d = cv.shape

    def body(ds_ref, c_ref, g_ref, b_ref, dc_ref, dg_ref, db_ref):
        first = pl.program_id(0) == 0
        xh, rstd = _ln_stats(c_ref[...])
        gv = g_ref[...]
        yv = xh * gv + b_ref[...]
        sg = _sigmoid(yv)
        dln = ds_ref[...] * (sg * (1.0 + yv * (1.0 - sg)))
        dxh = dln * gv
        dc_ref[...] = rstd * (dxh - jnp.mean(dxh, axis=-1, keepdims=True)
                              - xh * jnp.mean(dxh * xh, axis=-1, keepdims=True))

        @pl.when(first)
        def _():
            dg_ref[...] = jnp.zeros_like(dg_ref)
            db_ref[...] = jnp.zeros_like(db_ref)
        dg_ref[...] += jnp.sum(dln * xh, axis=0, keepdims=True)
        db_ref[...] += jnp.sum(dln, axis=0, keepdims=True)

    row = pl.BlockSpec((ROWS, d), lambda i: (i, 0))
    vec = pl.BlockSpec((1, d), lambda i: (0, 0))
    return pl.pallas_call(
        body, name="ln_bwd", grid=(s // ROWS,), in_specs=[row, row, vec, vec], out_specs=[row, vec, vec],
        out_shape=[SDS((s, d), F32), SDS((1, d), F32), SDS((1, d), F32)],
        compiler_params=_cp("arbitrary"))(ds, cv, lng, lnb)


def _conv_bwd(dc, glu, ua, ug, dw, jobs=()):
    s, d = dc.shape
    hb = CT // CONV_PAD
    nsteps = s // CT
    full = CT + CONV_PAD

    def body(dc_ref, dcn_ref, x_ref, xp_ref, ua_ref, ug_ref, dw_ref, du_ref, ddw_ref, ddwb_ref, db1_ref):
        i = pl.program_id(0)
        keep_prev = (i > 0).astype(F32)
        keep_next = (i < nsteps - 1).astype(F32)

        @pl.when(i == 0)
        def _():
            ddw_ref[...] = jnp.zeros_like(ddw_ref)
            ddwb_ref[...] = jnp.zeros_like(ddwb_ref)
            db1_ref[...] = jnp.zeros_like(db1_ref)

        def chunk(ci, carry):
            off = pl.multiple_of(ci * 128, 128)
            ls = pl.ds(off, 128)
            ls2 = pl.ds(pl.multiple_of(d + ci * 128, 128), 128)
            dcc = dc_ref[:, ls]
            dcf = jnp.concatenate([dcc, dcn_ref[:, ls] * keep_next], axis=0)
            xf = jnp.concatenate([xp_ref[:, ls] * keep_prev, x_ref[:, ls]], axis=0)
            dglu = jnp.zeros((CT, 128), F32)
            for k in range(CONV_W):
                sh = CONV_W - 1 - k
                dshift = pltpu.roll(dcf, full - sh, 0) if sh else dcf
                dglu = dglu + dw_ref[pl.ds(k, 1), ls] * dshift[:CT]
                xs = pltpu.roll(xf, sh, 0) if sh else xf
                ddw_ref[pl.ds(k, 1), ls] += jnp.sum(dcc * xs[CONV_PAD:], axis=0, keepdims=True)
            ddwb_ref[:, ls] += jnp.sum(dcc, axis=0, keepdims=True)
            av, gv = ua_ref[:, ls], ug_ref[:, ls]
            sg = _sigmoid(gv)
            da = dglu * sg
            dgt = dglu * av * sg * (1.0 - sg)
            du_ref[:, ls] = da.astype(BF16)
            du_ref[:, ls2] = dgt.astype(BF16)
            db1_ref[:, ls] += jnp.sum(da, axis=0, keepdims=True)
            db1_ref[:, ls2] += jnp.sum(dgt, axis=0, keepdims=True)
            return carry

        lax.fori_loop(0, d // 128, chunk, 0)

    row = pl.BlockSpec((CT, d), lambda i: (i, 0))
    prev = pl.BlockSpec((CONV_PAD, d), lambda i: (jnp.maximum(i * hb - 1, 0), 0))
    nxt = pl.BlockSpec((CONV_PAD, d), lambda i: (jnp.minimum((i + 1) * hb, s // CONV_PAD - 1), 0))
    taps = pl.BlockSpec((CONV_PAD, d), lambda i: (0, 0))
    return _pc(
        body, name="conv_bwd", grid=(nsteps,),
        in_specs=[row, nxt, row, prev, row, row, taps],
        out_specs=[pl.BlockSpec((CT, 2 * d), lambda i: (i, 0)), taps, pl.BlockSpec((1, d), lambda i: (0, 0)),
                   pl.BlockSpec((1, 2 * d), lambda i: (0, 0))],
        out_shape=[SDS((s, 2 * d), BF16), SDS((CONV_PAD, d), F32), SDS((1, d), F32), SDS((1, 2 * d), F32)],
        sem=("arbitrary",), args=(dc, dc, glu, glu, ua, ug, dw), jobs=jobs)


TM = 1024
TS = 1024


def _glu_mm(n1, w1g, b1, jobs=()):
    s, d = n1.shape
    cw = w1g.shape[2]
    half = ND // 2

    def body(a_ref, wa_ref, wg_ref, ba_ref, bg_ref, ua_ref, ug_ref, glu_ref):
        a = a_ref[...]
        ua = _dot(a, wa_ref[0], NN) + ba_ref[...]
        ug = _dot(a, wg_ref[0], NN) + bg_ref[...]
        ua_ref[...], ug_ref[...] = ua, ug
        glu_ref[...] = ua * _sigmoid(ug)

    out = pl.BlockSpec((TM, cw), lambda m, i: (m, i))
    return _pc(
        body, name="glu_mm", grid=(s // TM, half),
        in_specs=[pl.BlockSpec((TM, d), lambda m, i: (m, 0)),
                  pl.BlockSpec((1, d, cw), lambda m, i: (i, 0, 0)),
                  pl.BlockSpec((1, d, cw), lambda m, i: (i + half, 0, 0)),
                  pl.BlockSpec((1, cw), lambda m, i: (0, i)),
                  pl.BlockSpec((1, cw), lambda m, i: (0, i + half))],
        out_specs=[out, out, out], out_shape=[SDS((s, d), F32)] * 3,
        sem=("parallel", "arbitrary"), args=(n1, w1g, w1g, b1, b1), jobs=jobs)


def _mm_rows(a, wg, name, res=None, bias=None, out_dtype=F32, tn=512, branches=False, jobs=()):
    s, kdim = a.shape
    _, kc, n = wg.shape
    assert kc * ND == kdim
    nx = 2 + (res is not None) + (bias is not None)

    def body(*refs):
        acc = _dot(refs[0][...], refs[1][...].reshape(kdim, tn), NN)
        for extra in refs[2:nx]:
            acc = acc + extra[...]
        refs[nx][...] = acc.astype(out_dtype)
        if branches:
            scr = refs[-1]
            _stage(scr, acc)
            for o_ref, dil in zip(refs[nx + 1:], SPLIT_DILATIONS):
                _split_rows(scr, o_ref, dil)

    ins, specs = [a, wg], [pl.BlockSpec((TM, kdim), lambda m, j: (m, 0)), pl.BlockSpec((ND, kc, tn), lambda m, j: (0, 0, j))]
    if res is not None:
        ins.append(res)
        specs.append(pl.BlockSpec((TM, tn), lambda m, j: (m, j)))
    if bias is not None:
        ins.append(bias)
        specs.append(pl.BlockSpec((1, tn), lambda m, j: (0, j)))
    out_specs, out_shape, scratch = [pl.BlockSpec((TM, tn), lambda m, j: (m, j))], [SDS((s, n), out_dtype)], []
    if branches:
        out_specs += _branch_specs(TM, tn, lambda dil, m, j: (0, m, j))
        out_shape += [SDS((dil, s // dil, n), out_dtype) for dil in SPLIT_DILATIONS]
        scratch = [pltpu.VMEM((tn // 128, TM, 128), F32)]
    outs = _pc(body, name=name, grid=(s // TM, n // tn), in_specs=specs, out_specs=out_specs, out_shape=out_shape,
               scratch=scratch, sem=("parallel", "arbitrary"), args=ins, jobs=jobs)
    return [outs[0][None]] + outs[1:] if branches else outs[0]


def _swiglu_mm(n2, wgg, wug, name, jobs=()):
    s, d = n2.shape
    fc = wgg.shape[2]

    def body(a_ref, wg_ref, wu_ref, g_ref, u_ref, act_ref):
        a = a_ref[...]
        g = _dot(a, wg_ref[0], NN)
        u = _dot(a, wu_ref[0], NN)
        g_ref[0], u_ref[0] = g.astype(BF16), u.astype(BF16)
        act_ref[0] = (g * _sigmoid(g) * u).astype(BF16)

    wspec = pl.BlockSpec((1, d, fc), lambda m, j: (j, 0, 0))
    out = pl.BlockSpec((1, TM, fc), lambda m, j: (j, m, 0))
    return _pc(
        body, name=name, grid=(s // TM, ND),
        in_specs=[pl.BlockSpec((TM, d), lambda m, j: (m, 0)), wspec, wspec],
        out_specs=[out, out, out], out_shape=[SDS((ND, s, fc), BF16)] * 3,
        sem=("parallel", "arbitrary"), args=(n2, wgg, wug), jobs=jobs)


def _down_mm(act, wdg, res, name, jobs=()):
    _, s, fc = act.shape
    d = wdg.shape[2]

    def body(a_ref, w_ref, r_ref, o_ref):
        @pl.when(pl.program_id(1) == 0)
        def _():
            o_ref[...] = r_ref[...]
        o_ref[...] += _dot(a_ref[0], w_ref[0], NN)

    row = pl.BlockSpec((TM, d), lambda m, j: (m, 0))
    return _pc(
        body, name=name, grid=(s // TM, ND),
        in_specs=[pl.BlockSpec((1, TM, fc), lambda m, j: (j, m, 0)),
                  pl.BlockSpec((1, fc, d), lambda m, j: (j, 0, 0)), row],
        out_specs=[row], out_shape=[SDS((s, d), F32)],
        sem=("parallel", "arbitrary"), args=(act, wdg, res), jobs=jobs)[0]


def _dact_mm(dh, wdg, gate, up, name, jobs=()):
    s, d = dh.shape
    fc = wdg.shape[1]

    def body(a_ref, w_ref, g_ref, u_ref, dg_ref, du_ref):
        dact = _dot(a_ref[...], w_ref[0], NT)
        g, u = g_ref[0].astype(F32), u_ref[0].astype(F32)
        sg = _sigmoid(g)
        du_ref[0] = (dact * (g * sg)).astype(BF16)
        dg_ref[0] = (dact * u * (sg * (1.0 + g * (1.0 - sg)))).astype(BF16)

    blk = pl.BlockSpec((1, TM, fc), lambda m, j: (j, m, 0))
    return _pc(
        body, name=name, grid=(s // TM, ND),
        in_specs=[pl.BlockSpec((TM, d), lambda m, j: (m, 0)),
                  pl.BlockSpec((1, fc, d), lambda m, j: (j, 0, 0)), blk, blk],
        out_specs=[blk, blk], out_shape=[SDS((ND, s, fc), BF16)] * 2,
        sem=("parallel", "arbitrary"), args=(dh, wdg, gate, up), jobs=jobs)


def _dwd_mm(act, dh, name, jobs=()):
    _, s, fc = act.shape
    d = dh.shape[1]
    nk = s // TS

    def body(a_ref, b_ref, o_ref, acc):
        k = pl.program_id(1)

        @pl.when(k == 0)
        def _():
            acc[...] = jnp.zeros_like(acc)
        acc[...] += _dot(a_ref[0], b_ref[...], TN)

        @pl.when(k == nk - 1)
        def _():
            o_ref[0] = acc[...].astype(BF16)

    return _pc(
        body, name=name, grid=(ND, nk),
        in_specs=[pl.BlockSpec((1, TS, fc), lambda j, k: (j, k, 0)), pl.BlockSpec((TS, d), lambda j, k: (k, 0))],
        out_specs=[pl.BlockSpec((1, fc, d), lambda j, k: (_slot(j), 0, 0))],
        out_shape=[SDS((ND, fc, d), BF16)], scratch=[pltpu.VMEM((fc, d), F32)],
        sem=("parallel", "arbitrary"), args=(act, dh), jobs=jobs)[0]


def _dwgu_mm(n2, dgate, dup, name, jobs=()):
    s, d = n2.shape
    fc = dgate.shape[2]
    nk = s // TS

    def body(a_ref, g_ref, u_ref, og_ref, ou_ref, accg, accu):
        k = pl.program_id(1)

        @pl.when(k == 0)
        def _():
            accg[...] = jnp.zeros_like(accg)
            accu[...] = jnp.zeros_like(accu)
        a = a_ref[...]
        accg[...] += _dot(a, g_ref[0], TN)
        accu[...] += _dot(a, u_ref[0], TN)

        @pl.when(k == nk - 1)
        def _():
            og_ref[0] = accg[...].astype(BF16)
            ou_ref[0] = accu[...].astype(BF16)

    blk = pl.BlockSpec((1, TS, fc), lambda j, k: (j, k, 0))
    out = pl.BlockSpec((1, d, fc), lambda j, k: (_slot(j), 0, 0))
    return _pc(
        body, name=name, grid=(ND, nk),
        in_specs=[pl.BlockSpec((TS, d), lambda j, k: (k, 0)), blk, blk], out_specs=[out, out],
        out_shape=[SDS((ND, d, fc), BF16)] * 2,
        scratch=[pltpu.VMEM((d, fc), F32), pltpu.VMEM((d, fc), F32)],
        sem=("parallel", "arbitrary"), args=(n2, dgate, dup), jobs=jobs)


def _dn_ffn_mm(dgate, dup, wgg, wug, name, jobs=()):
    _, s, fc = dgate.shape
    d = wgg.shape[1]

    def body(g_ref, u_ref, wg_ref, wu_ref, o_ref):
        j = pl.program_id(1)

        @pl.when(j == 0)
        def _():
            o_ref[...] = jnp.zeros_like(o_ref)
        o_ref[...] += _dot(g_ref[0], wg_ref[0], NT) + _dot(u_ref[0], wu_ref[0], NT)

    blk = pl.BlockSpec((1, TM, fc), lambda m, j: (j, m, 0))
    wspec = pl.BlockSpec((1, d, fc), lambda m, j: (j, 0, 0))
    return _pc(
        body, name=name, grid=(s // TM, ND), in_specs=[blk, blk, wspec, wspec],
        out_specs=[pl.BlockSpec((TM, d), lambda m, j: (m, 0))], out_shape=[SDS((s, d), F32)],
        sem=("parallel", "arbitrary"), args=(dgate, dup, wgg, wug), jobs=jobs)[0]


def _mm_rows_t(pairs, name, out_dtype, branches=False, jobs=()):
    s, n = pairs[0][0].shape
    _, kc, _ = pairs[0][1].shape
    np_ = len(pairs)
    grp = ND // 2
    wide = grp * kc

    def body(*refs):
        o_ref = refs[2 * np_]
        for i in range(grp):
            acc = None
            for p in range(np_):
                t = _dot(refs[p][...], refs[np_ + p][i], NT)
                acc = t if acc is None else acc + t
            o_ref[:, kc * i:kc * (i + 1)] = acc.astype(out_dtype)
            if branches:
                for c, ls in enumerate(_lane_chunks(kc)):
                    refs[-1][i * (kc // 128) + c] = acc[:, ls]
        if branches:
            for b_ref, dil in zip(refs[2 * np_ + 1:], SPLIT_DILATIONS):
                _split_rows(refs[-1], b_ref, dil)

    out_specs, out_shape, scratch = [pl.BlockSpec((TM, wide), lambda m, j: (m, j))], [SDS((s, kc * ND), out_dtype)], []
    if branches:
        out_specs += _branch_specs(TM, wide, lambda dil, m, j: (0, m, j))
        out_shape += [SDS((dil, s // dil, kc * ND), out_dtype) for dil in SPLIT_DILATIONS]
        scratch = [pltpu.VMEM((wide // 128, TM, 128), F32)]
    outs = _pc(
        body, name=name, grid=(s // TM, ND // grp),
        in_specs=[pl.BlockSpec((TM, n), lambda m, j: (m, 0))] * np_ + [pl.BlockSpec((grp, kc, n), lambda m, j: (j, 0, 0))] * np_,
        out_specs=out_specs, out_shape=out_shape, scratch=scratch,
        sem=("parallel", "arbitrary"), args=[p[0] for p in pairs] + [p[1] for p in pairs], jobs=jobs)
    return [outs[0][None]] + outs[1:] if branches else outs[0]


def _dw_rows_mm(a, b, name):
    s, kdim = a.shape
    n = b.shape[1]
    kc = kdim // ND
    ts = TS // 2
    nk = s // ts

    def body(a_ref, b_ref, o_ref, acc):
        k = pl.program_id(0)

        @pl.when(k == 0)
        def _():
            acc[...] = jnp.zeros_like(acc)
        acc[...] += _dot(a_ref[...], b_ref[...], TN)

        @pl.when(k == nk - 1)
        def _():
            for dev in range(ND):
                o_ref[_slot(dev)] = acc[kc * dev:kc * (dev + 1), :].astype(BF16)

    return pl.pallas_call(
        body, name=name, grid=(nk,),
        in_specs=[pl.BlockSpec((ts, kdim), lambda k: (k, 0)), pl.BlockSpec((ts, n), lambda k: (k, 0))],
        out_specs=pl.BlockSpec((ND, kc, n), lambda k: (0, 0, 0)), out_shape=SDS((ND, kc, n), BF16),
        scratch_shapes=[pltpu.VMEM((kdim, n), F32)], compiler_params=_cp("arbitrary"))(a, b)


def _dw1_mm(n1, du, jobs=()):
    s, d = n1.shape
    cw = du.shape[1] // ND
    nk = s // TS

    def body(a_ref, b_ref, o_ref, acc):
        k = pl.program_id(1)

        @pl.when(k == 0)
        def _():
            acc[...] = jnp.zeros_like(acc)
        acc[...] += _dot(a_ref[...], b_ref[...], TN)

        @pl.when(k == nk - 1)
        def _():
            o_ref[0] = acc[...].astype(BF16)

    return _pc(
        body, name="dw1_mm", grid=(ND, nk),
        in_specs=[pl.BlockSpec((TS, d), lambda j, k: (k, 0)), pl.BlockSpec((TS, cw), lambda j, k: (k, j))],
        out_specs=[pl.BlockSpec((1, d, cw), lambda j, k: (_slot(j), 0, 0))], out_shape=[SDS((ND, d, cw), BF16)],
        scratch=[pltpu.VMEM((d, cw), F32)], sem=("parallel", "arbitrary"), args=(n1, du), jobs=jobs)[0]


def _dn1_mm(du, w1g, jobs=()):
    s = du.shape[0]
    _, d, cw = w1g.shape

    def body(a_ref, w_ref, o_ref):
        j = pl.program_id(1)

        @pl.when(j == 0)
        def _():
            o_ref[...] = jnp.zeros_like(o_ref)
        o_ref[...] += _dot(a_ref[...], w_ref[0], NT)

    return _pc(
        body, name="dn1_mm", grid=(s // TM, ND),
        in_specs=[pl.BlockSpec((TM, cw), lambda m, j: (m, j)), pl.BlockSpec((1, d, cw), lambda m, j: (j, 0, 0))],
        out_specs=[pl.BlockSpec((TM, d), lambda m, j: (m, 0))], out_shape=[SDS((s, d), F32)],
        sem=("parallel", "arbitrary"), args=(du, w1g), jobs=jobs)[0]


NEG = -1e30


def _slopes(heads):
    return [2.0 ** (-8.0 * (h + 1) / heads) for h in range(heads)]


def _band(has_prev):
    qi = lax.broadcasted_iota(jnp.int32, (BLK, 2 * BLK), 0)
    ki = lax.broadcasted_iota(jnp.int32, (BLK, 2 * BLK), 1)
    j = qi - ki + BLK
    ok = (j >= 0) & (j <= BLK) & (has_prev | (ki >= BLK))
    return j.astype(F32), ok


SPLIT_DILATIONS = tuple(dil for dil in BRANCH_DILATIONS if dil > 1)


def _lane_chunks(w):
    return [slice(128 * c, 128 * (c + 1)) for c in range(w // 128)]


def _stage(scr, tile):
    for c, ls in enumerate(_lane_chunks(tile.shape[1])):
        scr[c] = tile[:, ls]


def _split_rows(scr, o_ref, dil):
    _, n, w = o_ref.shape
    for r in range(dil):
        for c, ls in enumerate(_lane_chunks(w)):
            o_ref[r, :, ls] = scr[c, pl.ds(r, n, stride=dil), :].astype(o_ref.dtype)


def _join_rows(i_ref, scr, dil):
    _, n, w = i_ref.shape
    for r in range(dil):
        for c, ls in enumerate(_lane_chunks(w)):
            scr[c, pl.ds(r, n, stride=dil), :] = i_ref[r, :, ls].astype(F32)


def _unstage(scr, w):
    return jnp.concatenate([scr[c] for c in range(w // 128)], axis=1)


def _branch_specs(rows, w, index):
    return [pl.BlockSpec((dil, rows // dil, w), functools.partial(index, dil)) for dil in SPLIT_DILATIONS]


def _attn_fwd(q, k, v, dil, jobs=()):
    _, l, d = q.shape
    heads = d // HEAD
    assert heads <= HEAD
    scale = HEAD ** -0.5
    slopes = _slopes(heads)

    def body(q_ref, kc_ref, kp_ref, vc_ref, vp_ref, o_ref, lse_ref):
        jf, ok = _band(pl.program_id(1) > 0)
        lane = lax.broadcasted_iota(jnp.int32, (BLK, HEAD), 1)
        lse = jnp.zeros((BLK, HEAD), F32)
        for h in range(heads):
            sl = slice(HEAD * h, HEAD * (h + 1))
            kh = jnp.concatenate([kp_ref[0, :, sl], kc_ref[0, :, sl]], axis=0)
            vh = jnp.concatenate([vp_ref[0, :, sl], vc_ref[0, :, sl]], axis=0)
            logits = jnp.where(ok, _dot(q_ref[0, :, sl], kh, NT) * scale + jf * (-slopes[h] * dil), NEG)
            m = jnp.max(logits, axis=-1, keepdims=True)
            p = jnp.exp(logits - m)
            den = jnp.sum(p, axis=-1, keepdims=True)
            o_ref[0, :, sl] = _dot(p.astype(BF16), vh, NN) / den
            lse = jnp.where(lane == h, m + jnp.log(den), lse)
        lse_ref[0] = lse

    cur = pl.BlockSpec((1, BLK, d), lambda r, b: (r, b, 0))
    prev = pl.BlockSpec((1, BLK, d), lambda r, b: (r, jnp.maximum(b - 1, 0), 0))
    return _pc(
        body, name=f"attn_fwd_d{dil}", grid=(dil, l // BLK),
        in_specs=[cur, cur, prev, cur, prev], out_specs=[cur, pl.BlockSpec((1, BLK, HEAD), lambda r, b: (r, b, 0))],
        out_shape=[SDS((dil, l, d), F32), SDS((dil, l, HEAD), F32)], sem=("parallel", "arbitrary"),
        args=(q, k, k, v, v), jobs=jobs)


def _attn_merge(outs, lses):
    _, s, d = outs[0].shape
    heads = d // HEAD
    nb = len(outs)
    nsplit = nb - 1

    def body(*refs):
        o_refs, l_refs = refs[:nb], refs[nb:2 * nb]
        att_refs, lse_refs = refs[2 * nb:3 * nb], refs[3 * nb:4 * nb]
        scr_o, scr_l, scr_att = refs[4 * nb:4 * nb + nsplit], refs[4 * nb + nsplit:4 * nb + 2 * nsplit], refs[-1]
        ls = [l_refs[0][...]]
        for k, dil in enumerate(SPLIT_DILATIONS):
            _join_rows(o_refs[1 + k], scr_o[k], dil)
            _join_rows(l_refs[1 + k], scr_l[k], dil)
            ls.append(scr_l[k][0])
        m = functools.reduce(jnp.maximum, ls)
        ws = [jnp.exp(v - m) for v in ls]
        den = functools.reduce(jnp.add, ws)
        ws = [w / den for w in ws]
        lse_refs[0][...] = m + jnp.log(den)
        scr_l[0][0] = m + jnp.log(den)
        for h in range(heads):
            sl = slice(HEAD * h, HEAD * (h + 1))
            slab = ws[0][:, h:h + 1] * o_refs[0][:, sl]
            for k in range(nsplit):
                slab = slab + ws[1 + k][:, h:h + 1] * scr_o[k][h]
            att_refs[0][:, sl] = slab.astype(BF16)
            scr_att[h] = slab
        for k, dil in enumerate(SPLIT_DILATIONS):
            _split_rows(scr_att, att_refs[1 + k], dil)
            _split_rows(scr_l[0], lse_refs[1 + k], dil)

    def specs(w):
        return [pl.BlockSpec((ROWS, w), lambda i: (i, 0))] + _branch_specs(ROWS, w, lambda dil, i: (0, i, 0))

    def shapes(w, dt):
        return [SDS((s, w), dt)] + [SDS((dil, s // dil, w), dt) for dil in SPLIT_DILATIONS]

    wide, narrow = pltpu.VMEM((heads, ROWS, 128), F32), pltpu.VMEM((1, ROWS, 128), F32)
    res = pl.pallas_call(
        body, name="attn_merge", grid=(s // ROWS,), in_specs=specs(d) + specs(HEAD), out_specs=specs(d) + specs(HEAD),
        out_shape=shapes(d, BF16) + shapes(HEAD, F32),
        scratch_shapes=[wide] * nsplit + [narrow] * nsplit + [wide],
        compiler_params=_cp("parallel"))(outs[0].reshape(s, d), *outs[1:], lses[0].reshape(s, HEAD), *lses[1:])
    return list(res[:nb]), list(res[nb:])


def _attn_bwd(q, k, v, do, o, lse, dil, jobs=()):
    _, l, d = q.shape
    nb = l // BLK
    heads = d // HEAD
    scale = HEAD ** -0.5
    slopes = _slopes(heads)

    def body(q_ref, kc_ref, kp_ref, vc_ref, vp_ref, do_ref, o_ref, lse_ref, dq_ref, dk_ref, dv_ref, ck, cv):
        b = pl.program_id(1)

        @pl.when(b == 0)
        def _():
            ck[...] = jnp.zeros_like(ck)
            cv[...] = jnp.zeros_like(cv)

        @pl.when(b < nb)
        def _():
            jf, ok = _band(b > 0)
            for h in range(heads):
                sl = slice(HEAD * h, HEAD * (h + 1))
                qh, doh = q_ref[0, :, sl], do_ref[0, :, sl]
                kh = jnp.concatenate([kp_ref[0, :, sl], kc_ref[0, :, sl]], axis=0)
                vh = jnp.concatenate([vp_ref[0, :, sl], vc_ref[0, :, sl]], axis=0)
                lse_h = lse_ref[0, :, h:h + 1]
                delta = jnp.sum(doh.astype(F32) * o_ref[0, :, sl].astype(F32), axis=-1, keepdims=True)
                p = jnp.where(ok, jnp.exp(_dot(qh, kh, NT) * scale + jf * (-slopes[h] * dil) - lse_h), 0.0)
                ds = (p * (_dot(doh, vh, NT) - delta)).astype(BF16)
                dq_ref[0, :, sl] = (_dot(ds, kh, NN) * scale).astype(BF16)
                dk2 = _dot(ds, qh, TN) * scale
                dv2 = _dot(p.astype(BF16), doh, TN)
                dk_ref[0, :, sl] = (ck[:, sl] + dk2[:BLK]).astype(BF16)
                dv_ref[0, :, sl] = (cv[:, sl] + dv2[:BLK]).astype(BF16)
                ck[:, sl] = dk2[BLK:]
                cv[:, sl] = dv2[BLK:]

        @pl.when(b == nb)
        def _():
            dk_ref[0] = ck[...].astype(BF16)
            dv_ref[0] = cv[...].astype(BF16)

    cur = pl.BlockSpec((1, BLK, d), lambda r, b: (r, jnp.minimum(b, nb - 1), 0))
    prev = pl.BlockSpec((1, BLK, d), lambda r, b: (r, jnp.clip(b - 1, 0, nb - 1), 0))
    lse_spec = pl.BlockSpec((1, BLK, HEAD), lambda r, b: (r, jnp.minimum(b, nb - 1), 0))
    return _pc(
        body, name=f"attn_bwd_d{dil}", grid=(dil, nb + 1),
        in_specs=[cur, cur, prev, cur, prev, cur, cur, lse_spec], out_specs=[cur, prev, prev],
        out_shape=[SDS((dil, l, d), BF16)] * 3,
        scratch=[pltpu.VMEM((BLK, d), F32), pltpu.VMEM((BLK, d), F32)],
        sem=("parallel", "arbitrary"), args=(q, k, k, v, v, do, o, lse), jobs=jobs)


def _sum_cast(xs, name):
    _, s, d = xs[0].shape
    nsplit = len(xs) - 1

    def body(*refs):
        i_refs, o_ref, scr = refs[:nsplit + 1], refs[nsplit + 1], refs[nsplit + 2:]
        acc = i_refs[0][...].astype(F32)
        for k, dil in enumerate(SPLIT_DILATIONS):
            _join_rows(i_refs[1 + k], scr[k], dil)
            acc = acc + _unstage(scr[k], d)
        o_ref[...] = acc.astype(BF16)

    row = pl.BlockSpec((ROWS, d), lambda i: (i, 0))
    return pl.pallas_call(
        body, name=name, grid=(s // ROWS,), in_specs=[row] + _branch_specs(ROWS, d, lambda dil, i: (0, i, 0)),
        out_specs=row, out_shape=SDS((s, d), BF16),
        scratch_shapes=[pltpu.VMEM((d // 128, ROWS, 128), F32)] * nsplit,
        compiler_params=_cp("parallel"))(xs[0].reshape(s, d), *xs[1:])


def _pack_rows(vs, width):
    flat = jnp.concatenate([v.reshape(-1) for v in vs])
    spans, at = [], 0
    for v in vs:
        spans.append((at, v.size))
        at += v.size
    rows = -(-at // width)
    rows = -(-rows // 8) * 8
    flat = jnp.pad(flat, (0, rows * width - at))
    return flat.reshape(rows, width), spans


def kernel(x, a_norm_g, conv_w1, conv_b1, conv_dw, conv_dw_b, conv_ln_g, conv_ln_b, conv_w2, conv_b2, kv_norm_g, w_k, w_v, b_norm_g, w_q, w_o, ffn_norm_g, ffn_w_gate, ffn_w_up, ffn_w_down, final_norm_g, loss_target, m_a_norm_g, m_conv_w1, m_conv_b1, m_conv_dw, m_conv_dw_b, m_conv_ln_g, m_conv_ln_b, m_conv_w2, m_conv_b2, m_kv_norm_g, m_w_k, m_w_v, m_b_norm_g, m_w_q, m_w_o, m_ffn_norm_g, m_ffn_w_gate, m_ffn_w_up, m_ffn_w_down, m_final_norm_g, v_a_norm_g, v_conv_w1, v_conv_b1, v_conv_dw, v_conv_dw_b, v_conv_ln_g, v_conv_ln_b, v_conv_w2, v_conv_b2, v_kv_norm_g, v_w_k, v_w_v, v_b_norm_g, v_w_q, v_w_o, v_ffn_norm_g, v_ffn_w_gate, v_ffn_w_up, v_ffn_w_down, v_final_norm_g):
    s, d = x.shape[1], x.shape[2]
    dc = d // ND
    h0 = x[0]
    target = loss_target[0]
    xi, yi, ci = lax.axis_index("x"), lax.axis_index("y"), lax.axis_index("c")
    me = 4 * xi + 2 * yi + ci
    c_idx = jnp.reshape(ci, (1,)).astype(jnp.int32)
    q_idx = jnp.reshape(2 * xi + yi, (1,)).astype(jnp.int32)

    bf = lambda w: w.astype(BF16)
    small_shards = [a_norm_g, conv_b1, conv_dw, conv_dw_b, conv_ln_g, conv_ln_b, conv_b2]
    sp, sp_spans = _pack_rows(small_shards, dc)
    w1g, spg = _all_gather([bf(conv_w1[0]), sp], "gather_first")
    spg = spg.reshape(ND, -1)

    def small_full(i, rows):
        at, size = sp_spans[i]
        return spg[:, at:at + size].reshape(ND, rows, size // rows).transpose(1, 0, 2).reshape(rows, -1)

    a_g = small_full(0, 1)
    b1 = small_full(1, 1)
    dw = jnp.pad(small_full(2, CONV_W), ((0, CONV_PAD - CONV_W), (0, 0)))
    dwb, lng, lnb, b2 = small_full(3, 1), small_full(4, 1), small_full(5, 1), small_full(6, 1)
    kv_g, q_g, fin_g = kv_norm_g.reshape(1, d), b_norm_g.reshape(1, d), final_norm_g.reshape(1, d)
    f_g = [ffn_norm_g[0:1], ffn_norm_g[1:2]]

    def send(*shards):
        return _job_gather_send([bf(t) for t in shards])

    def forward(job):
        return _job_gather_forward(job.result)

    (n1,) = _rms_fwd(h0, [a_g], "rms_a")
    s_a = send(conv_w2[0], ffn_w_gate[0])
    ua, ug, glu = _glu_mm(n1, w1g, b1, jobs=[s_a])
    f_a, s_b = forward(s_a), send(ffn_w_up[0])
    cv, sw = _conv_fwd(glu, dw, dwb, lng, lnb, jobs=[f_a, s_b])
    w2g, wg0 = f_a.result
    f_b, s_c = forward(s_b), send(ffn_w_down[0])
    h1 = _mm_rows(sw, w2g, "w2_mm", res=h0, bias=b2, jobs=[f_b, s_c])
    (wu0,) = f_b.result
    (n2a,) = _rms_fwd(h1, [f_g[0]], "rms_f0")
    f_c, s_d = forward(s_c), send(w_k, w_v, w_q[0], w_o[0])
    gate0, up0, act0 = _swiglu_mm(n2a, wg0, wu0, "swiglu_mm0", jobs=[f_c, s_d])
    (wd0,) = f_c.result
    f_d, s_e = forward(s_d), send(ffn_w_gate[1])
    h2 = _down_mm(act0, wd0, h1, "down_mm0", jobs=[f_d, s_e])
    wkg, wvg, wqg, wog = f_d.result
    kvn, qn = _rms_fwd(h2, [kv_g, q_g], "rms_kvq")
    f_e = forward(s_e)
    kk = _mm_rows(kvn, wkg, "k_mm", out_dtype=BF16, branches=True, jobs=[f_e])
    (wg1,) = f_e.result
    vv = _mm_rows(kvn, wvg, "v_mm", out_dtype=BF16, branches=True)
    qq = _mm_rows(qn, wqg, "q_mm", out_dtype=BF16, branches=True)
    branch = {dil: (qq[i], kk[i], vv[i]) for i, dil in enumerate(BRANCH_DILATIONS)}
    s_f = send(ffn_w_up[1])
    o1, l1 = _attn_fwd(*branch[1], 1, jobs=[s_f])
    f_f, s_g = forward(s_f), send(ffn_w_down[1])
    o4, l4 = _attn_fwd(*branch[4], 4, jobs=[f_f, s_g])
    (wu1,) = f_f.result
    f_h = forward(s_g)
    o16, l16 = _attn_fwd(*branch[16], 16, jobs=[f_h])
    (wd1,) = f_h.result
    atts, lses = _attn_merge([o1, o4, o16], [l1, l4, l16])
    att = atts[0]
    atts, lses = [att[None]] + atts[1:], [lses[0][None]] + lses[1:]
    h3 = _mm_rows(att, wog, "wo_mm", res=h2)
    (n2b,) = _rms_fwd(h3, [f_g[1]], "rms_f1")
    gate1, up1, act1 = _swiglu_mm(n2b, wg1, wu1, "swiglu_mm1")
    h4 = _down_mm(act1, wd1, h3, "down_mm1")

    flat = lambda g: g.reshape(ND, -1, g.shape[-1])
    chip_sums, cross = {}, {}

    def to_sibling(**grads):
        job = _job_scatter_sibling([flat(g) for g in grads.values()])
        job.names = list(grads)
        return job

    def add_up(job):
        for n, g, r in zip(job.names, job.ins, job.result):
            chip_sums[n] = _rs_add(g, r, c_idx, f"rs_add_{n}")

    def to_chips(*names):
        job = _job_scatter_cross([chip_sums[n] for n in names])
        job.names = names
        return job

    def landed(job):
        cross.update(zip(job.names, job.result))

    dh4, dh4b, d_fin, loss_row = _final_loss(h4, target, fin_g)
    dgate1, dup1 = _dact_mm(dh4b, wd1, gate1, up1, "dact_mm1")
    g_wd1 = _dwd_mm(act1, dh4b, "dwd_mm1")
    j1 = to_sibling(wd1=g_wd1)
    g_wg1, g_wu1 = _dwgu_mm(n2b, dgate1, dup1, "dwgu_mm1", jobs=[j1])
    add_up(j1)
    j2, j3 = to_chips("wd1"), to_sibling(wg1=g_wg1, wu1=g_wu1)
    dn2b = _dn_ffn_mm(dgate1, dup1, wg1, wu1, "dn_ffn_mm1", jobs=[j2, j3])
    landed(j2)
    add_up(j3)
    dh3, dh3b, d_f1 = _rms_bwd(h3, [(f_g[1], dn2b)], dh4, "rms_f1_bwd")
    g_wo = _dw_rows_mm(att, dh3b, "dwo_mm")
    j4 = to_sibling(wo=g_wo)
    datt = _mm_rows_t([(dh3b, wog)], "datt_mm", BF16, branches=True, jobs=[j4])
    add_up(j4)
    riders = {1: to_chips("wg1"), 4: to_chips("wu1"), 16: to_chips("wo")}
    dqs, dks, dvs = [], [], []
    for i, dil in enumerate(BRANCH_DILATIONS):
        qb, kb, vb = branch[dil]
        dq_b, dk_b, dv_b = _attn_bwd(qb, kb, vb, datt[i], atts[i], lses[i], dil, jobs=[riders[dil]])
        landed(riders[dil])
        dqs.append(dq_b)
        dks.append(dk_b)
        dvs.append(dv_b)
    dq, dk, dv = _sum_cast(dqs, "dq_sum"), _sum_cast(dks, "dk_sum"), _sum_cast(dvs, "dv_sum")
    g_wq = _dw_rows_mm(qn, dq, "dwq_mm")
    g_wk = _dw_rows_mm(kvn, dk, "dwk_mm")
    g_wv = _dw_rows_mm(kvn, dv, "dwv_mm")
    j5 = to_sibling(wq=g_wq, wk=g_wk, wv=g_wv)
    dqn = _mm_rows_t([(dq, wqg)], "dqn_mm", F32, jobs=[j5])
    add_up(j5)
    j6 = to_chips("wq", "wk")
    dkvn = _mm_rows_t([(dk, wkg), (dv, wvg)], "dkvn_mm", F32, jobs=[j6])
    landed(j6)
    dh2, dh2b, d_q, d_kv = _rms_bwd(h2, [(q_g, dqn), (kv_g, dkvn)], dh3, "rms_kvq_bwd")
    j7 = to_chips("wv")
    dgate0, dup0 = _dact_mm(dh2b, wd0, gate0, up0, "dact_mm0", jobs=[j7])
    landed(j7)
    g_wd0 = _dwd_mm(act0, dh2b, "dwd_mm0")
    j8 = to_sibling(wd0=g_wd0)
    g_wg0, g_wu0 = _dwgu_mm(n2a, dgate0, dup0, "dwgu_mm0", jobs=[j8])
    add_up(j8)
    j9, j10 = to_chips("wd0"), to_sibling(wg0=g_wg0, wu0=g_wu0)
    dn2a = _dn_ffn_mm(dgate0, dup0, wg0, wu0, "dn_ffn_mm0", jobs=[j9, j10])
    landed(j9)
    add_up(j10)
    dh1, dh1b, d_f0, d_b2 = _rms_bwd(h1, [(f_g[0], dn2a)], dh2, "rms_f0_bwd", colsum=True)
    g_w2 = _dw_rows_mm(sw, dh1b, "dw2_mm")
    j11 = to_sibling(w2=g_w2)
    dsw = _mm_rows_t([(dh1b, w2g)], "dsw_mm", F32, jobs=[j11])
    add_up(j11)
    dcv, d_lng, d_lnb = _ln_bwd(dsw, cv, lng, lnb)
    j12 = to_chips("wg0", "wu0")
    du, d_dw, d_dwb, d_b1 = _conv_bwd(dcv, glu, ua, ug, dw, jobs=[j12])
    landed(j12)
    j13 = to_chips("w2")
    g_w1 = _dw1_mm(n1, du, jobs=[j13])
    landed(j13)
    j14 = to_sibling(w1=g_w1)
    dn1 = _dn1_mm(du, w1g, jobs=[j14])
    add_up(j14)
    dx, _, d_a = _rms_bwd(h0, [(a_g, dn1)], dh1, "rms_a_bwd")

    small_g = [d_a, d_b1, d_dw[:CONV_W], d_dwb, d_lng, d_lnb, d_b2, d_kv, d_q, d_f0, d_f1, d_fin, loss_row]
    gp, gp_spans = _pack_rows(small_g, d)
    j15, j16 = to_chips("w1"), _job_gather_send([gp])
    _comm_call([j15, j16], "rs_w1_send_small")
    landed(j15)
    j17 = _job_gather_forward(j16.result)
    _comm_call([j17], "forward_small")
    (gpg,) = j17.result

    two = lambda t: t.reshape(-1, t.shape[-1])

    def adam(w, m, v, names, tag):
        res = None
        for part, n in enumerate(names):
            res = _adamw_big(two(w), two(m), two(v), chip_sums[n], cross[n], q_idx, f"adamw_{tag}{part}", part, res)
        return [t.reshape(w.shape) for t in res]

    big_out = [
        adam(conv_w1, m_conv_w1, v_conv_w1, ["w1"], "w1"), adam(conv_w2, m_conv_w2, v_conv_w2, ["w2"], "w2"),
        adam(w_k, m_w_k, v_w_k, ["wk"], "wk"), adam(w_v, m_w_v, v_w_v, ["wv"], "wv"),
        adam(w_q, m_w_q, v_w_q, ["wq"], "wq"), adam(w_o, m_w_o, v_w_o, ["wo"], "wo"),
        adam(ffn_w_gate, m_ffn_w_gate, v_ffn_w_gate, ["wg0", "wg1"], "wg"),
        adam(ffn_w_up, m_ffn_w_up, v_ffn_w_up, ["wu0", "wu1"], "wu"),
        adam(ffn_w_down, m_ffn_w_down, v_ffn_w_down, ["wd0", "wd1"], "wd")]

    gsum = _sum_devices(gpg, "sum_small_grads").reshape(-1)

    def gfull(i):
        at, size = gp_spans[i]
        return gsum[at:at + size]

    def shard_of(vec, rows):
        return lax.dynamic_slice_in_dim(vec.reshape(rows, -1), me * (vec.size // rows // ND), vec.size // rows // ND, axis=1)

    loss = gfull(12)[0]
    small_grads = [
        shard_of(gfull(0), 1), shard_of(gfull(1), 1), shard_of(gfull(2), CONV_W)[None], shard_of(gfull(3), 1),
        shard_of(gfull(4), 1), shard_of(gfull(5), 1), shard_of(gfull(6), 1),
        gfull(7), gfull(8)[None], jnp.stack([gfull(9), gfull(10)]), gfull(11)]
    small_w = [a_norm_g, conv_b1, conv_dw, conv_dw_b, conv_ln_g, conv_ln_b, conv_b2, kv_norm_g, b_norm_g, ffn_norm_g, final_norm_g]
    small_m = [m_a_norm_g, m_conv_b1, m_conv_dw, m_conv_dw_b, m_conv_ln_g, m_conv_ln_b, m_conv_b2, m_kv_norm_g, m_b_norm_g, m_ffn_norm_g, m_final_norm_g]
    small_v = [v_a_norm_g, v_conv_b1, v_conv_dw, v_conv_dw_b, v_conv_ln_g, v_conv_ln_b, v_conv_b2, v_kv_norm_g, v_b_norm_g, v_ffn_norm_g, v_final_norm_g]
    small_grads = [g.reshape(w.shape) for g, w in zip(small_grads, small_w)]
    wp, spans = _pack_rows(small_w, 128)
    gpk, _ = _pack_rows(small_grads, 128)
    mp, _ = _pack_rows(small_m, 128)
    vp, _ = _pack_rows(small_v, 128)
    dp, mnp, vnp = _adamw_small(wp, gpk, mp, vp, "adamw_small")

    def unpack(packed):
        flat = packed.reshape(-1)
        return [flat[at:at + size].reshape(w.shape) for (at, size), w in zip(spans, small_w)]

    small_out = list(zip(small_grads, unpack(dp), unpack(mnp), unpack(vnp)))

    order = ["a_norm_g", "conv_w1", "conv_b1", "conv_dw", "conv_dw_b", "conv_ln_g", "conv_ln_b", "conv_w2", "conv_b2",
             "kv_norm_g", "w_k", "w_v", "b_norm_g", "w_q", "w_o", "ffn_norm_g", "ffn_w_gate", "ffn_w_up", "ffn_w_down",
             "final_norm_g"]
    big_names = ["conv_w1", "conv_w2", "w_k", "w_v", "w_q", "w_o", "ffn_w_gate", "ffn_w_up", "ffn_w_down"]
    small_names = ["a_norm_g", "conv_b1", "conv_dw", "conv_dw_b", "conv_ln_g", "conv_ln_b", "conv_b2", "kv_norm_g",
                   "b_norm_g", "ffn_norm_g", "final_norm_g"]
    table = {n: big_out[i] for i, n in enumerate(big_names)}
    table.update({n: small_out[i] for i, n in enumerate(small_names)})
    result = [loss, dx[None]]
    for kind in range(4):
        result += [table[n][kind] for n in order]
    return tuple(result)
```

```python
import functools

import jax
import jax.numpy as jnp
from jax import lax
from jax.experimental import pallas as pl
from jax.experimental.pallas import tpu as pltpu

ND = 8
HEAD = 128
BLK = 128
BRANCH_DILATIONS = (1, 4, 16)
CONV_W = 31
CONV_PAD = 32
RMS_EPS = 1e-6
LN_EPS = 1e-5
LR, B1, B2, ADAM_EPS, WD, STEP = 0.001, 0.9, 0.999, 1e-08, 0.01, 10
VMEM_LIMIT = 56 * 1024 * 1024

F32, BF16 = jnp.float32, jnp.bfloat16
SDS = jax.ShapeDtypeStruct
MESH = pl.DeviceIdType.MESH
ANY = pl.BlockSpec(memory_space=pl.ANY)

NN = (((1,), (0,)), ((), ()))
NT = (((1,), (1,)), ((), ()))
TN = (((0,), (0,)), ((), ()))


def _dot(a, b, dims):
    return lax.dot_general(a, b, dims, preferred_element_type=F32)


def _cp(*sem):
    return pltpu.CompilerParams(dimension_semantics=sem, vmem_limit_bytes=VMEM_LIMIT)


def _slot(dev):
    return 4 * (dev % 2) + dev // 2


def _sigmoid(v):
    return 1.0 / (1.0 + jnp.exp(-v))


class _Job:
    def __init__(self, ins, out_shapes, alias, nsem, nlocal, make):
        self.ins, self.out_shapes, self.alias = list(ins), list(out_shapes), dict(alias)
        self.nsem, self.nlocal, self.make = nsem, nlocal, make
        self.result = None


def _coords():
    return lax.axis_index("x"), lax.axis_index("y"), lax.axis_index("c")


def _remote(src, dst, send, recv, k, to):
    return pltpu.make_async_remote_copy(src_ref=src, dst_ref=dst, send_sem=send.at[k], recv_sem=recv.at[k],
                                        device_id=to, device_id_type=MESH)


def _job_gather_send(shards):
    n = len(shards)

    def make(ins, outs, send, recv, local):
        x, y, c = _coords()
        targets = [(x, y, 1 - c), (1 - x, y, c), (x, 1 - y, c), (1 - x, 1 - y, c)]
        cps = []
        for a in range(n):
            dst = outs[a].at[4 * x + 2 * y + c]
            cps.append(pltpu.make_async_copy(ins[a], dst, local.at[a]))
            cps += [_remote(ins[a], dst, send, recv, 4 * a + k, t) for k, t in enumerate(targets)]
        return cps

    return _Job(shards, [SDS((ND,) + s.shape, s.dtype) for s in shards], {}, 4 * n, n, make)


def _job_gather_send_rows(shard, part, nparts, prev=None):
    rows = shard.shape[0] // nparts

    def make(ins, outs, send, recv, local):
        x, y, c = _coords()
        targets = [(x, y, 1 - c), (1 - x, y, c), (x, 1 - y, c), (1 - x, 1 - y, c)]
        src = ins[0].at[pl.ds(part * rows, rows)]
        dst = outs[0].at[4 * x + 2 * y + c].at[pl.ds(part * rows, rows)]
        return [pltpu.make_async_copy(src, dst, local.at[0])] + [
            _remote(src, dst, send, recv, k, t) for k, t in enumerate(targets)]

    ins = [shard] if prev is None else [shard, prev]
    return _Job(ins, [SDS((ND,) + shard.shape, shard.dtype)], {} if prev is None else {1: 0}, 4, 1, make)


def _job_gather_forward(gathered):
    n = len(gathered)

    def make(ins, outs, send, recv, local):
        x, y, c = _coords()
        cps = []
        for a in range(n):
            for k, (px, py) in enumerate([(1 - x, y), (x, 1 - y), (1 - x, 1 - y)]):
                blk = outs[a].at[4 * px + 2 * py + c]
                cps.append(_remote(blk, blk, send, recv, 3 * a + k, (x, y, 1 - c)))
        return cps

    return _Job(gathered, [SDS(g.shape, g.dtype) for g in gathered], {i: i for i in range(n)}, 3 * n, 0, make)


def _job_scatter_sibling(grads):
    n = len(grads)

    def make(ins, outs, send, recv, local):
        x, y, c = _coords()
        return [_remote(ins[a].at[pl.ds(4 * (1 - c), 4)], outs[a], send, recv, a, (x, y, 1 - c)) for a in range(n)]

    return _Job(grads, [SDS((4,) + g.shape[1:], g.dtype) for g in grads], {}, n, 0, make)


def _job_scatter_cross(sums):
    n = len(sums)

    def make(ins, outs, send, recv, local):
        x, y, c = _coords()
        chips = [(1 - x, y), (x, 1 - y), (1 - x, 1 - y)]
        return [_remote(ins[a].at[2 * px + py], outs[a].at[k], send, recv, 3 * a + k, (px, py, c))
                for a in range(n) for k, (px, py) in enumerate(chips)]

    return _Job(sums, [SDS((3,) + t.shape[1:], t.dtype) for t in sums], {}, 3 * n, 0, make)


def _pc(body, *, name, grid, in_specs, out_specs, out_shape, args, scratch=(), sem=(), alias=None, jobs=()):
    jobs = list(jobs)
    n_in, n_out, n_scr = len(in_specs), len(out_shape), len(scratch)
    aliases = dict(alias or {})
    job_args, job_shapes, job_scratch = [], [], []
    for j in jobs:
        for src, dst in j.alias.items():
            aliases[n_in + len(job_args) + src] = n_out + len(job_shapes) + dst
        job_args += j.ins
        job_shapes += j.out_shapes
        job_scratch += [pltpu.SemaphoreType.DMA((j.nsem,)), pltpu.SemaphoreType.DMA((j.nsem,)),
                        pltpu.SemaphoreType.DMA((max(j.nlocal, 1),))]

    def wrapped(*refs):
        ins = refs[:n_in]
        p = n_in + len(job_args)
        outs = refs[p:p + n_out]
        p += n_out + len(job_shapes)
        scr = refs[p:p + n_scr]
        sems = refs[p + n_scr:]
        copies = []
        pi, po = n_in, n_in + len(job_args) + n_out
        for k, j in enumerate(jobs):
            copies += j.make(refs[pi:pi + len(j.ins)], refs[po:po + len(j.out_shapes)], *sems[3 * k:3 * k + 3])
            pi += len(j.ins)
            po += len(j.out_shapes)
        gridded = bool(copies) and bool(grid)
        if gridded:
            ids = [pl.program_id(i) for i in range(len(grid))]
            first = functools.reduce(jnp.logical_and, [i == 0 for i in ids])
            last = functools.reduce(jnp.logical_and, [i == g - 1 for i, g in zip(ids, grid)])

            @pl.when(first)
            def _():
                for cp in copies:
                    cp.start()
        else:
            for cp in copies:
                cp.start()
        body(*ins, *outs, *scr)
        if gridded:
            @pl.when(last)
            def _():
                for cp in copies:
                    cp.wait()
        else:
            for cp in copies:
                cp.wait()

    kw = dict(grid=grid) if grid else {}
    semantics = ["arbitrary"] * len(grid) if jobs else list(sem)
    res = pl.pallas_call(
        wrapped, name=name, in_specs=list(in_specs) + [ANY] * len(job_args),
        out_specs=list(out_specs) + [ANY] * len(job_shapes), out_shape=list(out_shape) + job_shapes,
        scratch_shapes=list(scratch) + job_scratch, input_output_aliases=aliases,
        compiler_params=_cp(*semantics), **kw)(*args, *job_args)
    p = n_out
    for j in jobs:
        j.result = list(res[p:p + len(j.out_shapes)])
        p += len(j.out_shapes)
    return list(res[:n_out])


def _comm_call(jobs, name):
    _pc(lambda: None, name=name, grid=(), in_specs=[], out_specs=[], out_shape=[], args=[], jobs=jobs)


def _all_gather(arrs, name):
    n = len(arrs)

    def body(*refs):
        ins, outs = refs[:n], refs[n:2 * n]
        send_sems, recv_sems, local_sems = refs[2 * n:]
        x, y, c = lax.axis_index("x"), lax.axis_index("y"), lax.axis_index("c")
        me, sib = (x, y, c), (x, y, 1 - c)
        chips = [(1 - x, y), (x, 1 - y), (1 - x, 1 - y)]

        def copy(a, k, block, to, src=None):
            dst = outs[a].at[4 * block[0] + 2 * block[1] + block[2]]
            return pltpu.make_async_remote_copy(
                src_ref=dst if src is None else src, dst_ref=dst,
                send_sem=send_sems.at[7 * a + k], recv_sem=recv_sems.at[7 * a + k],
                device_id=to, device_id_type=MESH)

        mine = [pltpu.make_async_copy(ins[a], outs[a].at[4 * x + 2 * y + c], local_sems.at[a]) for a in range(n)]
        for cp in mine:
            cp.start()
        first = []
        for a in range(n):
            first.append(copy(a, 0, me, sib, src=ins[a]))
            first += [copy(a, 1 + j, me, (*chip, c), src=ins[a]) for j, chip in enumerate(chips)]
        for cp in first:
            cp.start()
        passed = []
        for a in range(n):
            for j, chip in enumerate(chips):
                copy(a, 1 + j, (*chip, c), me).wait_recv()
                fwd = copy(a, 4 + j, (*chip, c), sib)
                fwd.start()
                passed.append(fwd)
        for a in range(n):
            copy(a, 0, sib, me).wait_recv()
            for j, chip in enumerate(chips):
                copy(a, 4 + j, (*chip, 1 - c), me).wait_recv()
        for cp in first + passed:
            cp.wait_send()
        for cp in mine:
            cp.wait()

    return pl.pallas_call(
        body, name=name,
        out_shape=[SDS((ND,) + a.shape, a.dtype) for a in arrs],
        in_specs=[ANY] * n, out_specs=[ANY] * n,
        scratch_shapes=[pltpu.SemaphoreType.DMA((7 * n,)), pltpu.SemaphoreType.DMA((7 * n,)),
                        pltpu.SemaphoreType.DMA((n,))],
    )(*arrs)


ELEMENTWISE_TILE_BYTES = 3 * 512 * 1024


def _row_tile(rows, cols):
    fits = [t for t in range(16, rows + 1, 16) if rows % t == 0 and 4 * t * cols <= ELEMENTWISE_TILE_BYTES]
    return max(fits)


def _rs_add(g, r1, c_idx, name):
    _, rows, cols = g.shape

    def body(c_ref, g_ref, r_ref, o_ref):
        o_ref[...] = (g_ref[...].astype(F32) + r_ref[...].astype(F32)).astype(o_ref.dtype)

    return pl.pallas_call(
        body, name=name,
        grid_spec=pltpu.PrefetchScalarGridSpec(
            num_scalar_prefetch=1, grid=(4,),
            in_specs=[pl.BlockSpec((1, rows, cols), lambda q, c: (4 * c[0] + q, 0, 0)),
                      pl.BlockSpec((1, rows, cols), lambda q, c: (q, 0, 0))],
            out_specs=pl.BlockSpec((1, rows, cols), lambda q, c: (q, 0, 0))),
        out_shape=SDS((4, rows, cols), g.dtype),
        compiler_params=_cp("parallel"),
    )(c_idx, g, r1)


def _adam_math(w, g, m, v):
    m = B1 * m + (1.0 - B1) * g
    v = B2 * v + (1.0 - B2) * (g * g)
    m_hat = m / (1.0 - B1 ** STEP)
    v_hat = v / (1.0 - B2 ** STEP)
    delta = -LR * (m_hat / (jnp.sqrt(v_hat) + ADAM_EPS) + WD * w)
    return delta, m, v


def _adamw_big(w, m, v, t, r2, q_idx, name, part=0, prev=None):
    _, rows, cols = t.shape
    tr = _row_tile(rows, cols)
    nblk = rows // tr

    def body(q_ref, w_ref, m_ref, v_ref, t_ref, r_ref, *outs):
        g_out, d_out, m_out, v_out = outs[-4:]
        g = t_ref[0].astype(F32)
        for k in range(3):
            g = g + r_ref[k].astype(F32)
        d, mn, vn = _adam_math(w_ref[...], g, m_ref[...], v_ref[...])
        g_out[...], d_out[...], m_out[...], v_out[...] = g, d, mn, vn

    blk = pl.BlockSpec((tr, cols), lambda i, q: (part * nblk + i, 0))
    specs = [blk, blk, blk, pl.BlockSpec((1, tr, cols), lambda i, q: (q[0], i, 0)),
             pl.BlockSpec((3, tr, cols), lambda i, q: (0, i, 0))]
    ins = [q_idx, w, m, v, t, r2]
    alias = {}
    if prev is not None:
        specs += [ANY] * 4
        alias = {6 + k: k for k in range(4)}
        ins += list(prev)
    return pl.pallas_call(
        body, name=name,
        grid_spec=pltpu.PrefetchScalarGridSpec(num_scalar_prefetch=1, grid=(nblk,), in_specs=specs, out_specs=[blk] * 4),
        out_shape=[SDS(w.shape, F32)] * 4, input_output_aliases=alias,
        compiler_params=_cp("parallel"))(*ins)


def _sum_devices(g, name):
    _, rows, cols = g.shape

    def body(g_ref, o_ref):
        acc = g_ref[0]
        for k in range(1, ND):
            acc = acc + g_ref[k]
        o_ref[...] = acc

    return pl.pallas_call(body, name=name, out_shape=SDS((rows, cols), F32))(g)


def _adamw_small(w, g, m, v, name):
    def body(w_ref, g_ref, m_ref, v_ref, d_out, m_out, v_out):
        d, mn, vn = _adam_math(w_ref[...], g_ref[...], m_ref[...], v_ref[...])
        d_out[...], m_out[...], v_out[...] = d, mn, vn

    return pl.pallas_call(body, name=name, out_shape=[SDS(w.shape, F32)] * 3)(w, g, m, v)


ROWS = 256


def _rms_stats(x):
    r = lax.rsqrt(jnp.mean(x * x, axis=-1, keepdims=True) + RMS_EPS)
    return x * r, r


def _rms_fwd(x, gains, name, jobs=()):
    s, d = x.shape
    n = len(gains)

    def body(x_ref, *refs):
        xh, _ = _rms_stats(x_ref[...])
        for g_ref, o_ref in zip(refs[:n], refs[n:]):
            o_ref[...] = (xh * g_ref[...]).astype(BF16)

    row = pl.BlockSpec((ROWS, d), lambda i: (i, 0))
    vec = pl.BlockSpec((1, d), lambda i: (0, 0))
    return _pc(body, name=name, grid=(s // ROWS,), in_specs=[row] + [vec] * n, out_specs=[row] * n,
               out_shape=[SDS((s, d), BF16)] * n, sem=("parallel",), args=(x, *gains), jobs=jobs)


def _rms_bwd_rows(xh, r, gain, dy):
    u = dy * gain
    return r * (u - xh * jnp.mean(u * xh, axis=-1, keepdims=True))


def _rms_bwd(x, pairs, dres, name, colsum=False):
    s, d = x.shape
    n = len(pairs)

    def body(x_ref, dres_ref, *refs):
        g_refs, dy_refs = refs[:n], refs[n:2 * n]
        dx_ref, dxb_ref = refs[2 * n], refs[2 * n + 1]
        dg_refs = refs[2 * n + 2:2 * n + 2 + n]
        cs_ref = refs[-1] if colsum else None
        first = pl.program_id(0) == 0
        xh, r = _rms_stats(x_ref[...])
        dx = dres_ref[...]
        for g_ref, dy_ref, dg_ref in zip(g_refs, dy_refs, dg_refs):
            dy = dy_ref[...]
            dx = dx + _rms_bwd_rows(xh, r, g_ref[...], dy)

            @pl.when(first)
            def _():
                dg_ref[...] = jnp.zeros_like(dg_ref)
            dg_ref[...] += jnp.sum(dy * xh, axis=0, keepdims=True)
        dx_ref[...] = dx
        dxb_ref[...] = dx.astype(BF16)
        if colsum:
            @pl.when(first)
            def _():
                cs_ref[...] = jnp.zeros_like(cs_ref)
            cs_ref[...] += jnp.sum(dx, axis=0, keepdims=True)

    row = pl.BlockSpec((ROWS, d), lambda i: (i, 0))
    vec = pl.BlockSpec((1, d), lambda i: (0, 0))
    nvec = n + (1 if colsum else 0)
    outs = pl.pallas_call(
        body, name=name, grid=(s // ROWS,),
        in_specs=[row, row] + [vec] * n + [row] * n,
        out_specs=[row, row] + [vec] * nvec,
        out_shape=[SDS((s, d), F32), SDS((s, d), BF16)] + [SDS((1, d), F32)] * nvec,
        compiler_params=_cp("arbitrary"),
    )(x, dres, *[p[0] for p in pairs], *[p[1] for p in pairs])
    return outs


def _final_loss(h, target, gain):
    s, d = h.shape

    def body(h_ref, t_ref, g_ref, dh_ref, dhb_ref, dg_ref, loss_ref):
        first = pl.program_id(0) == 0
        xh, r = _rms_stats(h_ref[...])
        gain_v = g_ref[...]
        e = xh * gain_v - t_ref[...]
        dy = e * (1.0 / d)
        dx = _rms_bwd_rows(xh, r, gain_v, dy)
        dh_ref[...] = dx
        dhb_ref[...] = dx.astype(BF16)

        @pl.when(first)
        def _():
            dg_ref[...] = jnp.zeros_like(dg_ref)
            loss_ref[...] = jnp.zeros_like(loss_ref)
        dg_ref[...] += jnp.sum(dy * xh, axis=0, keepdims=True)
        loss_ref[...] += jnp.full((1, 128), 0.5 / d, F32) * jnp.sum(e * e)

    row = pl.BlockSpec((ROWS, d), lambda i: (i, 0))
    vec = pl.BlockSpec((1, d), lambda i: (0, 0))
    return pl.pallas_call(
        body, name="final_loss", grid=(s // ROWS,),
        in_specs=[row, row, vec], out_specs=[row, row, vec, pl.BlockSpec((1, 128), lambda i: (0, 0))],
        out_shape=[SDS((s, d), F32), SDS((s, d), BF16), SDS((1, d), F32), SDS((1, 128), F32)],
        compiler_params=_cp("arbitrary"))(h, target, gain)


CT = 128


def _ln_stats(cv):
    mu = jnp.mean(cv, axis=-1, keepdims=True)
    xc = cv - mu
    rstd = lax.rsqrt(jnp.mean(xc * xc, axis=-1, keepdims=True) + LN_EPS)
    return xc * rstd, rstd


def _conv_fwd(glu, dw, dwb, lng, lnb, jobs=()):
    s, d = glu.shape
    hb = CT // CONV_PAD

    def body(x_ref, halo_ref, dw_ref, dwb_ref, lng_ref, lnb_ref, c_ref, s_ref):
        keep = (pl.program_id(0) > 0).astype(F32)

        def chunk(ci, carry):
            ls = pl.ds(pl.multiple_of(ci * 128, 128), 128)
            xf = jnp.concatenate([halo_ref[:, ls] * keep, x_ref[:, ls]], axis=0)
            acc = jnp.zeros((CT, 128), F32)
            for k in range(CONV_W):
                sh = CONV_W - 1 - k
                xs = pltpu.roll(xf, sh, 0) if sh else xf
                acc = acc + dw_ref[pl.ds(k, 1), ls] * xs[CONV_PAD:]
            c_ref[:, ls] = acc + dwb_ref[:, ls]
            return carry

        lax.fori_loop(0, d // 128, chunk, 0)
        xh, _ = _ln_stats(c_ref[...])
        yv = xh * lng_ref[...] + lnb_ref[...]
        s_ref[...] = (yv * _sigmoid(yv)).astype(BF16)

    row = pl.BlockSpec((CT, d), lambda i: (i, 0))
    halo = pl.BlockSpec((CONV_PAD, d), lambda i: (jnp.maximum(i * hb - 1, 0), 0))
    vec = pl.BlockSpec((1, d), lambda i: (0, 0))
    taps = pl.BlockSpec((CONV_PAD, d), lambda i: (0, 0))
    return _pc(
        body, name="conv_fwd", grid=(s // CT,),
        in_specs=[row, halo, taps, vec, vec, vec], out_specs=[row, row],
        out_shape=[SDS((s, d), F32), SDS((s, d), BF16)], sem=("parallel",),
        args=(glu, glu, dw, dwb, lng, lnb), jobs=jobs)


def _ln_bwd(ds, cv, lng, lnb):
    s, d = cv.shape

    def body(ds_ref, c_ref, g_ref, b_ref, dc_ref, dg_ref, db_ref):
        first = pl.program_id(0) == 0
        xh, rstd = _ln_stats(c_ref[...])
        gv = g_ref[...]
        yv = xh * gv + b_ref[...]
        sg = _sigmoid(yv)
        dln = ds_ref[...] * (sg * (1.0 + yv * (1.0 - sg)))
        dxh = dln * gv
        dc_ref[...] = rstd * (dxh - jnp.mean(dxh, axis=-1, keepdims=True)
                              - xh * jnp.mean(dxh * xh, axis=-1, keepdims=True))

        @pl.when(first)
        def _():
            dg_ref[...] = jnp.zeros_like(dg_ref)
            db_ref[...] = jnp.zeros_like(db_ref)
        dg_ref[...] += jnp.sum(dln * xh, axis=0, keepdims=True)
        db_ref[...] += jnp.sum(dln, axis=0, keepdims=True)

    row = pl.BlockSpec((ROWS, d), lambda i: (i, 0))
    vec = pl.BlockSpec((1, d), lambda i: (0, 0))
    return pl.pallas_call(
        body, name="ln_bwd", grid=(s // ROWS,), in_specs=[row, row, vec, vec], out_specs=[row, vec, vec],
        out_shape=[SDS((s, d), F32), SDS((1, d), F32), SDS((1, d), F32)],
        compiler_params=_cp("arbitrary"))(ds, cv, lng, lnb)


def _conv_bwd(dc, glu, ua, ug, dw, jobs=()):
    s, d = dc.shape
    hb = CT // CONV_PAD
    nsteps = s // CT
    full = CT + CONV_PAD

    def body(dc_ref, dcn_ref, x_ref, xp_ref, ua_ref, ug_ref, dw_ref, du_ref, ddw_ref, ddwb_ref, db1_ref):
        i = pl.program_id(0)
        keep_prev = (i > 0).astype(F32)
        keep_next = (i < nsteps - 1).astype(F32)

        @pl.when(i == 0)
        def _():
            ddw_ref[...] = jnp.zeros_like(ddw_ref)
            ddwb_ref[...] = jnp.zeros_like(ddwb_ref)
            db1_ref[...] = jnp.zeros_like(db1_ref)

        def chunk(ci, carry):
            off = pl.multiple_of(ci * 128, 128)
            ls = pl.ds(off, 128)
            ls2 = pl.ds(pl.multiple_of(d + ci * 128, 128), 128)
            dcc = dc_ref[:, ls]
            dcf = jnp.concatenate([dcc, dcn_ref[:, ls] * keep_next], axis=0)
            xf = jnp.concatenate([xp_ref[:, ls] * keep_prev, x_ref[:, ls]], axis=0)
            dglu = jnp.zeros((CT, 128), F32)
            for k in range(CONV_W):
                sh = CONV_W - 1 - k
                dshift = pltpu.roll(dcf, full - sh, 0) if sh else dcf
                dglu = dglu + dw_ref[pl.ds(k, 1), ls] * dshift[:CT]
                xs = pltpu.roll(xf, sh, 0) if sh else xf
                ddw_ref[pl.ds(k, 1), ls] += jnp.sum(dcc * xs[CONV_PAD:], axis=0, keepdims=True)
            ddwb_ref[:, ls] += jnp.sum(dcc, axis=0, keepdims=True)
            av, gv = ua_ref[:, ls], ug_ref[:, ls]
            sg = _sigmoid(gv)
            da = dglu * sg
            dgt = dglu * av * sg * (1.0 - sg)
            du_ref[:, ls] = da.astype(BF16)
            du_ref[:, ls2] = dgt.astype(BF16)
            db1_ref[:, ls] += jnp.sum(da, axis=0, keepdims=True)
            db1_ref[:, ls2] += jnp.sum(dgt, axis=0, keepdims=True)
            return carry

        lax.fori_loop(0, d // 128, chunk, 0)

    row = pl.BlockSpec((CT, d), lambda i: (i, 0))
    prev = pl.BlockSpec((CONV_PAD, d), lambda i: (jnp.maximum(i * hb - 1, 0), 0))
    nxt = pl.BlockSpec((CONV_PAD, d), lambda i: (jnp.minimum((i + 1) * hb, s // CONV_PAD - 1), 0))
    taps = pl.BlockSpec((CONV_PAD, d), lambda i: (0, 0))
    return _pc(
        body, name="conv_bwd", grid=(nsteps,),
        in_specs=[row, nxt, row, prev, row, row, taps],
        out_specs=[pl.BlockSpec((CT, 2 * d), lambda i: (i, 0)), taps, pl.BlockSpec((1, d), lambda i: (0, 0)),
                   pl.BlockSpec((1, 2 * d), lambda i: (0, 0))],
        out_shape=[SDS((s, 2 * d), BF16), SDS((CONV_PAD, d), F32), SDS((1, d), F32), SDS((1, 2 * d), F32)],
        sem=("arbitrary",), args=(dc, dc, glu, glu, ua, ug, dw), jobs=jobs)


TM = 1024
TS = 1024


def _glu_mm(n1, w1g, b1, jobs=()):
    s, d = n1.shape
    cw = w1g.shape[2]
    half = ND // 2

    def body(a_ref, wa_ref, wg_ref, ba_ref, bg_ref, ua_ref, ug_ref, glu_ref):
        a = a_ref[...]
        ua = _dot(a, wa_ref[0], NN) + ba_ref[...]
        ug = _dot(a, wg_ref[0], NN) + bg_ref[...]
        ua_ref[...], ug_ref[...] = ua, ug
        glu_ref[...] = ua * _sigmoid(ug)

    out = pl.BlockSpec((TM, cw), lambda m, i: (m, i))
    return _pc(
        body, name="glu_mm", grid=(s // TM, half),
        in_specs=[pl.BlockSpec((TM, d), lambda m, i: (m, 0)),
                  pl.BlockSpec((1, d, cw), lambda m, i: (i, 0, 0)),
                  pl.BlockSpec((1, d, cw), lambda m, i: (i + half, 0, 0)),
                  pl.BlockSpec((1, cw), lambda m, i: (0, i)),
                  pl.BlockSpec((1, cw), lambda m, i: (0, i + half))],
        out_specs=[out, out, out], out_shape=[SDS((s, d), F32)] * 3,
        sem=("parallel", "arbitrary"), args=(n1, w1g, w1g, b1, b1), jobs=jobs)


def _mm_rows(a, wg, name, res=None, bias=None, out_dtype=F32, tn=512, branches=False, jobs=()):
    s, kdim = a.shape
    _, kc, n = wg.shape
    assert kc * ND == kdim
    nx = 2 + (res is not None) + (bias is not None)

    def body(*refs):
        acc = _dot(refs[0][...], refs[1][...].reshape(kdim, tn), NN)
        for extra in refs[2:nx]:
            acc = acc + extra[...]
        refs[nx][...] = acc.astype(out_dtype)
        if branches:
            scr = refs[-1]
            _stage(scr, acc)
            for o_ref, dil in zip(refs[nx + 1:], SPLIT_DILATIONS):
                _split_rows(scr, o_ref, dil)

    ins, specs = [a, wg], [pl.BlockSpec((TM, kdim), lambda m, j: (m, 0)), pl.BlockSpec((ND, kc, tn), lambda m, j: (0, 0, j))]
    if res is not None:
        ins.append(res)
        specs.append(pl.BlockSpec((TM, tn), lambda m, j: (m, j)))
    if bias is not None:
        ins.append(bias)
        specs.append(pl.BlockSpec((1, tn), lambda m, j: (0, j)))
    out_specs, out_shape, scratch = [pl.BlockSpec((TM, tn), lambda m, j: (m, j))], [SDS((s, n), out_dtype)], []
    if branches:
        out_specs += _branch_specs(TM, tn, lambda dil, m, j: (0, m, j))
        out_shape += [SDS((dil, s // dil, n), out_dtype) for dil in SPLIT_DILATIONS]
        scratch = [pltpu.VMEM((tn // 128, TM, 128), F32)]
    outs = _pc(body, name=name, grid=(s // TM, n // tn), in_specs=specs, out_specs=out_specs, out_shape=out_shape,
               scratch=scratch, sem=("parallel", "arbitrary"), args=ins, jobs=jobs)
    return [outs[0][None]] + outs[1:] if branches else outs[0]


def _swiglu_mm(n2, wgg, wug, name, jobs=()):
    s, d = n2.shape
    fc = wgg.shape[2]

    def body(a_ref, wg_ref, wu_ref, g_ref, u_ref, act_ref):
        a = a_ref[...]
        g = _dot(a, wg_ref[0], NN)
        u = _dot(a, wu_ref[0], NN)
        g_ref[0], u_ref[0] = g.astype(BF16), u.astype(BF16)
        act_ref[0] = (g * _sigmoid(g) * u).astype(BF16)

    wspec = pl.BlockSpec((1, d, fc), lambda m, j: (j, 0, 0))
    out = pl.BlockSpec((1, TM, fc), lambda m, j: (j, m, 0))
    return _pc(
        body, name=name, grid=(s // TM, ND),
        in_specs=[pl.BlockSpec((TM, d), lambda m, j: (m, 0)), wspec, wspec],
        out_specs=[out, out, out], out_shape=[SDS((ND, s, fc), BF16)] * 3,
        sem=("parallel", "arbitrary"), args=(n2, wgg, wug), jobs=jobs)


def _down_mm(act, wdg, res, name, jobs=()):
    _, s, fc = act.shape
    d = wdg.shape[2]

    def body(a_ref, w_ref, r_ref, o_ref):
        @pl.when(pl.program_id(1) == 0)
        def _():
            o_ref[...] = r_ref[...]
        o_ref[...] += _dot(a_ref[0], w_ref[0], NN)

    row = pl.BlockSpec((TM, d), lambda m, j: (m, 0))
    return _pc(
        body, name=name, grid=(s // TM, ND),
        in_specs=[pl.BlockSpec((1, TM, fc), lambda m, j: (j, m, 0)),
                  pl.BlockSpec((1, fc, d), lambda m, j: (j, 0, 0)), row],
        out_specs=[row], out_shape=[SDS((s, d), F32)],
        sem=("parallel", "arbitrary"), args=(act, wdg, res), jobs=jobs)[0]


def _dact_mm(dh, wdg, gate, up, name, jobs=()):
    s, d = dh.shape
    fc = wdg.shape[1]

    def body(a_ref, w_ref, g_ref, u_ref, dg_ref, du_ref):
        dact = _dot(a_ref[...], w_ref[0], NT)
        g, u = g_ref[0].astype(F32), u_ref[0].astype(F32)
        sg = _sigmoid(g)
        du_ref[0] = (dact * (g * sg)).astype(BF16)
        dg_ref[0] = (dact * u * (sg * (1.0 + g * (1.0 - sg)))).astype(BF16)

    blk = pl.BlockSpec((1, TM, fc), lambda m, j: (j, m, 0))
    return _pc(
        body, name=name, grid=(s // TM, ND),
        in_specs=[pl.BlockSpec((TM, d), lambda m, j: (m, 0)),
                  pl.BlockSpec((1, fc, d), lambda m, j: (j, 0, 0)), blk, blk],
        out_specs=[blk, blk], out_shape=[SDS((ND, s, fc), BF16)] * 2,
        sem=("parallel", "arbitrary"), args=(dh, wdg, gate, up), jobs=jobs)


def _dwd_mm(act, dh, name, jobs=()):
    _, s, fc = act.shape
    d = dh.shape[1]
    nk = s // TS

    def body(a_ref, b_ref, o_ref, acc):
        k = pl.program_id(1)

        @pl.when(k == 0)
        def _():
            acc[...] = jnp.zeros_like(acc)
        acc[...] += _dot(a_ref[0], b_ref[...], TN)

        @pl.when(k == nk - 1)
        def _():
            o_ref[0] = acc[...].astype(BF16)

    return _pc(
        body, name=name, grid=(ND, nk),
        in_specs=[pl.BlockSpec((1, TS, fc), lambda j, k: (j, k, 0)), pl.BlockSpec((TS, d), lambda j, k: (k, 0))],
        out_specs=[pl.BlockSpec((1, fc, d), lambda j, k: (_slot(j), 0, 0))],
        out_shape=[SDS((ND, fc, d), BF16)], scratch=[pltpu.VMEM((fc, d), F32)],
        sem=("parallel", "arbitrary"), args=(act, dh), jobs=jobs)[0]


def _dwgu_mm(n2, dgate, dup, name, jobs=()):
    s, d = n2.shape
    fc = dgate.shape[2]
    nk = s // TS

    def body(a_ref, g_ref, u_ref, og_ref, ou_ref, accg, accu):
        k = pl.program_id(1)

        @pl.when(k == 0)
        def _():
            accg[...] = jnp.zeros_like(accg)
            accu[...] = jnp.zeros_like(accu)
        a = a_ref[...]
        accg[...] += _dot(a, g_ref[0], TN)
        accu[...] += _dot(a, u_ref[0], TN)

        @pl.when(k == nk - 1)
        def _():
            og_ref[0] = accg[...].astype(BF16)
            ou_ref[0] = accu[...].astype(BF16)

    blk = pl.BlockSpec((1, TS, fc), lambda j, k: (j, k, 0))
    out = pl.BlockSpec((1, d, fc), lambda j, k: (_slot(j), 0, 0))
    return _pc(
        body, name=name, grid=(ND, nk),
        in_specs=[pl.BlockSpec((TS, d), lambda j, k: (k, 0)), blk, blk], out_specs=[out, out],
        out_shape=[SDS((ND, d, fc), BF16)] * 2,
        scratch=[pltpu.VMEM((d, fc), F32), pltpu.VMEM((d, fc), F32)],
        sem=("parallel", "arbitrary"), args=(n2, dgate, dup), jobs=jobs)


def _dn_ffn_mm(dgate, dup, wgg, wug, name, jobs=()):
    _, s, fc = dgate.shape
    d = wgg.shape[1]

    def body(g_ref, u_ref, wg_ref, wu_ref, o_ref):
        j = pl.program_id(1)

        @pl.when(j == 0)
        def _():
            o_ref[...] = jnp.zeros_like(o_ref)
        o_ref[...] += _dot(g_ref[0], wg_ref[0], NT) + _dot(u_ref[0], wu_ref[0], NT)

    blk = pl.BlockSpec((1, TM, fc), lambda m, j: (j, m, 0))
    wspec = pl.BlockSpec((1, d, fc), lambda m, j: (j, 0, 0))
    return _pc(
        body, name=name, grid=(s // TM, ND), in_specs=[blk, blk, wspec, wspec],
        out_specs=[pl.BlockSpec((TM, d), lambda m, j: (m, 0))], out_shape=[SDS((s, d), F32)],
        sem=("parallel", "arbitrary"), args=(dgate, dup, wgg, wug), jobs=jobs)[0]


def _mm_rows_t(pairs, name, out_dtype, branches=False, jobs=()):
    s, n = pairs[0][0].shape
    _, kc, _ = pairs[0][1].shape
    np_ = len(pairs)
    grp = ND // 2
    wide = grp * kc

    def body(*refs):
        o_ref = refs[2 * np_]
        for i in range(grp):
            acc = None
            for p in range(np_):
                t = _dot(refs[p][...], refs[np_ + p][i], NT)
                acc = t if acc is None else acc + t
            o_ref[:, kc * i:kc * (i + 1)] = acc.astype(out_dtype)
            if branches:
                for c, ls in enumerate(_lane_chunks(kc)):
                    refs[-1][i * (kc // 128) + c] = acc[:, ls]
        if branches:
            for b_ref, dil in zip(refs[2 * np_ + 1:], SPLIT_DILATIONS):
                _split_rows(refs[-1], b_ref, dil)

    out_specs, out_shape, scratch = [pl.BlockSpec((TM, wide), lambda m, j: (m, j))], [SDS((s, kc * ND), out_dtype)], []
    if branches:
        out_specs += _branch_specs(TM, wide, lambda dil, m, j: (0, m, j))
        out_shape += [SDS((dil, s // dil, kc * ND), out_dtype) for dil in SPLIT_DILATIONS]
        scratch = [pltpu.VMEM((wide // 128, TM, 128), F32)]
    outs = _pc(
        body, name=name, grid=(s // TM, ND // grp),
        in_specs=[pl.BlockSpec((TM, n), lambda m, j: (m, 0))] * np_ + [pl.BlockSpec((grp, kc, n), lambda m, j: (j, 0, 0))] * np_,
        out_specs=out_specs, out_shape=out_shape, scratch=scratch,
        sem=("parallel", "arbitrary"), args=[p[0] for p in pairs] + [p[1] for p in pairs], jobs=jobs)
    return [outs[0][None]] + outs[1:] if branches else outs[0]


def _dw_rows_mm(a, b, name):
    s, kdim = a.shape
    n = b.shape[1]
    kc = kdim // ND
    ts = TS // 2
    nk = s // ts

    def body(a_ref, b_ref, o_ref, acc):
        k = pl.program_id(0)

        @pl.when(k == 0)
        def _():
            acc[...] = jnp.zeros_like(acc)
        acc[...] += _dot(a_ref[...], b_ref[...], TN)

        @pl.when(k == nk - 1)
        def _():
            for dev in range(ND):
                o_ref[_slot(dev)] = acc[kc * dev:kc * (dev + 1), :].astype(BF16)

    return pl.pallas_call(
        body, name=name, grid=(nk,),
        in_specs=[pl.BlockSpec((ts, kdim), lambda k: (k, 0)), pl.BlockSpec((ts, n), lambda k: (k, 0))],
        out_specs=pl.BlockSpec((ND, kc, n), lambda k: (0, 0, 0)), out_shape=SDS((ND, kc, n), BF16),
        scratch_shapes=[pltpu.VMEM((kdim, n), F32)], compiler_params=_cp("arbitrary"))(a, b)


def _dw1_mm(n1, du, jobs=()):
    s, d = n1.shape
    cw = du.shape[1] // ND
    nk = s // TS

    def body(a_ref, b_ref, o_ref, acc):
        k = pl.program_id(1)

        @pl.when(k == 0)
        def _():
            acc[...] = jnp.zeros_like(acc)
        acc[...] += _dot(a_ref[...], b_ref[...], TN)

        @pl.when(k == nk - 1)
        def _():
            o_ref[0] = acc[...].astype(BF16)

    return _pc(
        body, name="dw1_mm", grid=(ND, nk),
        in_specs=[pl.BlockSpec((TS, d), lambda j, k: (k, 0)), pl.BlockSpec((TS, cw), lambda j, k: (k, j))],
        out_specs=[pl.BlockSpec((1, d, cw), lambda j, k: (_slot(j), 0, 0))], out_shape=[SDS((ND, d, cw), BF16)],
        scratch=[pltpu.VMEM((d, cw), F32)], sem=("parallel", "arbitrary"), args=(n1, du), jobs=jobs)[0]


def _dn1_mm(du, w1g, jobs=()):
    s = du.shape[0]
    _, d, cw = w1g.shape

    def body(a_ref, w_ref, o_ref):
        j = pl.program_id(1)

        @pl.when(j == 0)
        def _():
            o_ref[...] = jnp.zeros_like(o_ref)
        o_ref[...] += _dot(a_ref[...], w_ref[0], NT)

    return _pc(
        body, name="dn1_mm", grid=(s // TM, ND),
        in_specs=[pl.BlockSpec((TM, cw), lambda m, j: (m, j)), pl.BlockSpec((1, d, cw), lambda m, j: (j, 0, 0))],
        out_specs=[pl.BlockSpec((TM, d), lambda m, j: (m, 0))], out_shape=[SDS((s, d), F32)],
        sem=("parallel", "arbitrary"), args=(du, w1g), jobs=jobs)[0]


NEG = -1e30


def _slopes(heads):
    return [2.0 ** (-8.0 * (h + 1) / heads) for h in range(heads)]


def _band(has_prev):
    qi = lax.broadcasted_iota(jnp.int32, (BLK, 2 * BLK), 0)
    ki = lax.broadcasted_iota(jnp.int32, (BLK, 2 * BLK), 1)
    j = qi - ki + BLK
    ok = (j >= 0) & (j <= BLK) & (has_prev | (ki >= BLK))
    return j.astype(F32), ok


SPLIT_DILATIONS = tuple(dil for dil in BRANCH_DILATIONS if dil > 1)


def _lane_chunks(w):
    return [slice(128 * c, 128 * (c + 1)) for c in range(w // 128)]


def _stage(scr, tile):
    for c, ls in enumerate(_lane_chunks(tile.shape[1])):
        scr[c] = tile[:, ls]


def _split_rows(scr, o_ref, dil):
    _, n, w = o_ref.shape
    for r in range(dil):
        for c, ls in enumerate(_lane_chunks(w)):
            o_ref[r, :, ls] = scr[c, pl.ds(r, n, stride=dil), :].astype(o_ref.dtype)


def _join_rows(i_ref, scr, dil):
    _, n, w = i_ref.shape
    for r in range(dil):
        for c, ls in enumerate(_lane_chunks(w)):
            scr[c, pl.ds(r, n, stride=dil), :] = i_ref[r, :, ls].astype(F32)


def _unstage(scr, w):
    return jnp.concatenate([scr[c] for c in range(w // 128)], axis=1)


def _branch_specs(rows, w, index):
    return [pl.BlockSpec((dil, rows // dil, w), functools.partial(index, dil)) for dil in SPLIT_DILATIONS]


def _attn_fwd(q, k, v, dil, jobs=()):
    _, l, d = q.shape
    heads = d // HEAD
    assert heads <= HEAD
    scale = HEAD ** -0.5
    slopes = _slopes(heads)

    def body(q_ref, kc_ref, kp_ref, vc_ref, vp_ref, o_ref, lse_ref):
        jf, ok = _band(pl.program_id(1) > 0)
        lane = lax.broadcasted_iota(jnp.int32, (BLK, HEAD), 1)
        lse = jnp.zeros((BLK, HEAD), F32)
        for h in range(heads):
            sl = slice(HEAD * h, HEAD * (h + 1))
            kh = jnp.concatenate([kp_ref[0, :, sl], kc_ref[0, :, sl]], axis=0)
            vh = jnp.concatenate([vp_ref[0, :, sl], vc_ref[0, :, sl]], axis=0)
            logits = jnp.where(ok, _dot(q_ref[0, :, sl], kh, NT) * scale + jf * (-slopes[h] * dil), NEG)
            m = jnp.max(logits, axis=-1, keepdims=True)
            p = jnp.exp(logits - m)
            den = jnp.sum(p, axis=-1, keepdims=True)
            o_ref[0, :, sl] = _dot(p.astype(BF16), vh, NN) / den
            lse = jnp.where(lane == h, m + jnp.log(den), lse)
        lse_ref[0] = lse

    cur = pl.BlockSpec((1, BLK, d), lambda r, b: (r, b, 0))
    prev = pl.BlockSpec((1, BLK, d), lambda r, b: (r, jnp.maximum(b - 1, 0), 0))
    return _pc(
        body, name=f"attn_fwd_d{dil}", grid=(dil, l // BLK),
        in_specs=[cur, cur, prev, cur, prev], out_specs=[cur, pl.BlockSpec((1, BLK, HEAD), lambda r, b: (r, b, 0))],
        out_shape=[SDS((dil, l, d), F32), SDS((dil, l, HEAD), F32)], sem=("parallel", "arbitrary"),
        args=(q, k, k, v, v), jobs=jobs)


def _attn_merge(outs, lses):
    _, s, d = outs[0].shape
    heads = d // HEAD
    nb = len(outs)
    nsplit = nb - 1

    def body(*refs):
        o_refs, l_refs = refs[:nb], refs[nb:2 * nb]
        att_refs, lse_refs = refs[2 * nb:3 * nb], refs[3 * nb:4 * nb]
        scr_o, scr_l, scr_att = refs[4 * nb:4 * nb + nsplit], refs[4 * nb + nsplit:4 * nb + 2 * nsplit], refs[-1]
        ls = [l_refs[0][...]]
        for k, dil in enumerate(SPLIT_DILATIONS):
            _join_rows(o_refs[1 + k], scr_o[k], dil)
            _join_rows(l_refs[1 + k], scr_l[k], dil)
            ls.append(scr_l[k][0])
        m = functools.reduce(jnp.maximum, ls)
        ws = [jnp.exp(v - m) for v in ls]
        den = functools.reduce(jnp.add, ws)
        ws = [w / den for w in ws]
        lse_refs[0][...] = m + jnp.log(den)
        scr_l[0][0] = m + jnp.log(den)
        for h in range(heads):
            sl = slice(HEAD * h, HEAD * (h + 1))
            slab = ws[0][:, h:h + 1] * o_refs[0][:, sl]
            for k in range(nsplit):
                slab = slab + ws[1 + k][:, h:h + 1] * scr_o[k][h]
            att_refs[0][:, sl] = slab.astype(BF16)
            scr_att[h] = slab
        for k, dil in enumerate(SPLIT_DILATIONS):
            _split_rows(scr_att, att_refs[1 + k], dil)
            _split_rows(scr_l[0], lse_refs[1 + k], dil)

    def specs(w):
        return [pl.BlockSpec((ROWS, w), lambda i: (i, 0))] + _branch_specs(ROWS, w, lambda dil, i: (0, i, 0))

    def shapes(w, dt):
        return [SDS((s, w), dt)] + [SDS((dil, s // dil, w), dt) for dil in SPLIT_DILATIONS]

    wide, narrow = pltpu.VMEM((heads, ROWS, 128), F32), pltpu.VMEM((1, ROWS, 128), F32)
    res = pl.pallas_call(
        body, name="attn_merge", grid=(s // ROWS,), in_specs=specs(d) + specs(HEAD), out_specs=specs(d) + specs(HEAD),
        out_shape=shapes(d, BF16) + shapes(HEAD, F32),
        scratch_shapes=[wide] * nsplit + [narrow] * nsplit + [wide],
        compiler_params=_cp("parallel"))(outs[0].reshape(s, d), *outs[1:], lses[0].reshape(s, HEAD), *lses[1:])
    return list(res[:nb]), list(res[nb:])


def _attn_bwd(q, k, v, do, o, lse, dil, jobs=()):
    _, l, d = q.shape
    nb = l // BLK
    heads = d // HEAD
    scale = HEAD ** -0.5
    slopes = _slopes(heads)

    def body(q_ref, kc_ref, kp_ref, vc_ref, vp_ref, do_ref, o_ref, lse_ref, dq_ref, dk_ref, dv_ref, ck, cv):
        b = pl.program_id(1)

        @pl.when(b == 0)
        def _():
            ck[...] = jnp.zeros_like(ck)
            cv[...] = jnp.zeros_like(cv)

        @pl.when(b < nb)
        def _():
            jf, ok = _band(b > 0)
            for h in range(heads):
                sl = slice(HEAD * h, HEAD * (h + 1))
                qh, doh = q_ref[0, :, sl], do_ref[0, :, sl]
                kh = jnp.concatenate([kp_ref[0, :, sl], kc_ref[0, :, sl]], axis=0)
                vh = jnp.concatenate([vp_ref[0, :, sl], vc_ref[0, :, sl]], axis=0)
                lse_h = lse_ref[0, :, h:h + 1]
                delta = jnp.sum(doh.astype(F32) * o_ref[0, :, sl].astype(F32), axis=-1, keepdims=True)
                p = jnp.where(ok, jnp.exp(_dot(qh, kh, NT) * scale + jf * (-slopes[h] * dil) - lse_h), 0.0)
                ds = (p * (_dot(doh, vh, NT) - delta)).astype(BF16)
                dq_ref[0, :, sl] = (_dot(ds, kh, NN) * scale).astype(BF16)
                dk2 = _dot(ds, qh, TN) * scale
                dv2 = _dot(p.astype(BF16), doh, TN)
                dk_ref[0, :, sl] = (ck[:, sl] + dk2[:BLK]).astype(BF16)
                dv_ref[0, :, sl] = (cv[:, sl] + dv2[:BLK]).astype(BF16)
                ck[:, sl] = dk2[BLK:]
                cv[:, sl] = dv2[BLK:]

        @pl.when(b == nb)
        def _():
            dk_ref[0] = ck[...].astype(BF16)
            dv_ref[0] = cv[...].astype(BF16)

    cur = pl.BlockSpec((1, BLK, d), lambda r, b: (r, jnp.minimum(b, nb - 1), 0))
    prev = pl.BlockSpec((1, BLK, d), lambda r, b: (r, jnp.clip(b - 1, 0, nb - 1), 0))
    lse_spec = pl.BlockSpec((1, BLK, HEAD), lambda r, b: (r, jnp.minimum(b, nb - 1), 0))
    return _pc(
        body, name=f"attn_bwd_d{dil}", grid=(dil, nb + 1),
        in_specs=[cur, cur, prev, cur, prev, cur, cur, lse_spec], out_specs=[cur, prev, prev],
        out_shape=[SDS((dil, l, d), BF16)] * 3,
        scratch=[pltpu.VMEM((BLK, d), F32), pltpu.VMEM((BLK, d), F32)],
        sem=("parallel", "arbitrary"), args=(q, k, k, v, v, do, o, lse), jobs=jobs)


def _sum_cast(xs, name):
    _, s, d = xs[0].shape
    nsplit = len(xs) - 1

    def body(*refs):
        i_refs, o_ref, scr = refs[:nsplit + 1], refs[nsplit + 1], refs[nsplit + 2:]
        acc = i_refs[0][...].astype(F32)
        for k, dil in enumerate(SPLIT_DILATIONS):
            _join_rows(i_refs[1 + k], scr[k], dil)
            acc = acc + _unstage(scr[k], d)
        o_ref[...] = acc.astype(BF16)

    row = pl.BlockSpec((ROWS, d), lambda i: (i, 0))
    return pl.pallas_call(
        body, name=name, grid=(s // ROWS,), in_specs=[row] + _branch_specs(ROWS, d, lambda dil, i: (0, i, 0)),
        out_specs=row, out_shape=SDS((s, d), BF16),
        scratch_shapes=[pltpu.VMEM((d // 128, ROWS, 128), F32)] * nsplit,
        compiler_params=_cp("parallel"))(xs[0].reshape(s, d), *xs[1:])


def _pack_rows(vs, width):
    flat = jnp.concatenate([v.reshape(-1) for v in vs])
    spans, at = [], 0
    for v in vs:
        spans.append((at, v.size))
        at += v.size
    rows = -(-at // width)
    rows = -(-rows // 8) * 8
    flat = jnp.pad(flat, (0, rows * width - at))
    return flat.reshape(rows, width), spans


def kernel(x, a_norm_g, conv_w1, conv_b1, conv_dw, conv_dw_b, conv_ln_g, conv_ln_b, conv_w2, conv_b2, kv_norm_g, w_k, w_v, b_norm_g, w_q, w_o, ffn_norm_g, ffn_w_gate, ffn_w_up, ffn_w_down, final_norm_g, loss_target, m_a_norm_g, m_conv_w1, m_conv_b1, m_conv_dw, m_conv_dw_b, m_conv_ln_g, m_conv_ln_b, m_conv_w2, m_conv_b2, m_kv_norm_g, m_w_k, m_w_v, m_b_norm_g, m_w_q, m_w_o, m_ffn_norm_g, m_ffn_w_gate, m_ffn_w_up, m_ffn_w_down, m_final_norm_g, v_a_norm_g, v_conv_w1, v_conv_b1, v_conv_dw, v_conv_dw_b, v_conv_ln_g, v_conv_ln_b, v_conv_w2, v_conv_b2, v_kv_norm_g, v_w_k, v_w_v, v_b_norm_g, v_w_q, v_w_o, v_ffn_norm_g, v_ffn_w_gate, v_ffn_w_up, v_ffn_w_down, v_final_norm_g):
    s, d = x.shape[1], x.shape[2]
    dc = d // ND
    h0 = x[0]
    target = loss_target[0]
    xi, yi, ci = lax.axis_index("x"), lax.axis_index("y"), lax.axis_index("c")
    me = 4 * xi + 2 * yi + ci
    c_idx = jnp.reshape(ci, (1,)).astype(jnp.int32)
    q_idx = jnp.reshape(2 * xi + yi, (1,)).astype(jnp.int32)

    bf = lambda w: w.astype(BF16)
    small_shards = [a_norm_g, conv_b1, conv_dw, conv_dw_b, conv_ln_g, conv_ln_b, conv_b2]
    sp, sp_spans = _pack_rows(small_shards, dc)
    w1g, spg = _all_gather([bf(conv_w1[0]), sp], "gather_first")
    spg = spg.reshape(ND, -1)

    def small_full(i, rows):
        at, size = sp_spans[i]
        return spg[:, at:at + size].reshape(ND, rows, size // rows).transpose(1, 0, 2).reshape(rows, -1)

    a_g = small_full(0, 1)
    b1 = small_full(1, 1)
    dw = jnp.pad(small_full(2, CONV_W), ((0, CONV_PAD - CONV_W), (0, 0)))
    dwb, lng, lnb, b2 = small_full(3, 1), small_full(4, 1), small_full(5, 1), small_full(6, 1)
    kv_g, q_g, fin_g = kv_norm_g.reshape(1, d), b_norm_g.reshape(1, d), final_norm_g.reshape(1, d)
    f_g = [ffn_norm_g[0:1], ffn_norm_g[1:2]]

    def send(*shards):
        return _job_gather_send([bf(t) for t in shards])

    def forward(job):
        return _job_gather_forward(job.result)

    def send_half(w, part, first=None):
        return _job_gather_send_rows(bf(w), part, 2, None if first is None else first.result[0])

    s_w2 = send(conv_w2[0])
    (n1,) = _rms_fwd(h0, [a_g], "rms_a", jobs=[s_w2])
    f_w2, s_g0 = forward(s_w2), send(ffn_w_gate[0])
    ua, ug, glu = _glu_mm(n1, w1g, b1, jobs=[f_w2, s_g0])
    (w2g,) = f_w2.result
    f_g0, s_u0 = forward(s_g0), send(ffn_w_up[0])
    cv, sw = _conv_fwd(glu, dw, dwb, lng, lnb, jobs=[f_g0, s_u0])
    (wg0,) = f_g0.result
    f_u0 = forward(s_u0)
    h1 = _mm_rows(sw, w2g, "w2_mm", res=h0, bias=b2, jobs=[f_u0])
    (wu0,) = f_u0.result
    (n2a,) = _rms_fwd(h1, [f_g[0]], "rms_f0")
    s_mid = send(ffn_w_down[0], w_k, w_v)
    gate0, up0, act0 = _swiglu_mm(n2a, wg0, wu0, "swiglu_mm0", jobs=[s_mid])
    f_mid = forward(s_mid)
    _comm_call([f_mid], "forward_mid")
    wd0, wkg, wvg = f_mid.result
    s_qo = send(w_q[0], w_o[0])
    h2 = _down_mm(act0, wd0, h1, "down_mm0", jobs=[s_qo])
    kvn, qn = _rms_fwd(h2, [kv_g, q_g], "rms_kvq")
    f_qo, s_g1a = forward(s_qo), send_half(ffn_w_gate[1], 0)
    kk = _mm_rows(kvn, wkg, "k_mm", out_dtype=BF16, branches=True, jobs=[f_qo, s_g1a])
    wqg, wog = f_qo.result
    s_g1b = send_half(ffn_w_gate[1], 1, s_g1a)
    vv = _mm_rows(kvn, wvg, "v_mm", out_dtype=BF16, branches=True, jobs=[s_g1b])
    f_g1, s_u1a = forward(s_g1b), send_half(ffn_w_up[1], 0)
    qq = _mm_rows(qn, wqg, "q_mm", out_dtype=BF16, branches=True, jobs=[f_g1, s_u1a])
    (wg1,) = f_g1.result
    branch = {dil: (qq[i], kk[i], vv[i]) for i, dil in enumerate(BRANCH_DILATIONS)}
    s_u1b = send_half(ffn_w_up[1], 1, s_u1a)
    o1, l1 = _attn_fwd(*branch[1], 1, jobs=[s_u1b])
    f_u1, s_d1a = forward(s_u1b), send_half(ffn_w_down[1], 0)
    o4, l4 = _attn_fwd(*branch[4], 4, jobs=[f_u1, s_d1a])
    (wu1,) = f_u1.result
    s_d1b = send_half(ffn_w_down[1], 1, s_d1a)
    o16, l16 = _attn_fwd(*branch[16], 16, jobs=[s_d1b])
    atts, lses = _attn_merge([o1, o4, o16], [l1, l4, l16])
    att = atts[0]
    atts, lses = [att[None]] + atts[1:], [lses[0][None]] + lses[1:]
    f_d1 = forward(s_d1b)
    h3 = _mm_rows(att, wog, "wo_mm", res=h2, jobs=[f_d1])
    (wd1,) = f_d1.result
    (n2b,) = _rms_fwd(h3, [f_g[1]], "rms_f1")
    gate1, up1, act1 = _swiglu_mm(n2b, wg1, wu1, "swiglu_mm1")
    h4 = _down_mm(act1, wd1, h3, "down_mm1")

    flat = lambda g: g.reshape(ND, -1, g.shape[-1])
    chip_sums, cross = {}, {}

    def to_sibling(**grads):
        job = _job_scatter_sibling([flat(g) for g in grads.values()])
        job.names = list(grads)
        return job

    def add_up(job):
        for n, g, r in zip(job.names, job.ins, job.result):
            chip_sums[n] = _rs_add(g, r, c_idx, f"rs_add_{n}")

    def to_chips(*names):
        job = _job_scatter_cross([chip_sums[n] for n in names])
        job.names = names
        return job

    def landed(job):
        cross.update(zip(job.names, job.result))

    dh4, dh4b, d_fin, loss_row = _final_loss(h4, target, fin_g)
    dgate1, dup1 = _dact_mm(dh4b, wd1, gate1, up1, "dact_mm1")
    g_wd1 = _dwd_mm(act1, dh4b, "dwd_mm1")
    j1 = to_sibling(wd1=g_wd1)
    g_wg1, g_wu1 = _dwgu_mm(n2b, dgate1, dup1, "dwgu_mm1", jobs=[j1])
    add_up(j1)
    j2, j3 = to_chips("wd1"), to_sibling(wg1=g_wg1, wu1=g_wu1)
    dn2b = _dn_ffn_mm(dgate1, dup1, wg1, wu1, "dn_ffn_mm1", jobs=[j2, j3])
    landed(j2)
    add_up(j3)
    dh3, dh3b, d_f1 = _rms_bwd(h3, [(f_g[1], dn2b)], dh4, "rms_f1_bwd")
    g_wo = _dw_rows_mm(att, dh3b, "dwo_mm")
    j4 = to_sibling(wo=g_wo)
    datt = _mm_rows_t([(dh3b, wog)], "datt_mm", BF16, branches=True, jobs=[j4])
    add_up(j4)
    riders = {1: to_chips("wg1"), 4: to_chips("wu1"), 16: to_chips("wo")}
    dqs, dks, dvs = [], [], []
    for i, dil in enumerate(BRANCH_DILATIONS):
        qb, kb, vb = branch[dil]
        dq_b, dk_b, dv_b = _attn_bwd(qb, kb, vb, datt[i], atts[i], lses[i], dil, jobs=[riders[dil]])
        landed(riders[dil])
        dqs.append(dq_b)
        dks.append(dk_b)
        dvs.append(dv_b)
    dq, dk, dv = _sum_cast(dqs, "dq_sum"), _sum_cast(dks, "dk_sum"), _sum_cast(dvs, "dv_sum")
    g_wq = _dw_rows_mm(qn, dq, "dwq_mm")
    g_wk = _dw_rows_mm(kvn, dk, "dwk_mm")
    g_wv = _dw_rows_mm(kvn, dv, "dwv_mm")
    j5 = to_sibling(wq=g_wq, wk=g_wk, wv=g_wv)
    dqn = _mm_rows_t([(dq, wqg)], "dqn_mm", F32, jobs=[j5])
    add_up(j5)
    j6 = to_chips("wq", "wk")
    dkvn = _mm_rows_t([(dk, wkg), (dv, wvg)], "dkvn_mm", F32, jobs=[j6])
    landed(j6)
    dh2, dh2b, d_q, d_kv = _rms_bwd(h2, [(q_g, dqn), (kv_g, dkvn)], dh3, "rms_kvq_bwd")
    j7 = to_chips("wv")
    dgate0, dup0 = _dact_mm(dh2b, wd0, gate0, up0, "dact_mm0", jobs=[j7])
    landed(j7)
    g_wd0 = _dwd_mm(act0, dh2b, "dwd_mm0")
    j8 = to_sibling(wd0=g_wd0)
    g_wg0, g_wu0 = _dwgu_mm(n2a, dgate0, dup0, "dwgu_mm0", jobs=[j8])
    add_up(j8)
    j9, j10 = to_chips("wd0"), to_sibling(wg0=g_wg0, wu0=g_wu0)
    dn2a = _dn_ffn_mm(dgate0, dup0, wg0, wu0, "dn_ffn_mm0", jobs=[j9, j10])
    landed(j9)
    add_up(j10)
    dh1, dh1b, d_f0, d_b2 = _rms_bwd(h1, [(f_g[0], dn2a)], dh2, "rms_f0_bwd", colsum=True)
    g_w2 = _dw_rows_mm(sw, dh1b, "dw2_mm")
    j11 = to_sibling(w2=g_w2)
    dsw = _mm_rows_t([(dh1b, w2g)], "dsw_mm", F32, jobs=[j11])
    add_up(j11)
    dcv, d_lng, d_lnb = _ln_bwd(dsw, cv, lng, lnb)
    j12 = to_chips("wg0", "wu0")
    du, d_dw, d_dwb, d_b1 = _conv_bwd(dcv, glu, ua, ug, dw, jobs=[j12])
    landed(j12)
    j13 = to_chips("w2")
    g_w1 = _dw1_mm(n1, du, jobs=[j13])
    landed(j13)
    j14 = to_sibling(w1=g_w1)
    dn1 = _dn1_mm(du, w1g, jobs=[j14])
    add_up(j14)
    dx, _, d_a = _rms_bwd(h0, [(a_g, dn1)], dh1, "rms_a_bwd")

    small_g = [d_a, d_b1, d_dw[:CONV_W], d_dwb, d_lng, d_lnb, d_b2, d_kv, d_q, d_f0, d_f1, d_fin, loss_row]
    gp, gp_spans = _pack_rows(small_g, d)
    j15, j16 = to_chips("w1"), _job_gather_send([gp])
    _comm_call([j15, j16], "rs_w1_send_small")
    landed(j15)
    j17 = _job_gather_forward(j16.result)
    _comm_call([j17], "forward_small")
    (gpg,) = j17.result

    two = lambda t: t.reshape(-1, t.shape[-1])

    def adam(w, m, v, names, tag):
        res = None
        for part, n in enumerate(names):
            res = _adamw_big(two(w), two(m), two(v), chip_sums[n], cross[n], q_idx, f"adamw_{tag}{part}", part, res)
        return [t.reshape(w.shape) for t in res]

    big_out = [
        adam(conv_w1, m_conv_w1, v_conv_w1, ["w1"], "w1"), adam(conv_w2, m_conv_w2, v_conv_w2, ["w2"], "w2"),
        adam(w_k, m_w_k, v_w_k, ["wk"], "wk"), adam(w_v, m_w_v, v_w_v, ["wv"], "wv"),
        adam(w_q, m_w_q, v_w_q, ["wq"], "wq"), adam(w_o, m_w_o, v_w_o, ["wo"], "wo"),
        adam(ffn_w_gate, m_ffn_w_gate, v_ffn_w_gate, ["wg0", "wg1"], "wg"),
        adam(ffn_w_up, m_ffn_w_up, v_ffn_w_up, ["wu0", "wu1"], "wu"),
        adam(ffn_w_down, m_ffn_w_down, v_ffn_w_down, ["wd0", "wd1"], "wd")]

    gsum = _sum_devices(gpg, "sum_small_grads").reshape(-1)

    def gfull(i):
        at, size = gp_spans[i]
        return gsum[at:at + size]

    def shard_of(vec, rows):
        return lax.dynamic_slice_in_dim(vec.reshape(rows, -1), me * (vec.size // rows // ND), vec.size // rows // ND, axis=1)

    loss = gfull(12)[0]
    small_grads = [
        shard_of(gfull(0), 1), shard_of(gfull(1), 1), shard_of(gfull(2), CONV_W)[None], shard_of(gfull(3), 1),
        shard_of(gfull(4), 1), shard_of(gfull(5), 1), shard_of(gfull(6), 1),
        gfull(7), gfull(8)[None], jnp.stack([gfull(9), gfull(10)]), gfull(11)]
    small_w = [a_norm_g, conv_b1, conv_dw, conv_dw_b, conv_ln_g, conv_ln_b, conv_b2, kv_norm_g, b_norm_g, ffn_norm_g, final_norm_g]
    small_m = [m_a_norm_g, m_conv_b1, m_conv_dw, m_conv_dw_b, m_conv_ln_g, m_conv_ln_b, m_conv_b2, m_kv_norm_g, m_b_norm_g, m_ffn_norm_g, m_final_norm_g]
    small_v = [v_a_norm_g, v_conv_b1, v_conv_dw, v_conv_dw_b, v_conv_ln_g, v_conv_ln_b, v_conv_b2, v_kv_norm_g, v_b_norm_g, v_ffn_norm_g, v_final_norm_g]
    small_grads = [g.reshape(w.shape) for g, w in zip(small_grads, small_w)]
    wp, spans = _pack_rows(small_w, 128)
    gpk, _ = _pack_rows(small_grads, 128)
    mp, _ = _pack_rows(small_m, 128)
    vp, _ = _pack_rows(small_v, 128)
    dp, mnp, vnp = _adamw_small(wp, gpk, mp, vp, "adamw_small")

    def unpack(packed):
        flat = packed.reshape(-1)
        return [flat[at:at + size].reshape(w.shape) for (at, size), w in zip(spans, small_w)]

    small_out = list(zip(small_grads, unpack(dp), unpack(mnp), unpack(vnp)))

    order = ["a_norm_g", "conv_w1", "conv_b1", "conv_dw", "conv_dw_b", "conv_ln_g", "conv_ln_b", "conv_w2", "conv_b2",
             "kv_norm_g", "w_k", "w_v", "b_norm_g", "w_q", "w_o", "ffn_norm_g", "ffn_w_gate", "ffn_w_up", "ffn_w_down",
             "final_norm_g"]
    big_names = ["conv_w1", "conv_w2", "w_k", "w_v", "w_q", "w_o", "ffn_w_gate", "ffn_w_up", "ffn_w_down"]
    small_names = ["a_norm_g", "conv_b1", "conv_dw", "conv_dw_b", "conv_ln_g", "conv_ln_b", "conv_b2", "kv_norm_g",
                   "b_norm_g", "ffn_norm_g", "final_norm_g"]
    table = {n: big_out[i] for i, n in enumerate(big_names)}
    table.update({n: small_out[i] for i, n in enumerate(small_names)})
    result = [loss, dx[None]]
    for kind in range(4):
        result += [table[n][kind] for n in order]
    return tuple(result)
```

```python
import functools

import jax
import jax.numpy as jnp
from jax import lax
from jax.experimental import pallas as pl
from jax.experimental.pallas import tpu as pltpu

ND = 8
HEAD = 128
BLK = 128
BRANCH_DILATIONS = (1, 4, 16)
CONV_W = 31
CONV_PAD = 32
RMS_EPS = 1e-6
LN_EPS = 1e-5
LR, B1, B2, ADAM_EPS, WD, STEP = 0.001, 0.9, 0.999, 1e-08, 0.01, 10
VMEM_LIMIT = 56 * 1024 * 1024
RESIDENT_BYTES = 4 * 1024 * 1024

F32, BF16 = jnp.float32, jnp.bfloat16
SDS = jax.ShapeDtypeStruct
MESH = pl.DeviceIdType.MESH
ANY = pl.BlockSpec(memory_space=pl.ANY)

NN = (((1,), (0,)), ((), ()))
NT = (((1,), (1,)), ((), ()))
TN = (((0,), (0,)), ((), ()))


def _dot(a, b, dims):
    return lax.dot_general(a, b, dims, preferred_element_type=F32)


def _cp(*sem):
    return pltpu.CompilerParams(dimension_semantics=sem, vmem_limit_bytes=VMEM_LIMIT)


def _slot(dev):
    return 4 * (dev % 2) + dev // 2


def _sigmoid(v):
    return 1.0 / (1.0 + jnp.exp(-v))


class _Job:
    def __init__(self, ins, out_shapes, alias, nsem, nlocal, make):
        self.ins, self.out_shapes, self.alias = list(ins), list(out_shapes), dict(alias)
        self.nsem, self.nlocal, self.make = nsem, nlocal, make
        self.result = None


def _coords():
    return lax.axis_index("x"), lax.axis_index("y"), lax.axis_index("c")


def _remote(src, dst, send, recv, k, to):
    return pltpu.make_async_remote_copy(src_ref=src, dst_ref=dst, send_sem=send.at[k], recv_sem=recv.at[k],
                                        device_id=to, device_id_type=MESH)


def _job_gather_send(shards):
    n = len(shards)

    def make(ins, outs, send, recv, local):
        x, y, c = _coords()
        targets = [(x, y, 1 - c), (1 - x, y, c), (x, 1 - y, c), (1 - x, 1 - y, c)]
        cps = []
        for a in range(n):
            dst = outs[a].at[4 * x + 2 * y + c]
            cps.append(pltpu.make_async_copy(ins[a], dst, local.at[a]))
            cps += [_remote(ins[a], dst, send, recv, 4 * a + k, t) for k, t in enumerate(targets)]
        return cps

    return _Job(shards, [SDS((ND,) + s.shape, s.dtype) for s in shards], {}, 4 * n, n, make)


def _job_gather_send_rows(shard, part, nparts, prev=None):
    rows = shard.shape[0] // nparts

    def make(ins, outs, send, recv, local):
        x, y, c = _coords()
        targets = [(x, y, 1 - c), (1 - x, y, c), (x, 1 - y, c), (1 - x, 1 - y, c)]
        src = ins[0].at[pl.ds(part * rows, rows)]
        dst = outs[0].at[4 * x + 2 * y + c].at[pl.ds(part * rows, rows)]
        return [pltpu.make_async_copy(src, dst, local.at[0])] + [
            _remote(src, dst, send, recv, k, t) for k, t in enumerate(targets)]

    ins = [shard] if prev is None else [shard, prev]
    return _Job(ins, [SDS((ND,) + shard.shape, shard.dtype)], {} if prev is None else {1: 0}, 4, 1, make)


def _job_gather_forward(gathered):
    n = len(gathered)

    def make(ins, outs, send, recv, local):
        x, y, c = _coords()
        cps = []
        for a in range(n):
            for k, (px, py) in enumerate([(1 - x, y), (x, 1 - y), (1 - x, 1 - y)]):
                blk = outs[a].at[4 * px + 2 * py + c]
                cps.append(_remote(blk, blk, send, recv, 3 * a + k, (x, y, 1 - c)))
        return cps

    return _Job(gathered, [SDS(g.shape, g.dtype) for g in gathered], {i: i for i in range(n)}, 3 * n, 0, make)


def _job_scatter_sibling(grads):
    n = len(grads)

    def make(ins, outs, send, recv, local):
        x, y, c = _coords()
        return [_remote(ins[a].at[pl.ds(4 * (1 - c), 4)], outs[a], send, recv, a, (x, y, 1 - c)) for a in range(n)]

    return _Job(grads, [SDS((4,) + g.shape[1:], g.dtype) for g in grads], {}, n, 0, make)


def _job_scatter_cross(sums):
    n = len(sums)

    def make(ins, outs, send, recv, local):
        x, y, c = _coords()
        chips = [(1 - x, y), (x, 1 - y), (1 - x, 1 - y)]
        return [_remote(ins[a].at[2 * px + py], outs[a].at[k], send, recv, 3 * a + k, (px, py, c))
                for a in range(n) for k, (px, py) in enumerate(chips)]

    return _Job(sums, [SDS((3,) + t.shape[1:], t.dtype) for t in sums], {}, 3 * n, 0, make)


def _pc(body, *, name, grid, in_specs, out_specs, out_shape, args, scratch=(), sem=(), alias=None, jobs=()):
    jobs = list(jobs)
    n_in, n_out, n_scr = len(in_specs), len(out_shape), len(scratch)
    aliases = dict(alias or {})
    job_args, job_shapes, job_scratch = [], [], []
    for j in jobs:
        for src, dst in j.alias.items():
            aliases[n_in + len(job_args) + src] = n_out + len(job_shapes) + dst
        job_args += j.ins
        job_shapes += j.out_shapes
        job_scratch += [pltpu.SemaphoreType.DMA((j.nsem,)), pltpu.SemaphoreType.DMA((j.nsem,)),
                        pltpu.SemaphoreType.DMA((max(j.nlocal, 1),))]

    def wrapped(*refs):
        ins = refs[:n_in]
        p = n_in + len(job_args)
        outs = refs[p:p + n_out]
        p += n_out + len(job_shapes)
        scr = refs[p:p + n_scr]
        sems = refs[p + n_scr:]
        copies = []
        pi, po = n_in, n_in + len(job_args) + n_out
        for k, j in enumerate(jobs):
            copies += j.make(refs[pi:pi + len(j.ins)], refs[po:po + len(j.out_shapes)], *sems[3 * k:3 * k + 3])
            pi += len(j.ins)
            po += len(j.out_shapes)
        gridded = bool(copies) and bool(grid)
        if gridded:
            ids = [pl.program_id(i) for i in range(len(grid))]
            first = functools.reduce(jnp.logical_and, [i == 0 for i in ids])
            last = functools.reduce(jnp.logical_and, [i == g - 1 for i, g in zip(ids, grid)])

            @pl.when(first)
            def _():
                for cp in copies:
                    cp.start()
        else:
            for cp in copies:
                cp.start()
        body(*ins, *outs, *scr)
        if gridded:
            @pl.when(last)
            def _():
                for cp in copies:
                    cp.wait()
        else:
            for cp in copies:
                cp.wait()

    kw = dict(grid=grid) if grid else {}
    semantics = ["arbitrary"] * len(grid) if jobs else list(sem)
    res = pl.pallas_call(
        wrapped, name=name, in_specs=list(in_specs) + [ANY] * len(job_args),
        out_specs=list(out_specs) + [ANY] * len(job_shapes), out_shape=list(out_shape) + job_shapes,
        scratch_shapes=list(scratch) + job_scratch, input_output_aliases=aliases,
        compiler_params=_cp(*semantics), **kw)(*args, *job_args)
    p = n_out
    for j in jobs:
        j.result = list(res[p:p + len(j.out_shapes)])
        p += len(j.out_shapes)
    return list(res[:n_out])


def _comm_call(jobs, name):
    _pc(lambda: None, name=name, grid=(), in_specs=[], out_specs=[], out_shape=[], args=[], jobs=jobs)


def _all_gather(arrs, name):
    n = len(arrs)

    def body(*refs):
        ins, outs = refs[:n], refs[n:2 * n]
        send_sems, recv_sems, local_sems = refs[2 * n:]
        x, y, c = lax.axis_index("x"), lax.axis_index("y"), lax.axis_index("c")
        me, sib = (x, y, c), (x, y, 1 - c)
        chips = [(1 - x, y), (x, 1 - y), (1 - x, 1 - y)]

        def copy(a, k, block, to, src=None):
            dst = outs[a].at[4 * block[0] + 2 * block[1] + block[2]]
            return pltpu.make_async_remote_copy(
                src_ref=dst if src is None else src, dst_ref=dst,
                send_sem=send_sems.at[7 * a + k], recv_sem=recv_sems.at[7 * a + k],
                device_id=to, device_id_type=MESH)

        mine = [pltpu.make_async_copy(ins[a], outs[a].at[4 * x + 2 * y + c], local_sems.at[a]) for a in range(n)]
        for cp in mine:
            cp.start()
        first = []
        for a in range(n):
            first.append(copy(a, 0, me, sib, src=ins[a]))
            first += [copy(a, 1 + j, me, (*chip, c), src=ins[a]) for j, chip in enumerate(chips)]
        for cp in first:
            cp.start()
        passed = []
        for a in range(n):
            for j, chip in enumerate(chips):
                copy(a, 1 + j, (*chip, c), me).wait_recv()
                fwd = copy(a, 4 + j, (*chip, c), sib)
                fwd.start()
                passed.append(fwd)
        for a in range(n):
            copy(a, 0, sib, me).wait_recv()
            for j, chip in enumerate(chips):
                copy(a, 4 + j, (*chip, 1 - c), me).wait_recv()
        for cp in first + passed:
            cp.wait_send()
        for cp in mine:
            cp.wait()

    return pl.pallas_call(
        body, name=name,
        out_shape=[SDS((ND,) + a.shape, a.dtype) for a in arrs],
        in_specs=[ANY] * n, out_specs=[ANY] * n,
        scratch_shapes=[pltpu.SemaphoreType.DMA((7 * n,)), pltpu.SemaphoreType.DMA((7 * n,)),
                        pltpu.SemaphoreType.DMA((n,))],
    )(*arrs)


ELEMENTWISE_TILE_BYTES = 3 * 512 * 1024


def _row_tile(rows, cols):
    fits = [t for t in range(16, rows + 1, 16) if rows % t == 0 and 4 * t * cols <= ELEMENTWISE_TILE_BYTES]
    return max(fits)


def _rs_add(g, r1, c_idx, name):
    _, rows, cols = g.shape

    def body(c_ref, g_ref, r_ref, o_ref):
        o_ref[...] = (g_ref[...].astype(F32) + r_ref[...].astype(F32)).astype(o_ref.dtype)

    return pl.pallas_call(
        body, name=name,
        grid_spec=pltpu.PrefetchScalarGridSpec(
            num_scalar_prefetch=1, grid=(4,),
            in_specs=[pl.BlockSpec((1, rows, cols), lambda q, c: (4 * c[0] + q, 0, 0)),
                      pl.BlockSpec((1, rows, cols), lambda q, c: (q, 0, 0))],
            out_specs=pl.BlockSpec((1, rows, cols), lambda q, c: (q, 0, 0))),
        out_shape=SDS((4, rows, cols), g.dtype),
        compiler_params=_cp("parallel"),
    )(c_idx, g, r1)


def _adam_math(w, g, m, v):
    m = B1 * m + (1.0 - B1) * g
    v = B2 * v + (1.0 - B2) * (g * g)
    m_hat = m / (1.0 - B1 ** STEP)
    v_hat = v / (1.0 - B2 ** STEP)
    delta = -LR * (m_hat / (jnp.sqrt(v_hat) + ADAM_EPS) + WD * w)
    return delta, m, v


def _adamw_big(w, m, v, t, r2, q_idx, name, part=0, prev=None):
    _, rows, cols = t.shape
    tr = _row_tile(rows, cols)
    nblk = rows // tr

    def body(q_ref, w_ref, m_ref, v_ref, t_ref, r_ref, *outs):
        g_out, d_out, m_out, v_out = outs[-4:]
        g = t_ref[0].astype(F32)
        for k in range(3):
            g = g + r_ref[k].astype(F32)
        d, mn, vn = _adam_math(w_ref[...], g, m_ref[...], v_ref[...])
        g_out[...], d_out[...], m_out[...], v_out[...] = g, d, mn, vn

    blk = pl.BlockSpec((tr, cols), lambda i, q: (part * nblk + i, 0))
    specs = [blk, blk, blk, pl.BlockSpec((1, tr, cols), lambda i, q: (q[0], i, 0)),
             pl.BlockSpec((3, tr, cols), lambda i, q: (0, i, 0))]
    ins = [q_idx, w, m, v, t, r2]
    alias = {}
    if prev is not None:
        specs += [ANY] * 4
        alias = {6 + k: k for k in range(4)}
        ins += list(prev)
    return pl.pallas_call(
        body, name=name,
        grid_spec=pltpu.PrefetchScalarGridSpec(num_scalar_prefetch=1, grid=(nblk,), in_specs=specs, out_specs=[blk] * 4),
        out_shape=[SDS(w.shape, F32)] * 4, input_output_aliases=alias,
        compiler_params=_cp("parallel"))(*ins)


def _sum_devices(g, name):
    _, rows, cols = g.shape

    def body(g_ref, o_ref):
        acc = g_ref[0]
        for k in range(1, ND):
            acc = acc + g_ref[k]
        o_ref[...] = acc

    return pl.pallas_call(body, name=name, out_shape=SDS((rows, cols), F32))(g)


def _adamw_small(w, g, m, v, name):
    def body(w_ref, g_ref, m_ref, v_ref, d_out, m_out, v_out):
        d, mn, vn = _adam_math(w_ref[...], g_ref[...], m_ref[...], v_ref[...])
        d_out[...], m_out[...], v_out[...] = d, mn, vn

    return pl.pallas_call(body, name=name, out_shape=[SDS(w.shape, F32)] * 3)(w, g, m, v)


ROWS = 256


def _rms_stats(x):
    r = lax.rsqrt(jnp.mean(x * x, axis=-1, keepdims=True) + RMS_EPS)
    return x * r, r


def _rms_fwd(x, gains, name, jobs=()):
    s, d = x.shape
    n = len(gains)

    def body(x_ref, *refs):
        xh, _ = _rms_stats(x_ref[...])
        for g_ref, o_ref in zip(refs[:n], refs[n:]):
            o_ref[...] = (xh * g_ref[...]).astype(BF16)

    row = pl.BlockSpec((ROWS, d), lambda i: (i, 0))
    vec = pl.BlockSpec((1, d), lambda i: (0, 0))
    return _pc(body, name=name, grid=(s // ROWS,), in_specs=[row] + [vec] * n, out_specs=[row] * n,
               out_shape=[SDS((s, d), BF16)] * n, sem=("parallel",), args=(x, *gains), jobs=jobs)


def _rms_bwd_rows(xh, r, gain, dy):
    u = dy * gain
    return r * (u - xh * jnp.mean(u * xh, axis=-1, keepdims=True))


def _rms_bwd(x, pairs, dres, name, colsum=False):
    s, d = x.shape
    n = len(pairs)

    def body(x_ref, dres_ref, *refs):
        g_refs, dy_refs = refs[:n], refs[n:2 * n]
        dx_ref, dxb_ref = refs[2 * n], refs[2 * n + 1]
        dg_refs = refs[2 * n + 2:2 * n + 2 + n]
        cs_ref = refs[-1] if colsum else None
        first = pl.program_id(0) == 0
        xh, r = _rms_stats(x_ref[...])
        dx = dres_ref[...]
        for g_ref, dy_ref, dg_ref in zip(g_refs, dy_refs, dg_refs):
            dy = dy_ref[...]
            dx = dx + _rms_bwd_rows(xh, r, g_ref[...], dy)

            @pl.when(first)
            def _():
                dg_ref[...] = jnp.zeros_like(dg_ref)
            dg_ref[...] += jnp.sum(dy * xh, axis=0, keepdims=True)
        dx_ref[...] = dx
        dxb_ref[...] = dx.astype(BF16)
        if colsum:
            @pl.when(first)
            def _():
                cs_ref[...] = jnp.zeros_like(cs_ref)
            cs_ref[...] += jnp.sum(dx, axis=0, keepdims=True)

    row = pl.BlockSpec((ROWS, d), lambda i: (i, 0))
    vec = pl.BlockSpec((1, d), lambda i: (0, 0))
    nvec = n + (1 if colsum else 0)
    outs = pl.pallas_call(
        body, name=name, grid=(s // ROWS,),
        in_specs=[row, row] + [vec] * n + [row] * n,
        out_specs=[row, row] + [vec] * nvec,
        out_shape=[SDS((s, d), F32), SDS((s, d), BF16)] + [SDS((1, d), F32)] * nvec,
        compiler_params=_cp("arbitrary"),
    )(x, dres, *[p[0] for p in pairs], *[p[1] for p in pairs])
    return outs


def _final_loss(h, target, gain):
    s, d = h.shape

    def body(h_ref, t_ref, g_ref, dh_ref, dhb_ref, dg_ref, loss_ref):
        first = pl.program_id(0) == 0
        xh, r = _rms_stats(h_ref[...])
        gain_v = g_ref[...]
        e = xh * gain_v - t_ref[...]
        dy = e * (1.0 / d)
        dx = _rms_bwd_rows(xh, r, gain_v, dy)
        dh_ref[...] = dx
        dhb_ref[...] = dx.astype(BF16)

        @pl.when(first)
        def _():
            dg_ref[...] = jnp.zeros_like(dg_ref)
            loss_ref[...] = jnp.zeros_like(loss_ref)
        dg_ref[...] += jnp.sum(dy * xh, axis=0, keepdims=True)
        loss_ref[...] += jnp.full((1, 128), 0.5 / d, F32) * jnp.sum(e * e)

    row = pl.BlockSpec((ROWS, d), lambda i: (i, 0))
    vec = pl.BlockSpec((1, d), lambda i: (0, 0))
    return pl.pallas_call(
        body, name="final_loss", grid=(s // ROWS,),
        in_specs=[row, row, vec], out_specs=[row, row, vec, pl.BlockSpec((1, 128), lambda i: (0, 0))],
        out_shape=[SDS((s, d), F32), SDS((s, d), BF16), SDS((1, d), F32), SDS((1, 128), F32)],
        compiler_params=_cp("arbitrary"))(h, target, gain)


CT = 128


def _ln_stats(cv):
    mu = jnp.mean(cv, axis=-1, keepdims=True)
    xc = cv - mu
    rstd = lax.rsqrt(jnp.mean(xc * xc, axis=-1, keepdims=True) + LN_EPS)
    return xc * rstd, rstd


def _conv_fwd(glu, dw, dwb, lng, lnb, jobs=()):
    s, d = glu.shape
    hb = CT // CONV_PAD

    def body(x_ref, halo_ref, dw_ref, dwb_ref, lng_ref, lnb_ref, c_ref, s_ref):
        keep = (pl.program_id(0) > 0).astype(F32)

        def chunk(ci, carry):
            ls = pl.ds(pl.multiple_of(ci * 128, 128), 128)
            xf = jnp.concatenate([halo_ref[:, ls] * keep, x_ref[:, ls]], axis=0)
            acc = jnp.zeros((CT, 128), F32)
            for k in range(CONV_W):
                sh = CONV_W - 1 - k
                xs = pltpu.roll(xf, sh, 0) if sh else xf
                acc = acc + dw_ref[pl.ds(k, 1), ls] * xs[CONV_PAD:]
            c_ref[:, ls] = acc + dwb_ref[:, ls]
            return carry

        lax.fori_loop(0, d // 128, chunk, 0)
        xh, _ = _ln_stats(c_ref[...])
        yv = xh * lng_ref[...] + lnb_ref[...]
        s_ref[...] = (yv * _sigmoid(yv)).astype(BF16)

    row = pl.BlockSpec((CT, d), lambda i: (i, 0))
    halo = pl.BlockSpec((CONV_PAD, d), lambda i: (jnp.maximum(i * hb - 1, 0), 0))
    vec = pl.BlockSpec((1, d), lambda i: (0, 0))
    taps = pl.BlockSpec((CONV_PAD, d), lambda i: (0, 0))
    return _pc(
        body, name="conv_fwd", grid=(s // CT,),
        in_specs=[row, halo, taps, vec, vec, vec], out_specs=[row, row],
        out_shape=[SDS((s, d), F32), SDS((s, d), BF16)], sem=("parallel",),
        args=(glu, glu, dw, dwb, lng, lnb), jobs=jobs)


def _ln_bwd(ds, cv, lng, lnb):
    s, d = cv.shape

    def body(ds_ref, c_ref, g_ref, b_ref, dc_ref, dg_ref, db_ref):
        first = pl.program_id(0) == 0
        xh, rstd = _ln_stats(c_ref[...])
        gv = g_ref[...]
        yv = xh * gv + b_ref[...]
        sg = _sigmoid(yv)
        dln = ds_ref[...] * (sg * (1.0 + yv * (1.0 - sg)))
        dxh = dln * gv
        dc_ref[...] = rstd * (dxh - jnp.mean(dxh, axis=-1, keepdims=True)
                              - xh * jnp.mean(dxh * xh, axis=-1, keepdims=True))

        @pl.when(first)
        def _():
            dg_ref[...] = jnp.zeros_like(dg_ref)
            db_ref[...] = jnp.zeros_like(db_ref)
        dg_ref[...] += jnp.sum(dln * xh, axis=0, keepdims=True)
        db_ref[...] += jnp.sum(dln, axis=0, keepdims=True)

    row = pl.BlockSpec((ROWS, d), lambda i: (i, 0))
    vec = pl.BlockSpec((1, d), lambda i: (0, 0))
    return pl.pallas_call(
        body, name="ln_bwd", grid=(s // ROWS,), in_specs=[row, row, vec, vec], out_specs=[row, vec, vec],
        out_shape=[SDS((s, d), F32), SDS((1, d), F32), SDS((1, d), F32)],
        compiler_params=_cp("arbitrary"))(ds, cv, lng, lnb)


def _conv_bwd(dc, glu, ua, ug, dw, jobs=()):
    s, d = dc.shape
    hb = CT // CONV_PAD
    nsteps = s // CT
    full = CT + CONV_PAD

    def body(dc_ref, dcn_ref, x_ref, xp_ref, ua_ref, ug_ref, dw_ref, du_ref, ddw_ref, ddwb_ref, db1_ref):
        i = pl.program_id(0)
        keep_prev = (i > 0).astype(F32)
        keep_next = (i < nsteps - 1).astype(F32)

        @pl.when(i == 0)
        def _():
            ddw_ref[...] = jnp.zeros_like(ddw_ref)
            ddwb_ref[...] = jnp.zeros_like(ddwb_ref)
            db1_ref[...] = jnp.zeros_like(db1_ref)

        def chunk(ci, carry):
            off = pl.multiple_of(ci * 128, 128)
            ls = pl.ds(off, 128)
            ls2 = pl.ds(pl.multiple_of(d + ci * 128, 128), 128)
            dcc = dc_ref[:, ls]
            dcf = jnp.concatenate([dcc, dcn_ref[:, ls] * keep_next], axis=0)
            xf = jnp.concatenate([xp_ref[:, ls] * keep_prev, x_ref[:, ls]], axis=0)
            dglu = jnp.zeros((CT, 128), F32)
            for k in range(CONV_W):
                sh = CONV_W - 1 - k
                dshift = pltpu.roll(dcf, full - sh, 0) if sh else dcf
                dglu = dglu + dw_ref[pl.ds(k, 1), ls] * dshift[:CT]
                xs = pltpu.roll(xf, sh, 0) if sh else xf
                ddw_ref[pl.ds(k, 1), ls] += jnp.sum(dcc * xs[CONV_PAD:], axis=0, keepdims=True)
            ddwb_ref[:, ls] += jnp.sum(dcc, axis=0, keepdims=True)
            av, gv = ua_ref[:, ls], ug_ref[:, ls]
            sg = _sigmoid(gv)
            da = dglu * sg
            dgt = dglu * av * sg * (1.0 - sg)
            du_ref[:, ls] = da.astype(BF16)
            du_ref[:, ls2] = dgt.astype(BF16)
            db1_ref[:, ls] += jnp.sum(da, axis=0, keepdims=True)
            db1_ref[:, ls2] += jnp.sum(dgt, axis=0, keepdims=True)
            return carry

        lax.fori_loop(0, d // 128, chunk, 0)

    row = pl.BlockSpec((CT, d), lambda i: (i, 0))
    prev = pl.BlockSpec((CONV_PAD, d), lambda i: (jnp.maximum(i * hb - 1, 0), 0))
    nxt = pl.BlockSpec((CONV_PAD, d), lambda i: (jnp.minimum((i + 1) * hb, s // CONV_PAD - 1), 0))
    taps = pl.BlockSpec((CONV_PAD, d), lambda i: (0, 0))
    return _pc(
        body, name="conv_bwd", grid=(nsteps,),
        in_specs=[row, nxt, row, prev, row, row, taps],
        out_specs=[pl.BlockSpec((CT, 2 * d), lambda i: (i, 0)), taps, pl.BlockSpec((1, d), lambda i: (0, 0)),
                   pl.BlockSpec((1, 2 * d), lambda i: (0, 0))],
        out_shape=[SDS((s, 2 * d), BF16), SDS((CONV_PAD, d), F32), SDS((1, d), F32), SDS((1, 2 * d), F32)],
        sem=("arbitrary",), args=(dc, dc, glu, glu, ua, ug, dw), jobs=jobs)


TM = 1024
TS = 1024


def _glu_mm(n1, w1g, b1, jobs=()):
    s, d = n1.shape
    cw = w1g.shape[2]
    half = ND // 2

    def body(a_ref, wa_ref, wg_ref, ba_ref, bg_ref, ua_ref, ug_ref, glu_ref):
        a = a_ref[...]
        ua = _dot(a, wa_ref[0], NN) + ba_ref[...]
        ug = _dot(a, wg_ref[0], NN) + bg_ref[...]
        ua_ref[...], ug_ref[...] = ua, ug
        glu_ref[...] = ua * _sigmoid(ug)

    out = pl.BlockSpec((TM, cw), lambda m, i: (m, i))
    return _pc(
        body, name="glu_mm", grid=(s // TM, half),
        in_specs=[pl.BlockSpec((TM, d), lambda m, i: (m, 0)),
                  pl.BlockSpec((1, d, cw), lambda m, i: (i, 0, 0)),
                  pl.BlockSpec((1, d, cw), lambda m, i: (i + half, 0, 0)),
                  pl.BlockSpec((1, cw), lambda m, i: (0, i)),
                  pl.BlockSpec((1, cw), lambda m, i: (0, i + half))],
        out_specs=[out, out, out], out_shape=[SDS((s, d), F32)] * 3,
        sem=("parallel", "arbitrary"), args=(n1, w1g, w1g, b1, b1), jobs=jobs)


def _mm_rows(a, wg, name, res=None, bias=None, out_dtype=F32, tn=512, branches=False, jobs=()):
    s, kdim = a.shape
    _, kc, n = wg.shape
    assert kc * ND == kdim
    nx = 2 + (res is not None) + (bias is not None)

    def body(*refs):
        acc = _dot(refs[0][...], refs[1][...].reshape(kdim, tn), NN)
        for extra in refs[2:nx]:
            acc = acc + extra[...]
        refs[nx][...] = acc.astype(out_dtype)
        if branches:
            scr = refs[-1]
            _stage(scr, acc)
            for o_ref, dil in zip(refs[nx + 1:], SPLIT_DILATIONS):
                _split_rows(scr, o_ref, dil)

    ins, specs = [a, wg], [pl.BlockSpec((TM, kdim), lambda m, j: (m, 0)), pl.BlockSpec((ND, kc, tn), lambda m, j: (0, 0, j))]
    if res is not None:
        ins.append(res)
        specs.append(pl.BlockSpec((TM, tn), lambda m, j: (m, j)))
    if bias is not None:
        ins.append(bias)
        specs.append(pl.BlockSpec((1, tn), lambda m, j: (0, j)))
    out_specs, out_shape, scratch = [pl.BlockSpec((TM, tn), lambda m, j: (m, j))], [SDS((s, n), out_dtype)], []
    if branches:
        out_specs += _branch_specs(TM, tn, lambda dil, m, j: (0, m, j))
        out_shape += [SDS((dil, s // dil, n), out_dtype) for dil in SPLIT_DILATIONS]
        scratch = [pltpu.VMEM((tn // 128, TM, 128), F32)]
    outs = _pc(body, name=name, grid=(s // TM, n // tn), in_specs=specs, out_specs=out_specs, out_shape=out_shape,
               scratch=scratch, sem=("parallel", "arbitrary"), args=ins, jobs=jobs)
    return [outs[0][None]] + outs[1:] if branches else outs[0]


def _swiglu_mm(n2, wgg, wug, name, jobs=()):
    s, d = n2.shape
    fc = wgg.shape[2]

    def body(a_ref, wg_ref, wu_ref, g_ref, u_ref, act_ref):
        a = a_ref[...]
        g = _dot(a, wg_ref[0], NN)
        u = _dot(a, wu_ref[0], NN)
        g_ref[0], u_ref[0] = g.astype(BF16), u.astype(BF16)
        act_ref[0] = (g * _sigmoid(g) * u).astype(BF16)

    wspec = pl.BlockSpec((1, d, fc), lambda m, j: (j, 0, 0))
    out = pl.BlockSpec((1, TM, fc), lambda m, j: (j, m, 0))
    return _pc(
        body, name=name, grid=(s // TM, ND),
        in_specs=[pl.BlockSpec((TM, d), lambda m, j: (m, 0)), wspec, wspec],
        out_specs=[out, out, out], out_shape=[SDS((ND, s, fc), BF16)] * 3,
        sem=("parallel", "arbitrary"), args=(n2, wgg, wug), jobs=jobs)


def _down_mm(act, wdg, res, name, jobs=()):
    _, s, fc = act.shape
    d = wdg.shape[2]

    def body(a_ref, w_ref, r_ref, o_ref):
        @pl.when(pl.program_id(1) == 0)
        def _():
            o_ref[...] = r_ref[...]
        o_ref[...] += _dot(a_ref[0], w_ref[0], NN)

    row = pl.BlockSpec((TM, d), lambda m, j: (m, 0))
    return _pc(
        body, name=name, grid=(s // TM, ND),
        in_specs=[pl.BlockSpec((1, TM, fc), lambda m, j: (j, m, 0)),
                  pl.BlockSpec((1, fc, d), lambda m, j: (j, 0, 0)), row],
        out_specs=[row], out_shape=[SDS((s, d), F32)],
        sem=("parallel", "arbitrary"), args=(act, wdg, res), jobs=jobs)[0]


def _dact_mm(dh, wdg, gate, up, name, jobs=()):
    s, d = dh.shape
    fc = wdg.shape[1]

    def body(a_ref, w_ref, g_ref, u_ref, dg_ref, du_ref):
        dact = _dot(a_ref[...], w_ref[0], NT)
        g, u = g_ref[0].astype(F32), u_ref[0].astype(F32)
        sg = _sigmoid(g)
        du_ref[0] = (dact * (g * sg)).astype(BF16)
        dg_ref[0] = (dact * u * (sg * (1.0 + g * (1.0 - sg)))).astype(BF16)

    blk = pl.BlockSpec((1, TM, fc), lambda m, j: (j, m, 0))
    return _pc(
        body, name=name, grid=(s // TM, ND),
        in_specs=[pl.BlockSpec((TM, d), lambda m, j: (m, 0)),
                  pl.BlockSpec((1, fc, d), lambda m, j: (j, 0, 0)), blk, blk],
        out_specs=[blk, blk], out_shape=[SDS((ND, s, fc), BF16)] * 2,
        sem=("parallel", "arbitrary"), args=(dh, wdg, gate, up), jobs=jobs)


def _dwd_mm(act, dh, name, jobs=()):
    _, s, fc = act.shape
    d = dh.shape[1]
    nk = s // TS

    def body(a_ref, b_ref, o_ref, acc):
        k = pl.program_id(1)

        @pl.when(k == 0)
        def _():
            acc[...] = jnp.zeros_like(acc)
        acc[...] += _dot(a_ref[0], b_ref[...], TN)

        @pl.when(k == nk - 1)
        def _():
            o_ref[0] = acc[...].astype(BF16)

    return _pc(
        body, name=name, grid=(ND, nk),
        in_specs=[pl.BlockSpec((1, TS, fc), lambda j, k: (j, k, 0)), pl.BlockSpec((TS, d), lambda j, k: (k, 0))],
        out_specs=[pl.BlockSpec((1, fc, d), lambda j, k: (_slot(j), 0, 0))],
        out_shape=[SDS((ND, fc, d), BF16)], scratch=[pltpu.VMEM((fc, d), F32)],
        sem=("parallel", "arbitrary"), args=(act, dh), jobs=jobs)[0]


def _dwgu_mm(n2, dgate, dup, name, jobs=()):
    s, d = n2.shape
    fc = dgate.shape[2]
    nk = s // TS

    def body(a_ref, g_ref, u_ref, og_ref, ou_ref, accg, accu):
        k = pl.program_id(1)

        @pl.when(k == 0)
        def _():
            accg[...] = jnp.zeros_like(accg)
            accu[...] = jnp.zeros_like(accu)
        a = a_ref[...]
        accg[...] += _dot(g_ref[0], a, TN)
        accu[...] += _dot(u_ref[0], a, TN)

        @pl.when(k == nk - 1)
        def _():
            og_ref[0] = accg[...].astype(BF16)
            ou_ref[0] = accu[...].astype(BF16)

    blk = pl.BlockSpec((1, TS, fc), lambda j, k: (j, k, 0))
    out = pl.BlockSpec((1, fc, d), lambda j, k: (_slot(j), 0, 0))
    return _pc(
        body, name=name, grid=(ND, nk),
        in_specs=[pl.BlockSpec((TS, d), lambda j, k: (k, 0)), blk, blk], out_specs=[out, out],
        out_shape=[SDS((ND, fc, d), BF16)] * 2,
        scratch=[pltpu.VMEM((fc, d), F32), pltpu.VMEM((fc, d), F32)],
        sem=("parallel", "arbitrary"), args=(n2, dgate, dup), jobs=jobs)


def _dn_ffn_mm(dgate, dup, wgg, wug, name, jobs=()):
    _, s, fc = dgate.shape
    d = wgg.shape[1]

    def body(g_ref, u_ref, wg_ref, wu_ref, o_ref):
        j = pl.program_id(1)

        @pl.when(j == 0)
        def _():
            o_ref[...] = jnp.zeros_like(o_ref)
        o_ref[...] += _dot(g_ref[0], wg_ref[0], NT) + _dot(u_ref[0], wu_ref[0], NT)

    blk = pl.BlockSpec((1, TM, fc), lambda m, j: (j, m, 0))
    wspec = pl.BlockSpec((1, d, fc), lambda m, j: (j, 0, 0))
    return _pc(
        body, name=name, grid=(s // TM, ND), in_specs=[blk, blk, wspec, wspec],
        out_specs=[pl.BlockSpec((TM, d), lambda m, j: (m, 0))], out_shape=[SDS((s, d), F32)],
        sem=("parallel", "arbitrary"), args=(dgate, dup, wgg, wug), jobs=jobs)[0]


def _mm_rows_t(pairs, name, out_dtype, branches=False, jobs=()):
    s, n = pairs[0][0].shape
    _, kc, _ = pairs[0][1].shape
    np_ = len(pairs)
    grp = ND // 2
    wide = grp * kc

    def body(*refs):
        o_ref = refs[2 * np_]
        for i in range(grp):
            acc = None
            for p in range(np_):
                t = _dot(refs[p][...], refs[np_ + p][i], NT)
                acc = t if acc is None else acc + t
            o_ref[:, kc * i:kc * (i + 1)] = acc.astype(out_dtype)
            if branches:
                for c, ls in enumerate(_lane_chunks(kc)):
                    refs[-1][i * (kc // 128) + c] = acc[:, ls]
        if branches:
            for b_ref, dil in zip(refs[2 * np_ + 1:], SPLIT_DILATIONS):
                _split_rows(refs[-1], b_ref, dil)

    out_specs, out_shape, scratch = [pl.BlockSpec((TM, wide), lambda m, j: (m, j))], [SDS((s, kc * ND), out_dtype)], []
    if branches:
        out_specs += _branch_specs(TM, wide, lambda dil, m, j: (0, m, j))
        out_shape += [SDS((dil, s // dil, kc * ND), out_dtype) for dil in SPLIT_DILATIONS]
        scratch = [pltpu.VMEM((wide // 128, TM, 128), F32)]
    outs = _pc(
        body, name=name, grid=(s // TM, ND // grp),
        in_specs=[pl.BlockSpec((TM, n), lambda m, j: (m, 0))] * np_ + [pl.BlockSpec((grp, kc, n), lambda m, j: (j, 0, 0))] * np_,
        out_specs=out_specs, out_shape=out_shape, scratch=scratch,
        sem=("parallel", "arbitrary"), args=[p[0] for p in pairs] + [p[1] for p in pairs], jobs=jobs)
    return [outs[0][None]] + outs[1:] if branches else outs[0]


def _dw_rows_mm(a, b, name):
    s, kdim = a.shape
    n = b.shape[1]
    kc = kdim // ND
    ts = TS // 2
    nk = s // ts

    def body(a_ref, b_ref, o_ref, acc):
        k = pl.program_id(0)

        @pl.when(k == 0)
        def _():
            acc[...] = jnp.zeros_like(acc)
        acc[...] += _dot(a_ref[...], b_ref[...], TN)

        @pl.when(k == nk - 1)
        def _():
            for dev in range(ND):
                o_ref[_slot(dev)] = acc[kc * dev:kc * (dev + 1), :].astype(BF16)

    return pl.pallas_call(
        body, name=name, grid=(nk,),
        in_specs=[pl.BlockSpec((ts, kdim), lambda k: (k, 0)), pl.BlockSpec((ts, n), lambda k: (k, 0))],
        out_specs=pl.BlockSpec((ND, kc, n), lambda k: (0, 0, 0)), out_shape=SDS((ND, kc, n), BF16),
        scratch_shapes=[pltpu.VMEM((kdim, n), F32)], compiler_params=_cp("arbitrary"))(a, b)


def _dw1_mm(n1, du, jobs=()):
    s, d = n1.shape
    cw = du.shape[1] // ND
    nk = s // TS

    def body(a_ref, b_ref, o_ref, acc):
        k = pl.program_id(1)

        @pl.when(k == 0)
        def _():
            acc[...] = jnp.zeros_like(acc)
        acc[...] += _dot(a_ref[...], b_ref[...], TN)

        @pl.when(k == nk - 1)
        def _():
            o_ref[0] = acc[...].astype(BF16)

    return _pc(
        body, name="dw1_mm", grid=(ND, nk),
        in_specs=[pl.BlockSpec((TS, d), lambda j, k: (k, 0)), pl.BlockSpec((TS, cw), lambda j, k: (k, j))],
        out_specs=[pl.BlockSpec((1, d, cw), lambda j, k: (_slot(j), 0, 0))], out_shape=[SDS((ND, d, cw), BF16)],
        scratch=[pltpu.VMEM((d, cw), F32)], sem=("parallel", "arbitrary"), args=(n1, du), jobs=jobs)[0]


def _dn1_mm(du, w1g, jobs=()):
    s = du.shape[0]
    _, d, cw = w1g.shape

    def body(a_ref, w_ref, o_ref):
        j = pl.program_id(1)

        @pl.when(j == 0)
        def _():
            o_ref[...] = jnp.zeros_like(o_ref)
        o_ref[...] += _dot(a_ref[...], w_ref[0], NT)

    return _pc(
        body, name="dn1_mm", grid=(s // TM, ND),
        in_specs=[pl.BlockSpec((TM, cw), lambda m, j: (m, j)), pl.BlockSpec((1, d, cw), lambda m, j: (j, 0, 0))],
        out_specs=[pl.BlockSpec((TM, d), lambda m, j: (m, 0))], out_shape=[SDS((s, d), F32)],
        sem=("parallel", "arbitrary"), args=(du, w1g), jobs=jobs)[0]


NEG = -1e30


def _slopes(heads):
    return [2.0 ** (-8.0 * (h + 1) / heads) for h in range(heads)]


def _band(has_prev):
    qi = lax.broadcasted_iota(jnp.int32, (BLK, 2 * BLK), 0)
    ki = lax.broadcasted_iota(jnp.int32, (BLK, 2 * BLK), 1)
    j = qi - ki + BLK
    ok = (j >= 0) & (j <= BLK) & (has_prev | (ki >= BLK))
    return j.astype(F32), ok


SPLIT_DILATIONS = tuple(dil for dil in BRANCH_DILATIONS if dil > 1)


def _lane_chunks(w):
    return [slice(128 * c, 128 * (c + 1)) for c in range(w // 128)]


def _stage(scr, tile):
    for c, ls in enumerate(_lane_chunks(tile.shape[1])):
        scr[c] = tile[:, ls]


def _split_rows(scr, o_ref, dil):
    _, n, w = o_ref.shape
    for r in range(dil):
        for c, ls in enumerate(_lane_chunks(w)):
            o_ref[r, :, ls] = scr[c, pl.ds(r, n, stride=dil), :].astype(o_ref.dtype)


def _join_rows(i_ref, scr, dil):
    _, n, w = i_ref.shape
    for r in range(dil):
        for c, ls in enumerate(_lane_chunks(w)):
            scr[c, pl.ds(r, n, stride=dil), :] = i_ref[r, :, ls].astype(F32)


def _unstage(scr, w):
    return jnp.concatenate([scr[c] for c in range(w // 128)], axis=1)


def _branch_specs(rows, w, index):
    return [pl.BlockSpec((dil, rows // dil, w), functools.partial(index, dil)) for dil in SPLIT_DILATIONS]


def _attn_fwd(q, k, v, dil, jobs=()):
    _, l, d = q.shape
    heads = d // HEAD
    assert heads <= HEAD
    scale = HEAD ** -0.5
    slopes = _slopes(heads)

    def body(q_ref, kc_ref, kp_ref, vc_ref, vp_ref, o_ref, lse_ref):
        jf, ok = _band(pl.program_id(1) > 0)
        lane = lax.broadcasted_iota(jnp.int32, (BLK, HEAD), 1)
        lse = jnp.zeros((BLK, HEAD), F32)
        for h in range(heads):
            sl = slice(HEAD * h, HEAD * (h + 1))
            kh = jnp.concatenate([kp_ref[0, :, sl], kc_ref[0, :, sl]], axis=0)
            vh = jnp.concatenate([vp_ref[0, :, sl], vc_ref[0, :, sl]], axis=0)
            logits = jnp.where(ok, _dot(q_ref[0, :, sl], kh, NT) * scale + jf * (-slopes[h] * dil), NEG)
            m = jnp.max(logits, axis=-1, keepdims=True)
            p = jnp.exp(logits - m)
            den = jnp.sum(p, axis=-1, keepdims=True)
            o_ref[0, :, sl] = _dot(p.astype(BF16), vh, NN) / den
            lse = jnp.where(lane == h, m + jnp.log(den), lse)
        lse_ref[0] = lse

    cur = pl.BlockSpec((1, BLK, d), lambda r, b: (r, b, 0))
    prev = pl.BlockSpec((1, BLK, d), lambda r, b: (r, jnp.maximum(b - 1, 0), 0))
    return _pc(
        body, name=f"attn_fwd_d{dil}", grid=(dil, l // BLK),
        in_specs=[cur, cur, prev, cur, prev], out_specs=[cur, pl.BlockSpec((1, BLK, HEAD), lambda r, b: (r, b, 0))],
        out_shape=[SDS((dil, l, d), F32), SDS((dil, l, HEAD), F32)], sem=("parallel", "arbitrary"),
        args=(q, k, k, v, v), jobs=jobs)


def _attn_merge(outs, lses):
    _, s, d = outs[0].shape
    heads = d // HEAD
    nb = len(outs)
    nsplit = nb - 1

    def body(*refs):
        o_refs, l_refs = refs[:nb], refs[nb:2 * nb]
        att_refs, lse_refs = refs[2 * nb:3 * nb], refs[3 * nb:4 * nb]
        scr_o, scr_l, scr_att = refs[4 * nb:4 * nb + nsplit], refs[4 * nb + nsplit:4 * nb + 2 * nsplit], refs[-1]
        ls = [l_refs[0][...]]
        for k, dil in enumerate(SPLIT_DILATIONS):
            _join_rows(o_refs[1 + k], scr_o[k], dil)
            _join_rows(l_refs[1 + k], scr_l[k], dil)
            ls.append(scr_l[k][0])
        m = functools.reduce(jnp.maximum, ls)
        ws = [jnp.exp(v - m) for v in ls]
        den = functools.reduce(jnp.add, ws)
        ws = [w / den for w in ws]
        lse_refs[0][...] = m + jnp.log(den)
        scr_l[0][0] = m + jnp.log(den)
        for h in range(heads):
            sl = slice(HEAD * h, HEAD * (h + 1))
            slab = ws[0][:, h:h + 1] * o_refs[0][:, sl]
            for k in range(nsplit):
                slab = slab + ws[1 + k][:, h:h + 1] * scr_o[k][h]
            att_refs[0][:, sl] = slab.astype(BF16)
            scr_att[h] = slab
        for k, dil in enumerate(SPLIT_DILATIONS):
            _split_rows(scr_att, att_refs[1 + k], dil)
            _split_rows(scr_l[0], lse_refs[1 + k], dil)

    def specs(w):
        return [pl.BlockSpec((ROWS, w), lambda i: (i, 0))] + _branch_specs(ROWS, w, lambda dil, i: (0, i, 0))

    def shapes(w, dt):
        return [SDS((s, w), dt)] + [SDS((dil, s // dil, w), dt) for dil in SPLIT_DILATIONS]

    wide, narrow = pltpu.VMEM((heads, ROWS, 128), F32), pltpu.VMEM((1, ROWS, 128), F32)
    res = pl.pallas_call(
        body, name="attn_merge", grid=(s // ROWS,), in_specs=specs(d) + specs(HEAD), out_specs=specs(d) + specs(HEAD),
        out_shape=shapes(d, BF16) + shapes(HEAD, F32),
        scratch_shapes=[wide] * nsplit + [narrow] * nsplit + [wide],
        compiler_params=_cp("parallel"))(outs[0].reshape(s, d), *outs[1:], lses[0].reshape(s, HEAD), *lses[1:])
    return list(res[:nb]), list(res[nb:])


def _attn_bwd(q, k, v, do, o, lse, dil, jobs=()):
    _, l, d = q.shape
    nb = l // BLK
    heads = d // HEAD
    scale = HEAD ** -0.5
    slopes = _slopes(heads)
    whole = 2 * l * d <= RESIDENT_BYTES
    steps = nb if whole else nb + 1

    def body(q_ref, kc_ref, kp_ref, vc_ref, vp_ref, do_ref, o_ref, lse_ref, dq_ref, dk_ref, dv_ref, ck, cv):
        b = pl.program_id(1)
        rows = pl.ds(pl.multiple_of(jnp.maximum(b - 1, 0) * BLK, BLK), BLK) if whole else slice(None)

        @pl.when(b == 0)
        def _():
            ck[...] = jnp.zeros_like(ck)
            cv[...] = jnp.zeros_like(cv)

        @pl.when(b < nb)
        def _():
            jf, ok = _band(b > 0)
            for h in range(heads):
                sl = slice(HEAD * h, HEAD * (h + 1))
                qh, doh = q_ref[0, :, sl], do_ref[0, :, sl]
                kh = jnp.concatenate([kp_ref[0, :, sl], kc_ref[0, :, sl]], axis=0)
                vh = jnp.concatenate([vp_ref[0, :, sl], vc_ref[0, :, sl]], axis=0)
                lse_h = lse_ref[0, :, h:h + 1]
                delta = jnp.sum(doh.astype(F32) * o_ref[0, :, sl].astype(F32), axis=-1, keepdims=True)
                p = jnp.where(ok, jnp.exp(_dot(qh, kh, NT) * scale + jf * (-slopes[h] * dil) - lse_h), 0.0)
                ds = (p * (_dot(doh, vh, NT) - delta)).astype(BF16)
                dq_ref[0, :, sl] = (_dot(ds, kh, NN) * scale).astype(BF16)
                dk2 = _dot(ds, qh, TN) * scale
                dv2 = _dot(p.astype(BF16), doh, TN)
                dk_ref[0, rows, sl] = (ck[:, sl] + dk2[:BLK]).astype(BF16)
                dv_ref[0, rows, sl] = (cv[:, sl] + dv2[:BLK]).astype(BF16)
                ck[:, sl] = dk2[BLK:]
                cv[:, sl] = dv2[BLK:]

        @pl.when(b == steps - 1)
        def _():
            last = pl.ds((nb - 1) * BLK, BLK) if whole else slice(None)
            dk_ref[0, last, :] = ck[...].astype(BF16)
            dv_ref[0, last, :] = cv[...].astype(BF16)

    cur = pl.BlockSpec((1, BLK, d), lambda r, b: (r, jnp.minimum(b, nb - 1), 0))
    prev = pl.BlockSpec((1, BLK, d), lambda r, b: (r, jnp.clip(b - 1, 0, nb - 1), 0))
    lse_spec = pl.BlockSpec((1, BLK, HEAD), lambda r, b: (r, jnp.minimum(b, nb - 1), 0))
    dkv = pl.BlockSpec((1, l, d), lambda r, b: (r, 0, 0)) if whole else prev
    return _pc(
        body, name=f"attn_bwd_d{dil}", grid=(dil, steps),
        in_specs=[cur, cur, prev, cur, prev, cur, cur, lse_spec], out_specs=[cur, dkv, dkv],
        out_shape=[SDS((dil, l, d), BF16)] * 3,
        scratch=[pltpu.VMEM((BLK, d), F32), pltpu.VMEM((BLK, d), F32)],
        sem=("parallel", "arbitrary"), args=(q, k, k, v, v, do, o, lse), jobs=jobs)


def _sum_cast(xs, name):
    _, s, d = xs[0].shape
    nsplit = len(xs) - 1

    def body(*refs):
        i_refs, o_ref, scr = refs[:nsplit + 1], refs[nsplit + 1], refs[nsplit + 2:]
        acc = i_refs[0][...].astype(F32)
        for k, dil in enumerate(SPLIT_DILATIONS):
            _join_rows(i_refs[1 + k], scr[k], dil)
            acc = acc + _unstage(scr[k], d)
        o_ref[...] = acc.astype(BF16)

    row = pl.BlockSpec((ROWS, d), lambda i: (i, 0))
    return pl.pallas_call(
        body, name=name, grid=(s // ROWS,), in_specs=[row] + _branch_specs(ROWS, d, lambda dil, i: (0, i, 0)),
        out_specs=row, out_shape=SDS((s, d), BF16),
        scratch_shapes=[pltpu.VMEM((d // 128, ROWS, 128), F32)] * nsplit,
        compiler_params=_cp("parallel"))(xs[0].reshape(s, d), *xs[1:])


def _pack_rows(vs, width):
    flat = jnp.concatenate([v.reshape(-1) for v in vs])
    spans, at = [], 0
    for v in vs:
        spans.append((at, v.size))
        at += v.size
    rows = -(-at // width)
    rows = -(-rows // 8) * 8
    flat = jnp.pad(flat, (0, rows * width - at))
    return flat.reshape(rows, width), spans


def kernel(x, a_norm_g, conv_w1, conv_b1, conv_dw, conv_dw_b, conv_ln_g, conv_ln_b, conv_w2, conv_b2, kv_norm_g, w_k, w_v, b_norm_g, w_q, w_o, ffn_norm_g, ffn_w_gate, ffn_w_up, ffn_w_down, final_norm_g, loss_target, m_a_norm_g, m_conv_w1, m_conv_b1, m_conv_dw, m_conv_dw_b, m_conv_ln_g, m_conv_ln_b, m_conv_w2, m_conv_b2, m_kv_norm_g, m_w_k, m_w_v, m_b_norm_g, m_w_q, m_w_o, m_ffn_norm_g, m_ffn_w_gate, m_ffn_w_up, m_ffn_w_down, m_final_norm_g, v_a_norm_g, v_conv_w1, v_conv_b1, v_conv_dw, v_conv_dw_b, v_conv_ln_g, v_conv_ln_b, v_conv_w2, v_conv_b2, v_kv_norm_g, v_w_k, v_w_v, v_b_norm_g, v_w_q, v_w_o, v_ffn_norm_g, v_ffn_w_gate, v_ffn_w_up, v_ffn_w_down, v_final_norm_g):
    s, d = x.shape[1], x.shape[2]
    dc = d // ND
    h0 = x[0]
    target = loss_target[0]
    xi, yi, ci = lax.axis_index("x"), lax.axis_index("y"), lax.axis_index("c")
    me = 4 * xi + 2 * yi + ci
    c_idx = jnp.reshape(ci, (1,)).astype(jnp.int32)
    q_idx = jnp.reshape(2 * xi + yi, (1,)).astype(jnp.int32)

    bf = lambda w: w.astype(BF16)
    small_shards = [a_norm_g, conv_b1, conv_dw, conv_dw_b, conv_ln_g, conv_ln_b, conv_b2]
    sp, sp_spans = _pack_rows(small_shards, dc)
    w1g, spg = _all_gather([bf(conv_w1[0]), sp], "gather_first")
    spg = spg.reshape(ND, -1)

    def small_full(i, rows):
        at, size = sp_spans[i]
        return spg[:, at:at + size].reshape(ND, rows, size // rows).transpose(1, 0, 2).reshape(rows, -1)

    a_g = small_full(0, 1)
    b1 = small_full(1, 1)
    dw = jnp.pad(small_full(2, CONV_W), ((0, CONV_PAD - CONV_W), (0, 0)))
    dwb, lng, lnb, b2 = small_full(3, 1), small_full(4, 1), small_full(5, 1), small_full(6, 1)
    kv_g, q_g, fin_g = kv_norm_g.reshape(1, d), b_norm_g.reshape(1, d), final_norm_g.reshape(1, d)
    f_g = [ffn_norm_g[0:1], ffn_norm_g[1:2]]

    def send(*shards):
        return _job_gather_send([bf(t) for t in shards])

    def forward(job):
        return _job_gather_forward(job.result)

    def send_half(w, part, first=None):
        return _job_gather_send_rows(bf(w), part, 2, None if first is None else first.result[0])

    s_w2 = send(conv_w2[0])
    (n1,) = _rms_fwd(h0, [a_g], "rms_a", jobs=[s_w2])
    f_w2, s_g0 = forward(s_w2), send(ffn_w_gate[0])
    ua, ug, glu = _glu_mm(n1, w1g, b1, jobs=[f_w2, s_g0])
    (w2g,) = f_w2.result
    f_g0, s_u0 = forward(s_g0), send(ffn_w_up[0])
    cv, sw = _conv_fwd(glu, dw, dwb, lng, lnb, jobs=[f_g0, s_u0])
    (wg0,) = f_g0.result
    f_u0 = forward(s_u0)
    h1 = _mm_rows(sw, w2g, "w2_mm", res=h0, bias=b2, jobs=[f_u0])
    (wu0,) = f_u0.result
    (n2a,) = _rms_fwd(h1, [f_g[0]], "rms_f0")
    s_mid = send(ffn_w_down[0], w_k, w_v)
    gate0, up0, act0 = _swiglu_mm(n2a, wg0, wu0, "swiglu_mm0", jobs=[s_mid])
    f_mid = forward(s_mid)
    _comm_call([f_mid], "forward_mid")
    wd0, wkg, wvg = f_mid.result
    s_qo = send(w_q[0], w_o[0])
    h2 = _down_mm(act0, wd0, h1, "down_mm0", jobs=[s_qo])
    kvn, qn = _rms_fwd(h2, [kv_g, q_g], "rms_kvq")
    f_qo, s_g1a = forward(s_qo), send_half(ffn_w_gate[1], 0)
    kk = _mm_rows(kvn, wkg, "k_mm", out_dtype=BF16, branches=True, jobs=[f_qo, s_g1a])
    wqg, wog = f_qo.result
    s_g1b = send_half(ffn_w_gate[1], 1, s_g1a)
    vv = _mm_rows(kvn, wvg, "v_mm", out_dtype=BF16, branches=True, jobs=[s_g1b])
    f_g1, s_u1a = forward(s_g1b), send_half(ffn_w_up[1], 0)
    qq = _mm_rows(qn, wqg, "q_mm", out_dtype=BF16, branches=True, jobs=[f_g1, s_u1a])
    (wg1,) = f_g1.result
    branch = {dil: (qq[i], kk[i], vv[i]) for i, dil in enumerate(BRANCH_DILATIONS)}
    s_u1b = send_half(ffn_w_up[1], 1, s_u1a)
    o1, l1 = _attn_fwd(*branch[1], 1, jobs=[s_u1b])
    f_u1, s_d1a = forward(s_u1b), send_half(ffn_w_down[1], 0)
    o4, l4 = _attn_fwd(*branch[4], 4, jobs=[f_u1, s_d1a])
    (wu1,) = f_u1.result
    s_d1b = send_half(ffn_w_down[1], 1, s_d1a)
    o16, l16 = _attn_fwd(*branch[16], 16, jobs=[s_d1b])
    atts, lses = _attn_merge([o1, o4, o16], [l1, l4, l16])
    att = atts[0]
    atts, lses = [att[None]] + atts[1:], [lses[0][None]] + lses[1:]
    f_d1 = forward(s_d1b)
    h3 = _mm_rows(att, wog, "wo_mm", res=h2, jobs=[f_d1])
    (wd1,) = f_d1.result
    (n2b,) = _rms_fwd(h3, [f_g[1]], "rms_f1")
    gate1, up1, act1 = _swiglu_mm(n2b, wg1, wu1, "swiglu_mm1")
    h4 = _down_mm(act1, wd1, h3, "down_mm1")

    flat = lambda g: g.reshape(ND, -1, g.shape[-1])
    chip_sums, cross = {}, {}

    def to_sibling(**grads):
        job = _job_scatter_sibling([flat(g) for g in grads.values()])
        job.names = list(grads)
        return job

    def add_up(job):
        for n, g, r in zip(job.names, job.ins, job.result):
            chip_sums[n] = _rs_add(g, r, c_idx, f"rs_add_{n}")

    def to_chips(*names):
        job = _job_scatter_cross([chip_sums[n] for n in names])
        job.names = names
        return job

    def landed(job):
        cross.update(zip(job.names, job.result))

    dh4, dh4b, d_fin, loss_row = _final_loss(h4, target, fin_g)
    dgate1, dup1 = _dact_mm(dh4b, wd1, gate1, up1, "dact_mm1")
    g_wd1 = _dwd_mm(act1, dh4b, "dwd_mm1")
    j1 = to_sibling(wd1=g_wd1)
    g_wg1, g_wu1 = _dwgu_mm(n2b, dgate1, dup1, "dwgu_mm1", jobs=[j1])
    add_up(j1)
    j2, j3 = to_chips("wd1"), to_sibling(wg1=g_wg1, wu1=g_wu1)
    dn2b = _dn_ffn_mm(dgate1, dup1, wg1, wu1, "dn_ffn_mm1", jobs=[j2, j3])
    landed(j2)
    add_up(j3)
    dh3, dh3b, d_f1 = _rms_bwd(h3, [(f_g[1], dn2b)], dh4, "rms_f1_bwd")
    g_wo = _dw_rows_mm(att, dh3b, "dwo_mm")
    j4 = to_sibling(wo=g_wo)
    datt = _mm_rows_t([(dh3b, wog)], "datt_mm", BF16, branches=True, jobs=[j4])
    add_up(j4)
    riders = {1: to_chips("wg1"), 4: to_chips("wu1"), 16: to_chips("wo")}
    dqs, dks, dvs = [], [], []
    for i, dil in enumerate(BRANCH_DILATIONS):
        qb, kb, vb = branch[dil]
        dq_b, dk_b, dv_b = _attn_bwd(qb, kb, vb, datt[i], atts[i], lses[i], dil, jobs=[riders[dil]])
        landed(riders[dil])
        dqs.append(dq_b)
        dks.append(dk_b)
        dvs.append(dv_b)
    dq, dk, dv = _sum_cast(dqs, "dq_sum"), _sum_cast(dks, "dk_sum"), _sum_cast(dvs, "dv_sum")
    g_wq = _dw_rows_mm(qn, dq, "dwq_mm")
    g_wk = _dw_rows_mm(kvn, dk, "dwk_mm")
    g_wv = _dw_rows_mm(kvn, dv, "dwv_mm")
    j5 = to_sibling(wq=g_wq, wk=g_wk, wv=g_wv)
    dqn = _mm_rows_t([(dq, wqg)], "dqn_mm", F32, jobs=[j5])
    add_up(j5)
    j6 = to_chips("wq", "wk")
    dkvn = _mm_rows_t([(dk, wkg), (dv, wvg)], "dkvn_mm", F32, jobs=[j6])
    landed(j6)
    dh2, dh2b, d_q, d_kv = _rms_bwd(h2, [(q_g, dqn), (kv_g, dkvn)], dh3, "rms_kvq_bwd")
    j7 = to_chips("wv")
    dgate0, dup0 = _dact_mm(dh2b, wd0, gate0, up0, "dact_mm0", jobs=[j7])
    landed(j7)
    g_wd0 = _dwd_mm(act0, dh2b, "dwd_mm0")
    j8 = to_sibling(wd0=g_wd0)
    g_wg0, g_wu0 = _dwgu_mm(n2a, dgate0, dup0, "dwgu_mm0", jobs=[j8])
    add_up(j8)
    j9, j10 = to_chips("wd0"), to_sibling(wg0=g_wg0, wu0=g_wu0)
    dn2a = _dn_ffn_mm(dgate0, dup0, wg0, wu0, "dn_ffn_mm0", jobs=[j9, j10])
    landed(j9)
    add_up(j10)
    dh1, dh1b, d_f0, d_b2 = _rms_bwd(h1, [(f_g[0], dn2a)], dh2, "rms_f0_bwd", colsum=True)
    g_w2 = _dw_rows_mm(sw, dh1b, "dw2_mm")
    j11 = to_sibling(w2=g_w2)
    dsw = _mm_rows_t([(dh1b, w2g)], "dsw_mm", F32, jobs=[j11])
    add_up(j11)
    dcv, d_lng, d_lnb = _ln_bwd(dsw, cv, lng, lnb)
    j12 = to_chips("wg0", "wu0")
    du, d_dw, d_dwb, d_b1 = _conv_bwd(dcv, glu, ua, ug, dw, jobs=[j12])
    landed(j12)
    j13 = to_chips("w2")
    g_w1 = _dw1_mm(n1, du, jobs=[j13])
    landed(j13)
    j14 = to_sibling(w1=g_w1)
    dn1 = _dn1_mm(du, w1g, jobs=[j14])
    add_up(j14)
    dx, _, d_a = _rms_bwd(h0, [(a_g, dn1)], dh1, "rms_a_bwd")

    small_g = [d_a, d_b1, d_dw[:CONV_W], d_dwb, d_lng, d_lnb, d_b2, d_kv, d_q, d_f0, d_f1, d_fin, loss_row]
    gp, gp_spans = _pack_rows(small_g, d)
    j15, j16 = to_chips("w1"), _job_gather_send([gp])
    _comm_call([j15, j16], "rs_w1_send_small")
    landed(j15)
    j17 = _job_gather_forward(j16.result)
    _comm_call([j17], "forward_small")
    (gpg,) = j17.result

    two = lambda t: t.reshape(-1, t.shape[-1])

    def adam(w, m, v, names, tag, swapped=False):
        view = (lambda t: jnp.swapaxes(t, 1, 2)) if swapped else (lambda t: t)
        w, m, v = view(w), view(m), view(v)
        res = None
        for part, n in enumerate(names):
            res = _adamw_big(two(w), two(m), two(v), chip_sums[n], cross[n], q_idx, f"adamw_{tag}{part}", part, res)
        return [view(t.reshape(w.shape)) for t in res]

    big_out = [
        adam(conv_w1, m_conv_w1, v_conv_w1, ["w1"], "w1"), adam(conv_w2, m_conv_w2, v_conv_w2, ["w2"], "w2"),
        adam(w_k, m_w_k, v_w_k, ["wk"], "wk"), adam(w_v, m_w_v, v_w_v, ["wv"], "wv"),
        adam(w_q, m_w_q, v_w_q, ["wq"], "wq"), adam(w_o, m_w_o, v_w_o, ["wo"], "wo"),
        adam(ffn_w_gate, m_ffn_w_gate, v_ffn_w_gate, ["wg0", "wg1"], "wg", swapped=True),
        adam(ffn_w_up, m_ffn_w_up, v_ffn_w_up, ["wu0", "wu1"], "wu", swapped=True),
        adam(ffn_w_down, m_ffn_w_down, v_ffn_w_down, ["wd0", "wd1"], "wd")]

    gsum = _sum_devices(gpg, "sum_small_grads").reshape(-1)

    def gfull(i):
        at, size = gp_spans[i]
        return gsum[at:at + size]

    def shard_of(vec, rows):
        return lax.dynamic_slice_in_dim(vec.reshape(rows, -1), me * (vec.size // rows // ND), vec.size // rows // ND, axis=1)

    loss = gfull(12)[0]
    small_grads = [
        shard_of(gfull(0), 1), shard_of(gfull(1), 1), shard_of(gfull(2), CONV_W)[None], shard_of(gfull(3), 1),
        shard_of(gfull(4), 1), shard_of(gfull(5), 1), shard_of(gfull(6), 1),
        gfull(7), gfull(8)[None], jnp.stack([gfull(9), gfull(10)]), gfull(11)]
    small_w = [a_norm_g, conv_b1, conv_dw, conv_dw_b, conv_ln_g, conv_ln_b, conv_b2, kv_norm_g, b_norm_g, ffn_norm_g, final_norm_g]
    small_m = [m_a_norm_g, m_conv_b1, m_conv_dw, m_conv_dw_b, m_conv_ln_g, m_conv_ln_b, m_conv_b2, m_kv_norm_g, m_b_norm_g, m_ffn_norm_g, m_final_norm_g]
    small_v = [v_a_norm_g, v_conv_b1, v_conv_dw, v_conv_dw_b, v_conv_ln_g, v_conv_ln_b, v_conv_b2, v_kv_norm_g, v_b_norm_g, v_ffn_norm_g, v_final_norm_g]
    small_grads = [g.reshape(w.shape) for g, w in zip(small_grads, small_w)]
    wp, spans = _pack_rows(small_w, 128)
    gpk, _ = _pack_rows(small_grads, 128)
    mp, _ = _pack_rows(small_m, 128)
    vp, _ = _pack_rows(small_v, 128)
    dp, mnp, vnp = _adamw_small(wp, gpk, mp, vp, "adamw_small")

    def unpack(packed):
        flat = packed.reshape(-1)
        return [flat[at:at + size].reshape(w.shape) for (at, size), w in zip(spans, small_w)]

    small_out = list(zip(small_grads, unpack(dp), unpack(mnp), unpack(vnp)))

    order = ["a_norm_g", "conv_w1", "conv_b1", "conv_dw", "conv_dw_b", "conv_ln_g", "conv_ln_b", "conv_w2", "conv_b2",
             "kv_norm_g", "w_k", "w_v", "b_norm_g", "w_q", "w_o", "ffn_norm_g", "ffn_w_gate", "ffn_w_up", "ffn_w_down",
             "final_norm_g"]
    big_names = ["conv_w1", "conv_w2", "w_k", "w_v", "w_q", "w_o", "ffn_w_gate", "ffn_w_up", "ffn_w_down"]
    small_names = ["a_norm_g", "conv_b1", "conv_dw", "conv_dw_b", "conv_ln_g", "conv_ln_b", "conv_b2", "kv_norm_g",
                   "b_norm_g", "ffn_norm_g", "final_norm_g"]
    table = {n: big_out[i] for i, n in enumerate(big_names)}
    table.update({n: small_out[i] for i, n in enumerate(small_names)})
    result = [loss, dx[None]]
    for kind in range(4):
        result += [table[n][kind] for n in order]
    return tuple(result)
```

```python
import functools

import jax
import jax.numpy as jnp
from jax import lax
from jax.experimental import pallas as pl
from jax.experimental.pallas import tpu as pltpu

ND = 8
HEAD = 128
BLK = 128
BRANCH_DILATIONS = (1, 4, 16)
CONV_W = 31
CONV_PAD = 32
RMS_EPS = 1e-6
LN_EPS = 1e-5
LR, B1, B2, ADAM_EPS, WD, STEP = 0.001, 0.9, 0.999, 1e-08, 0.01, 10
VMEM_LIMIT = 56 * 1024 * 1024
RESIDENT_BYTES = 4 * 1024 * 1024

F32, BF16 = jnp.float32, jnp.bfloat16
SDS = jax.ShapeDtypeStruct
MESH = pl.DeviceIdType.MESH
ANY = pl.BlockSpec(memory_space=pl.ANY)

NN = (((1,), (0,)), ((), ()))
NT = (((1,), (1,)), ((), ()))
TN = (((0,), (0,)), ((), ()))


def _dot(a, b, dims):
    return lax.dot_general(a, b, dims, preferred_element_type=F32)


def _cp(*sem):
    return pltpu.CompilerParams(dimension_semantics=sem, vmem_limit_bytes=VMEM_LIMIT)


def _slot(dev):
    return 4 * (dev % 2) + dev // 2


def _sigmoid(v):
    return 1.0 / (1.0 + jnp.exp(-v))


class _Job:
    def __init__(self, ins, out_shapes, alias, nsem, nlocal, make):
        self.ins, self.out_shapes, self.alias = list(ins), list(out_shapes), dict(alias)
        self.nsem, self.nlocal, self.make = nsem, nlocal, make
        self.result = None


def _coords():
    return lax.axis_index("x"), lax.axis_index("y"), lax.axis_index("c")


def _remote(src, dst, send, recv, k, to):
    return pltpu.make_async_remote_copy(src_ref=src, dst_ref=dst, send_sem=send.at[k], recv_sem=recv.at[k],
                                        device_id=to, device_id_type=MESH)


def _job_gather_send(shards):
    n = len(shards)

    def make(ins, outs, send, recv, local):
        x, y, c = _coords()
        targets = [(x, y, 1 - c), (1 - x, y, c), (x, 1 - y, c), (1 - x, 1 - y, c)]
        cps = []
        for a in range(n):
            dst = outs[a].at[4 * x + 2 * y + c]
            cps.append(pltpu.make_async_copy(ins[a], dst, local.at[a]))
            cps += [_remote(ins[a], dst, send, recv, 4 * a + k, t) for k, t in enumerate(targets)]
        return cps

    return _Job(shards, [SDS((ND,) + s.shape, s.dtype) for s in shards], {}, 4 * n, n, make)


def _job_gather_send_rows(shard, part, nparts, prev=None):
    rows = shard.shape[0] // nparts

    def make(ins, outs, send, recv, local):
        x, y, c = _coords()
        targets = [(x, y, 1 - c), (1 - x, y, c), (x, 1 - y, c), (1 - x, 1 - y, c)]
        src = ins[0].at[pl.ds(part * rows, rows)]
        dst = outs[0].at[4 * x + 2 * y + c].at[pl.ds(part * rows, rows)]
        return [pltpu.make_async_copy(src, dst, local.at[0])] + [
            _remote(src, dst, send, recv, k, t) for k, t in enumerate(targets)]

    ins = [shard] if prev is None else [shard, prev]
    return _Job(ins, [SDS((ND,) + shard.shape, shard.dtype)], {} if prev is None else {1: 0}, 4, 1, make)


def _job_gather_forward(gathered):
    n = len(gathered)

    def make(ins, outs, send, recv, local):
        x, y, c = _coords()
        cps = []
        for a in range(n):
            for k, (px, py) in enumerate([(1 - x, y), (x, 1 - y), (1 - x, 1 - y)]):
                blk = outs[a].at[4 * px + 2 * py + c]
                cps.append(_remote(blk, blk, send, recv, 3 * a + k, (x, y, 1 - c)))
        return cps

    return _Job(gathered, [SDS(g.shape, g.dtype) for g in gathered], {i: i for i in range(n)}, 3 * n, 0, make)


def _job_scatter_sibling(grads):
    n = len(grads)

    def make(ins, outs, send, recv, local):
        x, y, c = _coords()
        return [_remote(ins[a].at[pl.ds(4 * (1 - c), 4)], outs[a], send, recv, a, (x, y, 1 - c)) for a in range(n)]

    return _Job(grads, [SDS((4,) + g.shape[1:], g.dtype) for g in grads], {}, n, 0, make)


def _job_scatter_cross(sums):
    n = len(sums)

    def make(ins, outs, send, recv, local):
        x, y, c = _coords()
        chips = [(1 - x, y), (x, 1 - y), (1 - x, 1 - y)]
        return [_remote(ins[a].at[2 * px + py], outs[a].at[k], send, recv, 3 * a + k, (px, py, c))
                for a in range(n) for k, (px, py) in enumerate(chips)]

    return _Job(sums, [SDS((3,) + t.shape[1:], t.dtype) for t in sums], {}, 3 * n, 0, make)


def _pc(body, *, name, grid, in_specs, out_specs, out_shape, args, scratch=(), sem=(), alias=None, jobs=()):
    jobs = list(jobs)
    n_in, n_out, n_scr = len(in_specs), len(out_shape), len(scratch)
    aliases = dict(alias or {})
    job_args, job_shapes, job_scratch = [], [], []
    for j in jobs:
        for src, dst in j.alias.items():
            aliases[n_in + len(job_args) + src] = n_out + len(job_shapes) + dst
        job_args += j.ins
        job_shapes += j.out_shapes
        job_scratch += [pltpu.SemaphoreType.DMA((j.nsem,)), pltpu.SemaphoreType.DMA((j.nsem,)),
                        pltpu.SemaphoreType.DMA((max(j.nlocal, 1),))]

    def wrapped(*refs):
        ins = refs[:n_in]
        p = n_in + len(job_args)
        outs = refs[p:p + n_out]
        p += n_out + len(job_shapes)
        scr = refs[p:p + n_scr]
        sems = refs[p + n_scr:]
        copies = []
        pi, po = n_in, n_in + len(job_args) + n_out
        for k, j in enumerate(jobs):
            copies += j.make(refs[pi:pi + len(j.ins)], refs[po:po + len(j.out_shapes)], *sems[3 * k:3 * k + 3])
            pi += len(j.ins)
            po += len(j.out_shapes)
        gridded = bool(copies) and bool(grid)
        if gridded:
            ids = [pl.program_id(i) for i in range(len(grid))]
            first = functools.reduce(jnp.logical_and, [i == 0 for i in ids])
            last = functools.reduce(jnp.logical_and, [i == g - 1 for i, g in zip(ids, grid)])

            @pl.when(first)
            def _():
                for cp in copies:
                    cp.start()
        else:
            for cp in copies:
                cp.start()
        body(*ins, *outs, *scr)
        if gridded:
            @pl.when(last)
            def _():
                for cp in copies:
                    cp.wait()
        else:
            for cp in copies:
                cp.wait()

    kw = dict(grid=grid) if grid else {}
    semantics = ["arbitrary"] * len(grid) if jobs else list(sem)
    res = pl.pallas_call(
        wrapped, name=name, in_specs=list(in_specs) + [ANY] * len(job_args),
        out_specs=list(out_specs) + [ANY] * len(job_shapes), out_shape=list(out_shape) + job_shapes,
        scratch_shapes=list(scratch) + job_scratch, input_output_aliases=aliases,
        compiler_params=_cp(*semantics), **kw)(*args, *job_args)
    p = n_out
    for j in jobs:
        j.result = list(res[p:p + len(j.out_shapes)])
        p += len(j.out_shapes)
    return list(res[:n_out])


def _comm_call(jobs, name):
    _pc(lambda: None, name=name, grid=(), in_specs=[], out_specs=[], out_shape=[], args=[], jobs=jobs)


def _all_gather(arrs, name):
    n = len(arrs)

    def body(*refs):
        ins, outs = refs[:n], refs[n:2 * n]
        send_sems, recv_sems, local_sems = refs[2 * n:]
        x, y, c = lax.axis_index("x"), lax.axis_index("y"), lax.axis_index("c")
        me, sib = (x, y, c), (x, y, 1 - c)
        chips = [(1 - x, y), (x, 1 - y), (1 - x, 1 - y)]

        def copy(a, k, block, to, src=None):
            dst = outs[a].at[4 * block[0] + 2 * block[1] + block[2]]
            return pltpu.make_async_remote_copy(
                src_ref=dst if src is None else src, dst_ref=dst,
                send_sem=send_sems.at[7 * a + k], recv_sem=recv_sems.at[7 * a + k],
                device_id=to, device_id_type=MESH)

        mine = [pltpu.make_async_copy(ins[a], outs[a].at[4 * x + 2 * y + c], local_sems.at[a]) for a in range(n)]
        for cp in mine:
            cp.start()
        first = []
        for a in range(n):
            first.append(copy(a, 0, me, sib, src=ins[a]))
            first += [copy(a, 1 + j, me, (*chip, c), src=ins[a]) for j, chip in enumerate(chips)]
        for cp in first:
            cp.start()
        passed = []
        for a in range(n):
            for j, chip in enumerate(chips):
                copy(a, 1 + j, (*chip, c), me).wait_recv()
                fwd = copy(a, 4 + j, (*chip, c), sib)
                fwd.start()
                passed.append(fwd)
        for a in range(n):
            copy(a, 0, sib, me).wait_recv()
            for j, chip in enumerate(chips):
                copy(a, 4 + j, (*chip, 1 - c), me).wait_recv()
        for cp in first + passed:
            cp.wait_send()
        for cp in mine:
            cp.wait()

    return pl.pallas_call(
        body, name=name,
        out_shape=[SDS((ND,) + a.shape, a.dtype) for a in arrs],
        in_specs=[ANY] * n, out_specs=[ANY] * n,
        scratch_shapes=[pltpu.SemaphoreType.DMA((7 * n,)), pltpu.SemaphoreType.DMA((7 * n,)),
                        pltpu.SemaphoreType.DMA((n,))],
    )(*arrs)


ELEMENTWISE_TILE_BYTES = 3 * 512 * 1024


def _row_tile(rows, cols):
    fits = [t for t in range(16, rows + 1, 16) if rows % t == 0 and 4 * t * cols <= ELEMENTWISE_TILE_BYTES]
    return max(fits)


def _rs_add(g, r1, c_idx, name):
    _, rows, cols = g.shape

    def body(c_ref, g_ref, r_ref, o_ref):
        o_ref[...] = (g_ref[...].astype(F32) + r_ref[...].astype(F32)).astype(o_ref.dtype)

    return pl.pallas_call(
        body, name=name,
        grid_spec=pltpu.PrefetchScalarGridSpec(
            num_scalar_prefetch=1, grid=(4,),
            in_specs=[pl.BlockSpec((1, rows, cols), lambda q, c: (4 * c[0] + q, 0, 0)),
                      pl.BlockSpec((1, rows, cols), lambda q, c: (q, 0, 0))],
            out_specs=pl.BlockSpec((1, rows, cols), lambda q, c: (q, 0, 0))),
        out_shape=SDS((4, rows, cols), g.dtype),
        compiler_params=_cp("parallel"),
    )(c_idx, g, r1)


def _adam_math(w, g, m, v):
    m = B1 * m + (1.0 - B1) * g
    v = B2 * v + (1.0 - B2) * (g * g)
    m_hat = m / (1.0 - B1 ** STEP)
    v_hat = v / (1.0 - B2 ** STEP)
    delta = -LR * (m_hat / (jnp.sqrt(v_hat) + ADAM_EPS) + WD * w)
    return delta, m, v


def _adamw_big(w, m, v, t, r2, q_idx, name, part=0, prev=None):
    _, rows, cols = t.shape
    tr = _row_tile(rows, cols)
    nblk = rows // tr

    def body(q_ref, w_ref, m_ref, v_ref, t_ref, r_ref, *outs):
        g_out, d_out, m_out, v_out = outs[-4:]
        g = t_ref[0].astype(F32)
        for k in range(3):
            g = g + r_ref[k].astype(F32)
        d, mn, vn = _adam_math(w_ref[...], g, m_ref[...], v_ref[...])
        g_out[...], d_out[...], m_out[...], v_out[...] = g, d, mn, vn

    blk = pl.BlockSpec((tr, cols), lambda i, q: (part * nblk + i, 0))
    specs = [blk, blk, blk, pl.BlockSpec((1, tr, cols), lambda i, q: (q[0], i, 0)),
             pl.BlockSpec((3, tr, cols), lambda i, q: (0, i, 0))]
    ins = [q_idx, w, m, v, t, r2]
    alias = {}
    if prev is not None:
        specs += [ANY] * 4
        alias = {6 + k: k for k in range(4)}
        ins += list(prev)
    return pl.pallas_call(
        body, name=name,
        grid_spec=pltpu.PrefetchScalarGridSpec(num_scalar_prefetch=1, grid=(nblk,), in_specs=specs, out_specs=[blk] * 4),
        out_shape=[SDS(w.shape, F32)] * 4, input_output_aliases=alias,
        compiler_params=_cp("parallel"))(*ins)


def _sum_devices(g, name):
    _, rows, cols = g.shape

    def body(g_ref, o_ref):
        acc = g_ref[0]
        for k in range(1, ND):
            acc = acc + g_ref[k]
        o_ref[...] = acc

    return pl.pallas_call(body, name=name, out_shape=SDS((rows, cols), F32))(g)


def _adamw_small(w, g, m, v, name):
    def body(w_ref, g_ref, m_ref, v_ref, d_out, m_out, v_out):
        d, mn, vn = _adam_math(w_ref[...], g_ref[...], m_ref[...], v_ref[...])
        d_out[...], m_out[...], v_out[...] = d, mn, vn

    return pl.pallas_call(body, name=name, out_shape=[SDS(w.shape, F32)] * 3)(w, g, m, v)


ROWS = 256


def _rms_stats(x):
    r = lax.rsqrt(jnp.mean(x * x, axis=-1, keepdims=True) + RMS_EPS)
    return x * r, r


def _rms_fwd(x, gains, name, jobs=()):
    s, d = x.shape
    n = len(gains)

    def body(x_ref, *refs):
        xh, _ = _rms_stats(x_ref[...])
        for g_ref, o_ref in zip(refs[:n], refs[n:]):
            o_ref[...] = (xh * g_ref[...]).astype(BF16)

    row = pl.BlockSpec((ROWS, d), lambda i: (i, 0))
    vec = pl.BlockSpec((1, d), lambda i: (0, 0))
    return _pc(body, name=name, grid=(s // ROWS,), in_specs=[row] + [vec] * n, out_specs=[row] * n,
               out_shape=[SDS((s, d), BF16)] * n, sem=("parallel",), args=(x, *gains), jobs=jobs)


def _rms_bwd_rows(xh, r, gain, dy):
    u = dy * gain
    return r * (u - xh * jnp.mean(u * xh, axis=-1, keepdims=True))


def _rms_bwd(x, pairs, dres, name, colsum=False):
    s, d = x.shape
    n = len(pairs)

    def body(x_ref, dres_ref, *refs):
        g_refs, dy_refs = refs[:n], refs[n:2 * n]
        dx_ref, dxb_ref = refs[2 * n], refs[2 * n + 1]
        dg_refs = refs[2 * n + 2:2 * n + 2 + n]
        cs_ref = refs[-1] if colsum else None
        first = pl.program_id(0) == 0
        xh, r = _rms_stats(x_ref[...])
        dx = dres_ref[...]
        for g_ref, dy_ref, dg_ref in zip(g_refs, dy_refs, dg_refs):
            dy = dy_ref[...]
            dx = dx + _rms_bwd_rows(xh, r, g_ref[...], dy)

            @pl.when(first)
            def _():
                dg_ref[...] = jnp.zeros_like(dg_ref)
            dg_ref[...] += jnp.sum(dy * xh, axis=0, keepdims=True)
        dx_ref[...] = dx
        dxb_ref[...] = dx.astype(BF16)
        if colsum:
            @pl.when(first)
            def _():
                cs_ref[...] = jnp.zeros_like(cs_ref)
            cs_ref[...] += jnp.sum(dx, axis=0, keepdims=True)

    row = pl.BlockSpec((ROWS, d), lambda i: (i, 0))
    vec = pl.BlockSpec((1, d), lambda i: (0, 0))
    nvec = n + (1 if colsum else 0)
    outs = pl.pallas_call(
        body, name=name, grid=(s // ROWS,),
        in_specs=[row, row] + [vec] * n + [row] * n,
        out_specs=[row, row] + [vec] * nvec,
        out_shape=[SDS((s, d), F32), SDS((s, d), BF16)] + [SDS((1, d), F32)] * nvec,
        compiler_params=_cp("arbitrary"),
    )(x, dres, *[p[0] for p in pairs], *[p[1] for p in pairs])
    return outs


def _final_loss(h, target, gain):
    s, d = h.shape

    def body(h_ref, t_ref, g_ref, dh_ref, dhb_ref, dg_ref, loss_ref):
        first = pl.program_id(0) == 0
        xh, r = _rms_stats(h_ref[...])
        gain_v = g_ref[...]
        e = xh * gain_v - t_ref[...]
        dy = e * (1.0 / d)
        dx = _rms_bwd_rows(xh, r, gain_v, dy)
        dh_ref[...] = dx
        dhb_ref[...] = dx.astype(BF16)

        @pl.when(first)
        def _():
            dg_ref[...] = jnp.zeros_like(dg_ref)
            loss_ref[...] = jnp.zeros_like(loss_ref)
        dg_ref[...] += jnp.sum(dy * xh, axis=0, keepdims=True)
        loss_ref[...] += jnp.full((1, 128), 0.5 / d, F32) * jnp.sum(e * e)

    row = pl.BlockSpec((ROWS, d), lambda i: (i, 0))
    vec = pl.BlockSpec((1, d), lambda i: (0, 0))
    return pl.pallas_call(
        body, name="final_loss", grid=(s // ROWS,),
        in_specs=[row, row, vec], out_specs=[row, row, vec, pl.BlockSpec((1, 128), lambda i: (0, 0))],
        out_shape=[SDS((s, d), F32), SDS((s, d), BF16), SDS((1, d), F32), SDS((1, 128), F32)],
        compiler_params=_cp("arbitrary"))(h, target, gain)


CT = 128


def _ln_stats(cv):
    mu = jnp.mean(cv, axis=-1, keepdims=True)
    xc = cv - mu
    rstd = lax.rsqrt(jnp.mean(xc * xc, axis=-1, keepdims=True) + LN_EPS)
    return xc * rstd, rstd


def _conv_fwd(glu, dw, dwb, lng, lnb, jobs=()):
    s, d = glu.shape
    hb = CT // CONV_PAD

    def body(x_ref, halo_ref, dw_ref, dwb_ref, lng_ref, lnb_ref, c_ref, s_ref):
        keep = (pl.program_id(0) > 0).astype(F32)

        def chunk(ci, carry):
            ls = pl.ds(pl.multiple_of(ci * 128, 128), 128)
            xf = jnp.concatenate([halo_ref[:, ls] * keep, x_ref[:, ls]], axis=0)
            acc = jnp.zeros((CT, 128), F32)
            for k in range(CONV_W):
                sh = CONV_W - 1 - k
                xs = pltpu.roll(xf, sh, 0) if sh else xf
                acc = acc + dw_ref[pl.ds(k, 1), ls] * xs[CONV_PAD:]
            c_ref[:, ls] = acc + dwb_ref[:, ls]
            return carry

        lax.fori_loop(0, d // 128, chunk, 0)
        xh, _ = _ln_stats(c_ref[...])
        yv = xh * lng_ref[...] + lnb_ref[...]
        s_ref[...] = (yv * _sigmoid(yv)).astype(BF16)

    row = pl.BlockSpec((CT, d), lambda i: (i, 0))
    halo = pl.BlockSpec((CONV_PAD, d), lambda i: (jnp.maximum(i * hb - 1, 0), 0))
    vec = pl.BlockSpec((1, d), lambda i: (0, 0))
    taps = pl.BlockSpec((CONV_PAD, d), lambda i: (0, 0))
    return _pc(
        body, name="conv_fwd", grid=(s // CT,),
        in_specs=[row, halo, taps, vec, vec, vec], out_specs=[row, row],
        out_shape=[SDS((s, d), F32), SDS((s, d), BF16)], sem=("parallel",),
        args=(glu, glu, dw, dwb, lng, lnb), jobs=jobs)


def _ln_bwd(ds, cv, lng, lnb):
    s, d = cv.shape

    def body(ds_ref, c_ref, g_ref, b_ref, dc_ref, dg_ref, db_ref):
        first = pl.program_id(0) == 0
        xh, rstd = _ln_stats(c_ref[...])
        gv = g_ref[...]
        yv = xh * gv + b_ref[...]
        sg = _sigmoid(yv)
        dln = ds_ref[...] * (sg * (1.0 + yv * (1.0 - sg)))
        dxh = dln * gv
        dc_ref[...] = rstd * (dxh - jnp.mean(dxh, axis=-1, keepdims=True)
                              - xh * jnp.mean(dxh * xh, axis=-1, keepdims=True))

        @pl.when(first)
        def _():
            dg_ref[...] = jnp.zeros_like(dg_ref)
            db_ref[...] = jnp.zeros_like(db_ref)
        dg_ref[...] += jnp.sum(dln * xh, axis=0, keepdims=True)
        db_ref[...] += jnp.sum(dln, axis=0, keepdims=True)

    row = pl.BlockSpec((ROWS, d), lambda i: (i, 0))
    vec = pl.BlockSpec((1, d), lambda i: (0, 0))
    return pl.pallas_call(
        body, name="ln_bwd", grid=(s // ROWS,), in_specs=[row, row, vec, vec], out_specs=[row, vec, vec],
        out_shape=[SDS((s, d), F32), SDS((1, d), F32), SDS((1, d), F32)],
        compiler_params=_cp("arbitrary"))(ds, cv, lng, lnb)


def _conv_bwd(dc, glu, ua, ug, dw, jobs=()):
    s, d = dc.shape
    hb = CT // CONV_PAD
    nsteps = s // CT
    full = CT + CONV_PAD

    def body(dc_ref, dcn_ref, x_ref, xp_ref, ua_ref, ug_ref, dw_ref, du_ref, ddw_ref, ddwb_ref, db1_ref):
        i = pl.program_id(0)
        keep_prev = (i > 0).astype(F32)
        keep_next = (i < nsteps - 1).astype(F32)

        @pl.when(i == 0)
        def _():
            ddw_ref[...] = jnp.zeros_like(ddw_ref)
            ddwb_ref[...] = jnp.zeros_like(ddwb_ref)
            db1_ref[...] = jnp.zeros_like(db1_ref)

        def chunk(ci, carry):
            off = pl.multiple_of(ci * 128, 128)
            ls = pl.ds(off, 128)
            ls2 = pl.ds(pl.multiple_of(d + ci * 128, 128), 128)
            dcc = dc_ref[:, ls]
            dcf = jnp.concatenate([dcc, dcn_ref[:, ls] * keep_next], axis=0)
            xf = jnp.concatenate([xp_ref[:, ls] * keep_prev, x_ref[:, ls]], axis=0)
            dglu = jnp.zeros((CT, 128), F32)
            for k in range(CONV_W):
                sh = CONV_W - 1 - k
                dshift = pltpu.roll(dcf, full - sh, 0) if sh else dcf
                dglu = dglu + dw_ref[pl.ds(k, 1), ls] * dshift[:CT]
                xs = pltpu.roll(xf, sh, 0) if sh else xf
                ddw_ref[pl.ds(k, 1), ls] += jnp.sum(dcc * xs[CONV_PAD:], axis=0, keepdims=True)
            ddwb_ref[:, ls] += jnp.sum(dcc, axis=0, keepdims=True)
            av, gv = ua_ref[:, ls], ug_ref[:, ls]
            sg = _sigmoid(gv)
            da = dglu * sg
            dgt = dglu * av * sg * (1.0 - sg)
            du_ref[:, ls] = da.astype(BF16)
            du_ref[:, ls2] = dgt.astype(BF16)
            db1_ref[:, ls] += jnp.sum(da, axis=0, keepdims=True)
            db1_ref[:, ls2] += jnp.sum(dgt, axis=0, keepdims=True)
            return carry

        lax.fori_loop(0, d // 128, chunk, 0)

    row = pl.BlockSpec((CT, d), lambda i: (i, 0))
    prev = pl.BlockSpec((CONV_PAD, d), lambda i: (jnp.maximum(i * hb - 1, 0), 0))
    nxt = pl.BlockSpec((CONV_PAD, d), lambda i: (jnp.minimum((i + 1) * hb, s // CONV_PAD - 1), 0))
    taps = pl.BlockSpec((CONV_PAD, d), lambda i: (0, 0))
    return _pc(
        body, name="conv_bwd", grid=(nsteps,),
        in_specs=[row, nxt, row, prev, row, row, taps],
        out_specs=[pl.BlockSpec((CT, 2 * d), lambda i: (i, 0)), taps, pl.BlockSpec((1, d), lambda i: (0, 0)),
                   pl.BlockSpec((1, 2 * d), lambda i: (0, 0))],
        out_shape=[SDS((s, 2 * d), BF16), SDS((CONV_PAD, d), F32), SDS((1, d), F32), SDS((1, 2 * d), F32)],
        sem=("arbitrary",), args=(dc, dc, glu, glu, ua, ug, dw), jobs=jobs)


TM = 1024
TS = 1024


def _glu_mm(n1, w1g, b1, jobs=()):
    s, d = n1.shape
    cw = w1g.shape[2]
    half = ND // 2

    def body(a_ref, wa_ref, wg_ref, ba_ref, bg_ref, ua_ref, ug_ref, glu_ref):
        a = a_ref[...]
        ua = _dot(a, wa_ref[0], NN) + ba_ref[...]
        ug = _dot(a, wg_ref[0], NN) + bg_ref[...]
        ua_ref[...], ug_ref[...] = ua, ug
        glu_ref[...] = ua * _sigmoid(ug)

    out = pl.BlockSpec((TM, cw), lambda m, i: (m, i))
    return _pc(
        body, name="glu_mm", grid=(s // TM, half),
        in_specs=[pl.BlockSpec((TM, d), lambda m, i: (m, 0)),
                  pl.BlockSpec((1, d, cw), lambda m, i: (i, 0, 0)),
                  pl.BlockSpec((1, d, cw), lambda m, i: (i + half, 0, 0)),
                  pl.BlockSpec((1, cw), lambda m, i: (0, i)),
                  pl.BlockSpec((1, cw), lambda m, i: (0, i + half))],
        out_specs=[out, out, out], out_shape=[SDS((s, d), F32)] * 3,
        sem=("parallel", "arbitrary"), args=(n1, w1g, w1g, b1, b1), jobs=jobs)


def _mm_rows(a, wg, name, res=None, bias=None, out_dtype=F32, tn=512, branches=False, jobs=()):
    s, kdim = a.shape
    _, kc, n = wg.shape
    assert kc * ND == kdim
    nx = 2 + (res is not None) + (bias is not None)

    def body(*refs):
        acc = _dot(refs[0][...], refs[1][...].reshape(kdim, tn), NN)
        for extra in refs[2:nx]:
            acc = acc + extra[...]
        refs[nx][...] = acc.astype(out_dtype)
        if branches:
            scr = refs[-1]
            _stage(scr, acc)
            for o_ref, dil in zip(refs[nx + 1:], SPLIT_DILATIONS):
                _split_rows(scr, o_ref, dil)

    ins, specs = [a, wg], [pl.BlockSpec((TM, kdim), lambda m, j: (m, 0)), pl.BlockSpec((ND, kc, tn), lambda m, j: (0, 0, j))]
    if res is not None:
        ins.append(res)
        specs.append(pl.BlockSpec((TM, tn), lambda m, j: (m, j)))
    if bias is not None:
        ins.append(bias)
        specs.append(pl.BlockSpec((1, tn), lambda m, j: (0, j)))
    out_specs, out_shape, scratch = [pl.BlockSpec((TM, tn), lambda m, j: (m, j))], [SDS((s, n), out_dtype)], []
    if branches:
        out_specs += _branch_specs(TM, tn, lambda dil, m, j: (0, m, j))
        out_shape += [SDS((dil, s // dil, n), out_dtype) for dil in SPLIT_DILATIONS]
        scratch = [pltpu.VMEM((tn // 128, TM, 128), F32)]
    outs = _pc(body, name=name, grid=(s // TM, n // tn), in_specs=specs, out_specs=out_specs, out_shape=out_shape,
               scratch=scratch, sem=("parallel", "arbitrary"), args=ins, jobs=jobs)
    return [outs[0][None]] + outs[1:] if branches else outs[0]


def _swiglu_mm(n2, wgg, wug, name, jobs=()):
    s, d = n2.shape
    fc = wgg.shape[2]

    def body(a_ref, wg_ref, wu_ref, g_ref, u_ref, act_ref):
        a = a_ref[...]
        g = _dot(a, wg_ref[0], NN)
        u = _dot(a, wu_ref[0], NN)
        g_ref[0], u_ref[0] = g.astype(BF16), u.astype(BF16)
        act_ref[0] = (g * _sigmoid(g) * u).astype(BF16)

    wspec = pl.BlockSpec((1, d, fc), lambda m, j: (j, 0, 0))
    out = pl.BlockSpec((1, TM, fc), lambda m, j: (j, m, 0))
    return _pc(
        body, name=name, grid=(s // TM, ND),
        in_specs=[pl.BlockSpec((TM, d), lambda m, j: (m, 0)), wspec, wspec],
        out_specs=[out, out, out], out_shape=[SDS((ND, s, fc), BF16)] * 3,
        sem=("parallel", "arbitrary"), args=(n2, wgg, wug), jobs=jobs)


def _down_mm(act, wdg, res, name, jobs=()):
    _, s, fc = act.shape
    d = wdg.shape[2]

    def body(a_ref, w_ref, r_ref, o_ref):
        @pl.when(pl.program_id(1) == 0)
        def _():
            o_ref[...] = r_ref[...]
        o_ref[...] += _dot(a_ref[0], w_ref[0], NN)

    row = pl.BlockSpec((TM, d), lambda m, j: (m, 0))
    return _pc(
        body, name=name, grid=(s // TM, ND),
        in_specs=[pl.BlockSpec((1, TM, fc), lambda m, j: (j, m, 0)),
                  pl.BlockSpec((1, fc, d), lambda m, j: (j, 0, 0)), row],
        out_specs=[row], out_shape=[SDS((s, d), F32)],
        sem=("parallel", "arbitrary"), args=(act, wdg, res), jobs=jobs)[0]


def _dact_mm(dh, wdg, gate, up, name, jobs=()):
    s, d = dh.shape
    fc = wdg.shape[1]

    def body(a_ref, w_ref, g_ref, u_ref, dg_ref, du_ref):
        dact = _dot(a_ref[...], w_ref[0], NT)
        g, u = g_ref[0].astype(F32), u_ref[0].astype(F32)
        sg = _sigmoid(g)
        du_ref[0] = (dact * (g * sg)).astype(BF16)
        dg_ref[0] = (dact * u * (sg * (1.0 + g * (1.0 - sg)))).astype(BF16)

    blk = pl.BlockSpec((1, TM, fc), lambda m, j: (j, m, 0))
    return _pc(
        body, name=name, grid=(s // TM, ND),
        in_specs=[pl.BlockSpec((TM, d), lambda m, j: (m, 0)),
                  pl.BlockSpec((1, fc, d), lambda m, j: (j, 0, 0)), blk, blk],
        out_specs=[blk, blk], out_shape=[SDS((ND, s, fc), BF16)] * 2,
        sem=("parallel", "arbitrary"), args=(dh, wdg, gate, up), jobs=jobs)


def _dwd_mm(act, dh, name, jobs=()):
    _, s, fc = act.shape
    d = dh.shape[1]
    nk = s // TS

    def body(a_ref, b_ref, o_ref, acc):
        k = pl.program_id(1)

        @pl.when(k == 0)
        def _():
            acc[...] = jnp.zeros_like(acc)
        acc[...] += _dot(a_ref[0], b_ref[...], TN)

        @pl.when(k == nk - 1)
        def _():
            o_ref[0] = acc[...].astype(BF16)

    return _pc(
        body, name=name, grid=(ND, nk),
        in_specs=[pl.BlockSpec((1, TS, fc), lambda j, k: (j, k, 0)), pl.BlockSpec((TS, d), lambda j, k: (k, 0))],
        out_specs=[pl.BlockSpec((1, fc, d), lambda j, k: (_slot(j), 0, 0))],
        out_shape=[SDS((ND, fc, d), BF16)], scratch=[pltpu.VMEM((fc, d), F32)],
        sem=("parallel", "arbitrary"), args=(act, dh), jobs=jobs)[0]


def _dwgu_mm(n2, dgate, dup, name, jobs=()):
    s, d = n2.shape
    fc = dgate.shape[2]
    nk = s // TS

    def body(a_ref, g_ref, u_ref, og_ref, ou_ref, accg, accu):
        k = pl.program_id(1)

        @pl.when(k == 0)
        def _():
            accg[...] = jnp.zeros_like(accg)
            accu[...] = jnp.zeros_like(accu)
        a = a_ref[...]
        accg[...] += _dot(g_ref[0], a, TN)
        accu[...] += _dot(u_ref[0], a, TN)

        @pl.when(k == nk - 1)
        def _():
            og_ref[0] = accg[...].astype(BF16)
            ou_ref[0] = accu[...].astype(BF16)

    blk = pl.BlockSpec((1, TS, fc), lambda j, k: (j, k, 0))
    out = pl.BlockSpec((1, fc, d), lambda j, k: (_slot(j), 0, 0))
    return _pc(
        body, name=name, grid=(ND, nk),
        in_specs=[pl.BlockSpec((TS, d), lambda j, k: (k, 0)), blk, blk], out_specs=[out, out],
        out_shape=[SDS((ND, fc, d), BF16)] * 2,
        scratch=[pltpu.VMEM((fc, d), F32), pltpu.VMEM((fc, d), F32)],
        sem=("parallel", "arbitrary"), args=(n2, dgate, dup), jobs=jobs)


def _dn_ffn_mm(dgate, dup, wgg, wug, name, jobs=()):
    _, s, fc = dgate.shape
    d = wgg.shape[1]

    def body(g_ref, u_ref, wg_ref, wu_ref, o_ref):
        j = pl.program_id(1)

        @pl.when(j == 0)
        def _():
            o_ref[...] = jnp.zeros_like(o_ref)
        o_ref[...] += _dot(g_ref[0], wg_ref[0], NT) + _dot(u_ref[0], wu_ref[0], NT)

    blk = pl.BlockSpec((1, TM, fc), lambda m, j: (j, m, 0))
    wspec = pl.BlockSpec((1, d, fc), lambda m, j: (j, 0, 0))
    return _pc(
        body, name=name, grid=(s // TM, ND), in_specs=[blk, blk, wspec, wspec],
        out_specs=[pl.BlockSpec((TM, d), lambda m, j: (m, 0))], out_shape=[SDS((s, d), F32)],
        sem=("parallel", "arbitrary"), args=(dgate, dup, wgg, wug), jobs=jobs)[0]


def _mm_rows_t(pairs, name, out_dtype, branches=False, jobs=()):
    s, n = pairs[0][0].shape
    _, kc, _ = pairs[0][1].shape
    np_ = len(pairs)
    grp = ND // 2
    wide = grp * kc

    def body(*refs):
        o_ref = refs[2 * np_]
        for i in range(grp):
            acc = None
            for p in range(np_):
                t = _dot(refs[p][...], refs[np_ + p][i], NT)
                acc = t if acc is None else acc + t
            o_ref[:, kc * i:kc * (i + 1)] = acc.astype(out_dtype)
            if branches:
                for c, ls in enumerate(_lane_chunks(kc)):
                    refs[-1][i * (kc // 128) + c] = acc[:, ls]
        if branches:
            for b_ref, dil in zip(refs[2 * np_ + 1:], SPLIT_DILATIONS):
                _split_rows(refs[-1], b_ref, dil)

    out_specs, out_shape, scratch = [pl.BlockSpec((TM, wide), lambda m, j: (m, j))], [SDS((s, kc * ND), out_dtype)], []
    if branches:
        out_specs += _branch_specs(TM, wide, lambda dil, m, j: (0, m, j))
        out_shape += [SDS((dil, s // dil, kc * ND), out_dtype) for dil in SPLIT_DILATIONS]
        scratch = [pltpu.VMEM((wide // 128, TM, 128), F32)]
    outs = _pc(
        body, name=name, grid=(s // TM, ND // grp),
        in_specs=[pl.BlockSpec((TM, n), lambda m, j: (m, 0))] * np_ + [pl.BlockSpec((grp, kc, n), lambda m, j: (j, 0, 0))] * np_,
        out_specs=out_specs, out_shape=out_shape, scratch=scratch,
        sem=("parallel", "arbitrary"), args=[p[0] for p in pairs] + [p[1] for p in pairs], jobs=jobs)
    return [outs[0][None]] + outs[1:] if branches else outs[0]


def _dw_rows_mm(a, b, name):
    s, kdim = a.shape
    n = b.shape[1]
    kc = kdim // ND
    ts = TS // 2
    nk = s // ts

    def body(a_ref, b_ref, o_ref, acc):
        k = pl.program_id(0)

        @pl.when(k == 0)
        def _():
            acc[...] = jnp.zeros_like(acc)
        acc[...] += _dot(a_ref[...], b_ref[...], TN)

        @pl.when(k == nk - 1)
        def _():
            for dev in range(ND):
                o_ref[_slot(dev)] = acc[kc * dev:kc * (dev + 1), :].astype(BF16)

    return pl.pallas_call(
        body, name=name, grid=(nk,),
        in_specs=[pl.BlockSpec((ts, kdim), lambda k: (k, 0)), pl.BlockSpec((ts, n), lambda k: (k, 0))],
        out_specs=pl.BlockSpec((ND, kc, n), lambda k: (0, 0, 0)), out_shape=SDS((ND, kc, n), BF16),
        scratch_shapes=[pltpu.VMEM((kdim, n), F32)], compiler_params=_cp("arbitrary"))(a, b)


def _dw1_mm(n1, du, jobs=()):
    s, d = n1.shape
    cw = du.shape[1] // ND
    nk = s // TS

    def body(a_ref, b_ref, o_ref, acc):
        k = pl.program_id(1)

        @pl.when(k == 0)
        def _():
            acc[...] = jnp.zeros_like(acc)
        acc[...] += _dot(a_ref[...], b_ref[...], TN)

        @pl.when(k == nk - 1)
        def _():
            o_ref[0] = acc[...].astype(BF16)

    return _pc(
        body, name="dw1_mm", grid=(ND, nk),
        in_specs=[pl.BlockSpec((TS, d), lambda j, k: (k, 0)), pl.BlockSpec((TS, cw), lambda j, k: (k, j))],
        out_specs=[pl.BlockSpec((1, d, cw), lambda j, k: (_slot(j), 0, 0))], out_shape=[SDS((ND, d, cw), BF16)],
        scratch=[pltpu.VMEM((d, cw), F32)], sem=("parallel", "arbitrary"), args=(n1, du), jobs=jobs)[0]


def _dn1_mm(du, w1g, part, nparts, prev=None, jobs=()):
    s = du.shape[0]
    _, d, cw = w1g.shape
    steps = s // TM // nparts
    m0 = part * steps

    def body(a_ref, w_ref, *refs):
        o_ref = refs[-1]
        j = pl.program_id(1)

        @pl.when(j == 0)
        def _():
            o_ref[...] = jnp.zeros_like(o_ref)
        o_ref[...] += _dot(a_ref[...], w_ref[0], NT)

    ins = [du, w1g] if prev is None else [du, w1g, prev]
    specs = [pl.BlockSpec((TM, cw), lambda m, j: (m0 + m, j)), pl.BlockSpec((1, d, cw), lambda m, j: (j, 0, 0))]
    return _pc(
        body, name=f"dn1_mm{part}", grid=(steps, ND), in_specs=specs if prev is None else specs + [ANY],
        out_specs=[pl.BlockSpec((TM, d), lambda m, j: (m0 + m, 0))], out_shape=[SDS((s, d), F32)],
        alias=None if prev is None else {2: 0}, sem=("parallel", "arbitrary"), args=ins, jobs=jobs)[0]


NEG = -1e30


def _slopes(heads):
    return [2.0 ** (-8.0 * (h + 1) / heads) for h in range(heads)]


def _band(has_prev):
    qi = lax.broadcasted_iota(jnp.int32, (BLK, 2 * BLK), 0)
    ki = lax.broadcasted_iota(jnp.int32, (BLK, 2 * BLK), 1)
    j = qi - ki + BLK
    ok = (j >= 0) & (j <= BLK) & (has_prev | (ki >= BLK))
    return j.astype(F32), ok


SPLIT_DILATIONS = tuple(dil for dil in BRANCH_DILATIONS if dil > 1)


def _lane_chunks(w):
    return [slice(128 * c, 128 * (c + 1)) for c in range(w // 128)]


def _stage(scr, tile):
    for c, ls in enumerate(_lane_chunks(tile.shape[1])):
        scr[c] = tile[:, ls]


def _split_rows(scr, o_ref, dil):
    _, n, w = o_ref.shape
    for r in range(dil):
        for c, ls in enumerate(_lane_chunks(w)):
            o_ref[r, :, ls] = scr[c, pl.ds(r, n, stride=dil), :].astype(o_ref.dtype)


def _join_rows(i_ref, scr, dil):
    _, n, w = i_ref.shape
    for r in range(dil):
        for c, ls in enumerate(_lane_chunks(w)):
            scr[c, pl.ds(r, n, stride=dil), :] = i_ref[r, :, ls].astype(F32)


def _unstage(scr, w):
    return jnp.concatenate([scr[c] for c in range(w // 128)], axis=1)


def _branch_specs(rows, w, index):
    return [pl.BlockSpec((dil, rows // dil, w), functools.partial(index, dil)) for dil in SPLIT_DILATIONS]


def _attn_fwd(q, k, v, dil, jobs=()):
    _, l, d = q.shape
    heads = d // HEAD
    assert heads <= HEAD
    scale = HEAD ** -0.5
    slopes = _slopes(heads)

    def body(q_ref, kc_ref, kp_ref, vc_ref, vp_ref, o_ref, lse_ref):
        jf, ok = _band(pl.program_id(1) > 0)
        lane = lax.broadcasted_iota(jnp.int32, (BLK, HEAD), 1)
        lse = jnp.zeros((BLK, HEAD), F32)
        for h in range(heads):
            sl = slice(HEAD * h, HEAD * (h + 1))
            kh = jnp.concatenate([kp_ref[0, :, sl], kc_ref[0, :, sl]], axis=0)
            vh = jnp.concatenate([vp_ref[0, :, sl], vc_ref[0, :, sl]], axis=0)
            logits = jnp.where(ok, _dot(q_ref[0, :, sl], kh, NT) * scale + jf * (-slopes[h] * dil), NEG)
            m = jnp.max(logits, axis=-1, keepdims=True)
            p = jnp.exp(logits - m)
            den = jnp.sum(p, axis=-1, keepdims=True)
            o_ref[0, :, sl] = _dot(p.astype(BF16), vh, NN) / den
            lse = jnp.where(lane == h, m + jnp.log(den), lse)
        lse_ref[0] = lse

    cur = pl.BlockSpec((1, BLK, d), lambda r, b: (r, b, 0))
    prev = pl.BlockSpec((1, BLK, d), lambda r, b: (r, jnp.maximum(b - 1, 0), 0))
    return _pc(
        body, name=f"attn_fwd_d{dil}", grid=(dil, l // BLK),
        in_specs=[cur, cur, prev, cur, prev], out_specs=[cur, pl.BlockSpec((1, BLK, HEAD), lambda r, b: (r, b, 0))],
        out_shape=[SDS((dil, l, d), F32), SDS((dil, l, HEAD), F32)], sem=("parallel", "arbitrary"),
        args=(q, k, k, v, v), jobs=jobs)


def _attn_merge(outs, lses):
    _, s, d = outs[0].shape
    heads = d // HEAD
    nb = len(outs)
    nsplit = nb - 1

    def body(*refs):
        o_refs, l_refs = refs[:nb], refs[nb:2 * nb]
        att_refs, lse_refs = refs[2 * nb:3 * nb], refs[3 * nb:4 * nb]
        scr_o, scr_l, scr_att = refs[4 * nb:4 * nb + nsplit], refs[4 * nb + nsplit:4 * nb + 2 * nsplit], refs[-1]
        ls = [l_refs[0][...]]
        for k, dil in enumerate(SPLIT_DILATIONS):
            _join_rows(o_refs[1 + k], scr_o[k], dil)
            _join_rows(l_refs[1 + k], scr_l[k], dil)
            ls.append(scr_l[k][0])
        m = functools.reduce(jnp.maximum, ls)
        ws = [jnp.exp(v - m) for v in ls]
        den = functools.reduce(jnp.add, ws)
        ws = [w / den for w in ws]
        lse_refs[0][...] = m + jnp.log(den)
        scr_l[0][0] = m + jnp.log(den)
        for h in range(heads):
            sl = slice(HEAD * h, HEAD * (h + 1))
            slab = ws[0][:, h:h + 1] * o_refs[0][:, sl]
            for k in range(nsplit):
                slab = slab + ws[1 + k][:, h:h + 1] * scr_o[k][h]
            att_refs[0][:, sl] = slab.astype(BF16)
            scr_att[h] = slab
        for k, dil in enumerate(SPLIT_DILATIONS):
            _split_rows(scr_att, att_refs[1 + k], dil)
            _split_rows(scr_l[0], lse_refs[1 + k], dil)

    def specs(w):
        return [pl.BlockSpec((ROWS, w), lambda i: (i, 0))] + _branch_specs(ROWS, w, lambda dil, i: (0, i, 0))

    def shapes(w, dt):
        return [SDS((s, w), dt)] + [SDS((dil, s // dil, w), dt) for dil in SPLIT_DILATIONS]

    wide, narrow = pltpu.VMEM((heads, ROWS, 128), F32), pltpu.VMEM((1, ROWS, 128), F32)
    res = pl.pallas_call(
        body, name="attn_merge", grid=(s // ROWS,), in_specs=specs(d) + specs(HEAD), out_specs=specs(d) + specs(HEAD),
        out_shape=shapes(d, BF16) + shapes(HEAD, F32),
        scratch_shapes=[wide] * nsplit + [narrow] * nsplit + [wide],
        compiler_params=_cp("parallel"))(outs[0].reshape(s, d), *outs[1:], lses[0].reshape(s, HEAD), *lses[1:])
    return list(res[:nb]), list(res[nb:])


def _attn_bwd(q, k, v, do, o, lse, dil, jobs=()):
    _, l, d = q.shape
    nb = l // BLK
    heads = d // HEAD
    scale = HEAD ** -0.5
    slopes = _slopes(heads)
    whole = 2 * l * d <= RESIDENT_BYTES
    steps = nb if whole else nb + 1

    def body(q_ref, kc_ref, kp_ref, vc_ref, vp_ref, do_ref, o_ref, lse_ref, dq_ref, dk_ref, dv_ref, ck, cv):
        b = pl.program_id(1)
        rows = pl.ds(pl.multiple_of(jnp.maximum(b - 1, 0) * BLK, BLK), BLK) if whole else slice(None)

        @pl.when(b == 0)
        def _():
            ck[...] = jnp.zeros_like(ck)
            cv[...] = jnp.zeros_like(cv)

        @pl.when(b < nb)
        def _():
            jf, ok = _band(b > 0)
            for h in range(heads):
                sl = slice(HEAD * h, HEAD * (h + 1))
                qh, doh = q_ref[0, :, sl], do_ref[0, :, sl]
                kh = jnp.concatenate([kp_ref[0, :, sl], kc_ref[0, :, sl]], axis=0)
                vh = jnp.concatenate([vp_ref[0, :, sl], vc_ref[0, :, sl]], axis=0)
                lse_h = lse_ref[0, :, h:h + 1]
                delta = jnp.sum(doh.astype(F32) * o_ref[0, :, sl].astype(F32), axis=-1, keepdims=True)
                p = jnp.where(ok, jnp.exp(_dot(qh, kh, NT) * scale + jf * (-slopes[h] * dil) - lse_h), 0.0)
                ds = (p * (_dot(doh, vh, NT) - delta)).astype(BF16)
                dq_ref[0, :, sl] = (_dot(ds, kh, NN) * scale).astype(BF16)
                dk2 = _dot(ds, qh, TN) * scale
                dv2 = _dot(p.astype(BF16), doh, TN)
                dk_ref[0, rows, sl] = (ck[:, sl] + dk2[:BLK]).astype(BF16)
                dv_ref[0, rows, sl] = (cv[:, sl] + dv2[:BLK]).astype(BF16)
                ck[:, sl] = dk2[BLK:]
                cv[:, sl] = dv2[BLK:]

        @pl.when(b == steps - 1)
        def _():
            last = pl.ds((nb - 1) * BLK, BLK) if whole else slice(None)
            dk_ref[0, last, :] = ck[...].astype(BF16)
            dv_ref[0, last, :] = cv[...].astype(BF16)

    cur = pl.BlockSpec((1, BLK, d), lambda r, b: (r, jnp.minimum(b, nb - 1), 0))
    prev = pl.BlockSpec((1, BLK, d), lambda r, b: (r, jnp.clip(b - 1, 0, nb - 1), 0))
    lse_spec = pl.BlockSpec((1, BLK, HEAD), lambda r, b: (r, jnp.minimum(b, nb - 1), 0))
    dkv = pl.BlockSpec((1, l, d), lambda r, b: (r, 0, 0)) if whole else prev
    return _pc(
        body, name=f"attn_bwd_d{dil}", grid=(dil, steps),
        in_specs=[cur, cur, prev, cur, prev, cur, cur, lse_spec], out_specs=[cur, dkv, dkv],
        out_shape=[SDS((dil, l, d), BF16)] * 3,
        scratch=[pltpu.VMEM((BLK, d), F32), pltpu.VMEM((BLK, d), F32)],
        sem=("parallel", "arbitrary"), args=(q, k, k, v, v, do, o, lse), jobs=jobs)


def _sum_cast(xs, name):
    _, s, d = xs[0].shape
    nsplit = len(xs) - 1

    def body(*refs):
        i_refs, o_ref, scr = refs[:nsplit + 1], refs[nsplit + 1], refs[nsplit + 2:]
        acc = i_refs[0][...].astype(F32)
        for k, dil in enumerate(SPLIT_DILATIONS):
            _join_rows(i_refs[1 + k], scr[k], dil)
            acc = acc + _unstage(scr[k], d)
        o_ref[...] = acc.astype(BF16)

    row = pl.BlockSpec((ROWS, d), lambda i: (i, 0))
    return pl.pallas_call(
        body, name=name, grid=(s // ROWS,), in_specs=[row] + _branch_specs(ROWS, d, lambda dil, i: (0, i, 0)),
        out_specs=row, out_shape=SDS((s, d), BF16),
        scratch_shapes=[pltpu.VMEM((d // 128, ROWS, 128), F32)] * nsplit,
        compiler_params=_cp("parallel"))(xs[0].reshape(s, d), *xs[1:])


def _pack_rows(vs, width):
    flat = jnp.concatenate([v.reshape(-1) for v in vs])
    spans, at = [], 0
    for v in vs:
        spans.append((at, v.size))
        at += v.size
    rows = -(-at // width)
    rows = -(-rows // 8) * 8
    flat = jnp.pad(flat, (0, rows * width - at))
    return flat.reshape(rows, width), spans


def kernel(x, a_norm_g, conv_w1, conv_b1, conv_dw, conv_dw_b, conv_ln_g, conv_ln_b, conv_w2, conv_b2, kv_norm_g, w_k, w_v, b_norm_g, w_q, w_o, ffn_norm_g, ffn_w_gate, ffn_w_up, ffn_w_down, final_norm_g, loss_target, m_a_norm_g, m_conv_w1, m_conv_b1, m_conv_dw, m_conv_dw_b, m_conv_ln_g, m_conv_ln_b, m_conv_w2, m_conv_b2, m_kv_norm_g, m_w_k, m_w_v, m_b_norm_g, m_w_q, m_w_o, m_ffn_norm_g, m_ffn_w_gate, m_ffn_w_up, m_ffn_w_down, m_final_norm_g, v_a_norm_g, v_conv_w1, v_conv_b1, v_conv_dw, v_conv_dw_b, v_conv_ln_g, v_conv_ln_b, v_conv_w2, v_conv_b2, v_kv_norm_g, v_w_k, v_w_v, v_b_norm_g, v_w_q, v_w_o, v_ffn_norm_g, v_ffn_w_gate, v_ffn_w_up, v_ffn_w_down, v_final_norm_g):
    s, d = x.shape[1], x.shape[2]
    dc = d // ND
    h0 = x[0]
    target = loss_target[0]
    xi, yi, ci = lax.axis_index("x"), lax.axis_index("y"), lax.axis_index("c")
    me = 4 * xi + 2 * yi + ci
    c_idx = jnp.reshape(ci, (1,)).astype(jnp.int32)
    q_idx = jnp.reshape(2 * xi + yi, (1,)).astype(jnp.int32)

    bf = lambda w: w.astype(BF16)
    small_shards = [a_norm_g, conv_b1, conv_dw, conv_dw_b, conv_ln_g, conv_ln_b, conv_b2]
    sp, sp_spans = _pack_rows(small_shards, dc)
    w1g, spg = _all_gather([bf(conv_w1[0]), sp], "gather_first")
    spg = spg.reshape(ND, -1)

    def small_full(i, rows):
        at, size = sp_spans[i]
        return spg[:, at:at + size].reshape(ND, rows, size // rows).transpose(1, 0, 2).reshape(rows, -1)

    a_g = small_full(0, 1)
    b1 = small_full(1, 1)
    dw = jnp.pad(small_full(2, CONV_W), ((0, CONV_PAD - CONV_W), (0, 0)))
    dwb, lng, lnb, b2 = small_full(3, 1), small_full(4, 1), small_full(5, 1), small_full(6, 1)
    kv_g, q_g, fin_g = kv_norm_g.reshape(1, d), b_norm_g.reshape(1, d), final_norm_g.reshape(1, d)
    f_g = [ffn_norm_g[0:1], ffn_norm_g[1:2]]

    def send(*shards):
        return _job_gather_send([bf(t) for t in shards])

    def forward(job):
        return _job_gather_forward(job.result)

    def send_half(w, part, first=None):
        return _job_gather_send_rows(bf(w), part, 2, None if first is None else first.result[0])

    s_w2 = send(conv_w2[0])
    (n1,) = _rms_fwd(h0, [a_g], "rms_a", jobs=[s_w2])
    f_w2, s_g0 = forward(s_w2), send(ffn_w_gate[0])
    ua, ug, glu = _glu_mm(n1, w1g, b1, jobs=[f_w2, s_g0])
    (w2g,) = f_w2.result
    f_g0, s_u0 = forward(s_g0), send(ffn_w_up[0])
    cv, sw = _conv_fwd(glu, dw, dwb, lng, lnb, jobs=[f_g0, s_u0])
    (wg0,) = f_g0.result
    f_u0 = forward(s_u0)
    h1 = _mm_rows(sw, w2g, "w2_mm", res=h0, bias=b2, jobs=[f_u0])
    (wu0,) = f_u0.result
    (n2a,) = _rms_fwd(h1, [f_g[0]], "rms_f0")
    s_mid = send(ffn_w_down[0], w_k, w_v)
    gate0, up0, act0 = _swiglu_mm(n2a, wg0, wu0, "swiglu_mm0", jobs=[s_mid])
    f_mid = forward(s_mid)
    _comm_call([f_mid], "forward_mid")
    wd0, wkg, wvg = f_mid.result
    s_qo = send(w_q[0], w_o[0])
    h2 = _down_mm(act0, wd0, h1, "down_mm0", jobs=[s_qo])
    kvn, qn = _rms_fwd(h2, [kv_g, q_g], "rms_kvq")
    f_qo, s_g1a = forward(s_qo), send_half(ffn_w_gate[1], 0)
    kk = _mm_rows(kvn, wkg, "k_mm", out_dtype=BF16, branches=True, jobs=[f_qo, s_g1a])
    wqg, wog = f_qo.result
    s_g1b = send_half(ffn_w_gate[1], 1, s_g1a)
    vv = _mm_rows(kvn, wvg, "v_mm", out_dtype=BF16, branches=True, jobs=[s_g1b])
    f_g1, s_u1a = forward(s_g1b), send_half(ffn_w_up[1], 0)
    qq = _mm_rows(qn, wqg, "q_mm", out_dtype=BF16, branches=True, jobs=[f_g1, s_u1a])
    (wg1,) = f_g1.result
    branch = {dil: (qq[i], kk[i], vv[i]) for i, dil in enumerate(BRANCH_DILATIONS)}
    s_u1b = send_half(ffn_w_up[1], 1, s_u1a)
    o1, l1 = _attn_fwd(*branch[1], 1, jobs=[s_u1b])
    f_u1, s_d1a = forward(s_u1b), send_half(ffn_w_down[1], 0)
    o4, l4 = _attn_fwd(*branch[4], 4, jobs=[f_u1, s_d1a])
    (wu1,) = f_u1.result
    s_d1b = send_half(ffn_w_down[1], 1, s_d1a)
    o16, l16 = _attn_fwd(*branch[16], 16, jobs=[s_d1b])
    atts, lses = _attn_merge([o1, o4, o16], [l1, l4, l16])
    att = atts[0]
    atts, lses = [att[None]] + atts[1:], [lses[0][None]] + lses[1:]
    f_d1 = forward(s_d1b)
    h3 = _mm_rows(att, wog, "wo_mm", res=h2, jobs=[f_d1])
    (wd1,) = f_d1.result
    (n2b,) = _rms_fwd(h3, [f_g[1]], "rms_f1")
    gate1, up1, act1 = _swiglu_mm(n2b, wg1, wu1, "swiglu_mm1")
    h4 = _down_mm(act1, wd1, h3, "down_mm1")

    flat = lambda g: g.reshape(ND, -1, g.shape[-1])
    chip_sums, cross = {}, {}

    def to_sibling(**grads):
        job = _job_scatter_sibling([flat(g) for g in grads.values()])
        job.names = list(grads)
        return job

    def add_up(job):
        for n, g, r in zip(job.names, job.ins, job.result):
            chip_sums[n] = _rs_add(g, r, c_idx, f"rs_add_{n}")

    def to_chips(*names):
        job = _job_scatter_cross([chip_sums[n] for n in names])
        job.names = names
        return job

    def landed(job):
        cross.update(zip(job.names, job.result))

    dh4, dh4b, d_fin, loss_row = _final_loss(h4, target, fin_g)
    dgate1, dup1 = _dact_mm(dh4b, wd1, gate1, up1, "dact_mm1")
    g_wd1 = _dwd_mm(act1, dh4b, "dwd_mm1")
    j1 = to_sibling(wd1=g_wd1)
    g_wg1, g_wu1 = _dwgu_mm(n2b, dgate1, dup1, "dwgu_mm1", jobs=[j1])
    add_up(j1)
    j2, j3 = to_chips("wd1"), to_sibling(wg1=g_wg1, wu1=g_wu1)
    dn2b = _dn_ffn_mm(dgate1, dup1, wg1, wu1, "dn_ffn_mm1", jobs=[j2, j3])
    landed(j2)
    add_up(j3)
    dh3, dh3b, d_f1 = _rms_bwd(h3, [(f_g[1], dn2b)], dh4, "rms_f1_bwd")
    g_wo = _dw_rows_mm(att, dh3b, "dwo_mm")
    j4 = to_sibling(wo=g_wo)
    datt = _mm_rows_t([(dh3b, wog)], "datt_mm", BF16, branches=True, jobs=[j4])
    add_up(j4)
    riders = {1: to_chips("wg1"), 4: to_chips("wu1"), 16: to_chips("wo")}
    dqs, dks, dvs = [], [], []
    for i, dil in enumerate(BRANCH_DILATIONS):
        qb, kb, vb = branch[dil]
        dq_b, dk_b, dv_b = _attn_bwd(qb, kb, vb, datt[i], atts[i], lses[i], dil, jobs=[riders[dil]])
        landed(riders[dil])
        dqs.append(dq_b)
        dks.append(dk_b)
        dvs.append(dv_b)
    dq, dk, dv = _sum_cast(dqs, "dq_sum"), _sum_cast(dks, "dk_sum"), _sum_cast(dvs, "dv_sum")
    g_wq = _dw_rows_mm(qn, dq, "dwq_mm")
    g_wk = _dw_rows_mm(kvn, dk, "dwk_mm")
    g_wv = _dw_rows_mm(kvn, dv, "dwv_mm")
    j5 = to_sibling(wq=g_wq, wk=g_wk, wv=g_wv)
    dqn = _mm_rows_t([(dq, wqg)], "dqn_mm", F32, jobs=[j5])
    add_up(j5)
    j6 = to_chips("wq", "wk")
    dkvn = _mm_rows_t([(dk, wkg), (dv, wvg)], "dkvn_mm", F32, jobs=[j6])
    landed(j6)
    dh2, dh2b, d_q, d_kv = _rms_bwd(h2, [(q_g, dqn), (kv_g, dkvn)], dh3, "rms_kvq_bwd")
    j7 = to_chips("wv")
    dgate0, dup0 = _dact_mm(dh2b, wd0, gate0, up0, "dact_mm0", jobs=[j7])
    landed(j7)
    g_wd0 = _dwd_mm(act0, dh2b, "dwd_mm0")
    j8 = to_sibling(wd0=g_wd0)
    g_wg0, g_wu0 = _dwgu_mm(n2a, dgate0, dup0, "dwgu_mm0", jobs=[j8])
    add_up(j8)
    j9, j10 = to_chips("wd0"), to_sibling(wg0=g_wg0, wu0=g_wu0)
    dn2a = _dn_ffn_mm(dgate0, dup0, wg0, wu0, "dn_ffn_mm0", jobs=[j9, j10])
    landed(j9)
    add_up(j10)
    dh1, dh1b, d_f0, d_b2 = _rms_bwd(h1, [(f_g[0], dn2a)], dh2, "rms_f0_bwd", colsum=True)
    g_w2 = _dw_rows_mm(sw, dh1b, "dw2_mm")
    j11 = to_sibling(w2=g_w2)
    dsw = _mm_rows_t([(dh1b, w2g)], "dsw_mm", F32, jobs=[j11])
    add_up(j11)
    dcv, d_lng, d_lnb = _ln_bwd(dsw, cv, lng, lnb)
    j12 = to_chips("wg0", "wu0")
    du, d_dw, d_dwb, d_b1 = _conv_bwd(dcv, glu, ua, ug, dw, jobs=[j12])
    landed(j12)
    j13 = to_chips("w2")
    g_w1 = _dw1_mm(n1, du, jobs=[j13])
    landed(j13)
    j14 = to_sibling(w1=g_w1)
    dn1 = _dn1_mm(du, w1g, 0, 2, jobs=[j14])
    add_up(j14)
    j15 = to_chips("w1")
    dn1 = _dn1_mm(du, w1g, 1, 2, prev=dn1, jobs=[j15])
    landed(j15)
    dx, _, d_a = _rms_bwd(h0, [(a_g, dn1)], dh1, "rms_a_bwd")

    small_g = [d_a, d_b1, d_dw[:CONV_W], d_dwb, d_lng, d_lnb, d_b2, d_kv, d_q, d_f0, d_f1, d_fin, loss_row]
    gp, gp_spans = _pack_rows(small_g, d)
    (gpg,) = _all_gather([gp], "gather_small_grads")

    two = lambda t: t.reshape(-1, t.shape[-1])

    def adam(w, m, v, names, tag, swapped=False):
        view = (lambda t: jnp.swapaxes(t, 1, 2)) if swapped else (lambda t: t)
        w, m, v = view(w), view(m), view(v)
        res = None
        for part, n in enumerate(names):
            res = _adamw_big(two(w), two(m), two(v), chip_sums[n], cross[n], q_idx, f"adamw_{tag}{part}", part, res)
        return [view(t.reshape(w.shape)) for t in res]

    big_out = [
        adam(conv_w1, m_conv_w1, v_conv_w1, ["w1"], "w1"), adam(conv_w2, m_conv_w2, v_conv_w2, ["w2"], "w2"),
        adam(w_k, m_w_k, v_w_k, ["wk"], "wk"), adam(w_v, m_w_v, v_w_v, ["wv"], "wv"),
        adam(w_q, m_w_q, v_w_q, ["wq"], "wq"), adam(w_o, m_w_o, v_w_o, ["wo"], "wo"),
        adam(ffn_w_gate, m_ffn_w_gate, v_ffn_w_gate, ["wg0", "wg1"], "wg", swapped=True),
        adam(ffn_w_up, m_ffn_w_up, v_ffn_w_up, ["wu0", "wu1"], "wu", swapped=True),
        adam(ffn_w_down, m_ffn_w_down, v_ffn_w_down, ["wd0", "wd1"], "wd")]

    gsum = _sum_devices(gpg, "sum_small_grads").reshape(-1)

    def gfull(i):
        at, size = gp_spans[i]
        return gsum[at:at + size]

    def shard_of(vec, rows):
        return lax.dynamic_slice_in_dim(vec.reshape(rows, -1), me * (vec.size // rows // ND), vec.size // rows // ND, axis=1)

    loss = gfull(12)[0]
    small_grads = [
        shard_of(gfull(0), 1), shard_of(gfull(1), 1), shard_of(gfull(2), CONV_W)[None], shard_of(gfull(3), 1),
        shard_of(gfull(4), 1), shard_of(gfull(5), 1), shard_of(gfull(6), 1),
        gfull(7), gfull(8)[None], jnp.stack([gfull(9), gfull(10)]), gfull(11)]
    small_w = [a_norm_g, conv_b1, conv_dw, conv_dw_b, conv_ln_g, conv_ln_b, conv_b2, kv_norm_g, b_norm_g, ffn_norm_g, final_norm_g]
    small_m = [m_a_norm_g, m_conv_b1, m_conv_dw, m_conv_dw_b, m_conv_ln_g, m_conv_ln_b, m_conv_b2, m_kv_norm_g, m_b_norm_g, m_ffn_norm_g, m_final_norm_g]
    small_v = [v_a_norm_g, v_conv_b1, v_conv_dw, v_conv_dw_b, v_conv_ln_g, v_conv_ln_b, v_conv_b2, v_kv_norm_g, v_b_norm_g, v_ffn_norm_g, v_final_norm_g]
    small_grads = [g.reshape(w.shape) for g, w in zip(small_grads, small_w)]
    wp, spans = _pack_rows(small_w, 128)
    gpk, _ = _pack_rows(small_grads, 128)
    mp, _ = _pack_rows(small_m, 128)
    vp, _ = _pack_rows(small_v, 128)
    dp, mnp, vnp = _adamw_small(wp, gpk, mp, vp, "adamw_small")

    def unpack(packed):
        flat = packed.reshape(-1)
        return [flat[at:at + size].reshape(w.shape) for (at, size), w in zip(spans, small_w)]

    small_out = list(zip(small_grads, unpack(dp), unpack(mnp), unpack(vnp)))

    order = ["a_norm_g", "conv_w1", "conv_b1", "conv_dw", "conv_dw_b", "conv_ln_g", "conv_ln_b", "conv_w2", "conv_b2",
             "kv_norm_g", "w_k", "w_v", "b_norm_g", "w_q", "w_o", "ffn_norm_g", "ffn_w_gate", "ffn_w_up", "ffn_w_down",
             "final_norm_g"]
    big_names = ["conv_w1", "conv_w2", "w_k", "w_v", "w_q", "w_o", "ffn_w_gate", "ffn_w_up", "ffn_w_down"]
    small_names = ["a_norm_g", "conv_b1", "conv_dw", "conv_dw_b", "conv_ln_g", "conv_ln_b", "conv_b2", "kv_norm_g",
                   "b_norm_g", "ffn_norm_g", "final_norm_g"]
    table = {n: big_out[i] for i, n in enumerate(big_names)}
    table.update({n: small_out[i] for i, n in enumerate(small_names)})
    result = [loss, dx[None]]
    for kind in range(4):
        result += [table[n][kind] for n in order]
    return tuple(result)
```

```python
import functools

import jax
import jax.numpy as jnp
from jax import lax
from jax.experimental import pallas as pl
from jax.experimental.pallas import tpu as pltpu

ND = 8
HEAD = 128
BLK = 128
BRANCH_DILATIONS = (1, 4, 16)
CONV_W = 31
CONV_PAD = 32
RMS_EPS = 1e-6
LN_EPS = 1e-5
LR, B1, B2, ADAM_EPS, WD, STEP = 0.001, 0.9, 0.999, 1e-08, 0.01, 10
VMEM_LIMIT = 56 * 1024 * 1024
RESIDENT_BYTES = 4 * 1024 * 1024

F32, BF16 = jnp.float32, jnp.bfloat16
SDS = jax.ShapeDtypeStruct
MESH = pl.DeviceIdType.MESH
ANY = pl.BlockSpec(memory_space=pl.ANY)

NN = (((1,), (0,)), ((), ()))
NT = (((1,), (1,)), ((), ()))
TN = (((0,), (0,)), ((), ()))


def _dot(a, b, dims):
    return lax.dot_general(a, b, dims, preferred_element_type=F32)


def _cp(*sem):
    return pltpu.CompilerParams(dimension_semantics=sem, vmem_limit_bytes=VMEM_LIMIT)


def _slot(dev):
    return 4 * (dev % 2) + dev // 2


def _sigmoid(v):
    return 1.0 / (1.0 + jnp.exp(-v))


class _Job:
    def __init__(self, ins, out_shapes, alias, nsem, nlocal, make):
        self.ins, self.out_shapes, self.alias = list(ins), list(out_shapes), dict(alias)
        self.nsem, self.nlocal, self.make = nsem, nlocal, make
        self.result = None


def _coords():
    return lax.axis_index("x"), lax.axis_index("y"), lax.axis_index("c")


def _remote(src, dst, send, recv, k, to):
    return pltpu.make_async_remote_copy(src_ref=src, dst_ref=dst, send_sem=send.at[k], recv_sem=recv.at[k],
                                        device_id=to, device_id_type=MESH)


def _job_gather_send(shards):
    n = len(shards)

    def make(ins, outs, send, recv, local):
        x, y, c = _coords()
        targets = [(x, y, 1 - c), (1 - x, y, c), (x, 1 - y, c), (1 - x, 1 - y, c)]
        cps = []
        for a in range(n):
            dst = outs[a].at[4 * x + 2 * y + c]
            cps.append(pltpu.make_async_copy(ins[a], dst, local.at[a]))
            cps += [_remote(ins[a], dst, send, recv, 4 * a + k, t) for k, t in enumerate(targets)]
        return cps

    return _Job(shards, [SDS((ND,) + s.shape, s.dtype) for s in shards], {}, 4 * n, n, make)


def _job_gather_send_rows(shard, part, nparts, prev=None):
    rows = shard.shape[0] // nparts

    def make(ins, outs, send, recv, local):
        x, y, c = _coords()
        targets = [(x, y, 1 - c), (1 - x, y, c), (x, 1 - y, c), (1 - x, 1 - y, c)]
        src = ins[0].at[pl.ds(part * rows, rows)]
        dst = outs[0].at[4 * x + 2 * y + c].at[pl.ds(part * rows, rows)]
        return [pltpu.make_async_copy(src, dst, local.at[0])] + [
            _remote(src, dst, send, recv, k, t) for k, t in enumerate(targets)]

    ins = [shard] if prev is None else [shard, prev]
    return _Job(ins, [SDS((ND,) + shard.shape, shard.dtype)], {} if prev is None else {1: 0}, 4, 1, make)


def _job_gather_forward(gathered):
    n = len(gathered)

    def make(ins, outs, send, recv, local):
        x, y, c = _coords()
        cps = []
        for a in range(n):
            for k, (px, py) in enumerate([(1 - x, y), (x, 1 - y), (1 - x, 1 - y)]):
                blk = outs[a].at[4 * px + 2 * py + c]
                cps.append(_remote(blk, blk, send, recv, 3 * a + k, (x, y, 1 - c)))
        return cps

    return _Job(gathered, [SDS(g.shape, g.dtype) for g in gathered], {i: i for i in range(n)}, 3 * n, 0, make)


def _job_scatter_sibling(grads):
    n = len(grads)

    def make(ins, outs, send, recv, local):
        x, y, c = _coords()
        return [_remote(ins[a].at[pl.ds(4 * (1 - c), 4)], outs[a], send, recv, a, (x, y, 1 - c)) for a in range(n)]

    return _Job(grads, [SDS((4,) + g.shape[1:], g.dtype) for g in grads], {}, n, 0, make)


def _job_scatter_cross(sums):
    n = len(sums)

    def make(ins, outs, send, recv, local):
        x, y, c = _coords()
        chips = [(1 - x, y), (x, 1 - y), (1 - x, 1 - y)]
        return [_remote(ins[a].at[2 * px + py], outs[a].at[k], send, recv, 3 * a + k, (px, py, c))
                for a in range(n) for k, (px, py) in enumerate(chips)]

    return _Job(sums, [SDS((3,) + t.shape[1:], t.dtype) for t in sums], {}, 3 * n, 0, make)


def _pc(body, *, name, grid, in_specs, out_specs, out_shape, args, scratch=(), sem=(), alias=None, jobs=()):
    jobs = list(jobs)
    n_in, n_out, n_scr = len(in_specs), len(out_shape), len(scratch)
    aliases = dict(alias or {})
    job_args, job_shapes, job_scratch = [], [], []
    for j in jobs:
        for src, dst in j.alias.items():
            aliases[n_in + len(job_args) + src] = n_out + len(job_shapes) + dst
        job_args += j.ins
        job_shapes += j.out_shapes
        job_scratch += [pltpu.SemaphoreType.DMA((j.nsem,)), pltpu.SemaphoreType.DMA((j.nsem,)),
                        pltpu.SemaphoreType.DMA((max(j.nlocal, 1),))]

    def wrapped(*refs):
        ins = refs[:n_in]
        p = n_in + len(job_args)
        outs = refs[p:p + n_out]
        p += n_out + len(job_shapes)
        scr = refs[p:p + n_scr]
        sems = refs[p + n_scr:]
        copies = []
        pi, po = n_in, n_in + len(job_args) + n_out
        for k, j in enumerate(jobs):
            copies += j.make(refs[pi:pi + len(j.ins)], refs[po:po + len(j.out_shapes)], *sems[3 * k:3 * k + 3])
            pi += len(j.ins)
            po += len(j.out_shapes)
        gridded = bool(copies) and bool(grid)
        if gridded:
            ids = [pl.program_id(i) for i in range(len(grid))]
            first = functools.reduce(jnp.logical_and, [i == 0 for i in ids])
            last = functools.reduce(jnp.logical_and, [i == g - 1 for i, g in zip(ids, grid)])

            @pl.when(first)
            def _():
                for cp in copies:
                    cp.start()
        else:
            for cp in copies:
                cp.start()
        body(*ins, *outs, *scr)
        if gridded:
            @pl.when(last)
            def _():
                for cp in copies:
                    cp.wait()
        else:
            for cp in copies:
                cp.wait()

    kw = dict(grid=grid) if grid else {}
    semantics = ["arbitrary"] * len(grid) if jobs else list(sem)
    res = pl.pallas_call(
        wrapped, name=name, in_specs=list(in_specs) + [ANY] * len(job_args),
        out_specs=list(out_specs) + [ANY] * len(job_shapes), out_shape=list(out_shape) + job_shapes,
        scratch_shapes=list(scratch) + job_scratch, input_output_aliases=aliases,
        compiler_params=_cp(*semantics), **kw)(*args, *job_args)
    p = n_out
    for j in jobs:
        j.result = list(res[p:p + len(j.out_shapes)])
        p += len(j.out_shapes)
    return list(res[:n_out])


def _comm_call(jobs, name):
    _pc(lambda: None, name=name, grid=(), in_specs=[], out_specs=[], out_shape=[], args=[], jobs=jobs)


def _all_gather(arrs, name):
    n = len(arrs)

    def body(*refs):
        ins, outs = refs[:n], refs[n:2 * n]
        send_sems, recv_sems, local_sems = refs[2 * n:]
        x, y, c = lax.axis_index("x"), lax.axis_index("y"), lax.axis_index("c")
        me, sib = (x, y, c), (x, y, 1 - c)
        chips = [(1 - x, y), (x, 1 - y), (1 - x, 1 - y)]

        def copy(a, k, block, to, src=None):
            dst = outs[a].at[4 * block[0] + 2 * block[1] + block[2]]
            return pltpu.make_async_remote_copy(
                src_ref=dst if src is None else src, dst_ref=dst,
                send_sem=send_sems.at[7 * a + k], recv_sem=recv_sems.at[7 * a + k],
                device_id=to, device_id_type=MESH)

        mine = [pltpu.make_async_copy(ins[a], outs[a].at[4 * x + 2 * y + c], local_sems.at[a]) for a in range(n)]
        for cp in mine:
            cp.start()
        first = []
        for a in range(n):
            first.append(copy(a, 0, me, sib, src=ins[a]))
            first += [copy(a, 1 + j, me, (*chip, c), src=ins[a]) for j, chip in enumerate(chips)]
        for cp in first:
            cp.start()
        passed = []
        for a in range(n):
            for j, chip in enumerate(chips):
                copy(a, 1 + j, (*chip, c), me).wait_recv()
                fwd = copy(a, 4 + j, (*chip, c), sib)
                fwd.start()
                passed.append(fwd)
        for a in range(n):
            copy(a, 0, sib, me).wait_recv()
            for j, chip in enumerate(chips):
                copy(a, 4 + j, (*chip, 1 - c), me).wait_recv()
        for cp in first + passed:
            cp.wait_send()
        for cp in mine:
            cp.wait()

    return pl.pallas_call(
        body, name=name,
        out_shape=[SDS((ND,) + a.shape, a.dtype) for a in arrs],
        in_specs=[ANY] * n, out_specs=[ANY] * n,
        scratch_shapes=[pltpu.SemaphoreType.DMA((7 * n,)), pltpu.SemaphoreType.DMA((7 * n,)),
                        pltpu.SemaphoreType.DMA((n,))],
    )(*arrs)


ELEMENTWISE_TILE_BYTES = 3 * 512 * 1024


def _row_tile(rows, cols):
    fits = [t for t in range(16, rows + 1, 16) if rows % t == 0 and 4 * t * cols <= ELEMENTWISE_TILE_BYTES]
    return max(fits)


def _rs_add(g, r1, c_idx, name):
    _, rows, cols = g.shape

    def body(c_ref, g_ref, r_ref, o_ref):
        o_ref[...] = (g_ref[...].astype(F32) + r_ref[...].astype(F32)).astype(o_ref.dtype)

    return pl.pallas_call(
        body, name=name,
        grid_spec=pltpu.PrefetchScalarGridSpec(
            num_scalar_prefetch=1, grid=(4,),
            in_specs=[pl.BlockSpec((1, rows, cols), lambda q, c: (4 * c[0] + q, 0, 0)),
                      pl.BlockSpec((1, rows, cols), lambda q, c: (q, 0, 0))],
            out_specs=pl.BlockSpec((1, rows, cols), lambda q, c: (q, 0, 0))),
        out_shape=SDS((4, rows, cols), g.dtype),
        compiler_params=_cp("parallel"),
    )(c_idx, g, r1)


def _adam_math(w, g, m, v):
    m = B1 * m + (1.0 - B1) * g
    v = B2 * v + (1.0 - B2) * (g * g)
    m_hat = m / (1.0 - B1 ** STEP)
    v_hat = v / (1.0 - B2 ** STEP)
    delta = -LR * (m_hat / (jnp.sqrt(v_hat) + ADAM_EPS) + WD * w)
    return delta, m, v


def _adamw_big(w, m, v, t, r2, q_idx, name, part=0, prev=None):
    _, rows, cols = t.shape
    tr = _row_tile(rows, cols)
    nblk = rows // tr

    def body(q_ref, w_ref, m_ref, v_ref, t_ref, r_ref, *outs):
        g_out, d_out, m_out, v_out = outs[-4:]
        g = t_ref[0].astype(F32)
        for k in range(3):
            g = g + r_ref[k].astype(F32)
        d, mn, vn = _adam_math(w_ref[...], g, m_ref[...], v_ref[...])
        g_out[...], d_out[...], m_out[...], v_out[...] = g, d, mn, vn

    blk = pl.BlockSpec((tr, cols), lambda i, q: (part * nblk + i, 0))
    specs = [blk, blk, blk, pl.BlockSpec((1, tr, cols), lambda i, q: (q[0], i, 0)),
             pl.BlockSpec((3, tr, cols), lambda i, q: (0, i, 0))]
    ins = [q_idx, w, m, v, t, r2]
    alias = {}
    if prev is not None:
        specs += [ANY] * 4
        alias = {6 + k: k for k in range(4)}
        ins += list(prev)
    return pl.pallas_call(
        body, name=name,
        grid_spec=pltpu.PrefetchScalarGridSpec(num_scalar_prefetch=1, grid=(nblk,), in_specs=specs, out_specs=[blk] * 4),
        out_shape=[SDS(w.shape, F32)] * 4, input_output_aliases=alias,
        compiler_params=_cp("parallel"))(*ins)


def _sum_devices(g, name):
    _, rows, cols = g.shape

    def body(g_ref, o_ref):
        acc = g_ref[0]
        for k in range(1, ND):
            acc = acc + g_ref[k]
        o_ref[...] = acc

    return pl.pallas_call(body, name=name, out_shape=SDS((rows, cols), F32))(g)


def _adamw_small(w, g, m, v, name):
    def body(w_ref, g_ref, m_ref, v_ref, d_out, m_out, v_out):
        d, mn, vn = _adam_math(w_ref[...], g_ref[...], m_ref[...], v_ref[...])
        d_out[...], m_out[...], v_out[...] = d, mn, vn

    return pl.pallas_call(body, name=name, out_shape=[SDS(w.shape, F32)] * 3)(w, g, m, v)


ROWS = 256


def _rms_stats(x):
    r = lax.rsqrt(jnp.mean(x * x, axis=-1, keepdims=True) + RMS_EPS)
    return x * r, r


def _rms_fwd(x, gains, name, jobs=()):
    s, d = x.shape
    n = len(gains)

    def body(x_ref, *refs):
        xh, _ = _rms_stats(x_ref[...])
        for g_ref, o_ref in zip(refs[:n], refs[n:]):
            o_ref[...] = (xh * g_ref[...]).astype(BF16)

    row = pl.BlockSpec((ROWS, d), lambda i: (i, 0))
    vec = pl.BlockSpec((1, d), lambda i: (0, 0))
    return _pc(body, name=name, grid=(s // ROWS,), in_specs=[row] + [vec] * n, out_specs=[row] * n,
               out_shape=[SDS((s, d), BF16)] * n, sem=("parallel",), args=(x, *gains), jobs=jobs)


def _rms_bwd_rows(xh, r, gain, dy):
    u = dy * gain
    return r * (u - xh * jnp.mean(u * xh, axis=-1, keepdims=True))


def _rms_bwd(x, pairs, dres, name, colsum=False):
    s, d = x.shape
    n = len(pairs)

    def body(x_ref, dres_ref, *refs):
        g_refs, dy_refs = refs[:n], refs[n:2 * n]
        dx_ref, dxb_ref = refs[2 * n], refs[2 * n + 1]
        dg_refs = refs[2 * n + 2:2 * n + 2 + n]
        cs_ref = refs[-1] if colsum else None
        first = pl.program_id(0) == 0
        xh, r = _rms_stats(x_ref[...])
        dx = dres_ref[...]
        for g_ref, dy_ref, dg_ref in zip(g_refs, dy_refs, dg_refs):
            dy = dy_ref[...]
            dx = dx + _rms_bwd_rows(xh, r, g_ref[...], dy)

            @pl.when(first)
            def _():
                dg_ref[...] = jnp.zeros_like(dg_ref)
            dg_ref[...] += jnp.sum(dy * xh, axis=0, keepdims=True)
        dx_ref[...] = dx
        dxb_ref[...] = dx.astype(BF16)
        if colsum:
            @pl.when(first)
            def _():
                cs_ref[...] = jnp.zeros_like(cs_ref)
            cs_ref[...] += jnp.sum(dx, axis=0, keepdims=True)

    row = pl.BlockSpec((ROWS, d), lambda i: (i, 0))
    vec = pl.BlockSpec((1, d), lambda i: (0, 0))
    nvec = n + (1 if colsum else 0)
    outs = pl.pallas_call(
        body, name=name, grid=(s // ROWS,),
        in_specs=[row, row] + [vec] * n + [row] * n,
        out_specs=[row, row] + [vec] * nvec,
        out_shape=[SDS((s, d), F32), SDS((s, d), BF16)] + [SDS((1, d), F32)] * nvec,
        compiler_params=_cp("arbitrary"),
    )(x, dres, *[p[0] for p in pairs], *[p[1] for p in pairs])
    return outs


def _final_loss(h, target, gain):
    s, d = h.shape

    def body(h_ref, t_ref, g_ref, dh_ref, dhb_ref, dg_ref, loss_ref):
        first = pl.program_id(0) == 0
        xh, r = _rms_stats(h_ref[...])
        gain_v = g_ref[...]
        e = xh * gain_v - t_ref[...]
        dy = e * (1.0 / d)
        dx = _rms_bwd_rows(xh, r, gain_v, dy)
        dh_ref[...] = dx
        dhb_ref[...] = dx.astype(BF16)

        @pl.when(first)
        def _():
            dg_ref[...] = jnp.zeros_like(dg_ref)
            loss_ref[...] = jnp.zeros_like(loss_ref)
        dg_ref[...] += jnp.sum(dy * xh, axis=0, keepdims=True)
        loss_ref[...] += jnp.full((1, 128), 0.5 / d, F32) * jnp.sum(e * e)

    row = pl.BlockSpec((ROWS, d), lambda i: (i, 0))
    vec = pl.BlockSpec((1, d), lambda i: (0, 0))
    return pl.pallas_call(
        body, name="final_loss", grid=(s // ROWS,),
        in_specs=[row, row, vec], out_specs=[row, row, vec, pl.BlockSpec((1, 128), lambda i: (0, 0))],
        out_shape=[SDS((s, d), F32), SDS((s, d), BF16), SDS((1, d), F32), SDS((1, 128), F32)],
        compiler_params=_cp("arbitrary"))(h, target, gain)


CT = 128


def _ln_stats(cv):
    mu = jnp.mean(cv, axis=-1, keepdims=True)
    xc = cv - mu
    rstd = lax.rsqrt(jnp.mean(xc * xc, axis=-1, keepdims=True) + LN_EPS)
    return xc * rstd, rstd


def _conv_fwd(glu, dw, dwb, lng, lnb, jobs=()):
    s, d = glu.shape
    hb = CT // CONV_PAD

    def body(x_ref, halo_ref, dw_ref, dwb_ref, lng_ref, lnb_ref, c_ref, s_ref):
        keep = (pl.program_id(0) > 0).astype(F32)

        def chunk(ci, carry):
            ls = pl.ds(pl.multiple_of(ci * 128, 128), 128)
            xf = jnp.concatenate([halo_ref[:, ls] * keep, x_ref[:, ls]], axis=0)
            acc = jnp.zeros((CT, 128), F32)
            for k in range(CONV_W):
                sh = CONV_W - 1 - k
                xs = pltpu.roll(xf, sh, 0) if sh else xf
                acc = acc + dw_ref[pl.ds(k, 1), ls] * xs[CONV_PAD:]
            c_ref[:, ls] = acc + dwb_ref[:, ls]
            return carry

        lax.fori_loop(0, d // 128, chunk, 0)
        xh, _ = _ln_stats(c_ref[...])
        yv = xh * lng_ref[...] + lnb_ref[...]
        s_ref[...] = (yv * _sigmoid(yv)).astype(BF16)

    row = pl.BlockSpec((CT, d), lambda i: (i, 0))
    halo = pl.BlockSpec((CONV_PAD, d), lambda i: (jnp.maximum(i * hb - 1, 0), 0))
    vec = pl.BlockSpec((1, d), lambda i: (0, 0))
    taps = pl.BlockSpec((CONV_PAD, d), lambda i: (0, 0))
    return _pc(
        body, name="conv_fwd", grid=(s // CT,),
        in_specs=[row, halo, taps, vec, vec, vec], out_specs=[row, row],
        out_shape=[SDS((s, d), F32), SDS((s, d), BF16)], sem=("parallel",),
        args=(glu, glu, dw, dwb, lng, lnb), jobs=jobs)


def _ln_bwd(ds, cv, lng, lnb):
    s, d = cv.shape

    def body(ds_ref, c_ref, g_ref, b_ref, dc_ref, dg_ref, db_ref):
        first = pl.program_id(0) == 0
        xh, rstd = _ln_stats(c_ref[...])
        gv = g_ref[...]
        yv = xh * gv + b_ref[...]
        sg = _sigmoid(yv)
        dln = ds_ref[...] * (sg * (1.0 + yv * (1.0 - sg)))
        dxh = dln * gv
        dc_ref[...] = rstd * (dxh - jnp.mean(dxh, axis=-1, keepdims=True)
                              - xh * jnp.mean(dxh * xh, axis=-1, keepdims=True))

        @pl.when(first)
        def _():
            dg_ref[...] = jnp.zeros_like(dg_ref)
            db_ref[...] = jnp.zeros_like(db_ref)
        dg_ref[...] += jnp.sum(dln * xh, axis=0, keepdims=True)
        db_ref[...] += jnp.sum(dln, axis=0, keepdims=True)

    row = pl.BlockSpec((ROWS, d), lambda i: (i, 0))
    vec = pl.BlockSpec((1, d), lambda i: (0, 0))
    return pl.pallas_call(
        body, name="ln_bwd", grid=(s // ROWS,), in_specs=[row, row, vec, vec], out_specs=[row, vec, vec],
        out_shape=[SDS((s, d), F32), SDS((1, d), F32), SDS((1, d), F32)],
        compiler_params=_cp("arbitrary"))(ds, cv, lng, lnb)


def _conv_bwd(dc, glu, ua, ug, dw, jobs=()):
    s, d = dc.shape
    hb = CT // CONV_PAD
    nsteps = s // CT
    full = CT + CONV_PAD

    def body(dc_ref, dcn_ref, x_ref, xp_ref, ua_ref, ug_ref, dw_ref, du_ref, ddw_ref, ddwb_ref, db1_ref):
        i = pl.program_id(0)
        keep_prev = (i > 0).astype(F32)
        keep_next = (i < nsteps - 1).astype(F32)

        @pl.when(i == 0)
        def _():
            ddw_ref[...] = jnp.zeros_like(ddw_ref)
            ddwb_ref[...] = jnp.zeros_like(ddwb_ref)
            db1_ref[...] = jnp.zeros_like(db1_ref)

        def chunk(ci, carry):
            off = pl.multiple_of(ci * 128, 128)
            ls = pl.ds(off, 128)
            ls2 = pl.ds(pl.multiple_of(d + ci * 128, 128), 128)
            dcc = dc_ref[:, ls]
            dcf = jnp.concatenate([dcc, dcn_ref[:, ls] * keep_next], axis=0)
            xf = jnp.concatenate([xp_ref[:, ls] * keep_prev, x_ref[:, ls]], axis=0)
            dglu = jnp.zeros((CT, 128), F32)
            for k in range(CONV_W):
                sh = CONV_W - 1 - k
                dshift = pltpu.roll(dcf, full - sh, 0) if sh else dcf
                dglu = dglu + dw_ref[pl.ds(k, 1), ls] * dshift[:CT]
                xs = pltpu.roll(xf, sh, 0) if sh else xf
                ddw_ref[pl.ds(k, 1), ls] += jnp.sum(dcc * xs[CONV_PAD:], axis=0, keepdims=True)
            ddwb_ref[:, ls] += jnp.sum(dcc, axis=0, keepdims=True)
            av, gv = ua_ref[:, ls], ug_ref[:, ls]
            sg = _sigmoid(gv)
            da = dglu * sg
            dgt = dglu * av * sg * (1.0 - sg)
            du_ref[:, ls] = da.astype(BF16)
            du_ref[:, ls2] = dgt.astype(BF16)
            db1_ref[:, ls] += jnp.sum(da, axis=0, keepdims=True)
            db1_ref[:, ls2] += jnp.sum(dgt, axis=0, keepdims=True)
            return carry

        lax.fori_loop(0, d // 128, chunk, 0)

    row = pl.BlockSpec((CT, d), lambda i: (i, 0))
    prev = pl.BlockSpec((CONV_PAD, d), lambda i: (jnp.maximum(i * hb - 1, 0), 0))
    nxt = pl.BlockSpec((CONV_PAD, d), lambda i: (jnp.minimum((i + 1) * hb, s // CONV_PAD - 1), 0))
    taps = pl.BlockSpec((CONV_PAD, d), lambda i: (0, 0))
    return _pc(
        body, name="conv_bwd", grid=(nsteps,),
        in_specs=[row, nxt, row, prev, row, row, taps],
        out_specs=[pl.BlockSpec((CT, 2 * d), lambda i: (i, 0)), taps, pl.BlockSpec((1, d), lambda i: (0, 0)),
                   pl.BlockSpec((1, 2 * d), lambda i: (0, 0))],
        out_shape=[SDS((s, 2 * d), BF16), SDS((CONV_PAD, d), F32), SDS((1, d), F32), SDS((1, 2 * d), F32)],
        sem=("arbitrary",), args=(dc, dc, glu, glu, ua, ug, dw), jobs=jobs)


TM = 1024
TS = 1024


def _glu_mm(n1, w1g, b1, jobs=()):
    s, d = n1.shape
    cw = w1g.shape[2]
    half = ND // 2

    def body(a_ref, wa_ref, wg_ref, ba_ref, bg_ref, ua_ref, ug_ref, glu_ref):
        a = a_ref[...]
        ua = _dot(a, wa_ref[0], NN) + ba_ref[...]
        ug = _dot(a, wg_ref[0], NN) + bg_ref[...]
        ua_ref[...], ug_ref[...] = ua, ug
        glu_ref[...] = ua * _sigmoid(ug)

    out = pl.BlockSpec((TM, cw), lambda m, i: (m, i))
    return _pc(
        body, name="glu_mm", grid=(s // TM, half),
        in_specs=[pl.BlockSpec((TM, d), lambda m, i: (m, 0)),
                  pl.BlockSpec((1, d, cw), lambda m, i: (i, 0, 0)),
                  pl.BlockSpec((1, d, cw), lambda m, i: (i + half, 0, 0)),
                  pl.BlockSpec((1, cw), lambda m, i: (0, i)),
                  pl.BlockSpec((1, cw), lambda m, i: (0, i + half))],
        out_specs=[out, out, out], out_shape=[SDS((s, d), F32)] * 3,
        sem=("parallel", "arbitrary"), args=(n1, w1g, w1g, b1, b1), jobs=jobs)


def _mm_rows(a, wg, name, res=None, bias=None, out_dtype=F32, tn=512, branches=False, jobs=()):
    s, kdim = a.shape
    _, kc, n = wg.shape
    assert kc * ND == kdim
    nx = 2 + (res is not None) + (bias is not None)

    def body(*refs):
        acc = _dot(refs[0][...], refs[1][...].reshape(kdim, tn), NN)
        for extra in refs[2:nx]:
            acc = acc + extra[...]
        refs[nx][...] = acc.astype(out_dtype)
        if branches:
            scr = refs[-1]
            _stage(scr, acc)
            for o_ref, dil in zip(refs[nx + 1:], SPLIT_DILATIONS):
                _split_rows(scr, o_ref, dil)

    ins, specs = [a, wg], [pl.BlockSpec((TM, kdim), lambda m, j: (m, 0)), pl.BlockSpec((ND, kc, tn), lambda m, j: (0, 0, j))]
    if res is not None:
        ins.append(res)
        specs.append(pl.BlockSpec((TM, tn), lambda m, j: (m, j)))
    if bias is not None:
        ins.append(bias)
        specs.append(pl.BlockSpec((1, tn), lambda m, j: (0, j)))
    out_specs, out_shape, scratch = [pl.BlockSpec((TM, tn), lambda m, j: (m, j))], [SDS((s, n), out_dtype)], []
    if branches:
        out_specs += _branch_specs(TM, tn, lambda dil, m, j: (0, m, j))
        out_shape += [SDS((dil, s // dil, n), out_dtype) for dil in SPLIT_DILATIONS]
        scratch = [pltpu.VMEM((tn // 128, TM, 128), F32)]
    outs = _pc(body, name=name, grid=(s // TM, n // tn), in_specs=specs, out_specs=out_specs, out_shape=out_shape,
               scratch=scratch, sem=("parallel", "arbitrary"), args=ins, jobs=jobs)
    return [outs[0][None]] + outs[1:] if branches else outs[0]


def _swiglu_mm(n2, wgg, wug, name, jobs=()):
    s, d = n2.shape
    fc = wgg.shape[2]

    def body(a_ref, wg_ref, wu_ref, g_ref, u_ref, act_ref):
        a = a_ref[...]
        g = _dot(a, wg_ref[0], NN)
        u = _dot(a, wu_ref[0], NN)
        g_ref[0], u_ref[0] = g.astype(BF16), u.astype(BF16)
        act_ref[0] = (g * _sigmoid(g) * u).astype(BF16)

    wspec = pl.BlockSpec((1, d, fc), lambda m, j: (j, 0, 0))
    out = pl.BlockSpec((1, TM, fc), lambda m, j: (j, m, 0))
    return _pc(
        body, name=name, grid=(s // TM, ND),
        in_specs=[pl.BlockSpec((TM, d), lambda m, j: (m, 0)), wspec, wspec],
        out_specs=[out, out, out], out_shape=[SDS((ND, s, fc), BF16)] * 3,
        sem=("parallel", "arbitrary"), args=(n2, wgg, wug), jobs=jobs)


def _down_mm(act, wdg, res, name, jobs=()):
    _, s, fc = act.shape
    d = wdg.shape[2]

    def body(a_ref, w_ref, r_ref, o_ref):
        @pl.when(pl.program_id(1) == 0)
        def _():
            o_ref[...] = r_ref[...]
        o_ref[...] += _dot(a_ref[0], w_ref[0], NN)

    row = pl.BlockSpec((TM, d), lambda m, j: (m, 0))
    return _pc(
        body, name=name, grid=(s // TM, ND),
        in_specs=[pl.BlockSpec((1, TM, fc), lambda m, j: (j, m, 0)),
                  pl.BlockSpec((1, fc, d), lambda m, j: (j, 0, 0)), row],
        out_specs=[row], out_shape=[SDS((s, d), F32)],
        sem=("parallel", "arbitrary"), args=(act, wdg, res), jobs=jobs)[0]


def _dact_mm(dh, wdg, gate, up, name, jobs=()):
    s, d = dh.shape
    fc = wdg.shape[1]

    def body(a_ref, w_ref, g_ref, u_ref, dg_ref, du_ref):
        dact = _dot(a_ref[...], w_ref[0], NT)
        g, u = g_ref[0].astype(F32), u_ref[0].astype(F32)
        sg = _sigmoid(g)
        du_ref[0] = (dact * (g * sg)).astype(BF16)
        dg_ref[0] = (dact * u * (sg * (1.0 + g * (1.0 - sg)))).astype(BF16)

    blk = pl.BlockSpec((1, TM, fc), lambda m, j: (j, m, 0))
    return _pc(
        body, name=name, grid=(s // TM, ND),
        in_specs=[pl.BlockSpec((TM, d), lambda m, j: (m, 0)),
                  pl.BlockSpec((1, fc, d), lambda m, j: (j, 0, 0)), blk, blk],
        out_specs=[blk, blk], out_shape=[SDS((ND, s, fc), BF16)] * 2,
        sem=("parallel", "arbitrary"), args=(dh, wdg, gate, up), jobs=jobs)


def _dwd_mm(act, dh, name, jobs=()):
    _, s, fc = act.shape
    d = dh.shape[1]
    nk = s // TS

    def body(a_ref, b_ref, o_ref, acc):
        k = pl.program_id(1)

        @pl.when(k == 0)
        def _():
            acc[...] = jnp.zeros_like(acc)
        acc[...] += _dot(a_ref[0], b_ref[...], TN)

        @pl.when(k == nk - 1)
        def _():
            o_ref[0] = acc[...].astype(BF16)

    return _pc(
        body, name=name, grid=(ND, nk),
        in_specs=[pl.BlockSpec((1, TS, fc), lambda j, k: (j, k, 0)), pl.BlockSpec((TS, d), lambda j, k: (k, 0))],
        out_specs=[pl.BlockSpec((1, fc, d), lambda j, k: (_slot(j), 0, 0))],
        out_shape=[SDS((ND, fc, d), BF16)], scratch=[pltpu.VMEM((fc, d), F32)],
        sem=("parallel", "arbitrary"), args=(act, dh), jobs=jobs)[0]


def _dwgu_mm(n2, dgate, dup, name, jobs=()):
    s, d = n2.shape
    fc = dgate.shape[2]
    nk = s // TS

    def body(a_ref, g_ref, u_ref, og_ref, ou_ref, accg, accu):
        k = pl.program_id(1)

        @pl.when(k == 0)
        def _():
            accg[...] = jnp.zeros_like(accg)
            accu[...] = jnp.zeros_like(accu)
        a = a_ref[...]
        accg[...] += _dot(g_ref[0], a, TN)
        accu[...] += _dot(u_ref[0], a, TN)

        @pl.when(k == nk - 1)
        def _():
            og_ref[0] = accg[...].astype(BF16)
            ou_ref[0] = accu[...].astype(BF16)

    blk = pl.BlockSpec((1, TS, fc), lambda j, k: (j, k, 0))
    out = pl.BlockSpec((1, fc, d), lambda j, k: (_slot(j), 0, 0))
    return _pc(
        body, name=name, grid=(ND, nk),
        in_specs=[pl.BlockSpec((TS, d), lambda j, k: (k, 0)), blk, blk], out_specs=[out, out],
        out_shape=[SDS((ND, fc, d), BF16)] * 2,
        scratch=[pltpu.VMEM((fc, d), F32), pltpu.VMEM((fc, d), F32)],
        sem=("parallel", "arbitrary"), args=(n2, dgate, dup), jobs=jobs)


def _dn_ffn_mm(dgate, dup, wgg, wug, name, jobs=()):
    _, s, fc = dgate.shape
    d = wgg.shape[1]

    def body(g_ref, u_ref, wg_ref, wu_ref, o_ref):
        j = pl.program_id(1)

        @pl.when(j == 0)
        def _():
            o_ref[...] = jnp.zeros_like(o_ref)
        o_ref[...] += _dot(g_ref[0], wg_ref[0], NT) + _dot(u_ref[0], wu_ref[0], NT)

    blk = pl.BlockSpec((1, TM, fc), lambda m, j: (j, m, 0))
    wspec = pl.BlockSpec((1, d, fc), lambda m, j: (j, 0, 0))
    return _pc(
        body, name=name, grid=(s // TM, ND), in_specs=[blk, blk, wspec, wspec],
        out_specs=[pl.BlockSpec((TM, d), lambda m, j: (m, 0))], out_shape=[SDS((s, d), F32)],
        sem=("parallel", "arbitrary"), args=(dgate, dup, wgg, wug), jobs=jobs)[0]


def _mm_rows_t(pairs, name, out_dtype, branches=False, jobs=()):
    s, n = pairs[0][0].shape
    _, kc, _ = pairs[0][1].shape
    np_ = len(pairs)
    grp = ND // 2
    wide = grp * kc

    def body(*refs):
        o_ref = refs[2 * np_]
        for i in range(grp):
            acc = None
            for p in range(np_):
                t = _dot(refs[p][...], refs[np_ + p][i], NT)
                acc = t if acc is None else acc + t
            o_ref[:, kc * i:kc * (i + 1)] = acc.astype(out_dtype)
            if branches:
                for c, ls in enumerate(_lane_chunks(kc)):
                    refs[-1][i * (kc // 128) + c] = acc[:, ls]
        if branches:
            for b_ref, dil in zip(refs[2 * np_ + 1:], SPLIT_DILATIONS):
                _split_rows(refs[-1], b_ref, dil)

    out_specs, out_shape, scratch = [pl.BlockSpec((TM, wide), lambda m, j: (m, j))], [SDS((s, kc * ND), out_dtype)], []
    if branches:
        out_specs += _branch_specs(TM, wide, lambda dil, m, j: (0, m, j))
        out_shape += [SDS((dil, s // dil, kc * ND), out_dtype) for dil in SPLIT_DILATIONS]
        scratch = [pltpu.VMEM((wide // 128, TM, 128), F32)]
    outs = _pc(
        body, name=name, grid=(s // TM, ND // grp),
        in_specs=[pl.BlockSpec((TM, n), lambda m, j: (m, 0))] * np_ + [pl.BlockSpec((grp, kc, n), lambda m, j: (j, 0, 0))] * np_,
        out_specs=out_specs, out_shape=out_shape, scratch=scratch,
        sem=("parallel", "arbitrary"), args=[p[0] for p in pairs] + [p[1] for p in pairs], jobs=jobs)
    return [outs[0][None]] + outs[1:] if branches else outs[0]


def _dw_rows_mm(a, b, name):
    s, kdim = a.shape
    n = b.shape[1]
    kc = kdim // ND
    ts = TS // 2
    nk = s // ts

    def body(a_ref, b_ref, o_ref, acc):
        k = pl.program_id(0)

        @pl.when(k == 0)
        def _():
            acc[...] = jnp.zeros_like(acc)
        acc[...] += _dot(a_ref[...], b_ref[...], TN)

        @pl.when(k == nk - 1)
        def _():
            for dev in range(ND):
                o_ref[_slot(dev)] = acc[kc * dev:kc * (dev + 1), :].astype(BF16)

    return pl.pallas_call(
        body, name=name, grid=(nk,),
        in_specs=[pl.BlockSpec((ts, kdim), lambda k: (k, 0)), pl.BlockSpec((ts, n), lambda k: (k, 0))],
        out_specs=pl.BlockSpec((ND, kc, n), lambda k: (0, 0, 0)), out_shape=SDS((ND, kc, n), BF16),
        scratch_shapes=[pltpu.VMEM((kdim, n), F32)], compiler_params=_cp("arbitrary"))(a, b)


def _dw1_mm(n1, du, jobs=()):
    s, d = n1.shape
    cw = du.shape[1] // ND
    nk = s // TS

    def body(a_ref, b_ref, o_ref, acc):
        k = pl.program_id(1)

        @pl.when(k == 0)
        def _():
            acc[...] = jnp.zeros_like(acc)
        acc[...] += _dot(a_ref[...], b_ref[...], TN)

        @pl.when(k == nk - 1)
        def _():
            o_ref[0] = acc[...].astype(BF16)

    return _pc(
        body, name="dw1_mm", grid=(ND, nk),
        in_specs=[pl.BlockSpec((TS, d), lambda j, k: (k, 0)), pl.BlockSpec((TS, cw), lambda j, k: (k, j))],
        out_specs=[pl.BlockSpec((1, d, cw), lambda j, k: (_slot(j), 0, 0))], out_shape=[SDS((ND, d, cw), BF16)],
        scratch=[pltpu.VMEM((d, cw), F32)], sem=("parallel", "arbitrary"), args=(n1, du), jobs=jobs)[0]


def _dn1_mm(du, w1g, part, nparts, prev=None, jobs=()):
    s = du.shape[0]
    _, d, cw = w1g.shape
    steps = s // TM // nparts
    m0 = part * steps

    def body(a_ref, w_ref, *refs):
        o_ref = refs[-1]
        j = pl.program_id(1)

        @pl.when(j == 0)
        def _():
            o_ref[...] = jnp.zeros_like(o_ref)
        o_ref[...] += _dot(a_ref[...], w_ref[0], NT)

    ins = [du, w1g] if prev is None else [du, w1g, prev]
    specs = [pl.BlockSpec((TM, cw), lambda m, j: (m0 + m, j)), pl.BlockSpec((1, d, cw), lambda m, j: (j, 0, 0))]
    return _pc(
        body, name=f"dn1_mm{part}", grid=(steps, ND), in_specs=specs if prev is None else specs + [ANY],
        out_specs=[pl.BlockSpec((TM, d), lambda m, j: (m0 + m, 0))], out_shape=[SDS((s, d), F32)],
        alias=None if prev is None else {2: 0}, sem=("parallel", "arbitrary"), args=ins, jobs=jobs)[0]


NEG = -1e30


def _slopes(heads):
    return [2.0 ** (-8.0 * (h + 1) / heads) for h in range(heads)]


def _band(has_prev):
    qi = lax.broadcasted_iota(jnp.int32, (BLK, 2 * BLK), 0)
    ki = lax.broadcasted_iota(jnp.int32, (BLK, 2 * BLK), 1)
    j = qi - ki + BLK
    ok = (j >= 0) & (j <= BLK) & (has_prev | (ki >= BLK))
    return j.astype(F32), ok


SPLIT_DILATIONS = tuple(dil for dil in BRANCH_DILATIONS if dil > 1)


def _lane_chunks(w):
    return [slice(128 * c, 128 * (c + 1)) for c in range(w // 128)]


def _stage(scr, tile):
    for c, ls in enumerate(_lane_chunks(tile.shape[1])):
        scr[c] = tile[:, ls]


def _split_rows(scr, o_ref, dil):
    _, n, w = o_ref.shape
    for r in range(dil):
        for c, ls in enumerate(_lane_chunks(w)):
            o_ref[r, :, ls] = scr[c, pl.ds(r, n, stride=dil), :].astype(o_ref.dtype)


def _join_rows(i_ref, scr, dil):
    _, n, w = i_ref.shape
    for r in range(dil):
        for c, ls in enumerate(_lane_chunks(w)):
            scr[c, pl.ds(r, n, stride=dil), :] = i_ref[r, :, ls].astype(F32)


def _unstage(scr, w):
    return jnp.concatenate([scr[c] for c in range(w // 128)], axis=1)


def _branch_specs(rows, w, index):
    return [pl.BlockSpec((dil, rows // dil, w), functools.partial(index, dil)) for dil in SPLIT_DILATIONS]


def _attn_fwd(q, k, v, dil, jobs=()):
    _, l, d = q.shape
    heads = d // HEAD
    assert heads <= HEAD
    scale = HEAD ** -0.5
    slopes = _slopes(heads)

    def body(q_ref, kc_ref, kp_ref, vc_ref, vp_ref, o_ref, lse_ref):
        jf, ok = _band(pl.program_id(1) > 0)
        lane = lax.broadcasted_iota(jnp.int32, (BLK, HEAD), 1)
        lse = jnp.zeros((BLK, HEAD), F32)
        for h in range(heads):
            sl = slice(HEAD * h, HEAD * (h + 1))
            kh = jnp.concatenate([kp_ref[0, :, sl], kc_ref[0, :, sl]], axis=0)
            vh = jnp.concatenate([vp_ref[0, :, sl], vc_ref[0, :, sl]], axis=0)
            logits = jnp.where(ok, _dot(q_ref[0, :, sl], kh, NT) * scale + jf * (-slopes[h] * dil), NEG)
            m = jnp.max(logits, axis=-1, keepdims=True)
            p = jnp.exp(logits - m)
            den = jnp.sum(p, axis=-1, keepdims=True)
            o_ref[0, :, sl] = _dot(p.astype(BF16), vh, NN) / den
            lse = jnp.where(lane == h, m + jnp.log(den), lse)
        lse_ref[0] = lse

    cur = pl.BlockSpec((1, BLK, d), lambda r, b: (r, b, 0))
    prev = pl.BlockSpec((1, BLK, d), lambda r, b: (r, jnp.maximum(b - 1, 0), 0))
    return _pc(
        body, name=f"attn_fwd_d{dil}", grid=(dil, l // BLK),
        in_specs=[cur, cur, prev, cur, prev], out_specs=[cur, pl.BlockSpec((1, BLK, HEAD), lambda r, b: (r, b, 0))],
        out_shape=[SDS((dil, l, d), F32), SDS((dil, l, HEAD), F32)], sem=("parallel", "arbitrary"),
        args=(q, k, k, v, v), jobs=jobs)


def _attn_merge(outs, lses):
    _, s, d = outs[0].shape
    heads = d // HEAD
    nb = len(outs)
    nsplit = nb - 1

    def body(*refs):
        o_refs, l_refs = refs[:nb], refs[nb:2 * nb]
        att_refs, lse_refs = refs[2 * nb:3 * nb], refs[3 * nb:4 * nb]
        scr_o, scr_l, scr_att = refs[4 * nb:4 * nb + nsplit], refs[4 * nb + nsplit:4 * nb + 2 * nsplit], refs[-1]
        ls = [l_refs[0][...]]
        for k, dil in enumerate(SPLIT_DILATIONS):
            _join_rows(o_refs[1 + k], scr_o[k], dil)
            _join_rows(l_refs[1 + k], scr_l[k], dil)
            ls.append(scr_l[k][0])
        m = functools.reduce(jnp.maximum, ls)
        ws = [jnp.exp(v - m) for v in ls]
        den = functools.reduce(jnp.add, ws)
        ws = [w / den for w in ws]
        lse_refs[0][...] = m + jnp.log(den)
        scr_l[0][0] = m + jnp.log(den)
        for h in range(heads):
            sl = slice(HEAD * h, HEAD * (h + 1))
            slab = ws[0][:, h:h + 1] * o_refs[0][:, sl]
            for k in range(nsplit):
                slab = slab + ws[1 + k][:, h:h + 1] * scr_o[k][h]
            att_refs[0][:, sl] = slab.astype(BF16)
            scr_att[h] = slab
        for k, dil in enumerate(SPLIT_DILATIONS):
            _split_rows(scr_att, att_refs[1 + k], dil)
            _split_rows(scr_l[0], lse_refs[1 + k], dil)

    def specs(w):
        return [pl.BlockSpec((ROWS, w), lambda i: (i, 0))] + _branch_specs(ROWS, w, lambda dil, i: (0, i, 0))

    def shapes(w, dt):
        return [SDS((s, w), dt)] + [SDS((dil, s // dil, w), dt) for dil in SPLIT_DILATIONS]

    wide, narrow = pltpu.VMEM((heads, ROWS, 128), F32), pltpu.VMEM((1, ROWS, 128), F32)
    res = pl.pallas_call(
        body, name="attn_merge", grid=(s // ROWS,), in_specs=specs(d) + specs(HEAD), out_specs=specs(d) + specs(HEAD),
        out_shape=shapes(d, BF16) + shapes(HEAD, F32),
        scratch_shapes=[wide] * nsplit + [narrow] * nsplit + [wide],
        compiler_params=_cp("parallel"))(outs[0].reshape(s, d), *outs[1:], lses[0].reshape(s, HEAD), *lses[1:])
    return list(res[:nb]), list(res[nb:])


def _attn_bwd(q, k, v, do, o, lse, dil, jobs=()):
    _, l, d = q.shape
    nb = l // BLK
    heads = d // HEAD
    scale = HEAD ** -0.5
    slopes = _slopes(heads)
    whole = 2 * l * d <= RESIDENT_BYTES
    steps = nb if whole else nb + 1

    def body(q_ref, kc_ref, kp_ref, vc_ref, vp_ref, do_ref, o_ref, lse_ref, dq_ref, dk_ref, dv_ref, ck, cv):
        b = pl.program_id(1)
        rows = pl.ds(pl.multiple_of(jnp.maximum(b - 1, 0) * BLK, BLK), BLK) if whole else slice(None)

        @pl.when(b == 0)
        def _():
            ck[...] = jnp.zeros_like(ck)
            cv[...] = jnp.zeros_like(cv)

        @pl.when(b < nb)
        def _():
            jf, ok = _band(b > 0)
            for h in range(heads):
                sl = slice(HEAD * h, HEAD * (h + 1))
                qh, doh = q_ref[0, :, sl], do_ref[0, :, sl]
                kh = jnp.concatenate([kp_ref[0, :, sl], kc_ref[0, :, sl]], axis=0)
                vh = jnp.concatenate([vp_ref[0, :, sl], vc_ref[0, :, sl]], axis=0)
                lse_h = lse_ref[0, :, h:h + 1]
                delta = jnp.sum(doh.astype(F32) * o_ref[0, :, sl].astype(F32), axis=-1, keepdims=True)
                p = jnp.where(ok, jnp.exp(_dot(qh, kh, NT) * scale + jf * (-slopes[h] * dil) - lse_h), 0.0)
                ds = (p * (_dot(doh, vh, NT) - delta)).astype(BF16)
                dq_ref[0, :, sl] = (_dot(ds, kh, NN) * scale).astype(BF16)
                dk2 = _dot(ds, qh, TN) * scale
                dv2 = _dot(p.astype(BF16), doh, TN)
                dk_ref[0, rows, sl] = (ck[:, sl] + dk2[:BLK]).astype(BF16)
                dv_ref[0, rows, sl] = (cv[:, sl] + dv2[:BLK]).astype(BF16)
                ck[:, sl] = dk2[BLK:]
                cv[:, sl] = dv2[BLK:]

        @pl.when(b == steps - 1)
        def _():
            last = pl.ds((nb - 1) * BLK, BLK) if whole else slice(None)
            dk_ref[0, last, :] = ck[...].astype(BF16)
            dv_ref[0, last, :] = cv[...].astype(BF16)

    cur = pl.BlockSpec((1, BLK, d), lambda r, b: (r, jnp.minimum(b, nb - 1), 0))
    prev = pl.BlockSpec((1, BLK, d), lambda r, b: (r, jnp.clip(b - 1, 0, nb - 1), 0))
    lse_spec = pl.BlockSpec((1, BLK, HEAD), lambda r, b: (r, jnp.minimum(b, nb - 1), 0))
    dkv = pl.BlockSpec((1, l, d), lambda r, b: (r, 0, 0)) if whole else prev
    return _pc(
        body, name=f"attn_bwd_d{dil}", grid=(dil, steps),
        in_specs=[cur, cur, prev, cur, prev, cur, cur, lse_spec], out_specs=[cur, dkv, dkv],
        out_shape=[SDS((dil, l, d), BF16)] * 3,
        scratch=[pltpu.VMEM((BLK, d), F32), pltpu.VMEM((BLK, d), F32)],
        sem=("parallel", "arbitrary"), args=(q, k, k, v, v, do, o, lse), jobs=jobs)


def _sum_cast(xs, name):
    _, s, d = xs[0].shape
    nsplit = len(xs) - 1

    def body(*refs):
        i_refs, o_ref, scr = refs[:nsplit + 1], refs[nsplit + 1], refs[nsplit + 2:]
        acc = i_refs[0][...].astype(F32)
        for k, dil in enumerate(SPLIT_DILATIONS):
            _join_rows(i_refs[1 + k], scr[k], dil)
            acc = acc + _unstage(scr[k], d)
        o_ref[...] = acc.astype(BF16)

    row = pl.BlockSpec((ROWS, d), lambda i: (i, 0))
    return pl.pallas_call(
        body, name=name, grid=(s // ROWS,), in_specs=[row] + _branch_specs(ROWS, d, lambda dil, i: (0, i, 0)),
        out_specs=row, out_shape=SDS((s, d), BF16),
        scratch_shapes=[pltpu.VMEM((d // 128, ROWS, 128), F32)] * nsplit,
        compiler_params=_cp("parallel"))(xs[0].reshape(s, d), *xs[1:])


def _pack_rows(vs, width):
    flat = jnp.concatenate([v.reshape(-1) for v in vs])
    spans, at = [], 0
    for v in vs:
        spans.append((at, v.size))
        at += v.size
    rows = -(-at // width)
    rows = -(-rows // 8) * 8
    flat = jnp.pad(flat, (0, rows * width - at))
    return flat.reshape(rows, width), spans


def kernel(x, a_norm_g, conv_w1, conv_b1, conv_dw, conv_dw_b, conv_ln_g, conv_ln_b, conv_w2, conv_b2, kv_norm_g, w_k, w_v, b_norm_g, w_q, w_o, ffn_norm_g, ffn_w_gate, ffn_w_up, ffn_w_down, final_norm_g, loss_target, m_a_norm_g, m_conv_w1, m_conv_b1, m_conv_dw, m_conv_dw_b, m_conv_ln_g, m_conv_ln_b, m_conv_w2, m_conv_b2, m_kv_norm_g, m_w_k, m_w_v, m_b_norm_g, m_w_q, m_w_o, m_ffn_norm_g, m_ffn_w_gate, m_ffn_w_up, m_ffn_w_down, m_final_norm_g, v_a_norm_g, v_conv_w1, v_conv_b1, v_conv_dw, v_conv_dw_b, v_conv_ln_g, v_conv_ln_b, v_conv_w2, v_conv_b2, v_kv_norm_g, v_w_k, v_w_v, v_b_norm_g, v_w_q, v_w_o, v_ffn_norm_g, v_ffn_w_gate, v_ffn_w_up, v_ffn_w_down, v_final_norm_g):
    s, d = x.shape[1], x.shape[2]
    dc = d // ND
    h0 = x[0]
    target = loss_target[0]
    xi, yi, ci = lax.axis_index("x"), lax.axis_index("y"), lax.axis_index("c")
    me = 4 * xi + 2 * yi + ci
    c_idx = jnp.reshape(ci, (1,)).astype(jnp.int32)
    q_idx = jnp.reshape(2 * xi + yi, (1,)).astype(jnp.int32)

    bf = lambda w: w.astype(BF16)
    small_shards = [a_norm_g, conv_b1, conv_dw, conv_dw_b, conv_ln_g, conv_ln_b, conv_b2]
    sp, sp_spans = _pack_rows(small_shards, dc)
    w1g, spg = _all_gather([bf(conv_w1[0]), sp], "gather_first")
    spg = spg.reshape(ND, -1)

    def small_full(i, rows):
        at, size = sp_spans[i]
        return spg[:, at:at + size].reshape(ND, rows, size // rows).transpose(1, 0, 2).reshape(rows, -1)

    a_g = small_full(0, 1)
    b1 = small_full(1, 1)
    dw = jnp.pad(small_full(2, CONV_W), ((0, CONV_PAD - CONV_W), (0, 0)))
    dwb, lng, lnb, b2 = small_full(3, 1), small_full(4, 1), small_full(5, 1), small_full(6, 1)
    kv_g, q_g, fin_g = kv_norm_g.reshape(1, d), b_norm_g.reshape(1, d), final_norm_g.reshape(1, d)
    f_g = [ffn_norm_g[0:1], ffn_norm_g[1:2]]

    def send(*shards):
        return _job_gather_send([bf(t) for t in shards])

    def forward(job):
        return _job_gather_forward(job.result)

    def send_half(w, part, first=None):
        return _job_gather_send_rows(bf(w), part, 2, None if first is None else first.result[0])

    s_w2 = send(conv_w2[0])
    (n1,) = _rms_fwd(h0, [a_g], "rms_a", jobs=[s_w2])
    f_w2, s_g0 = forward(s_w2), send(ffn_w_gate[0])
    ua, ug, glu = _glu_mm(n1, w1g, b1, jobs=[f_w2, s_g0])
    (w2g,) = f_w2.result
    f_g0, s_u0 = forward(s_g0), send(ffn_w_up[0])
    cv, sw = _conv_fwd(glu, dw, dwb, lng, lnb, jobs=[f_g0, s_u0])
    (wg0,) = f_g0.result
    f_u0 = forward(s_u0)
    h1 = _mm_rows(sw, w2g, "w2_mm", res=h0, bias=b2, jobs=[f_u0])
    (wu0,) = f_u0.result
    (n2a,) = _rms_fwd(h1, [f_g[0]], "rms_f0")
    s_d0, s_kv = send(ffn_w_down[0]), send(w_k, w_v)
    gate0, up0, act0 = _swiglu_mm(n2a, wg0, wu0, "swiglu_mm0", jobs=[s_d0, s_kv])
    f_d0 = forward(s_d0)
    _comm_call([f_d0], "forward_mid")
    (wd0,) = f_d0.result
    f_kv, s_qo = forward(s_kv), send(w_q[0], w_o[0])
    h2 = _down_mm(act0, wd0, h1, "down_mm0", jobs=[f_kv, s_qo])
    wkg, wvg = f_kv.result
    kvn, qn = _rms_fwd(h2, [kv_g, q_g], "rms_kvq")
    f_qo, s_g1a = forward(s_qo), send_half(ffn_w_gate[1], 0)
    kk = _mm_rows(kvn, wkg, "k_mm", out_dtype=BF16, branches=True, jobs=[f_qo, s_g1a])
    wqg, wog = f_qo.result
    s_g1b = send_half(ffn_w_gate[1], 1, s_g1a)
    vv = _mm_rows(kvn, wvg, "v_mm", out_dtype=BF16, branches=True, jobs=[s_g1b])
    f_g1, s_u1a = forward(s_g1b), send_half(ffn_w_up[1], 0)
    qq = _mm_rows(qn, wqg, "q_mm", out_dtype=BF16, branches=True, jobs=[f_g1, s_u1a])
    (wg1,) = f_g1.result
    branch = {dil: (qq[i], kk[i], vv[i]) for i, dil in enumerate(BRANCH_DILATIONS)}
    s_u1b = send_half(ffn_w_up[1], 1, s_u1a)
    o1, l1 = _attn_fwd(*branch[1], 1, jobs=[s_u1b])
    f_u1, s_d1a = forward(s_u1b), send_half(ffn_w_down[1], 0)
    o4, l4 = _attn_fwd(*branch[4], 4, jobs=[f_u1, s_d1a])
    (wu1,) = f_u1.result
    s_d1b = send_half(ffn_w_down[1], 1, s_d1a)
    o16, l16 = _attn_fwd(*branch[16], 16, jobs=[s_d1b])
    atts, lses = _attn_merge([o1, o4, o16], [l1, l4, l16])
    att = atts[0]
    atts, lses = [att[None]] + atts[1:], [lses[0][None]] + lses[1:]
    f_d1 = forward(s_d1b)
    h3 = _mm_rows(att, wog, "wo_mm", res=h2, jobs=[f_d1])
    (wd1,) = f_d1.result
    (n2b,) = _rms_fwd(h3, [f_g[1]], "rms_f1")
    gate1, up1, act1 = _swiglu_mm(n2b, wg1, wu1, "swiglu_mm1")
    h4 = _down_mm(act1, wd1, h3, "down_mm1")

    flat = lambda g: g.reshape(ND, -1, g.shape[-1])
    chip_sums, cross = {}, {}

    def to_sibling(**grads):
        job = _job_scatter_sibling([flat(g) for g in grads.values()])
        job.names = list(grads)
        return job

    def add_up(job):
        for n, g, r in zip(job.names, job.ins, job.result):
            chip_sums[n] = _rs_add(g, r, c_idx, f"rs_add_{n}")

    def to_chips(*names):
        job = _job_scatter_cross([chip_sums[n] for n in names])
        job.names = names
        return job

    def landed(job):
        cross.update(zip(job.names, job.result))

    dh4, dh4b, d_fin, loss_row = _final_loss(h4, target, fin_g)
    dgate1, dup1 = _dact_mm(dh4b, wd1, gate1, up1, "dact_mm1")
    g_wd1 = _dwd_mm(act1, dh4b, "dwd_mm1")
    j1 = to_sibling(wd1=g_wd1)
    g_wg1, g_wu1 = _dwgu_mm(n2b, dgate1, dup1, "dwgu_mm1", jobs=[j1])
    add_up(j1)
    j2, j3 = to_chips("wd1"), to_sibling(wg1=g_wg1, wu1=g_wu1)
    dn2b = _dn_ffn_mm(dgate1, dup1, wg1, wu1, "dn_ffn_mm1", jobs=[j2, j3])
    landed(j2)
    add_up(j3)
    dh3, dh3b, d_f1 = _rms_bwd(h3, [(f_g[1], dn2b)], dh4, "rms_f1_bwd")
    g_wo = _dw_rows_mm(att, dh3b, "dwo_mm")
    j4 = to_sibling(wo=g_wo)
    datt = _mm_rows_t([(dh3b, wog)], "datt_mm", BF16, branches=True, jobs=[j4])
    add_up(j4)
    riders = {1: to_chips("wg1"), 4: to_chips("wu1"), 16: to_chips("wo")}
    dqs, dks, dvs = [], [], []
    for i, dil in enumerate(BRANCH_DILATIONS):
        qb, kb, vb = branch[dil]
        dq_b, dk_b, dv_b = _attn_bwd(qb, kb, vb, datt[i], atts[i], lses[i], dil, jobs=[riders[dil]])
        landed(riders[dil])
        dqs.append(dq_b)
        dks.append(dk_b)
        dvs.append(dv_b)
    dq, dk, dv = _sum_cast(dqs, "dq_sum"), _sum_cast(dks, "dk_sum"), _sum_cast(dvs, "dv_sum")
    g_wq = _dw_rows_mm(qn, dq, "dwq_mm")
    g_wk = _dw_rows_mm(kvn, dk, "dwk_mm")
    g_wv = _dw_rows_mm(kvn, dv, "dwv_mm")
    j5 = to_sibling(wq=g_wq, wk=g_wk, wv=g_wv)
    dqn = _mm_rows_t([(dq, wqg)], "dqn_mm", F32, jobs=[j5])
    add_up(j5)
    j6 = to_chips("wq", "wk")
    dkvn = _mm_rows_t([(dk, wkg), (dv, wvg)], "dkvn_mm", F32, jobs=[j6])
    landed(j6)
    dh2, dh2b, d_q, d_kv = _rms_bwd(h2, [(q_g, dqn), (kv_g, dkvn)], dh3, "rms_kvq_bwd")
    j7 = to_chips("wv")
    dgate0, dup0 = _dact_mm(dh2b, wd0, gate0, up0, "dact_mm0", jobs=[j7])
    landed(j7)
    g_wd0 = _dwd_mm(act0, dh2b, "dwd_mm0")
    j8 = to_sibling(wd0=g_wd0)
    g_wg0, g_wu0 = _dwgu_mm(n2a, dgate0, dup0, "dwgu_mm0", jobs=[j8])
    add_up(j8)
    j9, j10 = to_chips("wd0"), to_sibling(wg0=g_wg0, wu0=g_wu0)
    dn2a = _dn_ffn_mm(dgate0, dup0, wg0, wu0, "dn_ffn_mm0", jobs=[j9, j10])
    landed(j9)
    add_up(j10)
    dh1, dh1b, d_f0, d_b2 = _rms_bwd(h1, [(f_g[0], dn2a)], dh2, "rms_f0_bwd", colsum=True)
    g_w2 = _dw_rows_mm(sw, dh1b, "dw2_mm")
    j11 = to_sibling(w2=g_w2)
    dsw = _mm_rows_t([(dh1b, w2g)], "dsw_mm", F32, jobs=[j11])
    add_up(j11)
    dcv, d_lng, d_lnb = _ln_bwd(dsw, cv, lng, lnb)
    j12 = to_chips("wg0", "wu0")
    du, d_dw, d_dwb, d_b1 = _conv_bwd(dcv, glu, ua, ug, dw, jobs=[j12])
    landed(j12)
    j13 = to_chips("w2")
    g_w1 = _dw1_mm(n1, du, jobs=[j13])
    landed(j13)
    j14 = to_sibling(w1=g_w1)
    _comm_call([j14], "rs_w1_sibling")
    add_up(j14)
    j15 = to_chips("w1")
    dn1 = _dn1_mm(du, w1g, 0, 1, jobs=[j15])
    landed(j15)
    dx, _, d_a = _rms_bwd(h0, [(a_g, dn1)], dh1, "rms_a_bwd")

    small_g = [d_a, d_b1, d_dw[:CONV_W], d_dwb, d_lng, d_lnb, d_b2, d_kv, d_q, d_f0, d_f1, d_fin, loss_row]
    gp, gp_spans = _pack_rows(small_g, d)
    (gpg,) = _all_gather([gp], "gather_small_grads")

    two = lambda t: t.reshape(-1, t.shape[-1])

    def adam(w, m, v, names, tag, swapped=False):
        view = (lambda t: jnp.swapaxes(t, 1, 2)) if swapped else (lambda t: t)
        w, m, v = view(w), view(m), view(v)
        res = None
        for part, n in enumerate(names):
            res = _adamw_big(two(w), two(m), two(v), chip_sums[n], cross[n], q_idx, f"adamw_{tag}{part}", part, res)
        return [view(t.reshape(w.shape)) for t in res]

    big_out = [
        adam(conv_w1, m_conv_w1, v_conv_w1, ["w1"], "w1"), adam(conv_w2, m_conv_w2, v_conv_w2, ["w2"], "w2"),
        adam(w_k, m_w_k, v_w_k, ["wk"], "wk"), adam(w_v, m_w_v, v_w_v, ["wv"], "wv"),
        adam(w_q, m_w_q, v_w_q, ["wq"], "wq"), adam(w_o, m_w_o, v_w_o, ["wo"], "wo"),
        adam(ffn_w_gate, m_ffn_w_gate, v_ffn_w_gate, ["wg0", "wg1"], "wg", swapped=True),
        adam(ffn_w_up, m_ffn_w_up, v_ffn_w_up, ["wu0", "wu1"], "wu", swapped=True),
        adam(ffn_w_down, m_ffn_w_down, v_ffn_w_down, ["wd0", "wd1"], "wd")]

    gsum = _sum_devices(gpg, "sum_small_grads").reshape(-1)

    def gfull(i):
        at, size = gp_spans[i]
        return gsum[at:at + size]

    def shard_of(vec, rows):
        return lax.dynamic_slice_in_dim(vec.reshape(rows, -1), me * (vec.size // rows // ND), vec.size // rows // ND, axis=1)

    loss = gfull(12)[0]
    small_grads = [
        shard_of(gfull(0), 1), shard_of(gfull(1), 1), shard_of(gfull(2), CONV_W)[None], shard_of(gfull(3), 1),
        shard_of(gfull(4), 1), shard_of(gfull(5), 1), shard_of(gfull(6), 1),
        gfull(7), gfull(8)[None], jnp.stack([gfull(9), gfull(10)]), gfull(11)]
    small_w = [a_norm_g, conv_b1, conv_dw, conv_dw_b, conv_ln_g, conv_ln_b, conv_b2, kv_norm_g, b_norm_g, ffn_norm_g, final_norm_g]
    small_m = [m_a_norm_g, m_conv_b1, m_conv_dw, m_conv_dw_b, m_conv_ln_g, m_conv_ln_b, m_conv_b2, m_kv_norm_g, m_b_norm_g, m_ffn_norm_g, m_final_norm_g]
    small_v = [v_a_norm_g, v_conv_b1, v_conv_dw, v_conv_dw_b, v_conv_ln_g, v_conv_ln_b, v_conv_b2, v_kv_norm_g, v_b_norm_g, v_ffn_norm_g, v_final_norm_g]
    small_grads = [g.reshape(w.shape) for g, w in zip(small_grads, small_w)]
    wp, spans = _pack_rows(small_w, 128)
    gpk, _ = _pack_rows(small_grads, 128)
    mp, _ = _pack_rows(small_m, 128)
    vp, _ = _pack_rows(small_v, 128)
    dp, mnp, vnp = _adamw_small(wp, gpk, mp, vp, "adamw_small")

    def unpack(packed):
        flat = packed.reshape(-1)
        return [flat[at:at + size].reshape(w.shape) for (at, size), w in zip(spans, small_w)]

    small_out = list(zip(small_grads, unpack(dp), unpack(mnp), unpack(vnp)))

    order = ["a_norm_g", "conv_w1", "conv_b1", "conv_dw", "conv_dw_b", "conv_ln_g", "conv_ln_b", "conv_w2", "conv_b2",
             "kv_norm_g", "w_k", "w_v", "b_norm_g", "w_q", "w_o", "ffn_norm_g", "ffn_w_gate", "ffn_w_up", "ffn_w_down",
             "final_norm_g"]
    big_names = ["conv_w1", "conv_w2", "w_k", "w_v", "w_q", "w_o", "ffn_w_gate", "ffn_w_up", "ffn_w_down"]
    small_names = ["a_norm_g", "conv_b1", "conv_dw", "conv_dw_b", "conv_ln_g", "conv_ln_b", "conv_b2", "kv_norm_g",
                   "b_norm_g", "ffn_norm_g", "final_norm_g"]
    table = {n: big_out[i] for i, n in enumerate(big_names)}
    table.update({n: small_out[i] for i, n in enumerate(small_names)})
    result = [loss, dx[None]]
    for kind in range(4):
        result += [table[n][kind] for n in order]
    return tuple(result)
```

```python
import functools

import jax
import jax.numpy as jnp
from jax import lax
from jax.experimental import pallas as pl
from jax.experimental.pallas import tpu as pltpu

ND = 8
HEAD = 128
BLK = 128
BRANCH_DILATIONS = (1, 4, 16)
CONV_W = 31
CONV_PAD = 32
RMS_EPS = 1e-6
LN_EPS = 1e-5
LR, B1, B2, ADAM_EPS, WD, STEP = 0.001, 0.9, 0.999, 1e-08, 0.01, 10
VMEM_LIMIT = 56 * 1024 * 1024
RESIDENT_BYTES = 4 * 1024 * 1024

F32, BF16 = jnp.float32, jnp.bfloat16
SDS = jax.ShapeDtypeStruct
MESH = pl.DeviceIdType.MESH
ANY = pl.BlockSpec(memory_space=pl.ANY)

NN = (((1,), (0,)), ((), ()))
NT = (((1,), (1,)), ((), ()))
TN = (((0,), (0,)), ((), ()))


def _dot(a, b, dims):
    return lax.dot_general(a, b, dims, preferred_element_type=F32)


def _cp(*sem):
    return pltpu.CompilerParams(dimension_semantics=sem, vmem_limit_bytes=VMEM_LIMIT)


def _slot(dev):
    return 4 * (dev % 2) + dev // 2


def _sigmoid(v):
    return 1.0 / (1.0 + jnp.exp(-v))


class _Job:
    def __init__(self, ins, out_shapes, alias, nsem, nlocal, make):
        self.ins, self.out_shapes, self.alias = list(ins), list(out_shapes), dict(alias)
        self.nsem, self.nlocal, self.make = nsem, nlocal, make
        self.result = None


def _coords():
    return lax.axis_index("x"), lax.axis_index("y"), lax.axis_index("c")


def _remote(src, dst, send, recv, k, to):
    return pltpu.make_async_remote_copy(src_ref=src, dst_ref=dst, send_sem=send.at[k], recv_sem=recv.at[k],
                                        device_id=to, device_id_type=MESH)


def _job_gather_send(shards):
    n = len(shards)

    def make(ins, outs, send, recv, local):
        x, y, c = _coords()
        targets = [(x, y, 1 - c), (1 - x, y, c), (x, 1 - y, c), (1 - x, 1 - y, c)]
        cps = []
        for a in range(n):
            dst = outs[a].at[4 * x + 2 * y + c]
            cps.append(pltpu.make_async_copy(ins[a], dst, local.at[a]))
            cps += [_remote(ins[a], dst, send, recv, 4 * a + k, t) for k, t in enumerate(targets)]
        return cps

    return _Job(shards, [SDS((ND,) + s.shape, s.dtype) for s in shards], {}, 4 * n, n, make)


def _job_gather_send_rows(shard, part, nparts, prev=None):
    rows = shard.shape[0] // nparts

    def make(ins, outs, send, recv, local):
        x, y, c = _coords()
        targets = [(x, y, 1 - c), (1 - x, y, c), (x, 1 - y, c), (1 - x, 1 - y, c)]
        src = ins[0].at[pl.ds(part * rows, rows)]
        dst = outs[0].at[4 * x + 2 * y + c].at[pl.ds(part * rows, rows)]
        return [pltpu.make_async_copy(src, dst, local.at[0])] + [
            _remote(src, dst, send, recv, k, t) for k, t in enumerate(targets)]

    ins = [shard] if prev is None else [shard, prev]
    return _Job(ins, [SDS((ND,) + shard.shape, shard.dtype)], {} if prev is None else {1: 0}, 4, 1, make)


def _job_gather_forward(gathered):
    n = len(gathered)

    def make(ins, outs, send, recv, local):
        x, y, c = _coords()
        cps = []
        for a in range(n):
            for k, (px, py) in enumerate([(1 - x, y), (x, 1 - y), (1 - x, 1 - y)]):
                blk = outs[a].at[4 * px + 2 * py + c]
                cps.append(_remote(blk, blk, send, recv, 3 * a + k, (x, y, 1 - c)))
        return cps

    return _Job(gathered, [SDS(g.shape, g.dtype) for g in gathered], {i: i for i in range(n)}, 3 * n, 0, make)


def _job_scatter_sibling(grads):
    n = len(grads)

    def make(ins, outs, send, recv, local):
        x, y, c = _coords()
        return [_remote(ins[a].at[pl.ds(4 * (1 - c), 4)], outs[a], send, recv, a, (x, y, 1 - c)) for a in range(n)]

    return _Job(grads, [SDS((4,) + g.shape[1:], g.dtype) for g in grads], {}, n, 0, make)


def _job_scatter_cross(sums):
    n = len(sums)

    def make(ins, outs, send, recv, local):
        x, y, c = _coords()
        chips = [(1 - x, y), (x, 1 - y), (1 - x, 1 - y)]
        return [_remote(ins[a].at[2 * px + py], outs[a].at[k], send, recv, 3 * a + k, (px, py, c))
                for a in range(n) for k, (px, py) in enumerate(chips)]

    return _Job(sums, [SDS((3,) + t.shape[1:], t.dtype) for t in sums], {}, 3 * n, 0, make)


def _pc(body, *, name, grid, in_specs, out_specs, out_shape, args, scratch=(), sem=(), alias=None, jobs=()):
    jobs = list(jobs)
    n_in, n_out, n_scr = len(in_specs), len(out_shape), len(scratch)
    aliases = dict(alias or {})
    job_args, job_shapes, job_scratch = [], [], []
    for j in jobs:
        for src, dst in j.alias.items():
            aliases[n_in + len(job_args) + src] = n_out + len(job_shapes) + dst
        job_args += j.ins
        job_shapes += j.out_shapes
        job_scratch += [pltpu.SemaphoreType.DMA((j.nsem,)), pltpu.SemaphoreType.DMA((j.nsem,)),
                        pltpu.SemaphoreType.DMA((max(j.nlocal, 1),))]

    def wrapped(*refs):
        ins = refs[:n_in]
        p = n_in + len(job_args)
        outs = refs[p:p + n_out]
        p += n_out + len(job_shapes)
        scr = refs[p:p + n_scr]
        sems = refs[p + n_scr:]
        copies = []
        pi, po = n_in, n_in + len(job_args) + n_out
        for k, j in enumerate(jobs):
            copies += j.make(refs[pi:pi + len(j.ins)], refs[po:po + len(j.out_shapes)], *sems[3 * k:3 * k + 3])
            pi += len(j.ins)
            po += len(j.out_shapes)
        gridded = bool(copies) and bool(grid)
        if gridded:
            ids = [pl.program_id(i) for i in range(len(grid))]
            first = functools.reduce(jnp.logical_and, [i == 0 for i in ids])
            last = functools.reduce(jnp.logical_and, [i == g - 1 for i, g in zip(ids, grid)])

            @pl.when(first)
            def _():
                for cp in copies:
                    cp.start()
        else:
            for cp in copies:
                cp.start()
        body(*ins, *outs, *scr)
        if gridded:
            @pl.when(last)
            def _():
                for cp in copies:
                    cp.wait()
        else:
            for cp in copies:
                cp.wait()

    kw = dict(grid=grid) if grid else {}
    semantics = ["arbitrary"] * len(grid) if jobs else list(sem)
    res = pl.pallas_call(
        wrapped, name=name, in_specs=list(in_specs) + [ANY] * len(job_args),
        out_specs=list(out_specs) + [ANY] * len(job_shapes), out_shape=list(out_shape) + job_shapes,
        scratch_shapes=list(scratch) + job_scratch, input_output_aliases=aliases,
        compiler_params=_cp(*semantics), **kw)(*args, *job_args)
    p = n_out
    for j in jobs:
        j.result = list(res[p:p + len(j.out_shapes)])
        p += len(j.out_shapes)
    return list(res[:n_out])


def _comm_call(jobs, name):
    _pc(lambda: None, name=name, grid=(), in_specs=[], out_specs=[], out_shape=[], args=[], jobs=jobs)


def _all_gather(arrs, name):
    n = len(arrs)

    def body(*refs):
        ins, outs = refs[:n], refs[n:2 * n]
        send_sems, recv_sems, local_sems = refs[2 * n:]
        x, y, c = lax.axis_index("x"), lax.axis_index("y"), lax.axis_index("c")
        me, sib = (x, y, c), (x, y, 1 - c)
        chips = [(1 - x, y), (x, 1 - y), (1 - x, 1 - y)]

        def copy(a, k, block, to, src=None):
            dst = outs[a].at[4 * block[0] + 2 * block[1] + block[2]]
            return pltpu.make_async_remote_copy(
                src_ref=dst if src is None else src, dst_ref=dst,
                send_sem=send_sems.at[7 * a + k], recv_sem=recv_sems.at[7 * a + k],
                device_id=to, device_id_type=MESH)

        mine = [pltpu.make_async_copy(ins[a], outs[a].at[4 * x + 2 * y + c], local_sems.at[a]) for a in range(n)]
        for cp in mine:
            cp.start()
        first = []
        for a in range(n):
            first.append(copy(a, 0, me, sib, src=ins[a]))
            first += [copy(a, 1 + j, me, (*chip, c), src=ins[a]) for j, chip in enumerate(chips)]
        for cp in first:
            cp.start()
        passed = []
        for a in range(n):
            for j, chip in enumerate(chips):
                copy(a, 1 + j, (*chip, c), me).wait_recv()
                fwd = copy(a, 4 + j, (*chip, c), sib)
                fwd.start()
                passed.append(fwd)
        for a in range(n):
            copy(a, 0, sib, me).wait_recv()
            for j, chip in enumerate(chips):
                copy(a, 4 + j, (*chip, 1 - c), me).wait_recv()
        for cp in first + passed:
            cp.wait_send()
        for cp in mine:
            cp.wait()

    return pl.pallas_call(
        body, name=name,
        out_shape=[SDS((ND,) + a.shape, a.dtype) for a in arrs],
        in_specs=[ANY] * n, out_specs=[ANY] * n,
        scratch_shapes=[pltpu.SemaphoreType.DMA((7 * n,)), pltpu.SemaphoreType.DMA((7 * n,)),
                        pltpu.SemaphoreType.DMA((n,))],
    )(*arrs)


ELEMENTWISE_TILE_BYTES = 3 * 512 * 1024


def _row_tile(rows, cols):
    fits = [t for t in range(16, rows + 1, 16) if rows % t == 0 and 4 * t * cols <= ELEMENTWISE_TILE_BYTES]
    return max(fits)


def _rs_add(g, r1, c_idx, name):
    _, rows, cols = g.shape

    def body(c_ref, g_ref, r_ref, o_ref):
        o_ref[...] = (g_ref[...].astype(F32) + r_ref[...].astype(F32)).astype(o_ref.dtype)

    return pl.pallas_call(
        body, name=name,
        grid_spec=pltpu.PrefetchScalarGridSpec(
            num_scalar_prefetch=1, grid=(4,),
            in_specs=[pl.BlockSpec((1, rows, cols), lambda q, c: (4 * c[0] + q, 0, 0)),
                      pl.BlockSpec((1, rows, cols), lambda q, c: (q, 0, 0))],
            out_specs=pl.BlockSpec((1, rows, cols), lambda q, c: (q, 0, 0))),
        out_shape=SDS((4, rows, cols), g.dtype),
        compiler_params=_cp("parallel"),
    )(c_idx, g, r1)


def _adam_math(w, g, m, v):
    m = B1 * m + (1.0 - B1) * g
    v = B2 * v + (1.0 - B2) * (g * g)
    m_hat = m / (1.0 - B1 ** STEP)
    v_hat = v / (1.0 - B2 ** STEP)
    delta = -LR * (m_hat / (jnp.sqrt(v_hat) + ADAM_EPS) + WD * w)
    return delta, m, v


def _adamw_big(w, m, v, t, r2, q_idx, name, part=0, prev=None):
    _, rows, cols = t.shape
    tr = _row_tile(rows, cols)
    nblk = rows // tr

    def body(q_ref, w_ref, m_ref, v_ref, t_ref, r_ref, *outs):
        g_out, d_out, m_out, v_out = outs[-4:]
        g = t_ref[0].astype(F32)
        for k in range(3):
            g = g + r_ref[k].astype(F32)
        d, mn, vn = _adam_math(w_ref[...], g, m_ref[...], v_ref[...])
        g_out[...], d_out[...], m_out[...], v_out[...] = g, d, mn, vn

    blk = pl.BlockSpec((tr, cols), lambda i, q: (part * nblk + i, 0))
    specs = [blk, blk, blk, pl.BlockSpec((1, tr, cols), lambda i, q: (q[0], i, 0)),
             pl.BlockSpec((3, tr, cols), lambda i, q: (0, i, 0))]
    ins = [q_idx, w, m, v, t, r2]
    alias = {}
    if prev is not None:
        specs += [ANY] * 4
        alias = {6 + k: k for k in range(4)}
        ins += list(prev)
    return pl.pallas_call(
        body, name=name,
        grid_spec=pltpu.PrefetchScalarGridSpec(num_scalar_prefetch=1, grid=(nblk,), in_specs=specs, out_specs=[blk] * 4),
        out_shape=[SDS(w.shape, F32)] * 4, input_output_aliases=alias,
        compiler_params=_cp("parallel"))(*ins)


def _sum_devices(g, name):
    _, rows, cols = g.shape

    def body(g_ref, o_ref):
        acc = g_ref[0]
        for k in range(1, ND):
            acc = acc + g_ref[k]
        o_ref[...] = acc

    return pl.pallas_call(body, name=name, out_shape=SDS((rows, cols), F32))(g)


def _adamw_small(w, g, m, v, name):
    def body(w_ref, g_ref, m_ref, v_ref, d_out, m_out, v_out):
        d, mn, vn = _adam_math(w_ref[...], g_ref[...], m_ref[...], v_ref[...])
        d_out[...], m_out[...], v_out[...] = d, mn, vn

    return pl.pallas_call(body, name=name, out_shape=[SDS(w.shape, F32)] * 3)(w, g, m, v)


ROWS = 256


def _rms_stats(x):
    r = lax.rsqrt(jnp.mean(x * x, axis=-1, keepdims=True) + RMS_EPS)
    return x * r, r


def _rms_fwd(x, gains, name, jobs=()):
    s, d = x.shape
    n = len(gains)

    def body(x_ref, *refs):
        xh, _ = _rms_stats(x_ref[...])
        for g_ref, o_ref in zip(refs[:n], refs[n:]):
            o_ref[...] = (xh * g_ref[...]).astype(BF16)

    row = pl.BlockSpec((ROWS, d), lambda i: (i, 0))
    vec = pl.BlockSpec((1, d), lambda i: (0, 0))
    return _pc(body, name=name, grid=(s // ROWS,), in_specs=[row] + [vec] * n, out_specs=[row] * n,
               out_shape=[SDS((s, d), BF16)] * n, sem=("parallel",), args=(x, *gains), jobs=jobs)


def _rms_bwd_rows(xh, r, gain, dy):
    u = dy * gain
    return r * (u - xh * jnp.mean(u * xh, axis=-1, keepdims=True))


def _rms_bwd(x, pairs, dres, name, colsum=False):
    s, d = x.shape
    n = len(pairs)

    def body(x_ref, dres_ref, *refs):
        g_refs, dy_refs = refs[:n], refs[n:2 * n]
        dx_ref, dxb_ref = refs[2 * n], refs[2 * n + 1]
        dg_refs = refs[2 * n + 2:2 * n + 2 + n]
        cs_ref = refs[-1] if colsum else None
        first = pl.program_id(0) == 0
        xh, r = _rms_stats(x_ref[...])
        dx = dres_ref[...]
        for g_ref, dy_ref, dg_ref in zip(g_refs, dy_refs, dg_refs):
            dy = dy_ref[...]
            dx = dx + _rms_bwd_rows(xh, r, g_ref[...], dy)

            @pl.when(first)
            def _():
                dg_ref[...] = jnp.zeros_like(dg_ref)
            dg_ref[...] += jnp.sum(dy * xh, axis=0, keepdims=True)
        dx_ref[...] = dx
        dxb_ref[...] = dx.astype(BF16)
        if colsum:
            @pl.when(first)
            def _():
                cs_ref[...] = jnp.zeros_like(cs_ref)
            cs_ref[...] += jnp.sum(dx, axis=0, keepdims=True)

    row = pl.BlockSpec((ROWS, d), lambda i: (i, 0))
    vec = pl.BlockSpec((1, d), lambda i: (0, 0))
    nvec = n + (1 if colsum else 0)
    outs = pl.pallas_call(
        body, name=name, grid=(s // ROWS,),
        in_specs=[row, row] + [vec] * n + [row] * n,
        out_specs=[row, row] + [vec] * nvec,
        out_shape=[SDS((s, d), F32), SDS((s, d), BF16)] + [SDS((1, d), F32)] * nvec,
        compiler_params=_cp("arbitrary"),
    )(x, dres, *[p[0] for p in pairs], *[p[1] for p in pairs])
    return outs


def _final_loss(h, target, gain):
    s, d = h.shape

    def body(h_ref, t_ref, g_ref, dh_ref, dhb_ref, dg_ref, loss_ref):
        first = pl.program_id(0) == 0
        xh, r = _rms_stats(h_ref[...])
        gain_v = g_ref[...]
        e = xh * gain_v - t_ref[...]
        dy = e * (1.0 / d)
        dx = _rms_bwd_rows(xh, r, gain_v, dy)
        dh_ref[...] = dx
        dhb_ref[...] = dx.astype(BF16)

        @pl.when(first)
        def _():
            dg_ref[...] = jnp.zeros_like(dg_ref)
            loss_ref[...] = jnp.zeros_like(loss_ref)
        dg_ref[...] += jnp.sum(dy * xh, axis=0, keepdims=True)
        loss_ref[...] += jnp.full((1, 128), 0.5 / d, F32) * jnp.sum(e * e)

    row = pl.BlockSpec((ROWS, d), lambda i: (i, 0))
    vec = pl.BlockSpec((1, d), lambda i: (0, 0))
    return pl.pallas_call(
        body, name="final_loss", grid=(s // ROWS,),
        in_specs=[row, row, vec], out_specs=[row, row, vec, pl.BlockSpec((1, 128), lambda i: (0, 0))],
        out_shape=[SDS((s, d), F32), SDS((s, d), BF16), SDS((1, d), F32), SDS((1, 128), F32)],
        compiler_params=_cp("arbitrary"))(h, target, gain)


CT = 128


def _ln_stats(cv):
    mu = jnp.mean(cv, axis=-1, keepdims=True)
    xc = cv - mu
    rstd = lax.rsqrt(jnp.mean(xc * xc, axis=-1, keepdims=True) + LN_EPS)
    return xc * rstd, rstd


def _conv_fwd(glu, dw, dwb, lng, lnb, jobs=()):
    s, d = glu.shape
    hb = CT // CONV_PAD

    def body(x_ref, halo_ref, dw_ref, dwb_ref, lng_ref, lnb_ref, c_ref, s_ref):
        keep = (pl.program_id(0) > 0).astype(F32)

        def chunk(ci, carry):
            ls = pl.ds(pl.multiple_of(ci * 128, 128), 128)
            xf = jnp.concatenate([halo_ref[:, ls] * keep, x_ref[:, ls]], axis=0)
            acc = jnp.zeros((CT, 128), F32)
            for k in range(CONV_W):
                sh = CONV_W - 1 - k
                xs = pltpu.roll(xf, sh, 0) if sh else xf
                acc = acc + dw_ref[pl.ds(k, 1), ls] * xs[CONV_PAD:]
            c_ref[:, ls] = acc + dwb_ref[:, ls]
            return carry

        lax.fori_loop(0, d // 128, chunk, 0)
        xh, _ = _ln_stats(c_ref[...])
        yv = xh * lng_ref[...] + lnb_ref[...]
        s_ref[...] = (yv * _sigmoid(yv)).astype(BF16)

    row = pl.BlockSpec((CT, d), lambda i: (i, 0))
    halo = pl.BlockSpec((CONV_PAD, d), lambda i: (jnp.maximum(i * hb - 1, 0), 0))
    vec = pl.BlockSpec((1, d), lambda i: (0, 0))
    taps = pl.BlockSpec((CONV_PAD, d), lambda i: (0, 0))
    return _pc(
        body, name="conv_fwd", grid=(s // CT,),
        in_specs=[row, halo, taps, vec, vec, vec], out_specs=[row, row],
        out_shape=[SDS((s, d), F32), SDS((s, d), BF16)], sem=("parallel",),
        args=(glu, glu, dw, dwb, lng, lnb), jobs=jobs)


def _ln_bwd(ds, cv, lng, lnb):
    s, d = cv.shape

    def body(ds_ref, c_ref, g_ref, b_ref, dc_ref, dg_ref, db_ref):
        first = pl.program_id(0) == 0
        xh, rstd = _ln_stats(c_ref[...])
        gv = g_ref[...]
        yv = xh * gv + b_ref[...]
        sg = _sigmoid(yv)
        dln = ds_ref[...] * (sg * (1.0 + yv * (1.0 - sg)))
        dxh = dln * gv
        dc_ref[...] = rstd * (dxh - jnp.mean(dxh, axis=-1, keepdims=True)
                              - xh * jnp.mean(dxh * xh, axis=-1, keepdims=True))

        @pl.when(first)
        def _():
            dg_ref[...] = jnp.zeros_like(dg_ref)
            db_ref[...] = jnp.zeros_like(db_ref)
        dg_ref[...] += jnp.sum(dln * xh, axis=0, keepdims=True)
        db_ref[...] += jnp.sum(dln, axis=0, keepdims=True)

    row = pl.BlockSpec((ROWS, d), lambda i: (i, 0))
    vec = pl.BlockSpec((1, d), lambda i: (0, 0))
    return pl.pallas_call(
        body, name="ln_bwd", grid=(s // ROWS,), in_specs=[row, row, vec, vec], out_specs=[row, vec, vec],
        out_shape=[SDS((s, d), F32), SDS((1, d), F32), SDS((1, d), F32)],
        compiler_params=_cp("arbitrary"))(ds, cv, lng, lnb)


def _conv_bwd(dc, glu, ua, ug, dw, jobs=()):
    s, d = dc.shape
    hb = CT // CONV_PAD
    nsteps = s // CT
    full = CT + CONV_PAD

    def body(dc_ref, dcn_ref, x_ref, xp_ref, ua_ref, ug_ref, dw_ref, du_ref, ddw_ref, ddwb_ref, db1_ref, part):
        i = pl.program_id(0)
        keep_prev = (i > 0).astype(F32)
        keep_next = (i < nsteps - 1).astype(F32)

        @pl.when(i == 0)
        def _():
            part[...] = jnp.zeros_like(part)
            ddwb_ref[...] = jnp.zeros_like(ddwb_ref)
            db1_ref[...] = jnp.zeros_like(db1_ref)

        def chunk(ci, carry):
            off = pl.multiple_of(ci * 128, 128)
            ls = pl.ds(off, 128)
            ls2 = pl.ds(pl.multiple_of(d + ci * 128, 128), 128)
            dcc = dc_ref[:, ls]
            dcf = jnp.concatenate([dcc, dcn_ref[:, ls] * keep_next], axis=0)
            xf = jnp.concatenate([xp_ref[:, ls] * keep_prev, x_ref[:, ls]], axis=0)
            dglu = jnp.zeros((CT, 128), F32)
            for k in range(CONV_W):
                sh = CONV_W - 1 - k
                dshift = pltpu.roll(dcf, full - sh, 0) if sh else dcf
                dglu = dglu + dw_ref[pl.ds(k, 1), ls] * dshift[:CT]
                xs = pltpu.roll(xf, sh, 0) if sh else xf
                part[pl.ds(8 * k, 8), ls] += jnp.sum((dcc * xs[CONV_PAD:]).reshape(CT // 8, 8, 128), axis=0)
            ddwb_ref[:, ls] += jnp.sum(dcc, axis=0, keepdims=True)
            av, gv = ua_ref[:, ls], ug_ref[:, ls]
            sg = _sigmoid(gv)
            da = dglu * sg
            dgt = dglu * av * sg * (1.0 - sg)
            du_ref[:, ls] = da.astype(BF16)
            du_ref[:, ls2] = dgt.astype(BF16)
            db1_ref[:, ls] += jnp.sum(da, axis=0, keepdims=True)
            db1_ref[:, ls2] += jnp.sum(dgt, axis=0, keepdims=True)
            return carry

        lax.fori_loop(0, d // 128, chunk, 0)

        @pl.when(i == nsteps - 1)
        def _():
            ddw_ref[...] = jnp.sum(part[...].reshape(CONV_PAD, 8, d), axis=1)

    row = pl.BlockSpec((CT, d), lambda i: (i, 0))
    prev = pl.BlockSpec((CONV_PAD, d), lambda i: (jnp.maximum(i * hb - 1, 0), 0))
    nxt = pl.BlockSpec((CONV_PAD, d), lambda i: (jnp.minimum((i + 1) * hb, s // CONV_PAD - 1), 0))
    taps = pl.BlockSpec((CONV_PAD, d), lambda i: (0, 0))
    return _pc(
        body, name="conv_bwd", grid=(nsteps,),
        in_specs=[row, nxt, row, prev, row, row, taps],
        out_specs=[pl.BlockSpec((CT, 2 * d), lambda i: (i, 0)), taps, pl.BlockSpec((1, d), lambda i: (0, 0)),
                   pl.BlockSpec((1, 2 * d), lambda i: (0, 0))],
        out_shape=[SDS((s, 2 * d), BF16), SDS((CONV_PAD, d), F32), SDS((1, d), F32), SDS((1, 2 * d), F32)],
        scratch=[pltpu.VMEM((8 * CONV_PAD, d), F32)],
        sem=("arbitrary",), args=(dc, dc, glu, glu, ua, ug, dw), jobs=jobs)


TM = 1024
TS = 1024


def _glu_mm(n1, w1g, b1, jobs=()):
    s, d = n1.shape
    cw = w1g.shape[2]
    half = ND // 2

    def body(a_ref, wa_ref, wg_ref, ba_ref, bg_ref, ua_ref, ug_ref, glu_ref):
        a = a_ref[...]
        ua = _dot(a, wa_ref[0], NN) + ba_ref[...]
        ug = _dot(a, wg_ref[0], NN) + bg_ref[...]
        ua_ref[...], ug_ref[...] = ua, ug
        glu_ref[...] = ua * _sigmoid(ug)

    out = pl.BlockSpec((TM, cw), lambda m, i: (m, i))
    return _pc(
        body, name="glu_mm", grid=(s // TM, half),
        in_specs=[pl.BlockSpec((TM, d), lambda m, i: (m, 0)),
                  pl.BlockSpec((1, d, cw), lambda m, i: (i, 0, 0)),
                  pl.BlockSpec((1, d, cw), lambda m, i: (i + half, 0, 0)),
                  pl.BlockSpec((1, cw), lambda m, i: (0, i)),
                  pl.BlockSpec((1, cw), lambda m, i: (0, i + half))],
        out_specs=[out, out, out], out_shape=[SDS((s, d), F32)] * 3,
        sem=("parallel", "arbitrary"), args=(n1, w1g, w1g, b1, b1), jobs=jobs)


def _mm_rows(a, wg, name, res=None, bias=None, out_dtype=F32, tn=512, branches=False, jobs=()):
    s, kdim = a.shape
    _, kc, n = wg.shape
    assert kc * ND == kdim
    nx = 2 + (res is not None) + (bias is not None)

    def body(*refs):
        acc = _dot(refs[0][...], refs[1][...].reshape(kdim, tn), NN)
        for extra in refs[2:nx]:
            acc = acc + extra[...]
        refs[nx][...] = acc.astype(out_dtype)
        if branches:
            scr = refs[-1]
            _stage(scr, acc)
            for o_ref, dil in zip(refs[nx + 1:], SPLIT_DILATIONS):
                _split_rows(scr, o_ref, dil)

    ins, specs = [a, wg], [pl.BlockSpec((TM, kdim), lambda m, j: (m, 0)), pl.BlockSpec((ND, kc, tn), lambda m, j: (0, 0, j))]
    if res is not None:
        ins.append(res)
        specs.append(pl.BlockSpec((TM, tn), lambda m, j: (m, j)))
    if bias is not None:
        ins.append(bias)
        specs.append(pl.BlockSpec((1, tn), lambda m, j: (0, j)))
    out_specs, out_shape, scratch = [pl.BlockSpec((TM, tn), lambda m, j: (m, j))], [SDS((s, n), out_dtype)], []
    if branches:
        out_specs += _branch_specs(TM, tn, lambda dil, m, j: (0, m, j))
        out_shape += [SDS((dil, s // dil, n), out_dtype) for dil in SPLIT_DILATIONS]
        scratch = [pltpu.VMEM((tn // 128, TM, 128), F32)]
    outs = _pc(body, name=name, grid=(s // TM, n // tn), in_specs=specs, out_specs=out_specs, out_shape=out_shape,
               scratch=scratch, sem=("parallel", "arbitrary"), args=ins, jobs=jobs)
    return [outs[0][None]] + outs[1:] if branches else outs[0]


def _swiglu_mm(n2, wgg, wug, name, jobs=()):
    s, d = n2.shape
    fc = wgg.shape[2]

    def body(a_ref, wg_ref, wu_ref, g_ref, u_ref, act_ref):
        a = a_ref[...]
        g = _dot(a, wg_ref[0], NN)
        u = _dot(a, wu_ref[0], NN)
        g_ref[0], u_ref[0] = g.astype(BF16), u.astype(BF16)
        act_ref[0] = (g * _sigmoid(g) * u).astype(BF16)

    wspec = pl.BlockSpec((1, d, fc), lambda m, j: (j, 0, 0))
    out = pl.BlockSpec((1, TM, fc), lambda m, j: (j, m, 0))
    return _pc(
        body, name=name, grid=(s // TM, ND),
        in_specs=[pl.BlockSpec((TM, d), lambda m, j: (m, 0)), wspec, wspec],
        out_specs=[out, out, out], out_shape=[SDS((ND, s, fc), BF16)] * 3,
        sem=("parallel", "arbitrary"), args=(n2, wgg, wug), jobs=jobs)


def _down_mm(act, wdg, res, name, jobs=()):
    _, s, fc = act.shape
    d = wdg.shape[2]

    def body(a_ref, w_ref, r_ref, o_ref):
        @pl.when(pl.program_id(1) == 0)
        def _():
            o_ref[...] = r_ref[...]
        o_ref[...] += _dot(a_ref[0], w_ref[0], NN)

    row = pl.BlockSpec((TM, d), lambda m, j: (m, 0))
    return _pc(
        body, name=name, grid=(s // TM, ND),
        in_specs=[pl.BlockSpec((1, TM, fc), lambda m, j: (j, m, 0)),
                  pl.BlockSpec((1, fc, d), lambda m, j: (j, 0, 0)), row],
        out_specs=[row], out_shape=[SDS((s, d), F32)],
        sem=("parallel", "arbitrary"), args=(act, wdg, res), jobs=jobs)[0]


def _dact_mm(dh, wdg, gate, up, name, jobs=()):
    s, d = dh.shape
    fc = wdg.shape[1]

    def body(a_ref, w_ref, g_ref, u_ref, dg_ref, du_ref):
        dact = _dot(a_ref[...], w_ref[0], NT)
        g, u = g_ref[0].astype(F32), u_ref[0].astype(F32)
        sg = _sigmoid(g)
        du_ref[0] = (dact * (g * sg)).astype(BF16)
        dg_ref[0] = (dact * u * (sg * (1.0 + g * (1.0 - sg)))).astype(BF16)

    blk = pl.BlockSpec((1, TM, fc), lambda m, j: (j, m, 0))
    return _pc(
        body, name=name, grid=(s // TM, ND),
        in_specs=[pl.BlockSpec((TM, d), lambda m, j: (m, 0)),
                  pl.BlockSpec((1, fc, d), lambda m, j: (j, 0, 0)), blk, blk],
        out_specs=[blk, blk], out_shape=[SDS((ND, s, fc), BF16)] * 2,
        sem=("parallel", "arbitrary"), args=(dh, wdg, gate, up), jobs=jobs)


def _dwd_mm(act, dh, name, jobs=()):
    _, s, fc = act.shape
    d = dh.shape[1]
    nk = s // TS

    def body(a_ref, b_ref, o_ref, acc):
        k = pl.program_id(1)

        @pl.when(k == 0)
        def _():
            acc[...] = jnp.zeros_like(acc)
        acc[...] += _dot(a_ref[0], b_ref[...], TN)

        @pl.when(k == nk - 1)
        def _():
            o_ref[0] = acc[...].astype(BF16)

    return _pc(
        body, name=name, grid=(ND, nk),
        in_specs=[pl.BlockSpec((1, TS, fc), lambda j, k: (j, k, 0)), pl.BlockSpec((TS, d), lambda j, k: (k, 0))],
        out_specs=[pl.BlockSpec((1, fc, d), lambda j, k: (_slot(j), 0, 0))],
        out_shape=[SDS((ND, fc, d), BF16)], scratch=[pltpu.VMEM((fc, d), F32)],
        sem=("parallel", "arbitrary"), args=(act, dh), jobs=jobs)[0]


def _dwgu_mm(n2, dgate, dup, name, jobs=()):
    s, d = n2.shape
    fc = dgate.shape[2]
    nk = s // TS

    def body(a_ref, g_ref, u_ref, og_ref, ou_ref, accg, accu):
        k = pl.program_id(1)

        @pl.when(k == 0)
        def _():
            accg[...] = jnp.zeros_like(accg)
            accu[...] = jnp.zeros_like(accu)
        a = a_ref[...]
        accg[...] += _dot(g_ref[0], a, TN)
        accu[...] += _dot(u_ref[0], a, TN)

        @pl.when(k == nk - 1)
        def _():
            og_ref[0] = accg[...].astype(BF16)
            ou_ref[0] = accu[...].astype(BF16)

    blk = pl.BlockSpec((1, TS, fc), lambda j, k: (j, k, 0))
    out = pl.BlockSpec((1, fc, d), lambda j, k: (_slot(j), 0, 0))
    return _pc(
        body, name=name, grid=(ND, nk),
        in_specs=[pl.BlockSpec((TS, d), lambda j, k: (k, 0)), blk, blk], out_specs=[out, out],
        out_shape=[SDS((ND, fc, d), BF16)] * 2,
        scratch=[pltpu.VMEM((fc, d), F32), pltpu.VMEM((fc, d), F32)],
        sem=("parallel", "arbitrary"), args=(n2, dgate, dup), jobs=jobs)


def _dn_ffn_mm(dgate, dup, wgg, wug, name, jobs=()):
    _, s, fc = dgate.shape
    d = wgg.shape[1]

    def body(g_ref, u_ref, wg_ref, wu_ref, o_ref):
        j = pl.program_id(1)

        @pl.when(j == 0)
        def _():
            o_ref[...] = jnp.zeros_like(o_ref)
        o_ref[...] += _dot(g_ref[0], wg_ref[0], NT) + _dot(u_ref[0], wu_ref[0], NT)

    blk = pl.BlockSpec((1, TM, fc), lambda m, j: (j, m, 0))
    wspec = pl.BlockSpec((1, d, fc), lambda m, j: (j, 0, 0))
    return _pc(
        body, name=name, grid=(s // TM, ND), in_specs=[blk, blk, wspec, wspec],
        out_specs=[pl.BlockSpec((TM, d), lambda m, j: (m, 0))], out_shape=[SDS((s, d), F32)],
        sem=("parallel", "arbitrary"), args=(dgate, dup, wgg, wug), jobs=jobs)[0]


def _mm_rows_t(pairs, name, out_dtype, branches=False, jobs=()):
    s, n = pairs[0][0].shape
    _, kc, _ = pairs[0][1].shape
    np_ = len(pairs)
    grp = ND // 2
    wide = grp * kc

    def body(*refs):
        o_ref = refs[2 * np_]
        for i in range(grp):
            acc = None
            for p in range(np_):
                t = _dot(refs[p][...], refs[np_ + p][i], NT)
                acc = t if acc is None else acc + t
            o_ref[:, kc * i:kc * (i + 1)] = acc.astype(out_dtype)
            if branches:
                for c, ls in enumerate(_lane_chunks(kc)):
                    refs[-1][i * (kc // 128) + c] = acc[:, ls]
        if branches:
            for b_ref, dil in zip(refs[2 * np_ + 1:], SPLIT_DILATIONS):
                _split_rows(refs[-1], b_ref, dil)

    out_specs, out_shape, scratch = [pl.BlockSpec((TM, wide), lambda m, j: (m, j))], [SDS((s, kc * ND), out_dtype)], []
    if branches:
        out_specs += _branch_specs(TM, wide, lambda dil, m, j: (0, m, j))
        out_shape += [SDS((dil, s // dil, kc * ND), out_dtype) for dil in SPLIT_DILATIONS]
        scratch = [pltpu.VMEM((wide // 128, TM, 128), F32)]
    outs = _pc(
        body, name=name, grid=(s // TM, ND // grp),
        in_specs=[pl.BlockSpec((TM, n), lambda m, j: (m, 0))] * np_ + [pl.BlockSpec((grp, kc, n), lambda m, j: (j, 0, 0))] * np_,
        out_specs=out_specs, out_shape=out_shape, scratch=scratch,
        sem=("parallel", "arbitrary"), args=[p[0] for p in pairs] + [p[1] for p in pairs], jobs=jobs)
    return [outs[0][None]] + outs[1:] if branches else outs[0]


def _dw_rows_mm(a, b, name):
    s, kdim = a.shape
    n = b.shape[1]
    kc = kdim // ND
    ts = TS // 2
    nk = s // ts

    def body(a_ref, b_ref, o_ref, acc):
        k = pl.program_id(0)

        @pl.when(k == 0)
        def _():
            acc[...] = jnp.zeros_like(acc)
        acc[...] += _dot(a_ref[...], b_ref[...], TN)

        @pl.when(k == nk - 1)
        def _():
            for dev in range(ND):
                o_ref[_slot(dev)] = acc[kc * dev:kc * (dev + 1), :].astype(BF16)

    return pl.pallas_call(
        body, name=name, grid=(nk,),
        in_specs=[pl.BlockSpec((ts, kdim), lambda k: (k, 0)), pl.BlockSpec((ts, n), lambda k: (k, 0))],
        out_specs=pl.BlockSpec((ND, kc, n), lambda k: (0, 0, 0)), out_shape=SDS((ND, kc, n), BF16),
        scratch_shapes=[pltpu.VMEM((kdim, n), F32)], compiler_params=_cp("arbitrary"))(a, b)


def _dw1_mm(n1, du, jobs=()):
    s, d = n1.shape
    cw = du.shape[1] // ND
    nk = s // TS

    def body(a_ref, b_ref, o_ref, acc):
        k = pl.program_id(1)

        @pl.when(k == 0)
        def _():
            acc[...] = jnp.zeros_like(acc)
        acc[...] += _dot(a_ref[...], b_ref[...], TN)

        @pl.when(k == nk - 1)
        def _():
            o_ref[0] = acc[...].astype(BF16)

    return _pc(
        body, name="dw1_mm", grid=(ND, nk),
        in_specs=[pl.BlockSpec((TS, d), lambda j, k: (k, 0)), pl.BlockSpec((TS, cw), lambda j, k: (k, j))],
        out_specs=[pl.BlockSpec((1, d, cw), lambda j, k: (_slot(j), 0, 0))], out_shape=[SDS((ND, d, cw), BF16)],
        scratch=[pltpu.VMEM((d, cw), F32)], sem=("parallel", "arbitrary"), args=(n1, du), jobs=jobs)[0]


def _dn1_mm(du, w1g, part, nparts, prev=None, jobs=()):
    s = du.shape[0]
    _, d, cw = w1g.shape
    steps = s // TM // nparts
    m0 = part * steps

    def body(a_ref, w_ref, *refs):
        o_ref = refs[-1]
        j = pl.program_id(1)

        @pl.when(j == 0)
        def _():
            o_ref[...] = jnp.zeros_like(o_ref)
        o_ref[...] += _dot(a_ref[...], w_ref[0], NT)

    ins = [du, w1g] if prev is None else [du, w1g, prev]
    specs = [pl.BlockSpec((TM, cw), lambda m, j: (m0 + m, j)), pl.BlockSpec((1, d, cw), lambda m, j: (j, 0, 0))]
    return _pc(
        body, name=f"dn1_mm{part}", grid=(steps, ND), in_specs=specs if prev is None else specs + [ANY],
        out_specs=[pl.BlockSpec((TM, d), lambda m, j: (m0 + m, 0))], out_shape=[SDS((s, d), F32)],
        alias=None if prev is None else {2: 0}, sem=("parallel", "arbitrary"), args=ins, jobs=jobs)[0]


NEG = -1e30


def _slopes(heads):
    return [2.0 ** (-8.0 * (h + 1) / heads) for h in range(heads)]


def _band(has_prev):
    qi = lax.broadcasted_iota(jnp.int32, (BLK, 2 * BLK), 0)
    ki = lax.broadcasted_iota(jnp.int32, (BLK, 2 * BLK), 1)
    j = qi - ki + BLK
    ok = (j >= 0) & (j <= BLK) & (has_prev | (ki >= BLK))
    return j.astype(F32), ok


SPLIT_DILATIONS = tuple(dil for dil in BRANCH_DILATIONS if dil > 1)


def _lane_chunks(w):
    return [slice(128 * c, 128 * (c + 1)) for c in range(w // 128)]


def _stage(scr, tile):
    for c, ls in enumerate(_lane_chunks(tile.shape[1])):
        scr[c] = tile[:, ls]


def _split_rows(scr, o_ref, dil):
    _, n, w = o_ref.shape
    for r in range(dil):
        for c, ls in enumerate(_lane_chunks(w)):
            o_ref[r, :, ls] = scr[c, pl.ds(r, n, stride=dil), :].astype(o_ref.dtype)


def _join_rows(i_ref, scr, dil):
    _, n, w = i_ref.shape
    for r in range(dil):
        for c, ls in enumerate(_lane_chunks(w)):
            scr[c, pl.ds(r, n, stride=dil), :] = i_ref[r, :, ls].astype(F32)


def _unstage(scr, w):
    return jnp.concatenate([scr[c] for c in range(w // 128)], axis=1)


def _branch_specs(rows, w, index):
    return [pl.BlockSpec((dil, rows // dil, w), functools.partial(index, dil)) for dil in SPLIT_DILATIONS]


def _attn_fwd(q, k, v, dil, jobs=()):
    _, l, d = q.shape
    heads = d // HEAD
    assert heads <= HEAD
    scale = HEAD ** -0.5
    slopes = _slopes(heads)

    def body(q_ref, kc_ref, kp_ref, vc_ref, vp_ref, o_ref, lse_ref):
        jf, ok = _band(pl.program_id(1) > 0)
        lane = lax.broadcasted_iota(jnp.int32, (BLK, HEAD), 1)
        lse = jnp.zeros((BLK, HEAD), F32)
        for h in range(heads):
            sl = slice(HEAD * h, HEAD * (h + 1))
            kh = jnp.concatenate([kp_ref[0, :, sl], kc_ref[0, :, sl]], axis=0)
            vh = jnp.concatenate([vp_ref[0, :, sl], vc_ref[0, :, sl]], axis=0)
            logits = jnp.where(ok, _dot(q_ref[0, :, sl], kh, NT) * scale + jf * (-slopes[h] * dil), NEG)
            m = jnp.max(logits, axis=-1, keepdims=True)
            p = jnp.exp(logits - m)
            den = jnp.sum(p, axis=-1, keepdims=True)
            o_ref[0, :, sl] = _dot(p.astype(BF16), vh, NN) / den
            lse = jnp.where(lane == h, m + jnp.log(den), lse)
        lse_ref[0] = lse

    cur = pl.BlockSpec((1, BLK, d), lambda r, b: (r, b, 0))
    prev = pl.BlockSpec((1, BLK, d), lambda r, b: (r, jnp.maximum(b - 1, 0), 0))
    return _pc(
        body, name=f"attn_fwd_d{dil}", grid=(dil, l // BLK),
        in_specs=[cur, cur, prev, cur, prev], out_specs=[cur, pl.BlockSpec((1, BLK, HEAD), lambda r, b: (r, b, 0))],
        out_shape=[SDS((dil, l, d), F32), SDS((dil, l, HEAD), F32)], sem=("parallel", "arbitrary"),
        args=(q, k, k, v, v), jobs=jobs)


def _attn_merge(outs, lses):
    _, s, d = outs[0].shape
    heads = d // HEAD
    nb = len(outs)
    nsplit = nb - 1

    def body(*refs):
        o_refs, l_refs = refs[:nb], refs[nb:2 * nb]
        att_refs, lse_refs = refs[2 * nb:3 * nb], refs[3 * nb:4 * nb]
        scr_o, scr_l, scr_att = refs[4 * nb:4 * nb + nsplit], refs[4 * nb + nsplit:4 * nb + 2 * nsplit], refs[-1]
        ls = [l_refs[0][...]]
        for k, dil in enumerate(SPLIT_DILATIONS):
            _join_rows(o_refs[1 + k], scr_o[k], dil)
            _join_rows(l_refs[1 + k], scr_l[k], dil)
            ls.append(scr_l[k][0])
        m = functools.reduce(jnp.maximum, ls)
        ws = [jnp.exp(v - m) for v in ls]
        den = functools.reduce(jnp.add, ws)
        ws = [w / den for w in ws]
        lse_refs[0][...] = m + jnp.log(den)
        scr_l[0][0] = m + jnp.log(den)
        for h in range(heads):
            sl = slice(HEAD * h, HEAD * (h + 1))
            slab = ws[0][:, h:h + 1] * o_refs[0][:, sl]
            for k in range(nsplit):
                slab = slab + ws[1 + k][:, h:h + 1] * scr_o[k][h]
            att_refs[0][:, sl] = slab.astype(BF16)
            scr_att[h] = slab
        for k, dil in enumerate(SPLIT_DILATIONS):
            _split_rows(scr_att, att_refs[1 + k], dil)
            _split_rows(scr_l[0], lse_refs[1 + k], dil)

    def specs(w):
        return [pl.BlockSpec((ROWS, w), lambda i: (i, 0))] + _branch_specs(ROWS, w, lambda dil, i: (0, i, 0))

    def shapes(w, dt):
        return [SDS((s, w), dt)] + [SDS((dil, s // dil, w), dt) for dil in SPLIT_DILATIONS]

    wide, narrow = pltpu.VMEM((heads, ROWS, 128), F32), pltpu.VMEM((1, ROWS, 128), F32)
    res = pl.pallas_call(
        body, name="attn_merge", grid=(s // ROWS,), in_specs=specs(d) + specs(HEAD), out_specs=specs(d) + specs(HEAD),
        out_shape=shapes(d, BF16) + shapes(HEAD, F32),
        scratch_shapes=[wide] * nsplit + [narrow] * nsplit + [wide],
        compiler_params=_cp("parallel"))(outs[0].reshape(s, d), *outs[1:], lses[0].reshape(s, HEAD), *lses[1:])
    return list(res[:nb]), list(res[nb:])


def _attn_bwd(q, k, v, do, o, lse, dil, jobs=()):
    _, l, d = q.shape
    nb = l // BLK
    heads = d // HEAD
    scale = HEAD ** -0.5
    slopes = _slopes(heads)
    whole = 2 * l * d <= RESIDENT_BYTES
    steps = nb if whole else nb + 1

    def body(q_ref, kc_ref, kp_ref, vc_ref, vp_ref, do_ref, o_ref, lse_ref, dq_ref, dk_ref, dv_ref, ck, cv):
        b = pl.program_id(1)
        rows = pl.ds(pl.multiple_of(jnp.maximum(b - 1, 0) * BLK, BLK), BLK) if whole else slice(None)

        @pl.when(b == 0)
        def _():
            ck[...] = jnp.zeros_like(ck)
            cv[...] = jnp.zeros_like(cv)

        @pl.when(b < nb)
        def _():
            jf, ok = _band(b > 0)
            for h in range(heads):
                sl = slice(HEAD * h, HEAD * (h + 1))
                qh, doh = q_ref[0, :, sl], do_ref[0, :, sl]
                kh = jnp.concatenate([kp_ref[0, :, sl], kc_ref[0, :, sl]], axis=0)
                vh = jnp.concatenate([vp_ref[0, :, sl], vc_ref[0, :, sl]], axis=0)
                lse_h = lse_ref[0, :, h:h + 1]
                delta = jnp.sum(doh.astype(F32) * o_ref[0, :, sl].astype(F32), axis=-1, keepdims=True)
                p = jnp.where(ok, jnp.exp(_dot(qh, kh, NT) * scale + jf * (-slopes[h] * dil) - lse_h), 0.0)
                ds = (p * (_dot(doh, vh, NT) - delta)).astype(BF16)
                dq_ref[0, :, sl] = (_dot(ds, kh, NN) * scale).astype(BF16)
                dk2 = _dot(ds, qh, TN) * scale
                dv2 = _dot(p.astype(BF16), doh, TN)
                dk_ref[0, rows, sl] = (ck[:, sl] + dk2[:BLK]).astype(BF16)
                dv_ref[0, rows, sl] = (cv[:, sl] + dv2[:BLK]).astype(BF16)
                ck[:, sl] = dk2[BLK:]
                cv[:, sl] = dv2[BLK:]

        @pl.when(b == steps - 1)
        def _():
            last = pl.ds((nb - 1) * BLK, BLK) if whole else slice(None)
            dk_ref[0, last, :] = ck[...].astype(BF16)
            dv_ref[0, last, :] = cv[...].astype(BF16)

    cur = pl.BlockSpec((1, BLK, d), lambda r, b: (r, jnp.minimum(b, nb - 1), 0))
    prev = pl.BlockSpec((1, BLK, d), lambda r, b: (r, jnp.clip(b - 1, 0, nb - 1), 0))
    lse_spec = pl.BlockSpec((1, BLK, HEAD), lambda r, b: (r, jnp.minimum(b, nb - 1), 0))
    dkv = pl.BlockSpec((1, l, d), lambda r, b: (r, 0, 0)) if whole else prev
    return _pc(
        body, name=f"attn_bwd_d{dil}", grid=(dil, steps),
        in_specs=[cur, cur, prev, cur, prev, cur, cur, lse_spec], out_specs=[cur, dkv, dkv],
        out_shape=[SDS((dil, l, d), BF16)] * 3,
        scratch=[pltpu.VMEM((BLK, d), F32), pltpu.VMEM((BLK, d), F32)],
        sem=("parallel", "arbitrary"), args=(q, k, k, v, v, do, o, lse), jobs=jobs)


def _sum_cast(xs, name):
    _, s, d = xs[0].shape
    nsplit = len(xs) - 1

    def body(*refs):
        i_refs, o_ref, scr = refs[:nsplit + 1], refs[nsplit + 1], refs[nsplit + 2:]
        acc = i_refs[0][...].astype(F32)
        for k, dil in enumerate(SPLIT_DILATIONS):
            _join_rows(i_refs[1 + k], scr[k], dil)
            acc = acc + _unstage(scr[k], d)
        o_ref[...] = acc.astype(BF16)

    row = pl.BlockSpec((ROWS, d), lambda i: (i, 0))
    return pl.pallas_call(
        body, name=name, grid=(s // ROWS,), in_specs=[row] + _branch_specs(ROWS, d, lambda dil, i: (0, i, 0)),
        out_specs=row, out_shape=SDS((s, d), BF16),
        scratch_shapes=[pltpu.VMEM((d // 128, ROWS, 128), F32)] * nsplit,
        compiler_params=_cp("parallel"))(xs[0].reshape(s, d), *xs[1:])


def _pack_rows(vs, width):
    flat = jnp.concatenate([v.reshape(-1) for v in vs])
    spans, at = [], 0
    for v in vs:
        spans.append((at, v.size))
        at += v.size
    rows = -(-at // width)
    rows = -(-rows // 8) * 8
    flat = jnp.pad(flat, (0, rows * width - at))
    return flat.reshape(rows, width), spans


def kernel(x, a_norm_g, conv_w1, conv_b1, conv_dw, conv_dw_b, conv_ln_g, conv_ln_b, conv_w2, conv_b2, kv_norm_g, w_k, w_v, b_norm_g, w_q, w_o, ffn_norm_g, ffn_w_gate, ffn_w_up, ffn_w_down, final_norm_g, loss_target, m_a_norm_g, m_conv_w1, m_conv_b1, m_conv_dw, m_conv_dw_b, m_conv_ln_g, m_conv_ln_b, m_conv_w2, m_conv_b2, m_kv_norm_g, m_w_k, m_w_v, m_b_norm_g, m_w_q, m_w_o, m_ffn_norm_g, m_ffn_w_gate, m_ffn_w_up, m_ffn_w_down, m_final_norm_g, v_a_norm_g, v_conv_w1, v_conv_b1, v_conv_dw, v_conv_dw_b, v_conv_ln_g, v_conv_ln_b, v_conv_w2, v_conv_b2, v_kv_norm_g, v_w_k, v_w_v, v_b_norm_g, v_w_q, v_w_o, v_ffn_norm_g, v_ffn_w_gate, v_ffn_w_up, v_ffn_w_down, v_final_norm_g):
    s, d = x.shape[1], x.shape[2]
    dc = d // ND
    h0 = x[0]
    target = loss_target[0]
    xi, yi, ci = lax.axis_index("x"), lax.axis_index("y"), lax.axis_index("c")
    me = 4 * xi + 2 * yi + ci
    c_idx = jnp.reshape(ci, (1,)).astype(jnp.int32)
    q_idx = jnp.reshape(2 * xi + yi, (1,)).astype(jnp.int32)

    bf = lambda w: w.astype(BF16)
    small_shards = [a_norm_g, conv_b1, conv_dw, conv_dw_b, conv_ln_g, conv_ln_b, conv_b2]
    sp, sp_spans = _pack_rows(small_shards, dc)
    w1g, spg = _all_gather([bf(conv_w1[0]), sp], "gather_first")
    spg = spg.reshape(ND, -1)

    def small_full(i, rows):
        at, size = sp_spans[i]
        return spg[:, at:at + size].reshape(ND, rows, size // rows).transpose(1, 0, 2).reshape(rows, -1)

    a_g = small_full(0, 1)
    b1 = small_full(1, 1)
    dw = jnp.pad(small_full(2, CONV_W), ((0, CONV_PAD - CONV_W), (0, 0)))
    dwb, lng, lnb, b2 = small_full(3, 1), small_full(4, 1), small_full(5, 1), small_full(6, 1)
    kv_g, q_g, fin_g = kv_norm_g.reshape(1, d), b_norm_g.reshape(1, d), final_norm_g.reshape(1, d)
    f_g = [ffn_norm_g[0:1], ffn_norm_g[1:2]]

    def send(*shards):
        return _job_gather_send([bf(t) for t in shards])

    def forward(job):
        return _job_gather_forward(job.result)

    def send_half(w, part, first=None):
        return _job_gather_send_rows(bf(w), part, 2, None if first is None else first.result[0])

    s_w2 = send(conv_w2[0])
    (n1,) = _rms_fwd(h0, [a_g], "rms_a", jobs=[s_w2])
    f_w2, s_g0 = forward(s_w2), send(ffn_w_gate[0])
    ua, ug, glu = _glu_mm(n1, w1g, b1, jobs=[f_w2, s_g0])
    (w2g,) = f_w2.result
    f_g0, s_u0 = forward(s_g0), send(ffn_w_up[0])
    cv, sw = _conv_fwd(glu, dw, dwb, lng, lnb, jobs=[f_g0, s_u0])
    (wg0,) = f_g0.result
    f_u0 = forward(s_u0)
    h1 = _mm_rows(sw, w2g, "w2_mm", res=h0, bias=b2, jobs=[f_u0])
    (wu0,) = f_u0.result
    (n2a,) = _rms_fwd(h1, [f_g[0]], "rms_f0")
    s_d0, s_kv = send(ffn_w_down[0]), send(w_k, w_v)
    gate0, up0, act0 = _swiglu_mm(n2a, wg0, wu0, "swiglu_mm0", jobs=[s_d0, s_kv])
    f_d0 = forward(s_d0)
    _comm_call([f_d0], "forward_mid")
    (wd0,) = f_d0.result
    f_kv, s_qo = forward(s_kv), send(w_q[0], w_o[0])
    h2 = _down_mm(act0, wd0, h1, "down_mm0", jobs=[f_kv, s_qo])
    wkg, wvg = f_kv.result
    kvn, qn = _rms_fwd(h2, [kv_g, q_g], "rms_kvq")
    f_qo, s_g1a = forward(s_qo), send_half(ffn_w_gate[1], 0)
    kk = _mm_rows(kvn, wkg, "k_mm", out_dtype=BF16, branches=True, jobs=[f_qo, s_g1a])
    wqg, wog = f_qo.result
    s_g1b = send_half(ffn_w_gate[1], 1, s_g1a)
    vv = _mm_rows(kvn, wvg, "v_mm", out_dtype=BF16, branches=True, jobs=[s_g1b])
    f_g1, s_u1a = forward(s_g1b), send_half(ffn_w_up[1], 0)
    qq = _mm_rows(qn, wqg, "q_mm", out_dtype=BF16, branches=True, jobs=[f_g1, s_u1a])
    (wg1,) = f_g1.result
    branch = {dil: (qq[i], kk[i], vv[i]) for i, dil in enumerate(BRANCH_DILATIONS)}
    s_u1b = send_half(ffn_w_up[1], 1, s_u1a)
    o1, l1 = _attn_fwd(*branch[1], 1, jobs=[s_u1b])
    f_u1, s_d1a = forward(s_u1b), send_half(ffn_w_down[1], 0)
    o4, l4 = _attn_fwd(*branch[4], 4, jobs=[f_u1, s_d1a])
    (wu1,) = f_u1.result
    s_d1b = send_half(ffn_w_down[1], 1, s_d1a)
    o16, l16 = _attn_fwd(*branch[16], 16, jobs=[s_d1b])
    atts, lses = _attn_merge([o1, o4, o16], [l1, l4, l16])
    att = atts[0]
    atts, lses = [att[None]] + atts[1:], [lses[0][None]] + lses[1:]
    f_d1 = forward(s_d1b)
    h3 = _mm_rows(att, wog, "wo_mm", res=h2, jobs=[f_d1])
    (wd1,) = f_d1.result
    (n2b,) = _rms_fwd(h3, [f_g[1]], "rms_f1")
    gate1, up1, act1 = _swiglu_mm(n2b, wg1, wu1, "swiglu_mm1")
    h4 = _down_mm(act1, wd1, h3, "down_mm1")

    flat = lambda g: g.reshape(ND, -1, g.shape[-1])
    chip_sums, cross = {}, {}

    def to_sibling(**grads):
        job = _job_scatter_sibling([flat(g) for g in grads.values()])
        job.names = list(grads)
        return job

    def add_up(job):
        for n, g, r in zip(job.names, job.ins, job.result):
            chip_sums[n] = _rs_add(g, r, c_idx, f"rs_add_{n}")

    def to_chips(*names):
        job = _job_scatter_cross([chip_sums[n] for n in names])
        job.names = names
        return job

    def landed(job):
        cross.update(zip(job.names, job.result))

    dh4, dh4b, d_fin, loss_row = _final_loss(h4, target, fin_g)
    dgate1, dup1 = _dact_mm(dh4b, wd1, gate1, up1, "dact_mm1")
    g_wd1 = _dwd_mm(act1, dh4b, "dwd_mm1")
    j1 = to_sibling(wd1=g_wd1)
    g_wg1, g_wu1 = _dwgu_mm(n2b, dgate1, dup1, "dwgu_mm1", jobs=[j1])
    add_up(j1)
    j2, j3 = to_chips("wd1"), to_sibling(wg1=g_wg1, wu1=g_wu1)
    dn2b = _dn_ffn_mm(dgate1, dup1, wg1, wu1, "dn_ffn_mm1", jobs=[j2, j3])
    landed(j2)
    add_up(j3)
    dh3, dh3b, d_f1 = _rms_bwd(h3, [(f_g[1], dn2b)], dh4, "rms_f1_bwd")
    g_wo = _dw_rows_mm(att, dh3b, "dwo_mm")
    j4 = to_sibling(wo=g_wo)
    datt = _mm_rows_t([(dh3b, wog)], "datt_mm", BF16, branches=True, jobs=[j4])
    add_up(j4)
    riders = {1: to_chips("wg1"), 4: to_chips("wu1"), 16: to_chips("wo")}
    dqs, dks, dvs = [], [], []
    for i, dil in enumerate(BRANCH_DILATIONS):
        qb, kb, vb = branch[dil]
        dq_b, dk_b, dv_b = _attn_bwd(qb, kb, vb, datt[i], atts[i], lses[i], dil, jobs=[riders[dil]])
        landed(riders[dil])
        dqs.append(dq_b)
        dks.append(dk_b)
        dvs.append(dv_b)
    dq, dk, dv = _sum_cast(dqs, "dq_sum"), _sum_cast(dks, "dk_sum"), _sum_cast(dvs, "dv_sum")
    g_wq = _dw_rows_mm(qn, dq, "dwq_mm")
    g_wk = _dw_rows_mm(kvn, dk, "dwk_mm")
    g_wv = _dw_rows_mm(kvn, dv, "dwv_mm")
    j5 = to_sibling(wq=g_wq, wk=g_wk, wv=g_wv)
    dqn = _mm_rows_t([(dq, wqg)], "dqn_mm", F32, jobs=[j5])
    add_up(j5)
    j6 = to_chips("wq", "wk")
    dkvn = _mm_rows_t([(dk, wkg), (dv, wvg)], "dkvn_mm", F32, jobs=[j6])
    landed(j6)
    dh2, dh2b, d_q, d_kv = _rms_bwd(h2, [(q_g, dqn), (kv_g, dkvn)], dh3, "rms_kvq_bwd")
    j7 = to_chips("wv")
    dgate0, dup0 = _dact_mm(dh2b, wd0, gate0, up0, "dact_mm0", jobs=[j7])
    landed(j7)
    g_wd0 = _dwd_mm(act0, dh2b, "dwd_mm0")
    j8 = to_sibling(wd0=g_wd0)
    g_wg0, g_wu0 = _dwgu_mm(n2a, dgate0, dup0, "dwgu_mm0", jobs=[j8])
    add_up(j8)
    j9, j10 = to_chips("wd0"), to_sibling(wg0=g_wg0, wu0=g_wu0)
    dn2a = _dn_ffn_mm(dgate0, dup0, wg0, wu0, "dn_ffn_mm0", jobs=[j9, j10])
    landed(j9)
    add_up(j10)
    dh1, dh1b, d_f0, d_b2 = _rms_bwd(h1, [(f_g[0], dn2a)], dh2, "rms_f0_bwd", colsum=True)
    g_w2 = _dw_rows_mm(sw, dh1b, "dw2_mm")
    j11 = to_sibling(w2=g_w2)
    dsw = _mm_rows_t([(dh1b, w2g)], "dsw_mm", F32, jobs=[j11])
    add_up(j11)
    dcv, d_lng, d_lnb = _ln_bwd(dsw, cv, lng, lnb)
    j12 = to_chips("wg0", "wu0")
    du, d_dw, d_dwb, d_b1 = _conv_bwd(dcv, glu, ua, ug, dw, jobs=[j12])
    landed(j12)
    j13 = to_chips("w2")
    g_w1 = _dw1_mm(n1, du, jobs=[j13])
    landed(j13)
    j14 = to_sibling(w1=g_w1)
    _comm_call([j14], "rs_w1_sibling")
    add_up(j14)
    j15 = to_chips("w1")
    dn1 = _dn1_mm(du, w1g, 0, 1, jobs=[j15])
    landed(j15)
    dx, _, d_a = _rms_bwd(h0, [(a_g, dn1)], dh1, "rms_a_bwd")

    small_g = [d_a, d_b1, d_dw[:CONV_W], d_dwb, d_lng, d_lnb, d_b2, d_kv, d_q, d_f0, d_f1, d_fin, loss_row]
    gp, gp_spans = _pack_rows(small_g, d)
    (gpg,) = _all_gather([gp], "gather_small_grads")

    two = lambda t: t.reshape(-1, t.shape[-1])

    def adam(w, m, v, names, tag, swapped=False):
        view = (lambda t: jnp.swapaxes(t, 1, 2)) if swapped else (lambda t: t)
        w, m, v = view(w), view(m), view(v)
        res = None
        for part, n in enumerate(names):
            res = _adamw_big(two(w), two(m), two(v), chip_sums[n], cross[n], q_idx, f"adamw_{tag}{part}", part, res)
        return [view(t.reshape(w.shape)) for t in res]

    big_out = [
        adam(conv_w1, m_conv_w1, v_conv_w1, ["w1"], "w1"), adam(conv_w2, m_conv_w2, v_conv_w2, ["w2"], "w2"),
        adam(w_k, m_w_k, v_w_k, ["wk"], "wk"), adam(w_v, m_w_v, v_w_v, ["wv"], "wv"),
        adam(w_q, m_w_q, v_w_q, ["wq"], "wq"), adam(w_o, m_w_o, v_w_o, ["wo"], "wo"),
        adam(ffn_w_gate, m_ffn_w_gate, v_ffn_w_gate, ["wg0", "wg1"], "wg", swapped=True),
        adam(ffn_w_up, m_ffn_w_up, v_ffn_w_up, ["wu0", "wu1"], "wu", swapped=True),
        adam(ffn_w_down, m_ffn_w_down, v_ffn_w_down, ["wd0", "wd1"], "wd")]

    gsum = _sum_devices(gpg, "sum_small_grads").reshape(-1)

    def gfull(i):
        at, size = gp_spans[i]
        return gsum[at:at + size]

    def shard_of(vec, rows):
        return lax.dynamic_slice_in_dim(vec.reshape(rows, -1), me * (vec.size // rows // ND), vec.size // rows // ND, axis=1)

    loss = gfull(12)[0]
    small_grads = [
        shard_of(gfull(0), 1), shard_of(gfull(1), 1), shard_of(gfull(2), CONV_W)[None], shard_of(gfull(3), 1),
        shard_of(gfull(4), 1), shard_of(gfull(5), 1), shard_of(gfull(6), 1),
        gfull(7), gfull(8)[None], jnp.stack([gfull(9), gfull(10)]), gfull(11)]
    small_w = [a_norm_g, conv_b1, conv_dw, conv_dw_b, conv_ln_g, conv_ln_b, conv_b2, kv_norm_g, b_norm_g, ffn_norm_g, final_norm_g]
    small_m = [m_a_norm_g, m_conv_b1, m_conv_dw, m_conv_dw_b, m_conv_ln_g, m_conv_ln_b, m_conv_b2, m_kv_norm_g, m_b_norm_g, m_ffn_norm_g, m_final_norm_g]
    small_v = [v_a_norm_g, v_conv_b1, v_conv_dw, v_conv_dw_b, v_conv_ln_g, v_conv_ln_b, v_conv_b2, v_kv_norm_g, v_b_norm_g, v_ffn_norm_g, v_final_norm_g]
    small_grads = [g.reshape(w.shape) for g, w in zip(small_grads, small_w)]
    wp, spans = _pack_rows(small_w, 128)
    gpk, _ = _pack_rows(small_grads, 128)
    mp, _ = _pack_rows(small_m, 128)
    vp, _ = _pack_rows(small_v, 128)
    dp, mnp, vnp = _adamw_small(wp, gpk, mp, vp, "adamw_small")

    def unpack(packed):
        flat = packed.reshape(-1)
        return [flat[at:at + size].reshape(w.shape) for (at, size), w in zip(spans, small_w)]

    small_out = list(zip(small_grads, unpack(dp), unpack(mnp), unpack(vnp)))

    order = ["a_norm_g", "conv_w1", "conv_b1", "conv_dw", "conv_dw_b", "conv_ln_g", "conv_ln_b", "conv_w2", "conv_b2",
             "kv_norm_g", "w_k", "w_v", "b_norm_g", "w_q", "w_o", "ffn_norm_g", "ffn_w_gate", "ffn_w_up", "ffn_w_down",
             "final_norm_g"]
    big_names = ["conv_w1", "conv_w2", "w_k", "w_v", "w_q", "w_o", "ffn_w_gate", "ffn_w_up", "ffn_w_down"]
    small_names = ["a_norm_g", "conv_b1", "conv_dw", "conv_dw_b", "conv_ln_g", "conv_ln_b", "conv_b2", "kv_norm_g",
                   "b_norm_g", "ffn_norm_g", "final_norm_g"]
    table = {n: big_out[i] for i, n in enumerate(big_names)}
    table.update({n: small_out[i] for i, n in enumerate(small_names)})
    result = [loss, dx[None]]
    for kind in range(4):
        result += [table[n][kind] for n in order]
    return tuple(result)
```

```python
import functools

import jax
import jax.numpy as jnp
from jax import lax
from jax.experimental import pallas as pl
from jax.experimental.pallas import tpu as pltpu

ND = 8
LANES = 128
HEAD = 128
BLK = 128
BRANCH_DILATIONS = (1, 4, 16)
CONV_W = 31
CONV_PAD = 32
RMS_EPS = 1e-6
LN_EPS = 1e-5
LR, B1, B2, ADAM_EPS, WD, STEP = 0.001, 0.9, 0.999, 1e-08, 0.01, 10
VMEM_LIMIT = 56 * 1024 * 1024
RESIDENT_BYTES = 4 * 1024 * 1024

F32, BF16 = jnp.float32, jnp.bfloat16
SDS = jax.ShapeDtypeStruct
MESH = pl.DeviceIdType.MESH
ANY = pl.BlockSpec(memory_space=pl.ANY)

NN = (((1,), (0,)), ((), ()))
NT = (((1,), (1,)), ((), ()))
TN = (((0,), (0,)), ((), ()))


def _dot(a, b, dims):
    return lax.dot_general(a, b, dims, preferred_element_type=F32)


def _cp(*sem):
    return pltpu.CompilerParams(dimension_semantics=sem, vmem_limit_bytes=VMEM_LIMIT)


def _slot(dev):
    return 4 * (dev % 2) + dev // 2


def _sigmoid(v):
    return 1.0 / (1.0 + jnp.exp(-v))


class _Job:
    def __init__(self, ins, out_shapes, alias, nsem, nlocal, make):
        self.ins, self.out_shapes, self.alias = list(ins), list(out_shapes), dict(alias)
        self.nsem, self.nlocal, self.make = nsem, nlocal, make
        self.result = None


def _coords():
    return lax.axis_index("x"), lax.axis_index("y"), lax.axis_index("c")


def _remote(src, dst, send, recv, k, to):
    return pltpu.make_async_remote_copy(src_ref=src, dst_ref=dst, send_sem=send.at[k], recv_sem=recv.at[k],
                                        device_id=to, device_id_type=MESH)


def _job_gather_send(shards):
    n = len(shards)

    def make(ins, outs, send, recv, local):
        x, y, c = _coords()
        targets = [(x, y, 1 - c), (1 - x, y, c), (x, 1 - y, c), (1 - x, 1 - y, c)]
        cps = []
        for a in range(n):
            dst = outs[a].at[4 * x + 2 * y + c]
            cps.append(pltpu.make_async_copy(ins[a], dst, local.at[a]))
            cps += [_remote(ins[a], dst, send, recv, 4 * a + k, t) for k, t in enumerate(targets)]
        return cps

    return _Job(shards, [SDS((ND,) + s.shape, s.dtype) for s in shards], {}, 4 * n, n, make)


def _job_gather_send_rows(shard, part, nparts, prev=None):
    rows = shard.shape[0] // nparts

    def make(ins, outs, send, recv, local):
        x, y, c = _coords()
        targets = [(x, y, 1 - c), (1 - x, y, c), (x, 1 - y, c), (1 - x, 1 - y, c)]
        src = ins[0].at[pl.ds(part * rows, rows)]
        dst = outs[0].at[4 * x + 2 * y + c].at[pl.ds(part * rows, rows)]
        return [pltpu.make_async_copy(src, dst, local.at[0])] + [
            _remote(src, dst, send, recv, k, t) for k, t in enumerate(targets)]

    ins = [shard] if prev is None else [shard, prev]
    return _Job(ins, [SDS((ND,) + shard.shape, shard.dtype)], {} if prev is None else {1: 0}, 4, 1, make)


def _job_gather_forward(gathered):
    n = len(gathered)

    def make(ins, outs, send, recv, local):
        x, y, c = _coords()
        cps = []
        for a in range(n):
            for k, (px, py) in enumerate([(1 - x, y), (x, 1 - y), (1 - x, 1 - y)]):
                blk = outs[a].at[4 * px + 2 * py + c]
                cps.append(_remote(blk, blk, send, recv, 3 * a + k, (x, y, 1 - c)))
        return cps

    return _Job(gathered, [SDS(g.shape, g.dtype) for g in gathered], {i: i for i in range(n)}, 3 * n, 0, make)


def _job_scatter_sibling(grads):
    n = len(grads)

    def make(ins, outs, send, recv, local):
        x, y, c = _coords()
        return [_remote(ins[a].at[pl.ds(4 * (1 - c), 4)], outs[a], send, recv, a, (x, y, 1 - c)) for a in range(n)]

    return _Job(grads, [SDS((4,) + g.shape[1:], g.dtype) for g in grads], {}, n, 0, make)


def _job_scatter_cross(sums):
    n = len(sums)

    def make(ins, outs, send, recv, local):
        x, y, c = _coords()
        chips = [(1 - x, y), (x, 1 - y), (1 - x, 1 - y)]
        return [_remote(ins[a].at[2 * px + py], outs[a].at[k], send, recv, 3 * a + k, (px, py, c))
                for a in range(n) for k, (px, py) in enumerate(chips)]

    return _Job(sums, [SDS((3,) + t.shape[1:], t.dtype) for t in sums], {}, 3 * n, 0, make)


def _pc(body, *, name, grid, in_specs, out_specs, out_shape, args, scratch=(), sem=(), alias=None, jobs=()):
    jobs = list(jobs)
    n_in, n_out, n_scr = len(in_specs), len(out_shape), len(scratch)
    aliases = dict(alias or {})
    job_args, job_shapes, job_scratch = [], [], []
    for j in jobs:
        for src, dst in j.alias.items():
            aliases[n_in + len(job_args) + src] = n_out + len(job_shapes) + dst
        job_args += j.ins
        job_shapes += j.out_shapes
        job_scratch += [pltpu.SemaphoreType.DMA((j.nsem,)), pltpu.SemaphoreType.DMA((j.nsem,)),
                        pltpu.SemaphoreType.DMA((max(j.nlocal, 1),))]

    def wrapped(*refs):
        ins = refs[:n_in]
        p = n_in + len(job_args)
        outs = refs[p:p + n_out]
        p += n_out + len(job_shapes)
        scr = refs[p:p + n_scr]
        sems = refs[p + n_scr:]
        copies = []
        pi, po = n_in, n_in + len(job_args) + n_out
        for k, j in enumerate(jobs):
            copies += j.make(refs[pi:pi + len(j.ins)], refs[po:po + len(j.out_shapes)], *sems[3 * k:3 * k + 3])
            pi += len(j.ins)
            po += len(j.out_shapes)
        gridded = bool(copies) and bool(grid)
        if gridded:
            ids = [pl.program_id(i) for i in range(len(grid))]
            first = functools.reduce(jnp.logical_and, [i == 0 for i in ids])
            last = functools.reduce(jnp.logical_and, [i == g - 1 for i, g in zip(ids, grid)])

            @pl.when(first)
            def _():
                for cp in copies:
                    cp.start()
        else:
            for cp in copies:
                cp.start()
        body(*ins, *outs, *scr)
        if gridded:
            @pl.when(last)
            def _():
                for cp in copies:
                    cp.wait()
        else:
            for cp in copies:
                cp.wait()

    kw = dict(grid=grid) if grid else {}
    semantics = ["arbitrary"] * len(grid) if jobs else list(sem)
    res = pl.pallas_call(
        wrapped, name=name, in_specs=list(in_specs) + [ANY] * len(job_args),
        out_specs=list(out_specs) + [ANY] * len(job_shapes), out_shape=list(out_shape) + job_shapes,
        scratch_shapes=list(scratch) + job_scratch, input_output_aliases=aliases,
        compiler_params=_cp(*semantics), **kw)(*args, *job_args)
    p = n_out
    for j in jobs:
        j.result = list(res[p:p + len(j.out_shapes)])
        p += len(j.out_shapes)
    return list(res[:n_out])


def _comm_call(jobs, name):
    _pc(lambda: None, name=name, grid=(), in_specs=[], out_specs=[], out_shape=[], args=[], jobs=jobs)


def _all_gather(arrs, name):
    n = len(arrs)

    def body(*refs):
        ins, outs = refs[:n], refs[n:2 * n]
        send_sems, recv_sems, local_sems = refs[2 * n:]
        x, y, c = lax.axis_index("x"), lax.axis_index("y"), lax.axis_index("c")
        me, sib = (x, y, c), (x, y, 1 - c)
        chips = [(1 - x, y), (x, 1 - y), (1 - x, 1 - y)]

        def copy(a, k, block, to, src=None):
            dst = outs[a].at[4 * block[0] + 2 * block[1] + block[2]]
            return pltpu.make_async_remote_copy(
                src_ref=dst if src is None else src, dst_ref=dst,
                send_sem=send_sems.at[7 * a + k], recv_sem=recv_sems.at[7 * a + k],
                device_id=to, device_id_type=MESH)

        mine = [pltpu.make_async_copy(ins[a], outs[a].at[4 * x + 2 * y + c], local_sems.at[a]) for a in range(n)]
        for cp in mine:
            cp.start()
        first = []
        for a in range(n):
            first.append(copy(a, 0, me, sib, src=ins[a]))
            first += [copy(a, 1 + j, me, (*chip, c), src=ins[a]) for j, chip in enumerate(chips)]
        for cp in first:
            cp.start()
        passed = []
        for a in range(n):
            for j, chip in enumerate(chips):
                copy(a, 1 + j, (*chip, c), me).wait_recv()
                fwd = copy(a, 4 + j, (*chip, c), sib)
                fwd.start()
                passed.append(fwd)
        for a in range(n):
            copy(a, 0, sib, me).wait_recv()
            for j, chip in enumerate(chips):
                copy(a, 4 + j, (*chip, 1 - c), me).wait_recv()
        for cp in first + passed:
            cp.wait_send()
        for cp in mine:
            cp.wait()

    return pl.pallas_call(
        body, name=name,
        out_shape=[SDS((ND,) + a.shape, a.dtype) for a in arrs],
        in_specs=[ANY] * n, out_specs=[ANY] * n,
        scratch_shapes=[pltpu.SemaphoreType.DMA((7 * n,)), pltpu.SemaphoreType.DMA((7 * n,)),
                        pltpu.SemaphoreType.DMA((n,))],
    )(*arrs)


ELEMENTWISE_TILE_BYTES = 3 * 512 * 1024


def _row_tile(rows, cols):
    fits = [t for t in range(16, rows + 1, 16) if rows % t == 0 and 4 * t * cols <= ELEMENTWISE_TILE_BYTES]
    return max(fits)


def _rs_add(g, r1, c_idx, name):
    _, rows, cols = g.shape

    def body(c_ref, g_ref, r_ref, o_ref):
        o_ref[...] = (g_ref[...].astype(F32) + r_ref[...].astype(F32)).astype(o_ref.dtype)

    return pl.pallas_call(
        body, name=name,
        grid_spec=pltpu.PrefetchScalarGridSpec(
            num_scalar_prefetch=1, grid=(4,),
            in_specs=[pl.BlockSpec((1, rows, cols), lambda q, c: (4 * c[0] + q, 0, 0)),
                      pl.BlockSpec((1, rows, cols), lambda q, c: (q, 0, 0))],
            out_specs=pl.BlockSpec((1, rows, cols), lambda q, c: (q, 0, 0))),
        out_shape=SDS((4, rows, cols), g.dtype),
        compiler_params=_cp("parallel"),
    )(c_idx, g, r1)


def _adam_math(w, g, m, v):
    m = B1 * m + (1.0 - B1) * g
    v = B2 * v + (1.0 - B2) * (g * g)
    m_hat = m / (1.0 - B1 ** STEP)
    v_hat = v / (1.0 - B2 ** STEP)
    delta = -LR * (m_hat / (jnp.sqrt(v_hat) + ADAM_EPS) + WD * w)
    return delta, m, v


def _adamw_big(w, m, v, t, r2, q_idx, name, part=0, prev=None):
    _, rows, cols = t.shape
    tr = _row_tile(rows, cols)
    nblk = rows // tr

    def body(q_ref, w_ref, m_ref, v_ref, t_ref, r_ref, *outs):
        g_out, d_out, m_out, v_out = outs[-4:]
        g = t_ref[0].astype(F32)
        for k in range(3):
            g = g + r_ref[k].astype(F32)
        d, mn, vn = _adam_math(w_ref[...], g, m_ref[...], v_ref[...])
        g_out[...], d_out[...], m_out[...], v_out[...] = g, d, mn, vn

    blk = pl.BlockSpec((tr, cols), lambda i, q: (part * nblk + i, 0))
    specs = [blk, blk, blk, pl.BlockSpec((1, tr, cols), lambda i, q: (q[0], i, 0)),
             pl.BlockSpec((3, tr, cols), lambda i, q: (0, i, 0))]
    ins = [q_idx, w, m, v, t, r2]
    alias = {}
    if prev is not None:
        specs += [ANY] * 4
        alias = {6 + k: k for k in range(4)}
        ins += list(prev)
    return pl.pallas_call(
        body, name=name,
        grid_spec=pltpu.PrefetchScalarGridSpec(num_scalar_prefetch=1, grid=(nblk,), in_specs=specs, out_specs=[blk] * 4),
        out_shape=[SDS(w.shape, F32)] * 4, input_output_aliases=alias,
        compiler_params=_cp("parallel"))(*ins)


def _sum_devices(g, name):
    _, rows, cols = g.shape

    def body(g_ref, o_ref):
        acc = g_ref[0]
        for k in range(1, ND):
            acc = acc + g_ref[k]
        o_ref[...] = acc

    return pl.pallas_call(body, name=name, out_shape=SDS((rows, cols), F32))(g)


def _adamw_small(w, g, m, v, name):
    def body(w_ref, g_ref, m_ref, v_ref, d_out, m_out, v_out):
        d, mn, vn = _adam_math(w_ref[...], g_ref[...], m_ref[...], v_ref[...])
        d_out[...], m_out[...], v_out[...] = d, mn, vn

    return pl.pallas_call(body, name=name, out_shape=[SDS(w.shape, F32)] * 3)(w, g, m, v)


ROWS = 256


def _rms_stats(x):
    r = lax.rsqrt(jnp.mean(x * x, axis=-1, keepdims=True) + RMS_EPS)
    return x * r, r


def _rms_fwd(x, gains, name, jobs=()):
    s, d = x.shape
    n = len(gains)

    def body(x_ref, *refs):
        xh, _ = _rms_stats(x_ref[...])
        for g_ref, o_ref in zip(refs[:n], refs[n:]):
            o_ref[...] = (xh * g_ref[...]).astype(BF16)

    row = pl.BlockSpec((ROWS, d), lambda i: (i, 0))
    vec = pl.BlockSpec((1, d), lambda i: (0, 0))
    return _pc(body, name=name, grid=(s // ROWS,), in_specs=[row] + [vec] * n, out_specs=[row] * n,
               out_shape=[SDS((s, d), BF16)] * n, sem=("parallel",), args=(x, *gains), jobs=jobs)


def _rms_bwd_rows(xh, r, gain, dy):
    u = dy * gain
    return r * (u - xh * jnp.mean(u * xh, axis=-1, keepdims=True))


def _rms_bwd(x, pairs, dres, name, colsum=False):
    s, d = x.shape
    n = len(pairs)

    def body(x_ref, dres_ref, *refs):
        g_refs, dy_refs = refs[:n], refs[n:2 * n]
        dx_ref, dxb_ref = refs[2 * n], refs[2 * n + 1]
        dg_refs = refs[2 * n + 2:2 * n + 2 + n]
        cs_ref = refs[-1] if colsum else None
        first = pl.program_id(0) == 0
        xh, r = _rms_stats(x_ref[...])
        dx = dres_ref[...]
        for g_ref, dy_ref, dg_ref in zip(g_refs, dy_refs, dg_refs):
            dy = dy_ref[...]
            dx = dx + _rms_bwd_rows(xh, r, g_ref[...], dy)

            @pl.when(first)
            def _():
                dg_ref[...] = jnp.zeros_like(dg_ref)
            dg_ref[...] += jnp.sum(dy * xh, axis=0, keepdims=True)
        dx_ref[...] = dx
        dxb_ref[...] = dx.astype(BF16)
        if colsum:
            @pl.when(first)
            def _():
                cs_ref[...] = jnp.zeros_like(cs_ref)
            cs_ref[...] += jnp.sum(dx, axis=0, keepdims=True)

    row = pl.BlockSpec((ROWS, d), lambda i: (i, 0))
    vec = pl.BlockSpec((1, d), lambda i: (0, 0))
    nvec = n + (1 if colsum else 0)
    outs = pl.pallas_call(
        body, name=name, grid=(s // ROWS,),
        in_specs=[row, row] + [vec] * n + [row] * n,
        out_specs=[row, row] + [vec] * nvec,
        out_shape=[SDS((s, d), F32), SDS((s, d), BF16)] + [SDS((1, d), F32)] * nvec,
        compiler_params=_cp("arbitrary"),
    )(x, dres, *[p[0] for p in pairs], *[p[1] for p in pairs])
    return outs


def _final_loss(h, target, gain):
    s, d = h.shape

    def body(h_ref, t_ref, g_ref, dh_ref, dhb_ref, dg_ref, loss_ref):
        first = pl.program_id(0) == 0
        xh, r = _rms_stats(h_ref[...])
        gain_v = g_ref[...]
        e = xh * gain_v - t_ref[...]
        dy = e * (1.0 / d)
        dx = _rms_bwd_rows(xh, r, gain_v, dy)
        dh_ref[...] = dx
        dhb_ref[...] = dx.astype(BF16)

        @pl.when(first)
        def _():
            dg_ref[...] = jnp.zeros_like(dg_ref)
            loss_ref[...] = jnp.zeros_like(loss_ref)
        dg_ref[...] += jnp.sum(dy * xh, axis=0, keepdims=True)
        loss_ref[...] += jnp.full((1, LANES), 0.5 / d, F32) * jnp.sum(e * e)

    row = pl.BlockSpec((ROWS, d), lambda i: (i, 0))
    vec = pl.BlockSpec((1, d), lambda i: (0, 0))
    return pl.pallas_call(
        body, name="final_loss", grid=(s // ROWS,),
        in_specs=[row, row, vec], out_specs=[row, row, vec, pl.BlockSpec((1, LANES), lambda i: (0, 0))],
        out_shape=[SDS((s, d), F32), SDS((s, d), BF16), SDS((1, d), F32), SDS((1, LANES), F32)],
        compiler_params=_cp("arbitrary"))(h, target, gain)


CT = 128


def _ln_stats(cv):
    mu = jnp.mean(cv, axis=-1, keepdims=True)
    xc = cv - mu
    rstd = lax.rsqrt(jnp.mean(xc * xc, axis=-1, keepdims=True) + LN_EPS)
    return xc * rstd, rstd


def _conv_fwd(glu, dw, dwb, lng, lnb, jobs=()):
    s, d = glu.shape
    hb = CT // CONV_PAD

    def body(x_ref, halo_ref, dw_ref, dwb_ref, lng_ref, lnb_ref, c_ref, s_ref):
        keep = (pl.program_id(0) > 0).astype(F32)

        def chunk(ci, carry):
            ls = pl.ds(pl.multiple_of(ci * LANES, LANES), LANES)
            xf = jnp.concatenate([halo_ref[:, ls] * keep, x_ref[:, ls]], axis=0)
            acc = jnp.zeros((CT, LANES), F32)
            for k in range(CONV_W):
                sh = CONV_W - 1 - k
                xs = pltpu.roll(xf, sh, 0) if sh else xf
                acc = acc + dw_ref[pl.ds(k, 1), ls] * xs[CONV_PAD:]
            c_ref[:, ls] = acc + dwb_ref[:, ls]
            return carry

        lax.fori_loop(0, d // LANES, chunk, 0)
        xh, _ = _ln_stats(c_ref[...])
        yv = xh * lng_ref[...] + lnb_ref[...]
        s_ref[...] = (yv * _sigmoid(yv)).astype(BF16)

    row = pl.BlockSpec((CT, d), lambda i: (i, 0))
    halo = pl.BlockSpec((CONV_PAD, d), lambda i: (jnp.maximum(i * hb - 1, 0), 0))
    vec = pl.BlockSpec((1, d), lambda i: (0, 0))
    taps = pl.BlockSpec((CONV_PAD, d), lambda i: (0, 0))
    return _pc(
        body, name="conv_fwd", grid=(s // CT,),
        in_specs=[row, halo, taps, vec, vec, vec], out_specs=[row, row],
        out_shape=[SDS((s, d), F32), SDS((s, d), BF16)], sem=("parallel",),
        args=(glu, glu, dw, dwb, lng, lnb), jobs=jobs)


def _ln_bwd(ds, cv, lng, lnb):
    s, d = cv.shape

    def body(ds_ref, c_ref, g_ref, b_ref, dc_ref, dg_ref, db_ref):
        first = pl.program_id(0) == 0
        xh, rstd = _ln_stats(c_ref[...])
        gv = g_ref[...]
        yv = xh * gv + b_ref[...]
        sg = _sigmoid(yv)
        dln = ds_ref[...] * (sg * (1.0 + yv * (1.0 - sg)))
        dxh = dln * gv
        dc_ref[...] = rstd * (dxh - jnp.mean(dxh, axis=-1, keepdims=True)
                              - xh * jnp.mean(dxh * xh, axis=-1, keepdims=True))

        @pl.when(first)
        def _():
            dg_ref[...] = jnp.zeros_like(dg_ref)
            db_ref[...] = jnp.zeros_like(db_ref)
        dg_ref[...] += jnp.sum(dln * xh, axis=0, keepdims=True)
        db_ref[...] += jnp.sum(dln, axis=0, keepdims=True)

    row = pl.BlockSpec((ROWS, d), lambda i: (i, 0))
    vec = pl.BlockSpec((1, d), lambda i: (0, 0))
    return pl.pallas_call(
        body, name="ln_bwd", grid=(s // ROWS,), in_specs=[row, row, vec, vec], out_specs=[row, vec, vec],
        out_shape=[SDS((s, d), F32), SDS((1, d), F32), SDS((1, d), F32)],
        compiler_params=_cp("arbitrary"))(ds, cv, lng, lnb)


def _conv_bwd(dc, glu, ua, ug, dw, jobs=()):
    s, d = dc.shape
    hb = CT // CONV_PAD
    nsteps = s // CT
    full = CT + CONV_PAD

    def body(dc_ref, dcn_ref, x_ref, xp_ref, ua_ref, ug_ref, dw_ref, du_ref, ddw_ref, ddwb_ref, db1_ref, part):
        i = pl.program_id(0)
        keep_prev = (i > 0).astype(F32)
        keep_next = (i < nsteps - 1).astype(F32)

        @pl.when(i == 0)
        def _():
            part[...] = jnp.zeros_like(part)
            ddwb_ref[...] = jnp.zeros_like(ddwb_ref)
            db1_ref[...] = jnp.zeros_like(db1_ref)

        def chunk(ci, carry):
            off = pl.multiple_of(ci * LANES, LANES)
            ls = pl.ds(off, LANES)
            ls2 = pl.ds(pl.multiple_of(d + ci * LANES, LANES), LANES)
            dcc = dc_ref[:, ls]
            dcf = jnp.concatenate([dcc, dcn_ref[:, ls] * keep_next], axis=0)
            xf = jnp.concatenate([xp_ref[:, ls] * keep_prev, x_ref[:, ls]], axis=0)
            dglu = jnp.zeros((CT, LANES), F32)
            for k in range(CONV_W):
                sh = CONV_W - 1 - k
                dshift = pltpu.roll(dcf, full - sh, 0) if sh else dcf
                dglu = dglu + dw_ref[pl.ds(k, 1), ls] * dshift[:CT]
                xs = pltpu.roll(xf, sh, 0) if sh else xf
                part[pl.ds(8 * k, 8), ls] += jnp.sum((dcc * xs[CONV_PAD:]).reshape(CT // 8, 8, LANES), axis=0)
            ddwb_ref[:, ls] += jnp.sum(dcc, axis=0, keepdims=True)
            av, gv = ua_ref[:, ls], ug_ref[:, ls]
            sg = _sigmoid(gv)
            da = dglu * sg
            dgt = dglu * av * sg * (1.0 - sg)
            du_ref[:, ls] = da.astype(BF16)
            du_ref[:, ls2] = dgt.astype(BF16)
            db1_ref[:, ls] += jnp.sum(da, axis=0, keepdims=True)
            db1_ref[:, ls2] += jnp.sum(dgt, axis=0, keepdims=True)
            return carry

        lax.fori_loop(0, d // LANES, chunk, 0)

        @pl.when(i == nsteps - 1)
        def _():
            ddw_ref[...] = jnp.sum(part[...].reshape(CONV_PAD, 8, d), axis=1)

    row = pl.BlockSpec((CT, d), lambda i: (i, 0))
    prev = pl.BlockSpec((CONV_PAD, d), lambda i: (jnp.maximum(i * hb - 1, 0), 0))
    nxt = pl.BlockSpec((CONV_PAD, d), lambda i: (jnp.minimum((i + 1) * hb, s // CONV_PAD - 1), 0))
    taps = pl.BlockSpec((CONV_PAD, d), lambda i: (0, 0))
    return _pc(
        body, name="conv_bwd", grid=(nsteps,),
        in_specs=[row, nxt, row, prev, row, row, taps],
        out_specs=[pl.BlockSpec((CT, 2 * d), lambda i: (i, 0)), taps, pl.BlockSpec((1, d), lambda i: (0, 0)),
                   pl.BlockSpec((1, 2 * d), lambda i: (0, 0))],
        out_shape=[SDS((s, 2 * d), BF16), SDS((CONV_PAD, d), F32), SDS((1, d), F32), SDS((1, 2 * d), F32)],
        scratch=[pltpu.VMEM((8 * CONV_PAD, d), F32)],
        sem=("arbitrary",), args=(dc, dc, glu, glu, ua, ug, dw), jobs=jobs)


TM = 1024
TS = 1024


def _glu_mm(n1, w1g, b1, jobs=()):
    s, d = n1.shape
    cw = w1g.shape[2]
    half = ND // 2

    def body(a_ref, wa_ref, wg_ref, ba_ref, bg_ref, ua_ref, ug_ref, glu_ref):
        a = a_ref[...]
        ua = _dot(a, wa_ref[0], NN) + ba_ref[...]
        ug = _dot(a, wg_ref[0], NN) + bg_ref[...]
        ua_ref[...], ug_ref[...] = ua, ug
        glu_ref[...] = ua * _sigmoid(ug)

    out = pl.BlockSpec((TM, cw), lambda m, i: (m, i))
    return _pc(
        body, name="glu_mm", grid=(s // TM, half),
        in_specs=[pl.BlockSpec((TM, d), lambda m, i: (m, 0)),
                  pl.BlockSpec((1, d, cw), lambda m, i: (i, 0, 0)),
                  pl.BlockSpec((1, d, cw), lambda m, i: (i + half, 0, 0)),
                  pl.BlockSpec((1, cw), lambda m, i: (0, i)),
                  pl.BlockSpec((1, cw), lambda m, i: (0, i + half))],
        out_specs=[out, out, out], out_shape=[SDS((s, d), F32)] * 3,
        sem=("parallel", "arbitrary"), args=(n1, w1g, w1g, b1, b1), jobs=jobs)


def _mm_rows(a, wg, name, res=None, bias=None, out_dtype=F32, tn=512, branches=False, jobs=()):
    s, kdim = a.shape
    _, kc, n = wg.shape
    assert kc * ND == kdim
    nx = 2 + (res is not None) + (bias is not None)

    def body(*refs):
        acc = _dot(refs[0][...], refs[1][...].reshape(kdim, tn), NN)
        for extra in refs[2:nx]:
            acc = acc + extra[...]
        refs[nx][...] = acc.astype(out_dtype)
        if branches:
            scr = refs[-1]
            _stage(scr, acc)
            for o_ref, dil in zip(refs[nx + 1:], SPLIT_DILATIONS):
                _split_rows(scr, o_ref, dil)

    ins, specs = [a, wg], [pl.BlockSpec((TM, kdim), lambda m, j: (m, 0)), pl.BlockSpec((ND, kc, tn), lambda m, j: (0, 0, j))]
    if res is not None:
        ins.append(res)
        specs.append(pl.BlockSpec((TM, tn), lambda m, j: (m, j)))
    if bias is not None:
        ins.append(bias)
        specs.append(pl.BlockSpec((1, tn), lambda m, j: (0, j)))
    out_specs, out_shape, scratch = [pl.BlockSpec((TM, tn), lambda m, j: (m, j))], [SDS((s, n), out_dtype)], []
    if branches:
        out_specs += _branch_specs(TM, tn, lambda dil, m, j: (0, m, j))
        out_shape += [SDS((dil, s // dil, n), out_dtype) for dil in SPLIT_DILATIONS]
        scratch = [pltpu.VMEM((tn // LANES, TM, LANES), F32)]
    outs = _pc(body, name=name, grid=(s // TM, n // tn), in_specs=specs, out_specs=out_specs, out_shape=out_shape,
               scratch=scratch, sem=("parallel", "arbitrary"), args=ins, jobs=jobs)
    return [outs[0][None]] + outs[1:] if branches else outs[0]


def _swiglu_mm(n2, wgg, wug, name, jobs=()):
    s, d = n2.shape
    fc = wgg.shape[2]

    def body(a_ref, wg_ref, wu_ref, g_ref, u_ref, act_ref):
        a = a_ref[...]
        g = _dot(a, wg_ref[0], NN)
        u = _dot(a, wu_ref[0], NN)
        g_ref[0], u_ref[0] = g.astype(BF16), u.astype(BF16)
        act_ref[0] = (g * _sigmoid(g) * u).astype(BF16)

    wspec = pl.BlockSpec((1, d, fc), lambda m, j: (j, 0, 0))
    out = pl.BlockSpec((1, TM, fc), lambda m, j: (j, m, 0))
    return _pc(
        body, name=name, grid=(s // TM, ND),
        in_specs=[pl.BlockSpec((TM, d), lambda m, j: (m, 0)), wspec, wspec],
        out_specs=[out, out, out], out_shape=[SDS((ND, s, fc), BF16)] * 3,
        sem=("parallel", "arbitrary"), args=(n2, wgg, wug), jobs=jobs)


def _down_mm(act, wdg, res, name, jobs=()):
    _, s, fc = act.shape
    d = wdg.shape[2]

    def body(a_ref, w_ref, r_ref, o_ref):
        @pl.when(pl.program_id(1) == 0)
        def _():
            o_ref[...] = r_ref[...]
        o_ref[...] += _dot(a_ref[0], w_ref[0], NN)

    row = pl.BlockSpec((TM, d), lambda m, j: (m, 0))
    return _pc(
        body, name=name, grid=(s // TM, ND),
        in_specs=[pl.BlockSpec((1, TM, fc), lambda m, j: (j, m, 0)),
                  pl.BlockSpec((1, fc, d), lambda m, j: (j, 0, 0)), row],
        out_specs=[row], out_shape=[SDS((s, d), F32)],
        sem=("parallel", "arbitrary"), args=(act, wdg, res), jobs=jobs)[0]


def _dact_mm(dh, wdg, gate, up, name, jobs=()):
    s, d = dh.shape
    fc = wdg.shape[1]

    def body(a_ref, w_ref, g_ref, u_ref, dg_ref, du_ref):
        dact = _dot(a_ref[...], w_ref[0], NT)
        g, u = g_ref[0].astype(F32), u_ref[0].astype(F32)
        sg = _sigmoid(g)
        du_ref[0] = (dact * (g * sg)).astype(BF16)
        dg_ref[0] = (dact * u * (sg * (1.0 + g * (1.0 - sg)))).astype(BF16)

    blk = pl.BlockSpec((1, TM, fc), lambda m, j: (j, m, 0))
    return _pc(
        body, name=name, grid=(s // TM, ND),
        in_specs=[pl.BlockSpec((TM, d), lambda m, j: (m, 0)),
                  pl.BlockSpec((1, fc, d), lambda m, j: (j, 0, 0)), blk, blk],
        out_specs=[blk, blk], out_shape=[SDS((ND, s, fc), BF16)] * 2,
        sem=("parallel", "arbitrary"), args=(dh, wdg, gate, up), jobs=jobs)


def _dwd_mm(act, dh, name, jobs=()):
    _, s, fc = act.shape
    d = dh.shape[1]
    nk = s // TS

    def body(a_ref, b_ref, o_ref, acc):
        k = pl.program_id(1)

        @pl.when(k == 0)
        def _():
            acc[...] = jnp.zeros_like(acc)
        acc[...] += _dot(a_ref[0], b_ref[...], TN)

        @pl.when(k == nk - 1)
        def _():
            o_ref[0] = acc[...].astype(BF16)

    return _pc(
        body, name=name, grid=(ND, nk),
        in_specs=[pl.BlockSpec((1, TS, fc), lambda j, k: (j, k, 0)), pl.BlockSpec((TS, d), lambda j, k: (k, 0))],
        out_specs=[pl.BlockSpec((1, fc, d), lambda j, k: (_slot(j), 0, 0))],
        out_shape=[SDS((ND, fc, d), BF16)], scratch=[pltpu.VMEM((fc, d), F32)],
        sem=("parallel", "arbitrary"), args=(act, dh), jobs=jobs)[0]


def _dwgu_mm(n2, dgate, dup, name, jobs=()):
    s, d = n2.shape
    fc = dgate.shape[2]
    nk = s // TS

    def body(a_ref, g_ref, u_ref, og_ref, ou_ref, accg, accu):
        k = pl.program_id(1)

        @pl.when(k == 0)
        def _():
            accg[...] = jnp.zeros_like(accg)
            accu[...] = jnp.zeros_like(accu)
        a = a_ref[...]
        accg[...] += _dot(g_ref[0], a, TN)
        accu[...] += _dot(u_ref[0], a, TN)

        @pl.when(k == nk - 1)
        def _():
            og_ref[0] = accg[...].astype(BF16)
            ou_ref[0] = accu[...].astype(BF16)

    blk = pl.BlockSpec((1, TS, fc), lambda j, k: (j, k, 0))
    out = pl.BlockSpec((1, fc, d), lambda j, k: (_slot(j), 0, 0))
    return _pc(
        body, name=name, grid=(ND, nk),
        in_specs=[pl.BlockSpec((TS, d), lambda j, k: (k, 0)), blk, blk], out_specs=[out, out],
        out_shape=[SDS((ND, fc, d), BF16)] * 2,
        scratch=[pltpu.VMEM((fc, d), F32), pltpu.VMEM((fc, d), F32)],
        sem=("parallel", "arbitrary"), args=(n2, dgate, dup), jobs=jobs)


def _dn_ffn_mm(dgate, dup, wgg, wug, name, jobs=()):
    _, s, fc = dgate.shape
    d = wgg.shape[1]

    def body(g_ref, u_ref, wg_ref, wu_ref, o_ref):
        j = pl.program_id(1)

        @pl.when(j == 0)
        def _():
            o_ref[...] = jnp.zeros_like(o_ref)
        o_ref[...] += _dot(g_ref[0], wg_ref[0], NT) + _dot(u_ref[0], wu_ref[0], NT)

    blk = pl.BlockSpec((1, TM, fc), lambda m, j: (j, m, 0))
    wspec = pl.BlockSpec((1, d, fc), lambda m, j: (j, 0, 0))
    return _pc(
        body, name=name, grid=(s // TM, ND), in_specs=[blk, blk, wspec, wspec],
        out_specs=[pl.BlockSpec((TM, d), lambda m, j: (m, 0))], out_shape=[SDS((s, d), F32)],
        sem=("parallel", "arbitrary"), args=(dgate, dup, wgg, wug), jobs=jobs)[0]


def _mm_rows_t(pairs, name, out_dtype, branches=False, jobs=()):
    s, n = pairs[0][0].shape
    _, kc, _ = pairs[0][1].shape
    np_ = len(pairs)
    grp = ND // 2
    wide = grp * kc

    def body(*refs):
        o_ref = refs[2 * np_]
        for i in range(grp):
            acc = None
            for p in range(np_):
                t = _dot(refs[p][...], refs[np_ + p][i], NT)
                acc = t if acc is None else acc + t
            o_ref[:, kc * i:kc * (i + 1)] = acc.astype(out_dtype)
            if branches:
                for c, ls in enumerate(_lane_chunks(kc)):
                    refs[-1][i * (kc // LANES) + c] = acc[:, ls]
        if branches:
            for b_ref, dil in zip(refs[2 * np_ + 1:], SPLIT_DILATIONS):
                _split_rows(refs[-1], b_ref, dil)

    out_specs, out_shape, scratch = [pl.BlockSpec((TM, wide), lambda m, j: (m, j))], [SDS((s, kc * ND), out_dtype)], []
    if branches:
        out_specs += _branch_specs(TM, wide, lambda dil, m, j: (0, m, j))
        out_shape += [SDS((dil, s // dil, kc * ND), out_dtype) for dil in SPLIT_DILATIONS]
        scratch = [pltpu.VMEM((wide // LANES, TM, LANES), F32)]
    outs = _pc(
        body, name=name, grid=(s // TM, ND // grp),
        in_specs=[pl.BlockSpec((TM, n), lambda m, j: (m, 0))] * np_ + [pl.BlockSpec((grp, kc, n), lambda m, j: (j, 0, 0))] * np_,
        out_specs=out_specs, out_shape=out_shape, scratch=scratch,
        sem=("parallel", "arbitrary"), args=[p[0] for p in pairs] + [p[1] for p in pairs], jobs=jobs)
    return [outs[0][None]] + outs[1:] if branches else outs[0]


def _dw_rows_mm(a, b, name):
    s, kdim = a.shape
    n = b.shape[1]
    kc = kdim // ND
    ts = TS // 2
    nk = s // ts

    def body(a_ref, b_ref, o_ref, acc):
        k = pl.program_id(0)

        @pl.when(k == 0)
        def _():
            acc[...] = jnp.zeros_like(acc)
        acc[...] += _dot(a_ref[...], b_ref[...], TN)

        @pl.when(k == nk - 1)
        def _():
            for dev in range(ND):
                o_ref[_slot(dev)] = acc[kc * dev:kc * (dev + 1), :].astype(BF16)

    return pl.pallas_call(
        body, name=name, grid=(nk,),
        in_specs=[pl.BlockSpec((ts, kdim), lambda k: (k, 0)), pl.BlockSpec((ts, n), lambda k: (k, 0))],
        out_specs=pl.BlockSpec((ND, kc, n), lambda k: (0, 0, 0)), out_shape=SDS((ND, kc, n), BF16),
        scratch_shapes=[pltpu.VMEM((kdim, n), F32)], compiler_params=_cp("arbitrary"))(a, b)


def _dw1_mm(n1, du, jobs=()):
    s, d = n1.shape
    cw = du.shape[1] // ND
    nk = s // TS

    def body(a_ref, b_ref, o_ref, acc):
        k = pl.program_id(1)

        @pl.when(k == 0)
        def _():
            acc[...] = jnp.zeros_like(acc)
        acc[...] += _dot(a_ref[...], b_ref[...], TN)

        @pl.when(k == nk - 1)
        def _():
            o_ref[0] = acc[...].astype(BF16)

    return _pc(
        body, name="dw1_mm", grid=(ND, nk),
        in_specs=[pl.BlockSpec((TS, d), lambda j, k: (k, 0)), pl.BlockSpec((TS, cw), lambda j, k: (k, j))],
        out_specs=[pl.BlockSpec((1, d, cw), lambda j, k: (_slot(j), 0, 0))], out_shape=[SDS((ND, d, cw), BF16)],
        scratch=[pltpu.VMEM((d, cw), F32)], sem=("parallel", "arbitrary"), args=(n1, du), jobs=jobs)[0]


def _dn1_mm(du, w1g, part, nparts, prev=None, jobs=()):
    s = du.shape[0]
    _, d, cw = w1g.shape
    steps = s // TM // nparts
    m0 = part * steps

    def body(a_ref, w_ref, *refs):
        o_ref = refs[-1]
        j = pl.program_id(1)

        @pl.when(j == 0)
        def _():
            o_ref[...] = jnp.zeros_like(o_ref)
        o_ref[...] += _dot(a_ref[...], w_ref[0], NT)

    ins = [du, w1g] if prev is None else [du, w1g, prev]
    specs = [pl.BlockSpec((TM, cw), lambda m, j: (m0 + m, j)), pl.BlockSpec((1, d, cw), lambda m, j: (j, 0, 0))]
    return _pc(
        body, name=f"dn1_mm{part}", grid=(steps, ND), in_specs=specs if prev is None else specs + [ANY],
        out_specs=[pl.BlockSpec((TM, d), lambda m, j: (m0 + m, 0))], out_shape=[SDS((s, d), F32)],
        alias=None if prev is None else {2: 0}, sem=("parallel", "arbitrary"), args=ins, jobs=jobs)[0]


FAR = 1e33


def _slopes(heads):
    return [2.0 ** (-8.0 * (h + 1) / heads) for h in range(heads)]


def _band(has_prev):
    qi = lax.broadcasted_iota(jnp.int32, (BLK, 2 * BLK), 0)
    ki = lax.broadcasted_iota(jnp.int32, (BLK, 2 * BLK), 1)
    j = qi - ki + BLK
    ok = (j >= 0) & (j <= BLK) & (has_prev | (ki >= BLK))
    return jnp.where(ok, j.astype(F32), FAR)


SPLIT_DILATIONS = tuple(dil for dil in BRANCH_DILATIONS if dil > 1)


def _lane_chunks(w):
    return [slice(LANES * c, LANES * (c + 1)) for c in range(w // LANES)]


def _stage(scr, tile):
    for c, ls in enumerate(_lane_chunks(tile.shape[1])):
        scr[c] = tile[:, ls]


def _split_rows(scr, o_ref, dil):
    _, n, w = o_ref.shape
    for r in range(dil):
        for c, ls in enumerate(_lane_chunks(w)):
            o_ref[r, :, ls] = scr[c, pl.ds(r, n, stride=dil), :].astype(o_ref.dtype)


def _join_rows(i_ref, scr, dil):
    _, n, w = i_ref.shape
    for r in range(dil):
        for c, ls in enumerate(_lane_chunks(w)):
            scr[c, pl.ds(r, n, stride=dil), :] = i_ref[r, :, ls].astype(F32)


def _unstage(scr, w):
    return jnp.concatenate([scr[c] for c in range(w // LANES)], axis=1)


def _branch_specs(rows, w, index):
    return [pl.BlockSpec((dil, rows // dil, w), functools.partial(index, dil)) for dil in SPLIT_DILATIONS]


def _attn_fwd(q, k, v, dil, jobs=()):
    _, l, d = q.shape
    heads = d // HEAD
    assert heads <= HEAD
    scale = HEAD ** -0.5
    slopes = _slopes(heads)

    def body(q_ref, kc_ref, kp_ref, vc_ref, vp_ref, o_ref, lse_ref):
        dist = _band(pl.program_id(1) > 0)
        lane = lax.broadcasted_iota(jnp.int32, (BLK, HEAD), 1)
        lse = jnp.zeros((BLK, HEAD), F32)
        for h in range(heads):
            sl = slice(HEAD * h, HEAD * (h + 1))
            kh = jnp.concatenate([kp_ref[0, :, sl], kc_ref[0, :, sl]], axis=0)
            vh = jnp.concatenate([vp_ref[0, :, sl], vc_ref[0, :, sl]], axis=0)
            logits = _dot(q_ref[0, :, sl], kh, NT) * scale + dist * (-slopes[h] * dil)
            m = jnp.max(logits, axis=-1, keepdims=True)
            p = jnp.exp(logits - m)
            den = jnp.sum(p, axis=-1, keepdims=True)
            o_ref[0, :, sl] = _dot(p.astype(BF16), vh, NN) / den
            lse = jnp.where(lane == h, m + jnp.log(den), lse)
        lse_ref[0] = lse

    cur = pl.BlockSpec((1, BLK, d), lambda r, b: (r, b, 0))
    prev = pl.BlockSpec((1, BLK, d), lambda r, b: (r, jnp.maximum(b - 1, 0), 0))
    return _pc(
        body, name=f"attn_fwd_d{dil}", grid=(dil, l // BLK),
        in_specs=[cur, cur, prev, cur, prev], out_specs=[cur, pl.BlockSpec((1, BLK, HEAD), lambda r, b: (r, b, 0))],
        out_shape=[SDS((dil, l, d), F32), SDS((dil, l, HEAD), F32)], sem=("parallel", "arbitrary"),
        args=(q, k, k, v, v), jobs=jobs)


def _attn_merge(outs, lses):
    _, s, d = outs[0].shape
    heads = d // HEAD
    nb = len(outs)
    nsplit = nb - 1

    def body(*refs):
        o_refs, l_refs = refs[:nb], refs[nb:2 * nb]
        att_refs, lse_refs = refs[2 * nb:3 * nb], refs[3 * nb:4 * nb]
        scr_o, scr_l, scr_att = refs[4 * nb:4 * nb + nsplit], refs[4 * nb + nsplit:4 * nb + 2 * nsplit], refs[-1]
        ls = [l_refs[0][...]]
        for k, dil in enumerate(SPLIT_DILATIONS):
            _join_rows(o_refs[1 + k], scr_o[k], dil)
            _join_rows(l_refs[1 + k], scr_l[k], dil)
            ls.append(scr_l[k][0])
        m = functools.reduce(jnp.maximum, ls)
        ws = [jnp.exp(v - m) for v in ls]
        den = functools.reduce(jnp.add, ws)
        ws = [w / den for w in ws]
        lse_refs[0][...] = m + jnp.log(den)
        scr_l[0][0] = m + jnp.log(den)
        for h in range(heads):
            sl = slice(HEAD * h, HEAD * (h + 1))
            slab = ws[0][:, h:h + 1] * o_refs[0][:, sl]
            for k in range(nsplit):
                slab = slab + ws[1 + k][:, h:h + 1] * scr_o[k][h]
            att_refs[0][:, sl] = slab.astype(BF16)
            scr_att[h] = slab
        for k, dil in enumerate(SPLIT_DILATIONS):
            _split_rows(scr_att, att_refs[1 + k], dil)
            _split_rows(scr_l[0], lse_refs[1 + k], dil)

    def specs(w):
        return [pl.BlockSpec((ROWS, w), lambda i: (i, 0))] + _branch_specs(ROWS, w, lambda dil, i: (0, i, 0))

    def shapes(w, dt):
        return [SDS((s, w), dt)] + [SDS((dil, s // dil, w), dt) for dil in SPLIT_DILATIONS]

    wide, narrow = pltpu.VMEM((heads, ROWS, LANES), F32), pltpu.VMEM((1, ROWS, LANES), F32)
    res = pl.pallas_call(
        body, name="attn_merge", grid=(s // ROWS,), in_specs=specs(d) + specs(HEAD), out_specs=specs(d) + specs(HEAD),
        out_shape=shapes(d, BF16) + shapes(HEAD, F32),
        scratch_shapes=[wide] * nsplit + [narrow] * nsplit + [wide],
        compiler_params=_cp("parallel"))(outs[0].reshape(s, d), *outs[1:], lses[0].reshape(s, HEAD), *lses[1:])
    return list(res[:nb]), list(res[nb:])


def _attn_bwd(q, k, v, do, o, lse, dil, jobs=()):
    _, l, d = q.shape
    nb = l // BLK
    heads = d // HEAD
    scale = HEAD ** -0.5
    slopes = _slopes(heads)
    whole = 2 * l * d <= RESIDENT_BYTES
    steps = nb if whole else nb + 1

    def body(q_ref, kc_ref, kp_ref, vc_ref, vp_ref, do_ref, o_ref, lse_ref, dq_ref, dk_ref, dv_ref, ck, cv):
        b = pl.program_id(1)
        rows = pl.ds(pl.multiple_of(jnp.maximum(b - 1, 0) * BLK, BLK), BLK) if whole else slice(None)

        @pl.when(b == 0)
        def _():
            ck[...] = jnp.zeros_like(ck)
            cv[...] = jnp.zeros_like(cv)

        @pl.when(b < nb)
        def _():
            dist = _band(b > 0)
            for h in range(heads):
                sl = slice(HEAD * h, HEAD * (h + 1))
                qh, doh = q_ref[0, :, sl], do_ref[0, :, sl]
                kh = jnp.concatenate([kp_ref[0, :, sl], kc_ref[0, :, sl]], axis=0)
                vh = jnp.concatenate([vp_ref[0, :, sl], vc_ref[0, :, sl]], axis=0)
                lse_h = lse_ref[0, :, h:h + 1]
                delta = jnp.sum(doh.astype(F32) * o_ref[0, :, sl].astype(F32), axis=-1, keepdims=True)
                p = jnp.exp(_dot(qh, kh, NT) * scale + dist * (-slopes[h] * dil) - lse_h)
                ds = (p * (_dot(doh, vh, NT) - delta)).astype(BF16)
                dq_ref[0, :, sl] = (_dot(ds, kh, NN) * scale).astype(BF16)
                dk2 = _dot(ds, qh, TN) * scale
                dv2 = _dot(p.astype(BF16), doh, TN)
                dk_ref[0, rows, sl] = (ck[:, sl] + dk2[:BLK]).astype(BF16)
                dv_ref[0, rows, sl] = (cv[:, sl] + dv2[:BLK]).astype(BF16)
                ck[:, sl] = dk2[BLK:]
                cv[:, sl] = dv2[BLK:]

        @pl.when(b == steps - 1)
        def _():
            last = pl.ds((nb - 1) * BLK, BLK) if whole else slice(None)
            dk_ref[0, last, :] = ck[...].astype(BF16)
            dv_ref[0, last, :] = cv[...].astype(BF16)

    cur = pl.BlockSpec((1, BLK, d), lambda r, b: (r, jnp.minimum(b, nb - 1), 0))
    prev = pl.BlockSpec((1, BLK, d), lambda r, b: (r, jnp.clip(b - 1, 0, nb - 1), 0))
    lse_spec = pl.BlockSpec((1, BLK, HEAD), lambda r, b: (r, jnp.minimum(b, nb - 1), 0))
    dkv = pl.BlockSpec((1, l, d), lambda r, b: (r, 0, 0)) if whole else prev
    return _pc(
        body, name=f"attn_bwd_d{dil}", grid=(dil, steps),
        in_specs=[cur, cur, prev, cur, prev, cur, cur, lse_spec], out_specs=[cur, dkv, dkv],
        out_shape=[SDS((dil, l, d), BF16)] * 3,
        scratch=[pltpu.VMEM((BLK, d), F32), pltpu.VMEM((BLK, d), F32)],
        sem=("parallel", "arbitrary"), args=(q, k, k, v, v, do, o, lse), jobs=jobs)


def _sum_cast(xs, name):
    _, s, d = xs[0].shape
    nsplit = len(xs) - 1

    def body(*refs):
        i_refs, o_ref, scr = refs[:nsplit + 1], refs[nsplit + 1], refs[nsplit + 2:]
        acc = i_refs[0][...].astype(F32)
        for k, dil in enumerate(SPLIT_DILATIONS):
            _join_rows(i_refs[1 + k], scr[k], dil)
            acc = acc + _unstage(scr[k], d)
        o_ref[...] = acc.astype(BF16)

    row = pl.BlockSpec((ROWS, d), lambda i: (i, 0))
    return pl.pallas_call(
        body, name=name, grid=(s // ROWS,), in_specs=[row] + _branch_specs(ROWS, d, lambda dil, i: (0, i, 0)),
        out_specs=row, out_shape=SDS((s, d), BF16),
        scratch_shapes=[pltpu.VMEM((d // LANES, ROWS, LANES), F32)] * nsplit,
        compiler_params=_cp("parallel"))(xs[0].reshape(s, d), *xs[1:])


def _pack_rows(vs, width):
    flat = jnp.concatenate([v.reshape(-1) for v in vs])
    spans, at = [], 0
    for v in vs:
        spans.append((at, v.size))
        at += v.size
    rows = -(-at // width)
    rows = -(-rows // 8) * 8
    flat = jnp.pad(flat, (0, rows * width - at))
    return flat.reshape(rows, width), spans


def kernel(x, a_norm_g, conv_w1, conv_b1, conv_dw, conv_dw_b, conv_ln_g, conv_ln_b, conv_w2, conv_b2, kv_norm_g, w_k, w_v, b_norm_g, w_q, w_o, ffn_norm_g, ffn_w_gate, ffn_w_up, ffn_w_down, final_norm_g, loss_target, m_a_norm_g, m_conv_w1, m_conv_b1, m_conv_dw, m_conv_dw_b, m_conv_ln_g, m_conv_ln_b, m_conv_w2, m_conv_b2, m_kv_norm_g, m_w_k, m_w_v, m_b_norm_g, m_w_q, m_w_o, m_ffn_norm_g, m_ffn_w_gate, m_ffn_w_up, m_ffn_w_down, m_final_norm_g, v_a_norm_g, v_conv_w1, v_conv_b1, v_conv_dw, v_conv_dw_b, v_conv_ln_g, v_conv_ln_b, v_conv_w2, v_conv_b2, v_kv_norm_g, v_w_k, v_w_v, v_b_norm_g, v_w_q, v_w_o, v_ffn_norm_g, v_ffn_w_gate, v_ffn_w_up, v_ffn_w_down, v_final_norm_g):
    s, d = x.shape[1], x.shape[2]
    dc = d // ND
    h0 = x[0]
    target = loss_target[0]
    xi, yi, ci = lax.axis_index("x"), lax.axis_index("y"), lax.axis_index("c")
    me = 4 * xi + 2 * yi + ci
    c_idx = jnp.reshape(ci, (1,)).astype(jnp.int32)
    q_idx = jnp.reshape(2 * xi + yi, (1,)).astype(jnp.int32)

    bf = lambda w: w.astype(BF16)
    small_shards = [a_norm_g, conv_b1, conv_dw, conv_dw_b, conv_ln_g, conv_ln_b, conv_b2]
    sp, sp_spans = _pack_rows(small_shards, dc)
    w1g, spg = _all_gather([bf(conv_w1[0]), sp], "gather_first")
    spg = spg.reshape(ND, -1)

    def small_full(i, rows):
        at, size = sp_spans[i]
        return spg[:, at:at + size].reshape(ND, rows, size // rows).transpose(1, 0, 2).reshape(rows, -1)

    a_g = small_full(0, 1)
    b1 = small_full(1, 1)
    dw = jnp.pad(small_full(2, CONV_W), ((0, CONV_PAD - CONV_W), (0, 0)))
    dwb, lng, lnb, b2 = small_full(3, 1), small_full(4, 1), small_full(5, 1), small_full(6, 1)
    kv_g, q_g, fin_g = kv_norm_g.reshape(1, d), b_norm_g.reshape(1, d), final_norm_g.reshape(1, d)
    f_g = [ffn_norm_g[0:1], ffn_norm_g[1:2]]

    def send(*shards):
        return _job_gather_send([bf(t) for t in shards])

    def forward(job):
        return _job_gather_forward(job.result)

    def send_half(w, part, first=None):
        return _job_gather_send_rows(bf(w), part, 2, None if first is None else first.result[0])

    s_w2 = send(conv_w2[0])
    (n1,) = _rms_fwd(h0, [a_g], "rms_a", jobs=[s_w2])
    f_w2, s_g0 = forward(s_w2), send(ffn_w_gate[0])
    ua, ug, glu = _glu_mm(n1, w1g, b1, jobs=[f_w2, s_g0])
    (w2g,) = f_w2.result
    f_g0, s_u0 = forward(s_g0), send(ffn_w_up[0])
    cv, sw = _conv_fwd(glu, dw, dwb, lng, lnb, jobs=[f_g0, s_u0])
    (wg0,) = f_g0.result
    f_u0 = forward(s_u0)
    h1 = _mm_rows(sw, w2g, "w2_mm", res=h0, bias=b2, jobs=[f_u0])
    (wu0,) = f_u0.result
    (n2a,) = _rms_fwd(h1, [f_g[0]], "rms_f0")
    s_d0, s_kv = send(ffn_w_down[0]), send(w_k, w_v)
    gate0, up0, act0 = _swiglu_mm(n2a, wg0, wu0, "swiglu_mm0", jobs=[s_d0, s_kv])
    f_d0 = forward(s_d0)
    _comm_call([f_d0], "forward_mid")
    (wd0,) = f_d0.result
    f_kv, s_qo = forward(s_kv), send(w_q[0], w_o[0])
    h2 = _down_mm(act0, wd0, h1, "down_mm0", jobs=[f_kv, s_qo])
    wkg, wvg = f_kv.result
    kvn, qn = _rms_fwd(h2, [kv_g, q_g], "rms_kvq")
    f_qo, s_g1a = forward(s_qo), send_half(ffn_w_gate[1], 0)
    kk = _mm_rows(kvn, wkg, "k_mm", out_dtype=BF16, branches=True, jobs=[f_qo, s_g1a])
    wqg, wog = f_qo.result
    s_g1b = send_half(ffn_w_gate[1], 1, s_g1a)
    vv = _mm_rows(kvn, wvg, "v_mm", out_dtype=BF16, branches=True, jobs=[s_g1b])
    f_g1, s_u1a = forward(s_g1b), send_half(ffn_w_up[1], 0)
    qq = _mm_rows(qn, wqg, "q_mm", out_dtype=BF16, branches=True, jobs=[f_g1, s_u1a])
    (wg1,) = f_g1.result
    branch = {dil: (qq[i], kk[i], vv[i]) for i, dil in enumerate(BRANCH_DILATIONS)}
    s_u1b = send_half(ffn_w_up[1], 1, s_u1a)
    o1, l1 = _attn_fwd(*branch[1], 1, jobs=[s_u1b])
    f_u1, s_d1a = forward(s_u1b), send_half(ffn_w_down[1], 0)
    o4, l4 = _attn_fwd(*branch[4], 4, jobs=[f_u1, s_d1a])
    (wu1,) = f_u1.result
    s_d1b = send_half(ffn_w_down[1], 1, s_d1a)
    o16, l16 = _attn_fwd(*branch[16], 16, jobs=[s_d1b])
    atts, lses = _attn_merge([o1, o4, o16], [l1, l4, l16])
    att = atts[0]
    atts, lses = [att[None]] + atts[1:], [lses[0][None]] + lses[1:]
    f_d1 = forward(s_d1b)
    h3 = _mm_rows(att, wog, "wo_mm", res=h2, jobs=[f_d1])
    (wd1,) = f_d1.result
    (n2b,) = _rms_fwd(h3, [f_g[1]], "rms_f1")
    gate1, up1, act1 = _swiglu_mm(n2b, wg1, wu1, "swiglu_mm1")
    h4 = _down_mm(act1, wd1, h3, "down_mm1")

    flat = lambda g: g.reshape(ND, -1, g.shape[-1])
    chip_sums, cross = {}, {}

    def to_sibling(**grads):
        job = _job_scatter_sibling([flat(g) for g in grads.values()])
        job.names = list(grads)
        return job

    def add_up(job):
        for n, g, r in zip(job.names, job.ins, job.result):
            chip_sums[n] = _rs_add(g, r, c_idx, f"rs_add_{n}")

    def to_chips(*names):
        job = _job_scatter_cross([chip_sums[n] for n in names])
        job.names = names
        return job

    def landed(job):
        cross.update(zip(job.names, job.result))

    dh4, dh4b, d_fin, loss_row = _final_loss(h4, target, fin_g)
    dgate1, dup1 = _dact_mm(dh4b, wd1, gate1, up1, "dact_mm1")
    g_wd1 = _dwd_mm(act1, dh4b, "dwd_mm1")
    j1 = to_sibling(wd1=g_wd1)
    g_wg1, g_wu1 = _dwgu_mm(n2b, dgate1, dup1, "dwgu_mm1", jobs=[j1])
    add_up(j1)
    j2, j3 = to_chips("wd1"), to_sibling(wg1=g_wg1, wu1=g_wu1)
    dn2b = _dn_ffn_mm(dgate1, dup1, wg1, wu1, "dn_ffn_mm1", jobs=[j2, j3])
    landed(j2)
    add_up(j3)
    dh3, dh3b, d_f1 = _rms_bwd(h3, [(f_g[1], dn2b)], dh4, "rms_f1_bwd")
    g_wo = _dw_rows_mm(att, dh3b, "dwo_mm")
    j4 = to_sibling(wo=g_wo)
    datt = _mm_rows_t([(dh3b, wog)], "datt_mm", BF16, branches=True, jobs=[j4])
    add_up(j4)
    riders = {1: to_chips("wg1"), 4: to_chips("wu1"), 16: to_chips("wo")}
    dqs, dks, dvs = [], [], []
    for i, dil in enumerate(BRANCH_DILATIONS):
        qb, kb, vb = branch[dil]
        dq_b, dk_b, dv_b = _attn_bwd(qb, kb, vb, datt[i], atts[i], lses[i], dil, jobs=[riders[dil]])
        landed(riders[dil])
        dqs.append(dq_b)
        dks.append(dk_b)
        dvs.append(dv_b)
    dq, dk, dv = _sum_cast(dqs, "dq_sum"), _sum_cast(dks, "dk_sum"), _sum_cast(dvs, "dv_sum")
    g_wq = _dw_rows_mm(qn, dq, "dwq_mm")
    g_wk = _dw_rows_mm(kvn, dk, "dwk_mm")
    g_wv = _dw_rows_mm(kvn, dv, "dwv_mm")
    j5 = to_sibling(wq=g_wq, wk=g_wk, wv=g_wv)
    dqn = _mm_rows_t([(dq, wqg)], "dqn_mm", F32, jobs=[j5])
    add_up(j5)
    j6 = to_chips("wq", "wk")
    dkvn = _mm_rows_t([(dk, wkg), (dv, wvg)], "dkvn_mm", F32, jobs=[j6])
    landed(j6)
    dh2, dh2b, d_q, d_kv = _rms_bwd(h2, [(q_g, dqn), (kv_g, dkvn)], dh3, "rms_kvq_bwd")
    j7 = to_chips("wv")
    dgate0, dup0 = _dact_mm(dh2b, wd0, gate0, up0, "dact_mm0", jobs=[j7])
    landed(j7)
    g_wd0 = _dwd_mm(act0, dh2b, "dwd_mm0")
    j8 = to_sibling(wd0=g_wd0)
    g_wg0, g_wu0 = _dwgu_mm(n2a, dgate0, dup0, "dwgu_mm0", jobs=[j8])
    add_up(j8)
    j9, j10 = to_chips("wd0"), to_sibling(wg0=g_wg0, wu0=g_wu0)
    dn2a = _dn_ffn_mm(dgate0, dup0, wg0, wu0, "dn_ffn_mm0", jobs=[j9, j10])
    landed(j9)
    add_up(j10)
    dh1, dh1b, d_f0, d_b2 = _rms_bwd(h1, [(f_g[0], dn2a)], dh2, "rms_f0_bwd", colsum=True)
    g_w2 = _dw_rows_mm(sw, dh1b, "dw2_mm")
    j11 = to_sibling(w2=g_w2)
    dsw = _mm_rows_t([(dh1b, w2g)], "dsw_mm", F32, jobs=[j11])
    add_up(j11)
    dcv, d_lng, d_lnb = _ln_bwd(dsw, cv, lng, lnb)
    j12 = to_chips("wg0", "wu0")
    du, d_dw, d_dwb, d_b1 = _conv_bwd(dcv, glu, ua, ug, dw, jobs=[j12])
    landed(j12)
    j13 = to_chips("w2")
    g_w1 = _dw1_mm(n1, du, jobs=[j13])
    landed(j13)
    j14 = to_sibling(w1=g_w1)
    _comm_call([j14], "rs_w1_sibling")
    add_up(j14)
    j15 = to_chips("w1")
    dn1 = _dn1_mm(du, w1g, 0, 1, jobs=[j15])
    landed(j15)
    dx, _, d_a = _rms_bwd(h0, [(a_g, dn1)], dh1, "rms_a_bwd")

    small_g = [d_a, d_b1, d_dw[:CONV_W], d_dwb, d_lng, d_lnb, d_b2, d_kv, d_q, d_f0, d_f1, d_fin, loss_row]
    gp, gp_spans = _pack_rows(small_g, d)
    (gpg,) = _all_gather([gp], "gather_small_grads")

    two = lambda t: t.reshape(-1, t.shape[-1])

    def adam(w, m, v, names, tag, swapped=False):
        view = (lambda t: jnp.swapaxes(t, 1, 2)) if swapped else (lambda t: t)
        w, m, v = view(w), view(m), view(v)
        res = None
        for part, n in enumerate(names):
            res = _adamw_big(two(w), two(m), two(v), chip_sums[n], cross[n], q_idx, f"adamw_{tag}{part}", part, res)
        return [view(t.reshape(w.shape)) for t in res]

    big_out = [
        adam(conv_w1, m_conv_w1, v_conv_w1, ["w1"], "w1"), adam(conv_w2, m_conv_w2, v_conv_w2, ["w2"], "w2"),
        adam(w_k, m_w_k, v_w_k, ["wk"], "wk"), adam(w_v, m_w_v, v_w_v, ["wv"], "wv"),
        adam(w_q, m_w_q, v_w_q, ["wq"], "wq"), adam(w_o, m_w_o, v_w_o, ["wo"], "wo"),
        adam(ffn_w_gate, m_ffn_w_gate, v_ffn_w_gate, ["wg0", "wg1"], "wg", swapped=True),
        adam(ffn_w_up, m_ffn_w_up, v_ffn_w_up, ["wu0", "wu1"], "wu", swapped=True),
        adam(ffn_w_down, m_ffn_w_down, v_ffn_w_down, ["wd0", "wd1"], "wd")]

    gsum = _sum_devices(gpg, "sum_small_grads").reshape(-1)

    def gfull(i):
        at, size = gp_spans[i]
        return gsum[at:at + size]

    def shard_of(vec, rows):
        return lax.dynamic_slice_in_dim(vec.reshape(rows, -1), me * (vec.size // rows // ND), vec.size // rows // ND, axis=1)

    loss = gfull(12)[0]
    small_grads = [
        shard_of(gfull(0), 1), shard_of(gfull(1), 1), shard_of(gfull(2), CONV_W)[None], shard_of(gfull(3), 1),
        shard_of(gfull(4), 1), shard_of(gfull(5), 1), shard_of(gfull(6), 1),
        gfull(7), gfull(8)[None], jnp.stack([gfull(9), gfull(10)]), gfull(11)]
    small_w = [a_norm_g, conv_b1, conv_dw, conv_dw_b, conv_ln_g, conv_ln_b, conv_b2, kv_norm_g, b_norm_g, ffn_norm_g, final_norm_g]
    small_m = [m_a_norm_g, m_conv_b1, m_conv_dw, m_conv_dw_b, m_conv_ln_g, m_conv_ln_b, m_conv_b2, m_kv_norm_g, m_b_norm_g, m_ffn_norm_g, m_final_norm_g]
    small_v = [v_a_norm_g, v_conv_b1, v_conv_dw, v_conv_dw_b, v_conv_ln_g, v_conv_ln_b, v_conv_b2, v_kv_norm_g, v_b_norm_g, v_ffn_norm_g, v_final_norm_g]
    small_grads = [g.reshape(w.shape) for g, w in zip(small_grads, small_w)]
    wp, spans = _pack_rows(small_w, LANES)
    gpk, _ = _pack_rows(small_grads, LANES)
    mp, _ = _pack_rows(small_m, LANES)
    vp, _ = _pack_rows(small_v, LANES)
    dp, mnp, vnp = _adamw_small(wp, gpk, mp, vp, "adamw_small")

    def unpack(packed):
        flat = packed.reshape(-1)
        return [flat[at:at + size].reshape(w.shape) for (at, size), w in zip(spans, small_w)]

    small_out = list(zip(small_grads, unpack(dp), unpack(mnp), unpack(vnp)))

    order = ["a_norm_g", "conv_w1", "conv_b1", "conv_dw", "conv_dw_b", "conv_ln_g", "conv_ln_b", "conv_w2", "conv_b2",
             "kv_norm_g", "w_k", "w_v", "b_norm_g", "w_q", "w_o", "ffn_norm_g", "ffn_w_gate", "ffn_w_up", "ffn_w_down",
             "final_norm_g"]
    big_names = ["conv_w1", "conv_w2", "w_k", "w_v", "w_q", "w_o", "ffn_w_gate", "ffn_w_up", "ffn_w_down"]
    small_names = ["a_norm_g", "conv_b1", "conv_dw", "conv_dw_b", "conv_ln_g", "conv_ln_b", "conv_b2", "kv_norm_g",
                   "b_norm_g", "ffn_norm_g", "final_norm_g"]
    table = {n: big_out[i] for i, n in enumerate(big_names)}
    table.update({n: small_out[i] for i, n in enumerate(small_names)})
    result = [loss, dx[None]]
    for kind in range(4):
        result += [table[n][kind] for n in order]
    return tuple(result)
```

```python
import functools

import jax
import jax.numpy as jnp
from jax import lax
from jax.experimental import pallas as pl
from jax.experimental.pallas import tpu as pltpu

ND = 8
LANES = 128
HEAD = 128
BLK = 128
BRANCH_DILATIONS = (1, 4, 16)
CONV_W = 31
CONV_PAD = 32
RMS_EPS = 1e-6
LN_EPS = 1e-5
LR, B1, B2, ADAM_EPS, WD, STEP = 0.001, 0.9, 0.999, 1e-08, 0.01, 10
VMEM_LIMIT = 56 * 1024 * 1024
RESIDENT_BYTES = 4 * 1024 * 1024

F32, BF16 = jnp.float32, jnp.bfloat16
SDS = jax.ShapeDtypeStruct
MESH = pl.DeviceIdType.MESH
ANY = pl.BlockSpec(memory_space=pl.ANY)

NN = (((1,), (0,)), ((), ()))
NT = (((1,), (1,)), ((), ()))
TN = (((0,), (0,)), ((), ()))


def _dot(a, b, dims):
    return lax.dot_general(a, b, dims, preferred_element_type=F32)


def _cp(*sem):
    return pltpu.CompilerParams(dimension_semantics=sem, vmem_limit_bytes=VMEM_LIMIT)


def _slot(dev):
    return 4 * (dev % 2) + dev // 2


def _sigmoid(v):
    return 1.0 / (1.0 + jnp.exp(-v))


class _Job:
    def __init__(self, ins, out_shapes, alias, nsem, nlocal, make):
        self.ins, self.out_shapes, self.alias = list(ins), list(out_shapes), dict(alias)
        self.nsem, self.nlocal, self.make = nsem, nlocal, make
        self.result = None


def _coords():
    return lax.axis_index("x"), lax.axis_index("y"), lax.axis_index("c")


def _remote(src, dst, send, recv, k, to):
    return pltpu.make_async_remote_copy(src_ref=src, dst_ref=dst, send_sem=send.at[k], recv_sem=recv.at[k],
                                        device_id=to, device_id_type=MESH)


def _job_gather_send(shards):
    n = len(shards)

    def make(ins, outs, send, recv, local):
        x, y, c = _coords()
        targets = [(x, y, 1 - c), (1 - x, y, c), (x, 1 - y, c), (1 - x, 1 - y, c)]
        cps = []
        for a in range(n):
            dst = outs[a].at[4 * x + 2 * y + c]
            cps.append(pltpu.make_async_copy(ins[a], dst, local.at[a]))
            cps += [_remote(ins[a], dst, send, recv, 4 * a + k, t) for k, t in enumerate(targets)]
        return cps

    return _Job(shards, [SDS((ND,) + s.shape, s.dtype) for s in shards], {}, 4 * n, n, make)


def _job_gather_send_rows(shard, part, nparts, prev=None):
    rows = shard.shape[0] // nparts

    def make(ins, outs, send, recv, local):
        x, y, c = _coords()
        targets = [(x, y, 1 - c), (1 - x, y, c), (x, 1 - y, c), (1 - x, 1 - y, c)]
        src = ins[0].at[pl.ds(part * rows, rows)]
        dst = outs[0].at[4 * x + 2 * y + c].at[pl.ds(part * rows, rows)]
        return [pltpu.make_async_copy(src, dst, local.at[0])] + [
            _remote(src, dst, send, recv, k, t) for k, t in enumerate(targets)]

    ins = [shard] if prev is None else [shard, prev]
    return _Job(ins, [SDS((ND,) + shard.shape, shard.dtype)], {} if prev is None else {1: 0}, 4, 1, make)


def _job_gather_forward(gathered):
    n = len(gathered)

    def make(ins, outs, send, recv, local):
        x, y, c = _coords()
        cps = []
        for a in range(n):
            for k, (px, py) in enumerate([(1 - x, y), (x, 1 - y), (1 - x, 1 - y)]):
                blk = outs[a].at[4 * px + 2 * py + c]
                cps.append(_remote(blk, blk, send, recv, 3 * a + k, (x, y, 1 - c)))
        return cps

    return _Job(gathered, [SDS(g.shape, g.dtype) for g in gathered], {i: i for i in range(n)}, 3 * n, 0, make)


def _job_scatter_sibling(grads):
    n = len(grads)

    def make(ins, outs, send, recv, local):
        x, y, c = _coords()
        return [_remote(ins[a].at[pl.ds(4 * (1 - c), 4)], outs[a], send, recv, a, (x, y, 1 - c)) for a in range(n)]

    return _Job(grads, [SDS((4,) + g.shape[1:], g.dtype) for g in grads], {}, n, 0, make)


def _job_scatter_cross(sums):
    n = len(sums)

    def make(ins, outs, send, recv, local):
        x, y, c = _coords()
        chips = [(1 - x, y), (x, 1 - y), (1 - x, 1 - y)]
        return [_remote(ins[a].at[2 * px + py], outs[a].at[k], send, recv, 3 * a + k, (px, py, c))
                for a in range(n) for k, (px, py) in enumerate(chips)]

    return _Job(sums, [SDS((3,) + t.shape[1:], t.dtype) for t in sums], {}, 3 * n, 0, make)


def _pc(body, *, name, grid, in_specs, out_specs, out_shape, args, scratch=(), sem=(), alias=None, jobs=()):
    jobs = list(jobs)
    n_in, n_out, n_scr = len(in_specs), len(out_shape), len(scratch)
    aliases = dict(alias or {})
    job_args, job_shapes, job_scratch = [], [], []
    for j in jobs:
        for src, dst in j.alias.items():
            aliases[n_in + len(job_args) + src] = n_out + len(job_shapes) + dst
        job_args += j.ins
        job_shapes += j.out_shapes
        job_scratch += [pltpu.SemaphoreType.DMA((j.nsem,)), pltpu.SemaphoreType.DMA((j.nsem,)),
                        pltpu.SemaphoreType.DMA((max(j.nlocal, 1),))]

    def wrapped(*refs):
        ins = refs[:n_in]
        p = n_in + len(job_args)
        outs = refs[p:p + n_out]
        p += n_out + len(job_shapes)
        scr = refs[p:p + n_scr]
        sems = refs[p + n_scr:]
        copies = []
        pi, po = n_in, n_in + len(job_args) + n_out
        for k, j in enumerate(jobs):
            copies += j.make(refs[pi:pi + len(j.ins)], refs[po:po + len(j.out_shapes)], *sems[3 * k:3 * k + 3])
            pi += len(j.ins)
            po += len(j.out_shapes)
        gridded = bool(copies) and bool(grid)
        if gridded:
            ids = [pl.program_id(i) for i in range(len(grid))]
            first = functools.reduce(jnp.logical_and, [i == 0 for i in ids])
            last = functools.reduce(jnp.logical_and, [i == g - 1 for i, g in zip(ids, grid)])

            @pl.when(first)
            def _():
                for cp in copies:
                    cp.start()
        else:
            for cp in copies:
                cp.start()
        body(*ins, *outs, *scr)
        if gridded:
            @pl.when(last)
            def _():
                for cp in copies:
                    cp.wait()
        else:
            for cp in copies:
                cp.wait()

    kw = dict(grid=grid) if grid else {}
    semantics = ["arbitrary"] * len(grid) if jobs else list(sem)
    res = pl.pallas_call(
        wrapped, name=name, in_specs=list(in_specs) + [ANY] * len(job_args),
        out_specs=list(out_specs) + [ANY] * len(job_shapes), out_shape=list(out_shape) + job_shapes,
        scratch_shapes=list(scratch) + job_scratch, input_output_aliases=aliases,
        compiler_params=_cp(*semantics), **kw)(*args, *job_args)
    p = n_out
    for j in jobs:
        j.result = list(res[p:p + len(j.out_shapes)])
        p += len(j.out_shapes)
    return list(res[:n_out])


def _comm_call(jobs, name):
    _pc(lambda: None, name=name, grid=(), in_specs=[], out_specs=[], out_shape=[], args=[], jobs=jobs)


def _all_gather(arrs, name):
    n = len(arrs)

    def body(*refs):
        ins, outs = refs[:n], refs[n:2 * n]
        send_sems, recv_sems, local_sems = refs[2 * n:]
        x, y, c = lax.axis_index("x"), lax.axis_index("y"), lax.axis_index("c")
        me, sib = (x, y, c), (x, y, 1 - c)
        chips = [(1 - x, y), (x, 1 - y), (1 - x, 1 - y)]

        def copy(a, k, block, to, src=None):
            dst = outs[a].at[4 * block[0] + 2 * block[1] + block[2]]
            return pltpu.make_async_remote_copy(
                src_ref=dst if src is None else src, dst_ref=dst,
                send_sem=send_sems.at[7 * a + k], recv_sem=recv_sems.at[7 * a + k],
                device_id=to, device_id_type=MESH)

        mine = [pltpu.make_async_copy(ins[a], outs[a].at[4 * x + 2 * y + c], local_sems.at[a]) for a in range(n)]
        for cp in mine:
            cp.start()
        first = []
        for a in range(n):
            first.append(copy(a, 0, me, sib, src=ins[a]))
            first += [copy(a, 1 + j, me, (*chip, c), src=ins[a]) for j, chip in enumerate(chips)]
        for cp in first:
            cp.start()
        passed = []
        for a in range(n):
            for j, chip in enumerate(chips):
                copy(a, 1 + j, (*chip, c), me).wait_recv()
                fwd = copy(a, 4 + j, (*chip, c), sib)
                fwd.start()
                passed.append(fwd)
        for a in range(n):
            copy(a, 0, sib, me).wait_recv()
            for j, chip in enumerate(chips):
                copy(a, 4 + j, (*chip, 1 - c), me).wait_recv()
        for cp in first + passed:
            cp.wait_send()
        for cp in mine:
            cp.wait()

    return pl.pallas_call(
        body, name=name,
        out_shape=[SDS((ND,) + a.shape, a.dtype) for a in arrs],
        in_specs=[ANY] * n, out_specs=[ANY] * n,
        scratch_shapes=[pltpu.SemaphoreType.DMA((7 * n,)), pltpu.SemaphoreType.DMA((7 * n,)),
                        pltpu.SemaphoreType.DMA((n,))],
    )(*arrs)


ELEMENTWISE_TILE_BYTES = 3 * 512 * 1024


def _row_tile(rows, cols):
    fits = [t for t in range(16, rows + 1, 16) if rows % t == 0 and 4 * t * cols <= ELEMENTWISE_TILE_BYTES]
    return max(fits)


def _rs_add(g, r1, c_idx, name):
    _, rows, cols = g.shape

    def body(c_ref, g_ref, r_ref, o_ref):
        o_ref[...] = (g_ref[...].astype(F32) + r_ref[...].astype(F32)).astype(o_ref.dtype)

    return pl.pallas_call(
        body, name=name,
        grid_spec=pltpu.PrefetchScalarGridSpec(
            num_scalar_prefetch=1, grid=(4,),
            in_specs=[pl.BlockSpec((1, rows, cols), lambda q, c: (4 * c[0] + q, 0, 0)),
                      pl.BlockSpec((1, rows, cols), lambda q, c: (q, 0, 0))],
            out_specs=pl.BlockSpec((1, rows, cols), lambda q, c: (q, 0, 0))),
        out_shape=SDS((4, rows, cols), g.dtype),
        compiler_params=_cp("parallel"),
    )(c_idx, g, r1)


def _adam_math(w, g, m, v):
    m = B1 * m + (1.0 - B1) * g
    v = B2 * v + (1.0 - B2) * (g * g)
    m_hat = m / (1.0 - B1 ** STEP)
    v_hat = v / (1.0 - B2 ** STEP)
    delta = -LR * (m_hat / (jnp.sqrt(v_hat) + ADAM_EPS) + WD * w)
    return delta, m, v


def _adamw_big(w, m, v, t, r2, q_idx, name, part=0, prev=None):
    _, rows, cols = t.shape
    tr = _row_tile(rows, cols)
    nblk = rows // tr

    def body(q_ref, w_ref, m_ref, v_ref, t_ref, r_ref, *outs):
        g_out, d_out, m_out, v_out = outs[-4:]
        g = t_ref[0].astype(F32)
        for k in range(3):
            g = g + r_ref[k].astype(F32)
        d, mn, vn = _adam_math(w_ref[...], g, m_ref[...], v_ref[...])
        g_out[...], d_out[...], m_out[...], v_out[...] = g, d, mn, vn

    blk = pl.BlockSpec((tr, cols), lambda i, q: (part * nblk + i, 0))
    specs = [blk, blk, blk, pl.BlockSpec((1, tr, cols), lambda i, q: (q[0], i, 0)),
             pl.BlockSpec((3, tr, cols), lambda i, q: (0, i, 0))]
    ins = [q_idx, w, m, v, t, r2]
    alias = {}
    if prev is not None:
        specs += [ANY] * 4
        alias = {6 + k: k for k in range(4)}
        ins += list(prev)
    return pl.pallas_call(
        body, name=name,
        grid_spec=pltpu.PrefetchScalarGridSpec(num_scalar_prefetch=1, grid=(nblk,), in_specs=specs, out_specs=[blk] * 4),
        out_shape=[SDS(w.shape, F32)] * 4, input_output_aliases=alias,
        compiler_params=_cp("parallel"))(*ins)


def _sum_devices(g, name):
    _, rows, cols = g.shape

    def body(g_ref, o_ref):
        acc = g_ref[0]
        for k in range(1, ND):
            acc = acc + g_ref[k]
        o_ref[...] = acc

    return pl.pallas_call(body, name=name, out_shape=SDS((rows, cols), F32))(g)


def _adamw_small(w, g, m, v, name):
    def body(w_ref, g_ref, m_ref, v_ref, d_out, m_out, v_out):
        d, mn, vn = _adam_math(w_ref[...], g_ref[...], m_ref[...], v_ref[...])
        d_out[...], m_out[...], v_out[...] = d, mn, vn

    return pl.pallas_call(body, name=name, out_shape=[SDS(w.shape, F32)] * 3)(w, g, m, v)


ROWS = 256


def _rms_stats(x):
    r = lax.rsqrt(jnp.mean(x * x, axis=-1, keepdims=True) + RMS_EPS)
    return x * r, r


def _rms_fwd(x, gains, name, jobs=()):
    s, d = x.shape
    n = len(gains)

    def body(x_ref, *refs):
        xh, _ = _rms_stats(x_ref[...])
        for g_ref, o_ref in zip(refs[:n], refs[n:]):
            o_ref[...] = (xh * g_ref[...]).astype(BF16)

    row = pl.BlockSpec((ROWS, d), lambda i: (i, 0))
    vec = pl.BlockSpec((1, d), lambda i: (0, 0))
    return _pc(body, name=name, grid=(s // ROWS,), in_specs=[row] + [vec] * n, out_specs=[row] * n,
               out_shape=[SDS((s, d), BF16)] * n, sem=("parallel",), args=(x, *gains), jobs=jobs)


def _rms_bwd_rows(xh, r, gain, dy):
    u = dy * gain
    return r * (u - xh * jnp.mean(u * xh, axis=-1, keepdims=True))


def _rms_bwd(x, pairs, dres, name, colsum=False):
    s, d = x.shape
    n = len(pairs)

    def body(x_ref, dres_ref, *refs):
        g_refs, dy_refs = refs[:n], refs[n:2 * n]
        dx_ref, dxb_ref = refs[2 * n], refs[2 * n + 1]
        dg_refs = refs[2 * n + 2:2 * n + 2 + n]
        cs_ref = refs[-1] if colsum else None
        first = pl.program_id(0) == 0
        xh, r = _rms_stats(x_ref[...])
        dx = dres_ref[...]
        for g_ref, dy_ref, dg_ref in zip(g_refs, dy_refs, dg_refs):
            dy = dy_ref[...]
            dx = dx + _rms_bwd_rows(xh, r, g_ref[...], dy)

            @pl.when(first)
            def _():
                dg_ref[...] = jnp.zeros_like(dg_ref)
            dg_ref[...] += jnp.sum(dy * xh, axis=0, keepdims=True)
        dx_ref[...] = dx
        dxb_ref[...] = dx.astype(BF16)
        if colsum:
            @pl.when(first)
            def _():
                cs_ref[...] = jnp.zeros_like(cs_ref)
            cs_ref[...] += jnp.sum(dx, axis=0, keepdims=True)

    row = pl.BlockSpec((ROWS, d), lambda i: (i, 0))
    vec = pl.BlockSpec((1, d), lambda i: (0, 0))
    nvec = n + (1 if colsum else 0)
    outs = pl.pallas_call(
        body, name=name, grid=(s // ROWS,),
        in_specs=[row, row] + [vec] * n + [row] * n,
        out_specs=[row, row] + [vec] * nvec,
        out_shape=[SDS((s, d), F32), SDS((s, d), BF16)] + [SDS((1, d), F32)] * nvec,
        compiler_params=_cp("arbitrary"),
    )(x, dres, *[p[0] for p in pairs], *[p[1] for p in pairs])
    return outs


def _final_loss(h, target, gain):
    s, d = h.shape

    def body(h_ref, t_ref, g_ref, dh_ref, dhb_ref, dg_ref, loss_ref):
        first = pl.program_id(0) == 0
        xh, r = _rms_stats(h_ref[...])
        gain_v = g_ref[...]
        e = xh * gain_v - t_ref[...]
        dy = e * (1.0 / d)
        dx = _rms_bwd_rows(xh, r, gain_v, dy)
        dh_ref[...] = dx
        dhb_ref[...] = dx.astype(BF16)

        @pl.when(first)
        def _():
            dg_ref[...] = jnp.zeros_like(dg_ref)
            loss_ref[...] = jnp.zeros_like(loss_ref)
        dg_ref[...] += jnp.sum(dy * xh, axis=0, keepdims=True)
        loss_ref[...] += jnp.full((1, LANES), 0.5 / d, F32) * jnp.sum(e * e)

    row = pl.BlockSpec((ROWS, d), lambda i: (i, 0))
    vec = pl.BlockSpec((1, d), lambda i: (0, 0))
    return pl.pallas_call(
        body, name="final_loss", grid=(s // ROWS,),
        in_specs=[row, row, vec], out_specs=[row, row, vec, pl.BlockSpec((1, LANES), lambda i: (0, 0))],
        out_shape=[SDS((s, d), F32), SDS((s, d), BF16), SDS((1, d), F32), SDS((1, LANES), F32)],
        compiler_params=_cp("arbitrary"))(h, target, gain)


CT = 128


def _ln_stats(cv):
    mu = jnp.mean(cv, axis=-1, keepdims=True)
    xc = cv - mu
    rstd = lax.rsqrt(jnp.mean(xc * xc, axis=-1, keepdims=True) + LN_EPS)
    return xc * rstd, rstd


def _conv_fwd(glu, dw, dwb, lng, lnb, jobs=()):
    s, d = glu.shape
    hb = CT // CONV_PAD

    def body(x_ref, halo_ref, dw_ref, dwb_ref, lng_ref, lnb_ref, c_ref, s_ref):
        keep = (pl.program_id(0) > 0).astype(F32)

        def chunk(ci, carry):
            ls = pl.ds(pl.multiple_of(ci * LANES, LANES), LANES)
            xf = jnp.concatenate([halo_ref[:, ls] * keep, x_ref[:, ls]], axis=0)
            acc = jnp.zeros((CT, LANES), F32)
            for k in range(CONV_W):
                sh = CONV_W - 1 - k
                xs = pltpu.roll(xf, sh, 0) if sh else xf
                acc = acc + dw_ref[pl.ds(k, 1), ls] * xs[CONV_PAD:]
            c_ref[:, ls] = acc + dwb_ref[:, ls]
            return carry

        lax.fori_loop(0, d // LANES, chunk, 0)
        xh, _ = _ln_stats(c_ref[...])
        yv = xh * lng_ref[...] + lnb_ref[...]
        s_ref[...] = (yv * _sigmoid(yv)).astype(BF16)

    row = pl.BlockSpec((CT, d), lambda i: (i, 0))
    halo = pl.BlockSpec((CONV_PAD, d), lambda i: (jnp.maximum(i * hb - 1, 0), 0))
    vec = pl.BlockSpec((1, d), lambda i: (0, 0))
    taps = pl.BlockSpec((CONV_PAD, d), lambda i: (0, 0))
    return _pc(
        body, name="conv_fwd", grid=(s // CT,),
        in_specs=[row, halo, taps, vec, vec, vec], out_specs=[row, row],
        out_shape=[SDS((s, d), F32), SDS((s, d), BF16)], sem=("parallel",),
        args=(glu, glu, dw, dwb, lng, lnb), jobs=jobs)


def _ln_bwd(ds, cv, lng, lnb):
    s, d = cv.shape

    def body(ds_ref, c_ref, g_ref, b_ref, dc_ref, dg_ref, db_ref):
        first = pl.program_id(0) == 0
        xh, rstd = _ln_stats(c_ref[...])
        gv = g_ref[...]
        yv = xh * gv + b_ref[...]
        sg = _sigmoid(yv)
        dln = ds_ref[...] * (sg * (1.0 + yv * (1.0 - sg)))
        dxh = dln * gv
        dc_ref[...] = rstd * (dxh - jnp.mean(dxh, axis=-1, keepdims=True)
                              - xh * jnp.mean(dxh * xh, axis=-1, keepdims=True))

        @pl.when(first)
        def _():
            dg_ref[...] = jnp.zeros_like(dg_ref)
            db_ref[...] = jnp.zeros_like(db_ref)
        dg_ref[...] += jnp.sum(dln * xh, axis=0, keepdims=True)
        db_ref[...] += jnp.sum(dln, axis=0, keepdims=True)

    row = pl.BlockSpec((ROWS, d), lambda i: (i, 0))
    vec = pl.BlockSpec((1, d), lambda i: (0, 0))
    return pl.pallas_call(
        body, name="ln_bwd", grid=(s // ROWS,), in_specs=[row, row, vec, vec], out_specs=[row, vec, vec],
        out_shape=[SDS((s, d), F32), SDS((1, d), F32), SDS((1, d), F32)],
        compiler_params=_cp("arbitrary"))(ds, cv, lng, lnb)


def _conv_bwd(dc, glu, ua, ug, dw, jobs=()):
    s, d = dc.shape
    hb = CT // CONV_PAD
    nsteps = s // CT
    full = CT + CONV_PAD

    def body(dc_ref, dcn_ref, x_ref, xp_ref, ua_ref, ug_ref, dw_ref, du_ref, ddw_ref, ddwb_ref, db1_ref, part):
        i = pl.program_id(0)
        keep_prev = (i > 0).astype(F32)
        keep_next = (i < nsteps - 1).astype(F32)

        @pl.when(i == 0)
        def _():
            part[...] = jnp.zeros_like(part)
            ddwb_ref[...] = jnp.zeros_like(ddwb_ref)
            db1_ref[...] = jnp.zeros_like(db1_ref)

        def chunk(ci, carry):
            off = pl.multiple_of(ci * LANES, LANES)
            ls = pl.ds(off, LANES)
            ls2 = pl.ds(pl.multiple_of(d + ci * LANES, LANES), LANES)
            dcc = dc_ref[:, ls]
            dcf = jnp.concatenate([dcc, dcn_ref[:, ls] * keep_next], axis=0)
            xf = jnp.concatenate([xp_ref[:, ls] * keep_prev, x_ref[:, ls]], axis=0)
            dglu = jnp.zeros((CT, LANES), F32)
            for k in range(CONV_W):
                sh = CONV_W - 1 - k
                dshift = pltpu.roll(dcf, full - sh, 0) if sh else dcf
                dglu = dglu + dw_ref[pl.ds(k, 1), ls] * dshift[:CT]
                xs = pltpu.roll(xf, sh, 0) if sh else xf
                part[pl.ds(8 * k, 8), ls] += jnp.sum((dcc * xs[CONV_PAD:]).reshape(CT // 8, 8, LANES), axis=0)
            ddwb_ref[:, ls] += jnp.sum(dcc, axis=0, keepdims=True)
            av, gv = ua_ref[:, ls], ug_ref[:, ls]
            sg = _sigmoid(gv)
            da = dglu * sg
            dgt = dglu * av * sg * (1.0 - sg)
            du_ref[:, ls] = da.astype(BF16)
            du_ref[:, ls2] = dgt.astype(BF16)
            db1_ref[:, ls] += jnp.sum(da, axis=0, keepdims=True)
            db1_ref[:, ls2] += jnp.sum(dgt, axis=0, keepdims=True)
            return carry

        lax.fori_loop(0, d // LANES, chunk, 0)

        @pl.when(i == nsteps - 1)
        def _():
            ddw_ref[...] = jnp.sum(part[...].reshape(CONV_PAD, 8, d), axis=1)

    row = pl.BlockSpec((CT, d), lambda i: (i, 0))
    prev = pl.BlockSpec((CONV_PAD, d), lambda i: (jnp.maximum(i * hb - 1, 0), 0))
    nxt = pl.BlockSpec((CONV_PAD, d), lambda i: (jnp.minimum((i + 1) * hb, s // CONV_PAD - 1), 0))
    taps = pl.BlockSpec((CONV_PAD, d), lambda i: (0, 0))
    return _pc(
        body, name="conv_bwd", grid=(nsteps,),
        in_specs=[row, nxt, row, prev, row, row, taps],
        out_specs=[pl.BlockSpec((CT, 2 * d), lambda i: (i, 0)), taps, pl.BlockSpec((1, d), lambda i: (0, 0)),
                   pl.BlockSpec((1, 2 * d), lambda i: (0, 0))],
        out_shape=[SDS((s, 2 * d), BF16), SDS((CONV_PAD, d), F32), SDS((1, d), F32), SDS((1, 2 * d), F32)],
        scratch=[pltpu.VMEM((8 * CONV_PAD, d), F32)],
        sem=("arbitrary",), args=(dc, dc, glu, glu, ua, ug, dw), jobs=jobs)


TM = 1024
TS = 1024


def _glu_mm(n1, w1g, b1, jobs=()):
    s, d = n1.shape
    cw = w1g.shape[2]
    half = ND // 2

    def body(a_ref, wa_ref, wg_ref, ba_ref, bg_ref, ua_ref, ug_ref, glu_ref):
        a = a_ref[...]
        ua = _dot(a, wa_ref[0], NN) + ba_ref[...]
        ug = _dot(a, wg_ref[0], NN) + bg_ref[...]
        ua_ref[...], ug_ref[...] = ua, ug
        glu_ref[...] = ua * _sigmoid(ug)

    out = pl.BlockSpec((TM, cw), lambda m, i: (m, i))
    return _pc(
        body, name="glu_mm", grid=(s // TM, half),
        in_specs=[pl.BlockSpec((TM, d), lambda m, i: (m, 0)),
                  pl.BlockSpec((1, d, cw), lambda m, i: (i, 0, 0)),
                  pl.BlockSpec((1, d, cw), lambda m, i: (i + half, 0, 0)),
                  pl.BlockSpec((1, cw), lambda m, i: (0, i)),
                  pl.BlockSpec((1, cw), lambda m, i: (0, i + half))],
        out_specs=[out, out, out], out_shape=[SDS((s, d), F32)] * 3,
        sem=("parallel", "arbitrary"), args=(n1, w1g, w1g, b1, b1), jobs=jobs)


def _mm_rows(a, wg, name, res=None, bias=None, out_dtype=F32, tn=512, branches=False, jobs=()):
    s, kdim = a.shape
    _, kc, n = wg.shape
    assert kc * ND == kdim
    nx = 2 + (res is not None) + (bias is not None)

    def body(*refs):
        acc = _dot(refs[0][...], refs[1][...].reshape(kdim, tn), NN)
        for extra in refs[2:nx]:
            acc = acc + extra[...]
        refs[nx][...] = acc.astype(out_dtype)
        if branches:
            scr = refs[-1]
            _stage(scr, acc)
            for o_ref, dil in zip(refs[nx + 1:], SPLIT_DILATIONS):
                _split_rows(scr, o_ref, dil)

    ins, specs = [a, wg], [pl.BlockSpec((TM, kdim), lambda m, j: (m, 0)), pl.BlockSpec((ND, kc, tn), lambda m, j: (0, 0, j))]
    if res is not None:
        ins.append(res)
        specs.append(pl.BlockSpec((TM, tn), lambda m, j: (m, j)))
    if bias is not None:
        ins.append(bias)
        specs.append(pl.BlockSpec((1, tn), lambda m, j: (0, j)))
    out_specs, out_shape, scratch = [pl.BlockSpec((TM, tn), lambda m, j: (m, j))], [SDS((s, n), out_dtype)], []
    if branches:
        out_specs += _branch_specs(TM, tn, lambda dil, m, j: (0, m, j))
        out_shape += [SDS((dil, s // dil, n), out_dtype) for dil in SPLIT_DILATIONS]
        scratch = [pltpu.VMEM((tn // LANES, TM, LANES), F32)]
    outs = _pc(body, name=name, grid=(s // TM, n // tn), in_specs=specs, out_specs=out_specs, out_shape=out_shape,
               scratch=scratch, sem=("parallel", "arbitrary"), args=ins, jobs=jobs)
    return [outs[0][None]] + outs[1:] if branches else outs[0]


def _swiglu_mm(n2, wgg, wug, name, jobs=()):
    s, d = n2.shape
    fc = wgg.shape[2]

    def body(a_ref, wg_ref, wu_ref, g_ref, u_ref, act_ref):
        a = a_ref[...]
        g = _dot(a, wg_ref[0], NN)
        u = _dot(a, wu_ref[0], NN)
        g_ref[0], u_ref[0] = g.astype(BF16), u.astype(BF16)
        act_ref[0] = (g * _sigmoid(g) * u).astype(BF16)

    wspec = pl.BlockSpec((1, d, fc), lambda m, j: (j, 0, 0))
    out = pl.BlockSpec((1, TM, fc), lambda m, j: (j, m, 0))
    return _pc(
        body, name=name, grid=(s // TM, ND),
        in_specs=[pl.BlockSpec((TM, d), lambda m, j: (m, 0)), wspec, wspec],
        out_specs=[out, out, out], out_shape=[SDS((ND, s, fc), BF16)] * 3,
        sem=("parallel", "arbitrary"), args=(n2, wgg, wug), jobs=jobs)


def _down_mm(act, wdg, res, name, jobs=()):
    _, s, fc = act.shape
    d = wdg.shape[2]

    def body(a_ref, w_ref, r_ref, o_ref):
        @pl.when(pl.program_id(1) == 0)
        def _():
            o_ref[...] = r_ref[...]
        o_ref[...] += _dot(a_ref[0], w_ref[0], NN)

    row = pl.BlockSpec((TM, d), lambda m, j: (m, 0))
    return _pc(
        body, name=name, grid=(s // TM, ND),
        in_specs=[pl.BlockSpec((1, TM, fc), lambda m, j: (j, m, 0)),
                  pl.BlockSpec((1, fc, d), lambda m, j: (j, 0, 0)), row],
        out_specs=[row], out_shape=[SDS((s, d), F32)],
        sem=("parallel", "arbitrary"), args=(act, wdg, res), jobs=jobs)[0]


def _dact_mm(dh, wdg, gate, up, name, jobs=()):
    s, d = dh.shape
    fc = wdg.shape[1]

    def body(a_ref, w_ref, g_ref, u_ref, dg_ref, du_ref):
        dact = _dot(a_ref[...], w_ref[0], NT)
        g, u = g_ref[0].astype(F32), u_ref[0].astype(F32)
        sg = _sigmoid(g)
        du_ref[0] = (dact * (g * sg)).astype(BF16)
        dg_ref[0] = (dact * u * (sg * (1.0 + g * (1.0 - sg)))).astype(BF16)

    blk = pl.BlockSpec((1, TM, fc), lambda m, j: (j, m, 0))
    return _pc(
        body, name=name, grid=(s // TM, ND),
        in_specs=[pl.BlockSpec((TM, d), lambda m, j: (m, 0)),
                  pl.BlockSpec((1, fc, d), lambda m, j: (j, 0, 0)), blk, blk],
        out_specs=[blk, blk], out_shape=[SDS((ND, s, fc), BF16)] * 2,
        sem=("parallel", "arbitrary"), args=(dh, wdg, gate, up), jobs=jobs)


def _dwd_mm(act, dh, name, jobs=()):
    _, s, fc = act.shape
    d = dh.shape[1]
    nk = s // TS

    def body(a_ref, b_ref, o_ref, acc):
        k = pl.program_id(1)

        @pl.when(k == 0)
        def _():
            acc[...] = jnp.zeros_like(acc)
        acc[...] += _dot(a_ref[0], b_ref[...], TN)

        @pl.when(k == nk - 1)
        def _():
            o_ref[0] = acc[...].astype(BF16)

    return _pc(
        body, name=name, grid=(ND, nk),
        in_specs=[pl.BlockSpec((1, TS, fc), lambda j, k: (j, k, 0)), pl.BlockSpec((TS, d), lambda j, k: (k, 0))],
        out_specs=[pl.BlockSpec((1, fc, d), lambda j, k: (_slot(j), 0, 0))],
        out_shape=[SDS((ND, fc, d), BF16)], scratch=[pltpu.VMEM((fc, d), F32)],
        sem=("parallel", "arbitrary"), args=(act, dh), jobs=jobs)[0]


def _dwgu_mm(n2, dgate, dup, name, jobs=()):
    s, d = n2.shape
    fc = dgate.shape[2]
    nk = s // TS

    def body(a_ref, g_ref, u_ref, og_ref, ou_ref, accg, accu):
        k = pl.program_id(1)

        @pl.when(k == 0)
        def _():
            accg[...] = jnp.zeros_like(accg)
            accu[...] = jnp.zeros_like(accu)
        a = a_ref[...]
        accg[...] += _dot(g_ref[0], a, TN)
        accu[...] += _dot(u_ref[0], a, TN)

        @pl.when(k == nk - 1)
        def _():
            og_ref[0] = accg[...].astype(BF16)
            ou_ref[0] = accu[...].astype(BF16)

    blk = pl.BlockSpec((1, TS, fc), lambda j, k: (j, k, 0))
    out = pl.BlockSpec((1, fc, d), lambda j, k: (_slot(j), 0, 0))
    return _pc(
        body, name=name, grid=(ND, nk),
        in_specs=[pl.BlockSpec((TS, d), lambda j, k: (k, 0)), blk, blk], out_specs=[out, out],
        out_shape=[SDS((ND, fc, d), BF16)] * 2,
        scratch=[pltpu.VMEM((fc, d), F32), pltpu.VMEM((fc, d), F32)],
        sem=("parallel", "arbitrary"), args=(n2, dgate, dup), jobs=jobs)


def _dn_ffn_mm(dgate, dup, wgg, wug, name, jobs=()):
    _, s, fc = dgate.shape
    d = wgg.shape[1]

    def body(g_ref, u_ref, wg_ref, wu_ref, o_ref):
        j = pl.program_id(1)

        @pl.when(j == 0)
        def _():
            o_ref[...] = jnp.zeros_like(o_ref)
        o_ref[...] += _dot(g_ref[0], wg_ref[0], NT) + _dot(u_ref[0], wu_ref[0], NT)

    blk = pl.BlockSpec((1, TM, fc), lambda m, j: (j, m, 0))
    wspec = pl.BlockSpec((1, d, fc), lambda m, j: (j, 0, 0))
    return _pc(
        body, name=name, grid=(s // TM, ND), in_specs=[blk, blk, wspec, wspec],
        out_specs=[pl.BlockSpec((TM, d), lambda m, j: (m, 0))], out_shape=[SDS((s, d), F32)],
        sem=("parallel", "arbitrary"), args=(dgate, dup, wgg, wug), jobs=jobs)[0]


def _mm_rows_t(pairs, name, out_dtype, branches=False, jobs=()):
    s, n = pairs[0][0].shape
    _, kc, _ = pairs[0][1].shape
    np_ = len(pairs)
    grp = ND // 2
    wide = grp * kc

    def body(*refs):
        o_ref = refs[2 * np_]
        for i in range(grp):
            acc = None
            for p in range(np_):
                t = _dot(refs[p][...], refs[np_ + p][i], NT)
                acc = t if acc is None else acc + t
            o_ref[:, kc * i:kc * (i + 1)] = acc.astype(out_dtype)
            if branches:
                for c, ls in enumerate(_lane_chunks(kc)):
                    refs[-1][i * (kc // LANES) + c] = acc[:, ls]
        if branches:
            for b_ref, dil in zip(refs[2 * np_ + 1:], SPLIT_DILATIONS):
                _split_rows(refs[-1], b_ref, dil)

    out_specs, out_shape, scratch = [pl.BlockSpec((TM, wide), lambda m, j: (m, j))], [SDS((s, kc * ND), out_dtype)], []
    if branches:
        out_specs += _branch_specs(TM, wide, lambda dil, m, j: (0, m, j))
        out_shape += [SDS((dil, s // dil, kc * ND), out_dtype) for dil in SPLIT_DILATIONS]
        scratch = [pltpu.VMEM((wide // LANES, TM, LANES), F32)]
    outs = _pc(
        body, name=name, grid=(s // TM, ND // grp),
        in_specs=[pl.BlockSpec((TM, n), lambda m, j: (m, 0))] * np_ + [pl.BlockSpec((grp, kc, n), lambda m, j: (j, 0, 0))] * np_,
        out_specs=out_specs, out_shape=out_shape, scratch=scratch,
        sem=("parallel", "arbitrary"), args=[p[0] for p in pairs] + [p[1] for p in pairs], jobs=jobs)
    return [outs[0][None]] + outs[1:] if branches else outs[0]


def _dw_rows_mm(a, b, name):
    s, kdim = a.shape
    n = b.shape[1]
    kc = kdim // ND
    ts = TS // 2
    nk = s // ts

    def body(a_ref, b_ref, o_ref, acc):
        k = pl.program_id(0)

        @pl.when(k == 0)
        def _():
            acc[...] = jnp.zeros_like(acc)
        acc[...] += _dot(a_ref[...], b_ref[...], TN)

        @pl.when(k == nk - 1)
        def _():
            for dev in range(ND):
                o_ref[_slot(dev)] = acc[kc * dev:kc * (dev + 1), :].astype(BF16)

    return pl.pallas_call(
        body, name=name, grid=(nk,),
        in_specs=[pl.BlockSpec((ts, kdim), lambda k: (k, 0)), pl.BlockSpec((ts, n), lambda k: (k, 0))],
        out_specs=pl.BlockSpec((ND, kc, n), lambda k: (0, 0, 0)), out_shape=SDS((ND, kc, n), BF16),
        scratch_shapes=[pltpu.VMEM((kdim, n), F32)], compiler_params=_cp("arbitrary"))(a, b)


def _dw1_mm(n1, du, jobs=()):
    s, d = n1.shape
    cw = du.shape[1] // ND
    nk = s // TS

    def body(a_ref, b_ref, o_ref, acc):
        k = pl.program_id(1)

        @pl.when(k == 0)
        def _():
            acc[...] = jnp.zeros_like(acc)
        acc[...] += _dot(a_ref[...], b_ref[...], TN)

        @pl.when(k == nk - 1)
        def _():
            o_ref[0] = acc[...].astype(BF16)

    return _pc(
        body, name="dw1_mm", grid=(ND, nk),
        in_specs=[pl.BlockSpec((TS, d), lambda j, k: (k, 0)), pl.BlockSpec((TS, cw), lambda j, k: (k, j))],
        out_specs=[pl.BlockSpec((1, d, cw), lambda j, k: (_slot(j), 0, 0))], out_shape=[SDS((ND, d, cw), BF16)],
        scratch=[pltpu.VMEM((d, cw), F32)], sem=("parallel", "arbitrary"), args=(n1, du), jobs=jobs)[0]


def _dn1_mm(du, w1g, part, nparts, prev=None, jobs=()):
    s = du.shape[0]
    _, d, cw = w1g.shape
    steps = s // TM // nparts
    m0 = part * steps

    def body(a_ref, w_ref, *refs):
        o_ref = refs[-1]
        j = pl.program_id(1)

        @pl.when(j == 0)
        def _():
            o_ref[...] = jnp.zeros_like(o_ref)
        o_ref[...] += _dot(a_ref[...], w_ref[0], NT)

    ins = [du, w1g] if prev is None else [du, w1g, prev]
    specs = [pl.BlockSpec((TM, cw), lambda m, j: (m0 + m, j)), pl.BlockSpec((1, d, cw), lambda m, j: (j, 0, 0))]
    return _pc(
        body, name=f"dn1_mm{part}", grid=(steps, ND), in_specs=specs if prev is None else specs + [ANY],
        out_specs=[pl.BlockSpec((TM, d), lambda m, j: (m0 + m, 0))], out_shape=[SDS((s, d), F32)],
        alias=None if prev is None else {2: 0}, sem=("parallel", "arbitrary"), args=ins, jobs=jobs)[0]


FAR = 1e33


def _slopes(heads):
    return [2.0 ** (-8.0 * (h + 1) / heads) for h in range(heads)]


def _band(has_prev):
    qi = lax.broadcasted_iota(jnp.int32, (BLK, 2 * BLK), 0)
    ki = lax.broadcasted_iota(jnp.int32, (BLK, 2 * BLK), 1)
    j = qi - ki + BLK
    ok = (j >= 0) & (j <= BLK) & (has_prev | (ki >= BLK))
    return jnp.where(ok, j.astype(F32), FAR)


SPLIT_DILATIONS = tuple(dil for dil in BRANCH_DILATIONS if dil > 1)


def _lane_chunks(w):
    return [slice(LANES * c, LANES * (c + 1)) for c in range(w // LANES)]


def _stage(scr, tile):
    for c, ls in enumerate(_lane_chunks(tile.shape[1])):
        scr[c] = tile[:, ls]


def _split_rows(scr, o_ref, dil):
    _, n, w = o_ref.shape
    for r in range(dil):
        for c, ls in enumerate(_lane_chunks(w)):
            o_ref[r, :, ls] = scr[c, pl.ds(r, n, stride=dil), :].astype(o_ref.dtype)


def _join_rows(i_ref, scr, dil):
    _, n, w = i_ref.shape
    for r in range(dil):
        for c, ls in enumerate(_lane_chunks(w)):
            scr[c, pl.ds(r, n, stride=dil), :] = i_ref[r, :, ls].astype(F32)


def _unstage(scr, w):
    return jnp.concatenate([scr[c] for c in range(w // LANES)], axis=1)


def _branch_specs(rows, w, index):
    return [pl.BlockSpec((dil, rows // dil, w), functools.partial(index, dil)) for dil in SPLIT_DILATIONS]


def _attn_fwd(q, k, v, dil, jobs=()):
    _, l, d = q.shape
    heads = d // HEAD
    assert heads <= HEAD
    scale = HEAD ** -0.5
    slopes = _slopes(heads)

    def body(q_ref, kc_ref, kp_ref, vc_ref, vp_ref, o_ref, lse_ref):
        dist = _band(pl.program_id(1) > 0)
        lane = lax.broadcasted_iota(jnp.int32, (BLK, HEAD), 1)
        lse = jnp.zeros((BLK, HEAD), F32)
        for h in range(heads):
            sl = slice(HEAD * h, HEAD * (h + 1))
            kh = jnp.concatenate([kp_ref[0, :, sl], kc_ref[0, :, sl]], axis=0)
            vh = jnp.concatenate([vp_ref[0, :, sl], vc_ref[0, :, sl]], axis=0)
            logits = _dot(q_ref[0, :, sl], kh, NT) * scale + dist * (-slopes[h] * dil)
            m = jnp.max(logits, axis=-1, keepdims=True)
            p = jnp.exp(logits - m)
            den = jnp.sum(p, axis=-1, keepdims=True)
            o_ref[0, :, sl] = _dot(p.astype(BF16), vh, NN) / den
            lse = jnp.where(lane == h, m + jnp.log(den), lse)
        lse_ref[0] = lse

    cur = pl.BlockSpec((1, BLK, d), lambda r, b: (r, b, 0))
    prev = pl.BlockSpec((1, BLK, d), lambda r, b: (r, jnp.maximum(b - 1, 0), 0))
    return _pc(
        body, name=f"attn_fwd_d{dil}", grid=(dil, l // BLK),
        in_specs=[cur, cur, prev, cur, prev], out_specs=[cur, pl.BlockSpec((1, BLK, HEAD), lambda r, b: (r, b, 0))],
        out_shape=[SDS((dil, l, d), F32), SDS((dil, l, HEAD), F32)], sem=("parallel", "arbitrary"),
        args=(q, k, k, v, v), jobs=jobs)


def _attn_merge(outs, lses):
    _, s, d = outs[0].shape
    heads = d // HEAD
    nb = len(outs)
    nsplit = nb - 1

    def body(*refs):
        o_refs, l_refs = refs[:nb], refs[nb:2 * nb]
        att_refs, lse_refs = refs[2 * nb:3 * nb], refs[3 * nb:4 * nb]
        scr_o, scr_l, scr_att = refs[4 * nb:4 * nb + nsplit], refs[4 * nb + nsplit:4 * nb + 2 * nsplit], refs[-1]
        ls = [l_refs[0][...]]
        for k, dil in enumerate(SPLIT_DILATIONS):
            _join_rows(o_refs[1 + k], scr_o[k], dil)
            _join_rows(l_refs[1 + k], scr_l[k], dil)
            ls.append(scr_l[k][0])
        m = functools.reduce(jnp.maximum, ls)
        ws = [jnp.exp(v - m) for v in ls]
        den = functools.reduce(jnp.add, ws)
        ws = [w / den for w in ws]
        lse_refs[0][...] = m + jnp.log(den)
        scr_l[0][0] = m + jnp.log(den)
        for h in range(heads):
            sl = slice(HEAD * h, HEAD * (h + 1))
            slab = ws[0][:, h:h + 1] * o_refs[0][:, sl]
            for k in range(nsplit):
                slab = slab + ws[1 + k][:, h:h + 1] * scr_o[k][h]
            att_refs[0][:, sl] = slab.astype(BF16)
            scr_att[h] = slab
        for k, dil in enumerate(SPLIT_DILATIONS):
            _split_rows(scr_att, att_refs[1 + k], dil)
            _split_rows(scr_l[0], lse_refs[1 + k], dil)

    def specs(w):
        return [pl.BlockSpec((ROWS, w), lambda i: (i, 0))] + _branch_specs(ROWS, w, lambda dil, i: (0, i, 0))

    def shapes(w, dt):
        return [SDS((s, w), dt)] + [SDS((dil, s // dil, w), dt) for dil in SPLIT_DILATIONS]

    wide, narrow = pltpu.VMEM((heads, ROWS, LANES), F32), pltpu.VMEM((1, ROWS, LANES), F32)
    res = pl.pallas_call(
        body, name="attn_merge", grid=(s // ROWS,), in_specs=specs(d) + specs(HEAD), out_specs=specs(d) + specs(HEAD),
        out_shape=shapes(d, BF16) + shapes(HEAD, F32),
        scratch_shapes=[wide] * nsplit + [narrow] * nsplit + [wide],
        compiler_params=_cp("parallel"))(outs[0].reshape(s, d), *outs[1:], lses[0].reshape(s, HEAD), *lses[1:])
    return list(res[:nb]), list(res[nb:])


def _attn_bwd(q, k, v, do, o, lse, dil, jobs=()):
    _, l, d = q.shape
    nb = l // BLK
    heads = d // HEAD
    scale = HEAD ** -0.5
    slopes = _slopes(heads)
    whole = 2 * l * d <= RESIDENT_BYTES
    steps = nb if whole else nb + 1

    def body(q_ref, kc_ref, kp_ref, vc_ref, vp_ref, do_ref, o_ref, lse_ref, dq_ref, dk_ref, dv_ref, ck, cv):
        b = pl.program_id(1)
        rows = pl.ds(pl.multiple_of(jnp.maximum(b - 1, 0) * BLK, BLK), BLK) if whole else slice(None)

        @pl.when(b == 0)
        def _():
            ck[...] = jnp.zeros_like(ck)
            cv[...] = jnp.zeros_like(cv)

        @pl.when(b < nb)
        def _():
            dist = _band(b > 0)
            lanes = [slice(HEAD * h, HEAD * (h + 1)) for h in range(heads)]
            keys = [jnp.concatenate([kp_ref[0, :, sl], kc_ref[0, :, sl]], axis=0) for sl in lanes]
            vals = [jnp.concatenate([vp_ref[0, :, sl], vc_ref[0, :, sl]], axis=0) for sl in lanes]
            scores = [_dot(q_ref[0, :, sl], kh, NT) for sl, kh in zip(lanes, keys)]
            dps = [_dot(do_ref[0, :, sl], vh, NT) for sl, vh in zip(lanes, vals)]
            ps, dss = [], []
            for h, sl in enumerate(lanes):
                doh = do_ref[0, :, sl]
                delta = jnp.sum(doh.astype(F32) * o_ref[0, :, sl].astype(F32), axis=-1, keepdims=True)
                p = jnp.exp(scores[h] * scale + dist * (-slopes[h] * dil) - lse_ref[0, :, h:h + 1])
                ps.append(p.astype(BF16))
                dss.append((p * (dps[h] - delta)).astype(BF16))
            for h, sl in enumerate(lanes):
                qh, doh = q_ref[0, :, sl], do_ref[0, :, sl]
                dq_ref[0, :, sl] = (_dot(dss[h], keys[h], NN) * scale).astype(BF16)
                dk2 = _dot(dss[h], qh, TN) * scale
                dv2 = _dot(ps[h], doh, TN)
                dk_ref[0, rows, sl] = (ck[:, sl] + dk2[:BLK]).astype(BF16)
                dv_ref[0, rows, sl] = (cv[:, sl] + dv2[:BLK]).astype(BF16)
                ck[:, sl] = dk2[BLK:]
                cv[:, sl] = dv2[BLK:]

        @pl.when(b == steps - 1)
        def _():
            last = pl.ds((nb - 1) * BLK, BLK) if whole else slice(None)
            dk_ref[0, last, :] = ck[...].astype(BF16)
            dv_ref[0, last, :] = cv[...].astype(BF16)

    cur = pl.BlockSpec((1, BLK, d), lambda r, b: (r, jnp.minimum(b, nb - 1), 0))
    prev = pl.BlockSpec((1, BLK, d), lambda r, b: (r, jnp.clip(b - 1, 0, nb - 1), 0))
    lse_spec = pl.BlockSpec((1, BLK, HEAD), lambda r, b: (r, jnp.minimum(b, nb - 1), 0))
    dkv = pl.BlockSpec((1, l, d), lambda r, b: (r, 0, 0)) if whole else prev
    return _pc(
        body, name=f"attn_bwd_d{dil}", grid=(dil, steps),
        in_specs=[cur, cur, prev, cur, prev, cur, cur, lse_spec], out_specs=[cur, dkv, dkv],
        out_shape=[SDS((dil, l, d), BF16)] * 3,
        scratch=[pltpu.VMEM((BLK, d), F32), pltpu.VMEM((BLK, d), F32)],
        sem=("parallel", "arbitrary"), args=(q, k, k, v, v, do, o, lse), jobs=jobs)


def _sum_cast(xs, name):
    _, s, d = xs[0].shape
    nsplit = len(xs) - 1

    def body(*refs):
        i_refs, o_ref, scr = refs[:nsplit + 1], refs[nsplit + 1], refs[nsplit + 2:]
        acc = i_refs[0][...].astype(F32)
        for k, dil in enumerate(SPLIT_DILATIONS):
            _join_rows(i_refs[1 + k], scr[k], dil)
            acc = acc + _unstage(scr[k], d)
        o_ref[...] = acc.astype(BF16)

    row = pl.BlockSpec((ROWS, d), lambda i: (i, 0))
    return pl.pallas_call(
        body, name=name, grid=(s // ROWS,), in_specs=[row] + _branch_specs(ROWS, d, lambda dil, i: (0, i, 0)),
        out_specs=row, out_shape=SDS((s, d), BF16),
        scratch_shapes=[pltpu.VMEM((d // LANES, ROWS, LANES), F32)] * nsplit,
        compiler_params=_cp("parallel"))(xs[0].reshape(s, d), *xs[1:])


def _pack_rows(vs, width):
    flat = jnp.concatenate([v.reshape(-1) for v in vs])
    spans, at = [], 0
    for v in vs:
        spans.append((at, v.size))
        at += v.size
    rows = -(-at // width)
    rows = -(-rows // 8) * 8
    flat = jnp.pad(flat, (0, rows * width - at))
    return flat.reshape(rows, width), spans


def kernel(x, a_norm_g, conv_w1, conv_b1, conv_dw, conv_dw_b, conv_ln_g, conv_ln_b, conv_w2, conv_b2, kv_norm_g, w_k, w_v, b_norm_g, w_q, w_o, ffn_norm_g, ffn_w_gate, ffn_w_up, ffn_w_down, final_norm_g, loss_target, m_a_norm_g, m_conv_w1, m_conv_b1, m_conv_dw, m_conv_dw_b, m_conv_ln_g, m_conv_ln_b, m_conv_w2, m_conv_b2, m_kv_norm_g, m_w_k, m_w_v, m_b_norm_g, m_w_q, m_w_o, m_ffn_norm_g, m_ffn_w_gate, m_ffn_w_up, m_ffn_w_down, m_final_norm_g, v_a_norm_g, v_conv_w1, v_conv_b1, v_conv_dw, v_conv_dw_b, v_conv_ln_g, v_conv_ln_b, v_conv_w2, v_conv_b2, v_kv_norm_g, v_w_k, v_w_v, v_b_norm_g, v_w_q, v_w_o, v_ffn_norm_g, v_ffn_w_gate, v_ffn_w_up, v_ffn_w_down, v_final_norm_g):
    s, d = x.shape[1], x.shape[2]
    dc = d // ND
    h0 = x[0]
    target = loss_target[0]
    xi, yi, ci = lax.axis_index("x"), lax.axis_index("y"), lax.axis_index("c")
    me = 4 * xi + 2 * yi + ci
    c_idx = jnp.reshape(ci, (1,)).astype(jnp.int32)
    q_idx = jnp.reshape(2 * xi + yi, (1,)).astype(jnp.int32)

    bf = lambda w: w.astype(BF16)
    small_shards = [a_norm_g, conv_b1, conv_dw, conv_dw_b, conv_ln_g, conv_ln_b, conv_b2]
    sp, sp_spans = _pack_rows(small_shards, dc)
    w1g, spg = _all_gather([bf(conv_w1[0]), sp], "gather_first")
    spg = spg.reshape(ND, -1)

    def small_full(i, rows):
        at, size = sp_spans[i]
        return spg[:, at:at + size].reshape(ND, rows, size // rows).transpose(1, 0, 2).reshape(rows, -1)

    a_g = small_full(0, 1)
    b1 = small_full(1, 1)
    dw = jnp.pad(small_full(2, CONV_W), ((0, CONV_PAD - CONV_W), (0, 0)))
    dwb, lng, lnb, b2 = small_full(3, 1), small_full(4, 1), small_full(5, 1), small_full(6, 1)
    kv_g, q_g, fin_g = kv_norm_g.reshape(1, d), b_norm_g.reshape(1, d), final_norm_g.reshape(1, d)
    f_g = [ffn_norm_g[0:1], ffn_norm_g[1:2]]

    def send(*shards):
        return _job_gather_send([bf(t) for t in shards])

    def forward(job):
        return _job_gather_forward(job.result)

    def send_half(w, part, first=None):
        return _job_gather_send_rows(bf(w), part, 2, None if first is None else first.result[0])

    s_w2 = send(conv_w2[0])
    (n1,) = _rms_fwd(h0, [a_g], "rms_a", jobs=[s_w2])
    f_w2, s_g0 = forward(s_w2), send(ffn_w_gate[0])
    ua, ug, glu = _glu_mm(n1, w1g, b1, jobs=[f_w2, s_g0])
    (w2g,) = f_w2.result
    f_g0, s_u0 = forward(s_g0), send(ffn_w_up[0])
    cv, sw = _conv_fwd(glu, dw, dwb, lng, lnb, jobs=[f_g0, s_u0])
    (wg0,) = f_g0.result
    f_u0 = forward(s_u0)
    h1 = _mm_rows(sw, w2g, "w2_mm", res=h0, bias=b2, jobs=[f_u0])
    (wu0,) = f_u0.result
    (n2a,) = _rms_fwd(h1, [f_g[0]], "rms_f0")
    s_d0, s_kv = send(ffn_w_down[0]), send(w_k, w_v)
    gate0, up0, act0 = _swiglu_mm(n2a, wg0, wu0, "swiglu_mm0", jobs=[s_d0, s_kv])
    f_d0 = forward(s_d0)
    _comm_call([f_d0], "forward_mid")
    (wd0,) = f_d0.result
    f_kv, s_qo = forward(s_kv), send(w_q[0], w_o[0])
    h2 = _down_mm(act0, wd0, h1, "down_mm0", jobs=[f_kv, s_qo])
    wkg, wvg = f_kv.result
    kvn, qn = _rms_fwd(h2, [kv_g, q_g], "rms_kvq")
    f_qo, s_g1a = forward(s_qo), send_half(ffn_w_gate[1], 0)
    kk = _mm_rows(kvn, wkg, "k_mm", out_dtype=BF16, branches=True, jobs=[f_qo, s_g1a])
    wqg, wog = f_qo.result
    s_g1b = send_half(ffn_w_gate[1], 1, s_g1a)
    vv = _mm_rows(kvn, wvg, "v_mm", out_dtype=BF16, branches=True, jobs=[s_g1b])
    f_g1, s_u1a = forward(s_g1b), send_half(ffn_w_up[1], 0)
    qq = _mm_rows(qn, wqg, "q_mm", out_dtype=BF16, branches=True, jobs=[f_g1, s_u1a])
    (wg1,) = f_g1.result
    branch = {dil: (qq[i], kk[i], vv[i]) for i, dil in enumerate(BRANCH_DILATIONS)}
    s_u1b = send_half(ffn_w_up[1], 1, s_u1a)
    o1, l1 = _attn_fwd(*branch[1], 1, jobs=[s_u1b])
    f_u1, s_d1a = forward(s_u1b), send_half(ffn_w_down[1], 0)
    o4, l4 = _attn_fwd(*branch[4], 4, jobs=[f_u1, s_d1a])
    (wu1,) = f_u1.result
    s_d1b = send_half(ffn_w_down[1], 1, s_d1a)
    o16, l16 = _attn_fwd(*branch[16], 16, jobs=[s_d1b])
    atts, lses = _attn_merge([o1, o4, o16], [l1, l4, l16])
    att = atts[0]
    atts, lses = [att[None]] + atts[1:], [lses[0][None]] + lses[1:]
    f_d1 = forward(s_d1b)
    h3 = _mm_rows(att, wog, "wo_mm", res=h2, jobs=[f_d1])
    (wd1,) = f_d1.result
    (n2b,) = _rms_fwd(h3, [f_g[1]], "rms_f1")
    gate1, up1, act1 = _swiglu_mm(n2b, wg1, wu1, "swiglu_mm1")
    h4 = _down_mm(act1, wd1, h3, "down_mm1")

    flat = lambda g: g.reshape(ND, -1, g.shape[-1])
    chip_sums, cross = {}, {}

    def to_sibling(**grads):
        job = _job_scatter_sibling([flat(g) for g in grads.values()])
        job.names = list(grads)
        return job

    def add_up(job):
        for n, g, r in zip(job.names, job.ins, job.result):
            chip_sums[n] = _rs_add(g, r, c_idx, f"rs_add_{n}")

    def to_chips(*names):
        job = _job_scatter_cross([chip_sums[n] for n in names])
        job.names = names
        return job

    def landed(job):
        cross.update(zip(job.names, job.result))

    dh4, dh4b, d_fin, loss_row = _final_loss(h4, target, fin_g)
    dgate1, dup1 = _dact_mm(dh4b, wd1, gate1, up1, "dact_mm1")
    g_wd1 = _dwd_mm(act1, dh4b, "dwd_mm1")
    j1 = to_sibling(wd1=g_wd1)
    g_wg1, g_wu1 = _dwgu_mm(n2b, dgate1, dup1, "dwgu_mm1", jobs=[j1])
    add_up(j1)
    j2, j3 = to_chips("wd1"), to_sibling(wg1=g_wg1, wu1=g_wu1)
    dn2b = _dn_ffn_mm(dgate1, dup1, wg1, wu1, "dn_ffn_mm1", jobs=[j2, j3])
    landed(j2)
    add_up(j3)
    dh3, dh3b, d_f1 = _rms_bwd(h3, [(f_g[1], dn2b)], dh4, "rms_f1_bwd")
    g_wo = _dw_rows_mm(att, dh3b, "dwo_mm")
    j4 = to_sibling(wo=g_wo)
    datt = _mm_rows_t([(dh3b, wog)], "datt_mm", BF16, branches=True, jobs=[j4])
    add_up(j4)
    riders = {1: to_chips("wg1"), 4: to_chips("wu1"), 16: to_chips("wo")}
    dqs, dks, dvs = [], [], []
    for i, dil in enumerate(BRANCH_DILATIONS):
        qb, kb, vb = branch[dil]
        dq_b, dk_b, dv_b = _attn_bwd(qb, kb, vb, datt[i], atts[i], lses[i], dil, jobs=[riders[dil]])
        landed(riders[dil])
        dqs.append(dq_b)
        dks.append(dk_b)
        dvs.append(dv_b)
    dq, dk, dv = _sum_cast(dqs, "dq_sum"), _sum_cast(dks, "dk_sum"), _sum_cast(dvs, "dv_sum")
    g_wq = _dw_rows_mm(qn, dq, "dwq_mm")
    g_wk = _dw_rows_mm(kvn, dk, "dwk_mm")
    g_wv = _dw_rows_mm(kvn, dv, "dwv_mm")
    j5 = to_sibling(wq=g_wq, wk=g_wk, wv=g_wv)
    dqn = _mm_rows_t([(dq, wqg)], "dqn_mm", F32, jobs=[j5])
    add_up(j5)
    j6 = to_chips("wq", "wk")
    dkvn = _mm_rows_t([(dk, wkg), (dv, wvg)], "dkvn_mm", F32, jobs=[j6])
    landed(j6)
    dh2, dh2b, d_q, d_kv = _rms_bwd(h2, [(q_g, dqn), (kv_g, dkvn)], dh3, "rms_kvq_bwd")
    j7 = to_chips("wv")
    dgate0, dup0 = _dact_mm(dh2b, wd0, gate0, up0, "dact_mm0", jobs=[j7])
    landed(j7)
    g_wd0 = _dwd_mm(act0, dh2b, "dwd_mm0")
    j8 = to_sibling(wd0=g_wd0)
    g_wg0, g_wu0 = _dwgu_mm(n2a, dgate0, dup0, "dwgu_mm0", jobs=[j8])
    add_up(j8)
    j9, j10 = to_chips("wd0"), to_sibling(wg0=g_wg0, wu0=g_wu0)
    dn2a = _dn_ffn_mm(dgate0, dup0, wg0, wu0, "dn_ffn_mm0", jobs=[j9, j10])
    landed(j9)
    add_up(j10)
    dh1, dh1b, d_f0, d_b2 = _rms_bwd(h1, [(f_g[0], dn2a)], dh2, "rms_f0_bwd", colsum=True)
    g_w2 = _dw_rows_mm(sw, dh1b, "dw2_mm")
    j11 = to_sibling(w2=g_w2)
    dsw = _mm_rows_t([(dh1b, w2g)], "dsw_mm", F32, jobs=[j11])
    add_up(j11)
    dcv, d_lng, d_lnb = _ln_bwd(dsw, cv, lng, lnb)
    j12 = to_chips("wg0", "wu0")
    du, d_dw, d_dwb, d_b1 = _conv_bwd(dcv, glu, ua, ug, dw, jobs=[j12])
    landed(j12)
    j13 = to_chips("w2")
    g_w1 = _dw1_mm(n1, du, jobs=[j13])
    landed(j13)
    j14 = to_sibling(w1=g_w1)
    _comm_call([j14], "rs_w1_sibling")
    add_up(j14)
    j15 = to_chips("w1")
    dn1 = _dn1_mm(du, w1g, 0, 1, jobs=[j15])
    landed(j15)
    dx, _, d_a = _rms_bwd(h0, [(a_g, dn1)], dh1, "rms_a_bwd")

    small_g = [d_a, d_b1, d_dw[:CONV_W], d_dwb, d_lng, d_lnb, d_b2, d_kv, d_q, d_f0, d_f1, d_fin, loss_row]
    gp, gp_spans = _pack_rows(small_g, d)
    (gpg,) = _all_gather([gp], "gather_small_grads")

    two = lambda t: t.reshape(-1, t.shape[-1])

    def adam(w, m, v, names, tag, swapped=False):
        view = (lambda t: jnp.swapaxes(t, 1, 2)) if swapped else (lambda t: t)
        w, m, v = view(w), view(m), view(v)
        res = None
        for part, n in enumerate(names):
            res = _adamw_big(two(w), two(m), two(v), chip_sums[n], cross[n], q_idx, f"adamw_{tag}{part}", part, res)
        return [view(t.reshape(w.shape)) for t in res]

    big_out = [
        adam(conv_w1, m_conv_w1, v_conv_w1, ["w1"], "w1"), adam(conv_w2, m_conv_w2, v_conv_w2, ["w2"], "w2"),
        adam(w_k, m_w_k, v_w_k, ["wk"], "wk"), adam(w_v, m_w_v, v_w_v, ["wv"], "wv"),
        adam(w_q, m_w_q, v_w_q, ["wq"], "wq"), adam(w_o, m_w_o, v_w_o, ["wo"], "wo"),
        adam(ffn_w_gate, m_ffn_w_gate, v_ffn_w_gate, ["wg0", "wg1"], "wg", swapped=True),
        adam(ffn_w_up, m_ffn_w_up, v_ffn_w_up, ["wu0", "wu1"], "wu", swapped=True),
        adam(ffn_w_down, m_ffn_w_down, v_ffn_w_down, ["wd0", "wd1"], "wd")]

    gsum = _sum_devices(gpg, "sum_small_grads").reshape(-1)

    def gfull(i):
        at, size = gp_spans[i]
        return gsum[at:at + size]

    def shard_of(vec, rows):
        return lax.dynamic_slice_in_dim(vec.reshape(rows, -1), me * (vec.size // rows // ND), vec.size // rows // ND, axis=1)

    loss = gfull(12)[0]
    small_grads = [
        shard_of(gfull(0), 1), shard_of(gfull(1), 1), shard_of(gfull(2), CONV_W)[None], shard_of(gfull(3), 1),
        shard_of(gfull(4), 1), shard_of(gfull(5), 1), shard_of(gfull(6), 1),
        gfull(7), gfull(8)[None], jnp.stack([gfull(9), gfull(10)]), gfull(11)]
    small_w = [a_norm_g, conv_b1, conv_dw, conv_dw_b, conv_ln_g, conv_ln_b, conv_b2, kv_norm_g, b_norm_g, ffn_norm_g, final_norm_g]
    small_m = [m_a_norm_g, m_conv_b1, m_conv_dw, m_conv_dw_b, m_conv_ln_g, m_conv_ln_b, m_conv_b2, m_kv_norm_g, m_b_norm_g, m_ffn_norm_g, m_final_norm_g]
    small_v = [v_a_norm_g, v_conv_b1, v_conv_dw, v_conv_dw_b, v_conv_ln_g, v_conv_ln_b, v_conv_b2, v_kv_norm_g, v_b_norm_g, v_ffn_norm_g, v_final_norm_g]
    small_grads = [g.reshape(w.shape) for g, w in zip(small_grads, small_w)]
    wp, spans = _pack_rows(small_w, LANES)
    gpk, _ = _pack_rows(small_grads, LANES)
    mp, _ = _pack_rows(small_m, LANES)
    vp, _ = _pack_rows(small_v, LANES)
    dp, mnp, vnp = _adamw_small(wp, gpk, mp, vp, "adamw_small")

    def unpack(packed):
        flat = packed.reshape(-1)
        return [flat[at:at + size].reshape(w.shape) for (at, size), w in zip(spans, small_w)]

    small_out = list(zip(small_grads, unpack(dp), unpack(mnp), unpack(vnp)))

    order = ["a_norm_g", "conv_w1", "conv_b1", "conv_dw", "conv_dw_b", "conv_ln_g", "conv_ln_b", "conv_w2", "conv_b2",
             "kv_norm_g", "w_k", "w_v", "b_norm_g", "w_q", "w_o", "ffn_norm_g", "ffn_w_gate", "ffn_w_up", "ffn_w_down",
             "final_norm_g"]
    big_names = ["conv_w1", "conv_w2", "w_k", "w_v", "w_q", "w_o", "ffn_w_gate", "ffn_w_up", "ffn_w_down"]
    small_names = ["a_norm_g", "conv_b1", "conv_dw", "conv_dw_b", "conv_ln_g", "conv_ln_b", "conv_b2", "kv_norm_g",
                   "b_norm_g", "ffn_norm_g", "final_norm_g"]
    table = {n: big_out[i] for i, n in enumerate(big_names)}
    table.update({n: small_out[i] for i, n in enumerate(small_names)})
    result = [loss, dx[None]]
    for kind in range(4):
        result += [table[n][kind] for n in order]
    return tuple(result)
```

```python
import functools

import jax
import jax.numpy as jnp
from jax import lax
from jax.experimental import pallas as pl
from jax.experimental.pallas import tpu as pltpu

ND = 8
LANES = 128
HEAD = 128
BLK = 128
BRANCH_DILATIONS = (1, 4, 16)
CONV_W = 31
CONV_PAD = 32
RMS_EPS = 1e-6
LN_EPS = 1e-5
LR, B1, B2, ADAM_EPS, WD, STEP = 0.001, 0.9, 0.999, 1e-08, 0.01, 10
VMEM_LIMIT = 56 * 1024 * 1024
RESIDENT_BYTES = 4 * 1024 * 1024

F32, BF16 = jnp.float32, jnp.bfloat16
SDS = jax.ShapeDtypeStruct
MESH = pl.DeviceIdType.MESH
ANY = pl.BlockSpec(memory_space=pl.ANY)

NN = (((1,), (0,)), ((), ()))
NT = (((1,), (1,)), ((), ()))
TN = (((0,), (0,)), ((), ()))


def _dot(a, b, dims):
    return lax.dot_general(a, b, dims, preferred_element_type=F32)


def _cp(*sem):
    return pltpu.CompilerParams(dimension_semantics=sem, vmem_limit_bytes=VMEM_LIMIT)


def _slot(dev):
    return 4 * (dev % 2) + dev // 2


def _sigmoid(v):
    return 1.0 / (1.0 + jnp.exp(-v))


class _Job:
    def __init__(self, ins, out_shapes, alias, nsem, nlocal, make):
        self.ins, self.out_shapes, self.alias = list(ins), list(out_shapes), dict(alias)
        self.nsem, self.nlocal, self.make = nsem, nlocal, make
        self.result = None


def _coords():
    return lax.axis_index("x"), lax.axis_index("y"), lax.axis_index("c")


def _remote(src, dst, send, recv, k, to):
    return pltpu.make_async_remote_copy(src_ref=src, dst_ref=dst, send_sem=send.at[k], recv_sem=recv.at[k],
                                        device_id=to, device_id_type=MESH)


def _job_gather_send(shards):
    n = len(shards)

    def make(ins, outs, send, recv, local):
        x, y, c = _coords()
        targets = [(x, y, 1 - c), (1 - x, y, c), (x, 1 - y, c), (1 - x, 1 - y, c)]
        cps = []
        for a in range(n):
            dst = outs[a].at[4 * x + 2 * y + c]
            cps.append(pltpu.make_async_copy(ins[a], dst, local.at[a]))
            cps += [_remote(ins[a], dst, send, recv, 4 * a + k, t) for k, t in enumerate(targets)]
        return cps

    return _Job(shards, [SDS((ND,) + s.shape, s.dtype) for s in shards], {}, 4 * n, n, make)


def _job_gather_send_rows(shard, part, nparts, prev=None):
    rows = shard.shape[0] // nparts

    def make(ins, outs, send, recv, local):
        x, y, c = _coords()
        targets = [(x, y, 1 - c), (1 - x, y, c), (x, 1 - y, c), (1 - x, 1 - y, c)]
        src = ins[0].at[pl.ds(part * rows, rows)]
        dst = outs[0].at[4 * x + 2 * y + c].at[pl.ds(part * rows, rows)]
        return [pltpu.make_async_copy(src, dst, local.at[0])] + [
            _remote(src, dst, send, recv, k, t) for k, t in enumerate(targets)]

    ins = [shard] if prev is None else [shard, prev]
    return _Job(ins, [SDS((ND,) + shard.shape, shard.dtype)], {} if prev is None else {1: 0}, 4, 1, make)


def _job_gather_forward(gathered):
    n = len(gathered)

    def make(ins, outs, send, recv, local):
        x, y, c = _coords()
        cps = []
        for a in range(n):
            for k, (px, py) in enumerate([(1 - x, y), (x, 1 - y), (1 - x, 1 - y)]):
                blk = outs[a].at[4 * px + 2 * py + c]
                cps.append(_remote(blk, blk, send, recv, 3 * a + k, (x, y, 1 - c)))
        return cps

    return _Job(gathered, [SDS(g.shape, g.dtype) for g in gathered], {i: i for i in range(n)}, 3 * n, 0, make)


def _job_scatter_sibling(grads):
    n = len(grads)

    def make(ins, outs, send, recv, local):
        x, y, c = _coords()
        return [_remote(ins[a].at[pl.ds(4 * (1 - c), 4)], outs[a], send, recv, a, (x, y, 1 - c)) for a in range(n)]

    return _Job(grads, [SDS((4,) + g.shape[1:], g.dtype) for g in grads], {}, n, 0, make)


def _job_scatter_cross(sums):
    n = len(sums)

    def make(ins, outs, send, recv, local):
        x, y, c = _coords()
        chips = [(1 - x, y), (x, 1 - y), (1 - x, 1 - y)]
        return [_remote(ins[a].at[2 * px + py], outs[a].at[k], send, recv, 3 * a + k, (px, py, c))
                for a in range(n) for k, (px, py) in enumerate(chips)]

    return _Job(sums, [SDS((3,) + t.shape[1:], t.dtype) for t in sums], {}, 3 * n, 0, make)


def _pc(body, *, name, grid, in_specs, out_specs, out_shape, args, scratch=(), sem=(), alias=None, jobs=()):
    jobs = list(jobs)
    n_in, n_out, n_scr = len(in_specs), len(out_shape), len(scratch)
    aliases = dict(alias or {})
    job_args, job_shapes, job_scratch = [], [], []
    for j in jobs:
        for src, dst in j.alias.items():
            aliases[n_in + len(job_args) + src] = n_out + len(job_shapes) + dst
        job_args += j.ins
        job_shapes += j.out_shapes
        job_scratch += [pltpu.SemaphoreType.DMA((j.nsem,)), pltpu.SemaphoreType.DMA((j.nsem,)),
                        pltpu.SemaphoreType.DMA((max(j.nlocal, 1),))]

    def wrapped(*refs):
        ins = refs[:n_in]
        p = n_in + len(job_args)
        outs = refs[p:p + n_out]
        p += n_out + len(job_shapes)
        scr = refs[p:p + n_scr]
        sems = refs[p + n_scr:]
        copies = []
        pi, po = n_in, n_in + len(job_args) + n_out
        for k, j in enumerate(jobs):
            copies += j.make(refs[pi:pi + len(j.ins)], refs[po:po + len(j.out_shapes)], *sems[3 * k:3 * k + 3])
            pi += len(j.ins)
            po += len(j.out_shapes)
        gridded = bool(copies) and bool(grid)
        if gridded:
            ids = [pl.program_id(i) for i in range(len(grid))]
            first = functools.reduce(jnp.logical_and, [i == 0 for i in ids])
            last = functools.reduce(jnp.logical_and, [i == g - 1 for i, g in zip(ids, grid)])

            @pl.when(first)
            def _():
                for cp in copies:
                    cp.start()
        else:
            for cp in copies:
                cp.start()
        body(*ins, *outs, *scr)
        if gridded:
            @pl.when(last)
            def _():
                for cp in copies:
                    cp.wait()
        else:
            for cp in copies:
                cp.wait()

    kw = dict(grid=grid) if grid else {}
    semantics = ["arbitrary"] * len(grid) if jobs else list(sem)
    res = pl.pallas_call(
        wrapped, name=name, in_specs=list(in_specs) + [ANY] * len(job_args),
        out_specs=list(out_specs) + [ANY] * len(job_shapes), out_shape=list(out_shape) + job_shapes,
        scratch_shapes=list(scratch) + job_scratch, input_output_aliases=aliases,
        compiler_params=_cp(*semantics), **kw)(*args, *job_args)
    p = n_out
    for j in jobs:
        j.result = list(res[p:p + len(j.out_shapes)])
        p += len(j.out_shapes)
    return list(res[:n_out])


def _comm_call(jobs, name):
    _pc(lambda: None, name=name, grid=(), in_specs=[], out_specs=[], out_shape=[], args=[], jobs=jobs)


def _all_gather(arrs, name):
    n = len(arrs)

    def body(*refs):
        ins, outs = refs[:n], refs[n:2 * n]
        send_sems, recv_sems, local_sems = refs[2 * n:]
        x, y, c = lax.axis_index("x"), lax.axis_index("y"), lax.axis_index("c")
        me, sib = (x, y, c), (x, y, 1 - c)
        chips = [(1 - x, y), (x, 1 - y), (1 - x, 1 - y)]

        def copy(a, k, block, to, src=None):
            dst = outs[a].at[4 * block[0] + 2 * block[1] + block[2]]
            return pltpu.make_async_remote_copy(
                src_ref=dst if src is None else src, dst_ref=dst,
                send_sem=send_sems.at[7 * a + k], recv_sem=recv_sems.at[7 * a + k],
                device_id=to, device_id_type=MESH)

        mine = [pltpu.make_async_copy(ins[a], outs[a].at[4 * x + 2 * y + c], local_sems.at[a]) for a in range(n)]
        for cp in mine:
            cp.start()
        first = []
        for a in range(n):
            first.append(copy(a, 0, me, sib, src=ins[a]))
            first += [copy(a, 1 + j, me, (*chip, c), src=ins[a]) for j, chip in enumerate(chips)]
        for cp in first:
            cp.start()
        passed = []
        for a in range(n):
            for j, chip in enumerate(chips):
                copy(a, 1 + j, (*chip, c), me).wait_recv()
                fwd = copy(a, 4 + j, (*chip, c), sib)
                fwd.start()
                passed.append(fwd)
        for a in range(n):
            copy(a, 0, sib, me).wait_recv()
            for j, chip in enumerate(chips):
                copy(a, 4 + j, (*chip, 1 - c), me).wait_recv()
        for cp in first + passed:
            cp.wait_send()
        for cp in mine:
            cp.wait()

    return pl.pallas_call(
        body, name=name,
        out_shape=[SDS((ND,) + a.shape, a.dtype) for a in arrs],
        in_specs=[ANY] * n, out_specs=[ANY] * n,
        scratch_shapes=[pltpu.SemaphoreType.DMA((7 * n,)), pltpu.SemaphoreType.DMA((7 * n,)),
                        pltpu.SemaphoreType.DMA((n,))],
    )(*arrs)


ELEMENTWISE_TILE_BYTES = 3 * 512 * 1024


def _row_tile(rows, cols):
    fits = [t for t in range(16, rows + 1, 16) if rows % t == 0 and 4 * t * cols <= ELEMENTWISE_TILE_BYTES]
    return max(fits)


def _rs_add(g, r1, c_idx, name):
    _, rows, cols = g.shape

    def body(c_ref, g_ref, r_ref, o_ref):
        o_ref[...] = (g_ref[...].astype(F32) + r_ref[...].astype(F32)).astype(o_ref.dtype)

    return pl.pallas_call(
        body, name=name,
        grid_spec=pltpu.PrefetchScalarGridSpec(
            num_scalar_prefetch=1, grid=(4,),
            in_specs=[pl.BlockSpec((1, rows, cols), lambda q, c: (4 * c[0] + q, 0, 0)),
                      pl.BlockSpec((1, rows, cols), lambda q, c: (q, 0, 0))],
            out_specs=pl.BlockSpec((1, rows, cols), lambda q, c: (q, 0, 0))),
        out_shape=SDS((4, rows, cols), g.dtype),
        compiler_params=_cp("parallel"),
    )(c_idx, g, r1)


def _adam_math(w, g, m, v):
    m = B1 * m + (1.0 - B1) * g
    v = B2 * v + (1.0 - B2) * (g * g)
    m_hat = m / (1.0 - B1 ** STEP)
    v_hat = v / (1.0 - B2 ** STEP)
    delta = -LR * (m_hat / (jnp.sqrt(v_hat) + ADAM_EPS) + WD * w)
    return delta, m, v


def _adamw_big(w, m, v, t, r2, q_idx, name, part=0, prev=None):
    _, rows, cols = t.shape
    tr = _row_tile(rows, cols)
    nblk = rows // tr

    def body(q_ref, w_ref, m_ref, v_ref, t_ref, r_ref, *outs):
        g_out, d_out, m_out, v_out = outs[-4:]
        g = t_ref[0].astype(F32)
        for k in range(3):
            g = g + r_ref[k].astype(F32)
        d, mn, vn = _adam_math(w_ref[...], g, m_ref[...], v_ref[...])
        g_out[...], d_out[...], m_out[...], v_out[...] = g, d, mn, vn

    blk = pl.BlockSpec((tr, cols), lambda i, q: (part * nblk + i, 0))
    specs = [blk, blk, blk, pl.BlockSpec((1, tr, cols), lambda i, q: (q[0], i, 0)),
             pl.BlockSpec((3, tr, cols), lambda i, q: (0, i, 0))]
    ins = [q_idx, w, m, v, t, r2]
    alias = {}
    if prev is not None:
        specs += [ANY] * 4
        alias = {6 + k: k for k in range(4)}
        ins += list(prev)
    return pl.pallas_call(
        body, name=name,
        grid_spec=pltpu.PrefetchScalarGridSpec(num_scalar_prefetch=1, grid=(nblk,), in_specs=specs, out_specs=[blk] * 4),
        out_shape=[SDS(w.shape, F32)] * 4, input_output_aliases=alias,
        compiler_params=_cp("parallel"))(*ins)


def _sum_devices(g, name):
    _, rows, cols = g.shape

    def body(g_ref, o_ref):
        acc = g_ref[0]
        for k in range(1, ND):
            acc = acc + g_ref[k]
        o_ref[...] = acc

    return pl.pallas_call(body, name=name, out_shape=SDS((rows, cols), F32))(g)


def _adamw_small(w, g, m, v, name):
    def body(w_ref, g_ref, m_ref, v_ref, d_out, m_out, v_out):
        d, mn, vn = _adam_math(w_ref[...], g_ref[...], m_ref[...], v_ref[...])
        d_out[...], m_out[...], v_out[...] = d, mn, vn

    return pl.pallas_call(body, name=name, out_shape=[SDS(w.shape, F32)] * 3)(w, g, m, v)


ROWS = 256


def _rms_stats(x):
    r = lax.rsqrt(jnp.mean(x * x, axis=-1, keepdims=True) + RMS_EPS)
    return x * r, r


def _rms_fwd(x, gains, name, jobs=()):
    s, d = x.shape
    n = len(gains)

    def body(x_ref, *refs):
        xh, _ = _rms_stats(x_ref[...])
        for g_ref, o_ref in zip(refs[:n], refs[n:]):
            o_ref[...] = (xh * g_ref[...]).astype(BF16)

    row = pl.BlockSpec((ROWS, d), lambda i: (i, 0))
    vec = pl.BlockSpec((1, d), lambda i: (0, 0))
    return _pc(body, name=name, grid=(s // ROWS,), in_specs=[row] + [vec] * n, out_specs=[row] * n,
               out_shape=[SDS((s, d), BF16)] * n, sem=("parallel",), args=(x, *gains), jobs=jobs)


def _rms_bwd_rows(xh, r, gain, dy):
    u = dy * gain
    return r * (u - xh * jnp.mean(u * xh, axis=-1, keepdims=True))


def _rms_bwd(x, pairs, dres, name, colsum=False):
    s, d = x.shape
    n = len(pairs)

    def body(x_ref, dres_ref, *refs):
        g_refs, dy_refs = refs[:n], refs[n:2 * n]
        dx_ref, dxb_ref = refs[2 * n], refs[2 * n + 1]
        dg_refs = refs[2 * n + 2:2 * n + 2 + n]
        cs_ref = refs[-1] if colsum else None
        first = pl.program_id(0) == 0
        xh, r = _rms_stats(x_ref[...])
        dx = dres_ref[...]
        for g_ref, dy_ref, dg_ref in zip(g_refs, dy_refs, dg_refs):
            dy = dy_ref[...]
            dx = dx + _rms_bwd_rows(xh, r, g_ref[...], dy)

            @pl.when(first)
            def _():
                dg_ref[...] = jnp.zeros_like(dg_ref)
            dg_ref[...] += jnp.sum(dy * xh, axis=0, keepdims=True)
        dx_ref[...] = dx
        dxb_ref[...] = dx.astype(BF16)
        if colsum:
            @pl.when(first)
            def _():
                cs_ref[...] = jnp.zeros_like(cs_ref)
            cs_ref[...] += jnp.sum(dx, axis=0, keepdims=True)

    row = pl.BlockSpec((ROWS, d), lambda i: (i, 0))
    vec = pl.BlockSpec((1, d), lambda i: (0, 0))
    nvec = n + (1 if colsum else 0)
    outs = pl.pallas_call(
        body, name=name, grid=(s // ROWS,),
        in_specs=[row, row] + [vec] * n + [row] * n,
        out_specs=[row, row] + [vec] * nvec,
        out_shape=[SDS((s, d), F32), SDS((s, d), BF16)] + [SDS((1, d), F32)] * nvec,
        compiler_params=_cp("arbitrary"),
    )(x, dres, *[p[0] for p in pairs], *[p[1] for p in pairs])
    return outs


def _final_loss(h, target, gain):
    s, d = h.shape

    def body(h_ref, t_ref, g_ref, dh_ref, dhb_ref, dg_ref, loss_ref):
        first = pl.program_id(0) == 0
        xh, r = _rms_stats(h_ref[...])
        gain_v = g_ref[...]
        e = xh * gain_v - t_ref[...]
        dy = e * (1.0 / d)
        dx = _rms_bwd_rows(xh, r, gain_v, dy)
        dh_ref[...] = dx
        dhb_ref[...] = dx.astype(BF16)

        @pl.when(first)
        def _():
            dg_ref[...] = jnp.zeros_like(dg_ref)
            loss_ref[...] = jnp.zeros_like(loss_ref)
        dg_ref[...] += jnp.sum(dy * xh, axis=0, keepdims=True)
        loss_ref[...] += jnp.full((1, LANES), 0.5 / d, F32) * jnp.sum(e * e)

    row = pl.BlockSpec((ROWS, d), lambda i: (i, 0))
    vec = pl.BlockSpec((1, d), lambda i: (0, 0))
    return pl.pallas_call(
        body, name="final_loss", grid=(s // ROWS,),
        in_specs=[row, row, vec], out_specs=[row, row, vec, pl.BlockSpec((1, LANES), lambda i: (0, 0))],
        out_shape=[SDS((s, d), F32), SDS((s, d), BF16), SDS((1, d), F32), SDS((1, LANES), F32)],
        compiler_params=_cp("arbitrary"))(h, target, gain)


CT = 128


def _ln_stats(cv):
    mu = jnp.mean(cv, axis=-1, keepdims=True)
    xc = cv - mu
    rstd = lax.rsqrt(jnp.mean(xc * xc, axis=-1, keepdims=True) + LN_EPS)
    return xc * rstd, rstd


def _conv_fwd(glu, dw, dwb, lng, lnb, jobs=()):
    s, d = glu.shape
    hb = CT // CONV_PAD

    def body(x_ref, halo_ref, dw_ref, dwb_ref, lng_ref, lnb_ref, c_ref, s_ref):
        keep = (pl.program_id(0) > 0).astype(F32)

        def chunk(ci, carry):
            ls = pl.ds(pl.multiple_of(ci * LANES, LANES), LANES)
            xf = jnp.concatenate([halo_ref[:, ls] * keep, x_ref[:, ls]], axis=0)
            acc = jnp.zeros((CT, LANES), F32)
            for k in range(CONV_W):
                sh = CONV_W - 1 - k
                xs = pltpu.roll(xf, sh, 0) if sh else xf
                acc = acc + dw_ref[pl.ds(k, 1), ls] * xs[CONV_PAD:]
            c_ref[:, ls] = acc + dwb_ref[:, ls]
            return carry

        lax.fori_loop(0, d // LANES, chunk, 0)
        xh, _ = _ln_stats(c_ref[...])
        yv = xh * lng_ref[...] + lnb_ref[...]
        s_ref[...] = (yv * _sigmoid(yv)).astype(BF16)

    row = pl.BlockSpec((CT, d), lambda i: (i, 0))
    halo = pl.BlockSpec((CONV_PAD, d), lambda i: (jnp.maximum(i * hb - 1, 0), 0))
    vec = pl.BlockSpec((1, d), lambda i: (0, 0))
    taps = pl.BlockSpec((CONV_PAD, d), lambda i: (0, 0))
    return _pc(
        body, name="conv_fwd", grid=(s // CT,),
        in_specs=[row, halo, taps, vec, vec, vec], out_specs=[row, row],
        out_shape=[SDS((s, d), F32), SDS((s, d), BF16)], sem=("parallel",),
        args=(glu, glu, dw, dwb, lng, lnb), jobs=jobs)


def _ln_bwd(ds, cv, lng, lnb):
    s, d = cv.shape

    def body(ds_ref, c_ref, g_ref, b_ref, dc_ref, dg_ref, db_ref):
        first = pl.program_id(0) == 0
        xh, rstd = _ln_stats(c_ref[...])
        gv = g_ref[...]
        yv = xh * gv + b_ref[...]
        sg = _sigmoid(yv)
        dln = ds_ref[...] * (sg * (1.0 + yv * (1.0 - sg)))
        dxh = dln * gv
        dc_ref[...] = rstd * (dxh - jnp.mean(dxh, axis=-1, keepdims=True)
                              - xh * jnp.mean(dxh * xh, axis=-1, keepdims=True))

        @pl.when(first)
        def _():
            dg_ref[...] = jnp.zeros_like(dg_ref)
            db_ref[...] = jnp.zeros_like(db_ref)
        dg_ref[...] += jnp.sum(dln * xh, axis=0, keepdims=True)
        db_ref[...] += jnp.sum(dln, axis=0, keepdims=True)

    row = pl.BlockSpec((ROWS, d), lambda i: (i, 0))
    vec = pl.BlockSpec((1, d), lambda i: (0, 0))
    return pl.pallas_call(
        body, name="ln_bwd", grid=(s // ROWS,), in_specs=[row, row, vec, vec], out_specs=[row, vec, vec],
        out_shape=[SDS((s, d), F32), SDS((1, d), F32), SDS((1, d), F32)],
        compiler_params=_cp("arbitrary"))(ds, cv, lng, lnb)


def _conv_bwd(dc, glu, ua, ug, dw, jobs=()):
    s, d = dc.shape
    hb = CT // CONV_PAD
    nsteps = s // CT
    full = CT + CONV_PAD

    def body(dc_ref, dcn_ref, x_ref, xp_ref, ua_ref, ug_ref, dw_ref, du_ref, ddw_ref, ddwb_ref, db1_ref, part):
        i = pl.program_id(0)
        keep_prev = (i > 0).astype(F32)
        keep_next = (i < nsteps - 1).astype(F32)

        @pl.when(i == 0)
        def _():
            part[...] = jnp.zeros_like(part)
            ddwb_ref[...] = jnp.zeros_like(ddwb_ref)
            db1_ref[...] = jnp.zeros_like(db1_ref)

        def chunk(ci, carry):
            off = pl.multiple_of(ci * LANES, LANES)
            ls = pl.ds(off, LANES)
            ls2 = pl.ds(pl.multiple_of(d + ci * LANES, LANES), LANES)
            dcc = dc_ref[:, ls]
            dcf = jnp.concatenate([dcc, dcn_ref[:, ls] * keep_next], axis=0)
            xf = jnp.concatenate([xp_ref[:, ls] * keep_prev, x_ref[:, ls]], axis=0)
            dglu = jnp.zeros((CT, LANES), F32)
            for k in range(CONV_W):
                sh = CONV_W - 1 - k
                dshift = pltpu.roll(dcf, full - sh, 0) if sh else dcf
                dglu = dglu + dw_ref[pl.ds(k, 1), ls] * dshift[:CT]
                xs = pltpu.roll(xf, sh, 0) if sh else xf
                part[pl.ds(8 * k, 8), ls] += jnp.sum((dcc * xs[CONV_PAD:]).reshape(CT // 8, 8, LANES), axis=0)
            ddwb_ref[:, ls] += jnp.sum(dcc, axis=0, keepdims=True)
            av, gv = ua_ref[:, ls], ug_ref[:, ls]
            sg = _sigmoid(gv)
            da = dglu * sg
            dgt = dglu * av * sg * (1.0 - sg)
            du_ref[:, ls] = da.astype(BF16)
            du_ref[:, ls2] = dgt.astype(BF16)
            db1_ref[:, ls] += jnp.sum(da, axis=0, keepdims=True)
            db1_ref[:, ls2] += jnp.sum(dgt, axis=0, keepdims=True)
            return carry

        lax.fori_loop(0, d // LANES, chunk, 0)

        @pl.when(i == nsteps - 1)
        def _():
            ddw_ref[...] = jnp.sum(part[...].reshape(CONV_PAD, 8, d), axis=1)

    row = pl.BlockSpec((CT, d), lambda i: (i, 0))
    prev = pl.BlockSpec((CONV_PAD, d), lambda i: (jnp.maximum(i * hb - 1, 0), 0))
    nxt = pl.BlockSpec((CONV_PAD, d), lambda i: (jnp.minimum((i + 1) * hb, s // CONV_PAD - 1), 0))
    taps = pl.BlockSpec((CONV_PAD, d), lambda i: (0, 0))
    return _pc(
        body, name="conv_bwd", grid=(nsteps,),
        in_specs=[row, nxt, row, prev, row, row, taps],
        out_specs=[pl.BlockSpec((CT, 2 * d), lambda i: (i, 0)), taps, pl.BlockSpec((1, d), lambda i: (0, 0)),
                   pl.BlockSpec((1, 2 * d), lambda i: (0, 0))],
        out_shape=[SDS((s, 2 * d), BF16), SDS((CONV_PAD, d), F32), SDS((1, d), F32), SDS((1, 2 * d), F32)],
        scratch=[pltpu.VMEM((8 * CONV_PAD, d), F32)],
        sem=("arbitrary",), args=(dc, dc, glu, glu, ua, ug, dw), jobs=jobs)


TM = 1024
TS = 1024


def _glu_mm(n1, w1g, b1, jobs=()):
    s, d = n1.shape
    cw = w1g.shape[2]
    half = ND // 2

    def body(a_ref, wa_ref, wg_ref, ba_ref, bg_ref, ua_ref, ug_ref, glu_ref):
        a = a_ref[...]
        ua = _dot(a, wa_ref[0], NN) + ba_ref[...]
        ug = _dot(a, wg_ref[0], NN) + bg_ref[...]
        ua_ref[...], ug_ref[...] = ua, ug
        glu_ref[...] = ua * _sigmoid(ug)

    out = pl.BlockSpec((TM, cw), lambda m, i: (m, i))
    return _pc(
        body, name="glu_mm", grid=(s // TM, half),
        in_specs=[pl.BlockSpec((TM, d), lambda m, i: (m, 0)),
                  pl.BlockSpec((1, d, cw), lambda m, i: (i, 0, 0)),
                  pl.BlockSpec((1, d, cw), lambda m, i: (i + half, 0, 0)),
                  pl.BlockSpec((1, cw), lambda m, i: (0, i)),
                  pl.BlockSpec((1, cw), lambda m, i: (0, i + half))],
        out_specs=[out, out, out], out_shape=[SDS((s, d), F32)] * 3,
        sem=("parallel", "arbitrary"), args=(n1, w1g, w1g, b1, b1), jobs=jobs)


def _mm_rows(a, wg, name, res=None, bias=None, out_dtype=F32, tn=512, branches=False, jobs=()):
    s, kdim = a.shape
    _, kc, n = wg.shape
    assert kc * ND == kdim
    nx = 2 + (res is not None) + (bias is not None)

    def body(*refs):
        acc = _dot(refs[0][...], refs[1][...].reshape(kdim, tn), NN)
        for extra in refs[2:nx]:
            acc = acc + extra[...]
        refs[nx][...] = acc.astype(out_dtype)
        if branches:
            scr = refs[-1]
            _stage(scr, acc)
            for o_ref, dil in zip(refs[nx + 1:], SPLIT_DILATIONS):
                _split_rows(scr, o_ref, dil)

    ins, specs = [a, wg], [pl.BlockSpec((TM, kdim), lambda m, j: (m, 0)), pl.BlockSpec((ND, kc, tn), lambda m, j: (0, 0, j))]
    if res is not None:
        ins.append(res)
        specs.append(pl.BlockSpec((TM, tn), lambda m, j: (m, j)))
    if bias is not None:
        ins.append(bias)
        specs.append(pl.BlockSpec((1, tn), lambda m, j: (0, j)))
    out_specs, out_shape, scratch = [pl.BlockSpec((TM, tn), lambda m, j: (m, j))], [SDS((s, n), out_dtype)], []
    if branches:
        out_specs += _branch_specs(TM, tn, lambda dil, m, j: (0, m, j))
        out_shape += [SDS((dil, s // dil, n), out_dtype) for dil in SPLIT_DILATIONS]
        scratch = [pltpu.VMEM((tn // LANES, TM, LANES), F32)]
    outs = _pc(body, name=name, grid=(s // TM, n // tn), in_specs=specs, out_specs=out_specs, out_shape=out_shape,
               scratch=scratch, sem=("parallel", "arbitrary"), args=ins, jobs=jobs)
    return [outs[0][None]] + outs[1:] if branches else outs[0]


def _swiglu_mm(n2, wgg, wug, name, jobs=()):
    s, d = n2.shape
    fc = wgg.shape[2]

    def body(a_ref, wg_ref, wu_ref, g_ref, u_ref, act_ref):
        a = a_ref[...]
        g = _dot(a, wg_ref[0], NN)
        u = _dot(a, wu_ref[0], NN)
        g_ref[0], u_ref[0] = g.astype(BF16), u.astype(BF16)
        act_ref[0] = (g * _sigmoid(g) * u).astype(BF16)

    wspec = pl.BlockSpec((1, d, fc), lambda m, j: (j, 0, 0))
    out = pl.BlockSpec((1, TM, fc), lambda m, j: (j, m, 0))
    return _pc(
        body, name=name, grid=(s // TM, ND),
        in_specs=[pl.BlockSpec((TM, d), lambda m, j: (m, 0)), wspec, wspec],
        out_specs=[out, out, out], out_shape=[SDS((ND, s, fc), BF16)] * 3,
        sem=("parallel", "arbitrary"), args=(n2, wgg, wug), jobs=jobs)


def _down_mm(act, wdg, res, name, jobs=()):
    _, s, fc = act.shape
    d = wdg.shape[2]

    def body(a_ref, w_ref, r_ref, o_ref):
        @pl.when(pl.program_id(1) == 0)
        def _():
            o_ref[...] = r_ref[...]
        o_ref[...] += _dot(a_ref[0], w_ref[0], NN)

    row = pl.BlockSpec((TM, d), lambda m, j: (m, 0))
    return _pc(
        body, name=name, grid=(s // TM, ND),
        in_specs=[pl.BlockSpec((1, TM, fc), lambda m, j: (j, m, 0)),
                  pl.BlockSpec((1, fc, d), lambda m, j: (j, 0, 0)), row],
        out_specs=[row], out_shape=[SDS((s, d), F32)],
        sem=("parallel", "arbitrary"), args=(act, wdg, res), jobs=jobs)[0]


def _dact_mm(dh, wdg, gate, up, name, jobs=()):
    s, d = dh.shape
    fc = wdg.shape[1]

    def body(a_ref, w_ref, g_ref, u_ref, dg_ref, du_ref):
        dact = _dot(a_ref[...], w_ref[0], NT)
        g, u = g_ref[0].astype(F32), u_ref[0].astype(F32)
        sg = _sigmoid(g)
        du_ref[0] = (dact * (g * sg)).astype(BF16)
        dg_ref[0] = (dact * u * (sg * (1.0 + g * (1.0 - sg)))).astype(BF16)

    blk = pl.BlockSpec((1, TM, fc), lambda m, j: (j, m, 0))
    return _pc(
        body, name=name, grid=(s // TM, ND),
        in_specs=[pl.BlockSpec((TM, d), lambda m, j: (m, 0)),
                  pl.BlockSpec((1, fc, d), lambda m, j: (j, 0, 0)), blk, blk],
        out_specs=[blk, blk], out_shape=[SDS((ND, s, fc), BF16)] * 2,
        sem=("parallel", "arbitrary"), args=(dh, wdg, gate, up), jobs=jobs)


def _dwd_mm(act, dh, name, jobs=()):
    _, s, fc = act.shape
    d = dh.shape[1]
    nk = s // TS

    def body(a_ref, b_ref, o_ref, acc):
        k = pl.program_id(1)

        @pl.when(k == 0)
        def _():
            acc[...] = jnp.zeros_like(acc)
        acc[...] += _dot(a_ref[0], b_ref[...], TN)

        @pl.when(k == nk - 1)
        def _():
            o_ref[0] = acc[...].astype(BF16)

    return _pc(
        body, name=name, grid=(ND, nk),
        in_specs=[pl.BlockSpec((1, TS, fc), lambda j, k: (j, k, 0)), pl.BlockSpec((TS, d), lambda j, k: (k, 0))],
        out_specs=[pl.BlockSpec((1, fc, d), lambda j, k: (_slot(j), 0, 0))],
        out_shape=[SDS((ND, fc, d), BF16)], scratch=[pltpu.VMEM((fc, d), F32)],
        sem=("parallel", "arbitrary"), args=(act, dh), jobs=jobs)[0]


def _dwgu_mm(n2, dgate, dup, name, jobs=()):
    s, d = n2.shape
    fc = dgate.shape[2]
    nk = s // TS

    def body(a_ref, g_ref, u_ref, og_ref, ou_ref, accg, accu):
        k = pl.program_id(1)

        @pl.when(k == 0)
        def _():
            accg[...] = jnp.zeros_like(accg)
            accu[...] = jnp.zeros_like(accu)
        a = a_ref[...]
        accg[...] += _dot(g_ref[0], a, TN)
        accu[...] += _dot(u_ref[0], a, TN)

        @pl.when(k == nk - 1)
        def _():
            og_ref[0] = accg[...].astype(BF16)
            ou_ref[0] = accu[...].astype(BF16)

    blk = pl.BlockSpec((1, TS, fc), lambda j, k: (j, k, 0))
    out = pl.BlockSpec((1, fc, d), lambda j, k: (_slot(j), 0, 0))
    return _pc(
        body, name=name, grid=(ND, nk),
        in_specs=[pl.BlockSpec((TS, d), lambda j, k: (k, 0)), blk, blk], out_specs=[out, out],
        out_shape=[SDS((ND, fc, d), BF16)] * 2,
        scratch=[pltpu.VMEM((fc, d), F32), pltpu.VMEM((fc, d), F32)],
        sem=("parallel", "arbitrary"), args=(n2, dgate, dup), jobs=jobs)


def _dn_ffn_mm(dgate, dup, wgg, wug, name, jobs=()):
    _, s, fc = dgate.shape
    d = wgg.shape[1]

    def body(g_ref, u_ref, wg_ref, wu_ref, o_ref):
        j = pl.program_id(1)

        @pl.when(j == 0)
        def _():
            o_ref[...] = jnp.zeros_like(o_ref)
        o_ref[...] += _dot(g_ref[0], wg_ref[0], NT) + _dot(u_ref[0], wu_ref[0], NT)

    blk = pl.BlockSpec((1, TM, fc), lambda m, j: (j, m, 0))
    wspec = pl.BlockSpec((1, d, fc), lambda m, j: (j, 0, 0))
    return _pc(
        body, name=name, grid=(s // TM, ND), in_specs=[blk, blk, wspec, wspec],
        out_specs=[pl.BlockSpec((TM, d), lambda m, j: (m, 0))], out_shape=[SDS((s, d), F32)],
        sem=("parallel", "arbitrary"), args=(dgate, dup, wgg, wug), jobs=jobs)[0]


def _mm_rows_t(pairs, name, out_dtype, branches=False, jobs=()):
    s, n = pairs[0][0].shape
    _, kc, _ = pairs[0][1].shape
    np_ = len(pairs)
    grp = ND // 2
    wide = grp * kc

    def body(*refs):
        o_ref = refs[2 * np_]
        for i in range(grp):
            acc = None
            for p in range(np_):
                t = _dot(refs[p][...], refs[np_ + p][i], NT)
                acc = t if acc is None else acc + t
            o_ref[:, kc * i:kc * (i + 1)] = acc.astype(out_dtype)
            if branches:
                for c, ls in enumerate(_lane_chunks(kc)):
                    refs[-1][i * (kc // LANES) + c] = acc[:, ls]
        if branches:
            for b_ref, dil in zip(refs[2 * np_ + 1:], SPLIT_DILATIONS):
                _split_rows(refs[-1], b_ref, dil)

    out_specs, out_shape, scratch = [pl.BlockSpec((TM, wide), lambda m, j: (m, j))], [SDS((s, kc * ND), out_dtype)], []
    if branches:
        out_specs += _branch_specs(TM, wide, lambda dil, m, j: (0, m, j))
        out_shape += [SDS((dil, s // dil, kc * ND), out_dtype) for dil in SPLIT_DILATIONS]
        scratch = [pltpu.VMEM((wide // LANES, TM, LANES), F32)]
    outs = _pc(
        body, name=name, grid=(s // TM, ND // grp),
        in_specs=[pl.BlockSpec((TM, n), lambda m, j: (m, 0))] * np_ + [pl.BlockSpec((grp, kc, n), lambda m, j: (j, 0, 0))] * np_,
        out_specs=out_specs, out_shape=out_shape, scratch=scratch,
        sem=("parallel", "arbitrary"), args=[p[0] for p in pairs] + [p[1] for p in pairs], jobs=jobs)
    return [outs[0][None]] + outs[1:] if branches else outs[0]


def _dw_rows_mm(a, b, name):
    s, kdim = a.shape
    n = b.shape[1]
    kc = kdim // ND
    ts = TS // 2
    nk = s // ts

    def body(a_ref, b_ref, o_ref, acc):
        k = pl.program_id(0)

        @pl.when(k == 0)
        def _():
            acc[...] = jnp.zeros_like(acc)
        acc[...] += _dot(a_ref[...], b_ref[...], TN)

        @pl.when(k == nk - 1)
        def _():
            for dev in range(ND):
                o_ref[_slot(dev)] = acc[kc * dev:kc * (dev + 1), :].astype(BF16)

    return pl.pallas_call(
        body, name=name, grid=(nk,),
        in_specs=[pl.BlockSpec((ts, kdim), lambda k: (k, 0)), pl.BlockSpec((ts, n), lambda k: (k, 0))],
        out_specs=pl.BlockSpec((ND, kc, n), lambda k: (0, 0, 0)), out_shape=SDS((ND, kc, n), BF16),
        scratch_shapes=[pltpu.VMEM((kdim, n), F32)], compiler_params=_cp("arbitrary"))(a, b)


def _dw1_mm(n1, du, jobs=()):
    s, d = n1.shape
    cw = du.shape[1] // ND
    nk = s // TS

    def body(a_ref, b_ref, o_ref, acc):
        k = pl.program_id(1)

        @pl.when(k == 0)
        def _():
            acc[...] = jnp.zeros_like(acc)
        acc[...] += _dot(a_ref[...], b_ref[...], TN)

        @pl.when(k == nk - 1)
        def _():
            o_ref[0] = acc[...].astype(BF16)

    return _pc(
        body, name="dw1_mm", grid=(ND, nk),
        in_specs=[pl.BlockSpec((TS, d), lambda j, k: (k, 0)), pl.BlockSpec((TS, cw), lambda j, k: (k, j))],
        out_specs=[pl.BlockSpec((1, d, cw), lambda j, k: (_slot(j), 0, 0))], out_shape=[SDS((ND, d, cw), BF16)],
        scratch=[pltpu.VMEM((d, cw), F32)], sem=("parallel", "arbitrary"), args=(n1, du), jobs=jobs)[0]


def _dn1_mm(du, w1g, part, nparts, prev=None, jobs=()):
    s = du.shape[0]
    _, d, cw = w1g.shape
    steps = s // TM // nparts
    m0 = part * steps

    def body(a_ref, w_ref, *refs):
        o_ref = refs[-1]
        j = pl.program_id(1)

        @pl.when(j == 0)
        def _():
            o_ref[...] = jnp.zeros_like(o_ref)
        o_ref[...] += _dot(a_ref[...], w_ref[0], NT)

    ins = [du, w1g] if prev is None else [du, w1g, prev]
    specs = [pl.BlockSpec((TM, cw), lambda m, j: (m0 + m, j)), pl.BlockSpec((1, d, cw), lambda m, j: (j, 0, 0))]
    return _pc(
        body, name=f"dn1_mm{part}", grid=(steps, ND), in_specs=specs if prev is None else specs + [ANY],
        out_specs=[pl.BlockSpec((TM, d), lambda m, j: (m0 + m, 0))], out_shape=[SDS((s, d), F32)],
        alias=None if prev is None else {2: 0}, sem=("parallel", "arbitrary"), args=ins, jobs=jobs)[0]


FAR = 1e33


def _slopes(heads):
    return [2.0 ** (-8.0 * (h + 1) / heads) for h in range(heads)]


def _band(has_prev):
    qi = lax.broadcasted_iota(jnp.int32, (BLK, 2 * BLK), 0)
    ki = lax.broadcasted_iota(jnp.int32, (BLK, 2 * BLK), 1)
    j = qi - ki + BLK
    ok = (j >= 0) & (j <= BLK) & (has_prev | (ki >= BLK))
    return jnp.where(ok, j.astype(F32), FAR)


SPLIT_DILATIONS = tuple(dil for dil in BRANCH_DILATIONS if dil > 1)


def _lane_chunks(w):
    return [slice(LANES * c, LANES * (c + 1)) for c in range(w // LANES)]


def _stage(scr, tile):
    for c, ls in enumerate(_lane_chunks(tile.shape[1])):
        scr[c] = tile[:, ls]


def _split_rows(scr, o_ref, dil):
    _, n, w = o_ref.shape
    for r in range(dil):
        for c, ls in enumerate(_lane_chunks(w)):
            o_ref[r, :, ls] = scr[c, pl.ds(r, n, stride=dil), :].astype(o_ref.dtype)


def _join_rows(i_ref, scr, dil):
    _, n, w = i_ref.shape
    for r in range(dil):
        for c, ls in enumerate(_lane_chunks(w)):
            scr[c, pl.ds(r, n, stride=dil), :] = i_ref[r, :, ls].astype(F32)


def _unstage(scr, w):
    return jnp.concatenate([scr[c] for c in range(w // LANES)], axis=1)


def _branch_specs(rows, w, index):
    return [pl.BlockSpec((dil, rows // dil, w), functools.partial(index, dil)) for dil in SPLIT_DILATIONS]


def _attn_fwd(q, k, v, dil, jobs=()):
    _, l, d = q.shape
    heads = d // HEAD
    assert heads <= HEAD
    scale = HEAD ** -0.5
    slopes = _slopes(heads)

    def body(q_ref, kc_ref, kp_ref, vc_ref, vp_ref, o_ref, lse_ref):
        dist = _band(pl.program_id(1) > 0)
        lane = lax.broadcasted_iota(jnp.int32, (BLK, HEAD), 1)
        lse = jnp.zeros((BLK, HEAD), F32)
        for h in range(heads):
            sl = slice(HEAD * h, HEAD * (h + 1))
            kh = jnp.concatenate([kp_ref[0, :, sl], kc_ref[0, :, sl]], axis=0)
            vh = jnp.concatenate([vp_ref[0, :, sl], vc_ref[0, :, sl]], axis=0)
            logits = _dot(q_ref[0, :, sl], kh, NT) * scale + dist * (-slopes[h] * dil)
            m = jnp.max(logits, axis=-1, keepdims=True)
            p = jnp.exp(logits - m)
            den = jnp.sum(p, axis=-1, keepdims=True)
            o_ref[0, :, sl] = _dot(p.astype(BF16), vh, NN) / den
            lse = jnp.where(lane == h, m + jnp.log(den), lse)
        lse_ref[0] = lse

    cur = pl.BlockSpec((1, BLK, d), lambda r, b: (r, b, 0))
    prev = pl.BlockSpec((1, BLK, d), lambda r, b: (r, jnp.maximum(b - 1, 0), 0))
    return _pc(
        body, name=f"attn_fwd_d{dil}", grid=(dil, l // BLK),
        in_specs=[cur, cur, prev, cur, prev], out_specs=[cur, pl.BlockSpec((1, BLK, HEAD), lambda r, b: (r, b, 0))],
        out_shape=[SDS((dil, l, d), F32), SDS((dil, l, HEAD), F32)], sem=("parallel", "arbitrary"),
        args=(q, k, k, v, v), jobs=jobs)


def _attn_merge(outs, lses):
    _, s, d = outs[0].shape
    heads = d // HEAD
    nb = len(outs)
    nsplit = nb - 1

    def body(*refs):
        o_refs, l_refs = refs[:nb], refs[nb:2 * nb]
        att_refs, lse_refs = refs[2 * nb:3 * nb], refs[3 * nb:4 * nb]
        scr_o, scr_l, scr_att = refs[4 * nb:4 * nb + nsplit], refs[4 * nb + nsplit:4 * nb + 2 * nsplit], refs[-1]
        ls = [l_refs[0][...]]
        for k, dil in enumerate(SPLIT_DILATIONS):
            _join_rows(o_refs[1 + k], scr_o[k], dil)
            _join_rows(l_refs[1 + k], scr_l[k], dil)
            ls.append(scr_l[k][0])
        m = functools.reduce(jnp.maximum, ls)
        ws = [jnp.exp(v - m) for v in ls]
        den = functools.reduce(jnp.add, ws)
        ws = [w / den for w in ws]
        lse_refs[0][...] = m + jnp.log(den)
        scr_l[0][0] = m + jnp.log(den)
        for h in range(heads):
            sl = slice(HEAD * h, HEAD * (h + 1))
            slab = ws[0][:, h:h + 1] * o_refs[0][:, sl]
            for k in range(nsplit):
                slab = slab + ws[1 + k][:, h:h + 1] * scr_o[k][h]
            att_refs[0][:, sl] = slab.astype(BF16)
            scr_att[h] = slab
        for k, dil in enumerate(SPLIT_DILATIONS):
            _split_rows(scr_att, att_refs[1 + k], dil)
            _split_rows(scr_l[0], lse_refs[1 + k], dil)

    def specs(w):
        return [pl.BlockSpec((ROWS, w), lambda i: (i, 0))] + _branch_specs(ROWS, w, lambda dil, i: (0, i, 0))

    def shapes(w, dt):
        return [SDS((s, w), dt)] + [SDS((dil, s // dil, w), dt) for dil in SPLIT_DILATIONS]

    wide, narrow = pltpu.VMEM((heads, ROWS, LANES), F32), pltpu.VMEM((1, ROWS, LANES), F32)
    res = pl.pallas_call(
        body, name="attn_merge", grid=(s // ROWS,), in_specs=specs(d) + specs(HEAD), out_specs=specs(d) + specs(HEAD),
        out_shape=shapes(d, BF16) + shapes(HEAD, F32),
        scratch_shapes=[wide] * nsplit + [narrow] * nsplit + [wide],
        compiler_params=_cp("parallel"))(outs[0].reshape(s, d), *outs[1:], lses[0].reshape(s, HEAD), *lses[1:])
    return list(res[:nb]), list(res[nb:])


def _attn_bwd(q, k, v, do, o, lse, dil, jobs=()):
    _, l, d = q.shape
    nb = l // BLK
    heads = d // HEAD
    scale = HEAD ** -0.5
    slopes = _slopes(heads)
    whole = 2 * l * d <= RESIDENT_BYTES
    steps = nb if whole else nb + 1

    def body(q_ref, kc_ref, kp_ref, vc_ref, vp_ref, do_ref, o_ref, lse_ref, dq_ref, dk_ref, dv_ref, ck, cv):
        b = pl.program_id(1)
        rows = pl.ds(pl.multiple_of(jnp.maximum(b - 1, 0) * BLK, BLK), BLK) if whole else slice(None)

        @pl.when(b == 0)
        def _():
            ck[...] = jnp.zeros_like(ck)
            cv[...] = jnp.zeros_like(cv)

        @pl.when(b < nb)
        def _():
            dist = _band(b > 0)
            lanes = [slice(HEAD * h, HEAD * (h + 1)) for h in range(heads)]
            keys = [jnp.concatenate([kp_ref[0, :, sl], kc_ref[0, :, sl]], axis=0) for sl in lanes]
            vals = [jnp.concatenate([vp_ref[0, :, sl], vc_ref[0, :, sl]], axis=0) for sl in lanes]
            scores = [_dot(q_ref[0, :, sl], kh, NT) for sl, kh in zip(lanes, keys)]
            dps = [_dot(do_ref[0, :, sl], vh, NT) for sl, vh in zip(lanes, vals)]
            ps, dss = [], []
            for h, sl in enumerate(lanes):
                doh = do_ref[0, :, sl]
                delta = jnp.sum(doh.astype(F32) * o_ref[0, :, sl].astype(F32), axis=-1, keepdims=True)
                p = jnp.exp(scores[h] * scale + dist * (-slopes[h] * dil) - lse_ref[0, :, h:h + 1])
                ps.append(p.astype(BF16))
                dss.append((p * (dps[h] - delta)).astype(BF16))
            for h, sl in enumerate(lanes):
                qh, doh = q_ref[0, :, sl], do_ref[0, :, sl]
                dq_ref[0, :, sl] = (_dot(dss[h], keys[h], NN) * scale).astype(BF16)
                dk2 = _dot(dss[h], qh, TN) * scale
                dv2 = _dot(ps[h], doh, TN)
                dk_ref[0, rows, sl] = (ck[:, sl] + dk2[:BLK]).astype(BF16)
                dv_ref[0, rows, sl] = (cv[:, sl] + dv2[:BLK]).astype(BF16)
                ck[:, sl] = dk2[BLK:]
                cv[:, sl] = dv2[BLK:]

        @pl.when(b == steps - 1)
        def _():
            last = pl.ds((nb - 1) * BLK, BLK) if whole else slice(None)
            dk_ref[0, last, :] = ck[...].astype(BF16)
            dv_ref[0, last, :] = cv[...].astype(BF16)

    cur = pl.BlockSpec((1, BLK, d), lambda r, b: (r, jnp.minimum(b, nb - 1), 0))
    prev = pl.BlockSpec((1, BLK, d), lambda r, b: (r, jnp.clip(b - 1, 0, nb - 1), 0))
    lse_spec = pl.BlockSpec((1, BLK, HEAD), lambda r, b: (r, jnp.minimum(b, nb - 1), 0))
    dkv = pl.BlockSpec((1, l, d), lambda r, b: (r, 0, 0)) if whole else prev
    return _pc(
        body, name=f"attn_bwd_d{dil}", grid=(dil, steps),
        in_specs=[cur, cur, prev, cur, prev, cur, cur, lse_spec], out_specs=[cur, dkv, dkv],
        out_shape=[SDS((dil, l, d), BF16)] * 3,
        scratch=[pltpu.VMEM((BLK, d), F32), pltpu.VMEM((BLK, d), F32)],
        sem=("parallel", "arbitrary"), args=(q, k, k, v, v, do, o, lse), jobs=jobs)


def _sum_cast(xs, name):
    _, s, d = xs[0].shape
    nsplit = len(xs) - 1

    def body(*refs):
        i_refs, o_ref, scr = refs[:nsplit + 1], refs[nsplit + 1], refs[nsplit + 2:]
        acc = i_refs[0][...].astype(F32)
        for k, dil in enumerate(SPLIT_DILATIONS):
            _join_rows(i_refs[1 + k], scr[k], dil)
            acc = acc + _unstage(scr[k], d)
        o_ref[...] = acc.astype(BF16)

    row = pl.BlockSpec((ROWS, d), lambda i: (i, 0))
    return pl.pallas_call(
        body, name=name, grid=(s // ROWS,), in_specs=[row] + _branch_specs(ROWS, d, lambda dil, i: (0, i, 0)),
        out_specs=row, out_shape=SDS((s, d), BF16),
        scratch_shapes=[pltpu.VMEM((d // LANES, ROWS, LANES), F32)] * nsplit,
        compiler_params=_cp("parallel"))(xs[0].reshape(s, d), *xs[1:])


def _pack_rows(vs, width):
    flat = jnp.concatenate([v.reshape(-1) for v in vs])
    spans, at = [], 0
    for v in vs:
        spans.append((at, v.size))
        at += v.size
    rows = -(-at // width)
    rows = -(-rows // 8) * 8
    flat = jnp.pad(flat, (0, rows * width - at))
    return flat.reshape(rows, width), spans


def kernel(x, a_norm_g, conv_w1, conv_b1, conv_dw, conv_dw_b, conv_ln_g, conv_ln_b, conv_w2, conv_b2, kv_norm_g, w_k, w_v, b_norm_g, w_q, w_o, ffn_norm_g, ffn_w_gate, ffn_w_up, ffn_w_down, final_norm_g, loss_target, m_a_norm_g, m_conv_w1, m_conv_b1, m_conv_dw, m_conv_dw_b, m_conv_ln_g, m_conv_ln_b, m_conv_w2, m_conv_b2, m_kv_norm_g, m_w_k, m_w_v, m_b_norm_g, m_w_q, m_w_o, m_ffn_norm_g, m_ffn_w_gate, m_ffn_w_up, m_ffn_w_down, m_final_norm_g, v_a_norm_g, v_conv_w1, v_conv_b1, v_conv_dw, v_conv_dw_b, v_conv_ln_g, v_conv_ln_b, v_conv_w2, v_conv_b2, v_kv_norm_g, v_w_k, v_w_v, v_b_norm_g, v_w_q, v_w_o, v_ffn_norm_g, v_ffn_w_gate, v_ffn_w_up, v_ffn_w_down, v_final_norm_g):
    s, d = x.shape[1], x.shape[2]
    dc = d // ND
    h0 = x[0]
    target = loss_target[0]
    xi, yi, ci = lax.axis_index("x"), lax.axis_index("y"), lax.axis_index("c")
    me = 4 * xi + 2 * yi + ci
    c_idx = jnp.reshape(ci, (1,)).astype(jnp.int32)
    q_idx = jnp.reshape(2 * xi + yi, (1,)).astype(jnp.int32)

    bf = lambda w: w.astype(BF16)
    small_shards = [a_norm_g, conv_b1, conv_dw, conv_dw_b, conv_ln_g, conv_ln_b, conv_b2]
    sp, sp_spans = _pack_rows(small_shards, dc)
    w1g, spg = _all_gather([bf(conv_w1[0]), sp], "gather_first")
    spg = spg.reshape(ND, -1)

    def small_full(i, rows):
        at, size = sp_spans[i]
        return spg[:, at:at + size].reshape(ND, rows, size // rows).transpose(1, 0, 2).reshape(rows, -1)

    a_g = small_full(0, 1)
    b1 = small_full(1, 1)
    dw = jnp.pad(small_full(2, CONV_W), ((0, CONV_PAD - CONV_W), (0, 0)))
    dwb, lng, lnb, b2 = small_full(3, 1), small_full(4, 1), small_full(5, 1), small_full(6, 1)
    kv_g, q_g, fin_g = kv_norm_g.reshape(1, d), b_norm_g.reshape(1, d), final_norm_g.reshape(1, d)
    f_g = [ffn_norm_g[0:1], ffn_norm_g[1:2]]

    def send(*shards):
        return _job_gather_send([bf(t) for t in shards])

    def forward(job):
        return _job_gather_forward(job.result)

    def send_half(w, part, first=None):
        return _job_gather_send_rows(bf(w), part, 2, None if first is None else first.result[0])

    s_w2 = send(conv_w2[0])
    (n1,) = _rms_fwd(h0, [a_g], "rms_a", jobs=[s_w2])
    f_w2, s_g0 = forward(s_w2), send(ffn_w_gate[0])
    ua, ug, glu = _glu_mm(n1, w1g, b1, jobs=[f_w2, s_g0])
    (w2g,) = f_w2.result
    f_g0, s_u0 = forward(s_g0), send(ffn_w_up[0])
    cv, sw = _conv_fwd(glu, dw, dwb, lng, lnb, jobs=[f_g0, s_u0])
    (wg0,) = f_g0.result
    f_u0 = forward(s_u0)
    h1 = _mm_rows(sw, w2g, "w2_mm", res=h0, bias=b2, jobs=[f_u0])
    (wu0,) = f_u0.result
    (n2a,) = _rms_fwd(h1, [f_g[0]], "rms_f0")
    s_d0, s_kv = send(ffn_w_down[0]), send(w_k, w_v)
    gate0, up0, act0 = _swiglu_mm(n2a, wg0, wu0, "swiglu_mm0", jobs=[s_d0, s_kv])
    f_d0 = forward(s_d0)
    _comm_call([f_d0], "forward_mid")
    (wd0,) = f_d0.result
    f_kv, s_qo = forward(s_kv), send(w_q[0], w_o[0])
    h2 = _down_mm(act0, wd0, h1, "down_mm0", jobs=[f_kv, s_qo])
    wkg, wvg = f_kv.result
    kvn, qn = _rms_fwd(h2, [kv_g, q_g], "rms_kvq")
    f_qo, s_g1a = forward(s_qo), send_half(ffn_w_gate[1], 0)
    kk = _mm_rows(kvn, wkg, "k_mm", out_dtype=BF16, branches=True, jobs=[f_qo, s_g1a])
    wqg, wog = f_qo.result
    s_g1b = send_half(ffn_w_gate[1], 1, s_g1a)
    vv = _mm_rows(kvn, wvg, "v_mm", out_dtype=BF16, branches=True, jobs=[s_g1b])
    f_g1, s_u1a = forward(s_g1b), send_half(ffn_w_up[1], 0)
    qq = _mm_rows(qn, wqg, "q_mm", out_dtype=BF16, branches=True, jobs=[f_g1, s_u1a])
    (wg1,) = f_g1.result
    branch = {dil: (qq[i], kk[i], vv[i]) for i, dil in enumerate(BRANCH_DILATIONS)}
    s_u1b = send_half(ffn_w_up[1], 1, s_u1a)
    o1, l1 = _attn_fwd(*branch[1], 1, jobs=[s_u1b])
    f_u1, s_d1a = forward(s_u1b), send_half(ffn_w_down[1], 0)
    o4, l4 = _attn_fwd(*branch[4], 4, jobs=[f_u1, s_d1a])
    (wu1,) = f_u1.result
    s_d1b = send_half(ffn_w_down[1], 1, s_d1a)
    o16, l16 = _attn_fwd(*branch[16], 16, jobs=[s_d1b])
    atts, lses = _attn_merge([o1, o4, o16], [l1, l4, l16])
    att = atts[0]
    atts, lses = [att[None]] + atts[1:], [lses[0][None]] + lses[1:]
    f_d1 = forward(s_d1b)
    h3 = _mm_rows(att, wog, "wo_mm", res=h2, jobs=[f_d1])
    (wd1,) = f_d1.result
    (n2b,) = _rms_fwd(h3, [f_g[1]], "rms_f1")
    gate1, up1, act1 = _swiglu_mm(n2b, wg1, wu1, "swiglu_mm1")
    h4 = _down_mm(act1, wd1, h3, "down_mm1")

    flat = lambda g: g.reshape(ND, -1, g.shape[-1])
    chip_sums, cross = {}, {}

    def to_sibling(**grads):
        job = _job_scatter_sibling([flat(g) for g in grads.values()])
        job.names = list(grads)
        return job

    def add_up(job):
        for n, g, r in zip(job.names, job.ins, job.result):
            chip_sums[n] = _rs_add(g, r, c_idx, f"rs_add_{n}")

    def to_chips(*names):
        job = _job_scatter_cross([chip_sums[n] for n in names])
        job.names = names
        return job

    def landed(job):
        cross.update(zip(job.names, job.result))

    dh4, dh4b, d_fin, loss_row = _final_loss(h4, target, fin_g)
    dgate1, dup1 = _dact_mm(dh4b, wd1, gate1, up1, "dact_mm1")
    g_wd1 = _dwd_mm(act1, dh4b, "dwd_mm1")
    j1 = to_sibling(wd1=g_wd1)
    g_wg1, g_wu1 = _dwgu_mm(n2b, dgate1, dup1, "dwgu_mm1", jobs=[j1])
    add_up(j1)
    j2, j3 = to_chips("wd1"), to_sibling(wg1=g_wg1, wu1=g_wu1)
    dn2b = _dn_ffn_mm(dgate1, dup1, wg1, wu1, "dn_ffn_mm1", jobs=[j2, j3])
    landed(j2)
    add_up(j3)
    dh3, dh3b, d_f1 = _rms_bwd(h3, [(f_g[1], dn2b)], dh4, "rms_f1_bwd")
    g_wo = _dw_rows_mm(att, dh3b, "dwo_mm")
    j4 = to_sibling(wo=g_wo)
    datt = _mm_rows_t([(dh3b, wog)], "datt_mm", BF16, branches=True, jobs=[j4])
    add_up(j4)
    riders = {1: [to_chips("wo")], 4: [], 16: []}
    dqs, dks, dvs = [], [], []
    for i, dil in enumerate(BRANCH_DILATIONS):
        qb, kb, vb = branch[dil]
        dq_b, dk_b, dv_b = _attn_bwd(qb, kb, vb, datt[i], atts[i], lses[i], dil, jobs=riders[dil])
        for job in riders[dil]:
            landed(job)
        dqs.append(dq_b)
        dks.append(dk_b)
        dvs.append(dv_b)
    dq, dk, dv = _sum_cast(dqs, "dq_sum"), _sum_cast(dks, "dk_sum"), _sum_cast(dvs, "dv_sum")
    g_wq = _dw_rows_mm(qn, dq, "dwq_mm")
    g_wk = _dw_rows_mm(kvn, dk, "dwk_mm")
    g_wv = _dw_rows_mm(kvn, dv, "dwv_mm")
    j5 = to_sibling(wq=g_wq, wk=g_wk, wv=g_wv)
    dqn = _mm_rows_t([(dq, wqg)], "dqn_mm", F32, jobs=[j5])
    add_up(j5)
    j6 = to_chips("wq", "wk")
    dkvn = _mm_rows_t([(dk, wkg), (dv, wvg)], "dkvn_mm", F32, jobs=[j6])
    landed(j6)
    dh2, dh2b, d_q, d_kv = _rms_bwd(h2, [(q_g, dqn), (kv_g, dkvn)], dh3, "rms_kvq_bwd")
    j7 = to_chips("wg1")
    dgate0, dup0 = _dact_mm(dh2b, wd0, gate0, up0, "dact_mm0", jobs=[j7])
    landed(j7)
    j7b = to_chips("wv")
    g_wd0 = _dwd_mm(act0, dh2b, "dwd_mm0", jobs=[j7b])
    landed(j7b)
    j8, j8b = to_sibling(wd0=g_wd0), to_chips("wu1")
    g_wg0, g_wu0 = _dwgu_mm(n2a, dgate0, dup0, "dwgu_mm0", jobs=[j8, j8b])
    add_up(j8)
    landed(j8b)
    j9, j10 = to_chips("wd0"), to_sibling(wg0=g_wg0, wu0=g_wu0)
    dn2a = _dn_ffn_mm(dgate0, dup0, wg0, wu0, "dn_ffn_mm0", jobs=[j9, j10])
    landed(j9)
    add_up(j10)
    dh1, dh1b, d_f0, d_b2 = _rms_bwd(h1, [(f_g[0], dn2a)], dh2, "rms_f0_bwd", colsum=True)
    g_w2 = _dw_rows_mm(sw, dh1b, "dw2_mm")
    j11 = to_sibling(w2=g_w2)
    dsw = _mm_rows_t([(dh1b, w2g)], "dsw_mm", F32, jobs=[j11])
    add_up(j11)
    dcv, d_lng, d_lnb = _ln_bwd(dsw, cv, lng, lnb)
    j12 = to_chips("wg0", "wu0")
    du, d_dw, d_dwb, d_b1 = _conv_bwd(dcv, glu, ua, ug, dw, jobs=[j12])
    landed(j12)
    j13 = to_chips("w2")
    g_w1 = _dw1_mm(n1, du, jobs=[j13])
    landed(j13)
    j14 = to_sibling(w1=g_w1)
    _comm_call([j14], "rs_w1_sibling")
    add_up(j14)
    j15 = to_chips("w1")
    dn1 = _dn1_mm(du, w1g, 0, 1, jobs=[j15])
    landed(j15)
    dx, _, d_a = _rms_bwd(h0, [(a_g, dn1)], dh1, "rms_a_bwd")

    small_g = [d_a, d_b1, d_dw[:CONV_W], d_dwb, d_lng, d_lnb, d_b2, d_kv, d_q, d_f0, d_f1, d_fin, loss_row]
    gp, gp_spans = _pack_rows(small_g, d)
    (gpg,) = _all_gather([gp], "gather_small_grads")

    two = lambda t: t.reshape(-1, t.shape[-1])

    def adam(w, m, v, names, tag, swapped=False):
        view = (lambda t: jnp.swapaxes(t, 1, 2)) if swapped else (lambda t: t)
        w, m, v = view(w), view(m), view(v)
        res = None
        for part, n in enumerate(names):
            res = _adamw_big(two(w), two(m), two(v), chip_sums[n], cross[n], q_idx, f"adamw_{tag}{part}", part, res)
        return [view(t.reshape(w.shape)) for t in res]

    big_out = [
        adam(conv_w1, m_conv_w1, v_conv_w1, ["w1"], "w1"), adam(conv_w2, m_conv_w2, v_conv_w2, ["w2"], "w2"),
        adam(w_k, m_w_k, v_w_k, ["wk"], "wk"), adam(w_v, m_w_v, v_w_v, ["wv"], "wv"),
        adam(w_q, m_w_q, v_w_q, ["wq"], "wq"), adam(w_o, m_w_o, v_w_o, ["wo"], "wo"),
        adam(ffn_w_gate, m_ffn_w_gate, v_ffn_w_gate, ["wg0", "wg1"], "wg", swapped=True),
        adam(ffn_w_up, m_ffn_w_up, v_ffn_w_up, ["wu0", "wu1"], "wu", swapped=True),
        adam(ffn_w_down, m_ffn_w_down, v_ffn_w_down, ["wd0", "wd1"], "wd")]

    gsum = _sum_devices(gpg, "sum_small_grads").reshape(-1)

    def gfull(i):
        at, size = gp_spans[i]
        return gsum[at:at + size]

    def shard_of(vec, rows):
        return lax.dynamic_slice_in_dim(vec.reshape(rows, -1), me * (vec.size // rows // ND), vec.size // rows // ND, axis=1)

    loss = gfull(12)[0]
    small_grads = [
        shard_of(gfull(0), 1), shard_of(gfull(1), 1), shard_of(gfull(2), CONV_W)[None], shard_of(gfull(3), 1),
        shard_of(gfull(4), 1), shard_of(gfull(5), 1), shard_of(gfull(6), 1),
        gfull(7), gfull(8)[None], jnp.stack([gfull(9), gfull(10)]), gfull(11)]
    small_w = [a_norm_g, conv_b1, conv_dw, conv_dw_b, conv_ln_g, conv_ln_b, conv_b2, kv_norm_g, b_norm_g, ffn_norm_g, final_norm_g]
    small_m = [m_a_norm_g, m_conv_b1, m_conv_dw, m_conv_dw_b, m_conv_ln_g, m_conv_ln_b, m_conv_b2, m_kv_norm_g, m_b_norm_g, m_ffn_norm_g, m_final_norm_g]
    small_v = [v_a_norm_g, v_conv_b1, v_conv_dw, v_conv_dw_b, v_conv_ln_g, v_conv_ln_b, v_conv_b2, v_kv_norm_g, v_b_norm_g, v_ffn_norm_g, v_final_norm_g]
    small_grads = [g.reshape(w.shape) for g, w in zip(small_grads, small_w)]
    wp, spans = _pack_rows(small_w, LANES)
    gpk, _ = _pack_rows(small_grads, LANES)
    mp, _ = _pack_rows(small_m, LANES)
    vp, _ = _pack_rows(small_v, LANES)
    dp, mnp, vnp = _adamw_small(wp, gpk, mp, vp, "adamw_small")

    def unpack(packed):
        flat = packed.reshape(-1)
        return [flat[at:at + size].reshape(w.shape) for (at, size), w in zip(spans, small_w)]

    small_out = list(zip(small_grads, unpack(dp), unpack(mnp), unpack(vnp)))

    order = ["a_norm_g", "conv_w1", "conv_b1", "conv_dw", "conv_dw_b", "conv_ln_g", "conv_ln_b", "conv_w2", "conv_b2",
             "kv_norm_g", "w_k", "w_v", "b_norm_g", "w_q", "w_o", "ffn_norm_g", "ffn_w_gate", "ffn_w_up", "ffn_w_down",
             "final_norm_g"]
    big_names = ["conv_w1", "conv_w2", "w_k", "w_v", "w_q", "w_o", "ffn_w_gate", "ffn_w_up", "ffn_w_down"]
    small_names = ["a_norm_g", "conv_b1", "conv_dw", "conv_dw_b", "conv_ln_g", "conv_ln_b", "conv_b2", "kv_norm_g",
                   "b_norm_g", "ffn_norm_g", "final_norm_g"]
    table = {n: big_out[i] for i, n in enumerate(big_names)}
    table.update({n: small_out[i] for i, n in enumerate(small_names)})
    result = [loss, dx[None]]
    for kind in range(4):
        result += [table[n][kind] for n in order]
    return tuple(result)
```
